```python
import numpy as np
import jax
import jax.numpy as jnp
from jax import lax

D_MODEL = 1024
BATCH = 8
SEQ = 2048
DEPTH = 4

N_MIXERS = 3
HEAD_DIM = 64
N_HEADS = D_MODEL // HEAD_DIM
D_FF = 2816
RMS_EPS = 1e-6
ROPE_THETA = 10000.0
BLOCK_Q = 128
NEG = -1e30

NSA_GROUPS = 4
NSA_REP = N_HEADS // NSA_GROUPS
NSA_CMP_LEN = 32
NSA_CMP_STRIDE = 16
NSA_CMP_HIDDEN = 2 * HEAD_DIM
NSA_SLC_LEN = 64
NSA_TOPK = 16
NSA_WINDOW = 512
NSA_SLC_QBLOCK = 32
NSA_FORCE_BONUS = 1e4

SWA_KV_HEADS = 2
SWA_REP = N_HEADS // SWA_KV_HEADS
SWA_WINDOW = 128

FOX_HEADS = N_HEADS

N_A = len(range(0, DEPTH, N_MIXERS))
N_B = len(range(1, DEPTH, N_MIXERS))
N_C = len(range(2, DEPTH, N_MIXERS))

Q_WIDTH = N_HEADS * HEAD_DIM
NSA_KV_WIDTH = 3 * 2 * NSA_GROUPS * HEAD_DIM
NSA_IN = Q_WIDTH + NSA_KV_WIDTH + 3 * N_HEADS
SWA_IN = Q_WIDTH + 2 * SWA_KV_HEADS * HEAD_DIM
FOX_IN = 3 * FOX_HEADS * HEAD_DIM + FOX_HEADS

kernel_name = 'hybrid_nsa_swa_fox_macaron'


def rmsnorm(x, g):
    xf = x.astype(jnp.float32)
    y = xf * lax.rsqrt(jnp.mean(xf * xf, axis=-1, keepdims=True) + RMS_EPS)
    return (y * g.astype(jnp.float32)).astype(x.dtype)


def swiglu(x, w_gu, w_down):
    g, u = jnp.split(x @ w_gu, 2, axis=-1)
    return (jax.nn.silu(g) * u) @ w_down


def rope_tables(S):
    inv = ROPE_THETA ** (-jnp.arange(0, HEAD_DIM, 2, dtype=jnp.float32) / HEAD_DIM)
    ang = jnp.arange(S, dtype=jnp.float32)[:, None] * inv[None, :]
    return jnp.cos(ang), jnp.sin(ang)


def apply_rope(x, cos, sin):
    x1, x2 = jnp.split(x.astype(jnp.float32), 2, axis=-1)
    c = cos[None, :, None, :]
    s = sin[None, :, None, :]
    return jnp.concatenate([x1 * c - x2 * s, x2 * c + x1 * s], axis=-1).astype(x.dtype)


def banded_attention(q, k, v, window, sinks=None):
    B, S, G, R, dh = q.shape
    span = window + BLOCK_Q
    kp = jnp.pad(k, ((0, 0), (window, 0), (0, 0), (0, 0)))
    vp = jnp.pad(v, ((0, 0), (window, 0), (0, 0), (0, 0)))
    scale = dh ** -0.5

    def step(i):
        start = i * BLOCK_Q
        qi = lax.dynamic_slice_in_dim(q, start, BLOCK_Q, axis=1)
        ki = lax.dynamic_slice_in_dim(kp, start, span, axis=1)
        vi = lax.dynamic_slice_in_dim(vp, start, span, axis=1)
        s = jnp.einsum('bqgrd,bkgd->bgrqk', qi, ki).astype(jnp.float32) * scale
        qpos = start + jnp.arange(BLOCK_Q)
        kpos = start - window + jnp.arange(span)
        diff = qpos[:, None] - kpos[None, :]
        mask = (diff >= 0) & (diff < window) & (kpos[None, :] >= 0)
        s = jnp.where(mask, s, NEG)
        if sinks is None:
            p = jax.nn.softmax(s, axis=-1)
        else:
            sk = sinks.astype(jnp.float32)[None, :, :, None, None]
            m = jnp.maximum(jnp.max(s, axis=-1, keepdims=True), sk)
            e = jnp.exp(s - m)
            p = e / (jnp.sum(e, axis=-1, keepdims=True) + jnp.exp(sk - m))
        return jnp.einsum('bgrqk,bkgd->bqgrd', p.astype(vi.dtype), vi)

    o = lax.map(step, jnp.arange(S // BLOCK_Q))
    return jnp.moveaxis(o, 0, 1).reshape(B, S, G, R, dh)


def compress_blocks(x, blk_idx, pe, w1, w2):
    B, _, G, dh = x.shape
    n_cmp, L = blk_idx.shape
    xb = x[:, blk_idx] + pe[:, None, :]
    xb = xb.transpose(0, 1, 3, 2, 4).reshape(B, n_cmp, G, L * dh)
    return jax.nn.gelu(xb @ w1) @ w2


def cmp_to_slc_overlap(n_cmp, n_slc):
    cs = np.arange(n_cmp) * NSA_CMP_STRIDE
    ce = cs + NSA_CMP_LEN
    ss = np.arange(n_slc) * NSA_SLC_LEN
    se = ss + NSA_SLC_LEN
    ov = np.clip(np.minimum(ce[:, None], se[None, :]) - np.maximum(cs[:, None], ss[None, :]), 0, None)
    return jnp.asarray(ov / NSA_CMP_LEN, dtype=jnp.float32)


def selected_block_attention(q, k, v, sel):
    B, S, G, R, dh = q.shape
    n_sel = sel.shape[-1]
    n_slc = S // NSA_SLC_LEN
    m = n_sel * NSA_SLC_LEN
    kblk = k.reshape(B, n_slc, NSA_SLC_LEN, G, dh).transpose(0, 3, 1, 2, 4)
    vblk = v.reshape(B, n_slc, NSA_SLC_LEN, G, dh).transpose(0, 3, 1, 2, 4)
    gather = jax.vmap(jax.vmap(lambda blocks, ids: blocks[ids]))
    offs = jnp.arange(NSA_SLC_LEN)
    scale = dh ** -0.5

    def step(i):
        start = i * NSA_SLC_QBLOCK
        qi = lax.dynamic_slice_in_dim(q, start, NSA_SLC_QBLOCK, axis=1)
        ids = lax.dynamic_slice_in_dim(sel, start, NSA_SLC_QBLOCK, axis=2)
        kg = gather(kblk, ids).reshape(B, G, NSA_SLC_QBLOCK, m, dh)
        vg = gather(vblk, ids).reshape(B, G, NSA_SLC_QBLOCK, m, dh)
        kpos = (ids[..., None] * NSA_SLC_LEN + offs).reshape(B, G, NSA_SLC_QBLOCK, m)
        qpos = start + jnp.arange(NSA_SLC_QBLOCK)
        mask = (kpos <= qpos[:, None])[:, :, None]
        s = jnp.einsum('bqgrd,bgqmd->bgrqm', qi, kg).astype(jnp.float32) * scale
        p = jax.nn.softmax(jnp.where(mask, s, NEG), axis=-1)
        return jnp.einsum('bgrqm,bgqmd->bqgrd', p.astype(vg.dtype), vg)

    o = lax.map(step, jnp.arange(S // NSA_SLC_QBLOCK))
    return jnp.moveaxis(o, 0, 1).reshape(B, S, G, R, dh)


def nsa_mixer(h, cos, sin, w_in, ck_pe, ck_w1, ck_w2, cv_pe, cv_w1, cv_w2, w_out):
    B, S, _ = h.shape
    G, R, dh = NSA_GROUPS, NSA_REP, HEAD_DIM
    scale = dh ** -0.5
    proj = h @ w_in
    q = apply_rope(proj[..., :Q_WIDTH].reshape(B, S, N_HEADS, dh), cos, sin).reshape(B, S, G, R, dh)
    kv = proj[..., Q_WIDTH:Q_WIDTH + NSA_KV_WIDTH].reshape(B, S, 3, 2, G, dh)
    gates = jax.nn.sigmoid(proj[..., Q_WIDTH + NSA_KV_WIDTH:].astype(jnp.float32)).reshape(B, S, 3, G, R, 1)
    k = apply_rope(kv[:, :, :, 0].reshape(B, S, 3 * G, dh), cos, sin).reshape(B, S, 3, G, dh)
    v = kv[:, :, :, 1]
    t = jnp.arange(S)

    n_cmp = (S - NSA_CMP_LEN) // NSA_CMP_STRIDE + 1
    blk_idx = jnp.arange(n_cmp)[:, None] * NSA_CMP_STRIDE + jnp.arange(NSA_CMP_LEN)[None, :]
    k_cmp = compress_blocks(k[:, :, 0], blk_idx, ck_pe, ck_w1, ck_w2)
    v_cmp = compress_blocks(v[:, :, 0], blk_idx, cv_pe, cv_w1, cv_w2)
    s_cmp = jnp.einsum('bsgrd,bngd->bgrsn', q, k_cmp).astype(jnp.float32) * scale
    cmp_valid = (jnp.arange(n_cmp) * NSA_CMP_STRIDE + NSA_CMP_LEN - 1)[None, :] <= t[:, None]
    p_cmp = jax.nn.softmax(jnp.where(cmp_valid, s_cmp, NEG), axis=-1) * cmp_valid
    o_cmp = jnp.einsum('bgrsn,bngd->bsgrd', p_cmp.astype(v_cmp.dtype), v_cmp)

    n_slc = S // NSA_SLC_LEN
    k_sel = min(NSA_TOPK, n_slc)
    imp = jnp.einsum('bgrsn,nj->bgsj', p_cmp, cmp_to_slc_overlap(n_cmp, n_slc))
    tb = (t // NSA_SLC_LEN)[:, None]
    j = jnp.arange(n_slc)[None, :]
    forced = (j == 0) | (j == tb) | (j == tb - 1)
    imp = jnp.where(j > tb, NEG, jnp.where(forced, NSA_FORCE_BONUS, imp))
    _, sel = lax.top_k(imp, k_sel)
    o_slc = selected_block_attention(q, k[:, :, 1], v[:, :, 1], sel)

    o_win = banded_attention(q, k[:, :, 2], v[:, :, 2], NSA_WINDOW)

    o = gates[:, :, 0] * o_cmp + gates[:, :, 1] * o_slc + gates[:, :, 2] * o_win
    return o.astype(h.dtype).reshape(B, S, Q_WIDTH) @ w_out


def swa_mixer(h, cos, sin, w_in, sinks, w_out):
    B, S, _ = h.shape
    proj = h @ w_in
    q = apply_rope(proj[..., :Q_WIDTH].reshape(B, S, N_HEADS, HEAD_DIM), cos, sin)
    q = q.reshape(B, S, SWA_KV_HEADS, SWA_REP, HEAD_DIM)
    kv = proj[..., Q_WIDTH:].reshape(B, S, 2, SWA_KV_HEADS, HEAD_DIM)
    k = apply_rope(kv[:, :, 0], cos, sin)
    v = kv[:, :, 1]
    o = banded_attention(q, k, v, SWA_WINDOW, sinks.reshape(SWA_KV_HEADS, SWA_REP))
    return o.reshape(B, S, Q_WIDTH) @ w_out


def fox_mixer(h, w_in, b_f, w_out):
    B, S, _ = h.shape
    dh = HEAD_DIM
    scale = dh ** -0.5
    proj = h @ w_in
    qkv = proj[..., :3 * FOX_HEADS * dh].reshape(B, S, 3, FOX_HEADS, dh)
    q, k, v = qkv[:, :, 0], qkv[:, :, 1], qkv[:, :, 2]
    log_f = jax.nn.log_sigmoid((proj[..., 3 * FOX_HEADS * dh:] + b_f).astype(jnp.float32))
    c = jnp.cumsum(log_f, axis=1).transpose(0, 2, 1)
    kpos = jnp.arange(S)

    def step(i):
        start = i * BLOCK_Q
        qi = lax.dynamic_slice_in_dim(q, start, BLOCK_Q, axis=1)
        ci = lax.dynamic_slice_in_dim(c, start, BLOCK_Q, axis=2)
        s = jnp.einsum('bqhd,bkhd->bhqk', qi, k).astype(jnp.float32) * scale
        s = s + ci[..., None] - c[:, :, None, :]
        qpos = start + jnp.arange(BLOCK_Q)
        p = jax.nn.softmax(jnp.where(kpos[None, :] <= qpos[:, None], s, NEG), axis=-1)
        return jnp.einsum('bhqk,bkhd->bqhd', p.astype(v.dtype), v)

    o = lax.map(step, jnp.arange(S // BLOCK_Q))
    return jnp.moveaxis(o, 0, 1).reshape(B, S, FOX_HEADS * dh) @ w_out


def _fwd_setup_inputs(seed: int = 0) -> dict:
    key = jax.random.key(seed)
    ks = iter(jax.random.split(key, 32))
    f32 = jnp.float32

    def nrm(shape, scale):
        return scale * jax.random.normal(next(ks), shape, f32)

    def gain(shape):
        return 1.0 + 0.01 * jax.random.normal(next(ks), shape, f32)

    L, dh, Hc = NSA_CMP_LEN, HEAD_DIM, NSA_CMP_HIDDEN
    return {
        'x': nrm((BATCH, SEQ, D_MODEL), 1.0),
        'ffn1_norm': gain((DEPTH, D_MODEL)),
        'ffn1_w_gu': nrm((DEPTH, D_MODEL, 2 * D_FF), D_MODEL ** -0.5),
        'ffn1_w_down': nrm((DEPTH, D_FF, D_MODEL), D_FF ** -0.5),
        'mix_norm': gain((DEPTH, D_MODEL)),
        'ffn2_norm': gain((DEPTH, D_MODEL)),
        'ffn2_w_gu': nrm((DEPTH, D_MODEL, 2 * D_FF), D_MODEL ** -0.5),
        'ffn2_w_down': nrm((DEPTH, D_FF, D_MODEL), D_FF ** -0.5),
        'nsa_w_in': nrm((N_A, D_MODEL, NSA_IN), D_MODEL ** -0.5),
        'nsa_ck_pe': nrm((N_A, L, dh), 0.1),
        'nsa_ck_w1': nrm((N_A, L * dh, Hc), (L * dh) ** -0.5),
        'nsa_ck_w2': nrm((N_A, Hc, dh), Hc ** -0.5),
        'nsa_cv_pe': nrm((N_A, L, dh), 0.1),
        'nsa_cv_w1': nrm((N_A, L * dh, Hc), (L * dh) ** -0.5),
        'nsa_cv_w2': nrm((N_A, Hc, dh), Hc ** -0.5),
        'nsa_w_out': nrm((N_A, Q_WIDTH, D_MODEL), Q_WIDTH ** -0.5),
        'swa_w_in': nrm((N_B, D_MODEL, SWA_IN), D_MODEL ** -0.5),
        'swa_sinks': nrm((N_B, N_HEADS), 1.0),
        'swa_w_out': nrm((N_B, Q_WIDTH, D_MODEL), Q_WIDTH ** -0.5),
        'fox_w_in': nrm((N_C, D_MODEL, FOX_IN), D_MODEL ** -0.5),
        'fox_b_f': jax.random.uniform(next(ks), (N_C, FOX_HEADS), f32, 1.0, 4.0),
        'fox_w_out': nrm((N_C, FOX_HEADS * HEAD_DIM, D_MODEL), (FOX_HEADS * HEAD_DIM) ** -0.5),
        'final_norm': gain((D_MODEL,)),
    }


def _fwd_reference(x, ffn1_norm, ffn1_w_gu, ffn1_w_down, mix_norm, ffn2_norm, ffn2_w_gu, ffn2_w_down,
              nsa_w_in, nsa_ck_pe, nsa_ck_w1, nsa_ck_w2, nsa_cv_pe, nsa_cv_w1, nsa_cv_w2, nsa_w_out,
              swa_w_in, swa_sinks, swa_w_out, fox_w_in, fox_b_f, fox_w_out, final_norm):
    S = x.shape[1]
    cos, sin = rope_tables(S)
    for i in range(DEPTH):
        kind, j = i % N_MIXERS, i // N_MIXERS
        x = x + 0.5 * swiglu(rmsnorm(x, ffn1_norm[i]), ffn1_w_gu[i], ffn1_w_down[i])
        h = rmsnorm(x, mix_norm[i])
        if kind == 0:
            y = nsa_mixer(h, cos, sin, nsa_w_in[j], nsa_ck_pe[j], nsa_ck_w1[j], nsa_ck_w2[j],
                          nsa_cv_pe[j], nsa_cv_w1[j], nsa_cv_w2[j], nsa_w_out[j])
        elif kind == 1:
            y = swa_mixer(h, cos, sin, swa_w_in[j], swa_sinks[j], swa_w_out[j])
        else:
            y = fox_mixer(h, fox_w_in[j], fox_b_f[j], fox_w_out[j])
        x = x + y
        x = x + 0.5 * swiglu(rmsnorm(x, ffn2_norm[i]), ffn2_w_gu[i], ffn2_w_down[i])
    return rmsnorm(x, final_norm)


import jax as _jax
import jax.numpy as _jnp

TWIN_FORMAT = 'train_step'
FWD_PARAMS = ['x', 'ffn1_norm', 'ffn1_w_gu', 'ffn1_w_down', 'mix_norm', 'ffn2_norm', 'ffn2_w_gu', 'ffn2_w_down', 'nsa_w_in', 'nsa_ck_pe', 'nsa_ck_w1', 'nsa_ck_w2', 'nsa_cv_pe', 'nsa_cv_w1', 'nsa_cv_w2', 'nsa_w_out', 'swa_w_in', 'swa_sinks', 'swa_w_out', 'fox_w_in', 'fox_b_f', 'fox_w_out', 'final_norm']
TWIN_WEIGHTS = ['ffn1_norm', 'ffn1_w_gu', 'ffn1_w_down', 'mix_norm', 'ffn2_norm', 'ffn2_w_gu', 'ffn2_w_down', 'nsa_w_in', 'nsa_ck_pe', 'nsa_ck_w1', 'nsa_ck_w2', 'nsa_cv_pe', 'nsa_cv_w1', 'nsa_cv_w2', 'nsa_w_out', 'swa_w_in', 'swa_sinks', 'swa_w_out', 'fox_w_in', 'fox_b_f', 'fox_w_out', 'final_norm']
TWIN_DIFF_INPUT = 'x'
TWIN_INPUTS = ['x', 'ffn1_norm', 'ffn1_w_gu', 'ffn1_w_down', 'mix_norm', 'ffn2_norm', 'ffn2_w_gu', 'ffn2_w_down', 'nsa_w_in', 'nsa_ck_pe', 'nsa_ck_w1', 'nsa_ck_w2', 'nsa_cv_pe', 'nsa_cv_w1', 'nsa_cv_w2', 'nsa_w_out', 'swa_w_in', 'swa_sinks', 'swa_w_out', 'fox_w_in', 'fox_b_f', 'fox_w_out', 'final_norm', 'loss_target', 'm_ffn1_norm', 'm_ffn1_w_gu', 'm_ffn1_w_down', 'm_mix_norm', 'm_ffn2_norm', 'm_ffn2_w_gu', 'm_ffn2_w_down', 'm_nsa_w_in', 'm_nsa_ck_pe', 'm_nsa_ck_w1', 'm_nsa_ck_w2', 'm_nsa_cv_pe', 'm_nsa_cv_w1', 'm_nsa_cv_w2', 'm_nsa_w_out', 'm_swa_w_in', 'm_swa_sinks', 'm_swa_w_out', 'm_fox_w_in', 'm_fox_b_f', 'm_fox_w_out', 'm_final_norm', 'v_ffn1_norm', 'v_ffn1_w_gu', 'v_ffn1_w_down', 'v_mix_norm', 'v_ffn2_norm', 'v_ffn2_w_gu', 'v_ffn2_w_down', 'v_nsa_w_in', 'v_nsa_ck_pe', 'v_nsa_ck_w1', 'v_nsa_ck_w2', 'v_nsa_cv_pe', 'v_nsa_cv_w1', 'v_nsa_cv_w2', 'v_nsa_w_out', 'v_swa_w_in', 'v_swa_sinks', 'v_swa_w_out', 'v_fox_w_in', 'v_fox_b_f', 'v_fox_w_out', 'v_final_norm']
TWIN_OUTPUTS = ['loss', 'grad_x', 'grad_ffn1_norm', 'grad_ffn1_w_gu', 'grad_ffn1_w_down', 'grad_mix_norm', 'grad_ffn2_norm', 'grad_ffn2_w_gu', 'grad_ffn2_w_down', 'grad_nsa_w_in', 'grad_nsa_ck_pe', 'grad_nsa_ck_w1', 'grad_nsa_ck_w2', 'grad_nsa_cv_pe', 'grad_nsa_cv_w1', 'grad_nsa_cv_w2', 'grad_nsa_w_out', 'grad_swa_w_in', 'grad_swa_sinks', 'grad_swa_w_out', 'grad_fox_w_in', 'grad_fox_b_f', 'grad_fox_w_out', 'grad_final_norm', 'delta_ffn1_norm', 'delta_ffn1_w_gu', 'delta_ffn1_w_down', 'delta_mix_norm', 'delta_ffn2_norm', 'delta_ffn2_w_gu', 'delta_ffn2_w_down', 'delta_nsa_w_in', 'delta_nsa_ck_pe', 'delta_nsa_ck_w1', 'delta_nsa_ck_w2', 'delta_nsa_cv_pe', 'delta_nsa_cv_w1', 'delta_nsa_cv_w2', 'delta_nsa_w_out', 'delta_swa_w_in', 'delta_swa_sinks', 'delta_swa_w_out', 'delta_fox_w_in', 'delta_fox_b_f', 'delta_fox_w_out', 'delta_final_norm', 'new_m_ffn1_norm', 'new_m_ffn1_w_gu', 'new_m_ffn1_w_down', 'new_m_mix_norm', 'new_m_ffn2_norm', 'new_m_ffn2_w_gu', 'new_m_ffn2_w_down', 'new_m_nsa_w_in', 'new_m_nsa_ck_pe', 'new_m_nsa_ck_w1', 'new_m_nsa_ck_w2', 'new_m_nsa_cv_pe', 'new_m_nsa_cv_w1', 'new_m_nsa_cv_w2', 'new_m_nsa_w_out', 'new_m_swa_w_in', 'new_m_swa_sinks', 'new_m_swa_w_out', 'new_m_fox_w_in', 'new_m_fox_b_f', 'new_m_fox_w_out', 'new_m_final_norm', 'new_v_ffn1_norm', 'new_v_ffn1_w_gu', 'new_v_ffn1_w_down', 'new_v_mix_norm', 'new_v_ffn2_norm', 'new_v_ffn2_w_gu', 'new_v_ffn2_w_down', 'new_v_nsa_w_in', 'new_v_nsa_ck_pe', 'new_v_nsa_ck_w1', 'new_v_nsa_ck_w2', 'new_v_nsa_cv_pe', 'new_v_nsa_cv_w1', 'new_v_nsa_cv_w2', 'new_v_nsa_w_out', 'new_v_swa_w_in', 'new_v_swa_sinks', 'new_v_swa_w_out', 'new_v_fox_w_in', 'new_v_fox_b_f', 'new_v_fox_w_out', 'new_v_final_norm']
TWIN_LEAF_KINDS = {'loss': 'loss', 'grad_x': 'grad_x', 'grad_ffn1_norm': 'grad_w', 'grad_ffn1_w_gu': 'grad_w', 'grad_ffn1_w_down': 'grad_w', 'grad_mix_norm': 'grad_w', 'grad_ffn2_norm': 'grad_w', 'grad_ffn2_w_gu': 'grad_w', 'grad_ffn2_w_down': 'grad_w', 'grad_nsa_w_in': 'grad_w', 'grad_nsa_ck_pe': 'grad_w', 'grad_nsa_ck_w1': 'grad_w', 'grad_nsa_ck_w2': 'grad_w', 'grad_nsa_cv_pe': 'grad_w', 'grad_nsa_cv_w1': 'grad_w', 'grad_nsa_cv_w2': 'grad_w', 'grad_nsa_w_out': 'grad_w', 'grad_swa_w_in': 'grad_w', 'grad_swa_sinks': 'grad_w', 'grad_swa_w_out': 'grad_w', 'grad_fox_w_in': 'grad_w', 'grad_fox_b_f': 'grad_w', 'grad_fox_w_out': 'grad_w', 'grad_final_norm': 'grad_w', 'delta_ffn1_norm': 'delta_w', 'delta_ffn1_w_gu': 'delta_w', 'delta_ffn1_w_down': 'delta_w', 'delta_mix_norm': 'delta_w', 'delta_ffn2_norm': 'delta_w', 'delta_ffn2_w_gu': 'delta_w', 'delta_ffn2_w_down': 'delta_w', 'delta_nsa_w_in': 'delta_w', 'delta_nsa_ck_pe': 'delta_w', 'delta_nsa_ck_w1': 'delta_w', 'delta_nsa_ck_w2': 'delta_w', 'delta_nsa_cv_pe': 'delta_w', 'delta_nsa_cv_w1': 'delta_w', 'delta_nsa_cv_w2': 'delta_w', 'delta_nsa_w_out': 'delta_w', 'delta_swa_w_in': 'delta_w', 'delta_swa_sinks': 'delta_w', 'delta_swa_w_out': 'delta_w', 'delta_fox_w_in': 'delta_w', 'delta_fox_b_f': 'delta_w', 'delta_fox_w_out': 'delta_w', 'delta_final_norm': 'delta_w', 'new_m_ffn1_norm': 'new_m', 'new_m_ffn1_w_gu': 'new_m', 'new_m_ffn1_w_down': 'new_m', 'new_m_mix_norm': 'new_m', 'new_m_ffn2_norm': 'new_m', 'new_m_ffn2_w_gu': 'new_m', 'new_m_ffn2_w_down': 'new_m', 'new_m_nsa_w_in': 'new_m', 'new_m_nsa_ck_pe': 'new_m', 'new_m_nsa_ck_w1': 'new_m', 'new_m_nsa_ck_w2': 'new_m', 'new_m_nsa_cv_pe': 'new_m', 'new_m_nsa_cv_w1': 'new_m', 'new_m_nsa_cv_w2': 'new_m', 'new_m_nsa_w_out': 'new_m', 'new_m_swa_w_in': 'new_m', 'new_m_swa_sinks': 'new_m', 'new_m_swa_w_out': 'new_m', 'new_m_fox_w_in': 'new_m', 'new_m_fox_b_f': 'new_m', 'new_m_fox_w_out': 'new_m', 'new_m_final_norm': 'new_m', 'new_v_ffn1_norm': 'new_v', 'new_v_ffn1_w_gu': 'new_v', 'new_v_ffn1_w_down': 'new_v', 'new_v_mix_norm': 'new_v', 'new_v_ffn2_norm': 'new_v', 'new_v_ffn2_w_gu': 'new_v', 'new_v_ffn2_w_down': 'new_v', 'new_v_nsa_w_in': 'new_v', 'new_v_nsa_ck_pe': 'new_v', 'new_v_nsa_ck_w1': 'new_v', 'new_v_nsa_ck_w2': 'new_v', 'new_v_nsa_cv_pe': 'new_v', 'new_v_nsa_cv_w1': 'new_v', 'new_v_nsa_cv_w2': 'new_v', 'new_v_nsa_w_out': 'new_v', 'new_v_swa_w_in': 'new_v', 'new_v_swa_sinks': 'new_v', 'new_v_swa_w_out': 'new_v', 'new_v_fox_w_in': 'new_v', 'new_v_fox_b_f': 'new_v', 'new_v_fox_w_out': 'new_v', 'new_v_final_norm': 'new_v'}


def _forward(args):
    return _fwd_reference(*[args[k] for k in FWD_PARAMS])


def _output_shape():
    out = _jax.eval_shape(lambda: _forward(_fwd_setup_inputs(0)))
    return out.shape, out.dtype

N_MICROBATCH = 1
ADAM_LR = 0.001
ADAM_B1 = 0.9
ADAM_B2 = 0.999
ADAM_EPS = 1e-08
ADAM_WD = 0.01
ADAM_STEP = 10
PER_EXAMPLE_BATCH_AXIS = {'x': 0, 'loss_target': 0}
SHARED_INPUTS = []
_WEIGHT_DTYPES = {'ffn1_norm': _jnp.float32, 'ffn1_w_gu': _jnp.float32, 'ffn1_w_down': _jnp.float32, 'mix_norm': _jnp.float32, 'ffn2_norm': _jnp.float32, 'ffn2_w_gu': _jnp.float32, 'ffn2_w_down': _jnp.float32, 'nsa_w_in': _jnp.float32, 'nsa_ck_pe': _jnp.float32, 'nsa_ck_w1': _jnp.float32, 'nsa_ck_w2': _jnp.float32, 'nsa_cv_pe': _jnp.float32, 'nsa_cv_w1': _jnp.float32, 'nsa_cv_w2': _jnp.float32, 'nsa_w_out': _jnp.float32, 'swa_w_in': _jnp.float32, 'swa_sinks': _jnp.float32, 'swa_w_out': _jnp.float32, 'fox_w_in': _jnp.float32, 'fox_b_f': _jnp.float32, 'fox_w_out': _jnp.float32, 'final_norm': _jnp.float32}
MOMENT_SCALE = {'ffn1_norm': 5.441380e-02, 'ffn1_w_gu': 2.301258e-02, 'ffn1_w_down': 3.753767e-02, 'mix_norm': 5.016727e-02, 'ffn2_norm': 5.102367e-02, 'ffn2_w_gu': 2.108067e-02, 'ffn2_w_down': 3.441598e-02, 'nsa_w_in': 3.007458e-02, 'nsa_ck_pe': 4.067910e-03, 'nsa_ck_w1': 2.418069e-02, 'nsa_ck_w2': 2.978829e-02, 'nsa_cv_pe': 4.307148e-02, 'nsa_cv_w1': 7.256044e-02, 'nsa_cv_w2': 1.206913e-01, 'nsa_w_out': 3.981624e-02, 'swa_w_in': 4.264173e-02, 'swa_sinks': 2.870207e-02, 'swa_w_out': 3.657996e-02, 'fox_w_in': 3.292139e-02, 'fox_b_f': 1.459149e-01, 'fox_w_out': 4.273660e-02, 'final_norm': 1.602256e+01}


def _to_microbatches(a, axis):
    t = _jnp.moveaxis(a, axis, 0)
    t = t.reshape((N_MICROBATCH, t.shape[0] // N_MICROBATCH) + t.shape[1:])
    return _jnp.moveaxis(t, 1, axis + 1)


def setup_inputs(seed: int = 0) -> dict:
    inp = _fwd_setup_inputs(seed)
    key = _jax.random.fold_in(_jax.random.key(seed), 7919)
    shape, _ = _output_shape()
    out = dict(inp)
    out["loss_target"] = _jax.random.normal(_jax.random.fold_in(key, 0), shape, _jnp.float32)
    for i, name in enumerate(TWIN_WEIGHTS):
        w = inp[name].astype(_jnp.float32)
        if MOMENT_SCALE is None:
            s = _jnp.sqrt(_jnp.mean(_jnp.square(w)) + 1e-30)
        else:
            s = MOMENT_SCALE[name]
        km, kv = _jax.random.split(_jax.random.fold_in(key, i + 1))
        out[name] = w
        out["m_" + name] = s * _jax.random.normal(km, w.shape, _jnp.float32)
        out["v_" + name] = (s * s) * _jax.random.uniform(kv, w.shape, _jnp.float32, 0.5, 1.5)
    if N_MICROBATCH > 1:
        for name, axis in PER_EXAMPLE_BATCH_AXIS.items():
            out[name] = _to_microbatches(out[name], axis)
    return {'x': out['x'], 'ffn1_norm': out['ffn1_norm'], 'ffn1_w_gu': out['ffn1_w_gu'], 'ffn1_w_down': out['ffn1_w_down'], 'mix_norm': out['mix_norm'], 'ffn2_norm': out['ffn2_norm'], 'ffn2_w_gu': out['ffn2_w_gu'], 'ffn2_w_down': out['ffn2_w_down'], 'nsa_w_in': out['nsa_w_in'], 'nsa_ck_pe': out['nsa_ck_pe'], 'nsa_ck_w1': out['nsa_ck_w1'], 'nsa_ck_w2': out['nsa_ck_w2'], 'nsa_cv_pe': out['nsa_cv_pe'], 'nsa_cv_w1': out['nsa_cv_w1'], 'nsa_cv_w2': out['nsa_cv_w2'], 'nsa_w_out': out['nsa_w_out'], 'swa_w_in': out['swa_w_in'], 'swa_sinks': out['swa_sinks'], 'swa_w_out': out['swa_w_out'], 'fox_w_in': out['fox_w_in'], 'fox_b_f': out['fox_b_f'], 'fox_w_out': out['fox_w_out'], 'final_norm': out['final_norm'], 'loss_target': out['loss_target'], 'm_ffn1_norm': out['m_ffn1_norm'], 'm_ffn1_w_gu': out['m_ffn1_w_gu'], 'm_ffn1_w_down': out['m_ffn1_w_down'], 'm_mix_norm': out['m_mix_norm'], 'm_ffn2_norm': out['m_ffn2_norm'], 'm_ffn2_w_gu': out['m_ffn2_w_gu'], 'm_ffn2_w_down': out['m_ffn2_w_down'], 'm_nsa_w_in': out['m_nsa_w_in'], 'm_nsa_ck_pe': out['m_nsa_ck_pe'], 'm_nsa_ck_w1': out['m_nsa_ck_w1'], 'm_nsa_ck_w2': out['m_nsa_ck_w2'], 'm_nsa_cv_pe': out['m_nsa_cv_pe'], 'm_nsa_cv_w1': out['m_nsa_cv_w1'], 'm_nsa_cv_w2': out['m_nsa_cv_w2'], 'm_nsa_w_out': out['m_nsa_w_out'], 'm_swa_w_in': out['m_swa_w_in'], 'm_swa_sinks': out['m_swa_sinks'], 'm_swa_w_out': out['m_swa_w_out'], 'm_fox_w_in': out['m_fox_w_in'], 'm_fox_b_f': out['m_fox_b_f'], 'm_fox_w_out': out['m_fox_w_out'], 'm_final_norm': out['m_final_norm'], 'v_ffn1_norm': out['v_ffn1_norm'], 'v_ffn1_w_gu': out['v_ffn1_w_gu'], 'v_ffn1_w_down': out['v_ffn1_w_down'], 'v_mix_norm': out['v_mix_norm'], 'v_ffn2_norm': out['v_ffn2_norm'], 'v_ffn2_w_gu': out['v_ffn2_w_gu'], 'v_ffn2_w_down': out['v_ffn2_w_down'], 'v_nsa_w_in': out['v_nsa_w_in'], 'v_nsa_ck_pe': out['v_nsa_ck_pe'], 'v_nsa_ck_w1': out['v_nsa_ck_w1'], 'v_nsa_ck_w2': out['v_nsa_ck_w2'], 'v_nsa_cv_pe': out['v_nsa_cv_pe'], 'v_nsa_cv_w1': out['v_nsa_cv_w1'], 'v_nsa_cv_w2': out['v_nsa_cv_w2'], 'v_nsa_w_out': out['v_nsa_w_out'], 'v_swa_w_in': out['v_swa_w_in'], 'v_swa_sinks': out['v_swa_sinks'], 'v_swa_w_out': out['v_swa_w_out'], 'v_fox_w_in': out['v_fox_w_in'], 'v_fox_b_f': out['v_fox_b_f'], 'v_fox_w_out': out['v_fox_w_out'], 'v_final_norm': out['v_final_norm']}


def _loss(weights, diff, rest, loss_target):
    with _jax.named_scope("forward"):
        args = {**rest, TWIN_DIFF_INPUT: diff, **{k: w.astype(_WEIGHT_DTYPES[k]) for k, w in weights.items()}}
        y = _forward(args)
    with _jax.named_scope("loss_head"):
        err = _jnp.square(y.astype(_jnp.float32) - loss_target)
        return 0.5 * _jnp.sum(_jnp.mean(err, axis=-1)) if err.ndim else 0.5 * err


def _adamw(w, g, m, v):
    m = ADAM_B1 * m + (1.0 - ADAM_B1) * g
    v = ADAM_B2 * v + (1.0 - ADAM_B2) * _jnp.square(g)
    m_hat = m / (1.0 - ADAM_B1 ** ADAM_STEP)
    v_hat = v / (1.0 - ADAM_B2 ** ADAM_STEP)
    delta = -ADAM_LR * (m_hat / (_jnp.sqrt(v_hat) + ADAM_EPS) + ADAM_WD * w)
    return delta, m, v


def reference(x, ffn1_norm, ffn1_w_gu, ffn1_w_down, mix_norm, ffn2_norm, ffn2_w_gu, ffn2_w_down, nsa_w_in, nsa_ck_pe, nsa_ck_w1, nsa_ck_w2, nsa_cv_pe, nsa_cv_w1, nsa_cv_w2, nsa_w_out, swa_w_in, swa_sinks, swa_w_out, fox_w_in, fox_b_f, fox_w_out, final_norm, loss_target, m_ffn1_norm, m_ffn1_w_gu, m_ffn1_w_down, m_mix_norm, m_ffn2_norm, m_ffn2_w_gu, m_ffn2_w_down, m_nsa_w_in, m_nsa_ck_pe, m_nsa_ck_w1, m_nsa_ck_w2, m_nsa_cv_pe, m_nsa_cv_w1, m_nsa_cv_w2, m_nsa_w_out, m_swa_w_in, m_swa_sinks, m_swa_w_out, m_fox_w_in, m_fox_b_f, m_fox_w_out, m_final_norm, v_ffn1_norm, v_ffn1_w_gu, v_ffn1_w_down, v_mix_norm, v_ffn2_norm, v_ffn2_w_gu, v_ffn2_w_down, v_nsa_w_in, v_nsa_ck_pe, v_nsa_ck_w1, v_nsa_ck_w2, v_nsa_cv_pe, v_nsa_cv_w1, v_nsa_cv_w2, v_nsa_w_out, v_swa_w_in, v_swa_sinks, v_swa_w_out, v_fox_w_in, v_fox_b_f, v_fox_w_out, v_final_norm):
    given = dict(x=x, ffn1_norm=ffn1_norm, ffn1_w_gu=ffn1_w_gu, ffn1_w_down=ffn1_w_down, mix_norm=mix_norm, ffn2_norm=ffn2_norm, ffn2_w_gu=ffn2_w_gu, ffn2_w_down=ffn2_w_down, nsa_w_in=nsa_w_in, nsa_ck_pe=nsa_ck_pe, nsa_ck_w1=nsa_ck_w1, nsa_ck_w2=nsa_ck_w2, nsa_cv_pe=nsa_cv_pe, nsa_cv_w1=nsa_cv_w1, nsa_cv_w2=nsa_cv_w2, nsa_w_out=nsa_w_out, swa_w_in=swa_w_in, swa_sinks=swa_sinks, swa_w_out=swa_w_out, fox_w_in=fox_w_in, fox_b_f=fox_b_f, fox_w_out=fox_w_out, final_norm=final_norm, loss_target=loss_target, m_ffn1_norm=m_ffn1_norm, m_ffn1_w_gu=m_ffn1_w_gu, m_ffn1_w_down=m_ffn1_w_down, m_mix_norm=m_mix_norm, m_ffn2_norm=m_ffn2_norm, m_ffn2_w_gu=m_ffn2_w_gu, m_ffn2_w_down=m_ffn2_w_down, m_nsa_w_in=m_nsa_w_in, m_nsa_ck_pe=m_nsa_ck_pe, m_nsa_ck_w1=m_nsa_ck_w1, m_nsa_ck_w2=m_nsa_ck_w2, m_nsa_cv_pe=m_nsa_cv_pe, m_nsa_cv_w1=m_nsa_cv_w1, m_nsa_cv_w2=m_nsa_cv_w2, m_nsa_w_out=m_nsa_w_out, m_swa_w_in=m_swa_w_in, m_swa_sinks=m_swa_sinks, m_swa_w_out=m_swa_w_out, m_fox_w_in=m_fox_w_in, m_fox_b_f=m_fox_b_f, m_fox_w_out=m_fox_w_out, m_final_norm=m_final_norm, v_ffn1_norm=v_ffn1_norm, v_ffn1_w_gu=v_ffn1_w_gu, v_ffn1_w_down=v_ffn1_w_down, v_mix_norm=v_mix_norm, v_ffn2_norm=v_ffn2_norm, v_ffn2_w_gu=v_ffn2_w_gu, v_ffn2_w_down=v_ffn2_w_down, v_nsa_w_in=v_nsa_w_in, v_nsa_ck_pe=v_nsa_ck_pe, v_nsa_ck_w1=v_nsa_ck_w1, v_nsa_ck_w2=v_nsa_ck_w2, v_nsa_cv_pe=v_nsa_cv_pe, v_nsa_cv_w1=v_nsa_cv_w1, v_nsa_cv_w2=v_nsa_cv_w2, v_nsa_w_out=v_nsa_w_out, v_swa_w_in=v_swa_w_in, v_swa_sinks=v_swa_sinks, v_swa_w_out=v_swa_w_out, v_fox_w_in=v_fox_w_in, v_fox_b_f=v_fox_b_f, v_fox_w_out=v_fox_w_out, v_final_norm=v_final_norm)
    weights = {n: given[n] for n in TWIN_WEIGHTS}
    shared = {n: given[n] for n in SHARED_INPUTS}
    per_example = {n: given[n] for n in ['x']}
    grad_fn = _jax.value_and_grad(_loss, argnums=(0, 1))

    def one_microbatch(ex, loss_target):
        ex = dict(ex)
        diff = ex.pop(TWIN_DIFF_INPUT)
        return grad_fn(weights, diff, {**shared, **ex}, loss_target)

    if N_MICROBATCH == 1:
        loss, (grad_w, grad_x) = one_microbatch(per_example, given["loss_target"])
    else:
        def body(carry, xs):
            loss_sum, grad_sum = carry
            l_k, (gw_k, gx_k) = one_microbatch(xs[0], xs[1])
            with _jax.named_scope("update"):
                return (loss_sum + l_k, _jax.tree.map(_jnp.add, grad_sum, gw_k)), gx_k

        init = (_jnp.zeros((), _jnp.float32), _jax.tree.map(_jnp.zeros_like, weights))
        (loss, grad_w), grad_x = _jax.lax.scan(body, init, (per_example, given["loss_target"]))
    with _jax.named_scope("update"):
        delta_w, new_m, new_v = {}, {}, {}
        for n in TWIN_WEIGHTS:
            delta_w[n], new_m[n], new_v[n] = _adamw(weights[n], grad_w[n], given["m_" + n], given["v_" + n])
    return (loss, grad_x, *[grad_w[n] for n in TWIN_WEIGHTS], *[delta_w[n] for n in TWIN_WEIGHTS],
            *[new_m[n] for n in TWIN_WEIGHTS], *[new_v[n] for n in TWIN_WEIGHTS])
```

```python
import math

import numpy as np
import jax
import jax.numpy as jnp
from jax import lax
from jax.experimental import pallas as pl
from jax.experimental.pallas import tpu as pltpu

F32 = jnp.float32
_CDT = jnp.bfloat16
_VMEM_LIMIT = 56 * 1024 * 1024

D_MODEL = 1024
DEPTH = 4
DH = 64
N_HEADS = 16
RMS_EPS = 1e-6
NEG = -1e30
SCALE = DH ** -0.5
BQ = 256
LANES = 128
NSA_G, NSA_R = 4, 4
NSA_WINDOW = 512
NSA_TOPK = 16
NSA_BONUS = 1e4
SWA_G, SWA_R = 2, 8
SWA_WINDOW = 128
NSA_IN, NSA_IN_PAD = 2608, 2688
FOX_IN, FOX_IN_PAD = 3088, 3200
LR, B1, B2, EPS, WD, STEP = 0.001, 0.9, 0.999, 1e-08, 0.01, 10
PACK_COLS, PACK_ROWS = 1024, 512
MESH = pl.DeviceIdType.MESH

SHARDED = [
    ("ffn1_w_gu", 2), ("ffn1_w_down", 1), ("ffn2_w_gu", 2), ("ffn2_w_down", 1),
    ("nsa_w_in", 2), ("nsa_ck_w1", 1), ("nsa_cv_w1", 1), ("nsa_w_out", 1),
    ("swa_w_in", 2), ("swa_w_out", 1), ("fox_w_in", 2), ("fox_w_out", 1),
]
REPLICATED = ["ffn1_norm", "mix_norm", "ffn2_norm", "nsa_ck_pe", "nsa_ck_w2", "nsa_cv_pe", "nsa_cv_w2",
              "swa_sinks", "fox_b_f", "final_norm"]
WEIGHTS = ['ffn1_norm', 'ffn1_w_gu', 'ffn1_w_down', 'mix_norm', 'ffn2_norm', 'ffn2_w_gu', 'ffn2_w_down',
           'nsa_w_in', 'nsa_ck_pe', 'nsa_ck_w1', 'nsa_ck_w2', 'nsa_cv_pe', 'nsa_cv_w1', 'nsa_cv_w2', 'nsa_w_out',
           'swa_w_in', 'swa_sinks', 'swa_w_out', 'fox_w_in', 'fox_b_f', 'fox_w_out', 'final_norm']


def _params(**kw):
    return pltpu.CompilerParams(vmem_limit_bytes=_VMEM_LIMIT, **kw)


def _dot(a, b):
    return lax.dot_general(a, b, (((1,), (0,)), ((), ())), preferred_element_type=F32)


def _dot_nt(a, b):
    return lax.dot_general(a, b, (((1,), (1,)), ((), ())), preferred_element_type=F32)


def _dot_tn(a, b):
    return lax.dot_general(a, b, (((0,), (0,)), ((), ())), preferred_element_type=F32)


def _split3(x):
    hi = x.astype(jnp.bfloat16)
    r1 = x - hi.astype(F32)
    mid = r1.astype(jnp.bfloat16)
    lo = (r1 - mid.astype(F32)).astype(jnp.bfloat16)
    return hi, mid, lo


def _dot_exact(x, p):
    hi, mid, lo = _split3(x)
    return _dot(hi, p) + _dot(mid, p) + _dot(lo, p)


def _dot_exact_left(p, x):
    hi, mid, lo = _split3(x)
    return _dot(p, hi) + _dot(p, mid) + _dot(p, lo)


def _mm(a, b, *, ta=False, tb=False, tm, tn, out_dtype=F32, res=None, alpha=1.0, name):
    M = a.shape[1] if ta else a.shape[0]
    K = a.shape[0] if ta else a.shape[1]
    N = b.shape[0] if tb else b.shape[1]
    assert (b.shape[1] if tb else b.shape[0]) == K
    tm, tn = min(tm, M), min(tn, N)
    assert M % tm == 0 and N % tn == 0, (name, M, N, tm, tn)
    a_spec = pl.BlockSpec((K, tm), lambda i, j: (0, i)) if ta else pl.BlockSpec((tm, K), lambda i, j: (i, 0))
    b_spec = pl.BlockSpec((tn, K), lambda i, j: (j, 0)) if tb else pl.BlockSpec((K, tn), lambda i, j: (0, j))
    o_spec = pl.BlockSpec((tm, tn), lambda i, j: (i, j))
    dn = (((0 if ta else 1,), (1 if tb else 0,)), ((), ()))
    has_res = res is not None

    def body(*refs):
        a_ref, b_ref = refs[0], refs[1]
        o_ref = refs[-1]
        acc = lax.dot_general(a_ref[...].astype(_CDT), b_ref[...].astype(_CDT), dn, preferred_element_type=F32)
        if alpha != 1.0:
            acc = acc * alpha
        if has_res:
            acc = refs[2][...] + acc
        o_ref[...] = acc.astype(out_dtype)

    ins = [a, b] + ([res] if has_res else [])
    specs = [a_spec, b_spec] + ([o_spec] if has_res else [])
    return pl.pallas_call(body, out_shape=jax.ShapeDtypeStruct((M, N), out_dtype), grid=(M // tm, N // tn),
                          in_specs=specs, out_specs=o_spec, compiler_params=_params(), name=name)(*ins)


def _rows2d(a):
    return a.reshape(-1, a.shape[-1])


def _row_tile(rows, cols, itemsize=4, budget=2 * 1024 * 1024):
    t = rows
    while t % 2 == 0 and t * cols * itemsize > budget and (t // 2) % 8 == 0:
        t //= 2
    return t


def _addn(*xs, name):
    shape = xs[0].shape
    x2 = [_rows2d(x) for x in xs]
    R, C = x2[0].shape
    tr = _row_tile(R, C)

    def body(*refs):
        acc = refs[0][...]
        for r in refs[1:-1]:
            acc = acc + r[...]
        refs[-1][...] = acc

    spec = pl.BlockSpec((tr, C), lambda i: (i, 0))
    out = pl.pallas_call(body, out_shape=jax.ShapeDtypeStruct((R, C), F32), grid=(R // tr,),
                         in_specs=[spec] * len(x2), out_specs=spec, compiler_params=_params(), name=name)(*x2)
    return out.reshape(shape)


def _rms_fwd(x, g, *, name):
    S, D = x.shape
    tr = 256

    def body(x_ref, g_ref, h_ref):
        xv = x_ref[...]
        rstd = lax.rsqrt(jnp.mean(xv * xv, axis=-1, keepdims=True) + RMS_EPS)
        h_ref[...] = (xv * rstd * g_ref[...]).astype(_CDT)

    return pl.pallas_call(body, out_shape=jax.ShapeDtypeStruct((S, D), _CDT), grid=(S // tr,),
                          in_specs=[pl.BlockSpec((tr, D), lambda i: (i, 0)), pl.BlockSpec((1, D), lambda i: (0, 0))],
                          out_specs=pl.BlockSpec((tr, D), lambda i: (i, 0)), compiler_params=_params(),
                          name=name)(x, g.reshape(1, D))


def _rms_bwd(dh, x, g, dres, *, name):
    S, D = x.shape
    tr = 256

    def body(dh_ref, x_ref, g_ref, dres_ref, dx_ref, dg_ref):
        xv = x_ref[...]
        rstd = lax.rsqrt(jnp.mean(xv * xv, axis=-1, keepdims=True) + RMS_EPS)
        xhat = xv * rstd
        dhv = dh_ref[...]
        dxhat = dhv * g_ref[...]
        dx_ref[...] = dres_ref[...] + rstd * (dxhat - xhat * jnp.mean(dxhat * xhat, axis=-1, keepdims=True))

        @pl.when(pl.program_id(0) == 0)
        def _():
            dg_ref[...] = jnp.zeros_like(dg_ref)

        dg_ref[...] += jnp.sum(dhv * xhat, axis=0, keepdims=True)

    row = pl.BlockSpec((tr, D), lambda i: (i, 0))
    vec = pl.BlockSpec((1, D), lambda i: (0, 0))
    dx, dg = pl.pallas_call(body, out_shape=(jax.ShapeDtypeStruct((S, D), F32), jax.ShapeDtypeStruct((1, D), F32)),
                            grid=(S // tr,), in_specs=[row, row, vec, row], out_specs=(row, vec),
                            compiler_params=_params(), name=name)(dh, x, g.reshape(1, D), dres)
    return dx, dg.reshape(D)


def _swiglu_fwd(gu, *, name):
    S, W = gu.shape
    H = W // 2
    tr = 256

    def body(gu_ref, a_ref):
        g = gu_ref[:, :H]
        u = gu_ref[:, H:]
        a_ref[...] = (g * jax.nn.sigmoid(g) * u).astype(_CDT)

    return pl.pallas_call(body, out_shape=jax.ShapeDtypeStruct((S, H), _CDT), grid=(S // tr,),
                          in_specs=[pl.BlockSpec((tr, W), lambda i: (i, 0))],
                          out_specs=pl.BlockSpec((tr, H), lambda i: (i, 0)), compiler_params=_params(), name=name)(gu)


def _swiglu_bwd(gu, da, *, name):
    S, W = gu.shape
    H = W // 2
    tr = 256

    def body(gu_ref, da_ref, d_ref):
        g = gu_ref[:, :H]
        u = gu_ref[:, H:]
        dav = da_ref[...]
        sg = jax.nn.sigmoid(g)
        silu = g * sg
        d_ref[:, :H] = dav * u * (sg + silu * (1.0 - sg))
        d_ref[:, H:] = dav * silu

    return pl.pallas_call(body, out_shape=jax.ShapeDtypeStruct((S, W), F32), grid=(S // tr,),
                          in_specs=[pl.BlockSpec((tr, W), lambda i: (i, 0)), pl.BlockSpec((tr, H), lambda i: (i, 0))],
                          out_specs=pl.BlockSpec((tr, W), lambda i: (i, 0)), compiler_params=_params(),
                          name=name)(gu, da)


def _loss_head(x, g, tgt, *, name):
    S, D = x.shape
    tr = 256

    def body(x_ref, g_ref, t_ref, loss_ref, dx_ref, dg_ref):
        xv = x_ref[...]
        rstd = lax.rsqrt(jnp.mean(xv * xv, axis=-1, keepdims=True) + RMS_EPS)
        xhat = xv * rstd
        err = xhat * g_ref[...] - t_ref[...]
        part = 0.5 * jnp.sum(jnp.mean(err * err, axis=-1, keepdims=True), axis=0, keepdims=True)
        dy = err * (1.0 / D)
        dxhat = dy * g_ref[...]
        dx_ref[...] = rstd * (dxhat - xhat * jnp.mean(dxhat * xhat, axis=-1, keepdims=True))

        @pl.when(pl.program_id(0) == 0)
        def _():
            dg_ref[...] = jnp.zeros_like(dg_ref)
            loss_ref[...] = jnp.zeros_like(loss_ref)

        dg_ref[...] += jnp.sum(dy * xhat, axis=0, keepdims=True)
        loss_ref[...] += jnp.broadcast_to(part, loss_ref.shape)

    row = pl.BlockSpec((tr, D), lambda i: (i, 0))
    vec = pl.BlockSpec((1, D), lambda i: (0, 0))
    loss, dx, dg = pl.pallas_call(
        body, out_shape=(jax.ShapeDtypeStruct((8, LANES), F32), jax.ShapeDtypeStruct((S, D), F32),
                         jax.ShapeDtypeStruct((1, D), F32)),
        grid=(S // tr,), in_specs=[row, vec, row], out_specs=(pl.BlockSpec((8, LANES), lambda i: (0, 0)), row, vec),
        compiler_params=_params(), name=name)(x, g.reshape(1, D), tgt)
    return loss[0, 0], dx, dg.reshape(D)


def _rope_tables(S):
    inv = 10000.0 ** (-jnp.arange(0, DH, 2, dtype=F32) / DH)
    ang = jnp.arange(S, dtype=F32)[:, None] * inv[None, :]
    cos, sin = jnp.cos(ang), jnp.sin(ang)
    return jnp.concatenate([cos, cos], -1), jnp.concatenate([-sin, sin], -1)


def _swap_matrix():
    p = np.zeros((DH, DH), np.float32)
    for j in range(DH // 2):
        p[j + DH // 2, j] = 1.0
        p[j, j + DH // 2] = 1.0
    return jnp.asarray(p, jnp.bfloat16)


def _rope(x, cc, ss, *, name):
    n, S, _ = x.shape

    def body(x_ref, c_ref, s_ref, p_ref, o_ref):
        xv = x_ref[0]
        o_ref[0] = xv * c_ref[...] + _dot_exact(xv, p_ref[...]) * s_ref[...]

    tab = pl.BlockSpec((S, DH), lambda i: (0, 0))
    blk = pl.BlockSpec((1, S, DH), lambda i: (i, 0, 0))
    return pl.pallas_call(body, out_shape=jax.ShapeDtypeStruct(x.shape, F32), grid=(n,),
                          in_specs=[blk, tab, tab, pl.BlockSpec((DH, DH), lambda i: (0, 0))], out_specs=blk,
                          compiler_params=_params(), name=name)(x, cc, ss, _swap_matrix())


def _key_range(kind, i, window, Sk):
    if kind == "cmp":
        return 0, Sk
    hi = (i + 1) * BQ
    if kind == "band":
        return max(0, i * BQ - window), hi
    return 0, hi


def _attn_mask(kind, i, lo, hi, window):
    shape = (BQ, hi - lo)
    qpos = i * BQ + lax.broadcasted_iota(jnp.int32, shape, 0)
    kpos = lo + lax.broadcasted_iota(jnp.int32, shape, 1)
    if kind == "cmp":
        return kpos * 16 + 31 <= qpos
    mask = kpos <= qpos
    if kind == "band":
        mask = mask & (qpos - kpos < window)
    return mask


def _sel_expand(lo, hi):
    shape = (LANES, hi - lo)
    j = lax.broadcasted_iota(jnp.int32, shape, 0)
    key = lo + lax.broadcasted_iota(jnp.int32, shape, 1)
    return (jnp.right_shift(key, 6) == j).astype(_CDT)


def _scores(kind, i, lo, hi, window, qb, kb, ccol_ref, crow_ref, sel_ref):
    s = _dot_nt(qb, kb) * SCALE
    if ccol_ref is not None:
        s = s + ccol_ref[0, i * BQ:(i + 1) * BQ, :] - crow_ref[0, :, lo:hi]
    mask = _attn_mask(kind, i, lo, hi, window)
    if sel_ref is not None:
        chosen = _dot_tn(sel_ref[0, :, i * BQ:(i + 1) * BQ].astype(_CDT), _sel_expand(lo, hi))
        mask = mask & (chosen > 0.5)
    return jnp.where(mask, s, NEG), mask


def _attn_fwd(q, k, v, *, kind, window=0, bias=None, sinks=None, selT=None, ovT=None, name):
    G, R, S, _ = q.shape
    Sk = k.shape[1]
    nq = S // BQ
    n_slc = S // 64
    has_bias, has_sink, has_sel, is_cmp = bias is not None, sinks is not None, selT is not None, kind == "cmp"

    def body(*refs):
        it = iter(refs)
        q_ref, k_ref, v_ref = next(it), next(it), next(it)
        ccol_ref, crow_ref = (next(it), next(it)) if has_bias else (None, None)
        sink_ref = next(it) if has_sink else None
        sel_ref = next(it) if has_sel else None
        ov_ref = next(it) if is_cmp else None
        o_ref, lse_ref = next(it), next(it)
        selo_ref, imp_ref = (next(it), next(it)) if is_cmp else (None, None)
        r = pl.program_id(1)
        for i in range(nq):
            lo, hi = _key_range(kind, i, window, Sk)
            rows = slice(i * BQ, (i + 1) * BQ)
            qb = q_ref[0, 0, rows, :].astype(_CDT)
            kb = k_ref[0, lo:hi, :].astype(_CDT)
            vb = v_ref[0, lo:hi, :].astype(_CDT)
            s, mask = _scores(kind, i, lo, hi, window, qb, kb, ccol_ref, crow_ref, sel_ref)
            m = jnp.max(s, axis=-1, keepdims=True)
            if has_sink:
                sk = sink_ref[0, 0, 0:1, 0:1]
                m = jnp.maximum(m, sk)
            e = jnp.exp(s - m)
            if is_cmp:
                e = jnp.where(mask, e, 0.0)
            l = jnp.sum(e, axis=-1, keepdims=True)
            if has_sink:
                l = l + jnp.exp(sk - m)
            if is_cmp:
                l = jnp.where(l > 0.0, l, 1.0)
            p = e * (1.0 / l)
            o_ref[0, 0, rows, :] = _dot(p.astype(_CDT), vb)
            lse_ref[0, 0, rows, :] = m + jnp.log(l)
            if is_cmp:
                part = _dot_nt(ov_ref[...].astype(_CDT), p.astype(_CDT))

                @pl.when(r == 0)
                def _():
                    imp_ref[:, rows] = part

                @pl.when(r != 0)
                def _():
                    imp_ref[:, rows] += part

        if is_cmp:
            @pl.when(r == R - 1)
            def _():
                shape = (LANES, S)
                j = lax.broadcasted_iota(jnp.int32, shape, 0)
                tb = jnp.right_shift(lax.broadcasted_iota(jnp.int32, shape, 1), 6)
                forced = (j == 0) | (j == tb) | (j == tb - 1)
                imp = jnp.where(j > tb, NEG, jnp.where(forced, NSA_BONUS, imp_ref[...]))
                imp = jnp.where(j >= n_slc, -3e38, imp)
                imp_ref[...] = imp
                cnt = jnp.zeros(shape, F32)
                for jp in range(n_slc):
                    row = imp_ref[jp:jp + 1, :]
                    ahead = (row > imp) | ((row == imp) & (jp < j))
                    cnt = cnt + ahead.astype(F32)
                selo_ref[0] = (cnt < float(min(NSA_TOPK, n_slc))).astype(F32)

    qspec = pl.BlockSpec((1, 1, S, DH), lambda g, r: (g, r, 0, 0))
    kspec = pl.BlockSpec((1, Sk, DH), lambda g, r: (g, 0, 0))
    ins, specs = [q, k, v], [qspec, kspec, kspec]
    if has_bias:
        ins += [bias[0], bias[1]]
        specs += [pl.BlockSpec((1, S, 1), lambda g, r: (g, 0, 0)), pl.BlockSpec((1, 1, S), lambda g, r: (g, 0, 0))]
    if has_sink:
        ins.append(sinks)
        specs.append(pl.BlockSpec((1, 1, 8, LANES), lambda g, r: (g, r, 0, 0)))
    if has_sel:
        ins.append(selT)
        specs.append(pl.BlockSpec((1, LANES, S), lambda g, r: (g, 0, 0)))
    if is_cmp:
        ins.append(ovT)
        specs.append(pl.BlockSpec((LANES, Sk), lambda g, r: (0, 0)))
    outs = [jax.ShapeDtypeStruct((G, R, S, DH), F32), jax.ShapeDtypeStruct((G, R, S, 1), F32)]
    ospecs = [qspec, pl.BlockSpec((1, 1, S, 1), lambda g, r: (g, r, 0, 0))]
    scratch = []
    if is_cmp:
        outs.append(jax.ShapeDtypeStruct((G, LANES, S), F32))
        ospecs.append(pl.BlockSpec((1, LANES, S), lambda g, r: (g, 0, 0)))
        scratch.append(pltpu.VMEM((LANES, S), F32))
    return pl.pallas_call(body, out_shape=tuple(outs), grid=(G, R), in_specs=specs, out_specs=tuple(ospecs),
                          scratch_shapes=scratch, compiler_params=_params(), name=name)(*ins)


def _attn_bwd(q, k, v, o, do, lse, *, kind, window=0, bias=None, sinks=None, selT=None, gate=None, name):
    G, R, S, _ = q.shape
    Sk = k.shape[1]
    nq = S // BQ
    has_bias, has_sink, has_sel, has_gate = bias is not None, sinks is not None, selT is not None, gate is not None

    def body(*refs):
        it = iter(refs)
        q_ref, k_ref, v_ref, o_ref, do_ref, lse_ref = (next(it) for _ in range(6))
        ccol_ref, crow_ref = (next(it), next(it)) if has_bias else (None, None)
        sink_ref = next(it) if has_sink else None
        sel_ref = next(it) if has_sel else None
        z_ref = next(it) if has_gate else None
        dq_ref, dk_ref, dv_ref = next(it), next(it), next(it)
        dc_ref, dccol_ref = (next(it), next(it)) if has_bias else (None, None)
        dsink_ref = next(it) if has_sink else None
        dz_ref = next(it) if has_gate else None
        r = pl.program_id(1)

        @pl.when(r == 0)
        def _():
            dk_ref[...] = jnp.zeros_like(dk_ref)
            dv_ref[...] = jnp.zeros_like(dv_ref)
            if has_bias:
                dc_ref[...] = jnp.zeros_like(dc_ref)

        dsink = jnp.zeros((1, 1), F32)
        for i in range(nq):
            lo, hi = _key_range(kind, i, window, Sk)
            rows = slice(i * BQ, (i + 1) * BQ)
            qb = q_ref[0, 0, rows, :].astype(_CDT)
            kb = k_ref[0, lo:hi, :].astype(_CDT)
            vb = v_ref[0, lo:hi, :].astype(_CDT)
            s, mask = _scores(kind, i, lo, hi, window, qb, kb, ccol_ref, crow_ref, sel_ref)
            lse_i = lse_ref[0, 0, rows, :]
            p = jnp.where(mask, jnp.exp(s - lse_i), 0.0)
            dob = do_ref[0, 0, rows, :]
            if has_gate:
                od = jnp.sum(o_ref[0, 0, rows, :] * dob, axis=-1, keepdims=True)
                sg = jax.nn.sigmoid(z_ref[0, 0, rows, :])
                dob = dob * sg
                dz_ref[0, 0, rows, :] = od * sg * (1.0 - sg)
            dob = dob.astype(_CDT)
            dp = _dot_nt(dob, vb)
            delta = jnp.sum(p * dp, axis=-1, keepdims=True)
            ds = p * (dp - delta)
            dsb = ds.astype(_CDT)
            dq_ref[0, 0, rows, :] = _dot(dsb, kb) * SCALE
            dk_ref[0, lo:hi, :] += _dot_tn(dsb, qb) * SCALE
            dv_ref[0, lo:hi, :] += _dot_tn(p.astype(_CDT), dob)
            if has_bias:
                dccol_ref[0, rows, :] = jnp.sum(ds, axis=-1, keepdims=True)
                dc_ref[0, :, lo:hi] -= jnp.sum(ds, axis=0, keepdims=True)
            if has_sink:
                sk = sink_ref[0, 0, 0:1, 0:1]
                dsink = dsink - jnp.sum(jnp.exp(sk - lse_i) * delta, axis=0, keepdims=True)
        if has_sink:
            dsink_ref[0, 0] = jnp.broadcast_to(dsink, (8, LANES))

    qspec = pl.BlockSpec((1, 1, S, DH), lambda g, r: (g, r, 0, 0))
    cspec = pl.BlockSpec((1, 1, S, 1), lambda g, r: (g, r, 0, 0))
    kspec = pl.BlockSpec((1, Sk, DH), lambda g, r: (g, 0, 0))
    ins, specs = [q, k, v, o, do, lse], [qspec, kspec, kspec, qspec, qspec, cspec]
    if has_bias:
        ins += [bias[0], bias[1]]
        specs += [pl.BlockSpec((1, S, 1), lambda g, r: (g, 0, 0)), pl.BlockSpec((1, 1, S), lambda g, r: (g, 0, 0))]
    if has_sink:
        ins.append(sinks)
        specs.append(pl.BlockSpec((1, 1, 8, LANES), lambda g, r: (g, r, 0, 0)))
    if has_sel:
        ins.append(selT)
        specs.append(pl.BlockSpec((1, LANES, S), lambda g, r: (g, 0, 0)))
    if has_gate:
        ins.append(gate)
        specs.append(cspec)
    names = ["dq", "dk", "dv"]
    outs = [jax.ShapeDtypeStruct((G, R, S, DH), F32), jax.ShapeDtypeStruct((G, Sk, DH), F32),
            jax.ShapeDtypeStruct((G, Sk, DH), F32)]
    ospecs = [qspec, kspec, kspec]
    if has_bias:
        assert R == 1
        names += ["dcrow", "dccol"]
        outs += [jax.ShapeDtypeStruct((G, 1, S), F32), jax.ShapeDtypeStruct((G, S, 1), F32)]
        ospecs += [pl.BlockSpec((1, 1, S), lambda g, r: (g, 0, 0)), pl.BlockSpec((1, S, 1), lambda g, r: (g, 0, 0))]
    if has_sink:
        names.append("dsink")
        outs.append(jax.ShapeDtypeStruct((G, R, 8, LANES), F32))
        ospecs.append(pl.BlockSpec((1, 1, 8, LANES), lambda g, r: (g, r, 0, 0)))
    if has_gate:
        names.append("dz")
        outs.append(jax.ShapeDtypeStruct((G, R, S, 1), F32))
        ospecs.append(cspec)
    res = pl.pallas_call(body, out_shape=tuple(outs), grid=(G, R), in_specs=specs, out_specs=tuple(ospecs),
                         compiler_params=_params(), name=name)(*ins)
    return dict(zip(names, res))


def _combine(o0, o1, o2, z, *, name):
    H, S, _ = o0.shape

    def body(o0_ref, o1_ref, o2_ref, z_ref, o_ref):
        acc = jax.nn.sigmoid(z_ref[0, 0]) * o0_ref[0]
        acc = acc + jax.nn.sigmoid(z_ref[1, 0]) * o1_ref[0]
        acc = acc + jax.nn.sigmoid(z_ref[2, 0]) * o2_ref[0]
        o_ref[0] = acc

    blk = pl.BlockSpec((1, S, DH), lambda h: (h, 0, 0))
    return pl.pallas_call(body, out_shape=jax.ShapeDtypeStruct((H, S, DH), F32), grid=(H,),
                          in_specs=[blk, blk, blk, pl.BlockSpec((3, 1, S, 1), lambda h: (0, h, 0, 0))], out_specs=blk,
                          compiler_params=_params(), name=name)(o0, o1, o2, z)


_GC = math.sqrt(2.0 / math.pi)


def _gelu(x):
    return 0.5 * x * (1.0 + jnp.tanh(_GC * (x + 0.044715 * x * x * x)))


def _gelu_grad(x):
    t = jnp.tanh(_GC * (x + 0.044715 * x * x * x))
    return 0.5 * (1.0 + t) + 0.5 * x * (1.0 - t * t) * _GC * (1.0 + 3.0 * 0.044715 * x * x)


def _make_xb(k):
    G, S, _ = k.shape
    chunks = k.reshape(G, S // 16, 16 * DH)
    shift = jnp.concatenate([chunks[:, 1:], jnp.zeros((G, 1, 16 * DH), k.dtype)], axis=1)
    return jnp.concatenate([chunks, shift], axis=-1)


def _unmake_xb(dxb, *, name):
    G, n, _ = dxb.shape
    a = dxb[..., :16 * DH]
    b = jnp.concatenate([jnp.zeros((G, 1, 16 * DH), F32), dxb[:, :-1, 16 * DH:]], axis=1)
    return _addn(a, b, name=name).reshape(G, n * 16, DH)


def _compress_fwd(xb, pe, w1, w2, *, name):
    G, n, W = xb.shape
    Hc = w1.shape[1]

    def body(xb_ref, pe_ref, w1_ref, w2_ref, kc_ref, hid_ref):
        xv = (xb_ref[0] + pe_ref[...]).astype(_CDT)
        hid = _dot(xv, w1_ref[...].astype(_CDT))
        hid_ref[0] = hid
        kc_ref[0] = _dot(_gelu(hid).astype(_CDT), w2_ref[...].astype(_CDT))

    return pl.pallas_call(
        body, out_shape=(jax.ShapeDtypeStruct((G, n, DH), F32), jax.ShapeDtypeStruct((G, n, Hc), F32)), grid=(G,),
        in_specs=[pl.BlockSpec((1, n, W), lambda g: (g, 0, 0)), pl.BlockSpec((1, W), lambda g: (0, 0)),
                  pl.BlockSpec((W, Hc), lambda g: (0, 0)), pl.BlockSpec((Hc, DH), lambda g: (0, 0))],
        out_specs=(pl.BlockSpec((1, n, DH), lambda g: (g, 0, 0)), pl.BlockSpec((1, n, Hc), lambda g: (g, 0, 0))),
        compiler_params=_params(), name=name)(xb, pe.reshape(1, W), w1, w2)


def _compress_bwd(xb, pe, w1, w2, hid, dkc, *, name):
    G, n, W = xb.shape
    Hc = w1.shape[1]

    def body(xb_ref, pe_ref, w1_ref, w2_ref, hid_ref, dkc_ref, dxb_ref, dw1_ref, dw2_ref, dpe_ref):
        @pl.when(pl.program_id(0) == 0)
        def _():
            dw1_ref[...] = jnp.zeros_like(dw1_ref)
            dw2_ref[...] = jnp.zeros_like(dw2_ref)
            dpe_ref[...] = jnp.zeros_like(dpe_ref)

        xv = (xb_ref[0] + pe_ref[...]).astype(_CDT)
        hid = hid_ref[0]
        dk = dkc_ref[0].astype(_CDT)
        dact = _dot_nt(dk, w2_ref[...].astype(_CDT))
        dhid = (dact * _gelu_grad(hid)).astype(_CDT)
        dw2_ref[...] += _dot_tn(_gelu(hid).astype(_CDT), dk)
        dxb = _dot_nt(dhid, w1_ref[...].astype(_CDT))
        dxb_ref[0] = dxb
        dw1_ref[...] += _dot_tn(xv, dhid)
        dpe_ref[...] += jnp.sum(dxb, axis=0, keepdims=True)

    return pl.pallas_call(
        body, out_shape=(jax.ShapeDtypeStruct((G, n, W), F32), jax.ShapeDtypeStruct((W, Hc), F32),
                         jax.ShapeDtypeStruct((Hc, DH), F32), jax.ShapeDtypeStruct((1, W), F32)), grid=(G,),
        in_specs=[pl.BlockSpec((1, n, W), lambda g: (g, 0, 0)), pl.BlockSpec((1, W), lambda g: (0, 0)),
                  pl.BlockSpec((W, Hc), lambda g: (0, 0)), pl.BlockSpec((Hc, DH), lambda g: (0, 0)),
                  pl.BlockSpec((1, n, Hc), lambda g: (g, 0, 0)), pl.BlockSpec((1, n, DH), lambda g: (g, 0, 0))],
        out_specs=(pl.BlockSpec((1, n, W), lambda g: (g, 0, 0)), pl.BlockSpec((W, Hc), lambda g: (0, 0)),
                   pl.BlockSpec((Hc, DH), lambda g: (0, 0)), pl.BlockSpec((1, W), lambda g: (0, 0))),
        compiler_params=_params(), name=name)(xb, pe.reshape(1, W), w1, w2, hid, dkc)


def _overlap_T(S):
    n_cmp, n_slc = S // 16 - 1, S // 64
    cs = np.arange(n_cmp) * 16
    ce = cs + 32
    ss = np.arange(n_slc) * 64
    se = ss + 64
    ov = np.clip(np.minimum(ce[:, None], se[None, :]) - np.maximum(cs[:, None], ss[None, :]), 0, None) / 32.0
    out = np.zeros((LANES, S // 16), np.float32)
    out[:n_slc, :n_cmp] = ov.T
    return jnp.asarray(out)


def _tri(n, upper):
    r = lax.broadcasted_iota(jnp.int32, (n, n), 0)
    c = lax.broadcasted_iota(jnp.int32, (n, n), 1)
    return ((c >= r) if upper else (c <= r)).astype(jnp.bfloat16)


def _fox_gate_fwd(zf, b, *, name):
    S, H = zf.shape
    nb = S // BQ

    def body(z_ref, b_ref, c_ref):
        tri = _tri(BQ, False)
        carry = jnp.zeros((1, H), F32)
        for i in range(nb):
            z = z_ref[i * BQ:(i + 1) * BQ, :] + b_ref[...]
            lf = jnp.minimum(z, 0.0) - jnp.log(1.0 + jnp.exp(-jnp.abs(z)))
            c_ref[i * BQ:(i + 1) * BQ, :] = _dot_exact_left(tri, lf) + carry
            carry = carry + jnp.sum(lf, axis=0, keepdims=True)

    return pl.pallas_call(body, out_shape=jax.ShapeDtypeStruct((S, H), F32), compiler_params=_params(),
                          name=name)(zf, b)


def _fox_gate_bwd(zf, b, dc, *, name):
    S, H = zf.shape
    nb = S // BQ

    def body(z_ref, b_ref, dc_ref, dz_ref, db_ref):
        tri = _tri(BQ, True)
        carry = jnp.zeros((1, H), F32)
        db = jnp.zeros((1, H), F32)
        for i in reversed(range(nb)):
            rows = slice(i * BQ, (i + 1) * BQ)
            dcb = dc_ref[rows, :]
            dlf = _dot_exact_left(tri, dcb) + carry
            carry = carry + jnp.sum(dcb, axis=0, keepdims=True)
            z = z_ref[rows, :] + b_ref[...]
            dz = dlf * jax.nn.sigmoid(-z)
            dz_ref[rows, :] = dz
            db = db + jnp.sum(dz, axis=0, keepdims=True)
        db_ref[...] = db

    return pl.pallas_call(body, out_shape=(jax.ShapeDtypeStruct((S, H), F32), jax.ShapeDtypeStruct((1, H), F32)),
                          compiler_params=_params(), name=name)(zf, b, dc)


def _to_heads(a):
    S = a.shape[0]
    return a.reshape(S, -1, DH).transpose(1, 0, 2)


def _from_heads(a):
    return a.transpose(1, 0, 2).reshape(a.shape[1], -1)


def _pad_lanes(a):
    return jnp.pad(a, ((0, 0), (0, LANES - a.shape[1])))


def _ffn_fwd(x, g, wgu, wd, tag):
    h = _rms_fwd(x, g, name=tag + "_rms")
    gu = _mm(h, wgu, tm=512, tn=1408, name=tag + "_gu")
    a = _swiglu_fwd(gu, name=tag + "_act")
    xo = _mm(a, wd, tm=512, tn=1024, res=x, alpha=0.5, name=tag + "_down")
    return xo, (x, h, gu, a)


def _ffn_bwd(dxo, saved, g, wgu, wd, tag):
    x, h, gu, a = saved
    da = _mm(dxo, wd, tb=True, tm=512, tn=1408, alpha=0.5, name=tag + "_dact")
    dwd = _mm(a, dxo, ta=True, tm=1408, tn=512, alpha=0.5, name=tag + "_dwd")
    dgu = _swiglu_bwd(gu, da, name=tag + "_dgu")
    dwgu = _mm(h, dgu, ta=True, tm=512, tn=512, name=tag + "_dwgu")
    dh = _mm(dgu, wgu, tb=True, tm=256, tn=512, name=tag + "_dh")
    dx, dg = _rms_bwd(dh, x, g, dxo, name=tag + "_drms")
    return dx, dg, dwgu, dwd


def _nsa_fwd(x, g, w, cc, ss, tag):
    S = x.shape[0]
    h = _rms_fwd(x, g, name=tag + "_rms")
    proj = _mm(h, w["w_in"], tm=512, tn=896, name=tag + "_in")
    q = _to_heads(proj[:, :1024])
    kv = proj[:, 1024:2560].reshape(S, 3, 2, NSA_G, DH)
    ks = [kv[:, b, 0].transpose(1, 0, 2) for b in range(3)]
    vs = [kv[:, b, 1].transpose(1, 0, 2) for b in range(3)]
    z = proj[:, 2560:NSA_IN].reshape(S, 3, N_HEADS).transpose(1, 2, 0)[..., None]
    roped = _rope(jnp.concatenate([q] + ks, axis=0), cc, ss, name=tag + "_rope")
    qr = roped[:N_HEADS].reshape(NSA_G, NSA_R, S, DH)
    kr = [roped[N_HEADS + 4 * b:N_HEADS + 4 * b + 4] for b in range(3)]
    xbk, xbv = _make_xb(kr[0]), _make_xb(vs[0])
    kc, hidk = _compress_fwd(xbk, w["ck_pe"], w["ck_w1"], w["ck_w2"], name=tag + "_ck")
    vc, hidv = _compress_fwd(xbv, w["cv_pe"], w["cv_w1"], w["cv_w2"], name=tag + "_cv")
    o0, lse0, selT = _attn_fwd(qr, kc, vc, kind="cmp", ovT=_overlap_T(S), name=tag + "_cmp")
    o1, lse1 = _attn_fwd(qr, kr[1], vs[1], kind="sel", selT=selT, name=tag + "_slc")
    o2, lse2 = _attn_fwd(qr, kr[2], vs[2], kind="band", window=NSA_WINDOW, name=tag + "_win")
    o = _combine(o0.reshape(N_HEADS, S, DH), o1.reshape(N_HEADS, S, DH), o2.reshape(N_HEADS, S, DH), z,
                 name=tag + "_mix")
    of = _from_heads(o)
    xo = _mm(of, w["w_out"], tm=512, tn=1024, res=x, name=tag + "_out")
    saved = (x, h, qr, kr, vs, z, xbk, xbv, hidk, hidv, kc, vc, (o0, o1, o2), (lse0, lse1, lse2), selT, of)
    return xo, saved


def _nsa_bwd(dxo, saved, g, w, cc, ss, tag):
    x, h, qr, kr, vs, z, xbk, xbv, hidk, hidv, kc, vc, os_, lses, selT, of = saved
    S = x.shape[0]
    dof = _mm(dxo, w["w_out"], tb=True, tm=512, tn=1024, name=tag + "_dof")
    dw_out = _mm(of, dxo, ta=True, tm=512, tn=1024, name=tag + "_dwout")
    do = _to_heads(dof).reshape(NSA_G, NSA_R, S, DH)
    zg = z.reshape(3, NSA_G, NSA_R, S, 1)
    b0 = _attn_bwd(qr, kc, vc, os_[0], do, lses[0], kind="cmp", gate=zg[0], name=tag + "_dcmp")
    b1 = _attn_bwd(qr, kr[1], vs[1], os_[1], do, lses[1], kind="sel", selT=selT, gate=zg[1], name=tag + "_dslc")
    b2 = _attn_bwd(qr, kr[2], vs[2], os_[2], do, lses[2], kind="band", window=NSA_WINDOW, gate=zg[2],
                   name=tag + "_dwin")
    dxbk, dck_w1, dck_w2, dck_pe = _compress_bwd(xbk, w["ck_pe"], w["ck_w1"], w["ck_w2"], hidk, b0["dk"],
                                                 name=tag + "_dck")
    dxbv, dcv_w1, dcv_w2, dcv_pe = _compress_bwd(xbv, w["cv_pe"], w["cv_w1"], w["cv_w2"], hidv, b0["dv"],
                                                 name=tag + "_dcv")
    dk0 = _unmake_xb(dxbk, name=tag + "_dk0")
    dv0 = _unmake_xb(dxbv, name=tag + "_dv0")
    dq = _addn(b0["dq"], b1["dq"], b2["dq"], name=tag + "_dqsum").reshape(N_HEADS, S, DH)
    unroped = _rope(jnp.concatenate([dq, dk0, b1["dk"], b2["dk"]], axis=0), cc, -ss, name=tag + "_drope")
    dks = [unroped[N_HEADS + 4 * b:N_HEADS + 4 * b + 4] for b in range(3)]
    dvs = [dv0, b1["dv"], b2["dv"]]
    dkv = jnp.stack([jnp.stack([dks[b], dvs[b]], axis=0) for b in range(3)], axis=0)
    dkv = dkv.transpose(3, 0, 1, 2, 4).reshape(S, 3 * 2 * NSA_G * DH)
    dz = jnp.stack([b0["dz"], b1["dz"], b2["dz"]], axis=0).reshape(3 * N_HEADS, S).T
    dproj = jnp.concatenate([_from_heads(unroped[:N_HEADS]), dkv, dz, jnp.zeros((S, NSA_IN_PAD - NSA_IN), F32)], axis=1)
    dh = _mm(dproj, w["w_in"], tb=True, tm=512, tn=512, name=tag + "_dh")
    dw_in = _mm(h, dproj, ta=True, tm=512, tn=896, name=tag + "_dwin_w")[:, :NSA_IN]
    dx, dg = _rms_bwd(dh, x, g, dxo, name=tag + "_drms")
    grads = dict(w_in=dw_in, w_out=dw_out, ck_pe=dck_pe.reshape(32, DH), ck_w1=dck_w1, ck_w2=dck_w2,
                 cv_pe=dcv_pe.reshape(32, DH), cv_w1=dcv_w1, cv_w2=dcv_w2)
    return dx, dg, grads


def _swa_fwd(x, g, w, cc, ss, tag):
    S = x.shape[0]
    h = _rms_fwd(x, g, name=tag + "_rms")
    proj = _mm(h, w["w_in"], tm=512, tn=640, name=tag + "_in")
    q = _to_heads(proj[:, :1024])
    kv = proj[:, 1024:].reshape(S, 2, SWA_G, DH)
    k, v = kv[:, 0].transpose(1, 0, 2), kv[:, 1].transpose(1, 0, 2)
    roped = _rope(jnp.concatenate([q, k], axis=0), cc, ss, name=tag + "_rope")
    qr = roped[:N_HEADS].reshape(SWA_G, SWA_R, S, DH)
    kr = roped[N_HEADS:]
    sinks = jnp.broadcast_to(w["sinks"].reshape(SWA_G, SWA_R, 1, 1), (SWA_G, SWA_R, 8, LANES))
    o, lse = _attn_fwd(qr, kr, v, kind="band", window=SWA_WINDOW, sinks=sinks, name=tag + "_attn")
    of = _from_heads(o.reshape(N_HEADS, S, DH))
    xo = _mm(of, w["w_out"], tm=512, tn=1024, res=x, name=tag + "_out")
    return xo, (x, h, qr, kr, v, sinks, o, lse, of)


def _swa_bwd(dxo, saved, g, w, cc, ss, tag):
    x, h, qr, kr, v, sinks, o, lse, of = saved
    S = x.shape[0]
    dof = _mm(dxo, w["w_out"], tb=True, tm=512, tn=1024, name=tag + "_dof")
    dw_out = _mm(of, dxo, ta=True, tm=512, tn=1024, name=tag + "_dwout")
    do = _to_heads(dof).reshape(SWA_G, SWA_R, S, DH)
    b = _attn_bwd(qr, kr, v, o, do, lse, kind="band", window=SWA_WINDOW, sinks=sinks, name=tag + "_dattn")
    unroped = _rope(jnp.concatenate([b["dq"].reshape(N_HEADS, S, DH), b["dk"]], axis=0), cc, -ss, name=tag + "_drope")
    dkv = jnp.stack([unroped[N_HEADS:], b["dv"]], axis=0).transpose(2, 0, 1, 3).reshape(S, 2 * SWA_G * DH)
    dproj = jnp.concatenate([_from_heads(unroped[:N_HEADS]), dkv], axis=1)
    dh = _mm(dproj, w["w_in"], tb=True, tm=512, tn=512, name=tag + "_dh")
    dw_in = _mm(h, dproj, ta=True, tm=512, tn=640, name=tag + "_dwin_w")
    dx, dg = _rms_bwd(dh, x, g, dxo, name=tag + "_drms")
    return dx, dg, dict(w_in=dw_in, w_out=dw_out, sinks=b["dsink"][:, :, 0, 0].reshape(N_HEADS))


def _fox_fwd(x, g, w, tag):
    S = x.shape[0]
    h = _rms_fwd(x, g, name=tag + "_rms")
    proj = _mm(h, w["w_in"], tm=512, tn=640, name=tag + "_in")
    q, k, v = (_to_heads(proj[:, i * 1024:(i + 1) * 1024]) for i in range(3))
    zf = _pad_lanes(proj[:, 3072:FOX_IN])
    bf = _pad_lanes(w["b_f"].reshape(1, N_HEADS))
    c = _fox_gate_fwd(zf, bf, name=tag + "_gate")[:, :N_HEADS]
    bias = (c.T[:, :, None], c.T[:, None, :])
    q4 = q.reshape(N_HEADS, 1, S, DH)
    o, lse = _attn_fwd(q4, k, v, kind="causal", bias=bias, name=tag + "_attn")
    of = _from_heads(o.reshape(N_HEADS, S, DH))
    xo = _mm(of, w["w_out"], tm=512, tn=1024, res=x, name=tag + "_out")
    return xo, (x, h, q4, k, v, zf, bf, bias, o, lse, of)


def _fox_bwd(dxo, saved, g, w, tag):
    x, h, q4, k, v, zf, bf, bias, o, lse, of = saved
    S = x.shape[0]
    dof = _mm(dxo, w["w_out"], tb=True, tm=512, tn=1024, name=tag + "_dof")
    dw_out = _mm(of, dxo, ta=True, tm=512, tn=1024, name=tag + "_dwout")
    do = _to_heads(dof).reshape(N_HEADS, 1, S, DH)
    b = _attn_bwd(q4, k, v, o, do, lse, kind="causal", bias=bias, name=tag + "_dattn")
    dc = _addn(b["dcrow"].reshape(N_HEADS, S), b["dccol"].reshape(N_HEADS, S), name=tag + "_dc")
    dzf, db = _fox_gate_bwd(zf, bf, _pad_lanes(dc.T), name=tag + "_dgate")
    dproj = jnp.concatenate([_from_heads(b["dq"].reshape(N_HEADS, S, DH)), _from_heads(b["dk"]), _from_heads(b["dv"]),
                             dzf[:, :N_HEADS], jnp.zeros((S, FOX_IN_PAD - FOX_IN), F32)], axis=1)
    dh = _mm(dproj, w["w_in"], tb=True, tm=512, tn=512, name=tag + "_dh")
    dw_in = _mm(h, dproj, ta=True, tm=512, tn=640, name=tag + "_dwin_w")[:, :FOX_IN]
    dx, dg = _rms_bwd(dh, x, g, dxo, name=tag + "_drms")
    return dx, dg, dict(w_in=dw_in, w_out=dw_out, b_f=db[0, :N_HEADS])


def _mixer_weights(W, i):
    kind, j = i % 3, i // 3
    if kind == 0:
        return {k: W["nsa_" + k][j] for k in ("w_in", "ck_pe", "ck_w1", "ck_w2", "cv_pe", "cv_w1", "cv_w2", "w_out")}
    if kind == 1:
        return {"w_in": W["swa_w_in"][j], "sinks": W["swa_sinks"][j], "w_out": W["swa_w_out"][j]}
    return {"w_in": W["fox_w_in"][j], "b_f": W["fox_b_f"][j], "w_out": W["fox_w_out"][j]}


def _local_step(x, tgt, W):
    S = x.shape[0]
    cc, ss = _rope_tables(S)
    saved = []
    for i in range(DEPTH):
        kind = i % 3
        x, s1 = _ffn_fwd(x, W["ffn1_norm"][i], W["ffn1_w_gu"][i], W["ffn1_w_down"][i], f"l{i}f1")
        mw = _mixer_weights(W, i)
        if kind == 0:
            x, s2 = _nsa_fwd(x, W["mix_norm"][i], mw, cc, ss, f"l{i}nsa")
        elif kind == 1:
            x, s2 = _swa_fwd(x, W["mix_norm"][i], mw, cc, ss, f"l{i}swa")
        else:
            x, s2 = _fox_fwd(x, W["mix_norm"][i], mw, f"l{i}fox")
        x, s3 = _ffn_fwd(x, W["ffn2_norm"][i], W["ffn2_w_gu"][i], W["ffn2_w_down"][i], f"l{i}f2")
        saved.append((s1, s2, s3))
    loss, dx, d_final = _loss_head(x, W["final_norm"], tgt, name="loss_head")

    per_layer = {k: [None] * DEPTH for k in ("ffn1_norm", "ffn1_w_gu", "ffn1_w_down", "mix_norm", "ffn2_norm",
                                              "ffn2_w_gu", "ffn2_w_down")}
    mix = {}
    for i in reversed(range(DEPTH)):
        kind, j = i % 3, i // 3
        s1, s2, s3 = saved[i]
        dx, dg, dwgu, dwd = _ffn_bwd(dx, s3, W["ffn2_norm"][i], W["ffn2_w_gu"][i], W["ffn2_w_down"][i], f"l{i}f2")
        per_layer["ffn2_norm"][i], per_layer["ffn2_w_gu"][i], per_layer["ffn2_w_down"][i] = dg, dwgu, dwd
        mw = _mixer_weights(W, i)
        if kind == 0:
            dx, dg, gm = _nsa_bwd(dx, s2, W["mix_norm"][i], mw, cc, ss, f"l{i}nsa")
            pre = "nsa_"
        elif kind == 1:
            dx, dg, gm = _swa_bwd(dx, s2, W["mix_norm"][i], mw, cc, ss, f"l{i}swa")
            pre = "swa_"
        else:
            dx, dg, gm = _fox_bwd(dx, s2, W["mix_norm"][i], mw, f"l{i}fox")
            pre = "fox_"
        per_layer["mix_norm"][i] = dg
        for k, val in gm.items():
            mix.setdefault(pre + k, {})[j] = val
        dx, dg, dwgu, dwd = _ffn_bwd(dx, s1, W["ffn1_norm"][i], W["ffn1_w_gu"][i], W["ffn1_w_down"][i], f"l{i}f1")
        per_layer["ffn1_norm"][i], per_layer["ffn1_w_gu"][i], per_layer["ffn1_w_down"][i] = dg, dwgu, dwd
    grads = {k: jnp.stack(v, axis=0) for k, v in per_layer.items()}
    for k, d in mix.items():
        grads[k] = jnp.stack([d[j] for j in sorted(d)], axis=0)
    grads["final_norm"] = d_final
    return loss, dx, grads


def _chip_peers():
    x, y, c = lax.axis_index("x"), lax.axis_index("y"), lax.axis_index("c")
    return x, y, c, [(1 - x, y), (x, 1 - y), (1 - x, 1 - y)]


def _gather_shards(flat):
    R, C = flat.shape

    def body(src, out, send_sems, recv_sems, local_sem):
        x, y, c, chips = _chip_peers()
        mine = pltpu.make_async_copy(src, out.at[2 * x + y], local_sem)
        mine.start()
        sends = [pltpu.make_async_remote_copy(src_ref=src, dst_ref=out.at[2 * x + y], send_sem=send_sems.at[j],
                                              recv_sem=recv_sems.at[j], device_id=(px, py, c), device_id_type=MESH)
                 for j, (px, py) in enumerate(chips)]
        for cp in sends:
            cp.start()
        for j, (px, py) in enumerate(chips):
            pltpu.make_async_remote_copy(src_ref=src, dst_ref=out.at[2 * px + py], send_sem=send_sems.at[j],
                                         recv_sem=recv_sems.at[j], device_id=(px, py, c),
                                         device_id_type=MESH).wait_recv()
        for cp in sends:
            cp.wait_send()
        mine.wait()

    return pl.pallas_call(
        body, out_shape=jax.ShapeDtypeStruct((4, R, C), flat.dtype),
        in_specs=[pl.BlockSpec(memory_space=pl.ANY)], out_specs=pl.BlockSpec(memory_space=pl.ANY),
        scratch_shapes=[pltpu.SemaphoreType.DMA((3,)), pltpu.SemaphoreType.DMA((3,)), pltpu.SemaphoreType.DMA],
        compiler_params=pltpu.CompilerParams(has_side_effects=True), name="gather_weights")(flat)


def _scatter_grads(send):
    def body(src, out, send_sems, recv_sems, local_sem):
        x, y, c, chips = _chip_peers()
        me = 2 * x + y
        mine = pltpu.make_async_copy(src.at[me], out.at[me], local_sem)
        mine.start()
        sends = [pltpu.make_async_remote_copy(src_ref=src.at[2 * px + py], dst_ref=out.at[me], send_sem=send_sems.at[j],
                                              recv_sem=recv_sems.at[j], device_id=(px, py, c), device_id_type=MESH)
                 for j, (px, py) in enumerate(chips)]
        for cp in sends:
            cp.start()
        for j, (px, py) in enumerate(chips):
            pltpu.make_async_remote_copy(src_ref=src.at[me], dst_ref=out.at[2 * px + py], send_sem=send_sems.at[j],
                                         recv_sem=recv_sems.at[j], device_id=(px, py, c),
                                         device_id_type=MESH).wait_recv()
        for cp in sends:
            cp.wait_send()
        mine.wait()

    return pl.pallas_call(
        body, out_shape=jax.ShapeDtypeStruct(send.shape, send.dtype),
        in_specs=[pl.BlockSpec(memory_space=pl.ANY)], out_specs=pl.BlockSpec(memory_space=pl.ANY),
        scratch_shapes=[pltpu.SemaphoreType.DMA((3,)), pltpu.SemaphoreType.DMA((3,)), pltpu.SemaphoreType.DMA],
        compiler_params=pltpu.CompilerParams(has_side_effects=True), name="scatter_grads")(send)


def _sum_slots(recv):
    _, R, C = recv.shape
    tr = _row_tile(R, C)

    def body(r_ref, o_ref):
        acc = r_ref[0].astype(F32) + r_ref[1].astype(F32)
        acc = acc + r_ref[2].astype(F32)
        o_ref[...] = acc + r_ref[3].astype(F32)

    return pl.pallas_call(body, out_shape=jax.ShapeDtypeStruct((R, C), F32), grid=(R // tr,),
                          in_specs=[pl.BlockSpec((4, tr, C), lambda i: (0, i, 0))],
                          out_specs=pl.BlockSpec((tr, C), lambda i: (i, 0)), compiler_params=_params(),
                          name="sum_chip_grads")(recv)


def _swap_sibling(part):
    def body(src, out, send_sem, recv_sem):
        x, y, c = lax.axis_index("x"), lax.axis_index("y"), lax.axis_index("c")
        cp = pltpu.make_async_remote_copy(src_ref=src, dst_ref=out, send_sem=send_sem, recv_sem=recv_sem,
                                          device_id=(x, y, 1 - c), device_id_type=MESH)
        cp.start()
        cp.wait()

    return pl.pallas_call(
        body, out_shape=jax.ShapeDtypeStruct(part.shape, part.dtype),
        in_specs=[pl.BlockSpec(memory_space=pl.ANY)], out_specs=pl.BlockSpec(memory_space=pl.ANY),
        scratch_shapes=[pltpu.SemaphoreType.DMA, pltpu.SemaphoreType.DMA],
        compiler_params=pltpu.CompilerParams(has_side_effects=True), name="swap_core_grads")(part)


def _flip(coord, bit):
    return 1 - coord if bit else coord


def _allreduce_small(v):
    n, C = v.shape

    def body(v_ref, o_ref, buf, send_sems, recv_sems):
        x, y, c = lax.axis_index("x"), lax.axis_index("y"), lax.axis_index("c")
        me = 4 * x + 2 * y + c
        buf[me] = v_ref[...]
        peers = [(_flip(x, (j >> 2) & 1), _flip(y, (j >> 1) & 1), _flip(c, j & 1)) for j in range(1, 8)]
        sends = [pltpu.make_async_remote_copy(src_ref=v_ref, dst_ref=buf.at[me], send_sem=send_sems.at[j],
                                              recv_sem=recv_sems.at[j], device_id=peer, device_id_type=MESH)
                 for j, peer in enumerate(peers)]
        for cp in sends:
            cp.start()
        for j, (px, py, pc) in enumerate(peers):
            pltpu.make_async_remote_copy(src_ref=v_ref, dst_ref=buf.at[4 * px + 2 * py + pc], send_sem=send_sems.at[j],
                                         recv_sem=recv_sems.at[j], device_id=(px, py, pc),
                                         device_id_type=MESH).wait_recv()
        for cp in sends:
            cp.wait_send()
        acc = buf[0]
        for d in range(1, 8):
            acc = acc + buf[d]
        o_ref[...] = acc

    return pl.pallas_call(
        body, out_shape=jax.ShapeDtypeStruct((n, C), F32),
        in_specs=[pl.BlockSpec(memory_space=pltpu.VMEM)], out_specs=pl.BlockSpec(memory_space=pltpu.VMEM),
        scratch_shapes=[pltpu.VMEM((8, n, C), F32), pltpu.SemaphoreType.DMA((7,)), pltpu.SemaphoreType.DMA((7,))],
        compiler_params=pltpu.CompilerParams(has_side_effects=True), name="allreduce_small")(v)


def _adamw(w, m, v, gs, *, name):
    shape = w.shape
    w2, m2, v2 = _rows2d(w), _rows2d(m), _rows2d(v)
    g2 = [_rows2d(g) for g in gs]
    R, C = w2.shape
    tr = _row_tile(R, C, budget=1024 * 1024)
    ng = len(g2)

    def body(*refs):
        w_ref, m_ref, v_ref = refs[:3]
        g = refs[3][...]
        for r in refs[4:3 + ng]:
            g = g + r[...]
        g_ref, d_ref, nm_ref, nv_ref = refs[3 + ng:]
        mn = B1 * m_ref[...] + (1.0 - B1) * g
        vn = B2 * v_ref[...] + (1.0 - B2) * (g * g)
        m_hat = mn / (1.0 - B1 ** STEP)
        v_hat = vn / (1.0 - B2 ** STEP)
        g_ref[...] = g
        d_ref[...] = -LR * (m_hat / (jnp.sqrt(v_hat) + EPS) + WD * w_ref[...])
        nm_ref[...] = mn
        nv_ref[...] = vn

    spec = pl.BlockSpec((tr, C), lambda i: (i, 0))
    outs = pl.pallas_call(body, out_shape=tuple(jax.ShapeDtypeStruct((R, C), F32) for _ in range(4)), grid=(R // tr,),
                          in_specs=[spec] * (3 + ng), out_specs=(spec,) * 4, compiler_params=_params(),
                          name=name)(w2, m2, v2, *g2)
    return tuple(o.reshape(shape) for o in outs)


def _pack_rows(parts):
    flat = jnp.concatenate([p.reshape(-1) for p in parts])
    n = flat.shape[0]
    rows = -(-n // PACK_COLS)
    rows = -(-rows // PACK_ROWS) * PACK_ROWS
    return jnp.pad(flat, (0, rows * PACK_COLS - n)).reshape(rows, PACK_COLS)


def _small_layout(shapes):
    offs, off = {}, 0
    for k in REPLICATED:
        n = int(np.prod(shapes[k]))
        offs[k] = (off, n)
        off += -(-n // LANES) * LANES
    return offs, off


def _pack_small(d, shapes):
    offs, total = _small_layout(shapes)
    parts = []
    for k in REPLICATED:
        n = offs[k][1]
        parts.append(jnp.pad(d[k].reshape(-1).astype(F32), (0, -(-n // LANES) * LANES - n)))
    rows = -(-(total // LANES) // 8) * 8
    return jnp.pad(jnp.concatenate(parts), (0, rows * LANES - total)).reshape(rows, LANES)


def _unpack_small(a, shapes):
    offs, _ = _small_layout(shapes)
    flat = a.reshape(-1)
    return {k: flat[offs[k][0]:offs[k][0] + offs[k][1]].reshape(shapes[k]) for k in REPLICATED}


def kernel(x, ffn1_norm, ffn1_w_gu, ffn1_w_down, mix_norm, ffn2_norm, ffn2_w_gu, ffn2_w_down, nsa_w_in, nsa_ck_pe, nsa_ck_w1, nsa_ck_w2, nsa_cv_pe, nsa_cv_w1, nsa_cv_w2, nsa_w_out, swa_w_in, swa_sinks, swa_w_out, fox_w_in, fox_b_f, fox_w_out, final_norm, loss_target, m_ffn1_norm, m_ffn1_w_gu, m_ffn1_w_down, m_mix_norm, m_ffn2_norm, m_ffn2_w_gu, m_ffn2_w_down, m_nsa_w_in, m_nsa_ck_pe, m_nsa_ck_w1, m_nsa_ck_w2, m_nsa_cv_pe, m_nsa_cv_w1, m_nsa_cv_w2, m_nsa_w_out, m_swa_w_in, m_swa_sinks, m_swa_w_out, m_fox_w_in, m_fox_b_f, m_fox_w_out, m_final_norm, v_ffn1_norm, v_ffn1_w_gu, v_ffn1_w_down, v_mix_norm, v_ffn2_norm, v_ffn2_w_gu, v_ffn2_w_down, v_nsa_w_in, v_nsa_ck_pe, v_nsa_ck_w1, v_nsa_ck_w2, v_nsa_cv_pe, v_nsa_cv_w1, v_nsa_cv_w2, v_nsa_w_out, v_swa_w_in, v_swa_sinks, v_swa_w_out, v_fox_w_in, v_fox_b_f, v_fox_w_out, v_final_norm):
    args = dict(locals())
    w = {k: args[k] for k in WEIGHTS}
    m = {k: args["m_" + k] for k in WEIGHTS}
    v = {k: args["v_" + k] for k in WEIGHTS}

    shard_shapes = {k: w[k].shape for k, _ in SHARDED}
    flat = _pack_rows([w[k].astype(jnp.bfloat16) for k, _ in SHARDED])
    gathered = _gather_shards(flat).reshape(4, -1)
    W, off = {}, 0
    for k, axis in SHARDED:
        n = int(np.prod(shard_shapes[k]))
        W[k] = jnp.concatenate([gathered[s, off:off + n].reshape(shard_shapes[k]) for s in range(4)], axis=axis)
        off += n
    W["nsa_w_in"] = jnp.pad(W["nsa_w_in"], ((0, 0), (0, 0), (0, NSA_IN_PAD - NSA_IN)))
    W["fox_w_in"] = jnp.pad(W["fox_w_in"], ((0, 0), (0, 0), (0, FOX_IN_PAD - FOX_IN)))
    for k in REPLICATED:
        W[k] = w[k]

    loss_part, dx, grads = _local_step(x[0], loss_target[0], W)
    loss = lax.psum(loss_part, ("x", "y", "c"))

    send = jnp.stack([
        _pack_rows([lax.slice_in_dim(grads[k], s * shard_shapes[k][axis], (s + 1) * shard_shapes[k][axis], axis=axis)
                    .astype(jnp.bfloat16) for k, axis in SHARDED]) for s in range(4)], axis=0)
    part = _sum_slots(_scatter_grads(send))
    other = _swap_sibling(part)
    small_shapes = {k: w[k].shape for k in REPLICATED}
    g_small = _unpack_small(_allreduce_small(_pack_small(grads, small_shapes)), small_shapes)

    out_g, out_d, out_m, out_v = {}, {}, {}, {}
    part_flat, other_flat, off = part.reshape(-1), other.reshape(-1), 0
    for k, _ in SHARDED:
        n = int(np.prod(shard_shapes[k]))
        gs = [part_flat[off:off + n].reshape(shard_shapes[k]), other_flat[off:off + n].reshape(shard_shapes[k])]
        out_g[k], out_d[k], out_m[k], out_v[k] = _adamw(w[k], m[k], v[k], gs, name="adamw_" + k)
        off += n
    sm = _adamw(_pack_small(w, small_shapes), _pack_small(m, small_shapes), _pack_small(v, small_shapes),
                [_pack_small(g_small, small_shapes)], name="adamw_small")
    for d, packed in zip((out_g, out_d, out_m, out_v), sm):
        d.update(_unpack_small(packed, small_shapes))
    return (loss, dx[None], *[out_g[k] for k in WEIGHTS], *[out_d[k] for k in WEIGHTS],
            *[out_m[k] for k in WEIGHTS], *[out_v[k] for k in WEIGHTS])
```

```python
import math

import numpy as np
import jax
import jax.numpy as jnp
from jax import lax
from jax.experimental import pallas as pl
from jax.experimental.pallas import tpu as pltpu

F32 = jnp.float32
_CDT = jnp.bfloat16
_WIRE = jnp.bfloat16
_VMEM_LIMIT = 56 * 1024 * 1024

D_MODEL = 1024
DEPTH = 4
DH = 64
N_HEADS = 16
RMS_EPS = 1e-6
NEG = -1e30
SCALE = DH ** -0.5
BQ = 256
LANES = 128
NSA_G, NSA_R = 4, 4
NSA_WINDOW = 512
NSA_TOPK = 16
NSA_BONUS = 1e4
SWA_G, SWA_R = 2, 8
SWA_WINDOW = 128
NSA_IN, NSA_IN_PAD = 2608, 2688
SWA_IN = 1280
FOX_IN, FOX_IN_PAD = 3088, 3200
LR, B1, B2, EPS, WD, STEP = 0.001, 0.9, 0.999, 1e-08, 0.01, 10
MESH = pl.DeviceIdType.MESH

REPLICATED = ["ffn1_norm", "mix_norm", "ffn2_norm", "nsa_ck_pe", "nsa_ck_w2", "nsa_cv_pe", "nsa_cv_w2",
              "swa_sinks", "fox_b_f", "final_norm"]
WEIGHTS = ['ffn1_norm', 'ffn1_w_gu', 'ffn1_w_down', 'mix_norm', 'ffn2_norm', 'ffn2_w_gu', 'ffn2_w_down',
           'nsa_w_in', 'nsa_ck_pe', 'nsa_ck_w1', 'nsa_ck_w2', 'nsa_cv_pe', 'nsa_cv_w1', 'nsa_cv_w2', 'nsa_w_out',
           'swa_w_in', 'swa_sinks', 'swa_w_out', 'fox_w_in', 'fox_b_f', 'fox_w_out', 'final_norm']


def _params(**kw):
    return pltpu.CompilerParams(vmem_limit_bytes=_VMEM_LIMIT, **kw)


def _dot(a, b):
    return lax.dot_general(a, b, (((1,), (0,)), ((), ())), preferred_element_type=F32)


def _dot_nt(a, b):
    return lax.dot_general(a, b, (((1,), (1,)), ((), ())), preferred_element_type=F32)


def _dot_tn(a, b):
    return lax.dot_general(a, b, (((0,), (0,)), ((), ())), preferred_element_type=F32)


def _split3(x):
    hi = x.astype(jnp.bfloat16)
    r1 = x - hi.astype(F32)
    mid = r1.astype(jnp.bfloat16)
    lo = (r1 - mid.astype(F32)).astype(jnp.bfloat16)
    return hi, mid, lo


def _dot_exact(x, p):
    hi, mid, lo = _split3(x)
    return _dot(hi, p) + _dot(mid, p) + _dot(lo, p)


def _dot_exact_left(p, x):
    hi, mid, lo = _split3(x)
    return _dot(p, hi) + _dot(p, mid) + _dot(p, lo)


def _mm(a, b, *, ta=False, tb=False, tm, tn, b_lead=(), into=None, o_lead=(), out_dtype=F32, res=None, alpha=1.0,
        name):
    M = a.shape[1] if ta else a.shape[0]
    K = a.shape[0] if ta else a.shape[1]
    bk, bn = (b.shape[-1], b.shape[-2]) if tb else (b.shape[-2], b.shape[-1])
    j_lead, k_lead = "j" in b_lead, "k" in b_lead
    N = bn * (b.shape[b_lead.index("j")] if j_lead else 1)
    nk = b.shape[b_lead.index("k")] if k_lead else 1
    tk = K // nk
    assert tk == bk, (name, K, nk, bk)
    tm = min(tm, M)
    tn = bn if j_lead else min(tn, N)
    assert M % tm == 0 and N % tn == 0, (name, M, N, tm, tn)
    nb, no = len(b_lead), len(o_lead)

    def pick(lead, j, k):
        return tuple(j if t == "j" else k if t == "k" else t for t in lead)

    a_spec = pl.BlockSpec((tk, tm), lambda i, j, k: (k, i)) if ta else pl.BlockSpec((tm, tk), lambda i, j, k: (i, k))
    if tb:
        b_spec = pl.BlockSpec((None,) * nb + (tn, tk),
                              lambda i, j, k: pick(b_lead, j, k) + (0 if j_lead else j, 0 if k_lead else k))
    else:
        b_spec = pl.BlockSpec((None,) * nb + (tk, tn),
                              lambda i, j, k: pick(b_lead, j, k) + (0 if k_lead else k, 0 if j_lead else j))
    r_spec = pl.BlockSpec((tm, tn), lambda i, j, k: (i, j))
    o_spec = pl.BlockSpec((None,) * no + (tm, tn), lambda i, j, k: pick(o_lead, j, k) + (i, 0 if "j" in o_lead else j))
    dn = (((0 if ta else 1,), (1 if tb else 0,)), ((), ()))
    has_res, has_into = res is not None, into is not None
    if has_into:
        out_dtype = into.dtype

    def body(*refs):
        a_ref, b_ref = refs[0], refs[1]
        r_ref = refs[2] if has_res else None
        o_ref = refs[2 + has_res + has_into]
        prod = lax.dot_general(a_ref[...].astype(_CDT), b_ref[...].astype(_CDT), dn, preferred_element_type=F32)

        def finish(acc):
            if alpha != 1.0:
                acc = acc * alpha
            if has_res:
                acc = r_ref[...] + acc
            o_ref[...] = acc.astype(out_dtype)

        if nk == 1:
            finish(prod)
        else:
            acc_ref = refs[-1]
            k = pl.program_id(2)

            @pl.when(k == 0)
            def _():
                acc_ref[...] = prod

            @pl.when(k != 0)
            def _():
                acc_ref[...] += prod

            @pl.when(k == nk - 1)
            def _():
                finish(acc_ref[...])

    ins, specs = [a, b], [a_spec, b_spec]
    if has_res:
        ins.append(res)
        specs.append(r_spec)
    aliases = {}
    if has_into:
        aliases = {len(ins): 0}
        ins.append(into)
        specs.append(pl.BlockSpec(memory_space=pl.ANY))
        out_shape = jax.ShapeDtypeStruct(into.shape, into.dtype)
    else:
        out_shape = jax.ShapeDtypeStruct((M, N), out_dtype)
    scratch = [pltpu.VMEM((tm, tn), F32)] if nk > 1 else []
    return pl.pallas_call(body, out_shape=out_shape, grid=(M // tm, N // tn, nk), in_specs=specs, out_specs=o_spec,
                          scratch_shapes=scratch, input_output_aliases=aliases, compiler_params=_params(),
                          name=name)(*ins)


def _rows2d(a):
    return a.reshape(-1, a.shape[-1])


def _row_tile(rows, cols, itemsize=4, budget=2 * 1024 * 1024):
    t = rows
    while t % 2 == 0 and t * cols * itemsize > budget and (t // 2) % 8 == 0:
        t //= 2
    return t


def _addn(*xs, name):
    shape = xs[0].shape
    x2 = [_rows2d(x) for x in xs]
    R, C = x2[0].shape
    tr = _row_tile(R, C)

    def body(*refs):
        acc = refs[0][...]
        for r in refs[1:-1]:
            acc = acc + r[...]
        refs[-1][...] = acc

    spec = pl.BlockSpec((tr, C), lambda i: (i, 0))
    out = pl.pallas_call(body, out_shape=jax.ShapeDtypeStruct((R, C), F32), grid=(R // tr,),
                         in_specs=[spec] * len(x2), out_specs=spec, compiler_params=_params(), name=name)(*x2)
    return out.reshape(shape)


def _rms_fwd(x, g, *, name):
    S, D = x.shape
    tr = 256

    def body(x_ref, g_ref, h_ref):
        xv = x_ref[...]
        rstd = lax.rsqrt(jnp.mean(xv * xv, axis=-1, keepdims=True) + RMS_EPS)
        h_ref[...] = (xv * rstd * g_ref[...]).astype(_CDT)

    return pl.pallas_call(body, out_shape=jax.ShapeDtypeStruct((S, D), _CDT), grid=(S // tr,),
                          in_specs=[pl.BlockSpec((tr, D), lambda i: (i, 0)), pl.BlockSpec((1, D), lambda i: (0, 0))],
                          out_specs=pl.BlockSpec((tr, D), lambda i: (i, 0)), compiler_params=_params(),
                          name=name)(x, g.reshape(1, D))


def _rms_bwd(dh, x, g, dres, *, name):
    S, D = x.shape
    tr = 256

    def body(dh_ref, x_ref, g_ref, dres_ref, dx_ref, dg_ref):
        xv = x_ref[...]
        rstd = lax.rsqrt(jnp.mean(xv * xv, axis=-1, keepdims=True) + RMS_EPS)
        xhat = xv * rstd
        dhv = dh_ref[...]
        dxhat = dhv * g_ref[...]
        dx_ref[...] = dres_ref[...] + rstd * (dxhat - xhat * jnp.mean(dxhat * xhat, axis=-1, keepdims=True))

        @pl.when(pl.program_id(0) == 0)
        def _():
            dg_ref[...] = jnp.zeros_like(dg_ref)

        dg_ref[...] += jnp.sum(dhv * xhat, axis=0, keepdims=True)

    row = pl.BlockSpec((tr, D), lambda i: (i, 0))
    vec = pl.BlockSpec((1, D), lambda i: (0, 0))
    dx, dg = pl.pallas_call(body, out_shape=(jax.ShapeDtypeStruct((S, D), F32), jax.ShapeDtypeStruct((1, D), F32)),
                            grid=(S // tr,), in_specs=[row, row, vec, row], out_specs=(row, vec),
                            compiler_params=_params(), name=name)(dh, x, g.reshape(1, D), dres)
    return dx, dg.reshape(D)


def _swiglu_fwd(gu, *, name):
    S, W = gu.shape
    H = W // 2
    tr = 256

    def body(gu_ref, a_ref):
        g = gu_ref[:, :H]
        u = gu_ref[:, H:]
        a_ref[...] = (g * jax.nn.sigmoid(g) * u).astype(_CDT)

    return pl.pallas_call(body, out_shape=jax.ShapeDtypeStruct((S, H), _CDT), grid=(S // tr,),
                          in_specs=[pl.BlockSpec((tr, W), lambda i: (i, 0))],
                          out_specs=pl.BlockSpec((tr, H), lambda i: (i, 0)), compiler_params=_params(), name=name)(gu)


def _swiglu_bwd(gu, da, *, name):
    S, W = gu.shape
    H = W // 2
    tr = 256

    def body(gu_ref, da_ref, d_ref):
        g = gu_ref[:, :H]
        u = gu_ref[:, H:]
        dav = da_ref[...]
        sg = jax.nn.sigmoid(g)
        silu = g * sg
        d_ref[:, :H] = dav * u * (sg + silu * (1.0 - sg))
        d_ref[:, H:] = dav * silu

    return pl.pallas_call(body, out_shape=jax.ShapeDtypeStruct((S, W), F32), grid=(S // tr,),
                          in_specs=[pl.BlockSpec((tr, W), lambda i: (i, 0)), pl.BlockSpec((tr, H), lambda i: (i, 0))],
                          out_specs=pl.BlockSpec((tr, W), lambda i: (i, 0)), compiler_params=_params(),
                          name=name)(gu, da)


def _loss_head(x, g, tgt, *, name):
    S, D = x.shape
    tr = 256

    def body(x_ref, g_ref, t_ref, loss_ref, dx_ref, dg_ref):
        xv = x_ref[...]
        rstd = lax.rsqrt(jnp.mean(xv * xv, axis=-1, keepdims=True) + RMS_EPS)
        xhat = xv * rstd
        err = xhat * g_ref[...] - t_ref[...]
        part = 0.5 * jnp.sum(jnp.mean(err * err, axis=-1, keepdims=True), axis=0, keepdims=True)
        dy = err * (1.0 / D)
        dxhat = dy * g_ref[...]
        dx_ref[...] = rstd * (dxhat - xhat * jnp.mean(dxhat * xhat, axis=-1, keepdims=True))

        @pl.when(pl.program_id(0) == 0)
        def _():
            dg_ref[...] = jnp.zeros_like(dg_ref)
            loss_ref[...] = jnp.zeros_like(loss_ref)

        dg_ref[...] += jnp.sum(dy * xhat, axis=0, keepdims=True)
        loss_ref[...] += jnp.broadcast_to(part, loss_ref.shape)

    row = pl.BlockSpec((tr, D), lambda i: (i, 0))
    vec = pl.BlockSpec((1, D), lambda i: (0, 0))
    loss, dx, dg = pl.pallas_call(
        body, out_shape=(jax.ShapeDtypeStruct((8, LANES), F32), jax.ShapeDtypeStruct((S, D), F32),
                         jax.ShapeDtypeStruct((1, D), F32)),
        grid=(S // tr,), in_specs=[row, vec, row], out_specs=(pl.BlockSpec((8, LANES), lambda i: (0, 0)), row, vec),
        compiler_params=_params(), name=name)(x, g.reshape(1, D), tgt)
    return loss[0, 0], dx, dg.reshape(D)


def _rope_tables(S):
    inv = 10000.0 ** (-jnp.arange(0, DH, 2, dtype=F32) / DH)
    ang = jnp.arange(S, dtype=F32)[:, None] * inv[None, :]
    cos, sin = jnp.cos(ang), jnp.sin(ang)
    return jnp.concatenate([cos, cos], -1), jnp.concatenate([-sin, sin], -1)


def _swap_matrix():
    p = np.zeros((DH, DH), np.float32)
    for j in range(DH // 2):
        p[j + DH // 2, j] = 1.0
        p[j, j + DH // 2] = 1.0
    return jnp.asarray(p, jnp.bfloat16)


def _rope(x, cc, ss, *, name):
    n, S, _ = x.shape

    def body(x_ref, c_ref, s_ref, p_ref, o_ref):
        xv = x_ref[0]
        o_ref[0] = xv * c_ref[...] + _dot_exact(xv, p_ref[...]) * s_ref[...]

    tab = pl.BlockSpec((S, DH), lambda i: (0, 0))
    blk = pl.BlockSpec((1, S, DH), lambda i: (i, 0, 0))
    return pl.pallas_call(body, out_shape=jax.ShapeDtypeStruct(x.shape, F32), grid=(n,),
                          in_specs=[blk, tab, tab, pl.BlockSpec((DH, DH), lambda i: (0, 0))], out_specs=blk,
                          compiler_params=_params(), name=name)(x, cc, ss, _swap_matrix())


def _key_range(kind, i, window, Sk):
    if kind == "cmp":
        return 0, Sk
    hi = (i + 1) * BQ
    if kind == "band":
        return max(0, i * BQ - window), hi
    return 0, hi


def _attn_mask(kind, i, lo, hi, window):
    shape = (BQ, hi - lo)
    qpos = i * BQ + lax.broadcasted_iota(jnp.int32, shape, 0)
    kpos = lo + lax.broadcasted_iota(jnp.int32, shape, 1)
    if kind == "cmp":
        return kpos * 16 + 31 <= qpos
    mask = kpos <= qpos
    if kind == "band":
        mask = mask & (qpos - kpos < window)
    return mask


def _sel_expand(lo, hi):
    shape = (LANES, hi - lo)
    j = lax.broadcasted_iota(jnp.int32, shape, 0)
    key = lo + lax.broadcasted_iota(jnp.int32, shape, 1)
    return (jnp.right_shift(key, 6) == j).astype(_CDT)


def _scores(kind, i, lo, hi, window, qb, kb, ccol_ref, crow_ref, sel_ref):
    s = _dot_nt(qb, kb) * SCALE
    if ccol_ref is not None:
        s = s + ccol_ref[0, i * BQ:(i + 1) * BQ, :] - crow_ref[0, :, lo:hi]
    mask = _attn_mask(kind, i, lo, hi, window)
    if sel_ref is not None:
        chosen = _dot_tn(sel_ref[0, :, i * BQ:(i + 1) * BQ].astype(_CDT), _sel_expand(lo, hi))
        mask = mask & (chosen > 0.5)
    return jnp.where(mask, s, NEG), mask


def _attn_fwd(q, k, v, *, kind, window=0, bias=None, sinks=None, selT=None, ovT=None, name):
    G, R, S, _ = q.shape
    Sk = k.shape[1]
    nq = S // BQ
    n_slc = S // 64
    has_bias, has_sink, has_sel, is_cmp = bias is not None, sinks is not None, selT is not None, kind == "cmp"

    def body(*refs):
        it = iter(refs)
        q_ref, k_ref, v_ref = next(it), next(it), next(it)
        ccol_ref, crow_ref = (next(it), next(it)) if has_bias else (None, None)
        sink_ref = next(it) if has_sink else None
        sel_ref = next(it) if has_sel else None
        ov_ref = next(it) if is_cmp else None
        o_ref, lse_ref = next(it), next(it)
        selo_ref, imp_ref = (next(it), next(it)) if is_cmp else (None, None)
        r = pl.program_id(1)
        for i in range(nq):
            lo, hi = _key_range(kind, i, window, Sk)
            rows = slice(i * BQ, (i + 1) * BQ)
            qb = q_ref[0, 0, rows, :].astype(_CDT)
            kb = k_ref[0, lo:hi, :].astype(_CDT)
            vb = v_ref[0, lo:hi, :].astype(_CDT)
            s, mask = _scores(kind, i, lo, hi, window, qb, kb, ccol_ref, crow_ref, sel_ref)
            m = jnp.max(s, axis=-1, keepdims=True)
            if has_sink:
                sk = sink_ref[0, 0, 0:1, 0:1]
                m = jnp.maximum(m, sk)
            e = jnp.exp(s - m)
            if is_cmp:
                e = jnp.where(mask, e, 0.0)
            l = jnp.sum(e, axis=-1, keepdims=True)
            if has_sink:
                l = l + jnp.exp(sk - m)
            if is_cmp:
                l = jnp.where(l > 0.0, l, 1.0)
            p = e * (1.0 / l)
            o_ref[0, 0, rows, :] = _dot(p.astype(_CDT), vb)
            lse_ref[0, 0, rows, :] = m + jnp.log(l)
            if is_cmp:
                part = _dot_nt(ov_ref[...].astype(_CDT), p.astype(_CDT))

                @pl.when(r == 0)
                def _():
                    imp_ref[:, rows] = part

                @pl.when(r != 0)
                def _():
                    imp_ref[:, rows] += part

        if is_cmp:
            @pl.when(r == R - 1)
            def _():
                shape = (LANES, S)
                j = lax.broadcasted_iota(jnp.int32, shape, 0)
                tb = jnp.right_shift(lax.broadcasted_iota(jnp.int32, shape, 1), 6)
                forced = (j == 0) | (j == tb) | (j == tb - 1)
                imp = jnp.where(j > tb, NEG, jnp.where(forced, NSA_BONUS, imp_ref[...]))
                imp = jnp.where(j >= n_slc, -3e38, imp)
                imp_ref[...] = imp
                cnt = jnp.zeros(shape, F32)
                for jp in range(n_slc):
                    row = imp_ref[jp:jp + 1, :]
                    ahead = (row > imp) | ((row == imp) & (jp < j))
                    cnt = cnt + ahead.astype(F32)
                selo_ref[0] = (cnt < float(min(NSA_TOPK, n_slc))).astype(F32)

    qspec = pl.BlockSpec((1, 1, S, DH), lambda g, r: (g, r, 0, 0))
    kspec = pl.BlockSpec((1, Sk, DH), lambda g, r: (g, 0, 0))
    ins, specs = [q, k, v], [qspec, kspec, kspec]
    if has_bias:
        ins += [bias[0], bias[1]]
        specs += [pl.BlockSpec((1, S, 1), lambda g, r: (g, 0, 0)), pl.BlockSpec((1, 1, S), lambda g, r: (g, 0, 0))]
    if has_sink:
        ins.append(sinks)
        specs.append(pl.BlockSpec((1, 1, 8, LANES), lambda g, r: (g, r, 0, 0)))
    if has_sel:
        ins.append(selT)
        specs.append(pl.BlockSpec((1, LANES, S), lambda g, r: (g, 0, 0)))
    if is_cmp:
        ins.append(ovT)
        specs.append(pl.BlockSpec((LANES, Sk), lambda g, r: (0, 0)))
    outs = [jax.ShapeDtypeStruct((G, R, S, DH), F32), jax.ShapeDtypeStruct((G, R, S, 1), F32)]
    ospecs = [qspec, pl.BlockSpec((1, 1, S, 1), lambda g, r: (g, r, 0, 0))]
    scratch = []
    if is_cmp:
        outs.append(jax.ShapeDtypeStruct((G, LANES, S), F32))
        ospecs.append(pl.BlockSpec((1, LANES, S), lambda g, r: (g, 0, 0)))
        scratch.append(pltpu.VMEM((LANES, S), F32))
    return pl.pallas_call(body, out_shape=tuple(outs), grid=(G, R), in_specs=specs, out_specs=tuple(ospecs),
                          scratch_shapes=scratch, compiler_params=_params(), name=name)(*ins)


def _attn_bwd(q, k, v, o, do, lse, *, kind, window=0, bias=None, sinks=None, selT=None, gate=None, name):
    G, R, S, _ = q.shape
    Sk = k.shape[1]
    nq = S // BQ
    has_bias, has_sink, has_sel, has_gate = bias is not None, sinks is not None, selT is not None, gate is not None

    def body(*refs):
        it = iter(refs)
        q_ref, k_ref, v_ref, o_ref, do_ref, lse_ref = (next(it) for _ in range(6))
        ccol_ref, crow_ref = (next(it), next(it)) if has_bias else (None, None)
        sink_ref = next(it) if has_sink else None
        sel_ref = next(it) if has_sel else None
        z_ref = next(it) if has_gate else None
        dq_ref, dk_ref, dv_ref = next(it), next(it), next(it)
        dc_ref, dccol_ref = (next(it), next(it)) if has_bias else (None, None)
        dsink_ref = next(it) if has_sink else None
        dz_ref = next(it) if has_gate else None
        r = pl.program_id(1)

        @pl.when(r == 0)
        def _():
            dk_ref[...] = jnp.zeros_like(dk_ref)
            dv_ref[...] = jnp.zeros_like(dv_ref)
            if has_bias:
                dc_ref[...] = jnp.zeros_like(dc_ref)

        dsink = jnp.zeros((1, 1), F32)
        for i in range(nq):
            lo, hi = _key_range(kind, i, window, Sk)
            rows = slice(i * BQ, (i + 1) * BQ)
            qb = q_ref[0, 0, rows, :].astype(_CDT)
            kb = k_ref[0, lo:hi, :].astype(_CDT)
            vb = v_ref[0, lo:hi, :].astype(_CDT)
            s, mask = _scores(kind, i, lo, hi, window, qb, kb, ccol_ref, crow_ref, sel_ref)
            lse_i = lse_ref[0, 0, rows, :]
            p = jnp.where(mask, jnp.exp(s - lse_i), 0.0)
            dob = do_ref[0, 0, rows, :]
            if has_gate:
                od = jnp.sum(o_ref[0, 0, rows, :] * dob, axis=-1, keepdims=True)
                sg = jax.nn.sigmoid(z_ref[0, 0, rows, :])
                dob = dob * sg
                dz_ref[0, 0, rows, :] = od * sg * (1.0 - sg)
            dob = dob.astype(_CDT)
            dp = _dot_nt(dob, vb)
            delta = jnp.sum(p * dp, axis=-1, keepdims=True)
            ds = p * (dp - delta)
            dsb = ds.astype(_CDT)
            dq_ref[0, 0, rows, :] = _dot(dsb, kb) * SCALE
            dk_ref[0, lo:hi, :] += _dot_tn(dsb, qb) * SCALE
            dv_ref[0, lo:hi, :] += _dot_tn(p.astype(_CDT), dob)
            if has_bias:
                dccol_ref[0, rows, :] = jnp.sum(ds, axis=-1, keepdims=True)
                dc_ref[0, :, lo:hi] -= jnp.sum(ds, axis=0, keepdims=True)
            if has_sink:
                sk = sink_ref[0, 0, 0:1, 0:1]
                dsink = dsink - jnp.sum(jnp.exp(sk - lse_i) * delta, axis=0, keepdims=True)
        if has_sink:
            dsink_ref[0, 0] = jnp.broadcast_to(dsink, (8, LANES))

    qspec = pl.BlockSpec((1, 1, S, DH), lambda g, r: (g, r, 0, 0))
    cspec = pl.BlockSpec((1, 1, S, 1), lambda g, r: (g, r, 0, 0))
    kspec = pl.BlockSpec((1, Sk, DH), lambda g, r: (g, 0, 0))
    ins, specs = [q, k, v, o, do, lse], [qspec, kspec, kspec, qspec, qspec, cspec]
    if has_bias:
        ins += [bias[0], bias[1]]
        specs += [pl.BlockSpec((1, S, 1), lambda g, r: (g, 0, 0)), pl.BlockSpec((1, 1, S), lambda g, r: (g, 0, 0))]
    if has_sink:
        ins.append(sinks)
        specs.append(pl.BlockSpec((1, 1, 8, LANES), lambda g, r: (g, r, 0, 0)))
    if has_sel:
        ins.append(selT)
        specs.append(pl.BlockSpec((1, LANES, S), lambda g, r: (g, 0, 0)))
    if has_gate:
        ins.append(gate)
        specs.append(cspec)
    names = ["dq", "dk", "dv"]
    outs = [jax.ShapeDtypeStruct((G, R, S, DH), F32), jax.ShapeDtypeStruct((G, Sk, DH), F32),
            jax.ShapeDtypeStruct((G, Sk, DH), F32)]
    ospecs = [qspec, kspec, kspec]
    if has_bias:
        assert R == 1
        names += ["dcrow", "dccol"]
        outs += [jax.ShapeDtypeStruct((G, 1, S), F32), jax.ShapeDtypeStruct((G, S, 1), F32)]
        ospecs += [pl.BlockSpec((1, 1, S), lambda g, r: (g, 0, 0)), pl.BlockSpec((1, S, 1), lambda g, r: (g, 0, 0))]
    if has_sink:
        names.append("dsink")
        outs.append(jax.ShapeDtypeStruct((G, R, 8, LANES), F32))
        ospecs.append(pl.BlockSpec((1, 1, 8, LANES), lambda g, r: (g, r, 0, 0)))
    if has_gate:
        names.append("dz")
        outs.append(jax.ShapeDtypeStruct((G, R, S, 1), F32))
        ospecs.append(cspec)
    res = pl.pallas_call(body, out_shape=tuple(outs), grid=(G, R), in_specs=specs, out_specs=tuple(ospecs),
                         compiler_params=_params(), name=name)(*ins)
    return dict(zip(names, res))


def _combine(o0, o1, o2, z, *, name):
    H, S, _ = o0.shape

    def body(o0_ref, o1_ref, o2_ref, z_ref, o_ref):
        acc = jax.nn.sigmoid(z_ref[0, 0]) * o0_ref[0]
        acc = acc + jax.nn.sigmoid(z_ref[1, 0]) * o1_ref[0]
        acc = acc + jax.nn.sigmoid(z_ref[2, 0]) * o2_ref[0]
        o_ref[0] = acc

    blk = pl.BlockSpec((1, S, DH), lambda h: (h, 0, 0))
    return pl.pallas_call(body, out_shape=jax.ShapeDtypeStruct((H, S, DH), F32), grid=(H,),
                          in_specs=[blk, blk, blk, pl.BlockSpec((3, 1, S, 1), lambda h: (0, h, 0, 0))], out_specs=blk,
                          compiler_params=_params(), name=name)(o0, o1, o2, z)


_GC = math.sqrt(2.0 / math.pi)


def _gelu(x):
    return 0.5 * x * (1.0 + jnp.tanh(_GC * (x + 0.044715 * x * x * x)))


def _gelu_grad(x):
    t = jnp.tanh(_GC * (x + 0.044715 * x * x * x))
    return 0.5 * (1.0 + t) + 0.5 * x * (1.0 - t * t) * _GC * (1.0 + 3.0 * 0.044715 * x * x)


def _make_xb(k):
    G, S, _ = k.shape
    chunks = k.reshape(G, S // 16, 16 * DH)
    shift = jnp.concatenate([chunks[:, 1:], jnp.zeros((G, 1, 16 * DH), k.dtype)], axis=1)
    return jnp.concatenate([chunks, shift], axis=-1)


def _unmake_xb(dxb, *, name):
    G, n, _ = dxb.shape
    a = dxb[..., :16 * DH]
    b = jnp.concatenate([jnp.zeros((G, 1, 16 * DH), F32), dxb[:, :-1, 16 * DH:]], axis=1)
    return _addn(a, b, name=name).reshape(G, n * 16, DH)


def _compress_fwd(xb, pe, w1, w2, *, name):
    G, n, W = xb.shape
    Hc = w1.shape[1]

    def body(xb_ref, pe_ref, w1_ref, w2_ref, kc_ref, hid_ref):
        xv = (xb_ref[0] + pe_ref[...]).astype(_CDT)
        hid = _dot(xv, w1_ref[...].astype(_CDT))
        hid_ref[0] = hid
        kc_ref[0] = _dot(_gelu(hid).astype(_CDT), w2_ref[...].astype(_CDT))

    return pl.pallas_call(
        body, out_shape=(jax.ShapeDtypeStruct((G, n, DH), F32), jax.ShapeDtypeStruct((G, n, Hc), F32)), grid=(G,),
        in_specs=[pl.BlockSpec((1, n, W), lambda g: (g, 0, 0)), pl.BlockSpec((1, W), lambda g: (0, 0)),
                  pl.BlockSpec((W, Hc), lambda g: (0, 0)), pl.BlockSpec((Hc, DH), lambda g: (0, 0))],
        out_specs=(pl.BlockSpec((1, n, DH), lambda g: (g, 0, 0)), pl.BlockSpec((1, n, Hc), lambda g: (g, 0, 0))),
        compiler_params=_params(), name=name)(xb, pe.reshape(1, W), w1, w2)


def _compress_bwd(xb, pe, w1, w2, hid, dkc, *, name):
    G, n, W = xb.shape
    Hc = w1.shape[1]

    def body(xb_ref, pe_ref, w1_ref, w2_ref, hid_ref, dkc_ref, dxb_ref, dw1_ref, dw2_ref, dpe_ref):
        @pl.when(pl.program_id(0) == 0)
        def _():
            dw1_ref[...] = jnp.zeros_like(dw1_ref)
            dw2_ref[...] = jnp.zeros_like(dw2_ref)
            dpe_ref[...] = jnp.zeros_like(dpe_ref)

        xv = (xb_ref[0] + pe_ref[...]).astype(_CDT)
        hid = hid_ref[0]
        dk = dkc_ref[0].astype(_CDT)
        dact = _dot_nt(dk, w2_ref[...].astype(_CDT))
        dhid = (dact * _gelu_grad(hid)).astype(_CDT)
        dw2_ref[...] += _dot_tn(_gelu(hid).astype(_CDT), dk)
        dxb = _dot_nt(dhid, w1_ref[...].astype(_CDT))
        dxb_ref[0] = dxb
        dw1_ref[...] += _dot_tn(xv, dhid)
        dpe_ref[...] += jnp.sum(dxb, axis=0, keepdims=True)

    return pl.pallas_call(
        body, out_shape=(jax.ShapeDtypeStruct((G, n, W), F32), jax.ShapeDtypeStruct((W, Hc), F32),
                         jax.ShapeDtypeStruct((Hc, DH), F32), jax.ShapeDtypeStruct((1, W), F32)), grid=(G,),
        in_specs=[pl.BlockSpec((1, n, W), lambda g: (g, 0, 0)), pl.BlockSpec((1, W), lambda g: (0, 0)),
                  pl.BlockSpec((W, Hc), lambda g: (0, 0)), pl.BlockSpec((Hc, DH), lambda g: (0, 0)),
                  pl.BlockSpec((1, n, Hc), lambda g: (g, 0, 0)), pl.BlockSpec((1, n, DH), lambda g: (g, 0, 0))],
        out_specs=(pl.BlockSpec((1, n, W), lambda g: (g, 0, 0)), pl.BlockSpec((W, Hc), lambda g: (0, 0)),
                   pl.BlockSpec((Hc, DH), lambda g: (0, 0)), pl.BlockSpec((1, W), lambda g: (0, 0))),
        compiler_params=_params(), name=name)(xb, pe.reshape(1, W), w1, w2, hid, dkc)


def _overlap_T(S):
    n_cmp, n_slc = S // 16 - 1, S // 64
    cs = np.arange(n_cmp) * 16
    ce = cs + 32
    ss = np.arange(n_slc) * 64
    se = ss + 64
    ov = np.clip(np.minimum(ce[:, None], se[None, :]) - np.maximum(cs[:, None], ss[None, :]), 0, None) / 32.0
    out = np.zeros((LANES, S // 16), np.float32)
    out[:n_slc, :n_cmp] = ov.T
    return jnp.asarray(out)


def _tri(n, upper):
    r = lax.broadcasted_iota(jnp.int32, (n, n), 0)
    c = lax.broadcasted_iota(jnp.int32, (n, n), 1)
    return ((c >= r) if upper else (c <= r)).astype(jnp.bfloat16)


def _fox_gate_fwd(zf, b, *, name):
    S, H = zf.shape
    nb = S // BQ

    def body(z_ref, b_ref, c_ref):
        tri = _tri(BQ, False)
        carry = jnp.zeros((1, H), F32)
        for i in range(nb):
            z = z_ref[i * BQ:(i + 1) * BQ, :] + b_ref[...]
            lf = jnp.minimum(z, 0.0) - jnp.log(1.0 + jnp.exp(-jnp.abs(z)))
            c_ref[i * BQ:(i + 1) * BQ, :] = _dot_exact_left(tri, lf) + carry
            carry = carry + jnp.sum(lf, axis=0, keepdims=True)

    return pl.pallas_call(body, out_shape=jax.ShapeDtypeStruct((S, H), F32), compiler_params=_params(),
                          name=name)(zf, b)


def _fox_gate_bwd(zf, b, dc, *, name):
    S, H = zf.shape
    nb = S // BQ

    def body(z_ref, b_ref, dc_ref, dz_ref, db_ref):
        tri = _tri(BQ, True)
        carry = jnp.zeros((1, H), F32)
        db = jnp.zeros((1, H), F32)
        for i in reversed(range(nb)):
            rows = slice(i * BQ, (i + 1) * BQ)
            dcb = dc_ref[rows, :]
            dlf = _dot_exact_left(tri, dcb) + carry
            carry = carry + jnp.sum(dcb, axis=0, keepdims=True)
            z = z_ref[rows, :] + b_ref[...]
            dz = dlf * jax.nn.sigmoid(-z)
            dz_ref[rows, :] = dz
            db = db + jnp.sum(dz, axis=0, keepdims=True)
        db_ref[...] = db

    return pl.pallas_call(body, out_shape=(jax.ShapeDtypeStruct((S, H), F32), jax.ShapeDtypeStruct((1, H), F32)),
                          compiler_params=_params(), name=name)(zf, b, dc)


def _to_heads(a):
    S = a.shape[0]
    return a.reshape(S, -1, DH).transpose(1, 0, 2)


def _from_heads(a):
    return a.transpose(1, 0, 2).reshape(a.shape[1], -1)


def _pad_lanes(a):
    return jnp.pad(a, ((0, 0), (0, LANES - a.shape[1])))


def _ffn_fwd(x, g, W, l, tag):
    h = _rms_fwd(x, g, name=tag + "_rms")
    gu = _mm(h, W["gu"], b_lead=(l, "j"), tm=512, tn=1408, name=tag + "_gu")
    a = _swiglu_fwd(gu, name=tag + "_act")
    xo = _mm(a, W["down"], b_lead=(l,), tm=512, tn=1024, res=x, alpha=0.5, name=tag + "_down")
    return xo, (x, h, gu, a)


def _ffn_bwd(dxo, saved, g, W, gs, l, tag):
    x, h, gu, a = saved
    da = _mm(dxo, W["down"], tb=True, b_lead=(l,), tm=512, tn=1408, alpha=0.5, name=tag + "_dact")
    gs["down"] = _mm(a, dxo, ta=True, tm=1408, tn=1024, alpha=0.5, into=gs["down"], o_lead=(l,), name=tag + "_dwd")
    dgu = _swiglu_bwd(gu, da, name=tag + "_dgu")
    gs["gu"] = _mm(h, dgu, ta=True, tm=512, tn=1408, into=gs["gu"], o_lead=(l, "j"), name=tag + "_dwgu")
    dh = _mm(dgu, W["gu"], tb=True, b_lead=(l, "k"), tm=512, tn=1024, name=tag + "_dh")
    dx, dg = _rms_bwd(dh, x, g, dxo, name=tag + "_drms")
    return dx, dg


def _nsa_fwd(x, g, w, cc, ss, tag):
    S = x.shape[0]
    h = _rms_fwd(x, g, name=tag + "_rms")
    proj = _mm(h, w["w_in"], tm=512, tn=896, name=tag + "_in")
    q = _to_heads(proj[:, :1024])
    kv = proj[:, 1024:2560].reshape(S, 3, 2, NSA_G, DH)
    ks = [kv[:, b, 0].transpose(1, 0, 2) for b in range(3)]
    vs = [kv[:, b, 1].transpose(1, 0, 2) for b in range(3)]
    z = proj[:, 2560:NSA_IN].reshape(S, 3, N_HEADS).transpose(1, 2, 0)[..., None]
    roped = _rope(jnp.concatenate([q] + ks, axis=0), cc, ss, name=tag + "_rope")
    qr = roped[:N_HEADS].reshape(NSA_G, NSA_R, S, DH)
    kr = [roped[N_HEADS + 4 * b:N_HEADS + 4 * b + 4] for b in range(3)]
    xbk, xbv = _make_xb(kr[0]), _make_xb(vs[0])
    kc, hidk = _compress_fwd(xbk, w["ck_pe"], w["ck_w1"], w["ck_w2"], name=tag + "_ck")
    vc, hidv = _compress_fwd(xbv, w["cv_pe"], w["cv_w1"], w["cv_w2"], name=tag + "_cv")
    o0, lse0, selT = _attn_fwd(qr, kc, vc, kind="cmp", ovT=_overlap_T(S), name=tag + "_cmp")
    o1, lse1 = _attn_fwd(qr, kr[1], vs[1], kind="sel", selT=selT, name=tag + "_slc")
    o2, lse2 = _attn_fwd(qr, kr[2], vs[2], kind="band", window=NSA_WINDOW, name=tag + "_win")
    o = _combine(o0.reshape(N_HEADS, S, DH), o1.reshape(N_HEADS, S, DH), o2.reshape(N_HEADS, S, DH), z,
                 name=tag + "_mix")
    of = _from_heads(o)
    xo = _mm(of, w["out"], b_lead=(w["out_l"],), tm=512, tn=1024, res=x, name=tag + "_out")
    saved = (x, h, qr, kr, vs, z, xbk, xbv, hidk, hidv, kc, vc, (o0, o1, o2), (lse0, lse1, lse2), selT, of)
    return xo, saved


def _out_bwd(dxo, of, w, gs, tag):
    dof = _mm(dxo, w["out"], tb=True, b_lead=(w["out_l"],), tm=512, tn=1024, name=tag + "_dof")
    gs["out"] = _mm(of, dxo, ta=True, tm=512, tn=1024, into=gs["out"], o_lead=(w["out_l"],), name=tag + "_dwout")
    return dof


def _split_cols(dw, n_in):
    cs = n_in // 4
    return dw[:, :n_in].reshape(D_MODEL, 4, cs).transpose(1, 0, 2).astype(_WIRE)


def _nsa_bwd(dxo, saved, g, w, gs, cc, ss, tag):
    x, h, qr, kr, vs, z, xbk, xbv, hidk, hidv, kc, vc, os_, lses, selT, of = saved
    S = x.shape[0]
    dof = _out_bwd(dxo, of, w, gs, tag)
    do = _to_heads(dof).reshape(NSA_G, NSA_R, S, DH)
    zg = z.reshape(3, NSA_G, NSA_R, S, 1)
    b0 = _attn_bwd(qr, kc, vc, os_[0], do, lses[0], kind="cmp", gate=zg[0], name=tag + "_dcmp")
    b1 = _attn_bwd(qr, kr[1], vs[1], os_[1], do, lses[1], kind="sel", selT=selT, gate=zg[1], name=tag + "_dslc")
    b2 = _attn_bwd(qr, kr[2], vs[2], os_[2], do, lses[2], kind="band", window=NSA_WINDOW, gate=zg[2],
                   name=tag + "_dwin")
    dxbk, dck_w1, dck_w2, dck_pe = _compress_bwd(xbk, w["ck_pe"], w["ck_w1"], w["ck_w2"], hidk, b0["dk"],
                                                 name=tag + "_dck")
    dxbv, dcv_w1, dcv_w2, dcv_pe = _compress_bwd(xbv, w["cv_pe"], w["cv_w1"], w["cv_w2"], hidv, b0["dv"],
                                                 name=tag + "_dcv")
    dk0 = _unmake_xb(dxbk, name=tag + "_dk0")
    dv0 = _unmake_xb(dxbv, name=tag + "_dv0")
    dq = _addn(b0["dq"], b1["dq"], b2["dq"], name=tag + "_dqsum").reshape(N_HEADS, S, DH)
    unroped = _rope(jnp.concatenate([dq, dk0, b1["dk"], b2["dk"]], axis=0), cc, -ss, name=tag + "_drope")
    dks = [unroped[N_HEADS + 4 * b:N_HEADS + 4 * b + 4] for b in range(3)]
    dvs = [dv0, b1["dv"], b2["dv"]]
    dkv = jnp.stack([jnp.stack([dks[b], dvs[b]], axis=0) for b in range(3)], axis=0)
    dkv = dkv.transpose(3, 0, 1, 2, 4).reshape(S, 3 * 2 * NSA_G * DH)
    dz = jnp.stack([b0["dz"], b1["dz"], b2["dz"]], axis=0).reshape(3 * N_HEADS, S).T
    dproj = jnp.concatenate([_from_heads(unroped[:N_HEADS]), dkv, dz, jnp.zeros((S, NSA_IN_PAD - NSA_IN), F32)], axis=1)
    dh = _mm(dproj, w["w_in"], tb=True, tm=512, tn=512, name=tag + "_dh")
    dw_in = _mm(h, dproj, ta=True, tm=512, tn=896, name=tag + "_dwin_w")
    dx, dg = _rms_bwd(dh, x, g, dxo, name=tag + "_drms")
    grads = dict(w_in=_split_cols(dw_in, NSA_IN), ck_pe=dck_pe.reshape(32, DH), ck_w1=dck_w1.astype(_WIRE),
                 ck_w2=dck_w2, cv_pe=dcv_pe.reshape(32, DH), cv_w1=dcv_w1.astype(_WIRE), cv_w2=dcv_w2)
    return dx, dg, grads


def _swa_fwd(x, g, w, cc, ss, tag):
    S = x.shape[0]
    h = _rms_fwd(x, g, name=tag + "_rms")
    proj = _mm(h, w["w_in"], tm=512, tn=640, name=tag + "_in")
    q = _to_heads(proj[:, :1024])
    kv = proj[:, 1024:].reshape(S, 2, SWA_G, DH)
    k, v = kv[:, 0].transpose(1, 0, 2), kv[:, 1].transpose(1, 0, 2)
    roped = _rope(jnp.concatenate([q, k], axis=0), cc, ss, name=tag + "_rope")
    qr = roped[:N_HEADS].reshape(SWA_G, SWA_R, S, DH)
    kr = roped[N_HEADS:]
    sinks = jnp.broadcast_to(w["sinks"].reshape(SWA_G, SWA_R, 1, 1), (SWA_G, SWA_R, 8, LANES))
    o, lse = _attn_fwd(qr, kr, v, kind="band", window=SWA_WINDOW, sinks=sinks, name=tag + "_attn")
    of = _from_heads(o.reshape(N_HEADS, S, DH))
    xo = _mm(of, w["out"], b_lead=(w["out_l"],), tm=512, tn=1024, res=x, name=tag + "_out")
    return xo, (x, h, qr, kr, v, sinks, o, lse, of)


def _swa_bwd(dxo, saved, g, w, gs, cc, ss, tag):
    x, h, qr, kr, v, sinks, o, lse, of = saved
    S = x.shape[0]
    dof = _out_bwd(dxo, of, w, gs, tag)
    do = _to_heads(dof).reshape(SWA_G, SWA_R, S, DH)
    b = _attn_bwd(qr, kr, v, o, do, lse, kind="band", window=SWA_WINDOW, sinks=sinks, name=tag + "_dattn")
    unroped = _rope(jnp.concatenate([b["dq"].reshape(N_HEADS, S, DH), b["dk"]], axis=0), cc, -ss, name=tag + "_drope")
    dkv = jnp.stack([unroped[N_HEADS:], b["dv"]], axis=0).transpose(2, 0, 1, 3).reshape(S, 2 * SWA_G * DH)
    dproj = jnp.concatenate([_from_heads(unroped[:N_HEADS]), dkv], axis=1)
    dh = _mm(dproj, w["w_in"], tb=True, tm=512, tn=512, name=tag + "_dh")
    dw_in = _mm(h, dproj, ta=True, tm=512, tn=640, name=tag + "_dwin_w")
    dx, dg = _rms_bwd(dh, x, g, dxo, name=tag + "_drms")
    return dx, dg, dict(w_in=_split_cols(dw_in, SWA_IN), sinks=b["dsink"][:, :, 0, 0].reshape(N_HEADS))


def _fox_fwd(x, g, w, tag):
    S = x.shape[0]
    h = _rms_fwd(x, g, name=tag + "_rms")
    proj = _mm(h, w["w_in"], tm=512, tn=640, name=tag + "_in")
    q, k, v = (_to_heads(proj[:, i * 1024:(i + 1) * 1024]) for i in range(3))
    zf = _pad_lanes(proj[:, 3072:FOX_IN])
    bf = _pad_lanes(w["b_f"].reshape(1, N_HEADS))
    c = _fox_gate_fwd(zf, bf, name=tag + "_gate")[:, :N_HEADS]
    bias = (c.T[:, :, None], c.T[:, None, :])
    q4 = q.reshape(N_HEADS, 1, S, DH)
    o, lse = _attn_fwd(q4, k, v, kind="causal", bias=bias, name=tag + "_attn")
    of = _from_heads(o.reshape(N_HEADS, S, DH))
    xo = _mm(of, w["out"], b_lead=(w["out_l"],), tm=512, tn=1024, res=x, name=tag + "_out")
    return xo, (x, h, q4, k, v, zf, bf, bias, o, lse, of)


def _fox_bwd(dxo, saved, g, w, gs, tag):
    x, h, q4, k, v, zf, bf, bias, o, lse, of = saved
    S = x.shape[0]
    dof = _out_bwd(dxo, of, w, gs, tag)
    do = _to_heads(dof).reshape(N_HEADS, 1, S, DH)
    b = _attn_bwd(q4, k, v, o, do, lse, kind="causal", bias=bias, name=tag + "_dattn")
    dc = _addn(b["dcrow"].reshape(N_HEADS, S), b["dccol"].reshape(N_HEADS, S), name=tag + "_dc")
    dzf, db = _fox_gate_bwd(zf, bf, _pad_lanes(dc.T), name=tag + "_dgate")
    dproj = jnp.concatenate([_from_heads(b["dq"].reshape(N_HEADS, S, DH)), _from_heads(b["dk"]), _from_heads(b["dv"]),
                             dzf[:, :N_HEADS], jnp.zeros((S, FOX_IN_PAD - FOX_IN), F32)], axis=1)
    dh = _mm(dproj, w["w_in"], tb=True, tm=512, tn=512, name=tag + "_dh")
    dw_in = _mm(h, dproj, ta=True, tm=512, tn=640, name=tag + "_dwin_w")
    dx, dg = _rms_bwd(dh, x, g, dxo, name=tag + "_drms")
    return dx, dg, dict(w_in=_split_cols(dw_in, FOX_IN), b_f=db[0, :N_HEADS])


GROUPS = {
    "gu": (("ffn1_w_gu", "ffn2_w_gu"), 2),
    "down": (("ffn1_w_down", "ffn2_w_down"), 1),
    "out": (("nsa_w_out", "swa_w_out", "fox_w_out"), 1),
    "cw1": (("nsa_ck_w1", "nsa_cv_w1"), 1),
    "nsa_in": (("nsa_w_in",), 2),
    "swa_in": (("swa_w_in",), 2),
    "fox_in": (("fox_w_in",), 2),
}
OUT_SLAB = {0: 0, 3: 1, 1: 2, 2: 3}


def _mixer_weights(W, i):
    kind, j = i % 3, i // 3
    w = {"out": W["out"], "out_l": OUT_SLAB[i]}
    if kind == 0:
        w.update(w_in=W["nsa_in"][j], ck_w1=W["cw1"][j], cv_w1=W["cw1"][2 + j])
        w.update({k: W["nsa_" + k][j] for k in ("ck_pe", "ck_w2", "cv_pe", "cv_w2")})
    elif kind == 1:
        w.update(w_in=W["swa_in"][j], sinks=W["swa_sinks"][j])
    else:
        w.update(w_in=W["fox_in"][j], b_f=W["fox_b_f"][j])
    return w


def _local_step(x, tgt, W):
    S = x.shape[0]
    cc, ss = _rope_tables(S)
    saved = []
    for i in range(DEPTH):
        kind = i % 3
        x, s1 = _ffn_fwd(x, W["ffn1_norm"][i], W, i, f"l{i}f1")
        mw = _mixer_weights(W, i)
        if kind == 0:
            x, s2 = _nsa_fwd(x, W["mix_norm"][i], mw, cc, ss, f"l{i}nsa")
        elif kind == 1:
            x, s2 = _swa_fwd(x, W["mix_norm"][i], mw, cc, ss, f"l{i}swa")
        else:
            x, s2 = _fox_fwd(x, W["mix_norm"][i], mw, f"l{i}fox")
        x, s3 = _ffn_fwd(x, W["ffn2_norm"][i], W, DEPTH + i, f"l{i}f2")
        saved.append((s1, s2, s3))
    loss, dx, d_final = _loss_head(x, W["final_norm"], tgt, name="loss_head")

    gs = {k: jnp.zeros(W[k].shape, _WIRE) for k in ("gu", "down", "out")}
    norms = {k: [None] * DEPTH for k in ("ffn1_norm", "mix_norm", "ffn2_norm")}
    mix = {}
    for i in reversed(range(DEPTH)):
        kind, j = i % 3, i // 3
        s1, s2, s3 = saved[i]
        dx, norms["ffn2_norm"][i] = _ffn_bwd(dx, s3, W["ffn2_norm"][i], W, gs, DEPTH + i, f"l{i}f2")
        mw = _mixer_weights(W, i)
        if kind == 0:
            dx, dg, gm = _nsa_bwd(dx, s2, W["mix_norm"][i], mw, gs, cc, ss, f"l{i}nsa")
            pre = "nsa_"
        elif kind == 1:
            dx, dg, gm = _swa_bwd(dx, s2, W["mix_norm"][i], mw, gs, cc, ss, f"l{i}swa")
            pre = "swa_"
        else:
            dx, dg, gm = _fox_bwd(dx, s2, W["mix_norm"][i], mw, gs, f"l{i}fox")
            pre = "fox_"
        norms["mix_norm"][i] = dg
        for k, val in gm.items():
            mix.setdefault(pre + k, {})[j] = val
        dx, norms["ffn1_norm"][i] = _ffn_bwd(dx, s1, W["ffn1_norm"][i], W, gs, i, f"l{i}f1")
    mix = {k: jnp.stack([d[j] for j in sorted(d)], axis=0) for k, d in mix.items()}
    gs["cw1"] = jnp.concatenate([mix.pop("nsa_ck_w1"), mix.pop("nsa_cv_w1")], axis=0)
    gs["nsa_in"], gs["swa_in"], gs["fox_in"] = mix.pop("nsa_w_in"), mix.pop("swa_w_in"), mix.pop("fox_w_in")
    small = {k: jnp.stack(v, axis=0) for k, v in norms.items()}
    small.update(mix)
    small["final_norm"] = d_final
    return loss, dx, gs, small


def _chip_peers():
    x, y, c = lax.axis_index("x"), lax.axis_index("y"), lax.axis_index("c")
    return x, y, c, [(1 - x, y), (x, 1 - y), (1 - x, 1 - y)]


_ANY = pl.BlockSpec(memory_space=pl.ANY)


def _gather_groups(shards):
    n = len(shards)

    def body(*refs):
        srcs, outs = refs[:n], refs[n:2 * n]
        ici_send, ici_recv, d2d_send, d2d_recv, local_sems = refs[2 * n:]
        x, y, c, chips = _chip_peers()
        me = 2 * x + y
        sibling = (x, y, 1 - c)
        local = [pltpu.make_async_copy(srcs[g], outs[g].at[:, me], local_sems.at[g]) for g in range(n)]
        for cp in local:
            cp.start()
        sends = [pltpu.make_async_remote_copy(src_ref=srcs[g].at[:, c], dst_ref=outs[g].at[:, me, c],
                                              send_sem=ici_send.at[g, j], recv_sem=ici_recv.at[g, j],
                                              device_id=(px, py, c), device_id_type=MESH)
                 for g in range(n) for j, (px, py) in enumerate(chips)]
        for cp in sends:
            cp.start()
        passed = []
        for g in range(n):
            for j, (px, py) in enumerate(chips):
                landed = outs[g].at[:, 2 * px + py, c]
                pltpu.make_async_remote_copy(src_ref=landed, dst_ref=landed, send_sem=ici_send.at[g, j],
                                             recv_sem=ici_recv.at[g, j], device_id=(px, py, c),
                                             device_id_type=MESH).wait_recv()
                cp = pltpu.make_async_remote_copy(src_ref=landed, dst_ref=landed, send_sem=d2d_send.at[g, j],
                                                  recv_sem=d2d_recv.at[g, j], device_id=sibling, device_id_type=MESH)
                cp.start()
                passed.append(cp)
        for g in range(n):
            for j, (px, py) in enumerate(chips):
                theirs = outs[g].at[:, 2 * px + py, 1 - c]
                pltpu.make_async_remote_copy(src_ref=theirs, dst_ref=theirs, send_sem=d2d_send.at[g, j],
                                             recv_sem=d2d_recv.at[g, j], device_id=sibling,
                                             device_id_type=MESH).wait_recv()
        for cp in sends + passed:
            cp.wait_send()
        for cp in local:
            cp.wait()

    out_shape = tuple(jax.ShapeDtypeStruct((s.shape[0], 4) + s.shape[1:], s.dtype) for s in shards)
    return pl.pallas_call(
        body, out_shape=out_shape, in_specs=[_ANY] * n, out_specs=(_ANY,) * n,
        scratch_shapes=[pltpu.SemaphoreType.DMA((n, 3)), pltpu.SemaphoreType.DMA((n, 3)),
                        pltpu.SemaphoreType.DMA((n, 3)), pltpu.SemaphoreType.DMA((n, 3)),
                        pltpu.SemaphoreType.DMA((n,))],
        compiler_params=pltpu.CompilerParams(has_side_effects=True), name="gather_weights")(*shards)


def _scatter_groups(sends):
    n = len(sends)

    def body(*refs):
        srcs, outs = refs[:n], refs[n:2 * n]
        send_sems, recv_sems, local_sems = refs[2 * n:]
        x, y, c, chips = _chip_peers()
        me = 2 * x + y
        local = [pltpu.make_async_copy(srcs[g].at[:, me], outs[g].at[me], local_sems.at[g]) for g in range(n)]
        for cp in local:
            cp.start()
        cps = [pltpu.make_async_remote_copy(src_ref=srcs[g].at[:, 2 * px + py], dst_ref=outs[g].at[me],
                                            send_sem=send_sems.at[g, j], recv_sem=recv_sems.at[g, j],
                                            device_id=(px, py, c), device_id_type=MESH)
               for g in range(n) for j, (px, py) in enumerate(chips)]
        for cp in cps:
            cp.start()
        for g in range(n):
            for j, (px, py) in enumerate(chips):
                slot = outs[g].at[2 * px + py]
                pltpu.make_async_remote_copy(src_ref=slot, dst_ref=slot, send_sem=send_sems.at[g, j],
                                             recv_sem=recv_sems.at[g, j], device_id=(px, py, c),
                                             device_id_type=MESH).wait_recv()
        for cp in cps:
            cp.wait_send()
        for cp in local:
            cp.wait()

    out_shape = tuple(jax.ShapeDtypeStruct((4, s.shape[0]) + s.shape[2:], s.dtype) for s in sends)
    return pl.pallas_call(
        body, out_shape=out_shape, in_specs=[_ANY] * n, out_specs=(_ANY,) * n,
        scratch_shapes=[pltpu.SemaphoreType.DMA((n, 3)), pltpu.SemaphoreType.DMA((n, 3)),
                        pltpu.SemaphoreType.DMA((n,))],
        compiler_params=pltpu.CompilerParams(has_side_effects=True), name="scatter_grads")(*sends)


def _sum_slots(recv, *, name):
    C = recv.shape[-1]
    r3 = recv.reshape(4, -1, C)
    R = r3.shape[1]
    tr = _row_tile(R, C)

    def body(r_ref, o_ref):
        acc = r_ref[0].astype(F32) + r_ref[1].astype(F32)
        acc = acc + r_ref[2].astype(F32)
        o_ref[...] = acc + r_ref[3].astype(F32)

    return pl.pallas_call(body, out_shape=jax.ShapeDtypeStruct((R, C), F32), grid=(R // tr,),
                          in_specs=[pl.BlockSpec((4, tr, C), lambda i: (0, i, 0))],
                          out_specs=pl.BlockSpec((tr, C), lambda i: (i, 0)), compiler_params=_params(),
                          name=name)(r3)


def _swap_sibling(parts):
    n = len(parts)

    def body(*refs):
        srcs, outs = refs[:n], refs[n:2 * n]
        send_sems, recv_sems = refs[2 * n:]
        x, y, c = lax.axis_index("x"), lax.axis_index("y"), lax.axis_index("c")
        cps = [pltpu.make_async_remote_copy(src_ref=srcs[g], dst_ref=outs[g], send_sem=send_sems.at[g],
                                            recv_sem=recv_sems.at[g], device_id=(x, y, 1 - c), device_id_type=MESH)
               for g in range(n)]
        for cp in cps:
            cp.start()
        for cp in cps:
            cp.wait()

    return pl.pallas_call(
        body, out_shape=tuple(jax.ShapeDtypeStruct(p.shape, p.dtype) for p in parts),
        in_specs=[_ANY] * n, out_specs=(_ANY,) * n,
        scratch_shapes=[pltpu.SemaphoreType.DMA((n,)), pltpu.SemaphoreType.DMA((n,))],
        compiler_params=pltpu.CompilerParams(has_side_effects=True), name="swap_core_grads")(*parts)


def _flip(coord, bit):
    return 1 - coord if bit else coord


def _allreduce_small(v):
    n, C = v.shape

    def body(v_ref, o_ref, buf, send_sems, recv_sems):
        x, y, c = lax.axis_index("x"), lax.axis_index("y"), lax.axis_index("c")
        me = 4 * x + 2 * y + c
        buf[me] = v_ref[...]
        peers = [(_flip(x, (j >> 2) & 1), _flip(y, (j >> 1) & 1), _flip(c, j & 1)) for j in range(1, 8)]
        sends = [pltpu.make_async_remote_copy(src_ref=v_ref, dst_ref=buf.at[me], send_sem=send_sems.at[j],
                                              recv_sem=recv_sems.at[j], device_id=peer, device_id_type=MESH)
                 for j, peer in enumerate(peers)]
        for cp in sends:
            cp.start()
        for j, (px, py, pc) in enumerate(peers):
            pltpu.make_async_remote_copy(src_ref=v_ref, dst_ref=buf.at[4 * px + 2 * py + pc], send_sem=send_sems.at[j],
                                         recv_sem=recv_sems.at[j], device_id=(px, py, pc),
                                         device_id_type=MESH).wait_recv()
        for cp in sends:
            cp.wait_send()
        acc = buf[0]
        for d in range(1, 8):
            acc = acc + buf[d]
        o_ref[...] = acc

    return pl.pallas_call(
        body, out_shape=jax.ShapeDtypeStruct((n, C), F32),
        in_specs=[pl.BlockSpec(memory_space=pltpu.VMEM)], out_specs=pl.BlockSpec(memory_space=pltpu.VMEM),
        scratch_shapes=[pltpu.VMEM((8, n, C), F32), pltpu.SemaphoreType.DMA((7,)), pltpu.SemaphoreType.DMA((7,))],
        compiler_params=pltpu.CompilerParams(has_side_effects=True), name="allreduce_small")(v)


def _adamw(w, m, v, gs, *, row0=0, name):
    shape = w.shape
    w2, m2, v2 = _rows2d(w), _rows2d(m), _rows2d(v)
    g2 = [_rows2d(g) for g in gs]
    R, C = w2.shape
    tr = _row_tile(math.gcd(R, row0), C, budget=1024 * 1024)
    ng = len(g2)
    blk0 = row0 // tr

    def body(*refs):
        w_ref, m_ref, v_ref = refs[:3]
        g = refs[3][...]
        for r in refs[4:3 + ng]:
            g = g + r[...]
        g_ref, d_ref, nm_ref, nv_ref = refs[3 + ng:]
        mn = B1 * m_ref[...] + (1.0 - B1) * g
        vn = B2 * v_ref[...] + (1.0 - B2) * (g * g)
        m_hat = mn / (1.0 - B1 ** STEP)
        v_hat = vn / (1.0 - B2 ** STEP)
        g_ref[...] = g
        d_ref[...] = -LR * (m_hat / (jnp.sqrt(v_hat) + EPS) + WD * w_ref[...])
        nm_ref[...] = mn
        nv_ref[...] = vn

    spec = pl.BlockSpec((tr, C), lambda i: (i, 0))
    gspec = pl.BlockSpec((tr, C), lambda i: (blk0 + i, 0))
    outs = pl.pallas_call(body, out_shape=tuple(jax.ShapeDtypeStruct((R, C), F32) for _ in range(4)), grid=(R // tr,),
                          in_specs=[spec] * 3 + [gspec] * ng, out_specs=(spec,) * 4, compiler_params=_params(),
                          name=name)(w2, m2, v2, *g2)
    return tuple(o.reshape(shape) for o in outs)


def _small_layout(shapes):
    offs, off = {}, 0
    for k in REPLICATED:
        n = int(np.prod(shapes[k]))
        offs[k] = (off, n)
        off += -(-n // LANES) * LANES
    return offs, off


def _pack_small(d, shapes):
    offs, total = _small_layout(shapes)
    parts = []
    for k in REPLICATED:
        n = offs[k][1]
        parts.append(jnp.pad(d[k].reshape(-1).astype(F32), (0, -(-n // LANES) * LANES - n)))
    rows = -(-(total // LANES) // 8) * 8
    return jnp.pad(jnp.concatenate(parts), (0, rows * LANES - total)).reshape(rows, LANES)


def _unpack_small(a, shapes):
    offs, _ = _small_layout(shapes)
    flat = a.reshape(-1)
    return {k: flat[offs[k][0]:offs[k][0] + offs[k][1]].reshape(shapes[k]) for k in REPLICATED}


def _group_shards(w):
    shards = []
    for members, _ in GROUPS.values():
        s = jnp.concatenate([w[k].astype(_WIRE) for k in members], axis=0)
        shards.append(s.reshape(s.shape[0], 2, s.shape[1] // 2, s.shape[2]))
    return shards


def _gather_weights(w):
    W = {}
    for gname, gth in zip(GROUPS, _gather_groups(_group_shards(w))):
        L, _, _, h, C = gth.shape
        if GROUPS[gname][1] == 1:
            W[gname] = gth.reshape(L, 8 * h, C)
        elif gname == "gu":
            W[gname] = gth.reshape(L, 4, 2 * h, C)
        else:
            W[gname] = gth.reshape(L, 4, 2 * h, C).transpose(0, 2, 1, 3).reshape(L, 2 * h, 4 * C)
    W["nsa_in"] = jnp.pad(W["nsa_in"], ((0, 0), (0, 0), (0, NSA_IN_PAD - NSA_IN)))
    W["fox_in"] = jnp.pad(W["fox_in"], ((0, 0), (0, 0), (0, FOX_IN_PAD - FOX_IN)))
    for k in REPLICATED:
        W[k] = w[k]
    return W


def _reduce_and_update(w, m, v, gs, small):
    sends = []
    for gname, (members, _) in GROUPS.items():
        rows, C = w[members[0]].shape[1:]
        sends.append(gs[gname].reshape(-1, 4, rows, C))
    parts = [_sum_slots(r, name="sum_" + gname) for gname, r in zip(GROUPS, _scatter_groups(sends))]
    others = _swap_sibling(parts)
    small_shapes = {k: w[k].shape for k in REPLICATED}
    g_small = _unpack_small(_allreduce_small(_pack_small(small, small_shapes)), small_shapes)

    out_g, out_d, out_m, out_v = {}, {}, {}, {}
    for (members, _), part, other in zip(GROUPS.values(), parts, others):
        row0 = 0
        for k in members:
            out_g[k], out_d[k], out_m[k], out_v[k] = _adamw(w[k], m[k], v[k], [part, other], row0=row0,
                                                            name="adamw_" + k)
            row0 += w[k].shape[0] * w[k].shape[1]
    sm = _adamw(_pack_small(w, small_shapes), _pack_small(m, small_shapes), _pack_small(v, small_shapes),
                [_pack_small(g_small, small_shapes)], name="adamw_small")
    for d, packed in zip((out_g, out_d, out_m, out_v), sm):
        d.update(_unpack_small(packed, small_shapes))
    return out_g, out_d, out_m, out_v


def kernel(x, ffn1_norm, ffn1_w_gu, ffn1_w_down, mix_norm, ffn2_norm, ffn2_w_gu, ffn2_w_down, nsa_w_in, nsa_ck_pe, nsa_ck_w1, nsa_ck_w2, nsa_cv_pe, nsa_cv_w1, nsa_cv_w2, nsa_w_out, swa_w_in, swa_sinks, swa_w_out, fox_w_in, fox_b_f, fox_w_out, final_norm, loss_target, m_ffn1_norm, m_ffn1_w_gu, m_ffn1_w_down, m_mix_norm, m_ffn2_norm, m_ffn2_w_gu, m_ffn2_w_down, m_nsa_w_in, m_nsa_ck_pe, m_nsa_ck_w1, m_nsa_ck_w2, m_nsa_cv_pe, m_nsa_cv_w1, m_nsa_cv_w2, m_nsa_w_out, m_swa_w_in, m_swa_sinks, m_swa_w_out, m_fox_w_in, m_fox_b_f, m_fox_w_out, m_final_norm, v_ffn1_norm, v_ffn1_w_gu, v_ffn1_w_down, v_mix_norm, v_ffn2_norm, v_ffn2_w_gu, v_ffn2_w_down, v_nsa_w_in, v_nsa_ck_pe, v_nsa_ck_w1, v_nsa_ck_w2, v_nsa_cv_pe, v_nsa_cv_w1, v_nsa_cv_w2, v_nsa_w_out, v_swa_w_in, v_swa_sinks, v_swa_w_out, v_fox_w_in, v_fox_b_f, v_fox_w_out, v_final_norm):
    args = dict(locals())
    w = {k: args[k] for k in WEIGHTS}
    m = {k: args["m_" + k] for k in WEIGHTS}
    v = {k: args["v_" + k] for k in WEIGHTS}
    W = _gather_weights(w)
    loss_part, dx, gs, small = _local_step(x[0], loss_target[0], W)
    loss = lax.psum(loss_part, ("x", "y", "c"))
    out_g, out_d, out_m, out_v = _reduce_and_update(w, m, v, gs, small)
    return (loss, dx[None], *[out_g[k] for k in WEIGHTS], *[out_d[k] for k in WEIGHTS],
            *[out_m[k] for k in WEIGHTS], *[out_v[k] for k in WEIGHTS])
```

```python
import math

import numpy as np
import jax
import jax.numpy as jnp
from jax import lax
from jax.experimental import pallas as pl
from jax.experimental.pallas import tpu as pltpu

F32 = jnp.float32
_CDT = jnp.bfloat16
_WIRE = jnp.bfloat16
_VMEM_LIMIT = 56 * 1024 * 1024

D_MODEL = 1024
DEPTH = 4
DH = 64
N_HEADS = 16
RMS_EPS = 1e-6
NEG = -1e30
SCALE = DH ** -0.5
BQ = 256
LANES = 128
NSA_G, NSA_R = 4, 4
NSA_WINDOW = 512
NSA_TOPK = 16
NSA_BONUS = 1e4
SWA_G, SWA_R = 2, 8
SWA_WINDOW = 128
NSA_IN, NSA_IN_PAD = 2608, 2688
SWA_IN = 1280
FOX_IN, FOX_IN_PAD = 3088, 3200
LR, B1, B2, EPS, WD, STEP = 0.001, 0.9, 0.999, 1e-08, 0.01, 10
MESH = pl.DeviceIdType.MESH

REPLICATED = ["ffn1_norm", "mix_norm", "ffn2_norm", "nsa_ck_pe", "nsa_ck_w2", "nsa_cv_pe", "nsa_cv_w2",
              "swa_sinks", "fox_b_f", "final_norm"]
WEIGHTS = ['ffn1_norm', 'ffn1_w_gu', 'ffn1_w_down', 'mix_norm', 'ffn2_norm', 'ffn2_w_gu', 'ffn2_w_down',
           'nsa_w_in', 'nsa_ck_pe', 'nsa_ck_w1', 'nsa_ck_w2', 'nsa_cv_pe', 'nsa_cv_w1', 'nsa_cv_w2', 'nsa_w_out',
           'swa_w_in', 'swa_sinks', 'swa_w_out', 'fox_w_in', 'fox_b_f', 'fox_w_out', 'final_norm']


def _params(**kw):
    return pltpu.CompilerParams(vmem_limit_bytes=_VMEM_LIMIT, **kw)


def _dot(a, b):
    return lax.dot_general(a, b, (((1,), (0,)), ((), ())), preferred_element_type=F32)


def _dot_nt(a, b):
    return lax.dot_general(a, b, (((1,), (1,)), ((), ())), preferred_element_type=F32)


def _dot_tn(a, b):
    return lax.dot_general(a, b, (((0,), (0,)), ((), ())), preferred_element_type=F32)


def _split3(x):
    hi = x.astype(jnp.bfloat16)
    r1 = x - hi.astype(F32)
    mid = r1.astype(jnp.bfloat16)
    lo = (r1 - mid.astype(F32)).astype(jnp.bfloat16)
    return hi, mid, lo


def _dot_exact(x, p):
    hi, mid, lo = _split3(x)
    return _dot(hi, p) + _dot(mid, p) + _dot(lo, p)


def _dot_exact_left(p, x):
    hi, mid, lo = _split3(x)
    return _dot(p, hi) + _dot(p, mid) + _dot(p, lo)


_ANY = pl.BlockSpec(memory_space=pl.ANY)
_COMM = pltpu.CompilerParams(has_side_effects=True)


def _chip_peers():
    x, y, c = lax.axis_index("x"), lax.axis_index("y"), lax.axis_index("c")
    return x, y, c, [(1 - x, y), (x, 1 - y), (1 - x, 1 - y)]


class _Job:
    def __init__(self, ins, outs, nsem, copies, cost, alias):
        self.ins, self.outs, self.nsem, self.copies, self.cost, self.alias = ins, outs, nsem, copies, cost, alias
        self.results = None

    def inputs(self):
        return self.ins


class _Task:
    def __init__(self, src, out, parts):
        self.src, self.out, self.parts, self.done, self.jobs = src, out, parts, 0, []

    def result(self):
        return self.jobs[-1].results[0]

    def complete(self):
        return self.done == len(self.parts) and self.jobs[-1].results is not None

    def ready(self):
        if isinstance(self.src, _Task) and not self.src.complete():
            return False
        return not self.jobs or self.jobs[-1].results is not None

    def next_job(self, n):
        parts = self.parts[self.done:self.done + n]
        in_place = isinstance(self.src, _Task)
        ins = [] if in_place else [self.src]
        prev = self.jobs[-1].results[0] if self.jobs else (self.src.result() if in_place else None)
        alias = {}
        if prev is not None:
            alias = {len(ins): 0}
            ins = ins + [prev]

        def copies(in_refs, out_refs, sems):
            src = out_refs[0] if in_place else in_refs[0]
            out, first = [], 0
            for build, nsem, _ in parts:
                out += build(src, out_refs[0], sems, first)
                first += nsem
            return out

        job = _Job(ins, [self.out], sum(p[1] for p in parts), copies, sum(p[2] for p in parts), alias)
        self.done += n
        self.jobs.append(job)
        return job


def _pcall(body, *, out_shape, in_specs, out_specs, name, grid=(), scratch_shapes=(), aliases=None, jobs=(), ins):
    single = not isinstance(out_shape, (tuple, list))
    out_shape = (out_shape,) if single else tuple(out_shape)
    out_specs = (out_specs,) if single else tuple(out_specs)
    n_in, n_out, n_scr = len(ins), len(out_shape), len(scratch_shapes)
    side_ins = [a for j in jobs for a in j.inputs()]
    side_outs = [o for j in jobs for o in j.outs]
    aliases = dict(aliases or {})
    i0, o0 = n_in, n_out
    for j in jobs:
        for a, b in j.alias.items():
            aliases[i0 + a] = o0 + b
        i0 += len(j.ins)
        o0 += len(j.outs)

    def body2(*refs):
        in_refs, s_in = refs[:n_in], refs[n_in:n_in + len(side_ins)]
        r = n_in + len(side_ins)
        out_refs, s_out = refs[r:r + n_out], refs[r + n_out:r + n_out + len(side_outs)]
        r += n_out + len(side_outs)
        scr, sems = refs[r:r + n_scr], refs[r + n_scr:]
        def descriptors():
            out, a, b = [], 0, 0
            for j, sem in zip(jobs, sems):
                out += j.copies(s_in[a:a + len(j.ins)], s_out[b:b + len(j.outs)], sem)
                a += len(j.ins)
                b += len(j.outs)
            return out

        def at_step(steps, fn):
            cond = None
            for d, s in enumerate(steps):
                hit = pl.program_id(d) == s
                cond = hit if cond is None else jnp.logical_and(cond, hit)
            if cond is None:
                fn()
            else:
                pl.when(cond)(fn)

        def start_all():
            for cp in descriptors():
                cp.start()

        def wait_all():
            for cp in descriptors():
                cp.wait()

        if jobs:
            at_step([0] * len(grid), start_all)
        body(*in_refs, *out_refs, *scr)
        if jobs:
            at_step([n - 1 for n in grid], wait_all)

    res = pl.pallas_call(
        body2, out_shape=out_shape + tuple(side_outs), grid=grid, in_specs=list(in_specs) + [_ANY] * len(side_ins),
        out_specs=out_specs + (_ANY,) * len(side_outs),
        scratch_shapes=list(scratch_shapes) + [pltpu.SemaphoreType.DMA((j.nsem,)) for j in jobs],
        input_output_aliases=aliases,
        compiler_params=_params(has_side_effects=True) if jobs else _params(), name=name)(*ins, *side_ins)
    b = n_out
    for j in jobs:
        j.results = list(res[b:b + len(j.outs)])
        b += len(j.outs)
    return res[0] if single else tuple(res[:n_out])


def _comm_call(jobs, name):
    def body():
        pass

    _pcall(body, out_shape=(), in_specs=[], out_specs=(), name=name, jobs=jobs, ins=[])


def _gather_task1(shard, l):
    _, _, h, C = shard.shape
    mb = h * C * jnp.dtype(_WIRE).itemsize / 1e6

    def own(src, dst, sems, s0):
        x, y, _, _ = _chip_peers()
        return [pltpu.make_async_copy(src.at[l], dst.at[2 * x + y], sems.at[s0])]

    def to_peer(j):
        def build(src, dst, sems, s0):
            x, y, c, chips = _chip_peers()
            px, py = chips[j]
            return [pltpu.make_async_remote_copy(src_ref=src.at[l, c], dst_ref=dst.at[2 * x + y, c],
                                                 send_sem=sems.at[s0], recv_sem=sems.at[s0 + 1],
                                                 device_id=(px, py, c), device_id_type=MESH)]
        return build

    parts = [(own, 1, 1.0 + LOCAL_US_PER_MB * 2 * mb)] + [(to_peer(j), 2, ICI_US_PER_MB * mb) for j in range(3)]
    return _Task(shard, jax.ShapeDtypeStruct((4, 2, h, C), shard.dtype), parts)


def _gather_task2(task1):
    _, _, h, C = task1.out.shape
    mb = h * C * jnp.dtype(_WIRE).itemsize / 1e6

    def forward(j):
        def build(_, dst, sems, s0):
            x, y, c, chips = _chip_peers()
            px, py = chips[j]
            landed = dst.at[2 * px + py, c]
            return [pltpu.make_async_remote_copy(src_ref=landed, dst_ref=landed, send_sem=sems.at[s0],
                                                 recv_sem=sems.at[s0 + 1], device_id=(x, y, 1 - c),
                                                 device_id_type=MESH)]
        return build

    return _Task(task1, task1.out, [(forward(j), 2, 1.0 + D2D_US_PER_MB * mb) for j in range(3)])


def _scatter_task(G):
    mb = G.shape[1] * G.shape[2] * jnp.dtype(_WIRE).itemsize / 1e6

    def own(src, dst, sems, s0):
        x, y, _, _ = _chip_peers()
        return [pltpu.make_async_copy(src.at[2 * x + y], dst.at[0], sems.at[s0])]

    def to_peer(j):
        def build(src, dst, sems, s0):
            _, _, c, chips = _chip_peers()
            px, py = chips[j]
            return [pltpu.make_async_remote_copy(src_ref=src.at[2 * px + py], dst_ref=dst.at[1 + j],
                                                 send_sem=sems.at[s0], recv_sem=sems.at[s0 + 1],
                                                 device_id=(px, py, c), device_id_type=MESH)]
        return build

    parts = [(own, 1, 1.0 + LOCAL_US_PER_MB * mb)] + [(to_peer(j), 2, ICI_US_PER_MB * mb) for j in range(3)]
    return _Task(G, jax.ShapeDtypeStruct(G.shape, G.dtype), parts)


CARRIER_US = {
    "f_gu": 44, "f_act": 20, "f_down": 20, "f_dact": 26, "f_dwd": 20, "f_dgu": 38, "f_dwgu": 42, "f_dh": 46,
    "nsa_in": 25, "nsa_cmp": 86, "nsa_slc": 122, "nsa_win": 74, "nsa_out": 13, "nsa_dof": 11, "nsa_dwout": 11,
    "nsa_dcmp": 58, "nsa_dslc": 195, "nsa_dwin": 108, "nsa_dh": 26, "nsa_dwin_w": 25,
    "swa_in": 16, "swa_attn": 74, "swa_out": 13, "swa_dof": 11, "swa_dwout": 11, "swa_dattn": 74, "swa_dh": 17,
    "swa_dwin_w": 18,
    "fox_in": 34, "fox_attn": 104, "fox_out": 13, "fox_dof": 11, "fox_dwout": 11, "fox_dattn": 161, "fox_dh": 30,
    "fox_dwin_w": 30,
}
ICI_US_PER_MB = 14.0
D2D_US_PER_MB = 2.3
LOCAL_US_PER_MB = 1.5


class _Sched:
    def __init__(self):
        self.queue, self.credit, self.n_flush = [], 0.0, 0

    def push(self, task):
        self.queue.append(task)
        return task

    def _jobs(self, fits):
        jobs = []
        while self.queue and self.queue[0].ready():
            task, n = self.queue[0], 0
            while task.done + n < len(task.parts) and fits(task.parts[task.done + n][2]):
                n += 1
            if n:
                jobs.append(task.next_job(n))
            if task.done < len(task.parts):
                break
            self.queue.pop(0)
        return jobs

    def take(self, name):
        kind = name[2:].split("_", 1)
        key = ("f" if kind[0] in ("f1", "f2") else kind[0]) + "_" + kind[1]
        self.credit = min(self.credit, 0.0) + CARRIER_US.get(key, 0.0)

        def fits(cost):
            if self.credit < 0.5 * cost:
                return False
            self.credit -= cost
            return True

        return self._jobs(fits)

    def finish(self, task=None):
        while self.queue and (task is None or not task.complete()):
            jobs = []
            while self.queue and self.queue[0].ready():
                head = self.queue.pop(0)
                jobs.append(head.next_job(len(head.parts) - head.done))
                if head is task:
                    break
            assert jobs, "the task at the head of the queue waits for one that was never queued"
            _comm_call(jobs, name=f"exchange_{self.n_flush}")
            self.n_flush += 1


def _mm(a, b, *, ta=False, tb=False, tm, tn, b_lead=(), into=None, o_lead=(), out_dtype=F32, res=None, alpha=1.0,
        ctx=None, name):
    M = a.shape[1] if ta else a.shape[0]
    K = a.shape[0] if ta else a.shape[1]
    bk, bn = (b.shape[-1], b.shape[-2]) if tb else (b.shape[-2], b.shape[-1])
    j_lead, k_lead = "j" in b_lead, "k" in b_lead
    N = bn * (b.shape[b_lead.index("j")] if j_lead else 1)
    nk = b.shape[b_lead.index("k")] if k_lead else 1
    tk = K // nk
    assert tk == bk, (name, K, nk, bk)
    tm = min(tm, M)
    tn = bn if j_lead else min(tn, N)
    assert M % tm == 0 and N % tn == 0, (name, M, N, tm, tn)
    nb, no = len(b_lead), len(o_lead)

    def pick(lead, j, k):
        return tuple(j if t == "j" else k if t == "k" else t for t in lead)

    a_spec = pl.BlockSpec((tk, tm), lambda i, j, k: (k, i)) if ta else pl.BlockSpec((tm, tk), lambda i, j, k: (i, k))
    if tb:
        b_spec = pl.BlockSpec((None,) * nb + (tn, tk),
                              lambda i, j, k: pick(b_lead, j, k) + (0 if j_lead else j, 0 if k_lead else k))
    else:
        b_spec = pl.BlockSpec((None,) * nb + (tk, tn),
                              lambda i, j, k: pick(b_lead, j, k) + (0 if k_lead else k, 0 if j_lead else j))
    r_spec = pl.BlockSpec((tm, tn), lambda i, j, k: (i, j))
    o_spec = pl.BlockSpec((None,) * no + (tm, tn), lambda i, j, k: pick(o_lead, j, k) + (i, 0 if "j" in o_lead else j))
    dn = (((0 if ta else 1,), (1 if tb else 0,)), ((), ()))
    has_res, has_into = res is not None, into is not None
    if has_into:
        out_dtype = into.dtype

    def body(*refs):
        a_ref, b_ref = refs[0], refs[1]
        r_ref = refs[2] if has_res else None
        o_ref = refs[2 + has_res + has_into]
        prod = lax.dot_general(a_ref[...].astype(_CDT), b_ref[...].astype(_CDT), dn, preferred_element_type=F32)

        def finish(acc):
            if alpha != 1.0:
                acc = acc * alpha
            if has_res:
                acc = r_ref[...] + acc
            o_ref[...] = acc.astype(out_dtype)

        if nk == 1:
            finish(prod)
        else:
            acc_ref = refs[-1]
            k = pl.program_id(2)

            @pl.when(k == 0)
            def _():
                acc_ref[...] = prod

            @pl.when(k != 0)
            def _():
                acc_ref[...] += prod

            @pl.when(k == nk - 1)
            def _():
                finish(acc_ref[...])

    ins, specs = [a, b], [a_spec, b_spec]
    if has_res:
        ins.append(res)
        specs.append(r_spec)
    aliases = {}
    if has_into:
        aliases = {len(ins): 0}
        ins.append(into)
        specs.append(_ANY)
        out_shape = jax.ShapeDtypeStruct(into.shape, into.dtype)
    elif o_lead == ("j",):
        out_shape = jax.ShapeDtypeStruct((N // tn, M, tn), out_dtype)
    else:
        assert not o_lead
        out_shape = jax.ShapeDtypeStruct((M, N), out_dtype)
    scratch = [pltpu.VMEM((tm, tn), F32)] if nk > 1 else []
    return _pcall(body, out_shape=out_shape, grid=(M // tm, N // tn, nk), in_specs=specs, out_specs=o_spec,
                  scratch_shapes=scratch, aliases=aliases, name=name, jobs=ctx.take(name) if ctx else (), ins=ins)


def _rows2d(a):
    return a.reshape(-1, a.shape[-1])


def _row_tile(rows, cols, itemsize=4, budget=2 * 1024 * 1024):
    t = rows
    while t % 2 == 0 and t * cols * itemsize > budget and (t // 2) % 8 == 0:
        t //= 2
    return t


def _addn(*xs, name):
    shape = xs[0].shape
    x2 = [_rows2d(x) for x in xs]
    R, C = x2[0].shape
    tr = _row_tile(R, C)

    def body(*refs):
        acc = refs[0][...]
        for r in refs[1:-1]:
            acc = acc + r[...]
        refs[-1][...] = acc

    spec = pl.BlockSpec((tr, C), lambda i: (i, 0))
    out = pl.pallas_call(body, out_shape=jax.ShapeDtypeStruct((R, C), F32), grid=(R // tr,),
                         in_specs=[spec] * len(x2), out_specs=spec, compiler_params=_params(), name=name)(*x2)
    return out.reshape(shape)


def _rms_fwd(x, g, *, name):
    S, D = x.shape
    tr = 256

    def body(x_ref, g_ref, h_ref):
        xv = x_ref[...]
        rstd = lax.rsqrt(jnp.mean(xv * xv, axis=-1, keepdims=True) + RMS_EPS)
        h_ref[...] = (xv * rstd * g_ref[...]).astype(_CDT)

    return pl.pallas_call(body, out_shape=jax.ShapeDtypeStruct((S, D), _CDT), grid=(S // tr,),
                          in_specs=[pl.BlockSpec((tr, D), lambda i: (i, 0)), pl.BlockSpec((1, D), lambda i: (0, 0))],
                          out_specs=pl.BlockSpec((tr, D), lambda i: (i, 0)), compiler_params=_params(),
                          name=name)(x, g.reshape(1, D))


def _rms_bwd(dh, x, g, dres, *, name):
    S, D = x.shape
    tr = 256

    def body(dh_ref, x_ref, g_ref, dres_ref, dx_ref, dg_ref):
        xv = x_ref[...]
        rstd = lax.rsqrt(jnp.mean(xv * xv, axis=-1, keepdims=True) + RMS_EPS)
        xhat = xv * rstd
        dhv = dh_ref[...]
        dxhat = dhv * g_ref[...]
        dx_ref[...] = dres_ref[...] + rstd * (dxhat - xhat * jnp.mean(dxhat * xhat, axis=-1, keepdims=True))

        @pl.when(pl.program_id(0) == 0)
        def _():
            dg_ref[...] = jnp.zeros_like(dg_ref)

        dg_ref[...] += jnp.sum(dhv * xhat, axis=0, keepdims=True)

    row = pl.BlockSpec((tr, D), lambda i: (i, 0))
    vec = pl.BlockSpec((1, D), lambda i: (0, 0))
    dx, dg = pl.pallas_call(body, out_shape=(jax.ShapeDtypeStruct((S, D), F32), jax.ShapeDtypeStruct((1, D), F32)),
                            grid=(S // tr,), in_specs=[row, row, vec, row], out_specs=(row, vec),
                            compiler_params=_params(), name=name)(dh, x, g.reshape(1, D), dres)
    return dx, dg.reshape(D)


def _swiglu_fwd(gu, *, ctx=None, name):
    S, W = gu.shape
    H = W // 2
    tr = 256

    def body(gu_ref, a_ref):
        g = gu_ref[:, :H]
        u = gu_ref[:, H:]
        a_ref[...] = (g * jax.nn.sigmoid(g) * u).astype(_CDT)

    return _pcall(body, out_shape=jax.ShapeDtypeStruct((S, H), _CDT), grid=(S // tr,),
                  in_specs=[pl.BlockSpec((tr, W), lambda i: (i, 0))], out_specs=pl.BlockSpec((tr, H), lambda i: (i, 0)),
                  name=name, jobs=ctx.take(name) if ctx else (), ins=[gu])


def _swiglu_bwd(gu, da, *, ctx=None, name):
    S, W = gu.shape
    H = W // 2
    tr = 256

    def body(gu_ref, da_ref, d_ref):
        g = gu_ref[:, :H]
        u = gu_ref[:, H:]
        dav = da_ref[...]
        sg = jax.nn.sigmoid(g)
        silu = g * sg
        d_ref[:, :H] = dav * u * (sg + silu * (1.0 - sg))
        d_ref[:, H:] = dav * silu

    return _pcall(body, out_shape=jax.ShapeDtypeStruct((S, W), F32), grid=(S // tr,),
                  in_specs=[pl.BlockSpec((tr, W), lambda i: (i, 0)), pl.BlockSpec((tr, H), lambda i: (i, 0))],
                  out_specs=pl.BlockSpec((tr, W), lambda i: (i, 0)), name=name,
                  jobs=ctx.take(name) if ctx else (), ins=[gu, da])


def _loss_head(x, g, tgt, *, name):
    S, D = x.shape
    tr = 256

    def body(x_ref, g_ref, t_ref, loss_ref, dx_ref, dg_ref):
        xv = x_ref[...]
        rstd = lax.rsqrt(jnp.mean(xv * xv, axis=-1, keepdims=True) + RMS_EPS)
        xhat = xv * rstd
        err = xhat * g_ref[...] - t_ref[...]
        part = 0.5 * jnp.sum(jnp.mean(err * err, axis=-1, keepdims=True), axis=0, keepdims=True)
        dy = err * (1.0 / D)
        dxhat = dy * g_ref[...]
        dx_ref[...] = rstd * (dxhat - xhat * jnp.mean(dxhat * xhat, axis=-1, keepdims=True))

        @pl.when(pl.program_id(0) == 0)
        def _():
            dg_ref[...] = jnp.zeros_like(dg_ref)
            loss_ref[...] = jnp.zeros_like(loss_ref)

        dg_ref[...] += jnp.sum(dy * xhat, axis=0, keepdims=True)
        loss_ref[...] += jnp.broadcast_to(part, loss_ref.shape)

    row = pl.BlockSpec((tr, D), lambda i: (i, 0))
    vec = pl.BlockSpec((1, D), lambda i: (0, 0))
    loss, dx, dg = pl.pallas_call(
        body, out_shape=(jax.ShapeDtypeStruct((8, LANES), F32), jax.ShapeDtypeStruct((S, D), F32),
                         jax.ShapeDtypeStruct((1, D), F32)),
        grid=(S // tr,), in_specs=[row, vec, row], out_specs=(pl.BlockSpec((8, LANES), lambda i: (0, 0)), row, vec),
        compiler_params=_params(), name=name)(x, g.reshape(1, D), tgt)
    return loss[0, 0], dx, dg.reshape(D)


def _rope_tables(S):
    inv = 10000.0 ** (-jnp.arange(0, DH, 2, dtype=F32) / DH)
    ang = jnp.arange(S, dtype=F32)[:, None] * inv[None, :]
    cos, sin = jnp.cos(ang), jnp.sin(ang)
    return jnp.concatenate([cos, cos], -1), jnp.concatenate([-sin, sin], -1)


def _swap_matrix():
    p = np.zeros((DH, DH), np.float32)
    for j in range(DH // 2):
        p[j + DH // 2, j] = 1.0
        p[j, j + DH // 2] = 1.0
    return jnp.asarray(p, jnp.bfloat16)


def _rope(x, cc, ss, *, name):
    n, S, _ = x.shape

    def body(x_ref, c_ref, s_ref, p_ref, o_ref):
        xv = x_ref[0]
        o_ref[0] = xv * c_ref[...] + _dot_exact(xv, p_ref[...]) * s_ref[...]

    tab = pl.BlockSpec((S, DH), lambda i: (0, 0))
    blk = pl.BlockSpec((1, S, DH), lambda i: (i, 0, 0))
    return pl.pallas_call(body, out_shape=jax.ShapeDtypeStruct(x.shape, F32), grid=(n,),
                          in_specs=[blk, tab, tab, pl.BlockSpec((DH, DH), lambda i: (0, 0))], out_specs=blk,
                          compiler_params=_params(), name=name)(x, cc, ss, _swap_matrix())


def _key_range(kind, i, window, Sk):
    if kind == "cmp":
        return 0, Sk
    hi = (i + 1) * BQ
    if kind == "band":
        return max(0, i * BQ - window), hi
    return 0, hi


def _attn_mask(kind, i, lo, hi, window):
    shape = (BQ, hi - lo)
    qpos = i * BQ + lax.broadcasted_iota(jnp.int32, shape, 0)
    kpos = lo + lax.broadcasted_iota(jnp.int32, shape, 1)
    if kind == "cmp":
        return kpos * 16 + 31 <= qpos
    mask = kpos <= qpos
    if kind == "band":
        mask = mask & (qpos - kpos < window)
    return mask


def _sel_expand(lo, hi):
    shape = (LANES, hi - lo)
    j = lax.broadcasted_iota(jnp.int32, shape, 0)
    key = lo + lax.broadcasted_iota(jnp.int32, shape, 1)
    return (jnp.right_shift(key, 6) == j).astype(_CDT)


def _scores(kind, i, lo, hi, window, qb, kb, ccol_ref, crow_ref, sel_ref):
    s = _dot_nt(qb, kb) * SCALE
    if ccol_ref is not None:
        s = s + ccol_ref[0, i * BQ:(i + 1) * BQ, :] - crow_ref[0, :, lo:hi]
    mask = _attn_mask(kind, i, lo, hi, window)
    if sel_ref is not None:
        chosen = _dot_tn(sel_ref[0, :, i * BQ:(i + 1) * BQ].astype(_CDT), _sel_expand(lo, hi))
        mask = mask & (chosen > 0.5)
    return jnp.where(mask, s, NEG), mask


def _attn_fwd(q, k, v, *, kind, window=0, bias=None, sinks=None, selT=None, ovT=None, ctx=None, name):
    G, R, S, _ = q.shape
    Sk = k.shape[1]
    nq = S // BQ
    n_slc = S // 64
    has_bias, has_sink, has_sel, is_cmp = bias is not None, sinks is not None, selT is not None, kind == "cmp"

    def body(*refs):
        it = iter(refs)
        q_ref, k_ref, v_ref = next(it), next(it), next(it)
        ccol_ref, crow_ref = (next(it), next(it)) if has_bias else (None, None)
        sink_ref = next(it) if has_sink else None
        sel_ref = next(it) if has_sel else None
        ov_ref = next(it) if is_cmp else None
        o_ref, lse_ref = next(it), next(it)
        selo_ref, imp_ref = (next(it), next(it)) if is_cmp else (None, None)
        r = pl.program_id(1)
        for i in range(nq):
            lo, hi = _key_range(kind, i, window, Sk)
            rows = slice(i * BQ, (i + 1) * BQ)
            qb = q_ref[0, 0, rows, :].astype(_CDT)
            kb = k_ref[0, lo:hi, :].astype(_CDT)
            vb = v_ref[0, lo:hi, :].astype(_CDT)
            s, mask = _scores(kind, i, lo, hi, window, qb, kb, ccol_ref, crow_ref, sel_ref)
            m = jnp.max(s, axis=-1, keepdims=True)
            if has_sink:
                sk = sink_ref[0, 0, 0:1, 0:1]
                m = jnp.maximum(m, sk)
            e = jnp.exp(s - m)
            if is_cmp:
                e = jnp.where(mask, e, 0.0)
            l = jnp.sum(e, axis=-1, keepdims=True)
            if has_sink:
                l = l + jnp.exp(sk - m)
            if is_cmp:
                l = jnp.where(l > 0.0, l, 1.0)
            p = e * (1.0 / l)
            o_ref[0, 0, rows, :] = _dot(p.astype(_CDT), vb)
            lse_ref[0, 0, rows, :] = m + jnp.log(l)
            if is_cmp:
                part = _dot_nt(ov_ref[...].astype(_CDT), p.astype(_CDT))

                @pl.when(r == 0)
                def _():
                    imp_ref[:, rows] = part

                @pl.when(r != 0)
                def _():
                    imp_ref[:, rows] += part

        if is_cmp:
            @pl.when(r == R - 1)
            def _():
                shape = (LANES, S)
                j = lax.broadcasted_iota(jnp.int32, shape, 0)
                tb = jnp.right_shift(lax.broadcasted_iota(jnp.int32, shape, 1), 6)
                forced = (j == 0) | (j == tb) | (j == tb - 1)
                imp = jnp.where(j > tb, NEG, jnp.where(forced, NSA_BONUS, imp_ref[...]))
                imp = jnp.where(j >= n_slc, -3e38, imp)
                imp_ref[...] = imp
                cnt = jnp.zeros(shape, F32)
                for jp in range(n_slc):
                    row = imp_ref[jp:jp + 1, :]
                    ahead = (row > imp) | ((row == imp) & (jp < j))
                    cnt = cnt + ahead.astype(F32)
                selo_ref[0] = (cnt < float(min(NSA_TOPK, n_slc))).astype(F32)

    qspec = pl.BlockSpec((1, 1, S, DH), lambda g, r: (g, r, 0, 0))
    kspec = pl.BlockSpec((1, Sk, DH), lambda g, r: (g, 0, 0))
    ins, specs = [q, k, v], [qspec, kspec, kspec]
    if has_bias:
        ins += [bias[0], bias[1]]
        specs += [pl.BlockSpec((1, S, 1), lambda g, r: (g, 0, 0)), pl.BlockSpec((1, 1, S), lambda g, r: (g, 0, 0))]
    if has_sink:
        ins.append(sinks)
        specs.append(pl.BlockSpec((1, 1, 8, LANES), lambda g, r: (g, r, 0, 0)))
    if has_sel:
        ins.append(selT)
        specs.append(pl.BlockSpec((1, LANES, S), lambda g, r: (g, 0, 0)))
    if is_cmp:
        ins.append(ovT)
        specs.append(pl.BlockSpec((LANES, Sk), lambda g, r: (0, 0)))
    outs = [jax.ShapeDtypeStruct((G, R, S, DH), F32), jax.ShapeDtypeStruct((G, R, S, 1), F32)]
    ospecs = [qspec, pl.BlockSpec((1, 1, S, 1), lambda g, r: (g, r, 0, 0))]
    scratch = []
    if is_cmp:
        outs.append(jax.ShapeDtypeStruct((G, LANES, S), F32))
        ospecs.append(pl.BlockSpec((1, LANES, S), lambda g, r: (g, 0, 0)))
        scratch.append(pltpu.VMEM((LANES, S), F32))
    return _pcall(body, out_shape=tuple(outs), grid=(G, R), in_specs=specs, out_specs=tuple(ospecs),
                  scratch_shapes=scratch, name=name, jobs=ctx.take(name) if ctx else (), ins=ins)


def _attn_bwd(q, k, v, o, do, lse, *, kind, window=0, bias=None, sinks=None, selT=None, gate=None, ctx=None, name):
    G, R, S, _ = q.shape
    Sk = k.shape[1]
    nq = S // BQ
    has_bias, has_sink, has_sel, has_gate = bias is not None, sinks is not None, selT is not None, gate is not None

    def body(*refs):
        it = iter(refs)
        q_ref, k_ref, v_ref, o_ref, do_ref, lse_ref = (next(it) for _ in range(6))
        ccol_ref, crow_ref = (next(it), next(it)) if has_bias else (None, None)
        sink_ref = next(it) if has_sink else None
        sel_ref = next(it) if has_sel else None
        z_ref = next(it) if has_gate else None
        dq_ref, dk_ref, dv_ref = next(it), next(it), next(it)
        dc_ref, dccol_ref = (next(it), next(it)) if has_bias else (None, None)
        dsink_ref = next(it) if has_sink else None
        dz_ref = next(it) if has_gate else None
        r = pl.program_id(1)

        @pl.when(r == 0)
        def _():
            dk_ref[...] = jnp.zeros_like(dk_ref)
            dv_ref[...] = jnp.zeros_like(dv_ref)
            if has_bias:
                dc_ref[...] = jnp.zeros_like(dc_ref)

        dsink = jnp.zeros((1, 1), F32)
        for i in range(nq):
            lo, hi = _key_range(kind, i, window, Sk)
            rows = slice(i * BQ, (i + 1) * BQ)
            qb = q_ref[0, 0, rows, :].astype(_CDT)
            kb = k_ref[0, lo:hi, :].astype(_CDT)
            vb = v_ref[0, lo:hi, :].astype(_CDT)
            s, mask = _scores(kind, i, lo, hi, window, qb, kb, ccol_ref, crow_ref, sel_ref)
            lse_i = lse_ref[0, 0, rows, :]
            p = jnp.where(mask, jnp.exp(s - lse_i), 0.0)
            dob = do_ref[0, 0, rows, :]
            if has_gate:
                od = jnp.sum(o_ref[0, 0, rows, :] * dob, axis=-1, keepdims=True)
                sg = jax.nn.sigmoid(z_ref[0, 0, rows, :])
                dob = dob * sg
                dz_ref[0, 0, rows, :] = od * sg * (1.0 - sg)
            dob = dob.astype(_CDT)
            dp = _dot_nt(dob, vb)
            delta = jnp.sum(p * dp, axis=-1, keepdims=True)
            ds = p * (dp - delta)
            dsb = ds.astype(_CDT)
            dq_ref[0, 0, rows, :] = _dot(dsb, kb) * SCALE
            dk_ref[0, lo:hi, :] += _dot_tn(dsb, qb) * SCALE
            dv_ref[0, lo:hi, :] += _dot_tn(p.astype(_CDT), dob)
            if has_bias:
                dccol_ref[0, rows, :] = jnp.sum(ds, axis=-1, keepdims=True)
                dc_ref[0, :, lo:hi] -= jnp.sum(ds, axis=0, keepdims=True)
            if has_sink:
                sk = sink_ref[0, 0, 0:1, 0:1]
                dsink = dsink - jnp.sum(jnp.exp(sk - lse_i) * delta, axis=0, keepdims=True)
        if has_sink:
            dsink_ref[0, 0] = jnp.broadcast_to(dsink, (8, LANES))

    qspec = pl.BlockSpec((1, 1, S, DH), lambda g, r: (g, r, 0, 0))
    cspec = pl.BlockSpec((1, 1, S, 1), lambda g, r: (g, r, 0, 0))
    kspec = pl.BlockSpec((1, Sk, DH), lambda g, r: (g, 0, 0))
    ins, specs = [q, k, v, o, do, lse], [qspec, kspec, kspec, qspec, qspec, cspec]
    if has_bias:
        ins += [bias[0], bias[1]]
        specs += [pl.BlockSpec((1, S, 1), lambda g, r: (g, 0, 0)), pl.BlockSpec((1, 1, S), lambda g, r: (g, 0, 0))]
    if has_sink:
        ins.append(sinks)
        specs.append(pl.BlockSpec((1, 1, 8, LANES), lambda g, r: (g, r, 0, 0)))
    if has_sel:
        ins.append(selT)
        specs.append(pl.BlockSpec((1, LANES, S), lambda g, r: (g, 0, 0)))
    if has_gate:
        ins.append(gate)
        specs.append(cspec)
    names = ["dq", "dk", "dv"]
    outs = [jax.ShapeDtypeStruct((G, R, S, DH), F32), jax.ShapeDtypeStruct((G, Sk, DH), F32),
            jax.ShapeDtypeStruct((G, Sk, DH), F32)]
    ospecs = [qspec, kspec, kspec]
    if has_bias:
        assert R == 1
        names += ["dcrow", "dccol"]
        outs += [jax.ShapeDtypeStruct((G, 1, S), F32), jax.ShapeDtypeStruct((G, S, 1), F32)]
        ospecs += [pl.BlockSpec((1, 1, S), lambda g, r: (g, 0, 0)), pl.BlockSpec((1, S, 1), lambda g, r: (g, 0, 0))]
    if has_sink:
        names.append("dsink")
        outs.append(jax.ShapeDtypeStruct((G, R, 8, LANES), F32))
        ospecs.append(pl.BlockSpec((1, 1, 8, LANES), lambda g, r: (g, r, 0, 0)))
    if has_gate:
        names.append("dz")
        outs.append(jax.ShapeDtypeStruct((G, R, S, 1), F32))
        ospecs.append(cspec)
    res = _pcall(body, out_shape=tuple(outs), grid=(G, R), in_specs=specs, out_specs=tuple(ospecs), name=name,
                 jobs=ctx.take(name) if ctx else (), ins=ins)
    return dict(zip(names, res))


def _combine(o0, o1, o2, z, *, name):
    H, S, _ = o0.shape

    def body(o0_ref, o1_ref, o2_ref, z_ref, o_ref):
        acc = jax.nn.sigmoid(z_ref[0, 0]) * o0_ref[0]
        acc = acc + jax.nn.sigmoid(z_ref[1, 0]) * o1_ref[0]
        acc = acc + jax.nn.sigmoid(z_ref[2, 0]) * o2_ref[0]
        o_ref[0] = acc

    blk = pl.BlockSpec((1, S, DH), lambda h: (h, 0, 0))
    return pl.pallas_call(body, out_shape=jax.ShapeDtypeStruct((H, S, DH), F32), grid=(H,),
                          in_specs=[blk, blk, blk, pl.BlockSpec((3, 1, S, 1), lambda h: (0, h, 0, 0))], out_specs=blk,
                          compiler_params=_params(), name=name)(o0, o1, o2, z)


_GC = math.sqrt(2.0 / math.pi)


def _gelu(x):
    return 0.5 * x * (1.0 + jnp.tanh(_GC * (x + 0.044715 * x * x * x)))


def _gelu_grad(x):
    t = jnp.tanh(_GC * (x + 0.044715 * x * x * x))
    return 0.5 * (1.0 + t) + 0.5 * x * (1.0 - t * t) * _GC * (1.0 + 3.0 * 0.044715 * x * x)


def _make_xb(k):
    G, S, _ = k.shape
    chunks = k.reshape(G, S // 16, 16 * DH)
    shift = jnp.concatenate([chunks[:, 1:], jnp.zeros((G, 1, 16 * DH), k.dtype)], axis=1)
    return jnp.concatenate([chunks, shift], axis=-1)


def _unmake_xb(dxb, *, name):
    G, n, _ = dxb.shape
    a = dxb[..., :16 * DH]
    b = jnp.concatenate([jnp.zeros((G, 1, 16 * DH), F32), dxb[:, :-1, 16 * DH:]], axis=1)
    return _addn(a, b, name=name).reshape(G, n * 16, DH)


def _compress_fwd(xb, pe, w1, w2, *, name):
    G, n, W = xb.shape
    Hc = w1.shape[1]

    def body(xb_ref, pe_ref, w1_ref, w2_ref, kc_ref, hid_ref):
        xv = (xb_ref[0] + pe_ref[...]).astype(_CDT)
        hid = _dot(xv, w1_ref[...].astype(_CDT))
        hid_ref[0] = hid
        kc_ref[0] = _dot(_gelu(hid).astype(_CDT), w2_ref[...].astype(_CDT))

    return pl.pallas_call(
        body, out_shape=(jax.ShapeDtypeStruct((G, n, DH), F32), jax.ShapeDtypeStruct((G, n, Hc), F32)), grid=(G,),
        in_specs=[pl.BlockSpec((1, n, W), lambda g: (g, 0, 0)), pl.BlockSpec((1, W), lambda g: (0, 0)),
                  pl.BlockSpec((W, Hc), lambda g: (0, 0)), pl.BlockSpec((Hc, DH), lambda g: (0, 0))],
        out_specs=(pl.BlockSpec((1, n, DH), lambda g: (g, 0, 0)), pl.BlockSpec((1, n, Hc), lambda g: (g, 0, 0))),
        compiler_params=_params(), name=name)(xb, pe.reshape(1, W), w1, w2)


def _compress_bwd(xb, pe, w1, w2, hid, dkc, *, name):
    G, n, W = xb.shape
    Hc = w1.shape[1]

    def body(xb_ref, pe_ref, w1_ref, w2_ref, hid_ref, dkc_ref, dxb_ref, dw1_ref, dw2_ref, dpe_ref):
        @pl.when(pl.program_id(0) == 0)
        def _():
            dw1_ref[...] = jnp.zeros_like(dw1_ref)
            dw2_ref[...] = jnp.zeros_like(dw2_ref)
            dpe_ref[...] = jnp.zeros_like(dpe_ref)

        xv = (xb_ref[0] + pe_ref[...]).astype(_CDT)
        hid = hid_ref[0]
        dk = dkc_ref[0].astype(_CDT)
        dact = _dot_nt(dk, w2_ref[...].astype(_CDT))
        dhid = (dact * _gelu_grad(hid)).astype(_CDT)
        dw2_ref[...] += _dot_tn(_gelu(hid).astype(_CDT), dk)
        dxb = _dot_nt(dhid, w1_ref[...].astype(_CDT))
        dxb_ref[0] = dxb
        dw1_ref[...] += _dot_tn(xv, dhid)
        dpe_ref[...] += jnp.sum(dxb, axis=0, keepdims=True)

    return pl.pallas_call(
        body, out_shape=(jax.ShapeDtypeStruct((G, n, W), F32), jax.ShapeDtypeStruct((W, Hc), F32),
                         jax.ShapeDtypeStruct((Hc, DH), F32), jax.ShapeDtypeStruct((1, W), F32)), grid=(G,),
        in_specs=[pl.BlockSpec((1, n, W), lambda g: (g, 0, 0)), pl.BlockSpec((1, W), lambda g: (0, 0)),
                  pl.BlockSpec((W, Hc), lambda g: (0, 0)), pl.BlockSpec((Hc, DH), lambda g: (0, 0)),
                  pl.BlockSpec((1, n, Hc), lambda g: (g, 0, 0)), pl.BlockSpec((1, n, DH), lambda g: (g, 0, 0))],
        out_specs=(pl.BlockSpec((1, n, W), lambda g: (g, 0, 0)), pl.BlockSpec((W, Hc), lambda g: (0, 0)),
                   pl.BlockSpec((Hc, DH), lambda g: (0, 0)), pl.BlockSpec((1, W), lambda g: (0, 0))),
        compiler_params=_params(), name=name)(xb, pe.reshape(1, W), w1, w2, hid, dkc)


def _overlap_T(S):
    n_cmp, n_slc = S // 16 - 1, S // 64
    cs = np.arange(n_cmp) * 16
    ce = cs + 32
    ss = np.arange(n_slc) * 64
    se = ss + 64
    ov = np.clip(np.minimum(ce[:, None], se[None, :]) - np.maximum(cs[:, None], ss[None, :]), 0, None) / 32.0
    out = np.zeros((LANES, S // 16), np.float32)
    out[:n_slc, :n_cmp] = ov.T
    return jnp.asarray(out)


def _tri(n, upper):
    r = lax.broadcasted_iota(jnp.int32, (n, n), 0)
    c = lax.broadcasted_iota(jnp.int32, (n, n), 1)
    return ((c >= r) if upper else (c <= r)).astype(jnp.bfloat16)


def _fox_gate_fwd(zf, b, *, name):
    S, H = zf.shape
    nb = S // BQ

    def body(z_ref, b_ref, c_ref):
        tri = _tri(BQ, False)
        carry = jnp.zeros((1, H), F32)
        for i in range(nb):
            z = z_ref[i * BQ:(i + 1) * BQ, :] + b_ref[...]
            lf = jnp.minimum(z, 0.0) - jnp.log(1.0 + jnp.exp(-jnp.abs(z)))
            c_ref[i * BQ:(i + 1) * BQ, :] = _dot_exact_left(tri, lf) + carry
            carry = carry + jnp.sum(lf, axis=0, keepdims=True)

    return pl.pallas_call(body, out_shape=jax.ShapeDtypeStruct((S, H), F32), compiler_params=_params(),
                          name=name)(zf, b)


def _fox_gate_bwd(zf, b, dc, *, name):
    S, H = zf.shape
    nb = S // BQ

    def body(z_ref, b_ref, dc_ref, dz_ref, db_ref):
        tri = _tri(BQ, True)
        carry = jnp.zeros((1, H), F32)
        db = jnp.zeros((1, H), F32)
        for i in reversed(range(nb)):
            rows = slice(i * BQ, (i + 1) * BQ)
            dcb = dc_ref[rows, :]
            dlf = _dot_exact_left(tri, dcb) + carry
            carry = carry + jnp.sum(dcb, axis=0, keepdims=True)
            z = z_ref[rows, :] + b_ref[...]
            dz = dlf * jax.nn.sigmoid(-z)
            dz_ref[rows, :] = dz
            db = db + jnp.sum(dz, axis=0, keepdims=True)
        db_ref[...] = db

    return pl.pallas_call(body, out_shape=(jax.ShapeDtypeStruct((S, H), F32), jax.ShapeDtypeStruct((1, H), F32)),
                          compiler_params=_params(), name=name)(zf, b, dc)


def _to_heads(a):
    S = a.shape[0]
    return a.reshape(S, -1, DH).transpose(1, 0, 2)


def _from_heads(a):
    return a.transpose(1, 0, 2).reshape(a.shape[1], -1)


def _pad_lanes(a):
    return jnp.pad(a, ((0, 0), (0, LANES - a.shape[1])))


def _ffn_fwd(x, g, P, l, tag):
    h = _rms_fwd(x, g, name=tag + "_rms")
    gu = _mm(h, P.weight("gu", l), b_lead=("j",), tm=512, tn=1408, ctx=P.ctx, name=tag + "_gu")
    a = _swiglu_fwd(gu, ctx=P.ctx, name=tag + "_act")
    xo = _mm(a, P.weight("down", l), tm=512, tn=1024, res=x, alpha=0.5, ctx=P.ctx, name=tag + "_down")
    return xo, (x, h, gu, a)


def _ffn_bwd(dxo, saved, g, P, l, tag):
    x, h, gu, a = saved
    da = _mm(dxo, P.weight("down", l), tb=True, tm=512, tn=1408, alpha=0.5, ctx=P.ctx, name=tag + "_dact")
    dwd = _mm(a, dxo, ta=True, tm=1408, tn=1024, alpha=0.5, out_dtype=_WIRE, ctx=P.ctx, name=tag + "_dwd")
    P.grad("down", l, dwd.reshape(4, -1, D_MODEL))
    dgu = _swiglu_bwd(gu, da, ctx=P.ctx, name=tag + "_dgu")
    P.grad("gu", l, _mm(h, dgu, ta=True, tm=512, tn=1408, out_dtype=_WIRE, o_lead=("j",), ctx=P.ctx,
                        name=tag + "_dwgu"))
    dh = _mm(dgu, P.weight("gu", l), tb=True, b_lead=("k",), tm=512, tn=1024, ctx=P.ctx, name=tag + "_dh")
    dx, dg = _rms_bwd(dh, x, g, dxo, name=tag + "_drms")
    return dx, dg


def _nsa_fwd(x, g, w, cc, ss, tag):
    S = x.shape[0]
    h = _rms_fwd(x, g, name=tag + "_rms")
    proj = _mm(h, w["w_in"], tm=512, tn=896, ctx=w["ctx"], name=tag + "_in")
    q = _to_heads(proj[:, :1024])
    kv = proj[:, 1024:2560].reshape(S, 3, 2, NSA_G, DH)
    ks = [kv[:, b, 0].transpose(1, 0, 2) for b in range(3)]
    vs = [kv[:, b, 1].transpose(1, 0, 2) for b in range(3)]
    z = proj[:, 2560:NSA_IN].reshape(S, 3, N_HEADS).transpose(1, 2, 0)[..., None]
    roped = _rope(jnp.concatenate([q] + ks, axis=0), cc, ss, name=tag + "_rope")
    qr = roped[:N_HEADS].reshape(NSA_G, NSA_R, S, DH)
    kr = [roped[N_HEADS + 4 * b:N_HEADS + 4 * b + 4] for b in range(3)]
    xbk, xbv = _make_xb(kr[0]), _make_xb(vs[0])
    kc, hidk = _compress_fwd(xbk, w["ck_pe"], w["ck_w1"], w["ck_w2"], name=tag + "_ck")
    vc, hidv = _compress_fwd(xbv, w["cv_pe"], w["cv_w1"], w["cv_w2"], name=tag + "_cv")
    ctx = w["ctx"]
    o0, lse0, selT = _attn_fwd(qr, kc, vc, kind="cmp", ovT=_overlap_T(S), ctx=ctx, name=tag + "_cmp")
    o1, lse1 = _attn_fwd(qr, kr[1], vs[1], kind="sel", selT=selT, ctx=ctx, name=tag + "_slc")
    o2, lse2 = _attn_fwd(qr, kr[2], vs[2], kind="band", window=NSA_WINDOW, ctx=ctx, name=tag + "_win")
    o = _combine(o0.reshape(N_HEADS, S, DH), o1.reshape(N_HEADS, S, DH), o2.reshape(N_HEADS, S, DH), z,
                 name=tag + "_mix")
    of = _from_heads(o)
    xo = _mm(of, w["out"], tm=512, tn=1024, res=x, ctx=w["ctx"], name=tag + "_out")
    saved = (x, h, qr, kr, vs, z, xbk, xbv, hidk, hidv, kc, vc, (o0, o1, o2), (lse0, lse1, lse2), selT, of)
    return xo, saved


def _out_bwd(dxo, of, w, tag):
    dof = _mm(dxo, w["out"], tb=True, tm=512, tn=1024, ctx=w["ctx"], name=tag + "_dof")
    dw = _mm(of, dxo, ta=True, tm=512, tn=1024, out_dtype=_WIRE, ctx=w["ctx"], name=tag + "_dwout")
    w["P"].grad("out", w["out_l"], dw.reshape(4, -1, D_MODEL))
    return dof


def _split_cols(dw, n_in):
    cs = n_in // 4
    return dw[:, :n_in].reshape(D_MODEL, 4, cs).transpose(1, 0, 2).astype(_WIRE)


def _nsa_bwd(dxo, saved, g, w, cc, ss, tag):
    x, h, qr, kr, vs, z, xbk, xbv, hidk, hidv, kc, vc, os_, lses, selT, of = saved
    S = x.shape[0]
    dof = _out_bwd(dxo, of, w, tag)
    do = _to_heads(dof).reshape(NSA_G, NSA_R, S, DH)
    zg = z.reshape(3, NSA_G, NSA_R, S, 1)
    ctx = w["ctx"]
    b0 = _attn_bwd(qr, kc, vc, os_[0], do, lses[0], kind="cmp", gate=zg[0], ctx=ctx, name=tag + "_dcmp")
    b1 = _attn_bwd(qr, kr[1], vs[1], os_[1], do, lses[1], kind="sel", selT=selT, gate=zg[1], ctx=ctx,
                   name=tag + "_dslc")
    b2 = _attn_bwd(qr, kr[2], vs[2], os_[2], do, lses[2], kind="band", window=NSA_WINDOW, gate=zg[2], ctx=ctx,
                   name=tag + "_dwin")
    dxbk, dck_w1, dck_w2, dck_pe = _compress_bwd(xbk, w["ck_pe"], w["ck_w1"], w["ck_w2"], hidk, b0["dk"],
                                                 name=tag + "_dck")
    dxbv, dcv_w1, dcv_w2, dcv_pe = _compress_bwd(xbv, w["cv_pe"], w["cv_w1"], w["cv_w2"], hidv, b0["dv"],
                                                 name=tag + "_dcv")
    dk0 = _unmake_xb(dxbk, name=tag + "_dk0")
    dv0 = _unmake_xb(dxbv, name=tag + "_dv0")
    dq = _addn(b0["dq"], b1["dq"], b2["dq"], name=tag + "_dqsum").reshape(N_HEADS, S, DH)
    unroped = _rope(jnp.concatenate([dq, dk0, b1["dk"], b2["dk"]], axis=0), cc, -ss, name=tag + "_drope")
    dks = [unroped[N_HEADS + 4 * b:N_HEADS + 4 * b + 4] for b in range(3)]
    dvs = [dv0, b1["dv"], b2["dv"]]
    dkv = jnp.stack([jnp.stack([dks[b], dvs[b]], axis=0) for b in range(3)], axis=0)
    dkv = dkv.transpose(3, 0, 1, 2, 4).reshape(S, 3 * 2 * NSA_G * DH)
    dz = jnp.stack([b0["dz"], b1["dz"], b2["dz"]], axis=0).reshape(3 * N_HEADS, S).T
    dproj = jnp.concatenate([_from_heads(unroped[:N_HEADS]), dkv, dz, jnp.zeros((S, NSA_IN_PAD - NSA_IN), F32)], axis=1)
    dh = _mm(dproj, w["w_in"], tb=True, tm=512, tn=512, ctx=w["ctx"], name=tag + "_dh")
    dw_in = _mm(h, dproj, ta=True, tm=512, tn=896, ctx=w["ctx"], name=tag + "_dwin_w")
    dx, dg = _rms_bwd(dh, x, g, dxo, name=tag + "_drms")
    P, j = w["P"], w["j"]
    P.grad("nsa_in", j, _split_cols(dw_in, NSA_IN))
    P.grad("cw1", j, dck_w1.astype(_WIRE).reshape(4, -1, dck_w1.shape[1]))
    P.grad("cw1", 2 + j, dcv_w1.astype(_WIRE).reshape(4, -1, dcv_w1.shape[1]))
    grads = dict(ck_pe=dck_pe.reshape(32, DH), ck_w2=dck_w2, cv_pe=dcv_pe.reshape(32, DH), cv_w2=dcv_w2)
    return dx, dg, grads


def _swa_fwd(x, g, w, cc, ss, tag):
    S = x.shape[0]
    h = _rms_fwd(x, g, name=tag + "_rms")
    proj = _mm(h, w["w_in"], tm=512, tn=640, ctx=w["ctx"], name=tag + "_in")
    q = _to_heads(proj[:, :1024])
    kv = proj[:, 1024:].reshape(S, 2, SWA_G, DH)
    k, v = kv[:, 0].transpose(1, 0, 2), kv[:, 1].transpose(1, 0, 2)
    roped = _rope(jnp.concatenate([q, k], axis=0), cc, ss, name=tag + "_rope")
    qr = roped[:N_HEADS].reshape(SWA_G, SWA_R, S, DH)
    kr = roped[N_HEADS:]
    sinks = jnp.broadcast_to(w["sinks"].reshape(SWA_G, SWA_R, 1, 1), (SWA_G, SWA_R, 8, LANES))
    o, lse = _attn_fwd(qr, kr, v, kind="band", window=SWA_WINDOW, sinks=sinks, ctx=w["ctx"], name=tag + "_attn")
    of = _from_heads(o.reshape(N_HEADS, S, DH))
    xo = _mm(of, w["out"], tm=512, tn=1024, res=x, ctx=w["ctx"], name=tag + "_out")
    return xo, (x, h, qr, kr, v, sinks, o, lse, of)


def _swa_bwd(dxo, saved, g, w, cc, ss, tag):
    x, h, qr, kr, v, sinks, o, lse, of = saved
    S = x.shape[0]
    dof = _out_bwd(dxo, of, w, tag)
    do = _to_heads(dof).reshape(SWA_G, SWA_R, S, DH)
    b = _attn_bwd(qr, kr, v, o, do, lse, kind="band", window=SWA_WINDOW, sinks=sinks, ctx=w["ctx"], name=tag + "_dattn")
    unroped = _rope(jnp.concatenate([b["dq"].reshape(N_HEADS, S, DH), b["dk"]], axis=0), cc, -ss, name=tag + "_drope")
    dkv = jnp.stack([unroped[N_HEADS:], b["dv"]], axis=0).transpose(2, 0, 1, 3).reshape(S, 2 * SWA_G * DH)
    dproj = jnp.concatenate([_from_heads(unroped[:N_HEADS]), dkv], axis=1)
    dh = _mm(dproj, w["w_in"], tb=True, tm=512, tn=512, ctx=w["ctx"], name=tag + "_dh")
    dw_in = _mm(h, dproj, ta=True, tm=512, tn=640, ctx=w["ctx"], name=tag + "_dwin_w")
    dx, dg = _rms_bwd(dh, x, g, dxo, name=tag + "_drms")
    w["P"].grad("swa_in", w["j"], _split_cols(dw_in, SWA_IN))
    return dx, dg, dict(sinks=b["dsink"][:, :, 0, 0].reshape(N_HEADS))


def _fox_fwd(x, g, w, tag):
    S = x.shape[0]
    h = _rms_fwd(x, g, name=tag + "_rms")
    proj = _mm(h, w["w_in"], tm=512, tn=640, ctx=w["ctx"], name=tag + "_in")
    q, k, v = (_to_heads(proj[:, i * 1024:(i + 1) * 1024]) for i in range(3))
    zf = _pad_lanes(proj[:, 3072:FOX_IN])
    bf = _pad_lanes(w["b_f"].reshape(1, N_HEADS))
    c = _fox_gate_fwd(zf, bf, name=tag + "_gate")[:, :N_HEADS]
    bias = (c.T[:, :, None], c.T[:, None, :])
    q4 = q.reshape(N_HEADS, 1, S, DH)
    o, lse = _attn_fwd(q4, k, v, kind="causal", bias=bias, ctx=w["ctx"], name=tag + "_attn")
    of = _from_heads(o.reshape(N_HEADS, S, DH))
    xo = _mm(of, w["out"], tm=512, tn=1024, res=x, ctx=w["ctx"], name=tag + "_out")
    return xo, (x, h, q4, k, v, zf, bf, bias, o, lse, of)


def _fox_bwd(dxo, saved, g, w, tag):
    x, h, q4, k, v, zf, bf, bias, o, lse, of = saved
    S = x.shape[0]
    dof = _out_bwd(dxo, of, w, tag)
    do = _to_heads(dof).reshape(N_HEADS, 1, S, DH)
    b = _attn_bwd(q4, k, v, o, do, lse, kind="causal", bias=bias, ctx=w["ctx"], name=tag + "_dattn")
    dc = _addn(b["dcrow"].reshape(N_HEADS, S), b["dccol"].reshape(N_HEADS, S), name=tag + "_dc")
    dzf, db = _fox_gate_bwd(zf, bf, _pad_lanes(dc.T), name=tag + "_dgate")
    dproj = jnp.concatenate([_from_heads(b["dq"].reshape(N_HEADS, S, DH)), _from_heads(b["dk"]), _from_heads(b["dv"]),
                             dzf[:, :N_HEADS], jnp.zeros((S, FOX_IN_PAD - FOX_IN), F32)], axis=1)
    dh = _mm(dproj, w["w_in"], tb=True, tm=512, tn=512, ctx=w["ctx"], name=tag + "_dh")
    dw_in = _mm(h, dproj, ta=True, tm=512, tn=640, ctx=w["ctx"], name=tag + "_dwin_w")
    dx, dg = _rms_bwd(dh, x, g, dxo, name=tag + "_drms")
    w["P"].grad("fox_in", w["j"], _split_cols(dw_in, FOX_IN))
    return dx, dg, dict(b_f=db[0, :N_HEADS])


GROUPS = {
    "gu": (("ffn1_w_gu", "ffn2_w_gu"), 2),
    "down": (("ffn1_w_down", "ffn2_w_down"), 1),
    "out": (("nsa_w_out", "swa_w_out", "fox_w_out"), 1),
    "cw1": (("nsa_ck_w1", "nsa_cv_w1"), 1),
    "nsa_in": (("nsa_w_in",), 2),
    "swa_in": (("swa_w_in",), 2),
    "fox_in": (("fox_w_in",), 2),
}
OUT_SLAB = {0: 0, 3: 1, 1: 2, 2: 3}


def _pieces_in_order():
    chunks = []
    for i in range(DEPTH):
        kind, j = i % 3, i // 3
        chunks.append([("gu", i), ("down", i)])
        if kind == 0:
            chunks.append([("nsa_in", j), ("cw1", j), ("cw1", 2 + j), ("out", OUT_SLAB[i])])
        else:
            chunks.append([("swa_in" if kind == 1 else "fox_in", j), ("out", OUT_SLAB[i])])
        chunks.append([("gu", DEPTH + i), ("down", DEPTH + i)])
    return chunks


def _consumer_layout(group, F):
    _, rows, C = F.shape
    if group == "gu":
        return F
    if GROUPS[group][1] == 1:
        return F.reshape(4 * rows, C)
    w = F.transpose(1, 0, 2).reshape(rows, 4 * C)
    pad = {"nsa_in": NSA_IN_PAD, "swa_in": SWA_IN, "fox_in": FOX_IN_PAD}[group] - 4 * C
    return jnp.pad(w, ((0, 0), (0, pad)))


class _Given:
    def __init__(self, pieces, small):
        self.pieces, self.small, self.ctx, self.grads = pieces, small, None, {}

    def weight(self, group, l):
        return _consumer_layout(group, self.pieces[group, l])

    def grad(self, group, l, G):
        self.grads[group, l] = G


class _MixerWeights(dict):
    def __init__(self, P, pieces, **given):
        super().__init__(P=P, ctx=P.ctx, **given)
        self.pieces = pieces

    def __missing__(self, key):
        self[key] = self["P"].weight(*self.pieces[key])
        return self[key]


def _mixer_weights(P, i):
    kind, j = i % 3, i // 3
    out = {"out": ("out", OUT_SLAB[i])}
    if kind == 0:
        small = {k: P.small["nsa_" + k][j] for k in ("ck_pe", "ck_w2", "cv_pe", "cv_w2")}
        return _MixerWeights(P, dict(out, w_in=("nsa_in", j), ck_w1=("cw1", j), cv_w1=("cw1", 2 + j)), j=j,
                             out_l=OUT_SLAB[i], **small)
    if kind == 1:
        return _MixerWeights(P, dict(out, w_in=("swa_in", j)), j=j, out_l=OUT_SLAB[i], sinks=P.small["swa_sinks"][j])
    return _MixerWeights(P, dict(out, w_in=("fox_in", j)), j=j, out_l=OUT_SLAB[i], b_f=P.small["fox_b_f"][j])


def _local_step(x, tgt, P):
    S = x.shape[0]
    cc, ss = _rope_tables(S)
    sm = P.small
    saved = []
    for i in range(DEPTH):
        kind = i % 3
        x, s1 = _ffn_fwd(x, sm["ffn1_norm"][i], P, i, f"l{i}f1")
        mw = _mixer_weights(P, i)
        if kind == 0:
            x, s2 = _nsa_fwd(x, sm["mix_norm"][i], mw, cc, ss, f"l{i}nsa")
        elif kind == 1:
            x, s2 = _swa_fwd(x, sm["mix_norm"][i], mw, cc, ss, f"l{i}swa")
        else:
            x, s2 = _fox_fwd(x, sm["mix_norm"][i], mw, f"l{i}fox")
        x, s3 = _ffn_fwd(x, sm["ffn2_norm"][i], P, DEPTH + i, f"l{i}f2")
        saved.append((s1, mw, s2, s3))
    loss, dx, d_final = _loss_head(x, sm["final_norm"], tgt, name="loss_head")

    norms = {k: [None] * DEPTH for k in ("ffn1_norm", "mix_norm", "ffn2_norm")}
    mix = {}
    for i in reversed(range(DEPTH)):
        kind, j = i % 3, i // 3
        s1, mw, s2, s3 = saved[i]
        dx, norms["ffn2_norm"][i] = _ffn_bwd(dx, s3, sm["ffn2_norm"][i], P, DEPTH + i, f"l{i}f2")
        if kind == 0:
            dx, dg, gm = _nsa_bwd(dx, s2, sm["mix_norm"][i], mw, cc, ss, f"l{i}nsa")
            pre = "nsa_"
        elif kind == 1:
            dx, dg, gm = _swa_bwd(dx, s2, sm["mix_norm"][i], mw, cc, ss, f"l{i}swa")
            pre = "swa_"
        else:
            dx, dg, gm = _fox_bwd(dx, s2, sm["mix_norm"][i], mw, f"l{i}fox")
            pre = "fox_"
        norms["mix_norm"][i] = dg
        for k, val in gm.items():
            mix.setdefault(pre + k, {})[j] = val
        dx, norms["ffn1_norm"][i] = _ffn_bwd(dx, s1, sm["ffn1_norm"][i], P, i, f"l{i}f1")
    small = {k: jnp.stack(v, axis=0) for k, v in norms.items()}
    small.update({k: jnp.stack([d[j] for j in sorted(d)], axis=0) for k, d in mix.items()})
    small["final_norm"] = d_final
    return loss, dx, small


def _chip_peers():
    x, y, c = lax.axis_index("x"), lax.axis_index("y"), lax.axis_index("c")
    return x, y, c, [(1 - x, y), (x, 1 - y), (1 - x, 1 - y)]


_ANY = pl.BlockSpec(memory_space=pl.ANY)


def _gather_groups(shards):
    n = len(shards)

    def body(*refs):
        srcs, outs = refs[:n], refs[n:2 * n]
        ici_send, ici_recv, d2d_send, d2d_recv, local_sems = refs[2 * n:]
        x, y, c, chips = _chip_peers()
        me = 2 * x + y
        sibling = (x, y, 1 - c)
        local = [pltpu.make_async_copy(srcs[g], outs[g].at[:, me], local_sems.at[g]) for g in range(n)]
        for cp in local:
            cp.start()
        sends = [pltpu.make_async_remote_copy(src_ref=srcs[g].at[:, c], dst_ref=outs[g].at[:, me, c],
                                              send_sem=ici_send.at[g, j], recv_sem=ici_recv.at[g, j],
                                              device_id=(px, py, c), device_id_type=MESH)
                 for g in range(n) for j, (px, py) in enumerate(chips)]
        for cp in sends:
            cp.start()
        passed = []
        for g in range(n):
            for j, (px, py) in enumerate(chips):
                landed = outs[g].at[:, 2 * px + py, c]
                pltpu.make_async_remote_copy(src_ref=landed, dst_ref=landed, send_sem=ici_send.at[g, j],
                                             recv_sem=ici_recv.at[g, j], device_id=(px, py, c),
                                             device_id_type=MESH).wait_recv()
                cp = pltpu.make_async_remote_copy(src_ref=landed, dst_ref=landed, send_sem=d2d_send.at[g, j],
                                                  recv_sem=d2d_recv.at[g, j], device_id=sibling, device_id_type=MESH)
                cp.start()
                passed.append(cp)
        for g in range(n):
            for j, (px, py) in enumerate(chips):
                theirs = outs[g].at[:, 2 * px + py, 1 - c]
                pltpu.make_async_remote_copy(src_ref=theirs, dst_ref=theirs, send_sem=d2d_send.at[g, j],
                                             recv_sem=d2d_recv.at[g, j], device_id=sibling,
                                             device_id_type=MESH).wait_recv()
        for cp in sends + passed:
            cp.wait_send()
        for cp in local:
            cp.wait()

    out_shape = tuple(jax.ShapeDtypeStruct((s.shape[0], 4) + s.shape[1:], s.dtype) for s in shards)
    return pl.pallas_call(
        body, out_shape=out_shape, in_specs=[_ANY] * n, out_specs=(_ANY,) * n,
        scratch_shapes=[pltpu.SemaphoreType.DMA((n, 3)), pltpu.SemaphoreType.DMA((n, 3)),
                        pltpu.SemaphoreType.DMA((n, 3)), pltpu.SemaphoreType.DMA((n, 3)),
                        pltpu.SemaphoreType.DMA((n,))],
        compiler_params=pltpu.CompilerParams(has_side_effects=True), name="gather_weights")(*shards)


def _scatter_groups(sends):
    n = len(sends)

    def body(*refs):
        srcs, outs = refs[:n], refs[n:2 * n]
        send_sems, recv_sems, local_sems = refs[2 * n:]
        x, y, c, chips = _chip_peers()
        me = 2 * x + y
        local = [pltpu.make_async_copy(srcs[g].at[:, me], outs[g].at[me], local_sems.at[g]) for g in range(n)]
        for cp in local:
            cp.start()
        cps = [pltpu.make_async_remote_copy(src_ref=srcs[g].at[:, 2 * px + py], dst_ref=outs[g].at[me],
                                            send_sem=send_sems.at[g, j], recv_sem=recv_sems.at[g, j],
                                            device_id=(px, py, c), device_id_type=MESH)
               for g in range(n) for j, (px, py) in enumerate(chips)]
        for cp in cps:
            cp.start()
        for g in range(n):
            for j, (px, py) in enumerate(chips):
                slot = outs[g].at[2 * px + py]
                pltpu.make_async_remote_copy(src_ref=slot, dst_ref=slot, send_sem=send_sems.at[g, j],
                                             recv_sem=recv_sems.at[g, j], device_id=(px, py, c),
                                             device_id_type=MESH).wait_recv()
        for cp in cps:
            cp.wait_send()
        for cp in local:
            cp.wait()

    out_shape = tuple(jax.ShapeDtypeStruct((4, s.shape[0]) + s.shape[2:], s.dtype) for s in sends)
    return pl.pallas_call(
        body, out_shape=out_shape, in_specs=[_ANY] * n, out_specs=(_ANY,) * n,
        scratch_shapes=[pltpu.SemaphoreType.DMA((n, 3)), pltpu.SemaphoreType.DMA((n, 3)),
                        pltpu.SemaphoreType.DMA((n,))],
        compiler_params=pltpu.CompilerParams(has_side_effects=True), name="scatter_grads")(*sends)


def _sum_slots(recv, into, row0, *, name):
    _, R, C = recv.shape
    tr = _row_tile(math.gcd(R, row0), C)
    blk0 = row0 // tr

    def body(r_ref, _, o_ref):
        acc = r_ref[0].astype(F32) + r_ref[1].astype(F32)
        acc = acc + r_ref[2].astype(F32)
        o_ref[...] = acc + r_ref[3].astype(F32)

    return pl.pallas_call(body, out_shape=jax.ShapeDtypeStruct(into.shape, F32), grid=(R // tr,),
                          in_specs=[pl.BlockSpec((4, tr, C), lambda i: (0, i, 0)), _ANY],
                          out_specs=pl.BlockSpec((tr, C), lambda i: (blk0 + i, 0)), input_output_aliases={1: 0},
                          compiler_params=_params(), name=name)(recv, into)


def _swap_sibling(parts):
    n = len(parts)

    def body(*refs):
        srcs, outs = refs[:n], refs[n:2 * n]
        send_sems, recv_sems = refs[2 * n:]
        x, y, c = lax.axis_index("x"), lax.axis_index("y"), lax.axis_index("c")
        cps = [pltpu.make_async_remote_copy(src_ref=srcs[g], dst_ref=outs[g], send_sem=send_sems.at[g],
                                            recv_sem=recv_sems.at[g], device_id=(x, y, 1 - c), device_id_type=MESH)
               for g in range(n)]
        for cp in cps:
            cp.start()
        for cp in cps:
            cp.wait()

    return pl.pallas_call(
        body, out_shape=tuple(jax.ShapeDtypeStruct(p.shape, p.dtype) for p in parts),
        in_specs=[_ANY] * n, out_specs=(_ANY,) * n,
        scratch_shapes=[pltpu.SemaphoreType.DMA((n,)), pltpu.SemaphoreType.DMA((n,))],
        compiler_params=pltpu.CompilerParams(has_side_effects=True), name="swap_core_grads")(*parts)


def _flip(coord, bit):
    return 1 - coord if bit else coord


def _allreduce_small(v):
    n, C = v.shape

    def body(v_ref, o_ref, buf, send_sems, recv_sems):
        x, y, c = lax.axis_index("x"), lax.axis_index("y"), lax.axis_index("c")
        me = 4 * x + 2 * y + c
        buf[me] = v_ref[...]
        peers = [(_flip(x, (j >> 2) & 1), _flip(y, (j >> 1) & 1), _flip(c, j & 1)) for j in range(1, 8)]
        sends = [pltpu.make_async_remote_copy(src_ref=v_ref, dst_ref=buf.at[me], send_sem=send_sems.at[j],
                                              recv_sem=recv_sems.at[j], device_id=peer, device_id_type=MESH)
                 for j, peer in enumerate(peers)]
        for cp in sends:
            cp.start()
        for j, (px, py, pc) in enumerate(peers):
            pltpu.make_async_remote_copy(src_ref=v_ref, dst_ref=buf.at[4 * px + 2 * py + pc], send_sem=send_sems.at[j],
                                         recv_sem=recv_sems.at[j], device_id=(px, py, pc),
                                         device_id_type=MESH).wait_recv()
        for cp in sends:
            cp.wait_send()
        acc = buf[0]
        for d in range(1, 8):
            acc = acc + buf[d]
        o_ref[...] = acc

    return pl.pallas_call(
        body, out_shape=jax.ShapeDtypeStruct((n, C), F32),
        in_specs=[pl.BlockSpec(memory_space=pltpu.VMEM)], out_specs=pl.BlockSpec(memory_space=pltpu.VMEM),
        scratch_shapes=[pltpu.VMEM((8, n, C), F32), pltpu.SemaphoreType.DMA((7,)), pltpu.SemaphoreType.DMA((7,))],
        compiler_params=pltpu.CompilerParams(has_side_effects=True), name="allreduce_small")(v)


def _adamw(w, m, v, gs, *, row0=0, name):
    shape = w.shape
    w2, m2, v2 = _rows2d(w), _rows2d(m), _rows2d(v)
    g2 = [_rows2d(g) for g in gs]
    R, C = w2.shape
    tr = _row_tile(math.gcd(R, row0), C, budget=1024 * 1024)
    ng = len(g2)
    blk0 = row0 // tr

    def body(*refs):
        w_ref, m_ref, v_ref = refs[:3]
        g = refs[3][...]
        for r in refs[4:3 + ng]:
            g = g + r[...]
        g_ref, d_ref, nm_ref, nv_ref = refs[3 + ng:]
        mn = B1 * m_ref[...] + (1.0 - B1) * g
        vn = B2 * v_ref[...] + (1.0 - B2) * (g * g)
        m_hat = mn / (1.0 - B1 ** STEP)
        v_hat = vn / (1.0 - B2 ** STEP)
        g_ref[...] = g
        d_ref[...] = -LR * (m_hat / (jnp.sqrt(v_hat) + EPS) + WD * w_ref[...])
        nm_ref[...] = mn
        nv_ref[...] = vn

    spec = pl.BlockSpec((tr, C), lambda i: (i, 0))
    gspec = pl.BlockSpec((tr, C), lambda i: (blk0 + i, 0))
    outs = pl.pallas_call(body, out_shape=tuple(jax.ShapeDtypeStruct((R, C), F32) for _ in range(4)), grid=(R // tr,),
                          in_specs=[spec] * 3 + [gspec] * ng, out_specs=(spec,) * 4, compiler_params=_params(),
                          name=name)(w2, m2, v2, *g2)
    return tuple(o.reshape(shape) for o in outs)


def _small_layout(shapes):
    offs, off = {}, 0
    for k in REPLICATED:
        n = int(np.prod(shapes[k]))
        offs[k] = (off, n)
        off += -(-n // LANES) * LANES
    return offs, off


def _pack_small(d, shapes):
    offs, total = _small_layout(shapes)
    parts = []
    for k in REPLICATED:
        n = offs[k][1]
        parts.append(jnp.pad(d[k].reshape(-1).astype(F32), (0, -(-n // LANES) * LANES - n)))
    rows = -(-(total // LANES) // 8) * 8
    return jnp.pad(jnp.concatenate(parts), (0, rows * LANES - total)).reshape(rows, LANES)


def _unpack_small(a, shapes):
    offs, _ = _small_layout(shapes)
    flat = a.reshape(-1)
    return {k: flat[offs[k][0]:offs[k][0] + offs[k][1]].reshape(shapes[k]) for k in REPLICATED}


def _group_shards(w):
    shards = []
    for members, _ in GROUPS.values():
        s = jnp.concatenate([w[k].astype(_WIRE) for k in members], axis=0)
        shards.append(s.reshape(s.shape[0], 2, s.shape[1] // 2, s.shape[2]))
    return shards


class _Exchanged:
    def __init__(self, w):
        self.small = {k: w[k] for k in REPLICATED}
        self.ctx = _Sched()
        shards = dict(zip(GROUPS, _group_shards(w)))
        self.gather, self.cache, self.scatter = {}, {}, {}
        chunks = _pieces_in_order()
        first = {p: _gather_task1(shards[p[0]], p[1]) for chunk in chunks for p in chunk}
        self.gather = {p: _gather_task2(task) for p, task in first.items()}
        for n, chunk in enumerate(chunks):
            for p in chunk:
                self.ctx.push(first[p])
            for p in (chunks[n - 1] if n else []):
                self.ctx.push(self.gather[p])
        for p in chunks[-1]:
            self.ctx.push(self.gather[p])

    def weight(self, group, l):
        if (group, l) not in self.cache:
            task = self.gather[group, l]
            self.ctx.finish(task)
            F = task.result()
            self.cache[group, l] = _consumer_layout(group, F.reshape(4, -1, F.shape[-1]))
        return self.cache[group, l]

    def grad(self, group, l, G):
        self.scatter[group, l] = self.ctx.push(_scatter_task(G))

    def partial_sums(self, w):
        self.ctx.finish()
        parts = []
        for group, (members, _) in GROUPS.items():
            rows, C = w[members[0]].shape[1:]
            n = sum(w[k].shape[0] for k in members)
            part = jnp.zeros((n * rows, C), F32)
            for l in range(n):
                part = _sum_slots(self.scatter[group, l].result(), part, l * rows, name=f"sum_{group}{l}")
            parts.append(part)
        return parts


def _reduce_and_update(w, m, v, parts, small):
    others = _swap_sibling(parts)
    small_shapes = {k: w[k].shape for k in REPLICATED}
    g_small = _unpack_small(_allreduce_small(_pack_small(small, small_shapes)), small_shapes)

    out_g, out_d, out_m, out_v = {}, {}, {}, {}
    for (members, _), part, other in zip(GROUPS.values(), parts, others):
        row0 = 0
        for k in members:
            out_g[k], out_d[k], out_m[k], out_v[k] = _adamw(w[k], m[k], v[k], [part, other], row0=row0,
                                                            name="adamw_" + k)
            row0 += w[k].shape[0] * w[k].shape[1]
    sm = _adamw(_pack_small(w, small_shapes), _pack_small(m, small_shapes), _pack_small(v, small_shapes),
                [_pack_small(g_small, small_shapes)], name="adamw_small")
    for d, packed in zip((out_g, out_d, out_m, out_v), sm):
        d.update(_unpack_small(packed, small_shapes))
    return out_g, out_d, out_m, out_v


def kernel(x, ffn1_norm, ffn1_w_gu, ffn1_w_down, mix_norm, ffn2_norm, ffn2_w_gu, ffn2_w_down, nsa_w_in, nsa_ck_pe, nsa_ck_w1, nsa_ck_w2, nsa_cv_pe, nsa_cv_w1, nsa_cv_w2, nsa_w_out, swa_w_in, swa_sinks, swa_w_out, fox_w_in, fox_b_f, fox_w_out, final_norm, loss_target, m_ffn1_norm, m_ffn1_w_gu, m_ffn1_w_down, m_mix_norm, m_ffn2_norm, m_ffn2_w_gu, m_ffn2_w_down, m_nsa_w_in, m_nsa_ck_pe, m_nsa_ck_w1, m_nsa_ck_w2, m_nsa_cv_pe, m_nsa_cv_w1, m_nsa_cv_w2, m_nsa_w_out, m_swa_w_in, m_swa_sinks, m_swa_w_out, m_fox_w_in, m_fox_b_f, m_fox_w_out, m_final_norm, v_ffn1_norm, v_ffn1_w_gu, v_ffn1_w_down, v_mix_norm, v_ffn2_norm, v_ffn2_w_gu, v_ffn2_w_down, v_nsa_w_in, v_nsa_ck_pe, v_nsa_ck_w1, v_nsa_ck_w2, v_nsa_cv_pe, v_nsa_cv_w1, v_nsa_cv_w2, v_nsa_w_out, v_swa_w_in, v_swa_sinks, v_swa_w_out, v_fox_w_in, v_fox_b_f, v_fox_w_out, v_final_norm):
    args = dict(locals())
    w = {k: args[k] for k in WEIGHTS}
    m = {k: args["m_" + k] for k in WEIGHTS}
    v = {k: args["v_" + k] for k in WEIGHTS}
    P = _Exchanged(w)
    loss_part, dx, small = _local_step(x[0], loss_target[0], P)
    loss = lax.psum(loss_part, ("x", "y", "c"))
    out_g, out_d, out_m, out_v = _reduce_and_update(w, m, v, P.partial_sums(w), small)
    return (loss, dx[None], *[out_g[k] for k in WEIGHTS], *[out_d[k] for k in WEIGHTS],
            *[out_m[k] for k in WEIGHTS], *[out_v[k] for k in WEIGHTS])
```

```python
import math

import numpy as np
import jax
import jax.numpy as jnp
from jax import lax
from jax.experimental import pallas as pl
from jax.experimental.pallas import tpu as pltpu

F32 = jnp.float32
_CDT = jnp.bfloat16
_WIRE = jnp.bfloat16
_VMEM_LIMIT = 56 * 1024 * 1024

D_MODEL = 1024
DEPTH = 4
DH = 64
N_HEADS = 16
RMS_EPS = 1e-6
NEG = -1e30
SCALE = DH ** -0.5
BQ = 256
LANES = 128
NSA_G, NSA_R = 4, 4
NSA_WINDOW = 512
NSA_TOPK = 16
NSA_BONUS = 1e4
SWA_G, SWA_R = 2, 8
SWA_WINDOW = 128
NSA_IN, NSA_IN_PAD = 2608, 2688
SWA_IN = 1280
FOX_IN, FOX_IN_PAD = 3088, 3200
LR, B1, B2, EPS, WD, STEP = 0.001, 0.9, 0.999, 1e-08, 0.01, 10
MESH = pl.DeviceIdType.MESH

REPLICATED = ["ffn1_norm", "mix_norm", "ffn2_norm", "nsa_ck_pe", "nsa_ck_w2", "nsa_cv_pe", "nsa_cv_w2",
              "swa_sinks", "fox_b_f", "final_norm"]
WEIGHTS = ['ffn1_norm', 'ffn1_w_gu', 'ffn1_w_down', 'mix_norm', 'ffn2_norm', 'ffn2_w_gu', 'ffn2_w_down',
           'nsa_w_in', 'nsa_ck_pe', 'nsa_ck_w1', 'nsa_ck_w2', 'nsa_cv_pe', 'nsa_cv_w1', 'nsa_cv_w2', 'nsa_w_out',
           'swa_w_in', 'swa_sinks', 'swa_w_out', 'fox_w_in', 'fox_b_f', 'fox_w_out', 'final_norm']


def _params(**kw):
    return pltpu.CompilerParams(vmem_limit_bytes=_VMEM_LIMIT, **kw)


def _dot(a, b):
    return lax.dot_general(a, b, (((1,), (0,)), ((), ())), preferred_element_type=F32)


def _dot_nt(a, b):
    return lax.dot_general(a, b, (((1,), (1,)), ((), ())), preferred_element_type=F32)


def _dot_tn(a, b):
    return lax.dot_general(a, b, (((0,), (0,)), ((), ())), preferred_element_type=F32)


def _split3(x):
    hi = x.astype(jnp.bfloat16)
    r1 = x - hi.astype(F32)
    mid = r1.astype(jnp.bfloat16)
    lo = (r1 - mid.astype(F32)).astype(jnp.bfloat16)
    return hi, mid, lo


def _dot_exact(x, p):
    hi, mid, lo = _split3(x)
    return _dot(hi, p) + _dot(mid, p) + _dot(lo, p)


def _dot_exact_left(p, x):
    hi, mid, lo = _split3(x)
    return _dot(p, hi) + _dot(p, mid) + _dot(p, lo)


_ANY = pl.BlockSpec(memory_space=pl.ANY)
_COMM = pltpu.CompilerParams(has_side_effects=True)


def _chip_peers():
    x, y, c = lax.axis_index("x"), lax.axis_index("y"), lax.axis_index("c")
    return x, y, c, [(1 - x, y), (x, 1 - y), (1 - x, 1 - y)]


class _Job:
    def __init__(self, ins, outs, nsem, copies, cost, alias):
        self.ins, self.outs, self.nsem, self.copies, self.cost, self.alias = ins, outs, nsem, copies, cost, alias
        self.results = None

    def inputs(self):
        return self.ins


class _Task:
    def __init__(self, src, out, parts):
        self.src, self.out, self.parts, self.done, self.jobs = src, out, parts, 0, []

    def result(self):
        return self.jobs[-1].results[0]

    def complete(self):
        return self.done == len(self.parts) and self.jobs[-1].results is not None

    def ready(self):
        if isinstance(self.src, _Task) and not self.src.complete():
            return False
        return not self.jobs or self.jobs[-1].results is not None

    def next_job(self, n):
        parts = self.parts[self.done:self.done + n]
        in_place = isinstance(self.src, _Task)
        ins = [] if in_place else [self.src]
        prev = self.jobs[-1].results[0] if self.jobs else (self.src.result() if in_place else None)
        alias = {}
        if prev is not None:
            alias = {len(ins): 0}
            ins = ins + [prev]

        def copies(in_refs, out_refs, sems):
            src = out_refs[0] if in_place else in_refs[0]
            out, first = [], 0
            for build, nsem, _ in parts:
                out += build(src, out_refs[0], sems, first)
                first += nsem
            return out

        job = _Job(ins, [self.out], sum(p[1] for p in parts), copies, sum(p[2] for p in parts), alias)
        self.done += n
        self.jobs.append(job)
        return job


def _pcall(body, *, out_shape, in_specs, out_specs, name, grid=(), scratch_shapes=(), aliases=None, jobs=(), ins):
    single = not isinstance(out_shape, (tuple, list))
    out_shape = (out_shape,) if single else tuple(out_shape)
    out_specs = (out_specs,) if single else tuple(out_specs)
    n_in, n_out, n_scr = len(ins), len(out_shape), len(scratch_shapes)
    side_ins = [a for j in jobs for a in j.inputs()]
    side_outs = [o for j in jobs for o in j.outs]
    aliases = dict(aliases or {})
    i0, o0 = n_in, n_out
    for j in jobs:
        for a, b in j.alias.items():
            aliases[i0 + a] = o0 + b
        i0 += len(j.ins)
        o0 += len(j.outs)

    def body2(*refs):
        in_refs, s_in = refs[:n_in], refs[n_in:n_in + len(side_ins)]
        r = n_in + len(side_ins)
        out_refs, s_out = refs[r:r + n_out], refs[r + n_out:r + n_out + len(side_outs)]
        r += n_out + len(side_outs)
        scr, sems = refs[r:r + n_scr], refs[r + n_scr:]
        def descriptors():
            out, a, b = [], 0, 0
            for j, sem in zip(jobs, sems):
                out += j.copies(s_in[a:a + len(j.ins)], s_out[b:b + len(j.outs)], sem)
                a += len(j.ins)
                b += len(j.outs)
            return out

        def at_step(steps, fn):
            cond = None
            for d, s in enumerate(steps):
                hit = pl.program_id(d) == s
                cond = hit if cond is None else jnp.logical_and(cond, hit)
            if cond is None:
                fn()
            else:
                pl.when(cond)(fn)

        def start_all():
            for cp in descriptors():
                cp.start()

        def wait_all():
            for cp in descriptors():
                cp.wait()

        if jobs:
            at_step([0] * len(grid), start_all)
        body(*in_refs, *out_refs, *scr)
        if jobs:
            at_step([n - 1 for n in grid], wait_all)

    res = pl.pallas_call(
        body2, out_shape=out_shape + tuple(side_outs), grid=grid, in_specs=list(in_specs) + [_ANY] * len(side_ins),
        out_specs=out_specs + (_ANY,) * len(side_outs),
        scratch_shapes=list(scratch_shapes) + [pltpu.SemaphoreType.DMA((j.nsem,)) for j in jobs],
        input_output_aliases=aliases,
        compiler_params=_params(has_side_effects=True) if jobs else _params(), name=name)(*ins, *side_ins)
    b = n_out
    for j in jobs:
        j.results = list(res[b:b + len(j.outs)])
        b += len(j.outs)
    return res[0] if single else tuple(res[:n_out])


def _comm_call(jobs, name):
    def body():
        pass

    _pcall(body, out_shape=(), in_specs=[], out_specs=(), name=name, jobs=jobs, ins=[])


def _gather_task1(shard, l):
    _, _, h, C = shard.shape
    mb = 3 * h * C * jnp.dtype(_WIRE).itemsize / 1e6
    n = _row_splits(h, ICI_US_PER_MB * mb)
    hr = h // n

    def rows_to_peers(r):
        def build(src, dst, sems, s0):
            x, y, c, chips = _chip_peers()
            me, rows = 2 * x + y, pl.ds(r * hr, hr)
            cps = [pltpu.make_async_copy(src.at[l], dst.at[me], sems.at[s0 + 6])] if r == 0 else []
            for j, (px, py) in enumerate(chips):
                cps.append(pltpu.make_async_remote_copy(src_ref=src.at[l, c, rows], dst_ref=dst.at[me, c, rows],
                                                        send_sem=sems.at[s0 + 2 * j], recv_sem=sems.at[s0 + 2 * j + 1],
                                                        device_id=(px, py, c), device_id_type=MESH))
            return cps
        return build

    parts = [(rows_to_peers(r), 7 if r == 0 else 6, ICI_US_PER_MB * mb / n) for r in range(n)]
    return _Task(shard, jax.ShapeDtypeStruct((4, 2, h, C), shard.dtype), parts)


def _gather_task2(task1):
    _, _, h, C = task1.out.shape
    mb = h * C * jnp.dtype(_WIRE).itemsize / 1e6

    def forward(j):
        def build(_, dst, sems, s0):
            x, y, c, chips = _chip_peers()
            px, py = chips[j]
            landed = dst.at[2 * px + py, c]
            return [pltpu.make_async_remote_copy(src_ref=landed, dst_ref=landed, send_sem=sems.at[s0],
                                                 recv_sem=sems.at[s0 + 1], device_id=(x, y, 1 - c),
                                                 device_id_type=MESH)]
        return build

    return _Task(task1, task1.out, [(forward(j), 2, 1.0 + D2D_US_PER_MB * mb) for j in range(3)])


def _scatter_task(G):
    mb = 3 * G.shape[1] * G.shape[2] * jnp.dtype(_WIRE).itemsize / 1e6
    n = _row_splits(G.shape[1], ICI_US_PER_MB * mb)
    rr = G.shape[1] // n

    def rows_to_peers(r):
        def build(src, dst, sems, s0):
            x, y, c, chips = _chip_peers()
            rows = pl.ds(r * rr, rr)
            cps = [pltpu.make_async_copy(src.at[2 * x + y], dst.at[0], sems.at[s0 + 6])] if r == 0 else []
            for j, (px, py) in enumerate(chips):
                cps.append(pltpu.make_async_remote_copy(src_ref=src.at[2 * px + py, rows], dst_ref=dst.at[1 + j, rows],
                                                        send_sem=sems.at[s0 + 2 * j], recv_sem=sems.at[s0 + 2 * j + 1],
                                                        device_id=(px, py, c), device_id_type=MESH))
            return cps
        return build

    parts = [(rows_to_peers(r), 7 if r == 0 else 6, ICI_US_PER_MB * mb / n) for r in range(n)]
    return _Task(G, jax.ShapeDtypeStruct(G.shape, G.dtype), parts)


def _row_splits(rows, cost):
    n = 8
    while n > 1 and (rows % (16 * n) or cost / n < PART_US):
        n //= 2
    return n


CARRIER_US = {
    "f_gu": 44, "f_act": 20, "f_down": 20, "f_dact": 26, "f_dwd": 20, "f_dgu": 38, "f_dwgu": 42, "f_dh": 46,
    "nsa_in": 25, "nsa_cmp": 86, "nsa_slc": 122, "nsa_win": 74, "nsa_out": 13, "nsa_dof": 11, "nsa_dwout": 11,
    "nsa_dcmp": 58, "nsa_dslc": 195, "nsa_dwin": 108, "nsa_dh": 26, "nsa_dwin_w": 25,
    "swa_in": 16, "swa_attn": 74, "swa_out": 13, "swa_dof": 11, "swa_dwout": 11, "swa_dattn": 74, "swa_dh": 17,
    "swa_dwin_w": 18,
    "fox_in": 34, "fox_attn": 104, "fox_out": 13, "fox_dof": 11, "fox_dwout": 11, "fox_dattn": 161, "fox_dh": 30,
    "fox_dwin_w": 30,
}
ICI_US_PER_MB = 15.0
PART_US = 12.0
D2D_US_PER_MB = 2.3
LOCAL_US_PER_MB = 1.5


class _Sched:
    def __init__(self):
        self.queue, self.credit, self.n_flush = [], 0.0, 0

    def push(self, task):
        self.queue.append(task)
        return task

    def _jobs(self, fits):
        jobs = []
        while self.queue and self.queue[0].ready():
            task, n = self.queue[0], 0
            while task.done + n < len(task.parts) and fits(task.parts[task.done + n][2]):
                n += 1
            if n:
                jobs.append(task.next_job(n))
            if task.done < len(task.parts):
                break
            self.queue.pop(0)
        return jobs

    def take(self, name):
        kind = name[2:].split("_", 1)
        key = ("f" if kind[0] in ("f1", "f2") else kind[0]) + "_" + kind[1]
        self.credit = min(self.credit, 0.0) + CARRIER_US.get(key, 0.0)

        def fits(cost):
            if self.credit < 0.5 * cost:
                return False
            self.credit -= cost
            return True

        return self._jobs(fits)

    def finish(self, task=None):
        while self.queue and (task is None or not task.complete()):
            jobs = []
            while self.queue and self.queue[0].ready():
                head = self.queue.pop(0)
                jobs.append(head.next_job(len(head.parts) - head.done))
                if head is task:
                    break
            assert jobs, "the task at the head of the queue waits for one that was never queued"
            _comm_call(jobs, name=f"exchange_{self.n_flush}")
            self.n_flush += 1


def _mm(a, b, *, ta=False, tb=False, tm, tn, b_lead=(), into=None, o_lead=(), out_dtype=F32, res=None, alpha=1.0,
        ctx=None, name):
    M = a.shape[1] if ta else a.shape[0]
    K = a.shape[0] if ta else a.shape[1]
    bk, bn = (b.shape[-1], b.shape[-2]) if tb else (b.shape[-2], b.shape[-1])
    j_lead, k_lead = "j" in b_lead, "k" in b_lead
    N = bn * (b.shape[b_lead.index("j")] if j_lead else 1)
    nk = b.shape[b_lead.index("k")] if k_lead else 1
    tk = K // nk
    assert tk == bk, (name, K, nk, bk)
    tm = min(tm, M)
    tn = bn if j_lead else min(tn, N)
    assert M % tm == 0 and N % tn == 0, (name, M, N, tm, tn)
    nb, no = len(b_lead), len(o_lead)

    def pick(lead, j, k):
        return tuple(j if t == "j" else k if t == "k" else t for t in lead)

    a_spec = pl.BlockSpec((tk, tm), lambda i, j, k: (k, i)) if ta else pl.BlockSpec((tm, tk), lambda i, j, k: (i, k))
    if tb:
        b_spec = pl.BlockSpec((None,) * nb + (tn, tk),
                              lambda i, j, k: pick(b_lead, j, k) + (0 if j_lead else j, 0 if k_lead else k))
    else:
        b_spec = pl.BlockSpec((None,) * nb + (tk, tn),
                              lambda i, j, k: pick(b_lead, j, k) + (0 if k_lead else k, 0 if j_lead else j))
    r_spec = pl.BlockSpec((tm, tn), lambda i, j, k: (i, j))
    o_spec = pl.BlockSpec((None,) * no + (tm, tn), lambda i, j, k: pick(o_lead, j, k) + (i, 0 if "j" in o_lead else j))
    dn = (((0 if ta else 1,), (1 if tb else 0,)), ((), ()))
    has_res, has_into = res is not None, into is not None
    if has_into:
        out_dtype = into.dtype

    def body(*refs):
        a_ref, b_ref = refs[0], refs[1]
        r_ref = refs[2] if has_res else None
        o_ref = refs[2 + has_res + has_into]
        prod = lax.dot_general(a_ref[...].astype(_CDT), b_ref[...].astype(_CDT), dn, preferred_element_type=F32)

        def finish(acc):
            if alpha != 1.0:
                acc = acc * alpha
            if has_res:
                acc = r_ref[...] + acc
            o_ref[...] = acc.astype(out_dtype)

        if nk == 1:
            finish(prod)
        else:
            acc_ref = refs[-1]
            k = pl.program_id(2)

            @pl.when(k == 0)
            def _():
                acc_ref[...] = prod

            @pl.when(k != 0)
            def _():
                acc_ref[...] += prod

            @pl.when(k == nk - 1)
            def _():
                finish(acc_ref[...])

    ins, specs = [a, b], [a_spec, b_spec]
    if has_res:
        ins.append(res)
        specs.append(r_spec)
    aliases = {}
    if has_into:
        aliases = {len(ins): 0}
        ins.append(into)
        specs.append(_ANY)
        out_shape = jax.ShapeDtypeStruct(into.shape, into.dtype)
    elif o_lead == ("j",):
        out_shape = jax.ShapeDtypeStruct((N // tn, M, tn), out_dtype)
    else:
        assert not o_lead
        out_shape = jax.ShapeDtypeStruct((M, N), out_dtype)
    scratch = [pltpu.VMEM((tm, tn), F32)] if nk > 1 else []
    return _pcall(body, out_shape=out_shape, grid=(M // tm, N // tn, nk), in_specs=specs, out_specs=o_spec,
                  scratch_shapes=scratch, aliases=aliases, name=name, jobs=ctx.take(name) if ctx else (), ins=ins)


def _rows2d(a):
    return a.reshape(-1, a.shape[-1])


def _row_tile(rows, cols, itemsize=4, budget=2 * 1024 * 1024):
    t = rows
    while t % 2 == 0 and t * cols * itemsize > budget and (t // 2) % 8 == 0:
        t //= 2
    return t


def _addn(*xs, name):
    shape = xs[0].shape
    x2 = [_rows2d(x) for x in xs]
    R, C = x2[0].shape
    tr = _row_tile(R, C)

    def body(*refs):
        acc = refs[0][...]
        for r in refs[1:-1]:
            acc = acc + r[...]
        refs[-1][...] = acc

    spec = pl.BlockSpec((tr, C), lambda i: (i, 0))
    out = pl.pallas_call(body, out_shape=jax.ShapeDtypeStruct((R, C), F32), grid=(R // tr,),
                         in_specs=[spec] * len(x2), out_specs=spec, compiler_params=_params(), name=name)(*x2)
    return out.reshape(shape)


def _rms_fwd(x, g, *, name):
    S, D = x.shape
    tr = 256

    def body(x_ref, g_ref, h_ref):
        xv = x_ref[...]
        rstd = lax.rsqrt(jnp.mean(xv * xv, axis=-1, keepdims=True) + RMS_EPS)
        h_ref[...] = (xv * rstd * g_ref[...]).astype(_CDT)

    return pl.pallas_call(body, out_shape=jax.ShapeDtypeStruct((S, D), _CDT), grid=(S // tr,),
                          in_specs=[pl.BlockSpec((tr, D), lambda i: (i, 0)), pl.BlockSpec((1, D), lambda i: (0, 0))],
                          out_specs=pl.BlockSpec((tr, D), lambda i: (i, 0)), compiler_params=_params(),
                          name=name)(x, g.reshape(1, D))


def _rms_bwd(dh, x, g, dres, *, name):
    S, D = x.shape
    tr = 256

    def body(dh_ref, x_ref, g_ref, dres_ref, dx_ref, dg_ref):
        xv = x_ref[...]
        rstd = lax.rsqrt(jnp.mean(xv * xv, axis=-1, keepdims=True) + RMS_EPS)
        xhat = xv * rstd
        dhv = dh_ref[...]
        dxhat = dhv * g_ref[...]
        dx_ref[...] = dres_ref[...] + rstd * (dxhat - xhat * jnp.mean(dxhat * xhat, axis=-1, keepdims=True))

        @pl.when(pl.program_id(0) == 0)
        def _():
            dg_ref[...] = jnp.zeros_like(dg_ref)

        dg_ref[...] += jnp.sum(dhv * xhat, axis=0, keepdims=True)

    row = pl.BlockSpec((tr, D), lambda i: (i, 0))
    vec = pl.BlockSpec((1, D), lambda i: (0, 0))
    dx, dg = pl.pallas_call(body, out_shape=(jax.ShapeDtypeStruct((S, D), F32), jax.ShapeDtypeStruct((1, D), F32)),
                            grid=(S // tr,), in_specs=[row, row, vec, row], out_specs=(row, vec),
                            compiler_params=_params(), name=name)(dh, x, g.reshape(1, D), dres)
    return dx, dg.reshape(D)


def _swiglu_fwd(gu, *, ctx=None, name):
    S, W = gu.shape
    H = W // 2
    tr = 256

    def body(gu_ref, a_ref):
        g = gu_ref[:, :H]
        u = gu_ref[:, H:]
        a_ref[...] = (g * jax.nn.sigmoid(g) * u).astype(_CDT)

    return _pcall(body, out_shape=jax.ShapeDtypeStruct((S, H), _CDT), grid=(S // tr,),
                  in_specs=[pl.BlockSpec((tr, W), lambda i: (i, 0))], out_specs=pl.BlockSpec((tr, H), lambda i: (i, 0)),
                  name=name, jobs=ctx.take(name) if ctx else (), ins=[gu])


def _swiglu_bwd(gu, da, *, ctx=None, name):
    S, W = gu.shape
    H = W // 2
    tr = 256

    def body(gu_ref, da_ref, d_ref):
        g = gu_ref[:, :H]
        u = gu_ref[:, H:]
        dav = da_ref[...]
        sg = jax.nn.sigmoid(g)
        silu = g * sg
        d_ref[:, :H] = dav * u * (sg + silu * (1.0 - sg))
        d_ref[:, H:] = dav * silu

    return _pcall(body, out_shape=jax.ShapeDtypeStruct((S, W), F32), grid=(S // tr,),
                  in_specs=[pl.BlockSpec((tr, W), lambda i: (i, 0)), pl.BlockSpec((tr, H), lambda i: (i, 0))],
                  out_specs=pl.BlockSpec((tr, W), lambda i: (i, 0)), name=name,
                  jobs=ctx.take(name) if ctx else (), ins=[gu, da])


def _loss_head(x, g, tgt, *, name):
    S, D = x.shape
    tr = 256

    def body(x_ref, g_ref, t_ref, loss_ref, dx_ref, dg_ref):
        xv = x_ref[...]
        rstd = lax.rsqrt(jnp.mean(xv * xv, axis=-1, keepdims=True) + RMS_EPS)
        xhat = xv * rstd
        err = xhat * g_ref[...] - t_ref[...]
        part = 0.5 * jnp.sum(jnp.mean(err * err, axis=-1, keepdims=True), axis=0, keepdims=True)
        dy = err * (1.0 / D)
        dxhat = dy * g_ref[...]
        dx_ref[...] = rstd * (dxhat - xhat * jnp.mean(dxhat * xhat, axis=-1, keepdims=True))

        @pl.when(pl.program_id(0) == 0)
        def _():
            dg_ref[...] = jnp.zeros_like(dg_ref)
            loss_ref[...] = jnp.zeros_like(loss_ref)

        dg_ref[...] += jnp.sum(dy * xhat, axis=0, keepdims=True)
        loss_ref[...] += jnp.broadcast_to(part, loss_ref.shape)

    row = pl.BlockSpec((tr, D), lambda i: (i, 0))
    vec = pl.BlockSpec((1, D), lambda i: (0, 0))
    loss, dx, dg = pl.pallas_call(
        body, out_shape=(jax.ShapeDtypeStruct((8, LANES), F32), jax.ShapeDtypeStruct((S, D), F32),
                         jax.ShapeDtypeStruct((1, D), F32)),
        grid=(S // tr,), in_specs=[row, vec, row], out_specs=(pl.BlockSpec((8, LANES), lambda i: (0, 0)), row, vec),
        compiler_params=_params(), name=name)(x, g.reshape(1, D), tgt)
    return loss[0, 0], dx, dg.reshape(D)


def _rope_tables(S):
    inv = 10000.0 ** (-jnp.arange(0, DH, 2, dtype=F32) / DH)
    ang = jnp.arange(S, dtype=F32)[:, None] * inv[None, :]
    cos, sin = jnp.cos(ang), jnp.sin(ang)
    return jnp.concatenate([cos, cos], -1), jnp.concatenate([-sin, sin], -1)


def _swap_matrix():
    p = np.zeros((DH, DH), np.float32)
    for j in range(DH // 2):
        p[j + DH // 2, j] = 1.0
        p[j, j + DH // 2] = 1.0
    return jnp.asarray(p, jnp.bfloat16)


def _rope(x, cc, ss, *, name):
    n, S, _ = x.shape

    def body(x_ref, c_ref, s_ref, p_ref, o_ref):
        xv = x_ref[0]
        o_ref[0] = xv * c_ref[...] + _dot_exact(xv, p_ref[...]) * s_ref[...]

    tab = pl.BlockSpec((S, DH), lambda i: (0, 0))
    blk = pl.BlockSpec((1, S, DH), lambda i: (i, 0, 0))
    return pl.pallas_call(body, out_shape=jax.ShapeDtypeStruct(x.shape, F32), grid=(n,),
                          in_specs=[blk, tab, tab, pl.BlockSpec((DH, DH), lambda i: (0, 0))], out_specs=blk,
                          compiler_params=_params(), name=name)(x, cc, ss, _swap_matrix())


def _key_range(kind, i, window, Sk):
    if kind == "cmp":
        return 0, Sk
    hi = (i + 1) * BQ
    if kind == "band":
        return max(0, i * BQ - window), hi
    return 0, hi


def _attn_mask(kind, i, lo, hi, window):
    shape = (BQ, hi - lo)
    qpos = i * BQ + lax.broadcasted_iota(jnp.int32, shape, 0)
    kpos = lo + lax.broadcasted_iota(jnp.int32, shape, 1)
    if kind == "cmp":
        return kpos * 16 + 31 <= qpos
    mask = kpos <= qpos
    if kind == "band":
        mask = mask & (qpos - kpos < window)
    return mask


def _sel_expand(lo, hi):
    shape = (LANES, hi - lo)
    j = lax.broadcasted_iota(jnp.int32, shape, 0)
    key = lo + lax.broadcasted_iota(jnp.int32, shape, 1)
    return (jnp.right_shift(key, 6) == j).astype(_CDT)


def _scores(kind, i, lo, hi, window, qb, kb, ccol_ref, crow_ref, sel_ref):
    s = _dot_nt(qb, kb) * SCALE
    if ccol_ref is not None:
        s = s + ccol_ref[0, i * BQ:(i + 1) * BQ, :] - crow_ref[0, :, lo:hi]
    mask = _attn_mask(kind, i, lo, hi, window)
    if sel_ref is not None:
        chosen = _dot_tn(sel_ref[0, :, i * BQ:(i + 1) * BQ].astype(_CDT), _sel_expand(lo, hi))
        mask = mask & (chosen > 0.5)
    return jnp.where(mask, s, NEG), mask


def _attn_fwd(q, k, v, *, kind, window=0, bias=None, sinks=None, selT=None, ovT=None, ctx=None, name):
    G, R, S, _ = q.shape
    Sk = k.shape[1]
    nq = S // BQ
    n_slc = S // 64
    has_bias, has_sink, has_sel, is_cmp = bias is not None, sinks is not None, selT is not None, kind == "cmp"

    def body(*refs):
        it = iter(refs)
        q_ref, k_ref, v_ref = next(it), next(it), next(it)
        ccol_ref, crow_ref = (next(it), next(it)) if has_bias else (None, None)
        sink_ref = next(it) if has_sink else None
        sel_ref = next(it) if has_sel else None
        ov_ref = next(it) if is_cmp else None
        o_ref, lse_ref = next(it), next(it)
        selo_ref, imp_ref = (next(it), next(it)) if is_cmp else (None, None)
        r = pl.program_id(1)
        for i in range(nq):
            lo, hi = _key_range(kind, i, window, Sk)
            rows = slice(i * BQ, (i + 1) * BQ)
            qb = q_ref[0, 0, rows, :].astype(_CDT)
            kb = k_ref[0, lo:hi, :].astype(_CDT)
            vb = v_ref[0, lo:hi, :].astype(_CDT)
            s, mask = _scores(kind, i, lo, hi, window, qb, kb, ccol_ref, crow_ref, sel_ref)
            m = jnp.max(s, axis=-1, keepdims=True)
            if has_sink:
                sk = sink_ref[0, 0, 0:1, 0:1]
                m = jnp.maximum(m, sk)
            e = jnp.exp(s - m)
            if is_cmp:
                e = jnp.where(mask, e, 0.0)
            l = jnp.sum(e, axis=-1, keepdims=True)
            if has_sink:
                l = l + jnp.exp(sk - m)
            if is_cmp:
                l = jnp.where(l > 0.0, l, 1.0)
            p = e * (1.0 / l)
            o_ref[0, 0, rows, :] = _dot(p.astype(_CDT), vb)
            lse_ref[0, 0, rows, :] = m + jnp.log(l)
            if is_cmp:
                part = _dot_nt(ov_ref[...].astype(_CDT), p.astype(_CDT))

                @pl.when(r == 0)
                def _():
                    imp_ref[:, rows] = part

                @pl.when(r != 0)
                def _():
                    imp_ref[:, rows] += part

        if is_cmp:
            @pl.when(r == R - 1)
            def _():
                shape = (LANES, S)
                j = lax.broadcasted_iota(jnp.int32, shape, 0)
                tb = jnp.right_shift(lax.broadcasted_iota(jnp.int32, shape, 1), 6)
                forced = (j == 0) | (j == tb) | (j == tb - 1)
                imp = jnp.where(j > tb, NEG, jnp.where(forced, NSA_BONUS, imp_ref[...]))
                imp = jnp.where(j >= n_slc, -3e38, imp)
                imp_ref[...] = imp
                cnt = jnp.zeros(shape, F32)
                for jp in range(n_slc):
                    row = imp_ref[jp:jp + 1, :]
                    ahead = (row > imp) | ((row == imp) & (jp < j))
                    cnt = cnt + ahead.astype(F32)
                selo_ref[0] = (cnt < float(min(NSA_TOPK, n_slc))).astype(F32)

    qspec = pl.BlockSpec((1, 1, S, DH), lambda g, r: (g, r, 0, 0))
    kspec = pl.BlockSpec((1, Sk, DH), lambda g, r: (g, 0, 0))
    ins, specs = [q, k, v], [qspec, kspec, kspec]
    if has_bias:
        ins += [bias[0], bias[1]]
        specs += [pl.BlockSpec((1, S, 1), lambda g, r: (g, 0, 0)), pl.BlockSpec((1, 1, S), lambda g, r: (g, 0, 0))]
    if has_sink:
        ins.append(sinks)
        specs.append(pl.BlockSpec((1, 1, 8, LANES), lambda g, r: (g, r, 0, 0)))
    if has_sel:
        ins.append(selT)
        specs.append(pl.BlockSpec((1, LANES, S), lambda g, r: (g, 0, 0)))
    if is_cmp:
        ins.append(ovT)
        specs.append(pl.BlockSpec((LANES, Sk), lambda g, r: (0, 0)))
    outs = [jax.ShapeDtypeStruct((G, R, S, DH), F32), jax.ShapeDtypeStruct((G, R, S, 1), F32)]
    ospecs = [qspec, pl.BlockSpec((1, 1, S, 1), lambda g, r: (g, r, 0, 0))]
    scratch = []
    if is_cmp:
        outs.append(jax.ShapeDtypeStruct((G, LANES, S), F32))
        ospecs.append(pl.BlockSpec((1, LANES, S), lambda g, r: (g, 0, 0)))
        scratch.append(pltpu.VMEM((LANES, S), F32))
    return _pcall(body, out_shape=tuple(outs), grid=(G, R), in_specs=specs, out_specs=tuple(ospecs),
                  scratch_shapes=scratch, name=name, jobs=ctx.take(name) if ctx else (), ins=ins)


def _attn_bwd(q, k, v, o, do, lse, *, kind, window=0, bias=None, sinks=None, selT=None, gate=None, ctx=None, name):
    G, R, S, _ = q.shape
    Sk = k.shape[1]
    nq = S // BQ
    has_bias, has_sink, has_sel, has_gate = bias is not None, sinks is not None, selT is not None, gate is not None

    def body(*refs):
        it = iter(refs)
        q_ref, k_ref, v_ref, o_ref, do_ref, lse_ref = (next(it) for _ in range(6))
        ccol_ref, crow_ref = (next(it), next(it)) if has_bias else (None, None)
        sink_ref = next(it) if has_sink else None
        sel_ref = next(it) if has_sel else None
        z_ref = next(it) if has_gate else None
        dq_ref, dk_ref, dv_ref = next(it), next(it), next(it)
        dc_ref, dccol_ref = (next(it), next(it)) if has_bias else (None, None)
        dsink_ref = next(it) if has_sink else None
        dz_ref = next(it) if has_gate else None
        r = pl.program_id(1)

        @pl.when(r == 0)
        def _():
            dk_ref[...] = jnp.zeros_like(dk_ref)
            dv_ref[...] = jnp.zeros_like(dv_ref)
            if has_bias:
                dc_ref[...] = jnp.zeros_like(dc_ref)

        dsink = jnp.zeros((1, 1), F32)
        for i in range(nq):
            lo, hi = _key_range(kind, i, window, Sk)
            rows = slice(i * BQ, (i + 1) * BQ)
            qb = q_ref[0, 0, rows, :].astype(_CDT)
            kb = k_ref[0, lo:hi, :].astype(_CDT)
            vb = v_ref[0, lo:hi, :].astype(_CDT)
            s, mask = _scores(kind, i, lo, hi, window, qb, kb, ccol_ref, crow_ref, sel_ref)
            lse_i = lse_ref[0, 0, rows, :]
            p = jnp.where(mask, jnp.exp(s - lse_i), 0.0)
            dob = do_ref[0, 0, rows, :]
            if has_gate:
                od = jnp.sum(o_ref[0, 0, rows, :] * dob, axis=-1, keepdims=True)
                sg = jax.nn.sigmoid(z_ref[0, 0, rows, :])
                dob = dob * sg
                dz_ref[0, 0, rows, :] = od * sg * (1.0 - sg)
            dob = dob.astype(_CDT)
            dp = _dot_nt(dob, vb)
            delta = jnp.sum(p * dp, axis=-1, keepdims=True)
            ds = p * (dp - delta)
            dsb = ds.astype(_CDT)
            dq_ref[0, 0, rows, :] = _dot(dsb, kb) * SCALE
            dk_ref[0, lo:hi, :] += _dot_tn(dsb, qb) * SCALE
            dv_ref[0, lo:hi, :] += _dot_tn(p.astype(_CDT), dob)
            if has_bias:
                dccol_ref[0, rows, :] = jnp.sum(ds, axis=-1, keepdims=True)
                dc_ref[0, :, lo:hi] -= jnp.sum(ds, axis=0, keepdims=True)
            if has_sink:
                sk = sink_ref[0, 0, 0:1, 0:1]
                dsink = dsink - jnp.sum(jnp.exp(sk - lse_i) * delta, axis=0, keepdims=True)
        if has_sink:
            dsink_ref[0, 0] = jnp.broadcast_to(dsink, (8, LANES))

    qspec = pl.BlockSpec((1, 1, S, DH), lambda g, r: (g, r, 0, 0))
    cspec = pl.BlockSpec((1, 1, S, 1), lambda g, r: (g, r, 0, 0))
    kspec = pl.BlockSpec((1, Sk, DH), lambda g, r: (g, 0, 0))
    ins, specs = [q, k, v, o, do, lse], [qspec, kspec, kspec, qspec, qspec, cspec]
    if has_bias:
        ins += [bias[0], bias[1]]
        specs += [pl.BlockSpec((1, S, 1), lambda g, r: (g, 0, 0)), pl.BlockSpec((1, 1, S), lambda g, r: (g, 0, 0))]
    if has_sink:
        ins.append(sinks)
        specs.append(pl.BlockSpec((1, 1, 8, LANES), lambda g, r: (g, r, 0, 0)))
    if has_sel:
        ins.append(selT)
        specs.append(pl.BlockSpec((1, LANES, S), lambda g, r: (g, 0, 0)))
    if has_gate:
        ins.append(gate)
        specs.append(cspec)
    names = ["dq", "dk", "dv"]
    outs = [jax.ShapeDtypeStruct((G, R, S, DH), F32), jax.ShapeDtypeStruct((G, Sk, DH), F32),
            jax.ShapeDtypeStruct((G, Sk, DH), F32)]
    ospecs = [qspec, kspec, kspec]
    if has_bias:
        assert R == 1
        names += ["dcrow", "dccol"]
        outs += [jax.ShapeDtypeStruct((G, 1, S), F32), jax.ShapeDtypeStruct((G, S, 1), F32)]
        ospecs += [pl.BlockSpec((1, 1, S), lambda g, r: (g, 0, 0)), pl.BlockSpec((1, S, 1), lambda g, r: (g, 0, 0))]
    if has_sink:
        names.append("dsink")
        outs.append(jax.ShapeDtypeStruct((G, R, 8, LANES), F32))
        ospecs.append(pl.BlockSpec((1, 1, 8, LANES), lambda g, r: (g, r, 0, 0)))
    if has_gate:
        names.append("dz")
        outs.append(jax.ShapeDtypeStruct((G, R, S, 1), F32))
        ospecs.append(cspec)
    res = _pcall(body, out_shape=tuple(outs), grid=(G, R), in_specs=specs, out_specs=tuple(ospecs), name=name,
                 jobs=ctx.take(name) if ctx else (), ins=ins)
    return dict(zip(names, res))


def _combine(o0, o1, o2, z, *, name):
    H, S, _ = o0.shape

    def body(o0_ref, o1_ref, o2_ref, z_ref, o_ref):
        acc = jax.nn.sigmoid(z_ref[0, 0]) * o0_ref[0]
        acc = acc + jax.nn.sigmoid(z_ref[1, 0]) * o1_ref[0]
        acc = acc + jax.nn.sigmoid(z_ref[2, 0]) * o2_ref[0]
        o_ref[0] = acc

    blk = pl.BlockSpec((1, S, DH), lambda h: (h, 0, 0))
    return pl.pallas_call(body, out_shape=jax.ShapeDtypeStruct((H, S, DH), F32), grid=(H,),
                          in_specs=[blk, blk, blk, pl.BlockSpec((3, 1, S, 1), lambda h: (0, h, 0, 0))], out_specs=blk,
                          compiler_params=_params(), name=name)(o0, o1, o2, z)


_GC = math.sqrt(2.0 / math.pi)


def _gelu(x):
    return 0.5 * x * (1.0 + jnp.tanh(_GC * (x + 0.044715 * x * x * x)))


def _gelu_grad(x):
    t = jnp.tanh(_GC * (x + 0.044715 * x * x * x))
    return 0.5 * (1.0 + t) + 0.5 * x * (1.0 - t * t) * _GC * (1.0 + 3.0 * 0.044715 * x * x)


def _make_xb(k):
    G, S, _ = k.shape
    chunks = k.reshape(G, S // 16, 16 * DH)
    shift = jnp.concatenate([chunks[:, 1:], jnp.zeros((G, 1, 16 * DH), k.dtype)], axis=1)
    return jnp.concatenate([chunks, shift], axis=-1)


def _unmake_xb(dxb, *, name):
    G, n, _ = dxb.shape
    a = dxb[..., :16 * DH]
    b = jnp.concatenate([jnp.zeros((G, 1, 16 * DH), F32), dxb[:, :-1, 16 * DH:]], axis=1)
    return _addn(a, b, name=name).reshape(G, n * 16, DH)


def _compress_fwd(xb, pe, w1, w2, *, name):
    G, n, W = xb.shape
    Hc = w1.shape[1]

    def body(xb_ref, pe_ref, w1_ref, w2_ref, kc_ref, hid_ref):
        xv = (xb_ref[0] + pe_ref[...]).astype(_CDT)
        hid = _dot(xv, w1_ref[...].astype(_CDT))
        hid_ref[0] = hid
        kc_ref[0] = _dot(_gelu(hid).astype(_CDT), w2_ref[...].astype(_CDT))

    return pl.pallas_call(
        body, out_shape=(jax.ShapeDtypeStruct((G, n, DH), F32), jax.ShapeDtypeStruct((G, n, Hc), F32)), grid=(G,),
        in_specs=[pl.BlockSpec((1, n, W), lambda g: (g, 0, 0)), pl.BlockSpec((1, W), lambda g: (0, 0)),
                  pl.BlockSpec((W, Hc), lambda g: (0, 0)), pl.BlockSpec((Hc, DH), lambda g: (0, 0))],
        out_specs=(pl.BlockSpec((1, n, DH), lambda g: (g, 0, 0)), pl.BlockSpec((1, n, Hc), lambda g: (g, 0, 0))),
        compiler_params=_params(), name=name)(xb, pe.reshape(1, W), w1, w2)


def _compress_bwd(xb, pe, w1, w2, hid, dkc, *, name):
    G, n, W = xb.shape
    Hc = w1.shape[1]

    def body(xb_ref, pe_ref, w1_ref, w2_ref, hid_ref, dkc_ref, dxb_ref, dw1_ref, dw2_ref, dpe_ref):
        @pl.when(pl.program_id(0) == 0)
        def _():
            dw1_ref[...] = jnp.zeros_like(dw1_ref)
            dw2_ref[...] = jnp.zeros_like(dw2_ref)
            dpe_ref[...] = jnp.zeros_like(dpe_ref)

        xv = (xb_ref[0] + pe_ref[...]).astype(_CDT)
        hid = hid_ref[0]
        dk = dkc_ref[0].astype(_CDT)
        dact = _dot_nt(dk, w2_ref[...].astype(_CDT))
        dhid = (dact * _gelu_grad(hid)).astype(_CDT)
        dw2_ref[...] += _dot_tn(_gelu(hid).astype(_CDT), dk)
        dxb = _dot_nt(dhid, w1_ref[...].astype(_CDT))
        dxb_ref[0] = dxb
        dw1_ref[...] += _dot_tn(xv, dhid)
        dpe_ref[...] += jnp.sum(dxb, axis=0, keepdims=True)

    return pl.pallas_call(
        body, out_shape=(jax.ShapeDtypeStruct((G, n, W), F32), jax.ShapeDtypeStruct((W, Hc), F32),
                         jax.ShapeDtypeStruct((Hc, DH), F32), jax.ShapeDtypeStruct((1, W), F32)), grid=(G,),
        in_specs=[pl.BlockSpec((1, n, W), lambda g: (g, 0, 0)), pl.BlockSpec((1, W), lambda g: (0, 0)),
                  pl.BlockSpec((W, Hc), lambda g: (0, 0)), pl.BlockSpec((Hc, DH), lambda g: (0, 0)),
                  pl.BlockSpec((1, n, Hc), lambda g: (g, 0, 0)), pl.BlockSpec((1, n, DH), lambda g: (g, 0, 0))],
        out_specs=(pl.BlockSpec((1, n, W), lambda g: (g, 0, 0)), pl.BlockSpec((W, Hc), lambda g: (0, 0)),
                   pl.BlockSpec((Hc, DH), lambda g: (0, 0)), pl.BlockSpec((1, W), lambda g: (0, 0))),
        compiler_params=_params(), name=name)(xb, pe.reshape(1, W), w1, w2, hid, dkc)


def _overlap_T(S):
    n_cmp, n_slc = S // 16 - 1, S // 64
    cs = np.arange(n_cmp) * 16
    ce = cs + 32
    ss = np.arange(n_slc) * 64
    se = ss + 64
    ov = np.clip(np.minimum(ce[:, None], se[None, :]) - np.maximum(cs[:, None], ss[None, :]), 0, None) / 32.0
    out = np.zeros((LANES, S // 16), np.float32)
    out[:n_slc, :n_cmp] = ov.T
    return jnp.asarray(out)


def _tri(n, upper):
    r = lax.broadcasted_iota(jnp.int32, (n, n), 0)
    c = lax.broadcasted_iota(jnp.int32, (n, n), 1)
    return ((c >= r) if upper else (c <= r)).astype(jnp.bfloat16)


def _fox_gate_fwd(zf, b, *, name):
    S, H = zf.shape
    nb = S // BQ

    def body(z_ref, b_ref, c_ref):
        tri = _tri(BQ, False)
        carry = jnp.zeros((1, H), F32)
        for i in range(nb):
            z = z_ref[i * BQ:(i + 1) * BQ, :] + b_ref[...]
            lf = jnp.minimum(z, 0.0) - jnp.log(1.0 + jnp.exp(-jnp.abs(z)))
            c_ref[i * BQ:(i + 1) * BQ, :] = _dot_exact_left(tri, lf) + carry
            carry = carry + jnp.sum(lf, axis=0, keepdims=True)

    return pl.pallas_call(body, out_shape=jax.ShapeDtypeStruct((S, H), F32), compiler_params=_params(),
                          name=name)(zf, b)


def _fox_gate_bwd(zf, b, dc, *, name):
    S, H = zf.shape
    nb = S // BQ

    def body(z_ref, b_ref, dc_ref, dz_ref, db_ref):
        tri = _tri(BQ, True)
        carry = jnp.zeros((1, H), F32)
        db = jnp.zeros((1, H), F32)
        for i in reversed(range(nb)):
            rows = slice(i * BQ, (i + 1) * BQ)
            dcb = dc_ref[rows, :]
            dlf = _dot_exact_left(tri, dcb) + carry
            carry = carry + jnp.sum(dcb, axis=0, keepdims=True)
            z = z_ref[rows, :] + b_ref[...]
            dz = dlf * jax.nn.sigmoid(-z)
            dz_ref[rows, :] = dz
            db = db + jnp.sum(dz, axis=0, keepdims=True)
        db_ref[...] = db

    return pl.pallas_call(body, out_shape=(jax.ShapeDtypeStruct((S, H), F32), jax.ShapeDtypeStruct((1, H), F32)),
                          compiler_params=_params(), name=name)(zf, b, dc)


def _to_heads(a):
    S = a.shape[0]
    return a.reshape(S, -1, DH).transpose(1, 0, 2)


def _from_heads(a):
    return a.transpose(1, 0, 2).reshape(a.shape[1], -1)


def _pad_lanes(a):
    return jnp.pad(a, ((0, 0), (0, LANES - a.shape[1])))


def _ffn_fwd(x, g, P, l, tag):
    h = _rms_fwd(x, g, name=tag + "_rms")
    gu = _mm(h, P.weight("gu", l), b_lead=("j",), tm=512, tn=1408, ctx=P.ctx, name=tag + "_gu")
    a = _swiglu_fwd(gu, ctx=P.ctx, name=tag + "_act")
    xo = _mm(a, P.weight("down", l), tm=512, tn=1024, res=x, alpha=0.5, ctx=P.ctx, name=tag + "_down")
    return xo, (x, h, gu, a)


def _ffn_bwd(dxo, saved, g, P, l, tag):
    x, h, gu, a = saved
    da = _mm(dxo, P.weight("down", l), tb=True, tm=512, tn=1408, alpha=0.5, ctx=P.ctx, name=tag + "_dact")
    dwd = _mm(a, dxo, ta=True, tm=1408, tn=1024, alpha=0.5, out_dtype=_WIRE, ctx=P.ctx, name=tag + "_dwd")
    P.grad("down", l, dwd.reshape(4, -1, D_MODEL))
    dgu = _swiglu_bwd(gu, da, ctx=P.ctx, name=tag + "_dgu")
    P.grad("gu", l, _mm(h, dgu, ta=True, tm=512, tn=1408, out_dtype=_WIRE, o_lead=("j",), ctx=P.ctx,
                        name=tag + "_dwgu"))
    dh = _mm(dgu, P.weight("gu", l), tb=True, b_lead=("k",), tm=512, tn=1024, ctx=P.ctx, name=tag + "_dh")
    dx, dg = _rms_bwd(dh, x, g, dxo, name=tag + "_drms")
    return dx, dg


def _nsa_fwd(x, g, w, cc, ss, tag):
    S = x.shape[0]
    h = _rms_fwd(x, g, name=tag + "_rms")
    proj = _mm(h, w["w_in"], tm=512, tn=896, ctx=w["ctx"], name=tag + "_in")
    q = _to_heads(proj[:, :1024])
    kv = proj[:, 1024:2560].reshape(S, 3, 2, NSA_G, DH)
    ks = [kv[:, b, 0].transpose(1, 0, 2) for b in range(3)]
    vs = [kv[:, b, 1].transpose(1, 0, 2) for b in range(3)]
    z = proj[:, 2560:NSA_IN].reshape(S, 3, N_HEADS).transpose(1, 2, 0)[..., None]
    roped = _rope(jnp.concatenate([q] + ks, axis=0), cc, ss, name=tag + "_rope")
    qr = roped[:N_HEADS].reshape(NSA_G, NSA_R, S, DH)
    kr = [roped[N_HEADS + 4 * b:N_HEADS + 4 * b + 4] for b in range(3)]
    xbk, xbv = _make_xb(kr[0]), _make_xb(vs[0])
    kc, hidk = _compress_fwd(xbk, w["ck_pe"], w["ck_w1"], w["ck_w2"], name=tag + "_ck")
    vc, hidv = _compress_fwd(xbv, w["cv_pe"], w["cv_w1"], w["cv_w2"], name=tag + "_cv")
    ctx = w["ctx"]
    o0, lse0, selT = _attn_fwd(qr, kc, vc, kind="cmp", ovT=_overlap_T(S), ctx=ctx, name=tag + "_cmp")
    o1, lse1 = _attn_fwd(qr, kr[1], vs[1], kind="sel", selT=selT, ctx=ctx, name=tag + "_slc")
    o2, lse2 = _attn_fwd(qr, kr[2], vs[2], kind="band", window=NSA_WINDOW, ctx=ctx, name=tag + "_win")
    o = _combine(o0.reshape(N_HEADS, S, DH), o1.reshape(N_HEADS, S, DH), o2.reshape(N_HEADS, S, DH), z,
                 name=tag + "_mix")
    of = _from_heads(o)
    xo = _mm(of, w["out"], tm=512, tn=1024, res=x, ctx=w["ctx"], name=tag + "_out")
    saved = (x, h, qr, kr, vs, z, xbk, xbv, hidk, hidv, kc, vc, (o0, o1, o2), (lse0, lse1, lse2), selT, of)
    return xo, saved


def _out_bwd(dxo, of, w, tag):
    dof = _mm(dxo, w["out"], tb=True, tm=512, tn=1024, ctx=w["ctx"], name=tag + "_dof")
    dw = _mm(of, dxo, ta=True, tm=512, tn=1024, out_dtype=_WIRE, ctx=w["ctx"], name=tag + "_dwout")
    w["P"].grad("out", w["out_l"], dw.reshape(4, -1, D_MODEL))
    return dof


def _split_cols(dw, n_in):
    cs = n_in // 4
    return dw[:, :n_in].reshape(D_MODEL, 4, cs).transpose(1, 0, 2).astype(_WIRE)


def _nsa_bwd(dxo, saved, g, w, cc, ss, tag):
    x, h, qr, kr, vs, z, xbk, xbv, hidk, hidv, kc, vc, os_, lses, selT, of = saved
    S = x.shape[0]
    dof = _out_bwd(dxo, of, w, tag)
    do = _to_heads(dof).reshape(NSA_G, NSA_R, S, DH)
    zg = z.reshape(3, NSA_G, NSA_R, S, 1)
    ctx = w["ctx"]
    b0 = _attn_bwd(qr, kc, vc, os_[0], do, lses[0], kind="cmp", gate=zg[0], ctx=ctx, name=tag + "_dcmp")
    b1 = _attn_bwd(qr, kr[1], vs[1], os_[1], do, lses[1], kind="sel", selT=selT, gate=zg[1], ctx=ctx,
                   name=tag + "_dslc")
    b2 = _attn_bwd(qr, kr[2], vs[2], os_[2], do, lses[2], kind="band", window=NSA_WINDOW, gate=zg[2], ctx=ctx,
                   name=tag + "_dwin")
    dxbk, dck_w1, dck_w2, dck_pe = _compress_bwd(xbk, w["ck_pe"], w["ck_w1"], w["ck_w2"], hidk, b0["dk"],
                                                 name=tag + "_dck")
    dxbv, dcv_w1, dcv_w2, dcv_pe = _compress_bwd(xbv, w["cv_pe"], w["cv_w1"], w["cv_w2"], hidv, b0["dv"],
                                                 name=tag + "_dcv")
    dk0 = _unmake_xb(dxbk, name=tag + "_dk0")
    dv0 = _unmake_xb(dxbv, name=tag + "_dv0")
    dq = _addn(b0["dq"], b1["dq"], b2["dq"], name=tag + "_dqsum").reshape(N_HEADS, S, DH)
    unroped = _rope(jnp.concatenate([dq, dk0, b1["dk"], b2["dk"]], axis=0), cc, -ss, name=tag + "_drope")
    dks = [unroped[N_HEADS + 4 * b:N_HEADS + 4 * b + 4] for b in range(3)]
    dvs = [dv0, b1["dv"], b2["dv"]]
    dkv = jnp.stack([jnp.stack([dks[b], dvs[b]], axis=0) for b in range(3)], axis=0)
    dkv = dkv.transpose(3, 0, 1, 2, 4).reshape(S, 3 * 2 * NSA_G * DH)
    dz = jnp.stack([b0["dz"], b1["dz"], b2["dz"]], axis=0).reshape(3 * N_HEADS, S).T
    dproj = jnp.concatenate([_from_heads(unroped[:N_HEADS]), dkv, dz, jnp.zeros((S, NSA_IN_PAD - NSA_IN), F32)], axis=1)
    dh = _mm(dproj, w["w_in"], tb=True, tm=512, tn=512, ctx=w["ctx"], name=tag + "_dh")
    dw_in = _mm(h, dproj, ta=True, tm=512, tn=896, ctx=w["ctx"], name=tag + "_dwin_w")
    dx, dg = _rms_bwd(dh, x, g, dxo, name=tag + "_drms")
    P, j = w["P"], w["j"]
    P.grad("nsa_in", j, _split_cols(dw_in, NSA_IN))
    P.grad("cw1", j, dck_w1.astype(_WIRE).reshape(4, -1, dck_w1.shape[1]))
    P.grad("cw1", 2 + j, dcv_w1.astype(_WIRE).reshape(4, -1, dcv_w1.shape[1]))
    grads = dict(ck_pe=dck_pe.reshape(32, DH), ck_w2=dck_w2, cv_pe=dcv_pe.reshape(32, DH), cv_w2=dcv_w2)
    return dx, dg, grads


def _swa_fwd(x, g, w, cc, ss, tag):
    S = x.shape[0]
    h = _rms_fwd(x, g, name=tag + "_rms")
    proj = _mm(h, w["w_in"], tm=512, tn=640, ctx=w["ctx"], name=tag + "_in")
    q = _to_heads(proj[:, :1024])
    kv = proj[:, 1024:].reshape(S, 2, SWA_G, DH)
    k, v = kv[:, 0].transpose(1, 0, 2), kv[:, 1].transpose(1, 0, 2)
    roped = _rope(jnp.concatenate([q, k], axis=0), cc, ss, name=tag + "_rope")
    qr = roped[:N_HEADS].reshape(SWA_G, SWA_R, S, DH)
    kr = roped[N_HEADS:]
    sinks = jnp.broadcast_to(w["sinks"].reshape(SWA_G, SWA_R, 1, 1), (SWA_G, SWA_R, 8, LANES))
    o, lse = _attn_fwd(qr, kr, v, kind="band", window=SWA_WINDOW, sinks=sinks, ctx=w["ctx"], name=tag + "_attn")
    of = _from_heads(o.reshape(N_HEADS, S, DH))
    xo = _mm(of, w["out"], tm=512, tn=1024, res=x, ctx=w["ctx"], name=tag + "_out")
    return xo, (x, h, qr, kr, v, sinks, o, lse, of)


def _swa_bwd(dxo, saved, g, w, cc, ss, tag):
    x, h, qr, kr, v, sinks, o, lse, of = saved
    S = x.shape[0]
    dof = _out_bwd(dxo, of, w, tag)
    do = _to_heads(dof).reshape(SWA_G, SWA_R, S, DH)
    b = _attn_bwd(qr, kr, v, o, do, lse, kind="band", window=SWA_WINDOW, sinks=sinks, ctx=w["ctx"], name=tag + "_dattn")
    unroped = _rope(jnp.concatenate([b["dq"].reshape(N_HEADS, S, DH), b["dk"]], axis=0), cc, -ss, name=tag + "_drope")
    dkv = jnp.stack([unroped[N_HEADS:], b["dv"]], axis=0).transpose(2, 0, 1, 3).reshape(S, 2 * SWA_G * DH)
    dproj = jnp.concatenate([_from_heads(unroped[:N_HEADS]), dkv], axis=1)
    dh = _mm(dproj, w["w_in"], tb=True, tm=512, tn=512, ctx=w["ctx"], name=tag + "_dh")
    dw_in = _mm(h, dproj, ta=True, tm=512, tn=640, ctx=w["ctx"], name=tag + "_dwin_w")
    dx, dg = _rms_bwd(dh, x, g, dxo, name=tag + "_drms")
    w["P"].grad("swa_in", w["j"], _split_cols(dw_in, SWA_IN))
    return dx, dg, dict(sinks=b["dsink"][:, :, 0, 0].reshape(N_HEADS))


def _fox_fwd(x, g, w, tag):
    S = x.shape[0]
    h = _rms_fwd(x, g, name=tag + "_rms")
    proj = _mm(h, w["w_in"], tm=512, tn=640, ctx=w["ctx"], name=tag + "_in")
    q, k, v = (_to_heads(proj[:, i * 1024:(i + 1) * 1024]) for i in range(3))
    zf = _pad_lanes(proj[:, 3072:FOX_IN])
    bf = _pad_lanes(w["b_f"].reshape(1, N_HEADS))
    c = _fox_gate_fwd(zf, bf, name=tag + "_gate")[:, :N_HEADS]
    bias = (c.T[:, :, None], c.T[:, None, :])
    q4 = q.reshape(N_HEADS, 1, S, DH)
    o, lse = _attn_fwd(q4, k, v, kind="causal", bias=bias, ctx=w["ctx"], name=tag + "_attn")
    of = _from_heads(o.reshape(N_HEADS, S, DH))
    xo = _mm(of, w["out"], tm=512, tn=1024, res=x, ctx=w["ctx"], name=tag + "_out")
    return xo, (x, h, q4, k, v, zf, bf, bias, o, lse, of)


def _fox_bwd(dxo, saved, g, w, tag):
    x, h, q4, k, v, zf, bf, bias, o, lse, of = saved
    S = x.shape[0]
    dof = _out_bwd(dxo, of, w, tag)
    do = _to_heads(dof).reshape(N_HEADS, 1, S, DH)
    b = _attn_bwd(q4, k, v, o, do, lse, kind="causal", bias=bias, ctx=w["ctx"], name=tag + "_dattn")
    dc = _addn(b["dcrow"].reshape(N_HEADS, S), b["dccol"].reshape(N_HEADS, S), name=tag + "_dc")
    dzf, db = _fox_gate_bwd(zf, bf, _pad_lanes(dc.T), name=tag + "_dgate")
    dproj = jnp.concatenate([_from_heads(b["dq"].reshape(N_HEADS, S, DH)), _from_heads(b["dk"]), _from_heads(b["dv"]),
                             dzf[:, :N_HEADS], jnp.zeros((S, FOX_IN_PAD - FOX_IN), F32)], axis=1)
    dh = _mm(dproj, w["w_in"], tb=True, tm=512, tn=512, ctx=w["ctx"], name=tag + "_dh")
    dw_in = _mm(h, dproj, ta=True, tm=512, tn=640, ctx=w["ctx"], name=tag + "_dwin_w")
    dx, dg = _rms_bwd(dh, x, g, dxo, name=tag + "_drms")
    w["P"].grad("fox_in", w["j"], _split_cols(dw_in, FOX_IN))
    return dx, dg, dict(b_f=db[0, :N_HEADS])


GROUPS = {
    "gu": (("ffn1_w_gu", "ffn2_w_gu"), 2),
    "down": (("ffn1_w_down", "ffn2_w_down"), 1),
    "out": (("nsa_w_out", "swa_w_out", "fox_w_out"), 1),
    "cw1": (("nsa_ck_w1", "nsa_cv_w1"), 1),
    "nsa_in": (("nsa_w_in",), 2),
    "swa_in": (("swa_w_in",), 2),
    "fox_in": (("fox_w_in",), 2),
}
OUT_SLAB = {0: 0, 3: 1, 1: 2, 2: 3}


def _pieces_in_order():
    chunks = []
    for i in range(DEPTH):
        kind, j = i % 3, i // 3
        chunks.append([("gu", i), ("down", i)])
        if kind == 0:
            chunks.append([("nsa_in", j), ("cw1", j), ("cw1", 2 + j), ("out", OUT_SLAB[i])])
        else:
            chunks.append([("swa_in" if kind == 1 else "fox_in", j), ("out", OUT_SLAB[i])])
        chunks.append([("gu", DEPTH + i), ("down", DEPTH + i)])
    return chunks


def _consumer_layout(group, F):
    _, rows, C = F.shape
    if group == "gu":
        return F
    if GROUPS[group][1] == 1:
        return F.reshape(4 * rows, C)
    w = F.transpose(1, 0, 2).reshape(rows, 4 * C)
    pad = {"nsa_in": NSA_IN_PAD, "swa_in": SWA_IN, "fox_in": FOX_IN_PAD}[group] - 4 * C
    return jnp.pad(w, ((0, 0), (0, pad)))


class _Given:
    def __init__(self, pieces, small):
        self.pieces, self.small, self.ctx, self.grads = pieces, small, None, {}

    def weight(self, group, l):
        return _consumer_layout(group, self.pieces[group, l])

    def grad(self, group, l, G):
        self.grads[group, l] = G


class _MixerWeights(dict):
    def __init__(self, P, pieces, **given):
        super().__init__(P=P, ctx=P.ctx, **given)
        self.pieces = pieces

    def __missing__(self, key):
        self[key] = self["P"].weight(*self.pieces[key])
        return self[key]


def _mixer_weights(P, i):
    kind, j = i % 3, i // 3
    out = {"out": ("out", OUT_SLAB[i])}
    if kind == 0:
        small = {k: P.small["nsa_" + k][j] for k in ("ck_pe", "ck_w2", "cv_pe", "cv_w2")}
        return _MixerWeights(P, dict(out, w_in=("nsa_in", j), ck_w1=("cw1", j), cv_w1=("cw1", 2 + j)), j=j,
                             out_l=OUT_SLAB[i], **small)
    if kind == 1:
        return _MixerWeights(P, dict(out, w_in=("swa_in", j)), j=j, out_l=OUT_SLAB[i], sinks=P.small["swa_sinks"][j])
    return _MixerWeights(P, dict(out, w_in=("fox_in", j)), j=j, out_l=OUT_SLAB[i], b_f=P.small["fox_b_f"][j])


def _local_step(x, tgt, P):
    S = x.shape[0]
    cc, ss = _rope_tables(S)
    sm = P.small
    saved = []
    for i in range(DEPTH):
        kind = i % 3
        x, s1 = _ffn_fwd(x, sm["ffn1_norm"][i], P, i, f"l{i}f1")
        mw = _mixer_weights(P, i)
        if kind == 0:
            x, s2 = _nsa_fwd(x, sm["mix_norm"][i], mw, cc, ss, f"l{i}nsa")
        elif kind == 1:
            x, s2 = _swa_fwd(x, sm["mix_norm"][i], mw, cc, ss, f"l{i}swa")
        else:
            x, s2 = _fox_fwd(x, sm["mix_norm"][i], mw, f"l{i}fox")
        x, s3 = _ffn_fwd(x, sm["ffn2_norm"][i], P, DEPTH + i, f"l{i}f2")
        saved.append((s1, mw, s2, s3))
    loss, dx, d_final = _loss_head(x, sm["final_norm"], tgt, name="loss_head")

    norms = {k: [None] * DEPTH for k in ("ffn1_norm", "mix_norm", "ffn2_norm")}
    mix = {}
    for i in reversed(range(DEPTH)):
        kind, j = i % 3, i // 3
        s1, mw, s2, s3 = saved[i]
        dx, norms["ffn2_norm"][i] = _ffn_bwd(dx, s3, sm["ffn2_norm"][i], P, DEPTH + i, f"l{i}f2")
        if kind == 0:
            dx, dg, gm = _nsa_bwd(dx, s2, sm["mix_norm"][i], mw, cc, ss, f"l{i}nsa")
            pre = "nsa_"
        elif kind == 1:
            dx, dg, gm = _swa_bwd(dx, s2, sm["mix_norm"][i], mw, cc, ss, f"l{i}swa")
            pre = "swa_"
        else:
            dx, dg, gm = _fox_bwd(dx, s2, sm["mix_norm"][i], mw, f"l{i}fox")
            pre = "fox_"
        norms["mix_norm"][i] = dg
        for k, val in gm.items():
            mix.setdefault(pre + k, {})[j] = val
        dx, norms["ffn1_norm"][i] = _ffn_bwd(dx, s1, sm["ffn1_norm"][i], P, i, f"l{i}f1")
    small = {k: jnp.stack(v, axis=0) for k, v in norms.items()}
    small.update({k: jnp.stack([d[j] for j in sorted(d)], axis=0) for k, d in mix.items()})
    small["final_norm"] = d_final
    return loss, dx, small


def _chip_peers():
    x, y, c = lax.axis_index("x"), lax.axis_index("y"), lax.axis_index("c")
    return x, y, c, [(1 - x, y), (x, 1 - y), (1 - x, 1 - y)]


_ANY = pl.BlockSpec(memory_space=pl.ANY)


def _gather_groups(shards):
    n = len(shards)

    def body(*refs):
        srcs, outs = refs[:n], refs[n:2 * n]
        ici_send, ici_recv, d2d_send, d2d_recv, local_sems = refs[2 * n:]
        x, y, c, chips = _chip_peers()
        me = 2 * x + y
        sibling = (x, y, 1 - c)
        local = [pltpu.make_async_copy(srcs[g], outs[g].at[:, me], local_sems.at[g]) for g in range(n)]
        for cp in local:
            cp.start()
        sends = [pltpu.make_async_remote_copy(src_ref=srcs[g].at[:, c], dst_ref=outs[g].at[:, me, c],
                                              send_sem=ici_send.at[g, j], recv_sem=ici_recv.at[g, j],
                                              device_id=(px, py, c), device_id_type=MESH)
                 for g in range(n) for j, (px, py) in enumerate(chips)]
        for cp in sends:
            cp.start()
        passed = []
        for g in range(n):
            for j, (px, py) in enumerate(chips):
                landed = outs[g].at[:, 2 * px + py, c]
                pltpu.make_async_remote_copy(src_ref=landed, dst_ref=landed, send_sem=ici_send.at[g, j],
                                             recv_sem=ici_recv.at[g, j], device_id=(px, py, c),
                                             device_id_type=MESH).wait_recv()
                cp = pltpu.make_async_remote_copy(src_ref=landed, dst_ref=landed, send_sem=d2d_send.at[g, j],
                                                  recv_sem=d2d_recv.at[g, j], device_id=sibling, device_id_type=MESH)
                cp.start()
                passed.append(cp)
        for g in range(n):
            for j, (px, py) in enumerate(chips):
                theirs = outs[g].at[:, 2 * px + py, 1 - c]
                pltpu.make_async_remote_copy(src_ref=theirs, dst_ref=theirs, send_sem=d2d_send.at[g, j],
                                             recv_sem=d2d_recv.at[g, j], device_id=sibling,
                                             device_id_type=MESH).wait_recv()
        for cp in sends + passed:
            cp.wait_send()
        for cp in local:
            cp.wait()

    out_shape = tuple(jax.ShapeDtypeStruct((s.shape[0], 4) + s.shape[1:], s.dtype) for s in shards)
    return pl.pallas_call(
        body, out_shape=out_shape, in_specs=[_ANY] * n, out_specs=(_ANY,) * n,
        scratch_shapes=[pltpu.SemaphoreType.DMA((n, 3)), pltpu.SemaphoreType.DMA((n, 3)),
                        pltpu.SemaphoreType.DMA((n, 3)), pltpu.SemaphoreType.DMA((n, 3)),
                        pltpu.SemaphoreType.DMA((n,))],
        compiler_params=pltpu.CompilerParams(has_side_effects=True), name="gather_weights")(*shards)


def _scatter_groups(sends):
    n = len(sends)

    def body(*refs):
        srcs, outs = refs[:n], refs[n:2 * n]
        send_sems, recv_sems, local_sems = refs[2 * n:]
        x, y, c, chips = _chip_peers()
        me = 2 * x + y
        local = [pltpu.make_async_copy(srcs[g].at[:, me], outs[g].at[me], local_sems.at[g]) for g in range(n)]
        for cp in local:
            cp.start()
        cps = [pltpu.make_async_remote_copy(src_ref=srcs[g].at[:, 2 * px + py], dst_ref=outs[g].at[me],
                                            send_sem=send_sems.at[g, j], recv_sem=recv_sems.at[g, j],
                                            device_id=(px, py, c), device_id_type=MESH)
               for g in range(n) for j, (px, py) in enumerate(chips)]
        for cp in cps:
            cp.start()
        for g in range(n):
            for j, (px, py) in enumerate(chips):
                slot = outs[g].at[2 * px + py]
                pltpu.make_async_remote_copy(src_ref=slot, dst_ref=slot, send_sem=send_sems.at[g, j],
                                             recv_sem=recv_sems.at[g, j], device_id=(px, py, c),
                                             device_id_type=MESH).wait_recv()
        for cp in cps:
            cp.wait_send()
        for cp in local:
            cp.wait()

    out_shape = tuple(jax.ShapeDtypeStruct((4, s.shape[0]) + s.shape[2:], s.dtype) for s in sends)
    return pl.pallas_call(
        body, out_shape=out_shape, in_specs=[_ANY] * n, out_specs=(_ANY,) * n,
        scratch_shapes=[pltpu.SemaphoreType.DMA((n, 3)), pltpu.SemaphoreType.DMA((n, 3)),
                        pltpu.SemaphoreType.DMA((n,))],
        compiler_params=pltpu.CompilerParams(has_side_effects=True), name="scatter_grads")(*sends)


def _sum_slots(recv, into, row0, *, name):
    _, R, C = recv.shape
    tr = _row_tile(math.gcd(R, row0), C)
    blk0 = row0 // tr

    def body(r_ref, _, o_ref):
        acc = r_ref[0].astype(F32) + r_ref[1].astype(F32)
        acc = acc + r_ref[2].astype(F32)
        o_ref[...] = acc + r_ref[3].astype(F32)

    return pl.pallas_call(body, out_shape=jax.ShapeDtypeStruct(into.shape, F32), grid=(R // tr,),
                          in_specs=[pl.BlockSpec((4, tr, C), lambda i: (0, i, 0)), _ANY],
                          out_specs=pl.BlockSpec((tr, C), lambda i: (blk0 + i, 0)), input_output_aliases={1: 0},
                          compiler_params=_params(), name=name)(recv, into)


def _swap_sibling(parts):
    n = len(parts)

    def body(*refs):
        srcs, outs = refs[:n], refs[n:2 * n]
        send_sems, recv_sems = refs[2 * n:]
        x, y, c = lax.axis_index("x"), lax.axis_index("y"), lax.axis_index("c")
        cps = [pltpu.make_async_remote_copy(src_ref=srcs[g], dst_ref=outs[g], send_sem=send_sems.at[g],
                                            recv_sem=recv_sems.at[g], device_id=(x, y, 1 - c), device_id_type=MESH)
               for g in range(n)]
        for cp in cps:
            cp.start()
        for cp in cps:
            cp.wait()

    return pl.pallas_call(
        body, out_shape=tuple(jax.ShapeDtypeStruct(p.shape, p.dtype) for p in parts),
        in_specs=[_ANY] * n, out_specs=(_ANY,) * n,
        scratch_shapes=[pltpu.SemaphoreType.DMA((n,)), pltpu.SemaphoreType.DMA((n,))],
        compiler_params=pltpu.CompilerParams(has_side_effects=True), name="swap_core_grads")(*parts)


def _flip(coord, bit):
    return 1 - coord if bit else coord


def _allreduce_small(v):
    n, C = v.shape

    def body(v_ref, o_ref, buf, send_sems, recv_sems):
        x, y, c = lax.axis_index("x"), lax.axis_index("y"), lax.axis_index("c")
        me = 4 * x + 2 * y + c
        buf[me] = v_ref[...]
        peers = [(_flip(x, (j >> 2) & 1), _flip(y, (j >> 1) & 1), _flip(c, j & 1)) for j in range(1, 8)]
        sends = [pltpu.make_async_remote_copy(src_ref=v_ref, dst_ref=buf.at[me], send_sem=send_sems.at[j],
                                              recv_sem=recv_sems.at[j], device_id=peer, device_id_type=MESH)
                 for j, peer in enumerate(peers)]
        for cp in sends:
            cp.start()
        for j, (px, py, pc) in enumerate(peers):
            pltpu.make_async_remote_copy(src_ref=v_ref, dst_ref=buf.at[4 * px + 2 * py + pc], send_sem=send_sems.at[j],
                                         recv_sem=recv_sems.at[j], device_id=(px, py, pc),
                                         device_id_type=MESH).wait_recv()
        for cp in sends:
            cp.wait_send()
        acc = buf[0]
        for d in range(1, 8):
            acc = acc + buf[d]
        o_ref[...] = acc

    return pl.pallas_call(
        body, out_shape=jax.ShapeDtypeStruct((n, C), F32),
        in_specs=[pl.BlockSpec(memory_space=pltpu.VMEM)], out_specs=pl.BlockSpec(memory_space=pltpu.VMEM),
        scratch_shapes=[pltpu.VMEM((8, n, C), F32), pltpu.SemaphoreType.DMA((7,)), pltpu.SemaphoreType.DMA((7,))],
        compiler_params=pltpu.CompilerParams(has_side_effects=True), name="allreduce_small")(v)


def _adamw(w, m, v, gs, *, row0=0, name):
    shape = w.shape
    w2, m2, v2 = _rows2d(w), _rows2d(m), _rows2d(v)
    g2 = [_rows2d(g) for g in gs]
    R, C = w2.shape
    tr = _row_tile(math.gcd(R, row0), C, budget=1024 * 1024)
    ng = len(g2)
    blk0 = row0 // tr

    def body(*refs):
        w_ref, m_ref, v_ref = refs[:3]
        g = refs[3][...]
        for r in refs[4:3 + ng]:
            g = g + r[...]
        g_ref, d_ref, nm_ref, nv_ref = refs[3 + ng:]
        mn = B1 * m_ref[...] + (1.0 - B1) * g
        vn = B2 * v_ref[...] + (1.0 - B2) * (g * g)
        m_hat = mn / (1.0 - B1 ** STEP)
        v_hat = vn / (1.0 - B2 ** STEP)
        g_ref[...] = g
        d_ref[...] = -LR * (m_hat / (jnp.sqrt(v_hat) + EPS) + WD * w_ref[...])
        nm_ref[...] = mn
        nv_ref[...] = vn

    spec = pl.BlockSpec((tr, C), lambda i: (i, 0))
    gspec = pl.BlockSpec((tr, C), lambda i: (blk0 + i, 0))
    outs = pl.pallas_call(body, out_shape=tuple(jax.ShapeDtypeStruct((R, C), F32) for _ in range(4)), grid=(R // tr,),
                          in_specs=[spec] * 3 + [gspec] * ng, out_specs=(spec,) * 4, compiler_params=_params(),
                          name=name)(w2, m2, v2, *g2)
    return tuple(o.reshape(shape) for o in outs)


def _small_layout(shapes):
    offs, off = {}, 0
    for k in REPLICATED:
        n = int(np.prod(shapes[k]))
        offs[k] = (off, n)
        off += -(-n // LANES) * LANES
    return offs, off


def _pack_small(d, shapes):
    offs, total = _small_layout(shapes)
    parts = []
    for k in REPLICATED:
        n = offs[k][1]
        parts.append(jnp.pad(d[k].reshape(-1).astype(F32), (0, -(-n // LANES) * LANES - n)))
    rows = -(-(total // LANES) // 8) * 8
    return jnp.pad(jnp.concatenate(parts), (0, rows * LANES - total)).reshape(rows, LANES)


def _unpack_small(a, shapes):
    offs, _ = _small_layout(shapes)
    flat = a.reshape(-1)
    return {k: flat[offs[k][0]:offs[k][0] + offs[k][1]].reshape(shapes[k]) for k in REPLICATED}


def _group_shards(w):
    shards = []
    for members, _ in GROUPS.values():
        s = jnp.concatenate([w[k].astype(_WIRE) for k in members], axis=0)
        shards.append(s.reshape(s.shape[0], 2, s.shape[1] // 2, s.shape[2]))
    return shards


class _Exchanged:
    def __init__(self, w):
        self.small = {k: w[k] for k in REPLICATED}
        self.ctx = _Sched()
        shards = dict(zip(GROUPS, _group_shards(w)))
        self.gather, self.cache, self.scatter = {}, {}, {}
        chunks = _pieces_in_order()
        first = {p: _gather_task1(shards[p[0]], p[1]) for chunk in chunks for p in chunk}
        self.gather = {p: _gather_task2(task) for p, task in first.items()}
        for n, chunk in enumerate(chunks):
            for p in chunk:
                self.ctx.push(first[p])
            for p in (chunks[n - 1] if n else []):
                self.ctx.push(self.gather[p])
        for p in chunks[-1]:
            self.ctx.push(self.gather[p])

    def weight(self, group, l):
        if (group, l) not in self.cache:
            task = self.gather[group, l]
            self.ctx.finish(task)
            F = task.result()
            self.cache[group, l] = _consumer_layout(group, F.reshape(4, -1, F.shape[-1]))
        return self.cache[group, l]

    def grad(self, group, l, G):
        self.scatter[group, l] = self.ctx.push(_scatter_task(G))

    def partial_sums(self, w):
        self.ctx.finish()
        parts = []
        for group, (members, _) in GROUPS.items():
            rows, C = w[members[0]].shape[1:]
            n = sum(w[k].shape[0] for k in members)
            part = jnp.zeros((n * rows, C), F32)
            for l in range(n):
                part = _sum_slots(self.scatter[group, l].result(), part, l * rows, name=f"sum_{group}{l}")
            parts.append(part)
        return parts


def _reduce_and_update(w, m, v, parts, small):
    others = _swap_sibling(parts)
    small_shapes = {k: w[k].shape for k in REPLICATED}
    g_small = _unpack_small(_allreduce_small(_pack_small(small, small_shapes)), small_shapes)

    out_g, out_d, out_m, out_v = {}, {}, {}, {}
    for (members, _), part, other in zip(GROUPS.values(), parts, others):
        row0 = 0
        for k in members:
            out_g[k], out_d[k], out_m[k], out_v[k] = _adamw(w[k], m[k], v[k], [part, other], row0=row0,
                                                            name="adamw_" + k)
            row0 += w[k].shape[0] * w[k].shape[1]
    sm = _adamw(_pack_small(w, small_shapes), _pack_small(m, small_shapes), _pack_small(v, small_shapes),
                [_pack_small(g_small, small_shapes)], name="adamw_small")
    for d, packed in zip((out_g, out_d, out_m, out_v), sm):
        d.update(_unpack_small(packed, small_shapes))
    return out_g, out_d, out_m, out_v


def kernel(x, ffn1_norm, ffn1_w_gu, ffn1_w_down, mix_norm, ffn2_norm, ffn2_w_gu, ffn2_w_down, nsa_w_in, nsa_ck_pe, nsa_ck_w1, nsa_ck_w2, nsa_cv_pe, nsa_cv_w1, nsa_cv_w2, nsa_w_out, swa_w_in, swa_sinks, swa_w_out, fox_w_in, fox_b_f, fox_w_out, final_norm, loss_target, m_ffn1_norm, m_ffn1_w_gu, m_ffn1_w_down, m_mix_norm, m_ffn2_norm, m_ffn2_w_gu, m_ffn2_w_down, m_nsa_w_in, m_nsa_ck_pe, m_nsa_ck_w1, m_nsa_ck_w2, m_nsa_cv_pe, m_nsa_cv_w1, m_nsa_cv_w2, m_nsa_w_out, m_swa_w_in, m_swa_sinks, m_swa_w_out, m_fox_w_in, m_fox_b_f, m_fox_w_out, m_final_norm, v_ffn1_norm, v_ffn1_w_gu, v_ffn1_w_down, v_mix_norm, v_ffn2_norm, v_ffn2_w_gu, v_ffn2_w_down, v_nsa_w_in, v_nsa_ck_pe, v_nsa_ck_w1, v_nsa_ck_w2, v_nsa_cv_pe, v_nsa_cv_w1, v_nsa_cv_w2, v_nsa_w_out, v_swa_w_in, v_swa_sinks, v_swa_w_out, v_fox_w_in, v_fox_b_f, v_fox_w_out, v_final_norm):
    args = dict(locals())
    w = {k: args[k] for k in WEIGHTS}
    m = {k: args["m_" + k] for k in WEIGHTS}
    v = {k: args["v_" + k] for k in WEIGHTS}
    P = _Exchanged(w)
    loss_part, dx, small = _local_step(x[0], loss_target[0], P)
    loss = lax.psum(loss_part, ("x", "y", "c"))
    out_g, out_d, out_m, out_v = _reduce_and_update(w, m, v, P.partial_sums(w), small)
    return (loss, dx[None], *[out_g[k] for k in WEIGHTS], *[out_d[k] for k in WEIGHTS],
            *[out_m[k] for k in WEIGHTS], *[out_v[k] for k in WEIGHTS])
```

```python
import math

import numpy as np
import jax
import jax.numpy as jnp
from jax import lax
from jax.experimental import pallas as pl
from jax.experimental.pallas import tpu as pltpu

F32 = jnp.float32
_CDT = jnp.bfloat16
_WIRE = jnp.bfloat16
_VMEM_LIMIT = 56 * 1024 * 1024

D_MODEL = 1024
DEPTH = 4
DH = 64
N_HEADS = 16
RMS_EPS = 1e-6
NEG = -1e30
SCALE = DH ** -0.5
BQ = 256
LANES = 128
NSA_G, NSA_R = 4, 4
NSA_WINDOW = 512
NSA_TOPK = 16
NSA_BONUS = 1e4
SWA_G, SWA_R = 2, 8
SWA_WINDOW = 128
NSA_IN, NSA_IN_PAD = 2608, 2688
SWA_IN = 1280
FOX_IN, FOX_IN_PAD = 3088, 3200
LR, B1, B2, EPS, WD, STEP = 0.001, 0.9, 0.999, 1e-08, 0.01, 10
MESH = pl.DeviceIdType.MESH

REPLICATED = ["ffn1_norm", "mix_norm", "ffn2_norm", "nsa_ck_pe", "nsa_ck_w2", "nsa_cv_pe", "nsa_cv_w2",
              "swa_sinks", "fox_b_f", "final_norm"]
WEIGHTS = ['ffn1_norm', 'ffn1_w_gu', 'ffn1_w_down', 'mix_norm', 'ffn2_norm', 'ffn2_w_gu', 'ffn2_w_down',
           'nsa_w_in', 'nsa_ck_pe', 'nsa_ck_w1', 'nsa_ck_w2', 'nsa_cv_pe', 'nsa_cv_w1', 'nsa_cv_w2', 'nsa_w_out',
           'swa_w_in', 'swa_sinks', 'swa_w_out', 'fox_w_in', 'fox_b_f', 'fox_w_out', 'final_norm']


def _params(**kw):
    return pltpu.CompilerParams(vmem_limit_bytes=_VMEM_LIMIT, **kw)


def _dot(a, b):
    return lax.dot_general(a, b, (((1,), (0,)), ((), ())), preferred_element_type=F32)


def _dot_nt(a, b):
    return lax.dot_general(a, b, (((1,), (1,)), ((), ())), preferred_element_type=F32)


def _dot_tn(a, b):
    return lax.dot_general(a, b, (((0,), (0,)), ((), ())), preferred_element_type=F32)


def _split3(x):
    hi = x.astype(jnp.bfloat16)
    r1 = x - hi.astype(F32)
    mid = r1.astype(jnp.bfloat16)
    lo = (r1 - mid.astype(F32)).astype(jnp.bfloat16)
    return hi, mid, lo


def _dot_exact(x, p):
    hi, mid, lo = _split3(x)
    return _dot(hi, p) + _dot(mid, p) + _dot(lo, p)


def _dot_exact_left(p, x):
    hi, mid, lo = _split3(x)
    return _dot(p, hi) + _dot(p, mid) + _dot(p, lo)


_ANY = pl.BlockSpec(memory_space=pl.ANY)
_COMM = pltpu.CompilerParams(has_side_effects=True)


def _chip_peers():
    x, y, c = lax.axis_index("x"), lax.axis_index("y"), lax.axis_index("c")
    return x, y, c, [(1 - x, y), (x, 1 - y), (1 - x, 1 - y)]


class _Job:
    def __init__(self, ins, outs, nsem, copies, cost, alias):
        self.ins, self.outs, self.nsem, self.copies, self.cost, self.alias = ins, outs, nsem, copies, cost, alias
        self.results = None

    def inputs(self):
        return self.ins


class _Task:
    def __init__(self, src, out, parts, init=None):
        self.src, self.out, self.parts, self.init, self.done, self.jobs = src, out, parts, init, 0, []

    def result(self):
        return self.jobs[-1].results[0]

    def complete(self):
        return self.done == len(self.parts) and self.jobs[-1].results is not None

    def ready(self):
        if isinstance(self.src, _Task) and not self.src.complete():
            return False
        return not self.jobs or self.jobs[-1].results is not None

    def next_job(self, n):
        parts = self.parts[self.done:self.done + n]
        in_place = isinstance(self.src, _Task)
        ins = [] if in_place else [self.src]
        prev = self.jobs[-1].results[0] if self.jobs else (self.src.result() if in_place else self.init)
        alias = {}
        if prev is not None:
            alias = {len(ins): 0}
            ins = ins + [prev]

        def copies(in_refs, out_refs, sems):
            src = out_refs[0] if in_place else in_refs[0]
            out, first = [], 0
            for build, nsem, _ in parts:
                out += build(src, out_refs[0], sems, first)
                first += nsem
            return out

        job = _Job(ins, [self.out], sum(p[1] for p in parts), copies, sum(p[2] for p in parts), alias)
        self.done += n
        self.jobs.append(job)
        return job


def _pcall(body, *, out_shape, in_specs, out_specs, name, grid=(), scratch_shapes=(), aliases=None, jobs=(), ins):
    single = not isinstance(out_shape, (tuple, list))
    out_shape = (out_shape,) if single else tuple(out_shape)
    out_specs = (out_specs,) if single else tuple(out_specs)
    n_in, n_out, n_scr = len(ins), len(out_shape), len(scratch_shapes)
    side_ins = [a for j in jobs for a in j.inputs()]
    side_outs = [o for j in jobs for o in j.outs]
    aliases = dict(aliases or {})
    i0, o0 = n_in, n_out
    for j in jobs:
        for a, b in j.alias.items():
            aliases[i0 + a] = o0 + b
        i0 += len(j.ins)
        o0 += len(j.outs)

    def body2(*refs):
        in_refs, s_in = refs[:n_in], refs[n_in:n_in + len(side_ins)]
        r = n_in + len(side_ins)
        out_refs, s_out = refs[r:r + n_out], refs[r + n_out:r + n_out + len(side_outs)]
        r += n_out + len(side_outs)
        scr, sems = refs[r:r + n_scr], refs[r + n_scr:]
        def descriptors():
            out, a, b = [], 0, 0
            for j, sem in zip(jobs, sems):
                out += j.copies(s_in[a:a + len(j.ins)], s_out[b:b + len(j.outs)], sem)
                a += len(j.ins)
                b += len(j.outs)
            return out

        def at_step(steps, fn):
            cond = None
            for d, s in enumerate(steps):
                hit = pl.program_id(d) == s
                cond = hit if cond is None else jnp.logical_and(cond, hit)
            if cond is None:
                fn()
            else:
                pl.when(cond)(fn)

        def start_all():
            for cp in descriptors():
                cp.start()

        def wait_all():
            for cp in descriptors():
                cp.wait()

        if jobs:
            at_step([0] * len(grid), start_all)
        body(*in_refs, *out_refs, *scr)
        if jobs:
            at_step([n - 1 for n in grid], wait_all)

    res = pl.pallas_call(
        body2, out_shape=out_shape + tuple(side_outs), grid=grid, in_specs=list(in_specs) + [_ANY] * len(side_ins),
        out_specs=out_specs + (_ANY,) * len(side_outs),
        scratch_shapes=list(scratch_shapes) + [pltpu.SemaphoreType.DMA((j.nsem,)) for j in jobs],
        input_output_aliases=aliases,
        compiler_params=_params(has_side_effects=True) if jobs else _params(), name=name)(*ins, *side_ins)
    b = n_out
    for j in jobs:
        j.results = list(res[b:b + len(j.outs)])
        b += len(j.outs)
    return res[0] if single else tuple(res[:n_out])


def _comm_call(jobs, name):
    def body():
        pass

    _pcall(body, out_shape=(), in_specs=[], out_specs=(), name=name, jobs=jobs, ins=[])


def _place_own(shard, l, me, *, name):
    _, _, h, C = shard.shape

    def body(_, s_ref, o_ref):
        o_ref[...] = s_ref[...]

    spec = pltpu.PrefetchScalarGridSpec(
        num_scalar_prefetch=1, grid=(2,),
        in_specs=[pl.BlockSpec((None, None, h, C), lambda i, me_ref: (l, i, 0, 0))],
        out_specs=pl.BlockSpec((None, None, h, C), lambda i, me_ref: (me_ref[0], i, 0, 0)))
    return pl.pallas_call(body, grid_spec=spec, out_shape=jax.ShapeDtypeStruct((4, 2, h, C), shard.dtype),
                          compiler_params=_params(), name=name)(me, shard)


def _gather_task1(shard, l, own):
    _, _, h, C = shard.shape
    mb = 3 * h * C * jnp.dtype(_WIRE).itemsize / 1e6
    n = _row_splits(h, ICI_US_PER_MB * mb)
    hr = h // n

    def rows_to_peers(r):
        def build(src, dst, sems, s0):
            x, y, c, chips = _chip_peers()
            me, rows = 2 * x + y, pl.ds(r * hr, hr)
            return [pltpu.make_async_remote_copy(src_ref=src.at[l, c, rows], dst_ref=dst.at[me, c, rows],
                                                 send_sem=sems.at[s0 + 2 * j], recv_sem=sems.at[s0 + 2 * j + 1],
                                                 device_id=(px, py, c), device_id_type=MESH)
                    for j, (px, py) in enumerate(chips)]
        return build

    parts = [(rows_to_peers(r), 6, ICI_US_PER_MB * mb / n) for r in range(n)]
    return _Task(shard, jax.ShapeDtypeStruct((4, 2, h, C), shard.dtype), parts, init=own)


def _gather_task2(task1):
    _, _, h, C = task1.out.shape
    mb = h * C * jnp.dtype(_WIRE).itemsize / 1e6

    def forward(j):
        def build(_, dst, sems, s0):
            x, y, c, chips = _chip_peers()
            px, py = chips[j]
            landed = dst.at[2 * px + py, c]
            return [pltpu.make_async_remote_copy(src_ref=landed, dst_ref=landed, send_sem=sems.at[s0],
                                                 recv_sem=sems.at[s0 + 1], device_id=(x, y, 1 - c),
                                                 device_id_type=MESH)]
        return build

    return _Task(task1, task1.out, [(forward(j), 2, 1.0 + D2D_US_PER_MB * mb) for j in range(3)])


def _scatter_task(G):
    mb = 3 * G.shape[1] * G.shape[2] * jnp.dtype(_WIRE).itemsize / 1e6
    n = _row_splits(G.shape[1], ICI_US_PER_MB * mb)
    rr = G.shape[1] // n

    def rows_to_peers(r):
        def build(src, dst, sems, s0):
            _, _, c, chips = _chip_peers()
            rows = pl.ds(r * rr, rr)
            return [pltpu.make_async_remote_copy(src_ref=src.at[2 * px + py, rows], dst_ref=dst.at[j, rows],
                                                 send_sem=sems.at[s0 + 2 * j], recv_sem=sems.at[s0 + 2 * j + 1],
                                                 device_id=(px, py, c), device_id_type=MESH)
                    for j, (px, py) in enumerate(chips)]
        return build

    parts = [(rows_to_peers(r), 6, ICI_US_PER_MB * mb / n) for r in range(n)]
    return _Task(G, jax.ShapeDtypeStruct((3,) + G.shape[1:], G.dtype), parts)


def _row_splits(rows, cost):
    n = 8
    while n > 1 and (rows % (16 * n) or cost / n < PART_US):
        n //= 2
    return n


CARRIER_US = {
    "f_gu": 44, "f_act": 20, "f_down": 20, "f_dact": 26, "f_dwd": 20, "f_dgu": 38, "f_dwgu": 42, "f_dh": 46,
    "nsa_in": 25, "nsa_cmp": 86, "nsa_slc": 122, "nsa_win": 74, "nsa_out": 13, "nsa_dof": 11, "nsa_dwout": 11,
    "nsa_dcmp": 58, "nsa_dslc": 195, "nsa_dwin": 108, "nsa_dh": 26, "nsa_dwin_w": 25,
    "swa_in": 16, "swa_attn": 74, "swa_out": 13, "swa_dof": 11, "swa_dwout": 11, "swa_dattn": 74, "swa_dh": 17,
    "swa_dwin_w": 18,
    "fox_in": 34, "fox_attn": 104, "fox_out": 13, "fox_dof": 11, "fox_dwout": 11, "fox_dattn": 161, "fox_dh": 30,
    "fox_dwin_w": 30,
}
ICI_US_PER_MB = 15.0
PART_US = 12.0
D2D_US_PER_MB = 2.3
LOCAL_US_PER_MB = 1.5


class _Sched:
    def __init__(self):
        self.queue, self.credit, self.n_flush = [], 0.0, 0

    def push(self, task):
        self.queue.append(task)
        return task

    def _jobs(self, fits):
        jobs = []
        while self.queue and self.queue[0].ready():
            task, n = self.queue[0], 0
            while task.done + n < len(task.parts) and fits(task.parts[task.done + n][2]):
                n += 1
            if n:
                jobs.append(task.next_job(n))
            if task.done < len(task.parts):
                break
            self.queue.pop(0)
        return jobs

    def take(self, name):
        kind = name[2:].split("_", 1)
        key = ("f" if kind[0] in ("f1", "f2") else kind[0]) + "_" + kind[1]
        self.credit = min(self.credit, 0.0) + CARRIER_US.get(key, 0.0)

        def fits(cost):
            if self.credit < 0.5 * cost:
                return False
            self.credit -= cost
            return True

        return self._jobs(fits)

    def finish(self, task=None):
        while self.queue and (task is None or not task.complete()):
            jobs = []
            while self.queue and self.queue[0].ready():
                head = self.queue.pop(0)
                jobs.append(head.next_job(len(head.parts) - head.done))
                if head is task:
                    break
            assert jobs, "the task at the head of the queue waits for one that was never queued"
            _comm_call(jobs, name=f"exchange_{self.n_flush}")
            self.n_flush += 1


def _mm(a, b, *, ta=False, tb=False, tm, tn, b_lead=(), into=None, o_lead=(), out_dtype=F32, res=None, alpha=1.0,
        ctx=None, name):
    M = a.shape[1] if ta else a.shape[0]
    K = a.shape[0] if ta else a.shape[1]
    bk, bn = (b.shape[-1], b.shape[-2]) if tb else (b.shape[-2], b.shape[-1])
    j_lead, k_lead = "j" in b_lead, "k" in b_lead
    N = bn * (b.shape[b_lead.index("j")] if j_lead else 1)
    nk = b.shape[b_lead.index("k")] if k_lead else 1
    tk = K // nk
    assert tk == bk, (name, K, nk, bk)
    tm = min(tm, M)
    tn = bn if j_lead else min(tn, N)
    assert M % tm == 0 and N % tn == 0, (name, M, N, tm, tn)
    nb, no = len(b_lead), len(o_lead)

    def pick(lead, j, k):
        return tuple(j if t == "j" else k if t == "k" else t for t in lead)

    a_spec = pl.BlockSpec((tk, tm), lambda i, j, k: (k, i)) if ta else pl.BlockSpec((tm, tk), lambda i, j, k: (i, k))
    if tb:
        b_spec = pl.BlockSpec((None,) * nb + (tn, tk),
                              lambda i, j, k: pick(b_lead, j, k) + (0 if j_lead else j, 0 if k_lead else k))
    else:
        b_spec = pl.BlockSpec((None,) * nb + (tk, tn),
                              lambda i, j, k: pick(b_lead, j, k) + (0 if k_lead else k, 0 if j_lead else j))
    r_spec = pl.BlockSpec((tm, tn), lambda i, j, k: (i, j))
    o_spec = pl.BlockSpec((None,) * no + (tm, tn), lambda i, j, k: pick(o_lead, j, k) + (i, 0 if "j" in o_lead else j))
    dn = (((0 if ta else 1,), (1 if tb else 0,)), ((), ()))
    has_res, has_into = res is not None, into is not None
    if has_into:
        out_dtype = into.dtype

    def body(*refs):
        a_ref, b_ref = refs[0], refs[1]
        r_ref = refs[2] if has_res else None
        o_ref = refs[2 + has_res + has_into]
        prod = lax.dot_general(a_ref[...].astype(_CDT), b_ref[...].astype(_CDT), dn, preferred_element_type=F32)

        def finish(acc):
            if alpha != 1.0:
                acc = acc * alpha
            if has_res:
                acc = r_ref[...] + acc
            o_ref[...] = acc.astype(out_dtype)

        if nk == 1:
            finish(prod)
        else:
            acc_ref = refs[-1]
            k = pl.program_id(2)

            @pl.when(k == 0)
            def _():
                acc_ref[...] = prod

            @pl.when(k != 0)
            def _():
                acc_ref[...] += prod

            @pl.when(k == nk - 1)
            def _():
                finish(acc_ref[...])

    ins, specs = [a, b], [a_spec, b_spec]
    if has_res:
        ins.append(res)
        specs.append(r_spec)
    aliases = {}
    if has_into:
        aliases = {len(ins): 0}
        ins.append(into)
        specs.append(_ANY)
        out_shape = jax.ShapeDtypeStruct(into.shape, into.dtype)
    elif o_lead == ("j",):
        out_shape = jax.ShapeDtypeStruct((N // tn, M, tn), out_dtype)
    else:
        assert not o_lead
        out_shape = jax.ShapeDtypeStruct((M, N), out_dtype)
    scratch = [pltpu.VMEM((tm, tn), F32)] if nk > 1 else []
    return _pcall(body, out_shape=out_shape, grid=(M // tm, N // tn, nk), in_specs=specs, out_specs=o_spec,
                  scratch_shapes=scratch, aliases=aliases, name=name, jobs=ctx.take(name) if ctx else (), ins=ins)


def _rows2d(a):
    return a.reshape(-1, a.shape[-1])


def _row_tile(rows, cols, itemsize=4, budget=2 * 1024 * 1024):
    t = rows
    while t % 2 == 0 and t * cols * itemsize > budget and (t // 2) % 8 == 0:
        t //= 2
    return t


def _addn(*xs, name):
    shape = xs[0].shape
    x2 = [_rows2d(x) for x in xs]
    R, C = x2[0].shape
    tr = _row_tile(R, C)

    def body(*refs):
        acc = refs[0][...]
        for r in refs[1:-1]:
            acc = acc + r[...]
        refs[-1][...] = acc

    spec = pl.BlockSpec((tr, C), lambda i: (i, 0))
    out = pl.pallas_call(body, out_shape=jax.ShapeDtypeStruct((R, C), F32), grid=(R // tr,),
                         in_specs=[spec] * len(x2), out_specs=spec, compiler_params=_params(), name=name)(*x2)
    return out.reshape(shape)


def _rms_fwd(x, g, *, name):
    S, D = x.shape
    tr = 256

    def body(x_ref, g_ref, h_ref):
        xv = x_ref[...]
        rstd = lax.rsqrt(jnp.mean(xv * xv, axis=-1, keepdims=True) + RMS_EPS)
        h_ref[...] = (xv * rstd * g_ref[...]).astype(_CDT)

    return pl.pallas_call(body, out_shape=jax.ShapeDtypeStruct((S, D), _CDT), grid=(S // tr,),
                          in_specs=[pl.BlockSpec((tr, D), lambda i: (i, 0)), pl.BlockSpec((1, D), lambda i: (0, 0))],
                          out_specs=pl.BlockSpec((tr, D), lambda i: (i, 0)), compiler_params=_params(),
                          name=name)(x, g.reshape(1, D))


def _rms_bwd(dh, x, g, dres, *, name):
    S, D = x.shape
    tr = 256

    def body(dh_ref, x_ref, g_ref, dres_ref, dx_ref, dg_ref):
        xv = x_ref[...]
        rstd = lax.rsqrt(jnp.mean(xv * xv, axis=-1, keepdims=True) + RMS_EPS)
        xhat = xv * rstd
        dhv = dh_ref[...]
        dxhat = dhv * g_ref[...]
        dx_ref[...] = dres_ref[...] + rstd * (dxhat - xhat * jnp.mean(dxhat * xhat, axis=-1, keepdims=True))

        @pl.when(pl.program_id(0) == 0)
        def _():
            dg_ref[...] = jnp.zeros_like(dg_ref)

        dg_ref[...] += jnp.sum(dhv * xhat, axis=0, keepdims=True)

    row = pl.BlockSpec((tr, D), lambda i: (i, 0))
    vec = pl.BlockSpec((1, D), lambda i: (0, 0))
    dx, dg = pl.pallas_call(body, out_shape=(jax.ShapeDtypeStruct((S, D), F32), jax.ShapeDtypeStruct((1, D), F32)),
                            grid=(S // tr,), in_specs=[row, row, vec, row], out_specs=(row, vec),
                            compiler_params=_params(), name=name)(dh, x, g.reshape(1, D), dres)
    return dx, dg.reshape(D)


def _swiglu_fwd(gu, *, ctx=None, name):
    S, W = gu.shape
    H = W // 2
    tr = 256

    def body(gu_ref, a_ref):
        g = gu_ref[:, :H]
        u = gu_ref[:, H:]
        a_ref[...] = (g * jax.nn.sigmoid(g) * u).astype(_CDT)

    return _pcall(body, out_shape=jax.ShapeDtypeStruct((S, H), _CDT), grid=(S // tr,),
                  in_specs=[pl.BlockSpec((tr, W), lambda i: (i, 0))], out_specs=pl.BlockSpec((tr, H), lambda i: (i, 0)),
                  name=name, jobs=ctx.take(name) if ctx else (), ins=[gu])


def _swiglu_bwd(gu, da, *, ctx=None, name):
    S, W = gu.shape
    H = W // 2
    tr = 256

    def body(gu_ref, da_ref, d_ref):
        g = gu_ref[:, :H]
        u = gu_ref[:, H:]
        dav = da_ref[...]
        sg = jax.nn.sigmoid(g)
        silu = g * sg
        d_ref[:, :H] = dav * u * (sg + silu * (1.0 - sg))
        d_ref[:, H:] = dav * silu

    return _pcall(body, out_shape=jax.ShapeDtypeStruct((S, W), F32), grid=(S // tr,),
                  in_specs=[pl.BlockSpec((tr, W), lambda i: (i, 0)), pl.BlockSpec((tr, H), lambda i: (i, 0))],
                  out_specs=pl.BlockSpec((tr, W), lambda i: (i, 0)), name=name,
                  jobs=ctx.take(name) if ctx else (), ins=[gu, da])


def _loss_head(x, g, tgt, *, name):
    S, D = x.shape
    tr = 256

    def body(x_ref, g_ref, t_ref, loss_ref, dx_ref, dg_ref):
        xv = x_ref[...]
        rstd = lax.rsqrt(jnp.mean(xv * xv, axis=-1, keepdims=True) + RMS_EPS)
        xhat = xv * rstd
        err = xhat * g_ref[...] - t_ref[...]
        part = 0.5 * jnp.sum(jnp.mean(err * err, axis=-1, keepdims=True), axis=0, keepdims=True)
        dy = err * (1.0 / D)
        dxhat = dy * g_ref[...]
        dx_ref[...] = rstd * (dxhat - xhat * jnp.mean(dxhat * xhat, axis=-1, keepdims=True))

        @pl.when(pl.program_id(0) == 0)
        def _():
            dg_ref[...] = jnp.zeros_like(dg_ref)
            loss_ref[...] = jnp.zeros_like(loss_ref)

        dg_ref[...] += jnp.sum(dy * xhat, axis=0, keepdims=True)
        loss_ref[...] += jnp.broadcast_to(part, loss_ref.shape)

    row = pl.BlockSpec((tr, D), lambda i: (i, 0))
    vec = pl.BlockSpec((1, D), lambda i: (0, 0))
    loss, dx, dg = pl.pallas_call(
        body, out_shape=(jax.ShapeDtypeStruct((8, LANES), F32), jax.ShapeDtypeStruct((S, D), F32),
                         jax.ShapeDtypeStruct((1, D), F32)),
        grid=(S // tr,), in_specs=[row, vec, row], out_specs=(pl.BlockSpec((8, LANES), lambda i: (0, 0)), row, vec),
        compiler_params=_params(), name=name)(x, g.reshape(1, D), tgt)
    return loss[0, 0], dx, dg.reshape(D)


def _rope_tables(S):
    inv = 10000.0 ** (-jnp.arange(0, DH, 2, dtype=F32) / DH)
    ang = jnp.arange(S, dtype=F32)[:, None] * inv[None, :]
    cos, sin = jnp.cos(ang), jnp.sin(ang)
    return jnp.concatenate([cos, cos], -1), jnp.concatenate([-sin, sin], -1)


def _swap_matrix():
    p = np.zeros((DH, DH), np.float32)
    for j in range(DH // 2):
        p[j + DH // 2, j] = 1.0
        p[j, j + DH // 2] = 1.0
    return jnp.asarray(p, jnp.bfloat16)


def _rope(x, cc, ss, *, name):
    n, S, _ = x.shape

    def body(x_ref, c_ref, s_ref, p_ref, o_ref):
        xv = x_ref[0]
        o_ref[0] = xv * c_ref[...] + _dot_exact(xv, p_ref[...]) * s_ref[...]

    tab = pl.BlockSpec((S, DH), lambda i: (0, 0))
    blk = pl.BlockSpec((1, S, DH), lambda i: (i, 0, 0))
    return pl.pallas_call(body, out_shape=jax.ShapeDtypeStruct(x.shape, F32), grid=(n,),
                          in_specs=[blk, tab, tab, pl.BlockSpec((DH, DH), lambda i: (0, 0))], out_specs=blk,
                          compiler_params=_params(), name=name)(x, cc, ss, _swap_matrix())


def _key_range(kind, i, window, Sk):
    if kind == "cmp":
        return 0, Sk
    hi = (i + 1) * BQ
    if kind == "band":
        return max(0, i * BQ - window), hi
    return 0, hi


def _attn_mask(kind, i, lo, hi, window):
    shape = (BQ, hi - lo)
    qpos = i * BQ + lax.broadcasted_iota(jnp.int32, shape, 0)
    kpos = lo + lax.broadcasted_iota(jnp.int32, shape, 1)
    if kind == "cmp":
        return kpos * 16 + 31 <= qpos
    mask = kpos <= qpos
    if kind == "band":
        mask = mask & (qpos - kpos < window)
    return mask


def _sel_expand(lo, hi):
    shape = (LANES, hi - lo)
    j = lax.broadcasted_iota(jnp.int32, shape, 0)
    key = lo + lax.broadcasted_iota(jnp.int32, shape, 1)
    return (jnp.right_shift(key, 6) == j).astype(_CDT)


def _eye():
    return lax.broadcasted_iota(jnp.int32, (BQ, BQ), 0) == lax.broadcasted_iota(jnp.int32, (BQ, BQ), 1)


def _to_col(row):
    return jnp.sum(jnp.where(_eye(), row, 0.0), axis=1, keepdims=True)


def _to_row(col):
    return jnp.sum(jnp.where(_eye(), col, 0.0), axis=0, keepdims=True)


def _scores(kind, i, lo, hi, window, qb, kb, crow_ref, sel_ref):
    s = _dot_nt(qb, kb) * SCALE
    if crow_ref is not None:
        s = s + _to_col(crow_ref[0, :, i * BQ:(i + 1) * BQ]) - crow_ref[0, :, lo:hi]
    mask = _attn_mask(kind, i, lo, hi, window)
    if sel_ref is not None:
        chosen = _dot_tn(sel_ref[0, :, i * BQ:(i + 1) * BQ].astype(_CDT), _sel_expand(lo, hi))
        mask = mask & (chosen > 0.5)
    return jnp.where(mask, s, NEG), mask


def _attn_fwd(q, k, v, *, kind, window=0, bias=None, sinks=None, selT=None, ovT=None, ctx=None, name):
    G, R, S, _ = q.shape
    Sk = k.shape[1]
    nq = S // BQ
    n_slc = S // 64
    has_bias, has_sink, has_sel, is_cmp = bias is not None, sinks is not None, selT is not None, kind == "cmp"

    def body(*refs):
        it = iter(refs)
        q_ref, k_ref, v_ref = next(it), next(it), next(it)
        crow_ref = next(it) if has_bias else None
        sink_ref = next(it) if has_sink else None
        sel_ref = next(it) if has_sel else None
        ov_ref = next(it) if is_cmp else None
        o_ref, lse_ref = next(it), next(it)
        selo_ref, imp_ref = (next(it), next(it)) if is_cmp else (None, None)
        r = pl.program_id(1)
        for i in range(nq):
            lo, hi = _key_range(kind, i, window, Sk)
            rows = slice(i * BQ, (i + 1) * BQ)
            qb = q_ref[0, 0, rows, :].astype(_CDT)
            kb = k_ref[0, lo:hi, :].astype(_CDT)
            vb = v_ref[0, lo:hi, :].astype(_CDT)
            s, mask = _scores(kind, i, lo, hi, window, qb, kb, crow_ref, sel_ref)
            m = jnp.max(s, axis=-1, keepdims=True)
            if has_sink:
                sk = sink_ref[0, 0, 0:1, 0:1]
                m = jnp.maximum(m, sk)
            e = jnp.exp(s - m)
            if is_cmp:
                e = jnp.where(mask, e, 0.0)
            l = jnp.sum(e, axis=-1, keepdims=True)
            if has_sink:
                l = l + jnp.exp(sk - m)
            if is_cmp:
                l = jnp.where(l > 0.0, l, 1.0)
            p = e * (1.0 / l)
            o_ref[0, 0, rows, :] = _dot(p.astype(_CDT), vb)
            lse_ref[0, 0, :, rows] = _to_row(m + jnp.log(l))
            if is_cmp:
                part = _dot_nt(ov_ref[...].astype(_CDT), p.astype(_CDT))

                @pl.when(r == 0)
                def _():
                    imp_ref[:, rows] = part

                @pl.when(r != 0)
                def _():
                    imp_ref[:, rows] += part

        if is_cmp:
            @pl.when(r == R - 1)
            def _():
                shape = (LANES, S)
                j = lax.broadcasted_iota(jnp.int32, shape, 0)
                tb = jnp.right_shift(lax.broadcasted_iota(jnp.int32, shape, 1), 6)
                forced = (j == 0) | (j == tb) | (j == tb - 1)
                imp = jnp.where(j > tb, NEG, jnp.where(forced, NSA_BONUS, imp_ref[...]))
                imp = jnp.where(j >= n_slc, -3e38, imp)
                imp_ref[...] = imp
                cnt = jnp.zeros(shape, F32)
                for jp in range(n_slc):
                    row = imp_ref[jp:jp + 1, :]
                    ahead = (row > imp) | ((row == imp) & (jp < j))
                    cnt = cnt + ahead.astype(F32)
                selo_ref[0] = (cnt < float(min(NSA_TOPK, n_slc))).astype(F32)

    qspec = pl.BlockSpec((1, 1, S, DH), lambda g, r: (g, r, 0, 0))
    kspec = pl.BlockSpec((1, Sk, DH), lambda g, r: (g, 0, 0))
    ins, specs = [q, k, v], [qspec, kspec, kspec]
    if has_bias:
        ins.append(bias)
        specs.append(pl.BlockSpec((1, 1, S), lambda g, r: (g, 0, 0)))
    if has_sink:
        ins.append(sinks)
        specs.append(pl.BlockSpec((1, 1, 8, LANES), lambda g, r: (g, r, 0, 0)))
    if has_sel:
        ins.append(selT)
        specs.append(pl.BlockSpec((1, LANES, S), lambda g, r: (g, 0, 0)))
    if is_cmp:
        ins.append(ovT)
        specs.append(pl.BlockSpec((LANES, Sk), lambda g, r: (0, 0)))
    outs = [jax.ShapeDtypeStruct((G, R, S, DH), F32), jax.ShapeDtypeStruct((G, R, 1, S), F32)]
    ospecs = [qspec, pl.BlockSpec((1, 1, 1, S), lambda g, r: (g, r, 0, 0))]
    scratch = []
    if is_cmp:
        outs.append(jax.ShapeDtypeStruct((G, LANES, S), F32))
        ospecs.append(pl.BlockSpec((1, LANES, S), lambda g, r: (g, 0, 0)))
        scratch.append(pltpu.VMEM((LANES, S), F32))
    return _pcall(body, out_shape=tuple(outs), grid=(G, R), in_specs=specs, out_specs=tuple(ospecs),
                  scratch_shapes=scratch, name=name, jobs=ctx.take(name) if ctx else (), ins=ins)


def _attn_bwd(q, k, v, o, do, lse, *, kind, window=0, bias=None, sinks=None, selT=None, gate=None, ctx=None, name):
    G, R, S, _ = q.shape
    Sk = k.shape[1]
    nq = S // BQ
    has_bias, has_sink, has_sel, has_gate = bias is not None, sinks is not None, selT is not None, gate is not None

    def body(*refs):
        it = iter(refs)
        q_ref, k_ref, v_ref, o_ref, do_ref, lse_ref = (next(it) for _ in range(6))
        crow_ref = next(it) if has_bias else None
        sink_ref = next(it) if has_sink else None
        sel_ref = next(it) if has_sel else None
        z_ref = next(it) if has_gate else None
        dq_ref, dk_ref, dv_ref = next(it), next(it), next(it)
        dc_ref = next(it) if has_bias else None
        dsink_ref = next(it) if has_sink else None
        dz_ref = next(it) if has_gate else None
        r = pl.program_id(1)

        @pl.when(r == 0)
        def _():
            dk_ref[...] = jnp.zeros_like(dk_ref)
            dv_ref[...] = jnp.zeros_like(dv_ref)
            if has_bias:
                dc_ref[...] = jnp.zeros_like(dc_ref)

        dsink = jnp.zeros((1, 1), F32)
        for i in range(nq):
            lo, hi = _key_range(kind, i, window, Sk)
            rows = slice(i * BQ, (i + 1) * BQ)
            qb = q_ref[0, 0, rows, :].astype(_CDT)
            kb = k_ref[0, lo:hi, :].astype(_CDT)
            vb = v_ref[0, lo:hi, :].astype(_CDT)
            s, mask = _scores(kind, i, lo, hi, window, qb, kb, crow_ref, sel_ref)
            lse_i = _to_col(lse_ref[0, 0, :, rows])
            p = jnp.where(mask, jnp.exp(s - lse_i), 0.0)
            dob = do_ref[0, 0, rows, :]
            if has_gate:
                od = jnp.sum(o_ref[0, 0, rows, :] * dob, axis=-1, keepdims=True)
                sg = jax.nn.sigmoid(_to_col(z_ref[0, 0, :, rows]))
                dob = dob * sg
                dz_ref[0, 0, :, rows] = _to_row(od * sg * (1.0 - sg))
            dob = dob.astype(_CDT)
            dp = _dot_nt(dob, vb)
            delta = jnp.sum(p * dp, axis=-1, keepdims=True)
            ds = p * (dp - delta)
            dsb = ds.astype(_CDT)
            dq_ref[0, 0, rows, :] = _dot(dsb, kb) * SCALE
            dk_ref[0, lo:hi, :] += _dot_tn(dsb, qb) * SCALE
            dv_ref[0, lo:hi, :] += _dot_tn(p.astype(_CDT), dob)
            if has_bias:
                dc_ref[0, :, rows] += _to_row(jnp.sum(ds, axis=-1, keepdims=True))
                dc_ref[0, :, lo:hi] -= jnp.sum(ds, axis=0, keepdims=True)
            if has_sink:
                sk = sink_ref[0, 0, 0:1, 0:1]
                dsink = dsink - jnp.sum(jnp.exp(sk - lse_i) * delta, axis=0, keepdims=True)
        if has_sink:
            dsink_ref[0, 0] = jnp.broadcast_to(dsink, (8, LANES))

    qspec = pl.BlockSpec((1, 1, S, DH), lambda g, r: (g, r, 0, 0))
    cspec = pl.BlockSpec((1, 1, 1, S), lambda g, r: (g, r, 0, 0))
    kspec = pl.BlockSpec((1, Sk, DH), lambda g, r: (g, 0, 0))
    ins, specs = [q, k, v, o, do, lse], [qspec, kspec, kspec, qspec, qspec, cspec]
    if has_bias:
        ins.append(bias)
        specs.append(pl.BlockSpec((1, 1, S), lambda g, r: (g, 0, 0)))
    if has_sink:
        ins.append(sinks)
        specs.append(pl.BlockSpec((1, 1, 8, LANES), lambda g, r: (g, r, 0, 0)))
    if has_sel:
        ins.append(selT)
        specs.append(pl.BlockSpec((1, LANES, S), lambda g, r: (g, 0, 0)))
    if has_gate:
        ins.append(gate)
        specs.append(cspec)
    names = ["dq", "dk", "dv"]
    outs = [jax.ShapeDtypeStruct((G, R, S, DH), F32), jax.ShapeDtypeStruct((G, Sk, DH), F32),
            jax.ShapeDtypeStruct((G, Sk, DH), F32)]
    ospecs = [qspec, kspec, kspec]
    if has_bias:
        assert R == 1
        names.append("dc")
        outs.append(jax.ShapeDtypeStruct((G, 1, S), F32))
        ospecs.append(pl.BlockSpec((1, 1, S), lambda g, r: (g, 0, 0)))
    if has_sink:
        names.append("dsink")
        outs.append(jax.ShapeDtypeStruct((G, R, 8, LANES), F32))
        ospecs.append(pl.BlockSpec((1, 1, 8, LANES), lambda g, r: (g, r, 0, 0)))
    if has_gate:
        names.append("dz")
        outs.append(jax.ShapeDtypeStruct((G, R, 1, S), F32))
        ospecs.append(cspec)
    res = _pcall(body, out_shape=tuple(outs), grid=(G, R), in_specs=specs, out_specs=tuple(ospecs), name=name,
                 jobs=ctx.take(name) if ctx else (), ins=ins)
    return dict(zip(names, res))


def _combine(o0, o1, o2, z, *, name):
    H, S, _ = o0.shape

    def body(o0_ref, o1_ref, o2_ref, z_ref, o_ref):
        for i in range(S // BQ):
            rows = slice(i * BQ, (i + 1) * BQ)
            acc = jax.nn.sigmoid(_to_col(z_ref[0, 0, :, rows])) * o0_ref[0, rows, :]
            acc = acc + jax.nn.sigmoid(_to_col(z_ref[1, 0, :, rows])) * o1_ref[0, rows, :]
            acc = acc + jax.nn.sigmoid(_to_col(z_ref[2, 0, :, rows])) * o2_ref[0, rows, :]
            o_ref[0, rows, :] = acc

    blk = pl.BlockSpec((1, S, DH), lambda h: (h, 0, 0))
    return pl.pallas_call(body, out_shape=jax.ShapeDtypeStruct((H, S, DH), F32), grid=(H,),
                          in_specs=[blk, blk, blk, pl.BlockSpec((3, 1, 1, S), lambda h: (0, h, 0, 0))], out_specs=blk,
                          compiler_params=_params(), name=name)(o0, o1, o2, z)


_GC = math.sqrt(2.0 / math.pi)


def _gelu(x):
    return 0.5 * x * (1.0 + jnp.tanh(_GC * (x + 0.044715 * x * x * x)))


def _gelu_grad(x):
    t = jnp.tanh(_GC * (x + 0.044715 * x * x * x))
    return 0.5 * (1.0 + t) + 0.5 * x * (1.0 - t * t) * _GC * (1.0 + 3.0 * 0.044715 * x * x)


def _make_xb(k):
    G, S, _ = k.shape
    chunks = k.reshape(G, S // 16, 16 * DH)
    shift = jnp.concatenate([chunks[:, 1:], jnp.zeros((G, 1, 16 * DH), k.dtype)], axis=1)
    return jnp.concatenate([chunks, shift], axis=-1)


def _unmake_xb(dxb, *, name):
    G, n, _ = dxb.shape
    a = dxb[..., :16 * DH]
    b = jnp.concatenate([jnp.zeros((G, 1, 16 * DH), F32), dxb[:, :-1, 16 * DH:]], axis=1)
    return _addn(a, b, name=name).reshape(G, n * 16, DH)


def _compress_fwd(xb, pe, w1, w2, *, name):
    G, n, W = xb.shape
    Hc = w1.shape[1]

    def body(xb_ref, pe_ref, w1_ref, w2_ref, kc_ref, hid_ref):
        xv = (xb_ref[0] + pe_ref[...]).astype(_CDT)
        hid = _dot(xv, w1_ref[...].astype(_CDT))
        hid_ref[0] = hid
        kc_ref[0] = _dot(_gelu(hid).astype(_CDT), w2_ref[...].astype(_CDT))

    return pl.pallas_call(
        body, out_shape=(jax.ShapeDtypeStruct((G, n, DH), F32), jax.ShapeDtypeStruct((G, n, Hc), F32)), grid=(G,),
        in_specs=[pl.BlockSpec((1, n, W), lambda g: (g, 0, 0)), pl.BlockSpec((1, W), lambda g: (0, 0)),
                  pl.BlockSpec((W, Hc), lambda g: (0, 0)), pl.BlockSpec((Hc, DH), lambda g: (0, 0))],
        out_specs=(pl.BlockSpec((1, n, DH), lambda g: (g, 0, 0)), pl.BlockSpec((1, n, Hc), lambda g: (g, 0, 0))),
        compiler_params=_params(), name=name)(xb, pe.reshape(1, W), w1, w2)


def _compress_bwd(xb, pe, w1, w2, hid, dkc, *, name):
    G, n, W = xb.shape
    Hc = w1.shape[1]

    def body(xb_ref, pe_ref, w1_ref, w2_ref, hid_ref, dkc_ref, dxb_ref, dw1_ref, dw2_ref, dpe_ref):
        @pl.when(pl.program_id(0) == 0)
        def _():
            dw1_ref[...] = jnp.zeros_like(dw1_ref)
            dw2_ref[...] = jnp.zeros_like(dw2_ref)
            dpe_ref[...] = jnp.zeros_like(dpe_ref)

        xv = (xb_ref[0] + pe_ref[...]).astype(_CDT)
        hid = hid_ref[0]
        dk = dkc_ref[0].astype(_CDT)
        dact = _dot_nt(dk, w2_ref[...].astype(_CDT))
        dhid = (dact * _gelu_grad(hid)).astype(_CDT)
        dw2_ref[...] += _dot_tn(_gelu(hid).astype(_CDT), dk)
        dxb = _dot_nt(dhid, w1_ref[...].astype(_CDT))
        dxb_ref[0] = dxb
        dw1_ref[...] += _dot_tn(xv, dhid)
        dpe_ref[...] += jnp.sum(dxb, axis=0, keepdims=True)

    return pl.pallas_call(
        body, out_shape=(jax.ShapeDtypeStruct((G, n, W), F32), jax.ShapeDtypeStruct((W, Hc), F32),
                         jax.ShapeDtypeStruct((Hc, DH), F32), jax.ShapeDtypeStruct((1, W), F32)), grid=(G,),
        in_specs=[pl.BlockSpec((1, n, W), lambda g: (g, 0, 0)), pl.BlockSpec((1, W), lambda g: (0, 0)),
                  pl.BlockSpec((W, Hc), lambda g: (0, 0)), pl.BlockSpec((Hc, DH), lambda g: (0, 0)),
                  pl.BlockSpec((1, n, Hc), lambda g: (g, 0, 0)), pl.BlockSpec((1, n, DH), lambda g: (g, 0, 0))],
        out_specs=(pl.BlockSpec((1, n, W), lambda g: (g, 0, 0)), pl.BlockSpec((W, Hc), lambda g: (0, 0)),
                   pl.BlockSpec((Hc, DH), lambda g: (0, 0)), pl.BlockSpec((1, W), lambda g: (0, 0))),
        compiler_params=_params(), name=name)(xb, pe.reshape(1, W), w1, w2, hid, dkc)


def _overlap_T(S):
    n_cmp, n_slc = S // 16 - 1, S // 64
    cs = np.arange(n_cmp) * 16
    ce = cs + 32
    ss = np.arange(n_slc) * 64
    se = ss + 64
    ov = np.clip(np.minimum(ce[:, None], se[None, :]) - np.maximum(cs[:, None], ss[None, :]), 0, None) / 32.0
    out = np.zeros((LANES, S // 16), np.float32)
    out[:n_slc, :n_cmp] = ov.T
    return jnp.asarray(out)


def _tri(n, upper):
    r = lax.broadcasted_iota(jnp.int32, (n, n), 0)
    c = lax.broadcasted_iota(jnp.int32, (n, n), 1)
    return ((c >= r) if upper else (c <= r)).astype(jnp.bfloat16)


def _fox_gate_fwd(zf, b, *, name):
    S, H = zf.shape
    nb = S // BQ

    def body(z_ref, b_ref, c_ref):
        tri = _tri(BQ, False)
        carry = jnp.zeros((1, H), F32)
        for i in range(nb):
            z = z_ref[i * BQ:(i + 1) * BQ, :] + b_ref[...]
            lf = jnp.minimum(z, 0.0) - jnp.log(1.0 + jnp.exp(-jnp.abs(z)))
            c_ref[i * BQ:(i + 1) * BQ, :] = _dot_exact_left(tri, lf) + carry
            carry = carry + jnp.sum(lf, axis=0, keepdims=True)

    return pl.pallas_call(body, out_shape=jax.ShapeDtypeStruct((S, H), F32), compiler_params=_params(),
                          name=name)(zf, b)


def _fox_gate_bwd(zf, b, dc, *, name):
    S, H = zf.shape
    nb = S // BQ

    def body(z_ref, b_ref, dc_ref, dz_ref, db_ref):
        tri = _tri(BQ, True)
        carry = jnp.zeros((1, H), F32)
        db = jnp.zeros((1, H), F32)
        for i in reversed(range(nb)):
            rows = slice(i * BQ, (i + 1) * BQ)
            dcb = dc_ref[rows, :]
            dlf = _dot_exact_left(tri, dcb) + carry
            carry = carry + jnp.sum(dcb, axis=0, keepdims=True)
            z = z_ref[rows, :] + b_ref[...]
            dz = dlf * jax.nn.sigmoid(-z)
            dz_ref[rows, :] = dz
            db = db + jnp.sum(dz, axis=0, keepdims=True)
        db_ref[...] = db

    return pl.pallas_call(body, out_shape=(jax.ShapeDtypeStruct((S, H), F32), jax.ShapeDtypeStruct((1, H), F32)),
                          compiler_params=_params(), name=name)(zf, b, dc)


def _to_heads(a):
    S = a.shape[0]
    return a.reshape(S, -1, DH).transpose(1, 0, 2)


def _from_heads(a):
    return a.transpose(1, 0, 2).reshape(a.shape[1], -1)


def _pad_lanes(a):
    return jnp.pad(a, ((0, 0), (0, LANES - a.shape[1])))


def _ffn_fwd(x, g, P, l, tag):
    h = _rms_fwd(x, g, name=tag + "_rms")
    gu = _mm(h, P.weight("gu", l), b_lead=("j",), tm=512, tn=1408, ctx=P.ctx, name=tag + "_gu")
    a = _swiglu_fwd(gu, ctx=P.ctx, name=tag + "_act")
    xo = _mm(a, P.weight("down", l), tm=512, tn=1024, res=x, alpha=0.5, ctx=P.ctx, name=tag + "_down")
    return xo, (x, h, gu, a)


def _ffn_bwd(dxo, saved, g, P, l, tag):
    x, h, gu, a = saved
    da = _mm(dxo, P.weight("down", l), tb=True, tm=512, tn=1408, alpha=0.5, ctx=P.ctx, name=tag + "_dact")
    dwd = _mm(a, dxo, ta=True, tm=1408, tn=1024, alpha=0.5, out_dtype=_WIRE, ctx=P.ctx, name=tag + "_dwd")
    P.grad("down", l, dwd.reshape(4, -1, D_MODEL))
    dgu = _swiglu_bwd(gu, da, ctx=P.ctx, name=tag + "_dgu")
    P.grad("gu", l, _mm(h, dgu, ta=True, tm=512, tn=1408, out_dtype=_WIRE, o_lead=("j",), ctx=P.ctx,
                        name=tag + "_dwgu"))
    dh = _mm(dgu, P.weight("gu", l), tb=True, b_lead=("k",), tm=512, tn=1024, ctx=P.ctx, name=tag + "_dh")
    dx, dg = _rms_bwd(dh, x, g, dxo, name=tag + "_drms")
    return dx, dg


def _nsa_fwd(x, g, w, cc, ss, tag):
    S = x.shape[0]
    h = _rms_fwd(x, g, name=tag + "_rms")
    proj = _mm(h, w["w_in"], tm=512, tn=896, ctx=w["ctx"], name=tag + "_in")
    q = _to_heads(proj[:, :1024])
    kv = proj[:, 1024:2560].reshape(S, 3, 2, NSA_G, DH)
    ks = [kv[:, b, 0].transpose(1, 0, 2) for b in range(3)]
    vs = [kv[:, b, 1].transpose(1, 0, 2) for b in range(3)]
    z = proj[:, 2560:NSA_IN].reshape(S, 3, N_HEADS).transpose(1, 2, 0)[:, :, None, :]
    roped = _rope(jnp.concatenate([q] + ks, axis=0), cc, ss, name=tag + "_rope")
    qr = roped[:N_HEADS].reshape(NSA_G, NSA_R, S, DH)
    kr = [roped[N_HEADS + 4 * b:N_HEADS + 4 * b + 4] for b in range(3)]
    xbk, xbv = _make_xb(kr[0]), _make_xb(vs[0])
    kc, hidk = _compress_fwd(xbk, w["ck_pe"], w["ck_w1"], w["ck_w2"], name=tag + "_ck")
    vc, hidv = _compress_fwd(xbv, w["cv_pe"], w["cv_w1"], w["cv_w2"], name=tag + "_cv")
    ctx = w["ctx"]
    o0, lse0, selT = _attn_fwd(qr, kc, vc, kind="cmp", ovT=_overlap_T(S), ctx=ctx, name=tag + "_cmp")
    o1, lse1 = _attn_fwd(qr, kr[1], vs[1], kind="sel", selT=selT, ctx=ctx, name=tag + "_slc")
    o2, lse2 = _attn_fwd(qr, kr[2], vs[2], kind="band", window=NSA_WINDOW, ctx=ctx, name=tag + "_win")
    o = _combine(o0.reshape(N_HEADS, S, DH), o1.reshape(N_HEADS, S, DH), o2.reshape(N_HEADS, S, DH), z,
                 name=tag + "_mix")
    of = _from_heads(o)
    xo = _mm(of, w["out"], tm=512, tn=1024, res=x, ctx=w["ctx"], name=tag + "_out")
    saved = (x, h, qr, kr, vs, z, xbk, xbv, hidk, hidv, kc, vc, (o0, o1, o2), (lse0, lse1, lse2), selT, of)
    return xo, saved


def _out_bwd(dxo, of, w, tag):
    dof = _mm(dxo, w["out"], tb=True, tm=512, tn=1024, ctx=w["ctx"], name=tag + "_dof")
    dw = _mm(of, dxo, ta=True, tm=512, tn=1024, out_dtype=_WIRE, ctx=w["ctx"], name=tag + "_dwout")
    w["P"].grad("out", w["out_l"], dw.reshape(4, -1, D_MODEL))
    return dof


def _split_cols(dw, n_in):
    cs = n_in // 4
    return dw[:, :n_in].reshape(D_MODEL, 4, cs).transpose(1, 0, 2).astype(_WIRE)


def _nsa_bwd(dxo, saved, g, w, cc, ss, tag):
    x, h, qr, kr, vs, z, xbk, xbv, hidk, hidv, kc, vc, os_, lses, selT, of = saved
    S = x.shape[0]
    dof = _out_bwd(dxo, of, w, tag)
    do = _to_heads(dof).reshape(NSA_G, NSA_R, S, DH)
    zg = z.reshape(3, NSA_G, NSA_R, 1, S)
    ctx = w["ctx"]
    b0 = _attn_bwd(qr, kc, vc, os_[0], do, lses[0], kind="cmp", gate=zg[0], ctx=ctx, name=tag + "_dcmp")
    b1 = _attn_bwd(qr, kr[1], vs[1], os_[1], do, lses[1], kind="sel", selT=selT, gate=zg[1], ctx=ctx,
                   name=tag + "_dslc")
    b2 = _attn_bwd(qr, kr[2], vs[2], os_[2], do, lses[2], kind="band", window=NSA_WINDOW, gate=zg[2], ctx=ctx,
                   name=tag + "_dwin")
    dxbk, dck_w1, dck_w2, dck_pe = _compress_bwd(xbk, w["ck_pe"], w["ck_w1"], w["ck_w2"], hidk, b0["dk"],
                                                 name=tag + "_dck")
    dxbv, dcv_w1, dcv_w2, dcv_pe = _compress_bwd(xbv, w["cv_pe"], w["cv_w1"], w["cv_w2"], hidv, b0["dv"],
                                                 name=tag + "_dcv")
    dk0 = _unmake_xb(dxbk, name=tag + "_dk0")
    dv0 = _unmake_xb(dxbv, name=tag + "_dv0")
    dq = _addn(b0["dq"], b1["dq"], b2["dq"], name=tag + "_dqsum").reshape(N_HEADS, S, DH)
    unroped = _rope(jnp.concatenate([dq, dk0, b1["dk"], b2["dk"]], axis=0), cc, -ss, name=tag + "_drope")
    dks = [unroped[N_HEADS + 4 * b:N_HEADS + 4 * b + 4] for b in range(3)]
    dvs = [dv0, b1["dv"], b2["dv"]]
    dkv = jnp.stack([jnp.stack([dks[b], dvs[b]], axis=0) for b in range(3)], axis=0)
    dkv = dkv.transpose(3, 0, 1, 2, 4).reshape(S, 3 * 2 * NSA_G * DH)
    dz = jnp.stack([b0["dz"], b1["dz"], b2["dz"]], axis=0).reshape(3 * N_HEADS, S).T
    dproj = jnp.concatenate([_from_heads(unroped[:N_HEADS]), dkv, dz, jnp.zeros((S, NSA_IN_PAD - NSA_IN), F32)], axis=1)
    dh = _mm(dproj, w["w_in"], tb=True, tm=512, tn=512, ctx=w["ctx"], name=tag + "_dh")
    dw_in = _mm(h, dproj, ta=True, tm=512, tn=896, ctx=w["ctx"], name=tag + "_dwin_w")
    dx, dg = _rms_bwd(dh, x, g, dxo, name=tag + "_drms")
    P, j = w["P"], w["j"]
    P.grad("nsa_in", j, _split_cols(dw_in, NSA_IN))
    P.grad("cw1", j, dck_w1.astype(_WIRE).reshape(4, -1, dck_w1.shape[1]))
    P.grad("cw1", 2 + j, dcv_w1.astype(_WIRE).reshape(4, -1, dcv_w1.shape[1]))
    grads = dict(ck_pe=dck_pe.reshape(32, DH), ck_w2=dck_w2, cv_pe=dcv_pe.reshape(32, DH), cv_w2=dcv_w2)
    return dx, dg, grads


def _swa_fwd(x, g, w, cc, ss, tag):
    S = x.shape[0]
    h = _rms_fwd(x, g, name=tag + "_rms")
    proj = _mm(h, w["w_in"], tm=512, tn=640, ctx=w["ctx"], name=tag + "_in")
    q = _to_heads(proj[:, :1024])
    kv = proj[:, 1024:].reshape(S, 2, SWA_G, DH)
    k, v = kv[:, 0].transpose(1, 0, 2), kv[:, 1].transpose(1, 0, 2)
    roped = _rope(jnp.concatenate([q, k], axis=0), cc, ss, name=tag + "_rope")
    qr = roped[:N_HEADS].reshape(SWA_G, SWA_R, S, DH)
    kr = roped[N_HEADS:]
    sinks = jnp.broadcast_to(w["sinks"].reshape(SWA_G, SWA_R, 1, 1), (SWA_G, SWA_R, 8, LANES))
    o, lse = _attn_fwd(qr, kr, v, kind="band", window=SWA_WINDOW, sinks=sinks, ctx=w["ctx"], name=tag + "_attn")
    of = _from_heads(o.reshape(N_HEADS, S, DH))
    xo = _mm(of, w["out"], tm=512, tn=1024, res=x, ctx=w["ctx"], name=tag + "_out")
    return xo, (x, h, qr, kr, v, sinks, o, lse, of)


def _swa_bwd(dxo, saved, g, w, cc, ss, tag):
    x, h, qr, kr, v, sinks, o, lse, of = saved
    S = x.shape[0]
    dof = _out_bwd(dxo, of, w, tag)
    do = _to_heads(dof).reshape(SWA_G, SWA_R, S, DH)
    b = _attn_bwd(qr, kr, v, o, do, lse, kind="band", window=SWA_WINDOW, sinks=sinks, ctx=w["ctx"], name=tag + "_dattn")
    unroped = _rope(jnp.concatenate([b["dq"].reshape(N_HEADS, S, DH), b["dk"]], axis=0), cc, -ss, name=tag + "_drope")
    dkv = jnp.stack([unroped[N_HEADS:], b["dv"]], axis=0).transpose(2, 0, 1, 3).reshape(S, 2 * SWA_G * DH)
    dproj = jnp.concatenate([_from_heads(unroped[:N_HEADS]), dkv], axis=1)
    dh = _mm(dproj, w["w_in"], tb=True, tm=512, tn=512, ctx=w["ctx"], name=tag + "_dh")
    dw_in = _mm(h, dproj, ta=True, tm=512, tn=640, ctx=w["ctx"], name=tag + "_dwin_w")
    dx, dg = _rms_bwd(dh, x, g, dxo, name=tag + "_drms")
    w["P"].grad("swa_in", w["j"], _split_cols(dw_in, SWA_IN))
    return dx, dg, dict(sinks=b["dsink"][:, :, 0, 0].reshape(N_HEADS))


def _fox_fwd(x, g, w, tag):
    S = x.shape[0]
    h = _rms_fwd(x, g, name=tag + "_rms")
    proj = _mm(h, w["w_in"], tm=512, tn=640, ctx=w["ctx"], name=tag + "_in")
    q, k, v = (_to_heads(proj[:, i * 1024:(i + 1) * 1024]) for i in range(3))
    zf = _pad_lanes(proj[:, 3072:FOX_IN])
    bf = _pad_lanes(w["b_f"].reshape(1, N_HEADS))
    c = _fox_gate_fwd(zf, bf, name=tag + "_gate")[:, :N_HEADS]
    bias = c.T[:, None, :]
    q4 = q.reshape(N_HEADS, 1, S, DH)
    o, lse = _attn_fwd(q4, k, v, kind="causal", bias=bias, ctx=w["ctx"], name=tag + "_attn")
    of = _from_heads(o.reshape(N_HEADS, S, DH))
    xo = _mm(of, w["out"], tm=512, tn=1024, res=x, ctx=w["ctx"], name=tag + "_out")
    return xo, (x, h, q4, k, v, zf, bf, bias, o, lse, of)


def _fox_bwd(dxo, saved, g, w, tag):
    x, h, q4, k, v, zf, bf, bias, o, lse, of = saved
    S = x.shape[0]
    dof = _out_bwd(dxo, of, w, tag)
    do = _to_heads(dof).reshape(N_HEADS, 1, S, DH)
    b = _attn_bwd(q4, k, v, o, do, lse, kind="causal", bias=bias, ctx=w["ctx"], name=tag + "_dattn")
    dzf, db = _fox_gate_bwd(zf, bf, _pad_lanes(b["dc"].reshape(N_HEADS, S).T), name=tag + "_dgate")
    dproj = jnp.concatenate([_from_heads(b["dq"].reshape(N_HEADS, S, DH)), _from_heads(b["dk"]), _from_heads(b["dv"]),
                             dzf[:, :N_HEADS], jnp.zeros((S, FOX_IN_PAD - FOX_IN), F32)], axis=1)
    dh = _mm(dproj, w["w_in"], tb=True, tm=512, tn=512, ctx=w["ctx"], name=tag + "_dh")
    dw_in = _mm(h, dproj, ta=True, tm=512, tn=640, ctx=w["ctx"], name=tag + "_dwin_w")
    dx, dg = _rms_bwd(dh, x, g, dxo, name=tag + "_drms")
    w["P"].grad("fox_in", w["j"], _split_cols(dw_in, FOX_IN))
    return dx, dg, dict(b_f=db[0, :N_HEADS])


GROUPS = {
    "gu": (("ffn1_w_gu", "ffn2_w_gu"), 2),
    "down": (("ffn1_w_down", "ffn2_w_down"), 1),
    "out": (("nsa_w_out", "swa_w_out", "fox_w_out"), 1),
    "cw1": (("nsa_ck_w1", "nsa_cv_w1"), 1),
    "nsa_in": (("nsa_w_in",), 2),
    "swa_in": (("swa_w_in",), 2),
    "fox_in": (("fox_w_in",), 2),
}
OUT_SLAB = {0: 0, 3: 1, 1: 2, 2: 3}


def _pieces_in_order():
    chunks = []
    for i in range(DEPTH):
        kind, j = i % 3, i // 3
        chunks.append([("gu", i), ("down", i)])
        if kind == 0:
            chunks.append([("nsa_in", j), ("cw1", j), ("cw1", 2 + j), ("out", OUT_SLAB[i])])
        else:
            chunks.append([("swa_in" if kind == 1 else "fox_in", j), ("out", OUT_SLAB[i])])
        chunks.append([("gu", DEPTH + i), ("down", DEPTH + i)])
    return chunks


def _consumer_layout(group, F):
    _, rows, C = F.shape
    if group == "gu":
        return F
    if GROUPS[group][1] == 1:
        return F.reshape(4 * rows, C)
    w = F.transpose(1, 0, 2).reshape(rows, 4 * C)
    pad = {"nsa_in": NSA_IN_PAD, "swa_in": SWA_IN, "fox_in": FOX_IN_PAD}[group] - 4 * C
    return jnp.pad(w, ((0, 0), (0, pad)))


class _Given:
    def __init__(self, pieces, small):
        self.pieces, self.small, self.ctx, self.grads = pieces, small, None, {}

    def weight(self, group, l):
        return _consumer_layout(group, self.pieces[group, l])

    def grad(self, group, l, G):
        self.grads[group, l] = G


class _MixerWeights(dict):
    def __init__(self, P, pieces, **given):
        super().__init__(P=P, ctx=P.ctx, **given)
        self.pieces = pieces

    def __missing__(self, key):
        self[key] = self["P"].weight(*self.pieces[key])
        return self[key]


def _mixer_weights(P, i):
    kind, j = i % 3, i // 3
    out = {"out": ("out", OUT_SLAB[i])}
    if kind == 0:
        small = {k: P.small["nsa_" + k][j] for k in ("ck_pe", "ck_w2", "cv_pe", "cv_w2")}
        return _MixerWeights(P, dict(out, w_in=("nsa_in", j), ck_w1=("cw1", j), cv_w1=("cw1", 2 + j)), j=j,
                             out_l=OUT_SLAB[i], **small)
    if kind == 1:
        return _MixerWeights(P, dict(out, w_in=("swa_in", j)), j=j, out_l=OUT_SLAB[i], sinks=P.small["swa_sinks"][j])
    return _MixerWeights(P, dict(out, w_in=("fox_in", j)), j=j, out_l=OUT_SLAB[i], b_f=P.small["fox_b_f"][j])


def _local_step(x, tgt, P):
    S = x.shape[0]
    cc, ss = _rope_tables(S)
    sm = P.small
    saved = []
    for i in range(DEPTH):
        kind = i % 3
        x, s1 = _ffn_fwd(x, sm["ffn1_norm"][i], P, i, f"l{i}f1")
        mw = _mixer_weights(P, i)
        if kind == 0:
            x, s2 = _nsa_fwd(x, sm["mix_norm"][i], mw, cc, ss, f"l{i}nsa")
        elif kind == 1:
            x, s2 = _swa_fwd(x, sm["mix_norm"][i], mw, cc, ss, f"l{i}swa")
        else:
            x, s2 = _fox_fwd(x, sm["mix_norm"][i], mw, f"l{i}fox")
        x, s3 = _ffn_fwd(x, sm["ffn2_norm"][i], P, DEPTH + i, f"l{i}f2")
        saved.append((s1, mw, s2, s3))
    loss, dx, d_final = _loss_head(x, sm["final_norm"], tgt, name="loss_head")

    norms = {k: [None] * DEPTH for k in ("ffn1_norm", "mix_norm", "ffn2_norm")}
    mix = {}
    for i in reversed(range(DEPTH)):
        kind, j = i % 3, i // 3
        s1, mw, s2, s3 = saved[i]
        dx, norms["ffn2_norm"][i] = _ffn_bwd(dx, s3, sm["ffn2_norm"][i], P, DEPTH + i, f"l{i}f2")
        if kind == 0:
            dx, dg, gm = _nsa_bwd(dx, s2, sm["mix_norm"][i], mw, cc, ss, f"l{i}nsa")
            pre = "nsa_"
        elif kind == 1:
            dx, dg, gm = _swa_bwd(dx, s2, sm["mix_norm"][i], mw, cc, ss, f"l{i}swa")
            pre = "swa_"
        else:
            dx, dg, gm = _fox_bwd(dx, s2, sm["mix_norm"][i], mw, f"l{i}fox")
            pre = "fox_"
        norms["mix_norm"][i] = dg
        for k, val in gm.items():
            mix.setdefault(pre + k, {})[j] = val
        dx, norms["ffn1_norm"][i] = _ffn_bwd(dx, s1, sm["ffn1_norm"][i], P, i, f"l{i}f1")
    small = {k: jnp.stack(v, axis=0) for k, v in norms.items()}
    small.update({k: jnp.stack([d[j] for j in sorted(d)], axis=0) for k, d in mix.items()})
    small["final_norm"] = d_final
    return loss, dx, small


def _chip_peers():
    x, y, c = lax.axis_index("x"), lax.axis_index("y"), lax.axis_index("c")
    return x, y, c, [(1 - x, y), (x, 1 - y), (1 - x, 1 - y)]


_ANY = pl.BlockSpec(memory_space=pl.ANY)


def _gather_groups(shards):
    n = len(shards)

    def body(*refs):
        srcs, outs = refs[:n], refs[n:2 * n]
        ici_send, ici_recv, d2d_send, d2d_recv, local_sems = refs[2 * n:]
        x, y, c, chips = _chip_peers()
        me = 2 * x + y
        sibling = (x, y, 1 - c)
        local = [pltpu.make_async_copy(srcs[g], outs[g].at[:, me], local_sems.at[g]) for g in range(n)]
        for cp in local:
            cp.start()
        sends = [pltpu.make_async_remote_copy(src_ref=srcs[g].at[:, c], dst_ref=outs[g].at[:, me, c],
                                              send_sem=ici_send.at[g, j], recv_sem=ici_recv.at[g, j],
                                              device_id=(px, py, c), device_id_type=MESH)
                 for g in range(n) for j, (px, py) in enumerate(chips)]
        for cp in sends:
            cp.start()
        passed = []
        for g in range(n):
            for j, (px, py) in enumerate(chips):
                landed = outs[g].at[:, 2 * px + py, c]
                pltpu.make_async_remote_copy(src_ref=landed, dst_ref=landed, send_sem=ici_send.at[g, j],
                                             recv_sem=ici_recv.at[g, j], device_id=(px, py, c),
                                             device_id_type=MESH).wait_recv()
                cp = pltpu.make_async_remote_copy(src_ref=landed, dst_ref=landed, send_sem=d2d_send.at[g, j],
                                                  recv_sem=d2d_recv.at[g, j], device_id=sibling, device_id_type=MESH)
                cp.start()
                passed.append(cp)
        for g in range(n):
            for j, (px, py) in enumerate(chips):
                theirs = outs[g].at[:, 2 * px + py, 1 - c]
                pltpu.make_async_remote_copy(src_ref=theirs, dst_ref=theirs, send_sem=d2d_send.at[g, j],
                                             recv_sem=d2d_recv.at[g, j], device_id=sibling,
                                             device_id_type=MESH).wait_recv()
        for cp in sends + passed:
            cp.wait_send()
        for cp in local:
            cp.wait()

    out_shape = tuple(jax.ShapeDtypeStruct((s.shape[0], 4) + s.shape[1:], s.dtype) for s in shards)
    return pl.pallas_call(
        body, out_shape=out_shape, in_specs=[_ANY] * n, out_specs=(_ANY,) * n,
        scratch_shapes=[pltpu.SemaphoreType.DMA((n, 3)), pltpu.SemaphoreType.DMA((n, 3)),
                        pltpu.SemaphoreType.DMA((n, 3)), pltpu.SemaphoreType.DMA((n, 3)),
                        pltpu.SemaphoreType.DMA((n,))],
        compiler_params=pltpu.CompilerParams(has_side_effects=True), name="gather_weights")(*shards)


def _scatter_groups(sends):
    n = len(sends)

    def body(*refs):
        srcs, outs = refs[:n], refs[n:2 * n]
        send_sems, recv_sems, local_sems = refs[2 * n:]
        x, y, c, chips = _chip_peers()
        me = 2 * x + y
        local = [pltpu.make_async_copy(srcs[g].at[:, me], outs[g].at[me], local_sems.at[g]) for g in range(n)]
        for cp in local:
            cp.start()
        cps = [pltpu.make_async_remote_copy(src_ref=srcs[g].at[:, 2 * px + py], dst_ref=outs[g].at[me],
                                            send_sem=send_sems.at[g, j], recv_sem=recv_sems.at[g, j],
                                            device_id=(px, py, c), device_id_type=MESH)
               for g in range(n) for j, (px, py) in enumerate(chips)]
        for cp in cps:
            cp.start()
        for g in range(n):
            for j, (px, py) in enumerate(chips):
                slot = outs[g].at[2 * px + py]
                pltpu.make_async_remote_copy(src_ref=slot, dst_ref=slot, send_sem=send_sems.at[g, j],
                                             recv_sem=recv_sems.at[g, j], device_id=(px, py, c),
                                             device_id_type=MESH).wait_recv()
        for cp in cps:
            cp.wait_send()
        for cp in local:
            cp.wait()

    out_shape = tuple(jax.ShapeDtypeStruct((4, s.shape[0]) + s.shape[2:], s.dtype) for s in sends)
    return pl.pallas_call(
        body, out_shape=out_shape, in_specs=[_ANY] * n, out_specs=(_ANY,) * n,
        scratch_shapes=[pltpu.SemaphoreType.DMA((n, 3)), pltpu.SemaphoreType.DMA((n, 3)),
                        pltpu.SemaphoreType.DMA((n,))],
        compiler_params=pltpu.CompilerParams(has_side_effects=True), name="scatter_grads")(*sends)


def _sum_slots(own, recv, me, into, row0, *, name):
    _, R, C = recv.shape
    tr = _row_tile(math.gcd(R, row0), C)
    blk0 = row0 // tr

    def body(_, g_ref, r_ref, __, o_ref):
        acc = g_ref[...].astype(F32) + r_ref[0].astype(F32)
        acc = acc + r_ref[1].astype(F32)
        o_ref[...] = acc + r_ref[2].astype(F32)

    spec = pltpu.PrefetchScalarGridSpec(
        num_scalar_prefetch=1, grid=(R // tr,),
        in_specs=[pl.BlockSpec((None, tr, C), lambda i, me_ref: (me_ref[0], i, 0)),
                  pl.BlockSpec((3, tr, C), lambda i, me_ref: (0, i, 0)), _ANY],
        out_specs=pl.BlockSpec((tr, C), lambda i, me_ref: (blk0 + i, 0)))
    return pl.pallas_call(body, grid_spec=spec, out_shape=jax.ShapeDtypeStruct(into.shape, F32),
                          input_output_aliases={3: 0}, compiler_params=_params(), name=name)(me, own, recv, into)


def _swap_sibling(parts):
    n = len(parts)

    def body(*refs):
        srcs, outs = refs[:n], refs[n:2 * n]
        send_sems, recv_sems = refs[2 * n:]
        x, y, c = lax.axis_index("x"), lax.axis_index("y"), lax.axis_index("c")
        cps = [pltpu.make_async_remote_copy(src_ref=srcs[g], dst_ref=outs[g], send_sem=send_sems.at[g],
                                            recv_sem=recv_sems.at[g], device_id=(x, y, 1 - c), device_id_type=MESH)
               for g in range(n)]
        for cp in cps:
            cp.start()
        for cp in cps:
            cp.wait()

    return pl.pallas_call(
        body, out_shape=tuple(jax.ShapeDtypeStruct(p.shape, p.dtype) for p in parts),
        in_specs=[_ANY] * n, out_specs=(_ANY,) * n,
        scratch_shapes=[pltpu.SemaphoreType.DMA((n,)), pltpu.SemaphoreType.DMA((n,))],
        compiler_params=pltpu.CompilerParams(has_side_effects=True), name="swap_core_grads")(*parts)


def _flip(coord, bit):
    return 1 - coord if bit else coord


def _allreduce_small(v):
    n, C = v.shape

    def body(v_ref, o_ref, buf, send_sems, recv_sems):
        x, y, c = lax.axis_index("x"), lax.axis_index("y"), lax.axis_index("c")
        me = 4 * x + 2 * y + c
        buf[me] = v_ref[...]
        peers = [(_flip(x, (j >> 2) & 1), _flip(y, (j >> 1) & 1), _flip(c, j & 1)) for j in range(1, 8)]
        sends = [pltpu.make_async_remote_copy(src_ref=v_ref, dst_ref=buf.at[me], send_sem=send_sems.at[j],
                                              recv_sem=recv_sems.at[j], device_id=peer, device_id_type=MESH)
                 for j, peer in enumerate(peers)]
        for cp in sends:
            cp.start()
        for j, (px, py, pc) in enumerate(peers):
            pltpu.make_async_remote_copy(src_ref=v_ref, dst_ref=buf.at[4 * px + 2 * py + pc], send_sem=send_sems.at[j],
                                         recv_sem=recv_sems.at[j], device_id=(px, py, pc),
                                         device_id_type=MESH).wait_recv()
        for cp in sends:
            cp.wait_send()
        acc = buf[0]
        for d in range(1, 8):
            acc = acc + buf[d]
        o_ref[...] = acc

    return pl.pallas_call(
        body, out_shape=jax.ShapeDtypeStruct((n, C), F32),
        in_specs=[pl.BlockSpec(memory_space=pltpu.VMEM)], out_specs=pl.BlockSpec(memory_space=pltpu.VMEM),
        scratch_shapes=[pltpu.VMEM((8, n, C), F32), pltpu.SemaphoreType.DMA((7,)), pltpu.SemaphoreType.DMA((7,))],
        compiler_params=pltpu.CompilerParams(has_side_effects=True), name="allreduce_small")(v)


def _adamw(w, m, v, gs, *, row0=0, name):
    shape = w.shape
    w2, m2, v2 = _rows2d(w), _rows2d(m), _rows2d(v)
    g2 = [_rows2d(g) for g in gs]
    R, C = w2.shape
    tr = _row_tile(math.gcd(R, row0), C, budget=1024 * 1024)
    ng = len(g2)
    blk0 = row0 // tr

    def body(*refs):
        w_ref, m_ref, v_ref = refs[:3]
        g = refs[3][...]
        for r in refs[4:3 + ng]:
            g = g + r[...]
        g_ref, d_ref, nm_ref, nv_ref = refs[3 + ng:]
        mn = B1 * m_ref[...] + (1.0 - B1) * g
        vn = B2 * v_ref[...] + (1.0 - B2) * (g * g)
        m_hat = mn / (1.0 - B1 ** STEP)
        v_hat = vn / (1.0 - B2 ** STEP)
        g_ref[...] = g
        d_ref[...] = -LR * (m_hat / (jnp.sqrt(v_hat) + EPS) + WD * w_ref[...])
        nm_ref[...] = mn
        nv_ref[...] = vn

    spec = pl.BlockSpec((tr, C), lambda i: (i, 0))
    gspec = pl.BlockSpec((tr, C), lambda i: (blk0 + i, 0))
    outs = pl.pallas_call(body, out_shape=tuple(jax.ShapeDtypeStruct((R, C), F32) for _ in range(4)), grid=(R // tr,),
                          in_specs=[spec] * 3 + [gspec] * ng, out_specs=(spec,) * 4, compiler_params=_params(),
                          name=name)(w2, m2, v2, *g2)
    return tuple(o.reshape(shape) for o in outs)


def _small_layout(shapes):
    offs, off = {}, 0
    for k in REPLICATED:
        n = int(np.prod(shapes[k]))
        offs[k] = (off, n)
        off += -(-n // LANES) * LANES
    return offs, off


def _pack_small(d, shapes):
    offs, total = _small_layout(shapes)
    parts = []
    for k in REPLICATED:
        n = offs[k][1]
        parts.append(jnp.pad(d[k].reshape(-1).astype(F32), (0, -(-n // LANES) * LANES - n)))
    rows = -(-(total // LANES) // 8) * 8
    return jnp.pad(jnp.concatenate(parts), (0, rows * LANES - total)).reshape(rows, LANES)


def _unpack_small(a, shapes):
    offs, _ = _small_layout(shapes)
    flat = a.reshape(-1)
    return {k: flat[offs[k][0]:offs[k][0] + offs[k][1]].reshape(shapes[k]) for k in REPLICATED}


def _group_shards(w):
    shards = []
    for members, _ in GROUPS.values():
        s = jnp.concatenate([w[k].astype(_WIRE) for k in members], axis=0)
        shards.append(s.reshape(s.shape[0], 2, s.shape[1] // 2, s.shape[2]))
    return shards


class _Exchanged:
    def __init__(self, w):
        self.small = {k: w[k] for k in REPLICATED}
        self.ctx = _Sched()
        shards = dict(zip(GROUPS, _group_shards(w)))
        self.gather, self.cache, self.scatter = {}, {}, {}
        chunks = _pieces_in_order()
        self.me = jnp.reshape(2 * lax.axis_index("x") + lax.axis_index("y"), (1,)).astype(jnp.int32)
        first = {(g, l): _gather_task1(shards[g], l, _place_own(shards[g], l, self.me, name=f"own_{g}{l}"))
                 for chunk in chunks for g, l in chunk}
        self.gather = {p: _gather_task2(task) for p, task in first.items()}
        for n, chunk in enumerate(chunks):
            for p in chunk:
                self.ctx.push(first[p])
            for p in (chunks[n - 1] if n else []):
                self.ctx.push(self.gather[p])
        for p in chunks[-1]:
            self.ctx.push(self.gather[p])

    def weight(self, group, l):
        if (group, l) not in self.cache:
            task = self.gather[group, l]
            self.ctx.finish(task)
            F = task.result()
            self.cache[group, l] = _consumer_layout(group, F.reshape(4, -1, F.shape[-1]))
        return self.cache[group, l]

    def grad(self, group, l, G):
        self.scatter[group, l] = (G, self.ctx.push(_scatter_task(G)))

    def partial_sums(self, w):
        self.ctx.finish()
        parts = []
        for group, (members, _) in GROUPS.items():
            rows, C = w[members[0]].shape[1:]
            n = sum(w[k].shape[0] for k in members)
            part = jnp.zeros((n * rows, C), F32)
            for l in range(n):
                G, task = self.scatter[group, l]
                part = _sum_slots(G, task.result(), self.me, part, l * rows, name=f"sum_{group}{l}")
            parts.append(part)
        return parts


def _reduce_and_update(w, m, v, parts, small):
    others = _swap_sibling(parts)
    small_shapes = {k: w[k].shape for k in REPLICATED}
    g_small = _unpack_small(_allreduce_small(_pack_small(small, small_shapes)), small_shapes)

    out_g, out_d, out_m, out_v = {}, {}, {}, {}
    for (members, _), part, other in zip(GROUPS.values(), parts, others):
        row0 = 0
        for k in members:
            out_g[k], out_d[k], out_m[k], out_v[k] = _adamw(w[k], m[k], v[k], [part, other], row0=row0,
                                                            name="adamw_" + k)
            row0 += w[k].shape[0] * w[k].shape[1]
    sm = _adamw(_pack_small(w, small_shapes), _pack_small(m, small_shapes), _pack_small(v, small_shapes),
                [_pack_small(g_small, small_shapes)], name="adamw_small")
    for d, packed in zip((out_g, out_d, out_m, out_v), sm):
        d.update(_unpack_small(packed, small_shapes))
    return out_g, out_d, out_m, out_v


def kernel(x, ffn1_norm, ffn1_w_gu, ffn1_w_down, mix_norm, ffn2_norm, ffn2_w_gu, ffn2_w_down, nsa_w_in, nsa_ck_pe, nsa_ck_w1, nsa_ck_w2, nsa_cv_pe, nsa_cv_w1, nsa_cv_w2, nsa_w_out, swa_w_in, swa_sinks, swa_w_out, fox_w_in, fox_b_f, fox_w_out, final_norm, loss_target, m_ffn1_norm, m_ffn1_w_gu, m_ffn1_w_down, m_mix_norm, m_ffn2_norm, m_ffn2_w_gu, m_ffn2_w_down, m_nsa_w_in, m_nsa_ck_pe, m_nsa_ck_w1, m_nsa_ck_w2, m_nsa_cv_pe, m_nsa_cv_w1, m_nsa_cv_w2, m_nsa_w_out, m_swa_w_in, m_swa_sinks, m_swa_w_out, m_fox_w_in, m_fox_b_f, m_fox_w_out, m_final_norm, v_ffn1_norm, v_ffn1_w_gu, v_ffn1_w_down, v_mix_norm, v_ffn2_norm, v_ffn2_w_gu, v_ffn2_w_down, v_nsa_w_in, v_nsa_ck_pe, v_nsa_ck_w1, v_nsa_ck_w2, v_nsa_cv_pe, v_nsa_cv_w1, v_nsa_cv_w2, v_nsa_w_out, v_swa_w_in, v_swa_sinks, v_swa_w_out, v_fox_w_in, v_fox_b_f, v_fox_w_out, v_final_norm):
    args = dict(locals())
    w = {k: args[k] for k in WEIGHTS}
    m = {k: args["m_" + k] for k in WEIGHTS}
    v = {k: args["v_" + k] for k in WEIGHTS}
    P = _Exchanged(w)
    loss_part, dx, small = _local_step(x[0], loss_target[0], P)
    loss = lax.psum(loss_part, ("x", "y", "c"))
    out_g, out_d, out_m, out_v = _reduce_and_update(w, m, v, P.partial_sums(w), small)
    return (loss, dx[None], *[out_g[k] for k in WEIGHTS], *[out_d[k] for k in WEIGHTS],
            *[out_m[k] for k in WEIGHTS], *[out_v[k] for k in WEIGHTS])
```

```python
import math

import numpy as np
import jax
import jax.numpy as jnp
from jax import lax
from jax.experimental import pallas as pl
from jax.experimental.pallas import tpu as pltpu

F32 = jnp.float32
_CDT = jnp.bfloat16
_WIRE = jnp.bfloat16
_VMEM_LIMIT = 56 * 1024 * 1024

D_MODEL = 1024
DEPTH = 4
DH = 64
N_HEADS = 16
RMS_EPS = 1e-6
NEG = -1e30
SCALE = DH ** -0.5
BQ = 256
LANES = 128
NSA_G, NSA_R = 4, 4
NSA_WINDOW = 512
NSA_TOPK = 16
NSA_BONUS = 1e4
SWA_G, SWA_R = 2, 8
SWA_WINDOW = 128
NSA_IN, NSA_IN_PAD = 2608, 2688
SWA_IN = 1280
FOX_IN, FOX_IN_PAD = 3088, 3200
LR, B1, B2, EPS, WD, STEP = 0.001, 0.9, 0.999, 1e-08, 0.01, 10
MESH = pl.DeviceIdType.MESH

REPLICATED = ["ffn1_norm", "mix_norm", "ffn2_norm", "nsa_ck_pe", "nsa_ck_w2", "nsa_cv_pe", "nsa_cv_w2",
              "swa_sinks", "fox_b_f", "final_norm"]
WEIGHTS = ['ffn1_norm', 'ffn1_w_gu', 'ffn1_w_down', 'mix_norm', 'ffn2_norm', 'ffn2_w_gu', 'ffn2_w_down',
           'nsa_w_in', 'nsa_ck_pe', 'nsa_ck_w1', 'nsa_ck_w2', 'nsa_cv_pe', 'nsa_cv_w1', 'nsa_cv_w2', 'nsa_w_out',
           'swa_w_in', 'swa_sinks', 'swa_w_out', 'fox_w_in', 'fox_b_f', 'fox_w_out', 'final_norm']


def _params(**kw):
    return pltpu.CompilerParams(vmem_limit_bytes=_VMEM_LIMIT, **kw)


def _dot(a, b):
    return lax.dot_general(a, b, (((1,), (0,)), ((), ())), preferred_element_type=F32)


def _dot_nt(a, b):
    return lax.dot_general(a, b, (((1,), (1,)), ((), ())), preferred_element_type=F32)


def _dot_tn(a, b):
    return lax.dot_general(a, b, (((0,), (0,)), ((), ())), preferred_element_type=F32)


def _split3(x):
    hi = x.astype(jnp.bfloat16)
    r1 = x - hi.astype(F32)
    mid = r1.astype(jnp.bfloat16)
    lo = (r1 - mid.astype(F32)).astype(jnp.bfloat16)
    return hi, mid, lo


def _dot_exact(x, p):
    hi, mid, lo = _split3(x)
    return _dot(hi, p) + _dot(mid, p) + _dot(lo, p)


def _dot_exact_left(p, x):
    hi, mid, lo = _split3(x)
    return _dot(p, hi) + _dot(p, mid) + _dot(p, lo)


_ANY = pl.BlockSpec(memory_space=pl.ANY)


def _chip_peers():
    x, y, c = lax.axis_index("x"), lax.axis_index("y"), lax.axis_index("c")
    return x, y, c, [(1 - x, y), (x, 1 - y), (1 - x, 1 - y)]


class _Job:
    def __init__(self, ins, outs, nsem, copies, cost, alias):
        self.ins, self.outs, self.nsem, self.copies, self.cost, self.alias = ins, outs, nsem, copies, cost, alias
        self.results = None

    def inputs(self):
        return self.ins


class _Task:
    def __init__(self, src, out, parts, init=None):
        self.src, self.out, self.parts, self.init, self.done, self.jobs = src, out, parts, init, 0, []

    def result(self):
        return self.jobs[-1].results[0]

    def complete(self):
        return self.done == len(self.parts) and self.jobs[-1].results is not None

    def ready(self):
        if isinstance(self.src, _Task) and not self.src.complete():
            return False
        return not self.jobs or self.jobs[-1].results is not None

    def next_job(self, n):
        parts = self.parts[self.done:self.done + n]
        in_place = isinstance(self.src, _Task)
        ins = [] if in_place else [self.src]
        prev = self.jobs[-1].results[0] if self.jobs else (self.src.result() if in_place else self.init)
        alias = {}
        if prev is not None:
            alias = {len(ins): 0}
            ins = ins + [prev]

        def copies(in_refs, out_refs, sems):
            src = out_refs[0] if in_place else in_refs[0]
            out, first = [], 0
            for build, nsem, _ in parts:
                out += build(src, out_refs[0], sems, first)
                first += nsem
            return out

        job = _Job(ins, [self.out], sum(p[1] for p in parts), copies, sum(p[2] for p in parts), alias)
        self.done += n
        self.jobs.append(job)
        return job


def _pcall(body, *, out_shape, in_specs, out_specs, name, grid=(), scratch_shapes=(), aliases=None, jobs=(), ins):
    single = not isinstance(out_shape, (tuple, list))
    out_shape = (out_shape,) if single else tuple(out_shape)
    out_specs = (out_specs,) if single else tuple(out_specs)
    n_in, n_out, n_scr = len(ins), len(out_shape), len(scratch_shapes)
    side_ins = [a for j in jobs for a in j.inputs()]
    side_outs = [o for j in jobs for o in j.outs]
    aliases = dict(aliases or {})
    i0, o0 = n_in, n_out
    for j in jobs:
        for a, b in j.alias.items():
            aliases[i0 + a] = o0 + b
        i0 += len(j.ins)
        o0 += len(j.outs)

    def body2(*refs):
        in_refs, s_in = refs[:n_in], refs[n_in:n_in + len(side_ins)]
        r = n_in + len(side_ins)
        out_refs, s_out = refs[r:r + n_out], refs[r + n_out:r + n_out + len(side_outs)]
        r += n_out + len(side_outs)
        scr, sems = refs[r:r + n_scr], refs[r + n_scr:]
        def descriptors():
            out, a, b = [], 0, 0
            for j, sem in zip(jobs, sems):
                out += j.copies(s_in[a:a + len(j.ins)], s_out[b:b + len(j.outs)], sem)
                a += len(j.ins)
                b += len(j.outs)
            return out

        def at_step(steps, fn):
            cond = None
            for d, s in enumerate(steps):
                hit = pl.program_id(d) == s
                cond = hit if cond is None else jnp.logical_and(cond, hit)
            if cond is None:
                fn()
            else:
                pl.when(cond)(fn)

        def start_all():
            for cp in descriptors():
                cp.start()

        def wait_all():
            for cp in descriptors():
                cp.wait()

        if jobs:
            at_step([0] * len(grid), start_all)
        body(*in_refs, *out_refs, *scr)
        if jobs:
            at_step([n - 1 for n in grid], wait_all)

    res = pl.pallas_call(
        body2, out_shape=out_shape + tuple(side_outs), grid=grid, in_specs=list(in_specs) + [_ANY] * len(side_ins),
        out_specs=out_specs + (_ANY,) * len(side_outs),
        scratch_shapes=list(scratch_shapes) + [pltpu.SemaphoreType.DMA((j.nsem,)) for j in jobs],
        input_output_aliases=aliases,
        compiler_params=_params(has_side_effects=True) if jobs else _params(), name=name)(*ins, *side_ins)
    b = n_out
    for j in jobs:
        j.results = list(res[b:b + len(j.outs)])
        b += len(j.outs)
    return res[0] if single else tuple(res[:n_out])


def _comm_call(jobs, name):
    def body():
        pass

    _pcall(body, out_shape=(), in_specs=[], out_specs=(), name=name, jobs=jobs, ins=[])


def _place_own(shard, l, me, *, name):
    _, _, h, C = shard.shape

    def body(_, s_ref, o_ref):
        o_ref[...] = s_ref[...]

    spec = pltpu.PrefetchScalarGridSpec(
        num_scalar_prefetch=1, grid=(2,),
        in_specs=[pl.BlockSpec((None, None, h, C), lambda i, me_ref: (l, i, 0, 0))],
        out_specs=pl.BlockSpec((None, None, h, C), lambda i, me_ref: (me_ref[0], i, 0, 0)))
    return pl.pallas_call(body, grid_spec=spec, out_shape=jax.ShapeDtypeStruct((4, 2, h, C), shard.dtype),
                          compiler_params=_params(), name=name)(me, shard)


def _gather_task1(shard, l, own):
    _, _, h, C = shard.shape
    mb = 3 * h * C * jnp.dtype(_WIRE).itemsize / 1e6
    n = _row_splits(h, ICI_US_PER_MB * mb)
    hr = h // n

    def rows_to_peers(r):
        def build(src, dst, sems, s0):
            x, y, c, chips = _chip_peers()
            me, rows = 2 * x + y, pl.ds(r * hr, hr)
            return [pltpu.make_async_remote_copy(src_ref=src.at[l, c, rows], dst_ref=dst.at[me, c, rows],
                                                 send_sem=sems.at[s0 + 2 * j], recv_sem=sems.at[s0 + 2 * j + 1],
                                                 device_id=(px, py, c), device_id_type=MESH)
                    for j, (px, py) in enumerate(chips)]
        return build

    parts = [(rows_to_peers(r), 6, ICI_US_PER_MB * mb / n) for r in range(n)]
    return _Task(shard, jax.ShapeDtypeStruct((4, 2, h, C), shard.dtype), parts, init=own)


def _gather_task2(task1):
    _, _, h, C = task1.out.shape
    mb = h * C * jnp.dtype(_WIRE).itemsize / 1e6

    def forward(j):
        def build(_, dst, sems, s0):
            x, y, c, chips = _chip_peers()
            px, py = chips[j]
            landed = dst.at[2 * px + py, c]
            return [pltpu.make_async_remote_copy(src_ref=landed, dst_ref=landed, send_sem=sems.at[s0],
                                                 recv_sem=sems.at[s0 + 1], device_id=(x, y, 1 - c),
                                                 device_id_type=MESH)]
        return build

    return _Task(task1, task1.out, [(forward(j), 2, 1.0 + D2D_US_PER_MB * mb) for j in range(3)])


def _scatter_task(G):
    mb = 3 * G.shape[1] * G.shape[2] * jnp.dtype(_WIRE).itemsize / 1e6
    n = _row_splits(G.shape[1], ICI_US_PER_MB * mb)
    rr = G.shape[1] // n

    def rows_to_peers(r):
        def build(src, dst, sems, s0):
            _, _, c, chips = _chip_peers()
            rows = pl.ds(r * rr, rr)
            return [pltpu.make_async_remote_copy(src_ref=src.at[2 * px + py, rows], dst_ref=dst.at[j, rows],
                                                 send_sem=sems.at[s0 + 2 * j], recv_sem=sems.at[s0 + 2 * j + 1],
                                                 device_id=(px, py, c), device_id_type=MESH)
                    for j, (px, py) in enumerate(chips)]
        return build

    parts = [(rows_to_peers(r), 6, ICI_US_PER_MB * mb / n) for r in range(n)]
    return _Task(G, jax.ShapeDtypeStruct((3,) + G.shape[1:], G.dtype), parts)


def _row_splits(rows, cost):
    n = 8
    while n > 1 and (rows % (16 * n) or cost / n < PART_US):
        n //= 2
    return n


CARRIER_US = {
    "f_gu": 48, "f_down": 20, "f_dact": 42, "f_dwd": 20, "f_dwgu": 42, "f_dh": 46,
    "nsa_in": 25, "nsa_cmp": 86, "nsa_slc": 122, "nsa_win": 74, "nsa_dcmp": 58, "nsa_dslc": 195, "nsa_dwin": 108,
    "nsa_dh": 26, "nsa_dwin_w": 25,
    "swa_in": 16, "swa_attn": 74, "swa_dattn": 74, "swa_dh": 17, "swa_dwin_w": 18,
    "fox_in": 34, "fox_attn": 104, "fox_dattn": 161, "fox_dh": 30, "fox_dwin_w": 30,
}
ICI_US_PER_MB = 15.0
PART_US = 12.0
D2D_US_PER_MB = 2.3
LOCAL_US_PER_MB = 1.5


class _Sched:
    def __init__(self):
        self.queue, self.credit, self.n_flush = [], 0.0, 0

    def push(self, task):
        self.queue.append(task)
        return task

    def _jobs(self, fits):
        jobs = []
        while self.queue and self.queue[0].ready():
            task, n = self.queue[0], 0
            while task.done + n < len(task.parts) and fits(task.parts[task.done + n][2]):
                n += 1
            if n:
                jobs.append(task.next_job(n))
            if task.done < len(task.parts):
                break
            self.queue.pop(0)
        return jobs

    def take(self, name):
        kind = name[2:].split("_", 1)
        key = ("f" if kind[0] in ("f1", "f2") else kind[0]) + "_" + kind[1]
        self.credit = min(self.credit, 0.0) + CARRIER_US.get(key, 0.0)

        def fits(cost):
            if self.credit < 0.5 * cost:
                return False
            self.credit -= cost
            return True

        return self._jobs(fits)

    def finish(self, task=None):
        while self.queue and (task is None or not task.complete()):
            jobs = []
            while self.queue and self.queue[0].ready():
                head = self.queue.pop(0)
                jobs.append(head.next_job(len(head.parts) - head.done))
                if head is task:
                    break
            assert jobs, "the task at the head of the queue waits for one that was never queued"
            _comm_call(jobs, name=f"exchange_{self.n_flush}")
            self.n_flush += 1


def _mm(a, b, *, ta=False, tb=False, tm, tn, b_lead=(), into=None, o_lead=(), out_dtype=F32, res=None, alpha=1.0,
        ctx=None, name):
    M = a.shape[1] if ta else a.shape[0]
    K = a.shape[0] if ta else a.shape[1]
    bk, bn = (b.shape[-1], b.shape[-2]) if tb else (b.shape[-2], b.shape[-1])
    j_lead, k_lead = "j" in b_lead, "k" in b_lead
    N = bn * (b.shape[b_lead.index("j")] if j_lead else 1)
    nk = b.shape[b_lead.index("k")] if k_lead else 1
    tk = K // nk
    assert tk == bk, (name, K, nk, bk)
    tm = min(tm, M)
    tn = bn if j_lead else min(tn, N)
    assert M % tm == 0 and N % tn == 0, (name, M, N, tm, tn)
    nb, no = len(b_lead), len(o_lead)

    def pick(lead, j, k):
        return tuple(j if t == "j" else k if t == "k" else t for t in lead)

    a_spec = pl.BlockSpec((tk, tm), lambda i, j, k: (k, i)) if ta else pl.BlockSpec((tm, tk), lambda i, j, k: (i, k))
    if tb:
        b_spec = pl.BlockSpec((None,) * nb + (tn, tk),
                              lambda i, j, k: pick(b_lead, j, k) + (0 if j_lead else j, 0 if k_lead else k))
    else:
        b_spec = pl.BlockSpec((None,) * nb + (tk, tn),
                              lambda i, j, k: pick(b_lead, j, k) + (0 if k_lead else k, 0 if j_lead else j))
    r_spec = pl.BlockSpec((tm, tn), lambda i, j, k: (i, j))
    o_spec = pl.BlockSpec((None,) * no + (tm, tn), lambda i, j, k: pick(o_lead, j, k) + (i, 0 if "j" in o_lead else j))
    dn = (((0 if ta else 1,), (1 if tb else 0,)), ((), ()))
    has_res, has_into = res is not None, into is not None
    if has_into:
        out_dtype = into.dtype

    def body(*refs):
        a_ref, b_ref = refs[0], refs[1]
        r_ref = refs[2] if has_res else None
        o_ref = refs[2 + has_res + has_into]
        prod = lax.dot_general(a_ref[...].astype(_CDT), b_ref[...].astype(_CDT), dn, preferred_element_type=F32)

        def finish(acc):
            if alpha != 1.0:
                acc = acc * alpha
            if has_res:
                acc = r_ref[...] + acc
            o_ref[...] = acc.astype(out_dtype)

        if nk == 1:
            finish(prod)
        else:
            acc_ref = refs[-1]
            k = pl.program_id(2)

            @pl.when(k == 0)
            def _():
                acc_ref[...] = prod

            @pl.when(k != 0)
            def _():
                acc_ref[...] += prod

            @pl.when(k == nk - 1)
            def _():
                finish(acc_ref[...])

    ins, specs = [a, b], [a_spec, b_spec]
    if has_res:
        ins.append(res)
        specs.append(r_spec)
    aliases = {}
    if has_into:
        aliases = {len(ins): 0}
        ins.append(into)
        specs.append(_ANY)
        out_shape = jax.ShapeDtypeStruct(into.shape, into.dtype)
    elif o_lead == ("j",):
        out_shape = jax.ShapeDtypeStruct((N // tn, M, tn), out_dtype)
    else:
        assert not o_lead
        out_shape = jax.ShapeDtypeStruct((M, N), out_dtype)
    scratch = [pltpu.VMEM((tm, tn), F32)] if nk > 1 else []
    return _pcall(body, out_shape=out_shape, grid=(M // tm, N // tn, nk), in_specs=specs, out_specs=o_spec,
                  scratch_shapes=scratch, aliases=aliases, name=name, jobs=ctx.take(name) if ctx else (), ins=ins)


def _rows2d(a):
    return a.reshape(-1, a.shape[-1])


def _row_tile(rows, cols, itemsize=4, budget=2 * 1024 * 1024):
    t = rows
    while t % 2 == 0 and t * cols * itemsize > budget and (t // 2) % 8 == 0:
        t //= 2
    return t


def _addn(*xs, name):
    shape = xs[0].shape
    x2 = [_rows2d(x) for x in xs]
    R, C = x2[0].shape
    tr = _row_tile(R, C)

    def body(*refs):
        acc = refs[0][...]
        for r in refs[1:-1]:
            acc = acc + r[...]
        refs[-1][...] = acc

    spec = pl.BlockSpec((tr, C), lambda i: (i, 0))
    out = pl.pallas_call(body, out_shape=jax.ShapeDtypeStruct((R, C), F32), grid=(R // tr,),
                         in_specs=[spec] * len(x2), out_specs=spec, compiler_params=_params(), name=name)(*x2)
    return out.reshape(shape)


def _rms_fwd(x, g, *, name):
    S, D = x.shape
    tr = 256

    def body(x_ref, g_ref, h_ref):
        xv = x_ref[...]
        rstd = lax.rsqrt(jnp.mean(xv * xv, axis=-1, keepdims=True) + RMS_EPS)
        h_ref[...] = (xv * rstd * g_ref[...]).astype(_CDT)

    return pl.pallas_call(body, out_shape=jax.ShapeDtypeStruct((S, D), _CDT), grid=(S // tr,),
                          in_specs=[pl.BlockSpec((tr, D), lambda i: (i, 0)), pl.BlockSpec((1, D), lambda i: (0, 0))],
                          out_specs=pl.BlockSpec((tr, D), lambda i: (i, 0)), compiler_params=_params(),
                          name=name)(x, g.reshape(1, D))


def _rms_bwd(dh, x, g, dres, *, name):
    S, D = x.shape
    tr = 256

    def body(dh_ref, x_ref, g_ref, dres_ref, dx_ref, dg_ref):
        xv = x_ref[...]
        rstd = lax.rsqrt(jnp.mean(xv * xv, axis=-1, keepdims=True) + RMS_EPS)
        xhat = xv * rstd
        dhv = dh_ref[...]
        dxhat = dhv * g_ref[...]
        dx_ref[...] = dres_ref[...] + rstd * (dxhat - xhat * jnp.mean(dxhat * xhat, axis=-1, keepdims=True))

        @pl.when(pl.program_id(0) == 0)
        def _():
            dg_ref[...] = jnp.zeros_like(dg_ref)

        dg_ref[...] += jnp.sum(dhv * xhat, axis=0, keepdims=True)

    row = pl.BlockSpec((tr, D), lambda i: (i, 0))
    vec = pl.BlockSpec((1, D), lambda i: (0, 0))
    dx, dg = pl.pallas_call(body, out_shape=(jax.ShapeDtypeStruct((S, D), F32), jax.ShapeDtypeStruct((1, D), F32)),
                            grid=(S // tr,), in_specs=[row, row, vec, row], out_specs=(row, vec),
                            compiler_params=_params(), name=name)(dh, x, g.reshape(1, D), dres)
    return dx, dg.reshape(D)


def _loss_head(x, g, tgt, *, name):
    S, D = x.shape
    tr = 256

    def body(x_ref, g_ref, t_ref, loss_ref, dx_ref, dg_ref):
        xv = x_ref[...]
        rstd = lax.rsqrt(jnp.mean(xv * xv, axis=-1, keepdims=True) + RMS_EPS)
        xhat = xv * rstd
        err = xhat * g_ref[...] - t_ref[...]
        part = 0.5 * jnp.sum(jnp.mean(err * err, axis=-1, keepdims=True), axis=0, keepdims=True)
        dy = err * (1.0 / D)
        dxhat = dy * g_ref[...]
        dx_ref[...] = rstd * (dxhat - xhat * jnp.mean(dxhat * xhat, axis=-1, keepdims=True))

        @pl.when(pl.program_id(0) == 0)
        def _():
            dg_ref[...] = jnp.zeros_like(dg_ref)
            loss_ref[...] = jnp.zeros_like(loss_ref)

        dg_ref[...] += jnp.sum(dy * xhat, axis=0, keepdims=True)
        loss_ref[...] += jnp.broadcast_to(part, loss_ref.shape)

    row = pl.BlockSpec((tr, D), lambda i: (i, 0))
    vec = pl.BlockSpec((1, D), lambda i: (0, 0))
    loss, dx, dg = pl.pallas_call(
        body, out_shape=(jax.ShapeDtypeStruct((8, LANES), F32), jax.ShapeDtypeStruct((S, D), F32),
                         jax.ShapeDtypeStruct((1, D), F32)),
        grid=(S // tr,), in_specs=[row, vec, row], out_specs=(pl.BlockSpec((8, LANES), lambda i: (0, 0)), row, vec),
        compiler_params=_params(), name=name)(x, g.reshape(1, D), tgt)
    return loss[0, 0], dx, dg.reshape(D)


def _rope_tables(S):
    inv = 10000.0 ** (-jnp.arange(0, DH, 2, dtype=F32) / DH)
    ang = jnp.arange(S, dtype=F32)[:, None] * inv[None, :]
    cos, sin = jnp.cos(ang), jnp.sin(ang)
    return jnp.concatenate([cos, cos], -1), jnp.concatenate([-sin, sin], -1)


def _swap_matrix():
    p = np.zeros((DH, DH), np.float32)
    for j in range(DH // 2):
        p[j + DH // 2, j] = 1.0
        p[j, j + DH // 2] = 1.0
    return jnp.asarray(p, jnp.bfloat16)


def _rope(x, cc, ss, *, name):
    n, S, _ = x.shape

    def body(x_ref, c_ref, s_ref, p_ref, o_ref):
        xv = x_ref[0]
        o_ref[0] = xv * c_ref[...] + _dot_exact(xv, p_ref[...]) * s_ref[...]

    tab = pl.BlockSpec((S, DH), lambda i: (0, 0))
    blk = pl.BlockSpec((1, S, DH), lambda i: (i, 0, 0))
    return pl.pallas_call(body, out_shape=jax.ShapeDtypeStruct(x.shape, F32), grid=(n,),
                          in_specs=[blk, tab, tab, pl.BlockSpec((DH, DH), lambda i: (0, 0))], out_specs=blk,
                          compiler_params=_params(), name=name)(x, cc, ss, _swap_matrix())


def _key_range(kind, i, window, Sk):
    if kind == "cmp":
        return 0, Sk
    hi = (i + 1) * BQ
    if kind == "band":
        return max(0, i * BQ - window), hi
    return 0, hi


def _attn_mask(kind, i, lo, hi, window):
    shape = (BQ, hi - lo)
    qpos = i * BQ + lax.broadcasted_iota(jnp.int32, shape, 0)
    kpos = lo + lax.broadcasted_iota(jnp.int32, shape, 1)
    if kind == "cmp":
        return kpos * 16 + 31 <= qpos
    mask = kpos <= qpos
    if kind == "band":
        mask = mask & (qpos - kpos < window)
    return mask


def _sel_expand(lo, hi):
    shape = (LANES, hi - lo)
    j = lax.broadcasted_iota(jnp.int32, shape, 0)
    key = lo + lax.broadcasted_iota(jnp.int32, shape, 1)
    return (jnp.right_shift(key, 6) == j).astype(_CDT)


def _eye():
    return lax.broadcasted_iota(jnp.int32, (BQ, BQ), 0) == lax.broadcasted_iota(jnp.int32, (BQ, BQ), 1)


def _to_col(row):
    return jnp.sum(jnp.where(_eye(), row, 0.0), axis=1, keepdims=True)


def _to_row(col):
    return jnp.sum(jnp.where(_eye(), col, 0.0), axis=0, keepdims=True)


def _scores(kind, i, lo, hi, window, qb, kb, crow_ref, sel_ref):
    s = _dot_nt(qb, kb) * SCALE
    if crow_ref is not None:
        s = s + _to_col(crow_ref[0, :, i * BQ:(i + 1) * BQ]) - crow_ref[0, :, lo:hi]
    mask = _attn_mask(kind, i, lo, hi, window)
    if sel_ref is not None:
        chosen = _dot_tn(sel_ref[0, :, i * BQ:(i + 1) * BQ].astype(_CDT), _sel_expand(lo, hi))
        mask = mask & (chosen > 0.5)
    return jnp.where(mask, s, NEG), mask


def _attn_fwd(q, k, v, *, kind, window=0, bias=None, sinks=None, selT=None, ovT=None, ctx=None, name):
    G, R, S, _ = q.shape
    Sk = k.shape[1]
    nq = S // BQ
    n_slc = S // 64
    has_bias, has_sink, has_sel, is_cmp = bias is not None, sinks is not None, selT is not None, kind == "cmp"

    def body(*refs):
        it = iter(refs)
        q_ref, k_ref, v_ref = next(it), next(it), next(it)
        crow_ref = next(it) if has_bias else None
        sink_ref = next(it) if has_sink else None
        sel_ref = next(it) if has_sel else None
        ov_ref = next(it) if is_cmp else None
        o_ref, lse_ref = next(it), next(it)
        selo_ref, imp_ref = (next(it), next(it)) if is_cmp else (None, None)
        r = pl.program_id(1)
        for i in range(nq):
            lo, hi = _key_range(kind, i, window, Sk)
            rows = slice(i * BQ, (i + 1) * BQ)
            qb = q_ref[0, 0, rows, :].astype(_CDT)
            kb = k_ref[0, lo:hi, :].astype(_CDT)
            vb = v_ref[0, lo:hi, :].astype(_CDT)
            s, mask = _scores(kind, i, lo, hi, window, qb, kb, crow_ref, sel_ref)
            m = jnp.max(s, axis=-1, keepdims=True)
            if has_sink:
                sk = sink_ref[0, 0, 0:1, 0:1]
                m = jnp.maximum(m, sk)
            e = jnp.exp(s - m)
            if is_cmp:
                e = jnp.where(mask, e, 0.0)
            l = jnp.sum(e, axis=-1, keepdims=True)
            if has_sink:
                l = l + jnp.exp(sk - m)
            if is_cmp:
                l = jnp.where(l > 0.0, l, 1.0)
            p = e * (1.0 / l)
            o_ref[0, 0, rows, :] = _dot(p.astype(_CDT), vb)
            lse_ref[0, 0, :, rows] = _to_row(m + jnp.log(l))
            if is_cmp:
                part = _dot_nt(ov_ref[...].astype(_CDT), p.astype(_CDT))

                @pl.when(r == 0)
                def _():
                    imp_ref[:, rows] = part

                @pl.when(r != 0)
                def _():
                    imp_ref[:, rows] += part

        if is_cmp:
            @pl.when(r == R - 1)
            def _():
                shape = (LANES, S)
                j = lax.broadcasted_iota(jnp.int32, shape, 0)
                tb = jnp.right_shift(lax.broadcasted_iota(jnp.int32, shape, 1), 6)
                forced = (j == 0) | (j == tb) | (j == tb - 1)
                imp = jnp.where(j > tb, NEG, jnp.where(forced, NSA_BONUS, imp_ref[...]))
                imp = jnp.where(j >= n_slc, -3e38, imp)
                imp_ref[...] = imp
                cnt = jnp.zeros(shape, F32)
                for jp in range(n_slc):
                    row = imp_ref[jp:jp + 1, :]
                    ahead = (row > imp) | ((row == imp) & (jp < j))
                    cnt = cnt + ahead.astype(F32)
                selo_ref[0] = (cnt < float(min(NSA_TOPK, n_slc))).astype(F32)

    qspec = pl.BlockSpec((1, 1, S, DH), lambda g, r: (g, r, 0, 0))
    kspec = pl.BlockSpec((1, Sk, DH), lambda g, r: (g, 0, 0))
    ins, specs = [q, k, v], [qspec, kspec, kspec]
    if has_bias:
        ins.append(bias)
        specs.append(pl.BlockSpec((1, 1, S), lambda g, r: (g, 0, 0)))
    if has_sink:
        ins.append(sinks)
        specs.append(pl.BlockSpec((1, 1, 8, LANES), lambda g, r: (g, r, 0, 0)))
    if has_sel:
        ins.append(selT)
        specs.append(pl.BlockSpec((1, LANES, S), lambda g, r: (g, 0, 0)))
    if is_cmp:
        ins.append(ovT)
        specs.append(pl.BlockSpec((LANES, Sk), lambda g, r: (0, 0)))
    outs = [jax.ShapeDtypeStruct((G, R, S, DH), F32), jax.ShapeDtypeStruct((G, R, 1, S), F32)]
    ospecs = [qspec, pl.BlockSpec((1, 1, 1, S), lambda g, r: (g, r, 0, 0))]
    scratch = []
    if is_cmp:
        outs.append(jax.ShapeDtypeStruct((G, LANES, S), F32))
        ospecs.append(pl.BlockSpec((1, LANES, S), lambda g, r: (g, 0, 0)))
        scratch.append(pltpu.VMEM((LANES, S), F32))
    return _pcall(body, out_shape=tuple(outs), grid=(G, R), in_specs=specs, out_specs=tuple(ospecs),
                  scratch_shapes=scratch, name=name, jobs=ctx.take(name) if ctx else (), ins=ins)


def _attn_bwd(q, k, v, o, do, lse, *, kind, window=0, bias=None, sinks=None, selT=None, gate=None, ctx=None, name):
    G, R, S, _ = q.shape
    Sk = k.shape[1]
    nq = S // BQ
    has_bias, has_sink, has_sel, has_gate = bias is not None, sinks is not None, selT is not None, gate is not None

    def body(*refs):
        it = iter(refs)
        q_ref, k_ref, v_ref, o_ref, do_ref, lse_ref = (next(it) for _ in range(6))
        crow_ref = next(it) if has_bias else None
        sink_ref = next(it) if has_sink else None
        sel_ref = next(it) if has_sel else None
        z_ref = next(it) if has_gate else None
        dq_ref, dk_ref, dv_ref = next(it), next(it), next(it)
        dc_ref = next(it) if has_bias else None
        dsink_ref = next(it) if has_sink else None
        dz_ref = next(it) if has_gate else None
        r = pl.program_id(1)

        @pl.when(r == 0)
        def _():
            dk_ref[...] = jnp.zeros_like(dk_ref)
            dv_ref[...] = jnp.zeros_like(dv_ref)
            if has_bias:
                dc_ref[...] = jnp.zeros_like(dc_ref)

        dsink = jnp.zeros((1, 1), F32)
        for i in range(nq):
            lo, hi = _key_range(kind, i, window, Sk)
            rows = slice(i * BQ, (i + 1) * BQ)
            qb = q_ref[0, 0, rows, :].astype(_CDT)
            kb = k_ref[0, lo:hi, :].astype(_CDT)
            vb = v_ref[0, lo:hi, :].astype(_CDT)
            s, mask = _scores(kind, i, lo, hi, window, qb, kb, crow_ref, sel_ref)
            lse_i = _to_col(lse_ref[0, 0, :, rows])
            p = jnp.where(mask, jnp.exp(s - lse_i), 0.0)
            dob = do_ref[0, 0, rows, :]
            if has_gate:
                od = jnp.sum(o_ref[0, 0, rows, :] * dob, axis=-1, keepdims=True)
                sg = jax.nn.sigmoid(_to_col(z_ref[0, 0, :, rows]))
                dob = dob * sg
                dz_ref[0, 0, :, rows] = _to_row(od * sg * (1.0 - sg))
            dob = dob.astype(_CDT)
            dp = _dot_nt(dob, vb)
            delta = jnp.sum(p * dp, axis=-1, keepdims=True)
            ds = p * (dp - delta)
            dsb = ds.astype(_CDT)
            dq_ref[0, 0, rows, :] = _dot(dsb, kb) * SCALE
            dk_ref[0, lo:hi, :] += _dot_tn(dsb, qb) * SCALE
            dv_ref[0, lo:hi, :] += _dot_tn(p.astype(_CDT), dob)
            if has_bias:
                dc_ref[0, :, rows] += _to_row(jnp.sum(ds, axis=-1, keepdims=True))
                dc_ref[0, :, lo:hi] -= jnp.sum(ds, axis=0, keepdims=True)
            if has_sink:
                sk = sink_ref[0, 0, 0:1, 0:1]
                dsink = dsink - jnp.sum(jnp.exp(sk - lse_i) * delta, axis=0, keepdims=True)
        if has_sink:
            dsink_ref[0, 0] = jnp.broadcast_to(dsink, (8, LANES))

    qspec = pl.BlockSpec((1, 1, S, DH), lambda g, r: (g, r, 0, 0))
    cspec = pl.BlockSpec((1, 1, 1, S), lambda g, r: (g, r, 0, 0))
    kspec = pl.BlockSpec((1, Sk, DH), lambda g, r: (g, 0, 0))
    ins, specs = [q, k, v, o, do, lse], [qspec, kspec, kspec, qspec, qspec, cspec]
    if has_bias:
        ins.append(bias)
        specs.append(pl.BlockSpec((1, 1, S), lambda g, r: (g, 0, 0)))
    if has_sink:
        ins.append(sinks)
        specs.append(pl.BlockSpec((1, 1, 8, LANES), lambda g, r: (g, r, 0, 0)))
    if has_sel:
        ins.append(selT)
        specs.append(pl.BlockSpec((1, LANES, S), lambda g, r: (g, 0, 0)))
    if has_gate:
        ins.append(gate)
        specs.append(cspec)
    names = ["dq", "dk", "dv"]
    outs = [jax.ShapeDtypeStruct((G, R, S, DH), F32), jax.ShapeDtypeStruct((G, Sk, DH), F32),
            jax.ShapeDtypeStruct((G, Sk, DH), F32)]
    ospecs = [qspec, kspec, kspec]
    if has_bias:
        assert R == 1
        names.append("dc")
        outs.append(jax.ShapeDtypeStruct((G, 1, S), F32))
        ospecs.append(pl.BlockSpec((1, 1, S), lambda g, r: (g, 0, 0)))
    if has_sink:
        names.append("dsink")
        outs.append(jax.ShapeDtypeStruct((G, R, 8, LANES), F32))
        ospecs.append(pl.BlockSpec((1, 1, 8, LANES), lambda g, r: (g, r, 0, 0)))
    if has_gate:
        names.append("dz")
        outs.append(jax.ShapeDtypeStruct((G, R, 1, S), F32))
        ospecs.append(cspec)
    res = _pcall(body, out_shape=tuple(outs), grid=(G, R), in_specs=specs, out_specs=tuple(ospecs), name=name,
                 jobs=ctx.take(name) if ctx else (), ins=ins)
    return dict(zip(names, res))


def _combine(o0, o1, o2, z, *, name):
    H, S, _ = o0.shape

    def body(o0_ref, o1_ref, o2_ref, z_ref, o_ref):
        for i in range(S // BQ):
            rows = slice(i * BQ, (i + 1) * BQ)
            acc = jax.nn.sigmoid(_to_col(z_ref[0, 0, :, rows])) * o0_ref[0, rows, :]
            acc = acc + jax.nn.sigmoid(_to_col(z_ref[1, 0, :, rows])) * o1_ref[0, rows, :]
            acc = acc + jax.nn.sigmoid(_to_col(z_ref[2, 0, :, rows])) * o2_ref[0, rows, :]
            o_ref[0, rows, :] = acc

    blk = pl.BlockSpec((1, S, DH), lambda h: (h, 0, 0))
    return pl.pallas_call(body, out_shape=jax.ShapeDtypeStruct((H, S, DH), F32), grid=(H,),
                          in_specs=[blk, blk, blk, pl.BlockSpec((3, 1, 1, S), lambda h: (0, h, 0, 0))], out_specs=blk,
                          compiler_params=_params(), name=name)(o0, o1, o2, z)


_GC = math.sqrt(2.0 / math.pi)


def _gelu(x):
    return 0.5 * x * (1.0 + jnp.tanh(_GC * (x + 0.044715 * x * x * x)))


def _gelu_grad(x):
    t = jnp.tanh(_GC * (x + 0.044715 * x * x * x))
    return 0.5 * (1.0 + t) + 0.5 * x * (1.0 - t * t) * _GC * (1.0 + 3.0 * 0.044715 * x * x)


def _make_xb(k):
    G, S, _ = k.shape
    chunks = k.reshape(G, S // 16, 16 * DH)
    shift = jnp.concatenate([chunks[:, 1:], jnp.zeros((G, 1, 16 * DH), k.dtype)], axis=1)
    return jnp.concatenate([chunks, shift], axis=-1)


def _unmake_xb(dxb, *, name):
    G, n, _ = dxb.shape
    a = dxb[..., :16 * DH]
    b = jnp.concatenate([jnp.zeros((G, 1, 16 * DH), F32), dxb[:, :-1, 16 * DH:]], axis=1)
    return _addn(a, b, name=name).reshape(G, n * 16, DH)


def _compress_fwd(xb, pe, w1, w2, *, name):
    G, n, W = xb.shape
    Hc = w1.shape[1]

    def body(xb_ref, pe_ref, w1_ref, w2_ref, kc_ref, hid_ref):
        xv = (xb_ref[0] + pe_ref[...]).astype(_CDT)
        hid = _dot(xv, w1_ref[...].astype(_CDT))
        hid_ref[0] = hid
        kc_ref[0] = _dot(_gelu(hid).astype(_CDT), w2_ref[...].astype(_CDT))

    return pl.pallas_call(
        body, out_shape=(jax.ShapeDtypeStruct((G, n, DH), F32), jax.ShapeDtypeStruct((G, n, Hc), F32)), grid=(G,),
        in_specs=[pl.BlockSpec((1, n, W), lambda g: (g, 0, 0)), pl.BlockSpec((1, W), lambda g: (0, 0)),
                  pl.BlockSpec((W, Hc), lambda g: (0, 0)), pl.BlockSpec((Hc, DH), lambda g: (0, 0))],
        out_specs=(pl.BlockSpec((1, n, DH), lambda g: (g, 0, 0)), pl.BlockSpec((1, n, Hc), lambda g: (g, 0, 0))),
        compiler_params=_params(), name=name)(xb, pe.reshape(1, W), w1, w2)


def _compress_bwd(xb, pe, w1, w2, hid, dkc, *, name):
    G, n, W = xb.shape
    Hc = w1.shape[1]

    def body(xb_ref, pe_ref, w1_ref, w2_ref, hid_ref, dkc_ref, dxb_ref, dw1_ref, dw2_ref, dpe_ref):
        @pl.when(pl.program_id(0) == 0)
        def _():
            dw1_ref[...] = jnp.zeros_like(dw1_ref)
            dw2_ref[...] = jnp.zeros_like(dw2_ref)
            dpe_ref[...] = jnp.zeros_like(dpe_ref)

        xv = (xb_ref[0] + pe_ref[...]).astype(_CDT)
        hid = hid_ref[0]
        dk = dkc_ref[0].astype(_CDT)
        dact = _dot_nt(dk, w2_ref[...].astype(_CDT))
        dhid = (dact * _gelu_grad(hid)).astype(_CDT)
        dw2_ref[...] += _dot_tn(_gelu(hid).astype(_CDT), dk)
        dxb = _dot_nt(dhid, w1_ref[...].astype(_CDT))
        dxb_ref[0] = dxb
        dw1_ref[...] += _dot_tn(xv, dhid)
        dpe_ref[...] += jnp.sum(dxb, axis=0, keepdims=True)

    return pl.pallas_call(
        body, out_shape=(jax.ShapeDtypeStruct((G, n, W), F32), jax.ShapeDtypeStruct((W, Hc), F32),
                         jax.ShapeDtypeStruct((Hc, DH), F32), jax.ShapeDtypeStruct((1, W), F32)), grid=(G,),
        in_specs=[pl.BlockSpec((1, n, W), lambda g: (g, 0, 0)), pl.BlockSpec((1, W), lambda g: (0, 0)),
                  pl.BlockSpec((W, Hc), lambda g: (0, 0)), pl.BlockSpec((Hc, DH), lambda g: (0, 0)),
                  pl.BlockSpec((1, n, Hc), lambda g: (g, 0, 0)), pl.BlockSpec((1, n, DH), lambda g: (g, 0, 0))],
        out_specs=(pl.BlockSpec((1, n, W), lambda g: (g, 0, 0)), pl.BlockSpec((W, Hc), lambda g: (0, 0)),
                   pl.BlockSpec((Hc, DH), lambda g: (0, 0)), pl.BlockSpec((1, W), lambda g: (0, 0))),
        compiler_params=_params(), name=name)(xb, pe.reshape(1, W), w1, w2, hid, dkc)


def _overlap_T(S):
    n_cmp, n_slc = S // 16 - 1, S // 64
    cs = np.arange(n_cmp) * 16
    ce = cs + 32
    ss = np.arange(n_slc) * 64
    se = ss + 64
    ov = np.clip(np.minimum(ce[:, None], se[None, :]) - np.maximum(cs[:, None], ss[None, :]), 0, None) / 32.0
    out = np.zeros((LANES, S // 16), np.float32)
    out[:n_slc, :n_cmp] = ov.T
    return jnp.asarray(out)


def _tri(n, upper):
    r = lax.broadcasted_iota(jnp.int32, (n, n), 0)
    c = lax.broadcasted_iota(jnp.int32, (n, n), 1)
    return ((c >= r) if upper else (c <= r)).astype(jnp.bfloat16)


def _fox_gate_fwd(zf, b, *, name):
    S, H = zf.shape
    nb = S // BQ

    def body(z_ref, b_ref, c_ref):
        tri = _tri(BQ, False)
        carry = jnp.zeros((1, H), F32)
        for i in range(nb):
            z = z_ref[i * BQ:(i + 1) * BQ, :] + b_ref[...]
            lf = jnp.minimum(z, 0.0) - jnp.log(1.0 + jnp.exp(-jnp.abs(z)))
            c_ref[i * BQ:(i + 1) * BQ, :] = _dot_exact_left(tri, lf) + carry
            carry = carry + jnp.sum(lf, axis=0, keepdims=True)

    return pl.pallas_call(body, out_shape=jax.ShapeDtypeStruct((S, H), F32), compiler_params=_params(),
                          name=name)(zf, b)


def _fox_gate_bwd(zf, b, dc, *, name):
    S, H = zf.shape
    nb = S // BQ

    def body(z_ref, b_ref, dc_ref, dz_ref, db_ref):
        tri = _tri(BQ, True)
        carry = jnp.zeros((1, H), F32)
        db = jnp.zeros((1, H), F32)
        for i in reversed(range(nb)):
            rows = slice(i * BQ, (i + 1) * BQ)
            dcb = dc_ref[rows, :]
            dlf = _dot_exact_left(tri, dcb) + carry
            carry = carry + jnp.sum(dcb, axis=0, keepdims=True)
            z = z_ref[rows, :] + b_ref[...]
            dz = dlf * jax.nn.sigmoid(-z)
            dz_ref[rows, :] = dz
            db = db + jnp.sum(dz, axis=0, keepdims=True)
        db_ref[...] = db

    return pl.pallas_call(body, out_shape=(jax.ShapeDtypeStruct((S, H), F32), jax.ShapeDtypeStruct((1, H), F32)),
                          compiler_params=_params(), name=name)(zf, b, dc)


def _to_heads(a):
    S = a.shape[0]
    return a.reshape(S, -1, DH).transpose(1, 0, 2)


def _from_heads(a):
    return a.transpose(1, 0, 2).reshape(a.shape[1], -1)


def _pad_lanes(a):
    return jnp.pad(a, ((0, 0), (0, LANES - a.shape[1])))


FFN_TM = 512


def _ffn_up(h, w4, *, ctx, name):
    S, D = h.shape
    C = w4.shape[2]

    def body(h_ref, wg_ref, wu_ref, gu_ref, a_ref):
        hv = h_ref[...].astype(_CDT)
        g = _dot(hv, wg_ref[...].astype(_CDT))
        u = _dot(hv, wu_ref[...].astype(_CDT))
        gu_ref[0] = g
        gu_ref[1] = u
        a_ref[...] = (g * jax.nn.sigmoid(g) * u).astype(_CDT)

    return _pcall(
        body, out_shape=(jax.ShapeDtypeStruct((2, S, 2 * C), F32), jax.ShapeDtypeStruct((S, 2 * C), _CDT)),
        grid=(S // FFN_TM, 2),
        in_specs=[pl.BlockSpec((FFN_TM, D), lambda i, j: (i, 0)), pl.BlockSpec((None, D, C), lambda i, j: (j, 0, 0)),
                  pl.BlockSpec((None, D, C), lambda i, j: (j + 2, 0, 0))],
        out_specs=(pl.BlockSpec((2, FFN_TM, C), lambda i, j: (0, i, j)), pl.BlockSpec((FFN_TM, C), lambda i, j: (i, j))),
        name=name, jobs=ctx.take(name) if ctx else (), ins=[h, w4, w4])


def _ffn_dact(dxo, wd, gu, *, ctx, name):
    S, D = dxo.shape
    C = gu.shape[2] // 2

    def body(dx_ref, wd_ref, gu_ref, d_ref):
        da = _dot_nt(dx_ref[...].astype(_CDT), wd_ref[...].astype(_CDT)) * 0.5
        g, u = gu_ref[0], gu_ref[1]
        sg = jax.nn.sigmoid(g)
        silu = g * sg
        d_ref[0] = da * u * (sg + silu * (1.0 - sg))
        d_ref[1] = da * silu

    blk = pl.BlockSpec((2, FFN_TM, C), lambda i, j: (0, i, j))
    return _pcall(body, out_shape=jax.ShapeDtypeStruct(gu.shape, F32), grid=(S // FFN_TM, 2),
                  in_specs=[pl.BlockSpec((FFN_TM, D), lambda i, j: (i, 0)), pl.BlockSpec((C, D), lambda i, j: (j, 0)), blk],
                  out_specs=blk, name=name, jobs=ctx.take(name) if ctx else (), ins=[dxo, wd, gu])


def _ffn_dwgu(h, dgu, *, ctx, name):
    S, D = h.shape
    C = dgu.shape[2] // 2
    tm = 512

    def body(h_ref, d_ref, o_ref):
        o_ref[...] = _dot_tn(h_ref[...].astype(_CDT), d_ref[...].astype(_CDT)).astype(_WIRE)

    return _pcall(body, out_shape=jax.ShapeDtypeStruct((4, D, C), _WIRE), grid=(D // tm, 4),
                  in_specs=[pl.BlockSpec((S, tm), lambda i, j: (0, i)),
                            pl.BlockSpec((None, S, C), lambda i, j: (j // 2, 0, j % 2))],
                  out_specs=pl.BlockSpec((None, tm, C), lambda i, j: (j, i, 0)), name=name,
                  jobs=ctx.take(name) if ctx else (), ins=[h, dgu])


def _ffn_dh(dgu, w4, *, ctx, name):
    _, S, F2 = dgu.shape
    C, D = F2 // 2, w4.shape[1]

    def body(d_ref, w_ref, o_ref, acc_ref):
        k = pl.program_id(1)
        prod = _dot_nt(d_ref[...].astype(_CDT), w_ref[...].astype(_CDT))

        @pl.when(k == 0)
        def _():
            acc_ref[...] = prod

        @pl.when(k != 0)
        def _():
            acc_ref[...] += prod

        @pl.when(k == 3)
        def _():
            o_ref[...] = acc_ref[...]

    return _pcall(body, out_shape=jax.ShapeDtypeStruct((S, D), F32), grid=(S // FFN_TM, 4),
                  in_specs=[pl.BlockSpec((None, FFN_TM, C), lambda i, k: (k // 2, i, k % 2)),
                            pl.BlockSpec((None, D, C), lambda i, k: (k, 0, 0))],
                  out_specs=pl.BlockSpec((FFN_TM, D), lambda i, k: (i, 0)), scratch_shapes=[pltpu.VMEM((FFN_TM, D), F32)],
                  name=name, jobs=ctx.take(name) if ctx else (), ins=[dgu, w4])


def _ffn_fwd(x, g, P, l, tag):
    h = _rms_fwd(x, g, name=tag + "_rms")
    gu, a = _ffn_up(h, P.weight("gu", l), ctx=P.ctx, name=tag + "_gu")
    xo = _mm(a, P.weight("down", l), tm=512, tn=1024, res=x, alpha=0.5, ctx=P.ctx, name=tag + "_down")
    return xo, (x, h, gu, a)


def _ffn_bwd(dxo, saved, g, P, l, tag):
    x, h, gu, a = saved
    dgu = _ffn_dact(dxo, P.weight("down", l), gu, ctx=P.ctx, name=tag + "_dact")
    P.grad("gu", l, _ffn_dwgu(h, dgu, ctx=P.ctx, name=tag + "_dwgu"))
    dwd = _mm(a, dxo, ta=True, tm=1408, tn=1024, alpha=0.5, out_dtype=_WIRE, ctx=P.ctx, name=tag + "_dwd")
    P.grad("down", l, dwd.reshape(4, -1, D_MODEL))
    dh = _ffn_dh(dgu, P.weight("gu", l), ctx=P.ctx, name=tag + "_dh")
    dx, dg = _rms_bwd(dh, x, g, dxo, name=tag + "_drms")
    return dx, dg


def _nsa_fwd(x, g, w, cc, ss, tag):
    S = x.shape[0]
    h = _rms_fwd(x, g, name=tag + "_rms")
    proj = _mm(h, w["w_in"], tm=512, tn=896, ctx=w["ctx"], name=tag + "_in")
    q = _to_heads(proj[:, :1024])
    kv = proj[:, 1024:2560].reshape(S, 3, 2, NSA_G, DH)
    ks = [kv[:, b, 0].transpose(1, 0, 2) for b in range(3)]
    vs = [kv[:, b, 1].transpose(1, 0, 2) for b in range(3)]
    z = proj[:, 2560:NSA_IN].reshape(S, 3, N_HEADS).transpose(1, 2, 0)[:, :, None, :]
    roped = _rope(jnp.concatenate([q] + ks, axis=0), cc, ss, name=tag + "_rope")
    qr = roped[:N_HEADS].reshape(NSA_G, NSA_R, S, DH)
    kr = [roped[N_HEADS + 4 * b:N_HEADS + 4 * b + 4] for b in range(3)]
    xbk, xbv = _make_xb(kr[0]), _make_xb(vs[0])
    kc, hidk = _compress_fwd(xbk, w["ck_pe"], w["ck_w1"], w["ck_w2"], name=tag + "_ck")
    vc, hidv = _compress_fwd(xbv, w["cv_pe"], w["cv_w1"], w["cv_w2"], name=tag + "_cv")
    ctx = w["ctx"]
    o0, lse0, selT = _attn_fwd(qr, kc, vc, kind="cmp", ovT=_overlap_T(S), ctx=ctx, name=tag + "_cmp")
    o1, lse1 = _attn_fwd(qr, kr[1], vs[1], kind="sel", selT=selT, ctx=ctx, name=tag + "_slc")
    o2, lse2 = _attn_fwd(qr, kr[2], vs[2], kind="band", window=NSA_WINDOW, ctx=ctx, name=tag + "_win")
    o = _combine(o0.reshape(N_HEADS, S, DH), o1.reshape(N_HEADS, S, DH), o2.reshape(N_HEADS, S, DH), z,
                 name=tag + "_mix")
    of = _from_heads(o)
    xo = _mm(of, w["out"], tm=512, tn=1024, res=x, ctx=w["ctx"], name=tag + "_out")
    saved = (x, h, qr, kr, vs, z, xbk, xbv, hidk, hidv, kc, vc, (o0, o1, o2), (lse0, lse1, lse2), selT, of)
    return xo, saved


def _out_bwd(dxo, of, w, tag):
    dof = _mm(dxo, w["out"], tb=True, tm=512, tn=1024, ctx=w["ctx"], name=tag + "_dof")
    dw = _mm(of, dxo, ta=True, tm=512, tn=1024, out_dtype=_WIRE, ctx=w["ctx"], name=tag + "_dwout")
    w["P"].grad("out", w["out_l"], dw.reshape(4, -1, D_MODEL))
    return dof


def _split_cols(dw, n_in):
    cs = n_in // 4
    return dw[:, :n_in].reshape(D_MODEL, 4, cs).transpose(1, 0, 2).astype(_WIRE)


def _nsa_bwd(dxo, saved, g, w, cc, ss, tag):
    x, h, qr, kr, vs, z, xbk, xbv, hidk, hidv, kc, vc, os_, lses, selT, of = saved
    S = x.shape[0]
    dof = _out_bwd(dxo, of, w, tag)
    do = _to_heads(dof).reshape(NSA_G, NSA_R, S, DH)
    zg = z.reshape(3, NSA_G, NSA_R, 1, S)
    ctx = w["ctx"]
    b0 = _attn_bwd(qr, kc, vc, os_[0], do, lses[0], kind="cmp", gate=zg[0], ctx=ctx, name=tag + "_dcmp")
    b1 = _attn_bwd(qr, kr[1], vs[1], os_[1], do, lses[1], kind="sel", selT=selT, gate=zg[1], ctx=ctx,
                   name=tag + "_dslc")
    b2 = _attn_bwd(qr, kr[2], vs[2], os_[2], do, lses[2], kind="band", window=NSA_WINDOW, gate=zg[2], ctx=ctx,
                   name=tag + "_dwin")
    dxbk, dck_w1, dck_w2, dck_pe = _compress_bwd(xbk, w["ck_pe"], w["ck_w1"], w["ck_w2"], hidk, b0["dk"],
                                                 name=tag + "_dck")
    dxbv, dcv_w1, dcv_w2, dcv_pe = _compress_bwd(xbv, w["cv_pe"], w["cv_w1"], w["cv_w2"], hidv, b0["dv"],
                                                 name=tag + "_dcv")
    dk0 = _unmake_xb(dxbk, name=tag + "_dk0")
    dv0 = _unmake_xb(dxbv, name=tag + "_dv0")
    dq = _addn(b0["dq"], b1["dq"], b2["dq"], name=tag + "_dqsum").reshape(N_HEADS, S, DH)
    unroped = _rope(jnp.concatenate([dq, dk0, b1["dk"], b2["dk"]], axis=0), cc, -ss, name=tag + "_drope")
    dks = [unroped[N_HEADS + 4 * b:N_HEADS + 4 * b + 4] for b in range(3)]
    dvs = [dv0, b1["dv"], b2["dv"]]
    dkv = jnp.stack([jnp.stack([dks[b], dvs[b]], axis=0) for b in range(3)], axis=0)
    dkv = dkv.transpose(3, 0, 1, 2, 4).reshape(S, 3 * 2 * NSA_G * DH)
    dz = jnp.stack([b0["dz"], b1["dz"], b2["dz"]], axis=0).reshape(3 * N_HEADS, S).T
    dproj = jnp.concatenate([_from_heads(unroped[:N_HEADS]), dkv, dz, jnp.zeros((S, NSA_IN_PAD - NSA_IN), F32)], axis=1)
    dh = _mm(dproj, w["w_in"], tb=True, tm=512, tn=512, ctx=w["ctx"], name=tag + "_dh")
    dw_in = _mm(h, dproj, ta=True, tm=512, tn=896, ctx=w["ctx"], name=tag + "_dwin_w")
    dx, dg = _rms_bwd(dh, x, g, dxo, name=tag + "_drms")
    P, j = w["P"], w["j"]
    P.grad("nsa_in", j, _split_cols(dw_in, NSA_IN))
    P.grad("cw1", j, dck_w1.astype(_WIRE).reshape(4, -1, dck_w1.shape[1]))
    P.grad("cw1", 2 + j, dcv_w1.astype(_WIRE).reshape(4, -1, dcv_w1.shape[1]))
    grads = dict(ck_pe=dck_pe.reshape(32, DH), ck_w2=dck_w2, cv_pe=dcv_pe.reshape(32, DH), cv_w2=dcv_w2)
    return dx, dg, grads


def _swa_fwd(x, g, w, cc, ss, tag):
    S = x.shape[0]
    h = _rms_fwd(x, g, name=tag + "_rms")
    proj = _mm(h, w["w_in"], tm=512, tn=640, ctx=w["ctx"], name=tag + "_in")
    q = _to_heads(proj[:, :1024])
    kv = proj[:, 1024:].reshape(S, 2, SWA_G, DH)
    k, v = kv[:, 0].transpose(1, 0, 2), kv[:, 1].transpose(1, 0, 2)
    roped = _rope(jnp.concatenate([q, k], axis=0), cc, ss, name=tag + "_rope")
    qr = roped[:N_HEADS].reshape(SWA_G, SWA_R, S, DH)
    kr = roped[N_HEADS:]
    sinks = jnp.broadcast_to(w["sinks"].reshape(SWA_G, SWA_R, 1, 1), (SWA_G, SWA_R, 8, LANES))
    o, lse = _attn_fwd(qr, kr, v, kind="band", window=SWA_WINDOW, sinks=sinks, ctx=w["ctx"], name=tag + "_attn")
    of = _from_heads(o.reshape(N_HEADS, S, DH))
    xo = _mm(of, w["out"], tm=512, tn=1024, res=x, ctx=w["ctx"], name=tag + "_out")
    return xo, (x, h, qr, kr, v, sinks, o, lse, of)


def _swa_bwd(dxo, saved, g, w, cc, ss, tag):
    x, h, qr, kr, v, sinks, o, lse, of = saved
    S = x.shape[0]
    dof = _out_bwd(dxo, of, w, tag)
    do = _to_heads(dof).reshape(SWA_G, SWA_R, S, DH)
    b = _attn_bwd(qr, kr, v, o, do, lse, kind="band", window=SWA_WINDOW, sinks=sinks, ctx=w["ctx"], name=tag + "_dattn")
    unroped = _rope(jnp.concatenate([b["dq"].reshape(N_HEADS, S, DH), b["dk"]], axis=0), cc, -ss, name=tag + "_drope")
    dkv = jnp.stack([unroped[N_HEADS:], b["dv"]], axis=0).transpose(2, 0, 1, 3).reshape(S, 2 * SWA_G * DH)
    dproj = jnp.concatenate([_from_heads(unroped[:N_HEADS]), dkv], axis=1)
    dh = _mm(dproj, w["w_in"], tb=True, tm=512, tn=512, ctx=w["ctx"], name=tag + "_dh")
    dw_in = _mm(h, dproj, ta=True, tm=512, tn=640, ctx=w["ctx"], name=tag + "_dwin_w")
    dx, dg = _rms_bwd(dh, x, g, dxo, name=tag + "_drms")
    w["P"].grad("swa_in", w["j"], _split_cols(dw_in, SWA_IN))
    return dx, dg, dict(sinks=b["dsink"][:, :, 0, 0].reshape(N_HEADS))


def _fox_fwd(x, g, w, tag):
    S = x.shape[0]
    h = _rms_fwd(x, g, name=tag + "_rms")
    proj = _mm(h, w["w_in"], tm=512, tn=640, ctx=w["ctx"], name=tag + "_in")
    q, k, v = (_to_heads(proj[:, i * 1024:(i + 1) * 1024]) for i in range(3))
    zf = _pad_lanes(proj[:, 3072:FOX_IN])
    bf = _pad_lanes(w["b_f"].reshape(1, N_HEADS))
    c = _fox_gate_fwd(zf, bf, name=tag + "_gate")[:, :N_HEADS]
    bias = c.T[:, None, :]
    q4 = q.reshape(N_HEADS, 1, S, DH)
    o, lse = _attn_fwd(q4, k, v, kind="causal", bias=bias, ctx=w["ctx"], name=tag + "_attn")
    of = _from_heads(o.reshape(N_HEADS, S, DH))
    xo = _mm(of, w["out"], tm=512, tn=1024, res=x, ctx=w["ctx"], name=tag + "_out")
    return xo, (x, h, q4, k, v, zf, bf, bias, o, lse, of)


def _fox_bwd(dxo, saved, g, w, tag):
    x, h, q4, k, v, zf, bf, bias, o, lse, of = saved
    S = x.shape[0]
    dof = _out_bwd(dxo, of, w, tag)
    do = _to_heads(dof).reshape(N_HEADS, 1, S, DH)
    b = _attn_bwd(q4, k, v, o, do, lse, kind="causal", bias=bias, ctx=w["ctx"], name=tag + "_dattn")
    dzf, db = _fox_gate_bwd(zf, bf, _pad_lanes(b["dc"].reshape(N_HEADS, S).T), name=tag + "_dgate")
    dproj = jnp.concatenate([_from_heads(b["dq"].reshape(N_HEADS, S, DH)), _from_heads(b["dk"]), _from_heads(b["dv"]),
                             dzf[:, :N_HEADS], jnp.zeros((S, FOX_IN_PAD - FOX_IN), F32)], axis=1)
    dh = _mm(dproj, w["w_in"], tb=True, tm=512, tn=512, ctx=w["ctx"], name=tag + "_dh")
    dw_in = _mm(h, dproj, ta=True, tm=512, tn=640, ctx=w["ctx"], name=tag + "_dwin_w")
    dx, dg = _rms_bwd(dh, x, g, dxo, name=tag + "_drms")
    w["P"].grad("fox_in", w["j"], _split_cols(dw_in, FOX_IN))
    return dx, dg, dict(b_f=db[0, :N_HEADS])


GROUPS = {
    "gu": (("ffn1_w_gu", "ffn2_w_gu"), 2),
    "down": (("ffn1_w_down", "ffn2_w_down"), 1),
    "out": (("nsa_w_out", "swa_w_out", "fox_w_out"), 1),
    "cw1": (("nsa_ck_w1", "nsa_cv_w1"), 1),
    "nsa_in": (("nsa_w_in",), 2),
    "swa_in": (("swa_w_in",), 2),
    "fox_in": (("fox_w_in",), 2),
}
OUT_SLAB = {0: 0, 3: 1, 1: 2, 2: 3}


def _pieces_in_order():
    chunks = []
    for i in range(DEPTH):
        kind, j = i % 3, i // 3
        chunks.append([("gu", i), ("down", i)])
        if kind == 0:
            chunks.append([("nsa_in", j), ("cw1", j), ("cw1", 2 + j), ("out", OUT_SLAB[i])])
        else:
            chunks.append([("swa_in" if kind == 1 else "fox_in", j), ("out", OUT_SLAB[i])])
        chunks.append([("gu", DEPTH + i), ("down", DEPTH + i)])
    return chunks


def _consumer_layout(group, F):
    _, rows, C = F.shape
    if group == "gu":
        return F
    if GROUPS[group][1] == 1:
        return F.reshape(4 * rows, C)
    w = F.transpose(1, 0, 2).reshape(rows, 4 * C)
    pad = {"nsa_in": NSA_IN_PAD, "swa_in": SWA_IN, "fox_in": FOX_IN_PAD}[group] - 4 * C
    return jnp.pad(w, ((0, 0), (0, pad)))


class _Given:
    def __init__(self, pieces, small):
        self.pieces, self.small, self.ctx, self.grads = pieces, small, None, {}

    def weight(self, group, l):
        return _consumer_layout(group, self.pieces[group, l])

    def grad(self, group, l, G):
        self.grads[group, l] = G


class _MixerWeights(dict):
    def __init__(self, P, pieces, **given):
        super().__init__(P=P, ctx=P.ctx, **given)
        self.pieces = pieces

    def __missing__(self, key):
        self[key] = self["P"].weight(*self.pieces[key])
        return self[key]


def _mixer_weights(P, i):
    kind, j = i % 3, i // 3
    out = {"out": ("out", OUT_SLAB[i])}
    if kind == 0:
        small = {k: P.small["nsa_" + k][j] for k in ("ck_pe", "ck_w2", "cv_pe", "cv_w2")}
        return _MixerWeights(P, dict(out, w_in=("nsa_in", j), ck_w1=("cw1", j), cv_w1=("cw1", 2 + j)), j=j,
                             out_l=OUT_SLAB[i], **small)
    if kind == 1:
        return _MixerWeights(P, dict(out, w_in=("swa_in", j)), j=j, out_l=OUT_SLAB[i], sinks=P.small["swa_sinks"][j])
    return _MixerWeights(P, dict(out, w_in=("fox_in", j)), j=j, out_l=OUT_SLAB[i], b_f=P.small["fox_b_f"][j])


def _local_step(x, tgt, P):
    S = x.shape[0]
    cc, ss = _rope_tables(S)
    sm = P.small
    saved = []
    for i in range(DEPTH):
        kind = i % 3
        x, s1 = _ffn_fwd(x, sm["ffn1_norm"][i], P, i, f"l{i}f1")
        mw = _mixer_weights(P, i)
        if kind == 0:
            x, s2 = _nsa_fwd(x, sm["mix_norm"][i], mw, cc, ss, f"l{i}nsa")
        elif kind == 1:
            x, s2 = _swa_fwd(x, sm["mix_norm"][i], mw, cc, ss, f"l{i}swa")
        else:
            x, s2 = _fox_fwd(x, sm["mix_norm"][i], mw, f"l{i}fox")
        x, s3 = _ffn_fwd(x, sm["ffn2_norm"][i], P, DEPTH + i, f"l{i}f2")
        saved.append((s1, mw, s2, s3))
    loss, dx, d_final = _loss_head(x, sm["final_norm"], tgt, name="loss_head")

    norms = {k: [None] * DEPTH for k in ("ffn1_norm", "mix_norm", "ffn2_norm")}
    mix = {}
    for i in reversed(range(DEPTH)):
        kind, j = i % 3, i // 3
        s1, mw, s2, s3 = saved[i]
        dx, norms["ffn2_norm"][i] = _ffn_bwd(dx, s3, sm["ffn2_norm"][i], P, DEPTH + i, f"l{i}f2")
        if kind == 0:
            dx, dg, gm = _nsa_bwd(dx, s2, sm["mix_norm"][i], mw, cc, ss, f"l{i}nsa")
            pre = "nsa_"
        elif kind == 1:
            dx, dg, gm = _swa_bwd(dx, s2, sm["mix_norm"][i], mw, cc, ss, f"l{i}swa")
            pre = "swa_"
        else:
            dx, dg, gm = _fox_bwd(dx, s2, sm["mix_norm"][i], mw, f"l{i}fox")
            pre = "fox_"
        norms["mix_norm"][i] = dg
        for k, val in gm.items():
            mix.setdefault(pre + k, {})[j] = val
        dx, norms["ffn1_norm"][i] = _ffn_bwd(dx, s1, sm["ffn1_norm"][i], P, i, f"l{i}f1")
    small = {k: jnp.stack(v, axis=0) for k, v in norms.items()}
    small.update({k: jnp.stack([d[j] for j in sorted(d)], axis=0) for k, d in mix.items()})
    small["final_norm"] = d_final
    return loss, dx, small


def _sum_slots(own, recv, me, into, row0, *, name):
    _, R, C = recv.shape
    tr = _row_tile(math.gcd(R, row0), C)
    blk0 = row0 // tr

    def body(_, g_ref, r_ref, __, o_ref):
        acc = g_ref[...].astype(F32) + r_ref[0].astype(F32)
        acc = acc + r_ref[1].astype(F32)
        o_ref[...] = acc + r_ref[2].astype(F32)

    spec = pltpu.PrefetchScalarGridSpec(
        num_scalar_prefetch=1, grid=(R // tr,),
        in_specs=[pl.BlockSpec((None, tr, C), lambda i, me_ref: (me_ref[0], i, 0)),
                  pl.BlockSpec((3, tr, C), lambda i, me_ref: (0, i, 0)), _ANY],
        out_specs=pl.BlockSpec((tr, C), lambda i, me_ref: (blk0 + i, 0)))
    return pl.pallas_call(body, grid_spec=spec, out_shape=jax.ShapeDtypeStruct(into.shape, F32),
                          input_output_aliases={3: 0}, compiler_params=_params(), name=name)(me, own, recv, into)


def _swap_sibling(parts):
    n = len(parts)

    def body(*refs):
        srcs, outs = refs[:n], refs[n:2 * n]
        send_sems, recv_sems = refs[2 * n:]
        x, y, c = lax.axis_index("x"), lax.axis_index("y"), lax.axis_index("c")
        cps = [pltpu.make_async_remote_copy(src_ref=srcs[g], dst_ref=outs[g], send_sem=send_sems.at[g],
                                            recv_sem=recv_sems.at[g], device_id=(x, y, 1 - c), device_id_type=MESH)
               for g in range(n)]
        for cp in cps:
            cp.start()
        for cp in cps:
            cp.wait()

    return pl.pallas_call(
        body, out_shape=tuple(jax.ShapeDtypeStruct(p.shape, p.dtype) for p in parts),
        in_specs=[_ANY] * n, out_specs=(_ANY,) * n,
        scratch_shapes=[pltpu.SemaphoreType.DMA((n,)), pltpu.SemaphoreType.DMA((n,))],
        compiler_params=pltpu.CompilerParams(has_side_effects=True), name="swap_core_grads")(*parts)


def _flip(coord, bit):
    return 1 - coord if bit else coord


def _allreduce_small(v):
    n, C = v.shape

    def body(v_ref, o_ref, buf, send_sems, recv_sems):
        x, y, c = lax.axis_index("x"), lax.axis_index("y"), lax.axis_index("c")
        me = 4 * x + 2 * y + c
        buf[me] = v_ref[...]
        peers = [(_flip(x, (j >> 2) & 1), _flip(y, (j >> 1) & 1), _flip(c, j & 1)) for j in range(1, 8)]
        sends = [pltpu.make_async_remote_copy(src_ref=v_ref, dst_ref=buf.at[me], send_sem=send_sems.at[j],
                                              recv_sem=recv_sems.at[j], device_id=peer, device_id_type=MESH)
                 for j, peer in enumerate(peers)]
        for cp in sends:
            cp.start()
        for j, (px, py, pc) in enumerate(peers):
            pltpu.make_async_remote_copy(src_ref=v_ref, dst_ref=buf.at[4 * px + 2 * py + pc], send_sem=send_sems.at[j],
                                         recv_sem=recv_sems.at[j], device_id=(px, py, pc),
                                         device_id_type=MESH).wait_recv()
        for cp in sends:
            cp.wait_send()
        acc = buf[0]
        for d in range(1, 8):
            acc = acc + buf[d]
        o_ref[...] = acc

    return pl.pallas_call(
        body, out_shape=jax.ShapeDtypeStruct((n, C), F32),
        in_specs=[pl.BlockSpec(memory_space=pltpu.VMEM)], out_specs=pl.BlockSpec(memory_space=pltpu.VMEM),
        scratch_shapes=[pltpu.VMEM((8, n, C), F32), pltpu.SemaphoreType.DMA((7,)), pltpu.SemaphoreType.DMA((7,))],
        compiler_params=pltpu.CompilerParams(has_side_effects=True), name="allreduce_small")(v)


def _adamw(w, m, v, gs, *, row0=0, name):
    shape = w.shape
    w2, m2, v2 = _rows2d(w), _rows2d(m), _rows2d(v)
    g2 = [_rows2d(g) for g in gs]
    R, C = w2.shape
    tr = _row_tile(math.gcd(R, row0), C, budget=1024 * 1024)
    ng = len(g2)
    blk0 = row0 // tr

    def body(*refs):
        w_ref, m_ref, v_ref = refs[:3]
        g = refs[3][...]
        for r in refs[4:3 + ng]:
            g = g + r[...]
        g_ref, d_ref, nm_ref, nv_ref = refs[3 + ng:]
        mn = B1 * m_ref[...] + (1.0 - B1) * g
        vn = B2 * v_ref[...] + (1.0 - B2) * (g * g)
        m_hat = mn / (1.0 - B1 ** STEP)
        v_hat = vn / (1.0 - B2 ** STEP)
        g_ref[...] = g
        d_ref[...] = -LR * (m_hat / (jnp.sqrt(v_hat) + EPS) + WD * w_ref[...])
        nm_ref[...] = mn
        nv_ref[...] = vn

    spec = pl.BlockSpec((tr, C), lambda i: (i, 0))
    gspec = pl.BlockSpec((tr, C), lambda i: (blk0 + i, 0))
    outs = pl.pallas_call(body, out_shape=tuple(jax.ShapeDtypeStruct((R, C), F32) for _ in range(4)), grid=(R // tr,),
                          in_specs=[spec] * 3 + [gspec] * ng, out_specs=(spec,) * 4, compiler_params=_params(),
                          name=name)(w2, m2, v2, *g2)
    return tuple(o.reshape(shape) for o in outs)


def _small_layout(shapes):
    offs, off = {}, 0
    for k in REPLICATED:
        n = int(np.prod(shapes[k]))
        offs[k] = (off, n)
        off += -(-n // LANES) * LANES
    return offs, off


def _pack_small(d, shapes):
    offs, total = _small_layout(shapes)
    parts = []
    for k in REPLICATED:
        n = offs[k][1]
        parts.append(jnp.pad(d[k].reshape(-1).astype(F32), (0, -(-n // LANES) * LANES - n)))
    rows = -(-(total // LANES) // 8) * 8
    return jnp.pad(jnp.concatenate(parts), (0, rows * LANES - total)).reshape(rows, LANES)


def _unpack_small(a, shapes):
    offs, _ = _small_layout(shapes)
    flat = a.reshape(-1)
    return {k: flat[offs[k][0]:offs[k][0] + offs[k][1]].reshape(shapes[k]) for k in REPLICATED}


def _group_shards(w):
    shards = []
    for members, _ in GROUPS.values():
        s = jnp.concatenate([w[k].astype(_WIRE) for k in members], axis=0)
        shards.append(s.reshape(s.shape[0], 2, s.shape[1] // 2, s.shape[2]))
    return shards


class _Exchanged:
    def __init__(self, w):
        self.small = {k: w[k] for k in REPLICATED}
        self.ctx = _Sched()
        shards = dict(zip(GROUPS, _group_shards(w)))
        self.gather, self.cache, self.scatter = {}, {}, {}
        chunks = _pieces_in_order()
        self.me = jnp.reshape(2 * lax.axis_index("x") + lax.axis_index("y"), (1,)).astype(jnp.int32)
        first = {(g, l): _gather_task1(shards[g], l, _place_own(shards[g], l, self.me, name=f"own_{g}{l}"))
                 for chunk in chunks for g, l in chunk}
        self.gather = {p: _gather_task2(task) for p, task in first.items()}
        pieces = [p for chunk in chunks for p in chunk]
        for n, p in enumerate(pieces):
            self.ctx.push(first[p])
            if n:
                self.ctx.push(self.gather[pieces[n - 1]])
        self.ctx.push(self.gather[pieces[-1]])

    def weight(self, group, l):
        if (group, l) not in self.cache:
            task = self.gather[group, l]
            self.ctx.finish(task)
            F = task.result()
            self.cache[group, l] = _consumer_layout(group, F.reshape(4, -1, F.shape[-1]))
        return self.cache[group, l]

    def grad(self, group, l, G):
        self.scatter[group, l] = (G, self.ctx.push(_scatter_task(G)))

    def partial_sums(self, w):
        self.ctx.finish()
        parts = []
        for group, (members, _) in GROUPS.items():
            rows, C = w[members[0]].shape[1:]
            n = sum(w[k].shape[0] for k in members)
            part = jnp.zeros((n * rows, C), F32)
            for l in range(n):
                G, task = self.scatter[group, l]
                part = _sum_slots(G, task.result(), self.me, part, l * rows, name=f"sum_{group}{l}")
            parts.append(part)
        return parts


def _reduce_and_update(w, m, v, parts, small):
    others = _swap_sibling(parts)
    small_shapes = {k: w[k].shape for k in REPLICATED}
    g_small = _unpack_small(_allreduce_small(_pack_small(small, small_shapes)), small_shapes)

    out_g, out_d, out_m, out_v = {}, {}, {}, {}
    for (members, _), part, other in zip(GROUPS.values(), parts, others):
        row0 = 0
        for k in members:
            out_g[k], out_d[k], out_m[k], out_v[k] = _adamw(w[k], m[k], v[k], [part, other], row0=row0,
                                                            name="adamw_" + k)
            row0 += w[k].shape[0] * w[k].shape[1]
    sm = _adamw(_pack_small(w, small_shapes), _pack_small(m, small_shapes), _pack_small(v, small_shapes),
                [_pack_small(g_small, small_shapes)], name="adamw_small")
    for d, packed in zip((out_g, out_d, out_m, out_v), sm):
        d.update(_unpack_small(packed, small_shapes))
    return out_g, out_d, out_m, out_v


def kernel(x, ffn1_norm, ffn1_w_gu, ffn1_w_down, mix_norm, ffn2_norm, ffn2_w_gu, ffn2_w_down, nsa_w_in, nsa_ck_pe, nsa_ck_w1, nsa_ck_w2, nsa_cv_pe, nsa_cv_w1, nsa_cv_w2, nsa_w_out, swa_w_in, swa_sinks, swa_w_out, fox_w_in, fox_b_f, fox_w_out, final_norm, loss_target, m_ffn1_norm, m_ffn1_w_gu, m_ffn1_w_down, m_mix_norm, m_ffn2_norm, m_ffn2_w_gu, m_ffn2_w_down, m_nsa_w_in, m_nsa_ck_pe, m_nsa_ck_w1, m_nsa_ck_w2, m_nsa_cv_pe, m_nsa_cv_w1, m_nsa_cv_w2, m_nsa_w_out, m_swa_w_in, m_swa_sinks, m_swa_w_out, m_fox_w_in, m_fox_b_f, m_fox_w_out, m_final_norm, v_ffn1_norm, v_ffn1_w_gu, v_ffn1_w_down, v_mix_norm, v_ffn2_norm, v_ffn2_w_gu, v_ffn2_w_down, v_nsa_w_in, v_nsa_ck_pe, v_nsa_ck_w1, v_nsa_ck_w2, v_nsa_cv_pe, v_nsa_cv_w1, v_nsa_cv_w2, v_nsa_w_out, v_swa_w_in, v_swa_sinks, v_swa_w_out, v_fox_w_in, v_fox_b_f, v_fox_w_out, v_final_norm):
    args = dict(locals())
    w = {k: args[k] for k in WEIGHTS}
    m = {k: args["m_" + k] for k in WEIGHTS}
    v = {k: args["v_" + k] for k in WEIGHTS}
    P = _Exchanged(w)
    loss_part, dx, small = _local_step(x[0], loss_target[0], P)
    loss = lax.psum(loss_part, ("x", "y", "c"))
    out_g, out_d, out_m, out_v = _reduce_and_update(w, m, v, P.partial_sums(w), small)
    return (loss, dx[None], *[out_g[k] for k in WEIGHTS], *[out_d[k] for k in WEIGHTS],
            *[out_m[k] for k in WEIGHTS], *[out_v[k] for k in WEIGHTS])
```

```python
import math

import numpy as np
import jax
import jax.numpy as jnp
from jax import lax
from jax.experimental import pallas as pl
from jax.experimental.pallas import tpu as pltpu

F32 = jnp.float32
_CDT = jnp.bfloat16
_WIRE = jnp.bfloat16
_VMEM_LIMIT = 56 * 1024 * 1024

D_MODEL = 1024
DEPTH = 4
DH = 64
N_HEADS = 16
RMS_EPS = 1e-6
NEG = -1e30
SCALE = DH ** -0.5
BQ = 256
LANES = 128
NSA_G, NSA_R = 4, 4
NSA_WINDOW = 512
NSA_TOPK = 16
NSA_BONUS = 1e4
SWA_G, SWA_R = 2, 8
SWA_WINDOW = 128
NSA_ROTATED = tuple(hd < 16 or (hd - 16) % 8 < 4 for hd in range(40))
NSA_IN, NSA_IN_PAD = 2608, 2688
SWA_IN = 1280
SWA_ROTATED = (True,) * 18 + (False,) * 2
SWA_HEADS = (0, 16, 18)
FOX_IN, FOX_IN_PAD = 3088, 3200
FOX_HEADS = (0, 16, 32)
LR, B1, B2, EPS, WD, STEP = 0.001, 0.9, 0.999, 1e-08, 0.01, 10
MESH = pl.DeviceIdType.MESH

REPLICATED = ["ffn1_norm", "mix_norm", "ffn2_norm", "nsa_ck_pe", "nsa_ck_w2", "nsa_cv_pe", "nsa_cv_w2",
              "swa_sinks", "fox_b_f", "final_norm"]
WEIGHTS = ['ffn1_norm', 'ffn1_w_gu', 'ffn1_w_down', 'mix_norm', 'ffn2_norm', 'ffn2_w_gu', 'ffn2_w_down',
           'nsa_w_in', 'nsa_ck_pe', 'nsa_ck_w1', 'nsa_ck_w2', 'nsa_cv_pe', 'nsa_cv_w1', 'nsa_cv_w2', 'nsa_w_out',
           'swa_w_in', 'swa_sinks', 'swa_w_out', 'fox_w_in', 'fox_b_f', 'fox_w_out', 'final_norm']


def _params(**kw):
    return pltpu.CompilerParams(vmem_limit_bytes=_VMEM_LIMIT, **kw)


def _dot(a, b):
    return lax.dot_general(a, b, (((1,), (0,)), ((), ())), preferred_element_type=F32)


def _dot_nt(a, b):
    return lax.dot_general(a, b, (((1,), (1,)), ((), ())), preferred_element_type=F32)


def _dot_tn(a, b):
    return lax.dot_general(a, b, (((0,), (0,)), ((), ())), preferred_element_type=F32)


def _split3(x):
    hi = x.astype(jnp.bfloat16)
    r1 = x - hi.astype(F32)
    mid = r1.astype(jnp.bfloat16)
    lo = (r1 - mid.astype(F32)).astype(jnp.bfloat16)
    return hi, mid, lo


def _dot_exact(x, p):
    hi, mid, lo = _split3(x)
    return _dot(hi, p) + _dot(mid, p) + _dot(lo, p)


def _dot_exact_left(p, x):
    hi, mid, lo = _split3(x)
    return _dot(p, hi) + _dot(p, mid) + _dot(p, lo)


_ANY = pl.BlockSpec(memory_space=pl.ANY)


def _chip_peers():
    x, y, c = lax.axis_index("x"), lax.axis_index("y"), lax.axis_index("c")
    return x, y, c, [(1 - x, y), (x, 1 - y), (1 - x, 1 - y)]


class _Job:
    def __init__(self, ins, outs, nsem, copies, cost, alias):
        self.ins, self.outs, self.nsem, self.copies, self.cost, self.alias = ins, outs, nsem, copies, cost, alias
        self.results = None

    def inputs(self):
        return self.ins


class _Task:
    def __init__(self, src, out, parts, init=None):
        self.src, self.out, self.parts, self.init, self.done, self.jobs = src, out, parts, init, 0, []

    def result(self):
        return self.jobs[-1].results[0]

    def complete(self):
        return self.done == len(self.parts) and self.jobs[-1].results is not None

    def ready(self):
        if isinstance(self.src, _Task) and not self.src.complete():
            return False
        return not self.jobs or self.jobs[-1].results is not None

    def next_job(self, n):
        parts = self.parts[self.done:self.done + n]
        in_place = isinstance(self.src, _Task)
        ins = [] if in_place else [self.src]
        prev = self.jobs[-1].results[0] if self.jobs else (self.src.result() if in_place else self.init)
        alias = {}
        if prev is not None:
            alias = {len(ins): 0}
            ins = ins + [prev]

        def copies(in_refs, out_refs, sems):
            src = out_refs[0] if in_place else in_refs[0]
            out, first = [], 0
            for build, nsem, _ in parts:
                out += build(src, out_refs[0], sems, first)
                first += nsem
            return out

        job = _Job(ins, [self.out], sum(p[1] for p in parts), copies, sum(p[2] for p in parts), alias)
        self.done += n
        self.jobs.append(job)
        return job


def _pcall(body, *, out_shape, in_specs, out_specs, name, grid=(), scratch_shapes=(), aliases=None, jobs=(), ins):
    single = not isinstance(out_shape, (tuple, list))
    out_shape = (out_shape,) if single else tuple(out_shape)
    out_specs = (out_specs,) if single else tuple(out_specs)
    n_in, n_out, n_scr = len(ins), len(out_shape), len(scratch_shapes)
    side_ins = [a for j in jobs for a in j.inputs()]
    side_outs = [o for j in jobs for o in j.outs]
    aliases = dict(aliases or {})
    i0, o0 = n_in, n_out
    for j in jobs:
        for a, b in j.alias.items():
            aliases[i0 + a] = o0 + b
        i0 += len(j.ins)
        o0 += len(j.outs)

    def body2(*refs):
        in_refs, s_in = refs[:n_in], refs[n_in:n_in + len(side_ins)]
        r = n_in + len(side_ins)
        out_refs, s_out = refs[r:r + n_out], refs[r + n_out:r + n_out + len(side_outs)]
        r += n_out + len(side_outs)
        scr, sems = refs[r:r + n_scr], refs[r + n_scr:]
        def descriptors():
            out, a, b = [], 0, 0
            for j, sem in zip(jobs, sems):
                out += j.copies(s_in[a:a + len(j.ins)], s_out[b:b + len(j.outs)], sem)
                a += len(j.ins)
                b += len(j.outs)
            return out

        def at_step(steps, fn):
            cond = None
            for d, s in enumerate(steps):
                hit = pl.program_id(d) == s
                cond = hit if cond is None else jnp.logical_and(cond, hit)
            if cond is None:
                fn()
            else:
                pl.when(cond)(fn)

        def start_all():
            for cp in descriptors():
                cp.start()

        def wait_all():
            for cp in descriptors():
                cp.wait()

        if jobs:
            at_step([0] * len(grid), start_all)
        body(*in_refs, *out_refs, *scr)
        if jobs:
            at_step([n - 1 for n in grid], wait_all)

    res = pl.pallas_call(
        body2, out_shape=out_shape + tuple(side_outs), grid=grid, in_specs=list(in_specs) + [_ANY] * len(side_ins),
        out_specs=out_specs + (_ANY,) * len(side_outs),
        scratch_shapes=list(scratch_shapes) + [pltpu.SemaphoreType.DMA((j.nsem,)) for j in jobs],
        input_output_aliases=aliases,
        compiler_params=_params(has_side_effects=True) if jobs else _params(), name=name)(*ins, *side_ins)
    b = n_out
    for j in jobs:
        j.results = list(res[b:b + len(j.outs)])
        b += len(j.outs)
    return res[0] if single else tuple(res[:n_out])


def _comm_call(jobs, name):
    def body():
        pass

    _pcall(body, out_shape=(), in_specs=[], out_specs=(), name=name, jobs=jobs, ins=[])


def _place_own(shard, l, me, *, name):
    _, _, h, C = shard.shape

    def body(_, s_ref, o_ref):
        o_ref[...] = s_ref[...]

    spec = pltpu.PrefetchScalarGridSpec(
        num_scalar_prefetch=1, grid=(2,),
        in_specs=[pl.BlockSpec((None, None, h, C), lambda i, me_ref: (l, i, 0, 0))],
        out_specs=pl.BlockSpec((None, None, h, C), lambda i, me_ref: (me_ref[0], i, 0, 0)))
    return pl.pallas_call(body, grid_spec=spec, out_shape=jax.ShapeDtypeStruct((4, 2, h, C), shard.dtype),
                          compiler_params=_params(), name=name)(me, shard)


def _gather_task1(shard, l, own):
    _, _, h, C = shard.shape
    mb = 3 * h * C * jnp.dtype(_WIRE).itemsize / 1e6
    n = _row_splits(h, ICI_US_PER_MB * mb)
    hr = h // n

    def rows_to_peers(r):
        def build(src, dst, sems, s0):
            x, y, c, chips = _chip_peers()
            me, rows = 2 * x + y, pl.ds(r * hr, hr)
            return [pltpu.make_async_remote_copy(src_ref=src.at[l, c, rows], dst_ref=dst.at[me, c, rows],
                                                 send_sem=sems.at[s0 + 2 * j], recv_sem=sems.at[s0 + 2 * j + 1],
                                                 device_id=(px, py, c), device_id_type=MESH)
                    for j, (px, py) in enumerate(chips)]
        return build

    parts = [(rows_to_peers(r), 6, ICI_US_PER_MB * mb / n) for r in range(n)]
    return _Task(shard, jax.ShapeDtypeStruct((4, 2, h, C), shard.dtype), parts, init=own)


def _gather_task2(task1):
    _, _, h, C = task1.out.shape
    mb = h * C * jnp.dtype(_WIRE).itemsize / 1e6

    def forward(j):
        def build(_, dst, sems, s0):
            x, y, c, chips = _chip_peers()
            px, py = chips[j]
            landed = dst.at[2 * px + py, c]
            return [pltpu.make_async_remote_copy(src_ref=landed, dst_ref=landed, send_sem=sems.at[s0],
                                                 recv_sem=sems.at[s0 + 1], device_id=(x, y, 1 - c),
                                                 device_id_type=MESH)]
        return build

    return _Task(task1, task1.out, [(forward(j), 2, 1.0 + D2D_US_PER_MB * mb) for j in range(3)])


def _scatter_task(G):
    mb = 3 * G.shape[1] * G.shape[2] * jnp.dtype(_WIRE).itemsize / 1e6
    n = _row_splits(G.shape[1], ICI_US_PER_MB * mb)
    rr = G.shape[1] // n

    def rows_to_peers(r):
        def build(src, dst, sems, s0):
            _, _, c, chips = _chip_peers()
            rows = pl.ds(r * rr, rr)
            return [pltpu.make_async_remote_copy(src_ref=src.at[2 * px + py, rows], dst_ref=dst.at[j, rows],
                                                 send_sem=sems.at[s0 + 2 * j], recv_sem=sems.at[s0 + 2 * j + 1],
                                                 device_id=(px, py, c), device_id_type=MESH)
                    for j, (px, py) in enumerate(chips)]
        return build

    parts = [(rows_to_peers(r), 6, ICI_US_PER_MB * mb / n) for r in range(n)]
    return _Task(G, jax.ShapeDtypeStruct((3,) + G.shape[1:], G.dtype), parts)


def _row_splits(rows, cost):
    n = 8
    while n > 1 and (rows % (16 * n) or cost / n < PART_US):
        n //= 2
    return n


CARRIER_US = {
    "f_gu": 48, "f_down": 20, "f_dact": 42, "f_dwd": 20, "f_dwgu": 42, "f_dh": 46,
    "nsa_in": 25, "nsa_cmp": 86, "nsa_slc": 122, "nsa_win": 74, "nsa_dcmp": 58, "nsa_dslc": 195, "nsa_dwin": 108,
    "nsa_dh": 26, "nsa_dwin_w": 25,
    "swa_in": 16, "swa_attn": 74, "swa_dattn": 74, "swa_dh": 17, "swa_dwin_w": 18,
    "fox_in": 34, "fox_attn": 104, "fox_dattn": 161, "fox_dh": 30, "fox_dwin_w": 30,
}
ICI_US_PER_MB = 15.0
PART_US = 12.0
D2D_US_PER_MB = 2.3
LOCAL_US_PER_MB = 1.5


class _Sched:
    def __init__(self):
        self.queue, self.credit, self.n_flush = [], 0.0, 0

    def push(self, task):
        self.queue.append(task)
        return task

    def _jobs(self, fits):
        jobs = []
        while self.queue and self.queue[0].ready():
            task, n = self.queue[0], 0
            while task.done + n < len(task.parts) and fits(task.parts[task.done + n][2]):
                n += 1
            if n:
                jobs.append(task.next_job(n))
            if task.done < len(task.parts):
                break
            self.queue.pop(0)
        return jobs

    def take(self, name):
        kind = name[2:].split("_", 1)
        key = ("f" if kind[0] in ("f1", "f2") else kind[0]) + "_" + kind[1]
        self.credit = min(self.credit, 0.0) + CARRIER_US.get(key, 0.0)

        def fits(cost):
            if self.credit < 0.5 * cost:
                return False
            self.credit -= cost
            return True

        return self._jobs(fits)

    def finish(self, task=None):
        while self.queue and (task is None or not task.complete()):
            jobs = []
            while self.queue and self.queue[0].ready():
                head = self.queue.pop(0)
                jobs.append(head.next_job(len(head.parts) - head.done))
                if head is task:
                    break
            assert jobs, "the task at the head of the queue waits for one that was never queued"
            _comm_call(jobs, name=f"exchange_{self.n_flush}")
            self.n_flush += 1


def _mm(a, b, *, ta=False, tb=False, tm, tn, b_lead=(), into=None, o_lead=(), out_dtype=F32, res=None, alpha=1.0,
        ctx=None, name):
    M = a.shape[1] if ta else a.shape[0]
    K = a.shape[0] if ta else a.shape[1]
    bk, bn = (b.shape[-1], b.shape[-2]) if tb else (b.shape[-2], b.shape[-1])
    j_lead, k_lead = "j" in b_lead, "k" in b_lead
    N = bn * (b.shape[b_lead.index("j")] if j_lead else 1)
    nk = b.shape[b_lead.index("k")] if k_lead else 1
    tk = K // nk
    assert tk == bk, (name, K, nk, bk)
    tm = min(tm, M)
    tn = bn if j_lead else min(tn, N)
    assert M % tm == 0 and N % tn == 0, (name, M, N, tm, tn)
    nb, no = len(b_lead), len(o_lead)

    def pick(lead, j, k):
        return tuple(j if t == "j" else k if t == "k" else t for t in lead)

    a_spec = pl.BlockSpec((tk, tm), lambda i, j, k: (k, i)) if ta else pl.BlockSpec((tm, tk), lambda i, j, k: (i, k))
    if tb:
        b_spec = pl.BlockSpec((None,) * nb + (tn, tk),
                              lambda i, j, k: pick(b_lead, j, k) + (0 if j_lead else j, 0 if k_lead else k))
    else:
        b_spec = pl.BlockSpec((None,) * nb + (tk, tn),
                              lambda i, j, k: pick(b_lead, j, k) + (0 if k_lead else k, 0 if j_lead else j))
    r_spec = pl.BlockSpec((tm, tn), lambda i, j, k: (i, j))
    if o_lead == ("heads",):
        o_spec = pl.BlockSpec((tn // DH, tm, DH), lambda i, j, k: (j, i, 0))
    else:
        o_spec = pl.BlockSpec((None,) * no + (tm, tn),
                              lambda i, j, k: pick(o_lead, j, k) + (i, 0 if "j" in o_lead else j))
    dn = (((0 if ta else 1,), (1 if tb else 0,)), ((), ()))
    has_res, has_into = res is not None, into is not None
    if has_into:
        out_dtype = into.dtype

    def body(*refs):
        a_ref, b_ref = refs[0], refs[1]
        r_ref = refs[2] if has_res else None
        o_ref = refs[2 + has_res + has_into]
        prod = lax.dot_general(a_ref[...].astype(_CDT), b_ref[...].astype(_CDT), dn, preferred_element_type=F32)

        def finish(acc):
            if alpha != 1.0:
                acc = acc * alpha
            if has_res:
                acc = r_ref[...] + acc
            if o_lead == ("heads",):
                for hd in range(tn // DH):
                    o_ref[hd] = acc[:, hd * DH:(hd + 1) * DH].astype(out_dtype)
            else:
                o_ref[...] = acc.astype(out_dtype)

        if nk == 1:
            finish(prod)
        else:
            acc_ref = refs[-1]
            k = pl.program_id(2)

            @pl.when(k == 0)
            def _():
                acc_ref[...] = prod

            @pl.when(k != 0)
            def _():
                acc_ref[...] += prod

            @pl.when(k == nk - 1)
            def _():
                finish(acc_ref[...])

    ins, specs = [a, b], [a_spec, b_spec]
    if has_res:
        ins.append(res)
        specs.append(r_spec)
    aliases = {}
    if has_into:
        aliases = {len(ins): 0}
        ins.append(into)
        specs.append(_ANY)
        out_shape = jax.ShapeDtypeStruct(into.shape, into.dtype)
    elif o_lead == ("j",):
        out_shape = jax.ShapeDtypeStruct((N // tn, M, tn), out_dtype)
    elif o_lead == ("heads",):
        out_shape = jax.ShapeDtypeStruct((N // DH, M, DH), out_dtype)
    else:
        assert not o_lead
        out_shape = jax.ShapeDtypeStruct((M, N), out_dtype)
    scratch = [pltpu.VMEM((tm, tn), F32)] if nk > 1 else []
    return _pcall(body, out_shape=out_shape, grid=(M // tm, N // tn, nk), in_specs=specs, out_specs=o_spec,
                  scratch_shapes=scratch, aliases=aliases, name=name, jobs=ctx.take(name) if ctx else (), ins=ins)


def _rows2d(a):
    return a.reshape(-1, a.shape[-1])


def _row_tile(rows, cols, itemsize=4, budget=2 * 1024 * 1024):
    t = rows
    while t % 2 == 0 and t * cols * itemsize > budget and (t // 2) % 8 == 0:
        t //= 2
    return t


def _addn(*xs, name):
    shape = xs[0].shape
    x2 = [_rows2d(x) for x in xs]
    R, C = x2[0].shape
    tr = _row_tile(R, C)

    def body(*refs):
        acc = refs[0][...]
        for r in refs[1:-1]:
            acc = acc + r[...]
        refs[-1][...] = acc

    spec = pl.BlockSpec((tr, C), lambda i: (i, 0))
    out = pl.pallas_call(body, out_shape=jax.ShapeDtypeStruct((R, C), F32), grid=(R // tr,),
                         in_specs=[spec] * len(x2), out_specs=spec, compiler_params=_params(), name=name)(*x2)
    return out.reshape(shape)


def _rms_fwd(x, g, *, name):
    S, D = x.shape
    tr = 256

    def body(x_ref, g_ref, h_ref):
        xv = x_ref[...]
        rstd = lax.rsqrt(jnp.mean(xv * xv, axis=-1, keepdims=True) + RMS_EPS)
        h_ref[...] = (xv * rstd * g_ref[...]).astype(_CDT)

    return pl.pallas_call(body, out_shape=jax.ShapeDtypeStruct((S, D), _CDT), grid=(S // tr,),
                          in_specs=[pl.BlockSpec((tr, D), lambda i: (i, 0)), pl.BlockSpec((1, D), lambda i: (0, 0))],
                          out_specs=pl.BlockSpec((tr, D), lambda i: (i, 0)), compiler_params=_params(),
                          name=name)(x, g.reshape(1, D))


def _rms_bwd(dh, x, g, dres, *, name):
    S, D = x.shape
    tr = 256

    def body(dh_ref, x_ref, g_ref, dres_ref, dx_ref, dg_ref):
        xv = x_ref[...]
        rstd = lax.rsqrt(jnp.mean(xv * xv, axis=-1, keepdims=True) + RMS_EPS)
        xhat = xv * rstd
        dhv = dh_ref[...]
        dxhat = dhv * g_ref[...]
        dx_ref[...] = dres_ref[...] + rstd * (dxhat - xhat * jnp.mean(dxhat * xhat, axis=-1, keepdims=True))

        @pl.when(pl.program_id(0) == 0)
        def _():
            dg_ref[...] = jnp.zeros_like(dg_ref)

        dg_ref[...] += jnp.sum(dhv * xhat, axis=0, keepdims=True)

    row = pl.BlockSpec((tr, D), lambda i: (i, 0))
    vec = pl.BlockSpec((1, D), lambda i: (0, 0))
    dx, dg = pl.pallas_call(body, out_shape=(jax.ShapeDtypeStruct((S, D), F32), jax.ShapeDtypeStruct((1, D), F32)),
                            grid=(S // tr,), in_specs=[row, row, vec, row], out_specs=(row, vec),
                            compiler_params=_params(), name=name)(dh, x, g.reshape(1, D), dres)
    return dx, dg.reshape(D)


def _loss_head(x, g, tgt, *, name):
    S, D = x.shape
    tr = 256

    def body(x_ref, g_ref, t_ref, loss_ref, dx_ref, dg_ref):
        xv = x_ref[...]
        rstd = lax.rsqrt(jnp.mean(xv * xv, axis=-1, keepdims=True) + RMS_EPS)
        xhat = xv * rstd
        err = xhat * g_ref[...] - t_ref[...]
        part = 0.5 * jnp.sum(jnp.mean(err * err, axis=-1, keepdims=True), axis=0, keepdims=True)
        dy = err * (1.0 / D)
        dxhat = dy * g_ref[...]
        dx_ref[...] = rstd * (dxhat - xhat * jnp.mean(dxhat * xhat, axis=-1, keepdims=True))

        @pl.when(pl.program_id(0) == 0)
        def _():
            dg_ref[...] = jnp.zeros_like(dg_ref)
            loss_ref[...] = jnp.zeros_like(loss_ref)

        dg_ref[...] += jnp.sum(dy * xhat, axis=0, keepdims=True)
        loss_ref[...] += jnp.broadcast_to(part, loss_ref.shape)

    row = pl.BlockSpec((tr, D), lambda i: (i, 0))
    vec = pl.BlockSpec((1, D), lambda i: (0, 0))
    loss, dx, dg = pl.pallas_call(
        body, out_shape=(jax.ShapeDtypeStruct((8, LANES), F32), jax.ShapeDtypeStruct((S, D), F32),
                         jax.ShapeDtypeStruct((1, D), F32)),
        grid=(S // tr,), in_specs=[row, vec, row], out_specs=(pl.BlockSpec((8, LANES), lambda i: (0, 0)), row, vec),
        compiler_params=_params(), name=name)(x, g.reshape(1, D), tgt)
    return loss[0, 0], dx, dg.reshape(D)


def _rope_tables(S):
    inv = 10000.0 ** (-jnp.arange(0, DH, 2, dtype=F32) / DH)
    ang = jnp.arange(S, dtype=F32)[:, None] * inv[None, :]
    cos, sin = jnp.cos(ang), jnp.sin(ang)
    return jnp.concatenate([cos, cos], -1), jnp.concatenate([-sin, sin], -1)


def _swap_matrix():
    p = np.zeros((DH, DH), np.float32)
    for j in range(DH // 2):
        p[j + DH // 2, j] = 1.0
        p[j, j + DH // 2] = 1.0
    return jnp.asarray(p, jnp.bfloat16)


def _rope(x, cc, ss, rotated, *, name):
    n, S, _ = x.shape
    flags = jnp.asarray([int(r) for r in rotated], jnp.int32)
    ctab = jnp.stack([jnp.ones_like(cc), cc])
    stab = jnp.stack([jnp.zeros_like(ss), ss])

    def body(_, x_ref, c_ref, s_ref, p_ref, o_ref):
        xv = x_ref[0]
        o_ref[0] = xv * c_ref[...] + _dot_exact(xv, p_ref[...]) * s_ref[...]

    tab = pl.BlockSpec((None, S, DH), lambda i, f: (f[i], 0, 0))
    blk = pl.BlockSpec((1, S, DH), lambda i, f: (i, 0, 0))
    spec = pltpu.PrefetchScalarGridSpec(num_scalar_prefetch=1, grid=(n,),
                                        in_specs=[blk, tab, tab, pl.BlockSpec((DH, DH), lambda i, f: (0, 0))],
                                        out_specs=blk)
    return pl.pallas_call(body, grid_spec=spec, out_shape=jax.ShapeDtypeStruct(x.shape, F32),
                          compiler_params=_params(), name=name)(flags, x, ctab, stab, _swap_matrix())


def _key_range(kind, i, window, Sk):
    if kind == "cmp":
        return 0, Sk
    hi = (i + 1) * BQ
    if kind == "band":
        return max(0, i * BQ - window), hi
    return 0, hi


def _attn_mask(kind, i, lo, hi, window):
    shape = (BQ, hi - lo)
    qpos = i * BQ + lax.broadcasted_iota(jnp.int32, shape, 0)
    kpos = lo + lax.broadcasted_iota(jnp.int32, shape, 1)
    if kind == "cmp":
        return kpos * 16 + 31 <= qpos
    mask = kpos <= qpos
    if kind == "band":
        mask = mask & (qpos - kpos < window)
    return mask


def _sel_expand(lo, hi):
    shape = (LANES, hi - lo)
    j = lax.broadcasted_iota(jnp.int32, shape, 0)
    key = lo + lax.broadcasted_iota(jnp.int32, shape, 1)
    return (jnp.right_shift(key, 6) == j).astype(_CDT)


def _eye():
    return lax.broadcasted_iota(jnp.int32, (BQ, BQ), 0) == lax.broadcasted_iota(jnp.int32, (BQ, BQ), 1)


def _to_col(row):
    return jnp.sum(jnp.where(_eye(), row, 0.0), axis=1, keepdims=True)


def _to_row(col):
    return jnp.sum(jnp.where(_eye(), col, 0.0), axis=0, keepdims=True)


def _scores(kind, i, lo, hi, window, qb, kb, crow_ref, sel_ref):
    s = _dot_nt(qb, kb) * SCALE
    if crow_ref is not None:
        s = s + _to_col(crow_ref[0, :, i * BQ:(i + 1) * BQ]) - crow_ref[0, :, lo:hi]
    mask = _attn_mask(kind, i, lo, hi, window)
    if sel_ref is not None:
        chosen = _dot_tn(sel_ref[0, :, i * BQ:(i + 1) * BQ].astype(_CDT), _sel_expand(lo, hi))
        mask = mask & (chosen > 0.5)
    return jnp.where(mask, s, NEG), mask


def _attn_fwd(q, k, v, *, G, R, heads=(0, 0, 0), kind, window=0, bias=None, sinks=None, selT=None, ovT=None, ctx=None,
              name):
    S = q.shape[1]
    Sk = k.shape[1]
    q0, k0, v0 = heads
    nq = S // BQ
    n_slc = S // 64
    has_bias, has_sink, has_sel, is_cmp = bias is not None, sinks is not None, selT is not None, kind == "cmp"

    def body(*refs):
        it = iter(refs)
        q_ref, k_ref, v_ref = next(it), next(it), next(it)
        crow_ref = next(it) if has_bias else None
        sink_ref = next(it) if has_sink else None
        sel_ref = next(it) if has_sel else None
        ov_ref = next(it) if is_cmp else None
        o_ref, lse_ref = next(it), next(it)
        selo_ref, imp_ref = (next(it), next(it)) if is_cmp else (None, None)
        r = pl.program_id(1)
        for i in range(nq):
            lo, hi = _key_range(kind, i, window, Sk)
            rows = slice(i * BQ, (i + 1) * BQ)
            qb = q_ref[rows, :].astype(_CDT)
            kb = k_ref[0, lo:hi, :].astype(_CDT)
            vb = v_ref[0, lo:hi, :].astype(_CDT)
            s, mask = _scores(kind, i, lo, hi, window, qb, kb, crow_ref, sel_ref)
            m = jnp.max(s, axis=-1, keepdims=True)
            if has_sink:
                sk = sink_ref[0, 0, 0:1, 0:1]
                m = jnp.maximum(m, sk)
            e = jnp.exp(s - m)
            if is_cmp:
                e = jnp.where(mask, e, 0.0)
            l = jnp.sum(e, axis=-1, keepdims=True)
            if has_sink:
                l = l + jnp.exp(sk - m)
            if is_cmp:
                l = jnp.where(l > 0.0, l, 1.0)
            p = e * (1.0 / l)
            o_ref[0, 0, rows, :] = _dot(p.astype(_CDT), vb)
            lse_ref[0, 0, :, rows] = _to_row(m + jnp.log(l))
            if is_cmp:
                part = _dot_nt(ov_ref[...].astype(_CDT), p.astype(_CDT))

                @pl.when(r == 0)
                def _():
                    imp_ref[:, rows] = part

                @pl.when(r != 0)
                def _():
                    imp_ref[:, rows] += part

        if is_cmp:
            @pl.when(r == R - 1)
            def _():
                shape = (LANES, S)
                j = lax.broadcasted_iota(jnp.int32, shape, 0)
                tb = jnp.right_shift(lax.broadcasted_iota(jnp.int32, shape, 1), 6)
                forced = (j == 0) | (j == tb) | (j == tb - 1)
                imp = jnp.where(j > tb, NEG, jnp.where(forced, NSA_BONUS, imp_ref[...]))
                imp = jnp.where(j >= n_slc, -3e38, imp)
                imp_ref[...] = imp
                cnt = jnp.zeros(shape, F32)
                for jp in range(n_slc):
                    row = imp_ref[jp:jp + 1, :]
                    ahead = (row > imp) | ((row == imp) & (jp < j))
                    cnt = cnt + ahead.astype(F32)
                selo_ref[0] = (cnt < float(min(NSA_TOPK, n_slc))).astype(F32)

    qspec = pl.BlockSpec((1, 1, S, DH), lambda g, r: (g, r, 0, 0))
    ins = [q, k, v]
    specs = [pl.BlockSpec((None, S, DH), lambda g, r: (q0 + g * R + r, 0, 0)),
             pl.BlockSpec((1, Sk, DH), lambda g, r: (k0 + g, 0, 0)), pl.BlockSpec((1, Sk, DH), lambda g, r: (v0 + g, 0, 0))]
    if has_bias:
        ins.append(bias)
        specs.append(pl.BlockSpec((1, 1, S), lambda g, r: (g, 0, 0)))
    if has_sink:
        ins.append(sinks)
        specs.append(pl.BlockSpec((1, 1, 8, LANES), lambda g, r: (g, r, 0, 0)))
    if has_sel:
        ins.append(selT)
        specs.append(pl.BlockSpec((1, LANES, S), lambda g, r: (g, 0, 0)))
    if is_cmp:
        ins.append(ovT)
        specs.append(pl.BlockSpec((LANES, Sk), lambda g, r: (0, 0)))
    outs = [jax.ShapeDtypeStruct((G, R, S, DH), F32), jax.ShapeDtypeStruct((G, R, 1, S), F32)]
    ospecs = [qspec, pl.BlockSpec((1, 1, 1, S), lambda g, r: (g, r, 0, 0))]
    scratch = []
    if is_cmp:
        outs.append(jax.ShapeDtypeStruct((G, LANES, S), F32))
        ospecs.append(pl.BlockSpec((1, LANES, S), lambda g, r: (g, 0, 0)))
        scratch.append(pltpu.VMEM((LANES, S), F32))
    return _pcall(body, out_shape=tuple(outs), grid=(G, R), in_specs=specs, out_specs=tuple(ospecs),
                  scratch_shapes=scratch, name=name, jobs=ctx.take(name) if ctx else (), ins=ins)


def _attn_bwd(q, k, v, o, do, lse, *, heads=(0, 0, 0), kind, window=0, bias=None, sinks=None, selT=None, gate=None,
              ctx=None, name):
    G, R, S, _ = o.shape
    Sk = k.shape[1]
    q0, k0, v0 = heads
    nq = S // BQ
    has_bias, has_sink, has_sel, has_gate = bias is not None, sinks is not None, selT is not None, gate is not None

    def body(*refs):
        it = iter(refs)
        q_ref, k_ref, v_ref, o_ref, do_ref, lse_ref = (next(it) for _ in range(6))
        crow_ref = next(it) if has_bias else None
        sink_ref = next(it) if has_sink else None
        sel_ref = next(it) if has_sel else None
        z_ref = next(it) if has_gate else None
        dq_ref, dk_ref, dv_ref = next(it), next(it), next(it)
        dc_ref = next(it) if has_bias else None
        dsink_ref = next(it) if has_sink else None
        dz_ref = next(it) if has_gate else None
        r = pl.program_id(1)

        @pl.when(r == 0)
        def _():
            dk_ref[...] = jnp.zeros_like(dk_ref)
            dv_ref[...] = jnp.zeros_like(dv_ref)
            if has_bias:
                dc_ref[...] = jnp.zeros_like(dc_ref)

        dsink = jnp.zeros((1, 1), F32)
        for i in range(nq):
            lo, hi = _key_range(kind, i, window, Sk)
            rows = slice(i * BQ, (i + 1) * BQ)
            qb = q_ref[rows, :].astype(_CDT)
            kb = k_ref[0, lo:hi, :].astype(_CDT)
            vb = v_ref[0, lo:hi, :].astype(_CDT)
            s, mask = _scores(kind, i, lo, hi, window, qb, kb, crow_ref, sel_ref)
            lse_i = _to_col(lse_ref[0, 0, :, rows])
            p = jnp.where(mask, jnp.exp(s - lse_i), 0.0)
            dob = do_ref[0, 0, rows, :]
            if has_gate:
                od = jnp.sum(o_ref[0, 0, rows, :] * dob, axis=-1, keepdims=True)
                sg = jax.nn.sigmoid(_to_col(z_ref[0, 0, :, rows]))
                dob = dob * sg
                dz_ref[0, 0, :, rows] = _to_row(od * sg * (1.0 - sg))
            dob = dob.astype(_CDT)
            dp = _dot_nt(dob, vb)
            delta = jnp.sum(p * dp, axis=-1, keepdims=True)
            ds = p * (dp - delta)
            dsb = ds.astype(_CDT)
            dq_ref[0, 0, rows, :] = _dot(dsb, kb) * SCALE
            dk_ref[0, lo:hi, :] += _dot_tn(dsb, qb) * SCALE
            dv_ref[0, lo:hi, :] += _dot_tn(p.astype(_CDT), dob)
            if has_bias:
                dc_ref[0, :, rows] += _to_row(jnp.sum(ds, axis=-1, keepdims=True))
                dc_ref[0, :, lo:hi] -= jnp.sum(ds, axis=0, keepdims=True)
            if has_sink:
                sk = sink_ref[0, 0, 0:1, 0:1]
                dsink = dsink - jnp.sum(jnp.exp(sk - lse_i) * delta, axis=0, keepdims=True)
        if has_sink:
            dsink_ref[0, 0] = jnp.broadcast_to(dsink, (8, LANES))

    qspec = pl.BlockSpec((1, 1, S, DH), lambda g, r: (g, r, 0, 0))
    cspec = pl.BlockSpec((1, 1, 1, S), lambda g, r: (g, r, 0, 0))
    kspec = pl.BlockSpec((1, Sk, DH), lambda g, r: (g, 0, 0))
    ins = [q, k, v, o, do, lse]
    specs = [pl.BlockSpec((None, S, DH), lambda g, r: (q0 + g * R + r, 0, 0)),
             pl.BlockSpec((1, Sk, DH), lambda g, r: (k0 + g, 0, 0)), pl.BlockSpec((1, Sk, DH), lambda g, r: (v0 + g, 0, 0)),
             qspec, qspec, cspec]
    if has_bias:
        ins.append(bias)
        specs.append(pl.BlockSpec((1, 1, S), lambda g, r: (g, 0, 0)))
    if has_sink:
        ins.append(sinks)
        specs.append(pl.BlockSpec((1, 1, 8, LANES), lambda g, r: (g, r, 0, 0)))
    if has_sel:
        ins.append(selT)
        specs.append(pl.BlockSpec((1, LANES, S), lambda g, r: (g, 0, 0)))
    if has_gate:
        ins.append(gate)
        specs.append(cspec)
    names = ["dq", "dk", "dv"]
    outs = [jax.ShapeDtypeStruct((G, R, S, DH), F32), jax.ShapeDtypeStruct((G, Sk, DH), F32),
            jax.ShapeDtypeStruct((G, Sk, DH), F32)]
    ospecs = [qspec, kspec, kspec]
    if has_bias:
        assert R == 1
        names.append("dc")
        outs.append(jax.ShapeDtypeStruct((G, 1, S), F32))
        ospecs.append(pl.BlockSpec((1, 1, S), lambda g, r: (g, 0, 0)))
    if has_sink:
        names.append("dsink")
        outs.append(jax.ShapeDtypeStruct((G, R, 8, LANES), F32))
        ospecs.append(pl.BlockSpec((1, 1, 8, LANES), lambda g, r: (g, r, 0, 0)))
    if has_gate:
        names.append("dz")
        outs.append(jax.ShapeDtypeStruct((G, R, 1, S), F32))
        ospecs.append(cspec)
    res = _pcall(body, out_shape=tuple(outs), grid=(G, R), in_specs=specs, out_specs=tuple(ospecs), name=name,
                 jobs=ctx.take(name) if ctx else (), ins=ins)
    return dict(zip(names, res))


def _combine(o0, o1, o2, z, *, name):
    H, S, _ = o0.shape

    def body(o0_ref, o1_ref, o2_ref, z_ref, o_ref):
        for i in range(S // BQ):
            rows = slice(i * BQ, (i + 1) * BQ)
            acc = jax.nn.sigmoid(_to_col(z_ref[0, 0, :, rows])) * o0_ref[0, rows, :]
            acc = acc + jax.nn.sigmoid(_to_col(z_ref[1, 0, :, rows])) * o1_ref[0, rows, :]
            acc = acc + jax.nn.sigmoid(_to_col(z_ref[2, 0, :, rows])) * o2_ref[0, rows, :]
            o_ref[0, rows, :] = acc

    blk = pl.BlockSpec((1, S, DH), lambda h: (h, 0, 0))
    return pl.pallas_call(body, out_shape=jax.ShapeDtypeStruct((H, S, DH), F32), grid=(H,),
                          in_specs=[blk, blk, blk, pl.BlockSpec((3, 1, 1, S), lambda h: (0, h, 0, 0))], out_specs=blk,
                          compiler_params=_params(), name=name)(o0, o1, o2, z)


_GC = math.sqrt(2.0 / math.pi)


def _gelu(x):
    return 0.5 * x * (1.0 + jnp.tanh(_GC * (x + 0.044715 * x * x * x)))


def _gelu_grad(x):
    t = jnp.tanh(_GC * (x + 0.044715 * x * x * x))
    return 0.5 * (1.0 + t) + 0.5 * x * (1.0 - t * t) * _GC * (1.0 + 3.0 * 0.044715 * x * x)


def _make_xb(k):
    G, S, _ = k.shape
    chunks = k.reshape(G, S // 16, 16 * DH)
    shift = jnp.concatenate([chunks[:, 1:], jnp.zeros((G, 1, 16 * DH), k.dtype)], axis=1)
    return jnp.concatenate([chunks, shift], axis=-1)


def _unmake_xb(dxb, *, name):
    G, n, _ = dxb.shape
    a = dxb[..., :16 * DH]
    b = jnp.concatenate([jnp.zeros((G, 1, 16 * DH), F32), dxb[:, :-1, 16 * DH:]], axis=1)
    return _addn(a, b, name=name).reshape(G, n * 16, DH)


def _compress_fwd(xb, pe, w1, w2, *, name):
    G, n, W = xb.shape
    Hc = w1.shape[1]

    def body(xb_ref, pe_ref, w1_ref, w2_ref, kc_ref, hid_ref):
        xv = (xb_ref[0] + pe_ref[...]).astype(_CDT)
        hid = _dot(xv, w1_ref[...].astype(_CDT))
        hid_ref[0] = hid
        kc_ref[0] = _dot(_gelu(hid).astype(_CDT), w2_ref[...].astype(_CDT))

    return pl.pallas_call(
        body, out_shape=(jax.ShapeDtypeStruct((G, n, DH), F32), jax.ShapeDtypeStruct((G, n, Hc), F32)), grid=(G,),
        in_specs=[pl.BlockSpec((1, n, W), lambda g: (g, 0, 0)), pl.BlockSpec((1, W), lambda g: (0, 0)),
                  pl.BlockSpec((W, Hc), lambda g: (0, 0)), pl.BlockSpec((Hc, DH), lambda g: (0, 0))],
        out_specs=(pl.BlockSpec((1, n, DH), lambda g: (g, 0, 0)), pl.BlockSpec((1, n, Hc), lambda g: (g, 0, 0))),
        compiler_params=_params(), name=name)(xb, pe.reshape(1, W), w1, w2)


def _compress_bwd(xb, pe, w1, w2, hid, dkc, *, name):
    G, n, W = xb.shape
    Hc = w1.shape[1]

    def body(xb_ref, pe_ref, w1_ref, w2_ref, hid_ref, dkc_ref, dxb_ref, dw1_ref, dw2_ref, dpe_ref):
        @pl.when(pl.program_id(0) == 0)
        def _():
            dw1_ref[...] = jnp.zeros_like(dw1_ref)
            dw2_ref[...] = jnp.zeros_like(dw2_ref)
            dpe_ref[...] = jnp.zeros_like(dpe_ref)

        xv = (xb_ref[0] + pe_ref[...]).astype(_CDT)
        hid = hid_ref[0]
        dk = dkc_ref[0].astype(_CDT)
        dact = _dot_nt(dk, w2_ref[...].astype(_CDT))
        dhid = (dact * _gelu_grad(hid)).astype(_CDT)
        dw2_ref[...] += _dot_tn(_gelu(hid).astype(_CDT), dk)
        dxb = _dot_nt(dhid, w1_ref[...].astype(_CDT))
        dxb_ref[0] = dxb
        dw1_ref[...] += _dot_tn(xv, dhid)
        dpe_ref[...] += jnp.sum(dxb, axis=0, keepdims=True)

    return pl.pallas_call(
        body, out_shape=(jax.ShapeDtypeStruct((G, n, W), F32), jax.ShapeDtypeStruct((W, Hc), F32),
                         jax.ShapeDtypeStruct((Hc, DH), F32), jax.ShapeDtypeStruct((1, W), F32)), grid=(G,),
        in_specs=[pl.BlockSpec((1, n, W), lambda g: (g, 0, 0)), pl.BlockSpec((1, W), lambda g: (0, 0)),
                  pl.BlockSpec((W, Hc), lambda g: (0, 0)), pl.BlockSpec((Hc, DH), lambda g: (0, 0)),
                  pl.BlockSpec((1, n, Hc), lambda g: (g, 0, 0)), pl.BlockSpec((1, n, DH), lambda g: (g, 0, 0))],
        out_specs=(pl.BlockSpec((1, n, W), lambda g: (g, 0, 0)), pl.BlockSpec((W, Hc), lambda g: (0, 0)),
                   pl.BlockSpec((Hc, DH), lambda g: (0, 0)), pl.BlockSpec((1, W), lambda g: (0, 0))),
        compiler_params=_params(), name=name)(xb, pe.reshape(1, W), w1, w2, hid, dkc)


def _overlap_T(S):
    n_cmp, n_slc = S // 16 - 1, S // 64
    cs = np.arange(n_cmp) * 16
    ce = cs + 32
    ss = np.arange(n_slc) * 64
    se = ss + 64
    ov = np.clip(np.minimum(ce[:, None], se[None, :]) - np.maximum(cs[:, None], ss[None, :]), 0, None) / 32.0
    out = np.zeros((LANES, S // 16), np.float32)
    out[:n_slc, :n_cmp] = ov.T
    return jnp.asarray(out)


def _tri(n, upper):
    r = lax.broadcasted_iota(jnp.int32, (n, n), 0)
    c = lax.broadcasted_iota(jnp.int32, (n, n), 1)
    return ((c >= r) if upper else (c <= r)).astype(jnp.bfloat16)


def _fox_gate_fwd(zf, b, *, name):
    S, H = zf.shape
    nb = S // BQ

    def body(z_ref, b_ref, c_ref):
        tri = _tri(BQ, False)
        carry = jnp.zeros((1, H), F32)
        for i in range(nb):
            z = z_ref[i * BQ:(i + 1) * BQ, :] + b_ref[...]
            lf = jnp.minimum(z, 0.0) - jnp.log(1.0 + jnp.exp(-jnp.abs(z)))
            c_ref[i * BQ:(i + 1) * BQ, :] = _dot_exact_left(tri, lf) + carry
            carry = carry + jnp.sum(lf, axis=0, keepdims=True)

    return pl.pallas_call(body, out_shape=jax.ShapeDtypeStruct((S, H), F32), compiler_params=_params(),
                          name=name)(zf, b)


def _fox_gate_bwd(zf, b, dc, *, name):
    S, H = zf.shape
    nb = S // BQ

    def body(z_ref, b_ref, dc_ref, dz_ref, db_ref):
        tri = _tri(BQ, True)
        carry = jnp.zeros((1, H), F32)
        db = jnp.zeros((1, H), F32)
        for i in reversed(range(nb)):
            rows = slice(i * BQ, (i + 1) * BQ)
            dcb = dc_ref[rows, :]
            dlf = _dot_exact_left(tri, dcb) + carry
            carry = carry + jnp.sum(dcb, axis=0, keepdims=True)
            z = z_ref[rows, :] + b_ref[...]
            dz = dlf * jax.nn.sigmoid(-z)
            dz_ref[rows, :] = dz
            db = db + jnp.sum(dz, axis=0, keepdims=True)
        db_ref[...] = db

    return pl.pallas_call(body, out_shape=(jax.ShapeDtypeStruct((S, H), F32), jax.ShapeDtypeStruct((1, H), F32)),
                          compiler_params=_params(), name=name)(zf, b, dc)


def _from_heads(a):
    return a.transpose(1, 0, 2).reshape(a.shape[1], -1)


def _pad_lanes(a):
    return jnp.pad(a, ((0, 0), (0, LANES - a.shape[1])))


FFN_TM = 512


def _ffn_up(h, w4, *, ctx, name):
    S, D = h.shape
    C = w4.shape[2]

    def body(h_ref, wg_ref, wu_ref, gu_ref, a_ref):
        hv = h_ref[...].astype(_CDT)
        g = _dot(hv, wg_ref[...].astype(_CDT))
        u = _dot(hv, wu_ref[...].astype(_CDT))
        gu_ref[0] = g
        gu_ref[1] = u
        a_ref[...] = (g * jax.nn.sigmoid(g) * u).astype(_CDT)

    return _pcall(
        body, out_shape=(jax.ShapeDtypeStruct((2, S, 2 * C), F32), jax.ShapeDtypeStruct((S, 2 * C), _CDT)),
        grid=(S // FFN_TM, 2),
        in_specs=[pl.BlockSpec((FFN_TM, D), lambda i, j: (i, 0)), pl.BlockSpec((None, D, C), lambda i, j: (j, 0, 0)),
                  pl.BlockSpec((None, D, C), lambda i, j: (j + 2, 0, 0))],
        out_specs=(pl.BlockSpec((2, FFN_TM, C), lambda i, j: (0, i, j)), pl.BlockSpec((FFN_TM, C), lambda i, j: (i, j))),
        name=name, jobs=ctx.take(name) if ctx else (), ins=[h, w4, w4])


def _ffn_dact(dxo, wd, gu, *, ctx, name):
    S, D = dxo.shape
    C = gu.shape[2] // 2

    def body(dx_ref, wd_ref, gu_ref, d_ref):
        da = _dot_nt(dx_ref[...].astype(_CDT), wd_ref[...].astype(_CDT)) * 0.5
        g, u = gu_ref[0], gu_ref[1]
        sg = jax.nn.sigmoid(g)
        silu = g * sg
        d_ref[0] = da * u * (sg + silu * (1.0 - sg))
        d_ref[1] = da * silu

    blk = pl.BlockSpec((2, FFN_TM, C), lambda i, j: (0, i, j))
    return _pcall(body, out_shape=jax.ShapeDtypeStruct(gu.shape, F32), grid=(S // FFN_TM, 2),
                  in_specs=[pl.BlockSpec((FFN_TM, D), lambda i, j: (i, 0)), pl.BlockSpec((C, D), lambda i, j: (j, 0)), blk],
                  out_specs=blk, name=name, jobs=ctx.take(name) if ctx else (), ins=[dxo, wd, gu])


def _ffn_dwgu(h, dgu, *, ctx, name):
    S, D = h.shape
    C = dgu.shape[2] // 2
    tm = 512

    def body(h_ref, d_ref, o_ref):
        o_ref[...] = _dot_tn(h_ref[...].astype(_CDT), d_ref[...].astype(_CDT)).astype(_WIRE)

    return _pcall(body, out_shape=jax.ShapeDtypeStruct((4, D, C), _WIRE), grid=(D // tm, 4),
                  in_specs=[pl.BlockSpec((S, tm), lambda i, j: (0, i)),
                            pl.BlockSpec((None, S, C), lambda i, j: (j // 2, 0, j % 2))],
                  out_specs=pl.BlockSpec((None, tm, C), lambda i, j: (j, i, 0)), name=name,
                  jobs=ctx.take(name) if ctx else (), ins=[h, dgu])


def _ffn_dh(dgu, w4, *, ctx, name):
    _, S, F2 = dgu.shape
    C, D = F2 // 2, w4.shape[1]

    def body(d_ref, w_ref, o_ref, acc_ref):
        k = pl.program_id(1)
        prod = _dot_nt(d_ref[...].astype(_CDT), w_ref[...].astype(_CDT))

        @pl.when(k == 0)
        def _():
            acc_ref[...] = prod

        @pl.when(k != 0)
        def _():
            acc_ref[...] += prod

        @pl.when(k == 3)
        def _():
            o_ref[...] = acc_ref[...]

    return _pcall(body, out_shape=jax.ShapeDtypeStruct((S, D), F32), grid=(S // FFN_TM, 4),
                  in_specs=[pl.BlockSpec((None, FFN_TM, C), lambda i, k: (k // 2, i, k % 2)),
                            pl.BlockSpec((None, D, C), lambda i, k: (k, 0, 0))],
                  out_specs=pl.BlockSpec((FFN_TM, D), lambda i, k: (i, 0)), scratch_shapes=[pltpu.VMEM((FFN_TM, D), F32)],
                  name=name, jobs=ctx.take(name) if ctx else (), ins=[dgu, w4])


def _ffn_fwd(x, g, P, l, tag):
    h = _rms_fwd(x, g, name=tag + "_rms")
    gu, a = _ffn_up(h, P.weight("gu", l), ctx=P.ctx, name=tag + "_gu")
    xo = _mm(a, P.weight("down", l), tm=512, tn=1024, res=x, alpha=0.5, ctx=P.ctx, name=tag + "_down")
    return xo, (x, h, gu, a)


def _ffn_bwd(dxo, saved, g, P, l, tag):
    x, h, gu, a = saved
    dgu = _ffn_dact(dxo, P.weight("down", l), gu, ctx=P.ctx, name=tag + "_dact")
    P.grad("gu", l, _ffn_dwgu(h, dgu, ctx=P.ctx, name=tag + "_dwgu"))
    dwd = _mm(a, dxo, ta=True, tm=1408, tn=1024, alpha=0.5, out_dtype=_WIRE, ctx=P.ctx, name=tag + "_dwd")
    P.grad("down", l, dwd.reshape(4, -1, D_MODEL))
    dh = _ffn_dh(dgu, P.weight("gu", l), ctx=P.ctx, name=tag + "_dh")
    dx, dg = _rms_bwd(dh, x, g, dxo, name=tag + "_drms")
    return dx, dg


def _nsa_fwd(x, g, w, cc, ss, tag):
    S = x.shape[0]
    h = _rms_fwd(x, g, name=tag + "_rms")
    hm = _mm(h, w["w_in"], tm=512, tn=896, o_lead=("heads",), ctx=w["ctx"], name=tag + "_in")
    z = hm[40, :, :3 * N_HEADS].reshape(S, 3, N_HEADS).transpose(1, 2, 0)[:, :, None, :]
    roped = _rope(hm[:40], cc, ss, NSA_ROTATED, name=tag + "_rope")
    xbk, xbv = _make_xb(roped[16:20]), _make_xb(roped[20:24])
    kc, hidk = _compress_fwd(xbk, w["ck_pe"], w["ck_w1"], w["ck_w2"], name=tag + "_ck")
    vc, hidv = _compress_fwd(xbv, w["cv_pe"], w["cv_w1"], w["cv_w2"], name=tag + "_cv")
    ctx, GR = w["ctx"], dict(G=NSA_G, R=NSA_R)
    o0, lse0, selT = _attn_fwd(roped, kc, vc, kind="cmp", ovT=_overlap_T(S), ctx=ctx, name=tag + "_cmp", **GR)
    o1, lse1 = _attn_fwd(roped, roped, roped, heads=(0, 24, 28), kind="sel", selT=selT, ctx=ctx, name=tag + "_slc", **GR)
    o2, lse2 = _attn_fwd(roped, roped, roped, heads=(0, 32, 36), kind="band", window=NSA_WINDOW, ctx=ctx,
                         name=tag + "_win", **GR)
    o = _combine(o0.reshape(N_HEADS, S, DH), o1.reshape(N_HEADS, S, DH), o2.reshape(N_HEADS, S, DH), z,
                 name=tag + "_mix")
    of = _from_heads(o)
    xo = _mm(of, w["out"], tm=512, tn=1024, res=x, ctx=w["ctx"], name=tag + "_out")
    saved = (x, h, roped, z, xbk, xbv, hidk, hidv, kc, vc, (o0, o1, o2), (lse0, lse1, lse2), selT, of)
    return xo, saved


def _out_bwd(dxo, of, w, tag):
    dof = _mm(dxo, w["out"], tb=True, tm=512, tn=1024, o_lead=("heads",), ctx=w["ctx"], name=tag + "_dof")
    dw = _mm(of, dxo, ta=True, tm=512, tn=1024, out_dtype=_WIRE, ctx=w["ctx"], name=tag + "_dwout")
    w["P"].grad("out", w["out_l"], dw.reshape(4, -1, D_MODEL))
    return dof


def _split_cols(dw, n_in):
    cs = n_in // 4
    return dw[:, :n_in].reshape(D_MODEL, 4, cs).transpose(1, 0, 2).astype(_WIRE)


def _nsa_bwd(dxo, saved, g, w, cc, ss, tag):
    x, h, roped, z, xbk, xbv, hidk, hidv, kc, vc, os_, lses, selT, of = saved
    S = x.shape[0]
    do = _out_bwd(dxo, of, w, tag).reshape(NSA_G, NSA_R, S, DH)
    zg = z.reshape(3, NSA_G, NSA_R, 1, S)
    ctx = w["ctx"]
    b0 = _attn_bwd(roped, kc, vc, os_[0], do, lses[0], kind="cmp", gate=zg[0], ctx=ctx, name=tag + "_dcmp")
    b1 = _attn_bwd(roped, roped, roped, os_[1], do, lses[1], heads=(0, 24, 28), kind="sel", selT=selT, gate=zg[1],
                   ctx=ctx, name=tag + "_dslc")
    b2 = _attn_bwd(roped, roped, roped, os_[2], do, lses[2], heads=(0, 32, 36), kind="band", window=NSA_WINDOW,
                   gate=zg[2], ctx=ctx, name=tag + "_dwin")
    dxbk, dck_w1, dck_w2, dck_pe = _compress_bwd(xbk, w["ck_pe"], w["ck_w1"], w["ck_w2"], hidk, b0["dk"],
                                                 name=tag + "_dck")
    dxbv, dcv_w1, dcv_w2, dcv_pe = _compress_bwd(xbv, w["cv_pe"], w["cv_w1"], w["cv_w2"], hidv, b0["dv"],
                                                 name=tag + "_dcv")
    dk0 = _unmake_xb(dxbk, name=tag + "_dk0")
    dv0 = _unmake_xb(dxbv, name=tag + "_dv0")
    dq = _addn(b0["dq"], b1["dq"], b2["dq"], name=tag + "_dqsum").reshape(N_HEADS, S, DH)
    dz = jnp.stack([b0["dz"], b1["dz"], b2["dz"]], axis=0).reshape(3 * N_HEADS, S).T
    dgates = jnp.pad(dz, ((0, 0), (0, 2 * DH - 3 * N_HEADS))).reshape(S, 2, DH).transpose(1, 0, 2)
    dhm = jnp.concatenate([dq, dk0, dv0, b1["dk"], b1["dv"], b2["dk"], b2["dv"], dgates], axis=0)
    dproj = _from_heads(_rope(dhm, cc, -ss, NSA_ROTATED + (False, False), name=tag + "_drope"))
    dh = _mm(dproj, w["w_in"], tb=True, tm=512, tn=512, ctx=w["ctx"], name=tag + "_dh")
    dw_in = _mm(h, dproj, ta=True, tm=512, tn=896, ctx=w["ctx"], name=tag + "_dwin_w")
    dx, dg = _rms_bwd(dh, x, g, dxo, name=tag + "_drms")
    P, j = w["P"], w["j"]
    P.grad("nsa_in", j, _split_cols(dw_in, NSA_IN))
    P.grad("cw1", j, dck_w1.astype(_WIRE).reshape(4, -1, dck_w1.shape[1]))
    P.grad("cw1", 2 + j, dcv_w1.astype(_WIRE).reshape(4, -1, dcv_w1.shape[1]))
    grads = dict(ck_pe=dck_pe.reshape(32, DH), ck_w2=dck_w2, cv_pe=dcv_pe.reshape(32, DH), cv_w2=dcv_w2)
    return dx, dg, grads


def _swa_fwd(x, g, w, cc, ss, tag):
    S = x.shape[0]
    h = _rms_fwd(x, g, name=tag + "_rms")
    hm = _mm(h, w["w_in"], tm=512, tn=640, o_lead=("heads",), ctx=w["ctx"], name=tag + "_in")
    roped = _rope(hm, cc, ss, SWA_ROTATED, name=tag + "_rope")
    sinks = jnp.broadcast_to(w["sinks"].reshape(SWA_G, SWA_R, 1, 1), (SWA_G, SWA_R, 8, LANES))
    o, lse = _attn_fwd(roped, roped, roped, G=SWA_G, R=SWA_R, heads=SWA_HEADS, kind="band", window=SWA_WINDOW,
                       sinks=sinks, ctx=w["ctx"], name=tag + "_attn")
    of = _from_heads(o.reshape(N_HEADS, S, DH))
    xo = _mm(of, w["out"], tm=512, tn=1024, res=x, ctx=w["ctx"], name=tag + "_out")
    return xo, (x, h, roped, sinks, o, lse, of)


def _swa_bwd(dxo, saved, g, w, cc, ss, tag):
    x, h, roped, sinks, o, lse, of = saved
    S = x.shape[0]
    do = _out_bwd(dxo, of, w, tag).reshape(SWA_G, SWA_R, S, DH)
    b = _attn_bwd(roped, roped, roped, o, do, lse, heads=SWA_HEADS, kind="band", window=SWA_WINDOW, sinks=sinks,
                  ctx=w["ctx"], name=tag + "_dattn")
    dhm = jnp.concatenate([b["dq"].reshape(N_HEADS, S, DH), b["dk"], b["dv"]], axis=0)
    dproj = _from_heads(_rope(dhm, cc, -ss, SWA_ROTATED, name=tag + "_drope"))
    dh = _mm(dproj, w["w_in"], tb=True, tm=512, tn=512, ctx=w["ctx"], name=tag + "_dh")
    dw_in = _mm(h, dproj, ta=True, tm=512, tn=640, ctx=w["ctx"], name=tag + "_dwin_w")
    dx, dg = _rms_bwd(dh, x, g, dxo, name=tag + "_drms")
    w["P"].grad("swa_in", w["j"], _split_cols(dw_in, SWA_IN))
    return dx, dg, dict(sinks=b["dsink"][:, :, 0, 0].reshape(N_HEADS))


def _fox_fwd(x, g, w, tag):
    S = x.shape[0]
    h = _rms_fwd(x, g, name=tag + "_rms")
    hm = _mm(h, w["w_in"], tm=512, tn=640, o_lead=("heads",), ctx=w["ctx"], name=tag + "_in")
    zf = jnp.pad(hm[48], ((0, 0), (0, LANES - DH)))
    bf = _pad_lanes(w["b_f"].reshape(1, N_HEADS))
    c = _fox_gate_fwd(zf, bf, name=tag + "_gate")[:, :N_HEADS]
    bias = c.T[:, None, :]
    o, lse = _attn_fwd(hm, hm, hm, G=N_HEADS, R=1, heads=FOX_HEADS, kind="causal", bias=bias, ctx=w["ctx"],
                       name=tag + "_attn")
    of = _from_heads(o.reshape(N_HEADS, S, DH))
    xo = _mm(of, w["out"], tm=512, tn=1024, res=x, ctx=w["ctx"], name=tag + "_out")
    return xo, (x, h, hm, zf, bf, bias, o, lse, of)


def _fox_bwd(dxo, saved, g, w, tag):
    x, h, hm, zf, bf, bias, o, lse, of = saved
    S = x.shape[0]
    do = _out_bwd(dxo, of, w, tag).reshape(N_HEADS, 1, S, DH)
    b = _attn_bwd(hm, hm, hm, o, do, lse, heads=FOX_HEADS, kind="causal", bias=bias, ctx=w["ctx"], name=tag + "_dattn")
    dzf, db = _fox_gate_bwd(zf, bf, _pad_lanes(b["dc"].reshape(N_HEADS, S).T), name=tag + "_dgate")
    dgates = dzf.reshape(S, 2, DH).transpose(1, 0, 2)
    dhm = jnp.concatenate([b["dq"].reshape(N_HEADS, S, DH), b["dk"], b["dv"], dgates], axis=0)
    dproj = _from_heads(dhm)
    dh = _mm(dproj, w["w_in"], tb=True, tm=512, tn=512, ctx=w["ctx"], name=tag + "_dh")
    dw_in = _mm(h, dproj, ta=True, tm=512, tn=640, ctx=w["ctx"], name=tag + "_dwin_w")
    dx, dg = _rms_bwd(dh, x, g, dxo, name=tag + "_drms")
    w["P"].grad("fox_in", w["j"], _split_cols(dw_in, FOX_IN))
    return dx, dg, dict(b_f=db[0, :N_HEADS])


GROUPS = {
    "gu": (("ffn1_w_gu", "ffn2_w_gu"), 2),
    "down": (("ffn1_w_down", "ffn2_w_down"), 1),
    "out": (("nsa_w_out", "swa_w_out", "fox_w_out"), 1),
    "cw1": (("nsa_ck_w1", "nsa_cv_w1"), 1),
    "nsa_in": (("nsa_w_in",), 2),
    "swa_in": (("swa_w_in",), 2),
    "fox_in": (("fox_w_in",), 2),
}
OUT_SLAB = {0: 0, 3: 1, 1: 2, 2: 3}


def _pieces_in_order():
    chunks = []
    for i in range(DEPTH):
        kind, j = i % 3, i // 3
        chunks.append([("gu", i), ("down", i)])
        if kind == 0:
            chunks.append([("nsa_in", j), ("cw1", j), ("cw1", 2 + j), ("out", OUT_SLAB[i])])
        else:
            chunks.append([("swa_in" if kind == 1 else "fox_in", j), ("out", OUT_SLAB[i])])
        chunks.append([("gu", DEPTH + i), ("down", DEPTH + i)])
    return chunks


def _consumer_layout(group, F):
    _, rows, C = F.shape
    if group == "gu":
        return F
    if GROUPS[group][1] == 1:
        return F.reshape(4 * rows, C)
    w = F.transpose(1, 0, 2).reshape(rows, 4 * C)
    pad = {"nsa_in": NSA_IN_PAD, "swa_in": SWA_IN, "fox_in": FOX_IN_PAD}[group] - 4 * C
    return jnp.pad(w, ((0, 0), (0, pad)))


class _Given:
    def __init__(self, pieces, small):
        self.pieces, self.small, self.ctx, self.grads = pieces, small, None, {}

    def weight(self, group, l):
        return _consumer_layout(group, self.pieces[group, l])

    def grad(self, group, l, G):
        self.grads[group, l] = G


class _MixerWeights(dict):
    def __init__(self, P, pieces, **given):
        super().__init__(P=P, ctx=P.ctx, **given)
        self.pieces = pieces

    def __missing__(self, key):
        self[key] = self["P"].weight(*self.pieces[key])
        return self[key]


def _mixer_weights(P, i):
    kind, j = i % 3, i // 3
    out = {"out": ("out", OUT_SLAB[i])}
    if kind == 0:
        small = {k: P.small["nsa_" + k][j] for k in ("ck_pe", "ck_w2", "cv_pe", "cv_w2")}
        return _MixerWeights(P, dict(out, w_in=("nsa_in", j), ck_w1=("cw1", j), cv_w1=("cw1", 2 + j)), j=j,
                             out_l=OUT_SLAB[i], **small)
    if kind == 1:
        return _MixerWeights(P, dict(out, w_in=("swa_in", j)), j=j, out_l=OUT_SLAB[i], sinks=P.small["swa_sinks"][j])
    return _MixerWeights(P, dict(out, w_in=("fox_in", j)), j=j, out_l=OUT_SLAB[i], b_f=P.small["fox_b_f"][j])


def _local_step(x, tgt, P):
    S = x.shape[0]
    cc, ss = _rope_tables(S)
    sm = P.small
    saved = []
    for i in range(DEPTH):
        kind = i % 3
        x, s1 = _ffn_fwd(x, sm["ffn1_norm"][i], P, i, f"l{i}f1")
        mw = _mixer_weights(P, i)
        if kind == 0:
            x, s2 = _nsa_fwd(x, sm["mix_norm"][i], mw, cc, ss, f"l{i}nsa")
        elif kind == 1:
            x, s2 = _swa_fwd(x, sm["mix_norm"][i], mw, cc, ss, f"l{i}swa")
        else:
            x, s2 = _fox_fwd(x, sm["mix_norm"][i], mw, f"l{i}fox")
        x, s3 = _ffn_fwd(x, sm["ffn2_norm"][i], P, DEPTH + i, f"l{i}f2")
        saved.append((s1, mw, s2, s3))
    loss, dx, d_final = _loss_head(x, sm["final_norm"], tgt, name="loss_head")

    norms = {k: [None] * DEPTH for k in ("ffn1_norm", "mix_norm", "ffn2_norm")}
    mix = {}
    for i in reversed(range(DEPTH)):
        kind, j = i % 3, i // 3
        s1, mw, s2, s3 = saved[i]
        dx, norms["ffn2_norm"][i] = _ffn_bwd(dx, s3, sm["ffn2_norm"][i], P, DEPTH + i, f"l{i}f2")
        if kind == 0:
            dx, dg, gm = _nsa_bwd(dx, s2, sm["mix_norm"][i], mw, cc, ss, f"l{i}nsa")
            pre = "nsa_"
        elif kind == 1:
            dx, dg, gm = _swa_bwd(dx, s2, sm["mix_norm"][i], mw, cc, ss, f"l{i}swa")
            pre = "swa_"
        else:
            dx, dg, gm = _fox_bwd(dx, s2, sm["mix_norm"][i], mw, f"l{i}fox")
            pre = "fox_"
        norms["mix_norm"][i] = dg
        for k, val in gm.items():
            mix.setdefault(pre + k, {})[j] = val
        dx, norms["ffn1_norm"][i] = _ffn_bwd(dx, s1, sm["ffn1_norm"][i], P, i, f"l{i}f1")
    small = {k: jnp.stack(v, axis=0) for k, v in norms.items()}
    small.update({k: jnp.stack([d[j] for j in sorted(d)], axis=0) for k, d in mix.items()})
    small["final_norm"] = d_final
    return loss, dx, small


def _sum_slots(own, recv, me, into, row0, *, name):
    _, R, C = recv.shape
    tr = _row_tile(math.gcd(R, row0), C)
    blk0 = row0 // tr

    def body(_, g_ref, r_ref, __, o_ref):
        acc = g_ref[...].astype(F32) + r_ref[0].astype(F32)
        acc = acc + r_ref[1].astype(F32)
        o_ref[...] = acc + r_ref[2].astype(F32)

    spec = pltpu.PrefetchScalarGridSpec(
        num_scalar_prefetch=1, grid=(R // tr,),
        in_specs=[pl.BlockSpec((None, tr, C), lambda i, me_ref: (me_ref[0], i, 0)),
                  pl.BlockSpec((3, tr, C), lambda i, me_ref: (0, i, 0)), _ANY],
        out_specs=pl.BlockSpec((tr, C), lambda i, me_ref: (blk0 + i, 0)))
    return pl.pallas_call(body, grid_spec=spec, out_shape=jax.ShapeDtypeStruct(into.shape, F32),
                          input_output_aliases={3: 0}, compiler_params=_params(), name=name)(me, own, recv, into)


def _swap_sibling(parts):
    n = len(parts)

    def body(*refs):
        srcs, outs = refs[:n], refs[n:2 * n]
        send_sems, recv_sems = refs[2 * n:]
        x, y, c = lax.axis_index("x"), lax.axis_index("y"), lax.axis_index("c")
        cps = [pltpu.make_async_remote_copy(src_ref=srcs[g], dst_ref=outs[g], send_sem=send_sems.at[g],
                                            recv_sem=recv_sems.at[g], device_id=(x, y, 1 - c), device_id_type=MESH)
               for g in range(n)]
        for cp in cps:
            cp.start()
        for cp in cps:
            cp.wait()

    return pl.pallas_call(
        body, out_shape=tuple(jax.ShapeDtypeStruct(p.shape, p.dtype) for p in parts),
        in_specs=[_ANY] * n, out_specs=(_ANY,) * n,
        scratch_shapes=[pltpu.SemaphoreType.DMA((n,)), pltpu.SemaphoreType.DMA((n,))],
        compiler_params=pltpu.CompilerParams(has_side_effects=True), name="swap_core_grads")(*parts)


def _flip(coord, bit):
    return 1 - coord if bit else coord


def _allreduce_small(v):
    n, C = v.shape

    def body(v_ref, o_ref, buf, send_sems, recv_sems):
        x, y, c = lax.axis_index("x"), lax.axis_index("y"), lax.axis_index("c")
        me = 4 * x + 2 * y + c
        buf[me] = v_ref[...]
        peers = [(_flip(x, (j >> 2) & 1), _flip(y, (j >> 1) & 1), _flip(c, j & 1)) for j in range(1, 8)]
        sends = [pltpu.make_async_remote_copy(src_ref=v_ref, dst_ref=buf.at[me], send_sem=send_sems.at[j],
                                              recv_sem=recv_sems.at[j], device_id=peer, device_id_type=MESH)
                 for j, peer in enumerate(peers)]
        for cp in sends:
            cp.start()
        for j, (px, py, pc) in enumerate(peers):
            pltpu.make_async_remote_copy(src_ref=v_ref, dst_ref=buf.at[4 * px + 2 * py + pc], send_sem=send_sems.at[j],
                                         recv_sem=recv_sems.at[j], device_id=(px, py, pc),
                                         device_id_type=MESH).wait_recv()
        for cp in sends:
            cp.wait_send()
        acc = buf[0]
        for d in range(1, 8):
            acc = acc + buf[d]
        o_ref[...] = acc

    return pl.pallas_call(
        body, out_shape=jax.ShapeDtypeStruct((n, C), F32),
        in_specs=[pl.BlockSpec(memory_space=pltpu.VMEM)], out_specs=pl.BlockSpec(memory_space=pltpu.VMEM),
        scratch_shapes=[pltpu.VMEM((8, n, C), F32), pltpu.SemaphoreType.DMA((7,)), pltpu.SemaphoreType.DMA((7,))],
        compiler_params=pltpu.CompilerParams(has_side_effects=True), name="allreduce_small")(v)


def _adamw(w, m, v, gs, *, row0=0, name):
    shape = w.shape
    w3, m3, v3 = (a.reshape((-1,) + a.shape[-2:]) for a in (w, m, v))
    g2 = [_rows2d(g) for g in gs]
    L, rows, C = w3.shape
    tr = _row_tile(math.gcd(rows, row0), C, budget=1024 * 1024)
    ng = len(g2)
    blk0, nb = row0 // tr, rows // tr

    def body(*refs):
        w_ref, m_ref, v_ref = refs[:3]
        g = refs[3][...]
        for r in refs[4:3 + ng]:
            g = g + r[...]
        g_ref, d_ref, nm_ref, nv_ref = refs[3 + ng:]
        mn = B1 * m_ref[...] + (1.0 - B1) * g
        vn = B2 * v_ref[...] + (1.0 - B2) * (g * g)
        m_hat = mn / (1.0 - B1 ** STEP)
        v_hat = vn / (1.0 - B2 ** STEP)
        g_ref[...] = g
        d_ref[...] = -LR * (m_hat / (jnp.sqrt(v_hat) + EPS) + WD * w_ref[...])
        nm_ref[...] = mn
        nv_ref[...] = vn

    spec = pl.BlockSpec((None, tr, C), lambda l, i: (l, i, 0))
    gspec = pl.BlockSpec((tr, C), lambda l, i: (blk0 + l * nb + i, 0))
    outs = pl.pallas_call(body, out_shape=tuple(jax.ShapeDtypeStruct((L, rows, C), F32) for _ in range(4)),
                          grid=(L, nb), in_specs=[spec] * 3 + [gspec] * ng, out_specs=(spec,) * 4,
                          compiler_params=_params(), name=name)(w3, m3, v3, *g2)
    return tuple(o.reshape(shape) for o in outs)


def _small_layout(shapes):
    offs, off = {}, 0
    for k in REPLICATED:
        n = int(np.prod(shapes[k]))
        offs[k] = (off, n)
        off += -(-n // LANES) * LANES
    return offs, off


def _pack_small(d, shapes):
    offs, total = _small_layout(shapes)
    parts = []
    for k in REPLICATED:
        n = offs[k][1]
        parts.append(jnp.pad(d[k].reshape(-1).astype(F32), (0, -(-n // LANES) * LANES - n)))
    rows = -(-(total // LANES) // 8) * 8
    return jnp.pad(jnp.concatenate(parts), (0, rows * LANES - total)).reshape(rows, LANES)


def _unpack_small(a, shapes):
    offs, _ = _small_layout(shapes)
    flat = a.reshape(-1)
    return {k: flat[offs[k][0]:offs[k][0] + offs[k][1]].reshape(shapes[k]) for k in REPLICATED}


def _group_shards(w):
    shards = []
    for members, _ in GROUPS.values():
        s = jnp.concatenate([w[k].astype(_WIRE) for k in members], axis=0)
        shards.append(s.reshape(s.shape[0], 2, s.shape[1] // 2, s.shape[2]))
    return shards


class _Exchanged:
    def __init__(self, w):
        self.small = {k: w[k] for k in REPLICATED}
        self.ctx = _Sched()
        shards = dict(zip(GROUPS, _group_shards(w)))
        self.gather, self.cache, self.scatter = {}, {}, {}
        chunks = _pieces_in_order()
        self.me = jnp.reshape(2 * lax.axis_index("x") + lax.axis_index("y"), (1,)).astype(jnp.int32)
        first = {(g, l): _gather_task1(shards[g], l, _place_own(shards[g], l, self.me, name=f"own_{g}{l}"))
                 for chunk in chunks for g, l in chunk}
        self.gather = {p: _gather_task2(task) for p, task in first.items()}
        pieces = [p for chunk in chunks for p in chunk]
        for n, p in enumerate(pieces):
            self.ctx.push(first[p])
            if n:
                self.ctx.push(self.gather[pieces[n - 1]])
        self.ctx.push(self.gather[pieces[-1]])

    def weight(self, group, l):
        if (group, l) not in self.cache:
            task = self.gather[group, l]
            self.ctx.finish(task)
            F = task.result()
            self.cache[group, l] = _consumer_layout(group, F.reshape(4, -1, F.shape[-1]))
        return self.cache[group, l]

    def grad(self, group, l, G):
        self.scatter[group, l] = (G, self.ctx.push(_scatter_task(G)))

    def partial_sums(self, w):
        self.ctx.finish()
        parts = []
        for group, (members, _) in GROUPS.items():
            rows, C = w[members[0]].shape[1:]
            n = sum(w[k].shape[0] for k in members)
            part = jnp.zeros((n * rows, C), F32)
            for l in range(n):
                G, task = self.scatter[group, l]
                part = _sum_slots(G, task.result(), self.me, part, l * rows, name=f"sum_{group}{l}")
            parts.append(part)
        return parts


def _reduce_and_update(w, m, v, parts, small):
    others = _swap_sibling(parts)
    small_shapes = {k: w[k].shape for k in REPLICATED}
    g_small = _unpack_small(_allreduce_small(_pack_small(small, small_shapes)), small_shapes)

    out_g, out_d, out_m, out_v = {}, {}, {}, {}
    for (members, _), part, other in zip(GROUPS.values(), parts, others):
        row0 = 0
        for k in members:
            out_g[k], out_d[k], out_m[k], out_v[k] = _adamw(w[k], m[k], v[k], [part, other], row0=row0,
                                                            name="adamw_" + k)
            row0 += w[k].shape[0] * w[k].shape[1]
    sm = _adamw(_pack_small(w, small_shapes), _pack_small(m, small_shapes), _pack_small(v, small_shapes),
                [_pack_small(g_small, small_shapes)], name="adamw_small")
    for d, packed in zip((out_g, out_d, out_m, out_v), sm):
        d.update(_unpack_small(packed, small_shapes))
    return out_g, out_d, out_m, out_v


def kernel(x, ffn1_norm, ffn1_w_gu, ffn1_w_down, mix_norm, ffn2_norm, ffn2_w_gu, ffn2_w_down, nsa_w_in, nsa_ck_pe, nsa_ck_w1, nsa_ck_w2, nsa_cv_pe, nsa_cv_w1, nsa_cv_w2, nsa_w_out, swa_w_in, swa_sinks, swa_w_out, fox_w_in, fox_b_f, fox_w_out, final_norm, loss_target, m_ffn1_norm, m_ffn1_w_gu, m_ffn1_w_down, m_mix_norm, m_ffn2_norm, m_ffn2_w_gu, m_ffn2_w_down, m_nsa_w_in, m_nsa_ck_pe, m_nsa_ck_w1, m_nsa_ck_w2, m_nsa_cv_pe, m_nsa_cv_w1, m_nsa_cv_w2, m_nsa_w_out, m_swa_w_in, m_swa_sinks, m_swa_w_out, m_fox_w_in, m_fox_b_f, m_fox_w_out, m_final_norm, v_ffn1_norm, v_ffn1_w_gu, v_ffn1_w_down, v_mix_norm, v_ffn2_norm, v_ffn2_w_gu, v_ffn2_w_down, v_nsa_w_in, v_nsa_ck_pe, v_nsa_ck_w1, v_nsa_ck_w2, v_nsa_cv_pe, v_nsa_cv_w1, v_nsa_cv_w2, v_nsa_w_out, v_swa_w_in, v_swa_sinks, v_swa_w_out, v_fox_w_in, v_fox_b_f, v_fox_w_out, v_final_norm):
    args = dict(locals())
    w = {k: args[k] for k in WEIGHTS}
    m = {k: args["m_" + k] for k in WEIGHTS}
    v = {k: args["v_" + k] for k in WEIGHTS}
    P = _Exchanged(w)
    loss_part, dx, small = _local_step(x[0], loss_target[0], P)
    loss = lax.psum(loss_part, ("x", "y", "c"))
    out_g, out_d, out_m, out_v = _reduce_and_update(w, m, v, P.partial_sums(w), small)
    return (loss, dx[None], *[out_g[k] for k in WEIGHTS], *[out_d[k] for k in WEIGHTS],
            *[out_m[k] for k in WEIGHTS], *[out_v[k] for k in WEIGHTS])
```

```python
import math

import numpy as np
import jax
import jax.numpy as jnp
from jax import lax
from jax.experimental import pallas as pl
from jax.experimental.pallas import tpu as pltpu

F32 = jnp.float32
_CDT = jnp.bfloat16
_WIRE = jnp.bfloat16
_VMEM_LIMIT = 56 * 1024 * 1024

D_MODEL = 1024
DEPTH = 4
DH = 64
N_HEADS = 16
RMS_EPS = 1e-6
NEG = -1e30
SCALE = DH ** -0.5
BQ = 256
LANES = 128
NSA_G, NSA_R = 4, 4
NSA_WINDOW = 512
NSA_TOPK = 16
NSA_BONUS = 1e4
SWA_G, SWA_R = 2, 8
SWA_WINDOW = 128
NSA_ROTATED = tuple(hd < 16 or (hd < 40 and (hd - 16) % 8 < 4) for hd in range(42))
NSA_IN, NSA_IN_PAD = 2608, 2688
SWA_IN = 1280
SWA_ROTATED = (True,) * 18 + (False,) * 2
SWA_HEADS = (0, 16, 18)
FOX_IN, FOX_IN_PAD = 3088, 3200
FOX_HEADS = (0, 16, 32)
LR, B1, B2, EPS, WD, STEP = 0.001, 0.9, 0.999, 1e-08, 0.01, 10
MESH = pl.DeviceIdType.MESH

REPLICATED = ["ffn1_norm", "mix_norm", "ffn2_norm", "nsa_ck_pe", "nsa_ck_w2", "nsa_cv_pe", "nsa_cv_w2",
              "swa_sinks", "fox_b_f", "final_norm"]
WEIGHTS = ['ffn1_norm', 'ffn1_w_gu', 'ffn1_w_down', 'mix_norm', 'ffn2_norm', 'ffn2_w_gu', 'ffn2_w_down',
           'nsa_w_in', 'nsa_ck_pe', 'nsa_ck_w1', 'nsa_ck_w2', 'nsa_cv_pe', 'nsa_cv_w1', 'nsa_cv_w2', 'nsa_w_out',
           'swa_w_in', 'swa_sinks', 'swa_w_out', 'fox_w_in', 'fox_b_f', 'fox_w_out', 'final_norm']


def _params(**kw):
    return pltpu.CompilerParams(vmem_limit_bytes=_VMEM_LIMIT, **kw)


def _dot(a, b):
    return lax.dot_general(a, b, (((1,), (0,)), ((), ())), preferred_element_type=F32)


def _dot_nt(a, b):
    return lax.dot_general(a, b, (((1,), (1,)), ((), ())), preferred_element_type=F32)


def _dot_tn(a, b):
    return lax.dot_general(a, b, (((0,), (0,)), ((), ())), preferred_element_type=F32)


def _split3(x):
    hi = x.astype(jnp.bfloat16)
    r1 = x - hi.astype(F32)
    mid = r1.astype(jnp.bfloat16)
    lo = (r1 - mid.astype(F32)).astype(jnp.bfloat16)
    return hi, mid, lo


def _dot_exact(x, p):
    hi, mid, lo = _split3(x)
    return _dot(hi, p) + _dot(mid, p) + _dot(lo, p)


def _dot_exact_left(p, x):
    hi, mid, lo = _split3(x)
    return _dot(p, hi) + _dot(p, mid) + _dot(p, lo)


_ANY = pl.BlockSpec(memory_space=pl.ANY)


def _chip_peers():
    x, y, c = lax.axis_index("x"), lax.axis_index("y"), lax.axis_index("c")
    return x, y, c, [(1 - x, y), (x, 1 - y), (1 - x, 1 - y)]


class _Job:
    def __init__(self, ins, outs, nsem, copies, cost, alias):
        self.ins, self.outs, self.nsem, self.copies, self.cost, self.alias = ins, outs, nsem, copies, cost, alias
        self.results = None

    def inputs(self):
        return self.ins


class _Task:
    def __init__(self, src, out, parts, init=None):
        self.src, self.out, self.parts, self.init, self.done, self.jobs = src, out, parts, init, 0, []

    def result(self):
        return self.jobs[-1].results[0]

    def complete(self):
        return self.done == len(self.parts) and self.jobs[-1].results is not None

    def ready(self):
        if isinstance(self.src, _Task) and not self.src.complete():
            return False
        return not self.jobs or self.jobs[-1].results is not None

    def next_job(self, n):
        parts = self.parts[self.done:self.done + n]
        in_place = isinstance(self.src, _Task)
        ins = [] if in_place else [self.src]
        prev = self.jobs[-1].results[0] if self.jobs else (self.src.result() if in_place else self.init)
        alias = {}
        if prev is not None:
            alias = {len(ins): 0}
            ins = ins + [prev]

        def copies(in_refs, out_refs, sems):
            src = out_refs[0] if in_place else in_refs[0]
            out, first = [], 0
            for build, nsem, _ in parts:
                out += build(src, out_refs[0], sems, first)
                first += nsem
            return out

        job = _Job(ins, [self.out], sum(p[1] for p in parts), copies, sum(p[2] for p in parts), alias)
        self.done += n
        self.jobs.append(job)
        return job


def _pcall(body, *, out_shape, in_specs, out_specs, name, grid=(), scratch_shapes=(), aliases=None, jobs=(), ins):
    single = not isinstance(out_shape, (tuple, list))
    out_shape = (out_shape,) if single else tuple(out_shape)
    out_specs = (out_specs,) if single else tuple(out_specs)
    n_in, n_out, n_scr = len(ins), len(out_shape), len(scratch_shapes)
    side_ins = [a for j in jobs for a in j.inputs()]
    side_outs = [o for j in jobs for o in j.outs]
    aliases = dict(aliases or {})
    i0, o0 = n_in, n_out
    for j in jobs:
        for a, b in j.alias.items():
            aliases[i0 + a] = o0 + b
        i0 += len(j.ins)
        o0 += len(j.outs)

    def body2(*refs):
        in_refs, s_in = refs[:n_in], refs[n_in:n_in + len(side_ins)]
        r = n_in + len(side_ins)
        out_refs, s_out = refs[r:r + n_out], refs[r + n_out:r + n_out + len(side_outs)]
        r += n_out + len(side_outs)
        scr, sems = refs[r:r + n_scr], refs[r + n_scr:]
        def descriptors():
            out, a, b = [], 0, 0
            for j, sem in zip(jobs, sems):
                out += j.copies(s_in[a:a + len(j.ins)], s_out[b:b + len(j.outs)], sem)
                a += len(j.ins)
                b += len(j.outs)
            return out

        def at_step(steps, fn):
            cond = None
            for d, s in enumerate(steps):
                hit = pl.program_id(d) == s
                cond = hit if cond is None else jnp.logical_and(cond, hit)
            if cond is None:
                fn()
            else:
                pl.when(cond)(fn)

        def start_all():
            for cp in descriptors():
                cp.start()

        def wait_all():
            for cp in descriptors():
                cp.wait()

        if jobs:
            at_step([0] * len(grid), start_all)
        body(*in_refs, *out_refs, *scr)
        if jobs:
            at_step([n - 1 for n in grid], wait_all)

    res = pl.pallas_call(
        body2, out_shape=out_shape + tuple(side_outs), grid=grid, in_specs=list(in_specs) + [_ANY] * len(side_ins),
        out_specs=out_specs + (_ANY,) * len(side_outs),
        scratch_shapes=list(scratch_shapes) + [pltpu.SemaphoreType.DMA((j.nsem,)) for j in jobs],
        input_output_aliases=aliases,
        compiler_params=_params(has_side_effects=True) if jobs else _params(), name=name)(*ins, *side_ins)
    b = n_out
    for j in jobs:
        j.results = list(res[b:b + len(j.outs)])
        b += len(j.outs)
    return res[0] if single else tuple(res[:n_out])


def _comm_call(jobs, name):
    def body():
        pass

    _pcall(body, out_shape=(), in_specs=[], out_specs=(), name=name, jobs=jobs, ins=[])


def _place_own(shard, l, me, *, name):
    _, _, h, C = shard.shape

    def body(_, s_ref, o_ref):
        o_ref[...] = s_ref[...]

    spec = pltpu.PrefetchScalarGridSpec(
        num_scalar_prefetch=1, grid=(2,),
        in_specs=[pl.BlockSpec((None, None, h, C), lambda i, me_ref: (l, i, 0, 0))],
        out_specs=pl.BlockSpec((None, None, h, C), lambda i, me_ref: (me_ref[0], i, 0, 0)))
    return pl.pallas_call(body, grid_spec=spec, out_shape=jax.ShapeDtypeStruct((4, 2, h, C), shard.dtype),
                          compiler_params=_params(), name=name)(me, shard)


def _gather_task1(shard, l, own):
    _, _, h, C = shard.shape
    mb = 3 * h * C * jnp.dtype(_WIRE).itemsize / 1e6
    n = _row_splits(h, ICI_US_PER_MB * mb)
    hr = h // n

    def rows_to_peers(r):
        def build(src, dst, sems, s0):
            x, y, c, chips = _chip_peers()
            me, rows = 2 * x + y, pl.ds(r * hr, hr)
            return [pltpu.make_async_remote_copy(src_ref=src.at[l, c, rows], dst_ref=dst.at[me, c, rows],
                                                 send_sem=sems.at[s0 + 2 * j], recv_sem=sems.at[s0 + 2 * j + 1],
                                                 device_id=(px, py, c), device_id_type=MESH)
                    for j, (px, py) in enumerate(chips)]
        return build

    parts = [(rows_to_peers(r), 6, ICI_US_PER_MB * mb / n) for r in range(n)]
    return _Task(shard, jax.ShapeDtypeStruct((4, 2, h, C), shard.dtype), parts, init=own)


def _gather_task2(task1):
    _, _, h, C = task1.out.shape
    mb = h * C * jnp.dtype(_WIRE).itemsize / 1e6

    def forward(j):
        def build(_, dst, sems, s0):
            x, y, c, chips = _chip_peers()
            px, py = chips[j]
            landed = dst.at[2 * px + py, c]
            return [pltpu.make_async_remote_copy(src_ref=landed, dst_ref=landed, send_sem=sems.at[s0],
                                                 recv_sem=sems.at[s0 + 1], device_id=(x, y, 1 - c),
                                                 device_id_type=MESH)]
        return build

    return _Task(task1, task1.out, [(forward(j), 2, 1.0 + D2D_US_PER_MB * mb) for j in range(3)])


def _scatter_task(G):
    mb = 3 * G.shape[1] * G.shape[2] * jnp.dtype(_WIRE).itemsize / 1e6
    n = _row_splits(G.shape[1], ICI_US_PER_MB * mb)
    rr = G.shape[1] // n

    def rows_to_peers(r):
        def build(src, dst, sems, s0):
            _, _, c, chips = _chip_peers()
            rows = pl.ds(r * rr, rr)
            return [pltpu.make_async_remote_copy(src_ref=src.at[2 * px + py, rows], dst_ref=dst.at[j, rows],
                                                 send_sem=sems.at[s0 + 2 * j], recv_sem=sems.at[s0 + 2 * j + 1],
                                                 device_id=(px, py, c), device_id_type=MESH)
                    for j, (px, py) in enumerate(chips)]
        return build

    parts = [(rows_to_peers(r), 6, ICI_US_PER_MB * mb / n) for r in range(n)]
    return _Task(G, jax.ShapeDtypeStruct((3,) + G.shape[1:], G.dtype), parts)


def _row_splits(rows, cost):
    n = 8
    while n > 1 and (rows % (16 * n) or cost / n < PART_US):
        n //= 2
    return n


CARRIER_US = {
    "f_gu": 48, "f_down": 20, "f_dact": 42, "f_dwd": 20, "f_dwgu": 42, "f_dh": 46,
    "nsa_in": 25, "nsa_cmp": 86, "nsa_slc": 122, "nsa_win": 74, "nsa_dcmp": 58, "nsa_dslc": 195, "nsa_dwin": 108,
    "nsa_dh": 26, "nsa_dwin_w": 25,
    "swa_in": 16, "swa_attn": 74, "swa_dattn": 74, "swa_dh": 17, "swa_dwin_w": 18,
    "fox_in": 34, "fox_attn": 104, "fox_dattn": 161, "fox_dh": 30, "fox_dwin_w": 30,
}
ICI_US_PER_MB = 15.0
PART_US = 12.0
D2D_US_PER_MB = 2.3
LOCAL_US_PER_MB = 1.5


class _Sched:
    def __init__(self):
        self.queue, self.credit, self.n_flush = [], 0.0, 0

    def push(self, task):
        self.queue.append(task)
        return task

    def _jobs(self, fits):
        jobs = []
        while self.queue and self.queue[0].ready():
            task, n = self.queue[0], 0
            while task.done + n < len(task.parts) and fits(task.parts[task.done + n][2]):
                n += 1
            if n:
                jobs.append(task.next_job(n))
            if task.done < len(task.parts):
                break
            self.queue.pop(0)
        return jobs

    def take(self, name):
        kind = name[2:].split("_", 1)
        key = ("f" if kind[0] in ("f1", "f2") else kind[0]) + "_" + kind[1]
        self.credit = min(self.credit, 0.0) + CARRIER_US.get(key, 0.0)

        def fits(cost):
            if self.credit < 0.5 * cost:
                return False
            self.credit -= cost
            return True

        return self._jobs(fits)

    def finish(self, task=None):
        while self.queue and (task is None or not task.complete()):
            jobs = []
            while self.queue and self.queue[0].ready():
                head = self.queue.pop(0)
                jobs.append(head.next_job(len(head.parts) - head.done))
                if head is task:
                    break
            assert jobs, "the task at the head of the queue waits for one that was never queued"
            _comm_call(jobs, name=f"exchange_{self.n_flush}")
            self.n_flush += 1


def _mm(a, b, *, ta=False, tb=False, tm, tn, b_lead=(), into=None, o_lead=(), out_dtype=F32, res=None, alpha=1.0,
        ctx=None, name):
    M = a.shape[1] if ta else a.shape[0]
    K = a.shape[0] if ta else a.shape[1]
    bk, bn = (b.shape[-1], b.shape[-2]) if tb else (b.shape[-2], b.shape[-1])
    j_lead, k_lead = "j" in b_lead, "k" in b_lead
    N = bn * (b.shape[b_lead.index("j")] if j_lead else 1)
    nk = b.shape[b_lead.index("k")] if k_lead else 1
    tk = K // nk
    assert tk == bk, (name, K, nk, bk)
    tm = min(tm, M)
    tn = bn if j_lead else min(tn, N)
    assert M % tm == 0 and N % tn == 0, (name, M, N, tm, tn)
    nb, no = len(b_lead), len(o_lead)

    def pick(lead, j, k):
        return tuple(j if t == "j" else k if t == "k" else t for t in lead)

    a_spec = pl.BlockSpec((tk, tm), lambda i, j, k: (k, i)) if ta else pl.BlockSpec((tm, tk), lambda i, j, k: (i, k))
    if tb:
        b_spec = pl.BlockSpec((None,) * nb + (tn, tk),
                              lambda i, j, k: pick(b_lead, j, k) + (0 if j_lead else j, 0 if k_lead else k))
    else:
        b_spec = pl.BlockSpec((None,) * nb + (tk, tn),
                              lambda i, j, k: pick(b_lead, j, k) + (0 if k_lead else k, 0 if j_lead else j))
    r_spec = pl.BlockSpec((tm, tn), lambda i, j, k: (i, j))
    if o_lead == ("heads",):
        o_spec = pl.BlockSpec((tn // DH, tm, DH), lambda i, j, k: (j, i, 0))
    else:
        o_spec = pl.BlockSpec((None,) * no + (tm, tn),
                              lambda i, j, k: pick(o_lead, j, k) + (i, 0 if "j" in o_lead else j))
    dn = (((0 if ta else 1,), (1 if tb else 0,)), ((), ()))
    has_res, has_into = res is not None, into is not None
    if has_into:
        out_dtype = into.dtype

    def body(*refs):
        a_ref, b_ref = refs[0], refs[1]
        r_ref = refs[2] if has_res else None
        o_ref = refs[2 + has_res + has_into]
        prod = lax.dot_general(a_ref[...].astype(_CDT), b_ref[...].astype(_CDT), dn, preferred_element_type=F32)

        def finish(acc):
            if alpha != 1.0:
                acc = acc * alpha
            if has_res:
                acc = r_ref[...] + acc
            if o_lead == ("heads",):
                for hd in range(tn // DH):
                    o_ref[hd] = acc[:, hd * DH:(hd + 1) * DH].astype(out_dtype)
            else:
                o_ref[...] = acc.astype(out_dtype)

        if nk == 1:
            finish(prod)
        else:
            acc_ref = refs[-1]
            k = pl.program_id(2)

            @pl.when(k == 0)
            def _():
                acc_ref[...] = prod

            @pl.when(k != 0)
            def _():
                acc_ref[...] += prod

            @pl.when(k == nk - 1)
            def _():
                finish(acc_ref[...])

    ins, specs = [a, b], [a_spec, b_spec]
    if has_res:
        ins.append(res)
        specs.append(r_spec)
    aliases = {}
    if has_into:
        aliases = {len(ins): 0}
        ins.append(into)
        specs.append(_ANY)
        out_shape = jax.ShapeDtypeStruct(into.shape, into.dtype)
    elif o_lead == ("j",):
        out_shape = jax.ShapeDtypeStruct((N // tn, M, tn), out_dtype)
    elif o_lead == ("heads",):
        out_shape = jax.ShapeDtypeStruct((N // DH, M, DH), out_dtype)
    else:
        assert not o_lead
        out_shape = jax.ShapeDtypeStruct((M, N), out_dtype)
    scratch = [pltpu.VMEM((tm, tn), F32)] if nk > 1 else []
    return _pcall(body, out_shape=out_shape, grid=(M // tm, N // tn, nk), in_specs=specs, out_specs=o_spec,
                  scratch_shapes=scratch, aliases=aliases, name=name, jobs=ctx.take(name) if ctx else (), ins=ins)


def _rows2d(a):
    return a.reshape(-1, a.shape[-1])


def _row_tile(rows, cols, itemsize=4, budget=2 * 1024 * 1024):
    t = rows
    while t % 2 == 0 and t * cols * itemsize > budget and (t // 2) % 8 == 0:
        t //= 2
    return t


def _addn(*xs, name):
    shape = xs[0].shape
    x2 = [_rows2d(x) for x in xs]
    R, C = x2[0].shape
    tr = _row_tile(R, C)

    def body(*refs):
        acc = refs[0][...]
        for r in refs[1:-1]:
            acc = acc + r[...]
        refs[-1][...] = acc

    spec = pl.BlockSpec((tr, C), lambda i: (i, 0))
    out = pl.pallas_call(body, out_shape=jax.ShapeDtypeStruct((R, C), F32), grid=(R // tr,),
                         in_specs=[spec] * len(x2), out_specs=spec, compiler_params=_params(), name=name)(*x2)
    return out.reshape(shape)


def _rms_fwd(x, g, *, name):
    S, D = x.shape
    tr = 256

    def body(x_ref, g_ref, h_ref):
        xv = x_ref[...]
        rstd = lax.rsqrt(jnp.mean(xv * xv, axis=-1, keepdims=True) + RMS_EPS)
        h_ref[...] = (xv * rstd * g_ref[...]).astype(_CDT)

    return pl.pallas_call(body, out_shape=jax.ShapeDtypeStruct((S, D), _CDT), grid=(S // tr,),
                          in_specs=[pl.BlockSpec((tr, D), lambda i: (i, 0)), pl.BlockSpec((1, D), lambda i: (0, 0))],
                          out_specs=pl.BlockSpec((tr, D), lambda i: (i, 0)), compiler_params=_params(),
                          name=name)(x, g.reshape(1, D))


def _rms_bwd(dh, x, g, dres, *, name):
    S, D = x.shape
    tr = 256

    def body(dh_ref, x_ref, g_ref, dres_ref, dx_ref, dg_ref):
        xv = x_ref[...]
        rstd = lax.rsqrt(jnp.mean(xv * xv, axis=-1, keepdims=True) + RMS_EPS)
        xhat = xv * rstd
        dhv = dh_ref[...]
        dxhat = dhv * g_ref[...]
        dx_ref[...] = dres_ref[...] + rstd * (dxhat - xhat * jnp.mean(dxhat * xhat, axis=-1, keepdims=True))

        @pl.when(pl.program_id(0) == 0)
        def _():
            dg_ref[...] = jnp.zeros_like(dg_ref)

        dg_ref[...] += jnp.sum(dhv * xhat, axis=0, keepdims=True)

    row = pl.BlockSpec((tr, D), lambda i: (i, 0))
    vec = pl.BlockSpec((1, D), lambda i: (0, 0))
    dx, dg = pl.pallas_call(body, out_shape=(jax.ShapeDtypeStruct((S, D), F32), jax.ShapeDtypeStruct((1, D), F32)),
                            grid=(S // tr,), in_specs=[row, row, vec, row], out_specs=(row, vec),
                            compiler_params=_params(), name=name)(dh, x, g.reshape(1, D), dres)
    return dx, dg.reshape(D)


def _loss_head(x, g, tgt, *, name):
    S, D = x.shape
    tr = 256

    def body(x_ref, g_ref, t_ref, loss_ref, dx_ref, dg_ref):
        xv = x_ref[...]
        rstd = lax.rsqrt(jnp.mean(xv * xv, axis=-1, keepdims=True) + RMS_EPS)
        xhat = xv * rstd
        err = xhat * g_ref[...] - t_ref[...]
        part = 0.5 * jnp.sum(jnp.mean(err * err, axis=-1, keepdims=True), axis=0, keepdims=True)
        dy = err * (1.0 / D)
        dxhat = dy * g_ref[...]
        dx_ref[...] = rstd * (dxhat - xhat * jnp.mean(dxhat * xhat, axis=-1, keepdims=True))

        @pl.when(pl.program_id(0) == 0)
        def _():
            dg_ref[...] = jnp.zeros_like(dg_ref)
            loss_ref[...] = jnp.zeros_like(loss_ref)

        dg_ref[...] += jnp.sum(dy * xhat, axis=0, keepdims=True)
        loss_ref[...] += jnp.broadcast_to(part, loss_ref.shape)

    row = pl.BlockSpec((tr, D), lambda i: (i, 0))
    vec = pl.BlockSpec((1, D), lambda i: (0, 0))
    loss, dx, dg = pl.pallas_call(
        body, out_shape=(jax.ShapeDtypeStruct((8, LANES), F32), jax.ShapeDtypeStruct((S, D), F32),
                         jax.ShapeDtypeStruct((1, D), F32)),
        grid=(S // tr,), in_specs=[row, vec, row], out_specs=(pl.BlockSpec((8, LANES), lambda i: (0, 0)), row, vec),
        compiler_params=_params(), name=name)(x, g.reshape(1, D), tgt)
    return loss[0, 0], dx, dg.reshape(D)


def _rope_tables(S):
    inv = 10000.0 ** (-jnp.arange(0, DH, 2, dtype=F32) / DH)
    ang = jnp.arange(S, dtype=F32)[:, None] * inv[None, :]
    cos, sin = jnp.cos(ang), jnp.sin(ang)
    return jnp.concatenate([cos, cos], -1), jnp.concatenate([-sin, sin], -1)


def _swap_matrix():
    p = np.zeros((DH, DH), np.float32)
    for j in range(DH // 2):
        p[j + DH // 2, j] = 1.0
        p[j, j + DH // 2] = 1.0
    return jnp.asarray(p, jnp.bfloat16)


def _rope(x, cc, ss, rotated, *, name):
    n, S, _ = x.shape
    assert len(rotated) == n
    starts = [i for i in range(n) if rotated[i] and (i == 0 or not rotated[i - 1])]
    tab = pl.BlockSpec((S, DH), lambda i: (0, 0))
    for k, first in enumerate(starts):
        count = next((i for i in range(first, n) if not rotated[i]), n) - first

        def body(x_ref, c_ref, s_ref, p_ref, o_ref):
            xv = x_ref[0]
            o_ref[0] = xv * c_ref[...] + _dot_exact(xv, p_ref[...]) * s_ref[...]

        blk = pl.BlockSpec((1, S, DH), lambda i, first=first: (first + i, 0, 0))
        x = pl.pallas_call(body, out_shape=jax.ShapeDtypeStruct(x.shape, F32), grid=(count,),
                           in_specs=[blk, tab, tab, pl.BlockSpec((DH, DH), lambda i: (0, 0))], out_specs=blk,
                           input_output_aliases={0: 0}, compiler_params=_params(),
                           name=f"{name}{k}")(x, cc, ss, _swap_matrix())
    return x


def _key_range(kind, i, window, Sk):
    if kind == "cmp":
        return 0, Sk
    hi = (i + 1) * BQ
    if kind == "band":
        return max(0, i * BQ - window), hi
    return 0, hi


def _attn_mask(kind, i, lo, hi, window):
    shape = (BQ, hi - lo)
    qpos = i * BQ + lax.broadcasted_iota(jnp.int32, shape, 0)
    kpos = lo + lax.broadcasted_iota(jnp.int32, shape, 1)
    if kind == "cmp":
        return kpos * 16 + 31 <= qpos
    mask = kpos <= qpos
    if kind == "band":
        mask = mask & (qpos - kpos < window)
    return mask


def _sel_expand(n_slc, n_keys):
    shape = (n_slc, n_keys)
    j = lax.broadcasted_iota(jnp.int32, shape, 0)
    key = lax.broadcasted_iota(jnp.int32, shape, 1)
    return (jnp.right_shift(key, 6) == j).astype(_CDT)


def _eye():
    return lax.broadcasted_iota(jnp.int32, (BQ, BQ), 0) == lax.broadcasted_iota(jnp.int32, (BQ, BQ), 1)


def _to_col(row):
    return jnp.sum(jnp.where(_eye(), row, 0.0), axis=1, keepdims=True)


def _to_row(col):
    return jnp.sum(jnp.where(_eye(), col, 0.0), axis=0, keepdims=True)


def _scores(kind, i, lo, hi, window, qb, kb, crow_ref, sel_ref):
    s = _dot_nt(qb, kb) * SCALE
    if crow_ref is not None:
        s = s + _to_col(crow_ref[0, :, i * BQ:(i + 1) * BQ]) - crow_ref[0, :, lo:hi]
    mask = _attn_mask(kind, i, lo, hi, window)
    if sel_ref is not None:
        mask = mask & (sel_ref[0, i * BQ:(i + 1) * BQ, lo:hi].astype(F32) > 0.5)
    return jnp.where(mask, s, NEG), mask


def _attn_fwd(q, k, v, *, G, R, heads=(0, 0, 0), kind, window=0, bias=None, sinks=None, selT=None, ovT=None, ctx=None,
              name):
    S = q.shape[1]
    Sk = k.shape[1]
    q0, k0, v0 = heads
    nq = S // BQ
    n_slc = S // 64
    has_bias, has_sink, has_sel, is_cmp = bias is not None, sinks is not None, selT is not None, kind == "cmp"

    def body(*refs):
        it = iter(refs)
        q_ref, k_ref, v_ref = next(it), next(it), next(it)
        crow_ref = next(it) if has_bias else None
        sink_ref = next(it) if has_sink else None
        sel_ref = next(it) if has_sel else None
        ov_ref = next(it) if is_cmp else None
        o_ref, lse_ref = next(it), next(it)
        selo_ref, imp_ref = (next(it), next(it)) if is_cmp else (None, None)
        r = pl.program_id(1)
        for i in range(nq):
            lo, hi = _key_range(kind, i, window, Sk)
            rows = slice(i * BQ, (i + 1) * BQ)
            qb = q_ref[rows, :].astype(_CDT)
            kb = k_ref[0, lo:hi, :].astype(_CDT)
            vb = v_ref[0, lo:hi, :].astype(_CDT)
            s, mask = _scores(kind, i, lo, hi, window, qb, kb, crow_ref, sel_ref)
            m = jnp.max(s, axis=-1, keepdims=True)
            if has_sink:
                sk = sink_ref[0, 0, 0:1, 0:1]
                m = jnp.maximum(m, sk)
            e = jnp.exp(s - m)
            if is_cmp:
                e = jnp.where(mask, e, 0.0)
            l = jnp.sum(e, axis=-1, keepdims=True)
            if has_sink:
                l = l + jnp.exp(sk - m)
            if is_cmp:
                l = jnp.where(l > 0.0, l, 1.0)
            p = e * (1.0 / l)
            o_ref[0, 0, rows, :] = _dot(p.astype(_CDT), vb)
            lse_ref[0, 0, :, rows] = _to_row(m + jnp.log(l))
            if is_cmp:
                part = _dot_nt(ov_ref[...].astype(_CDT), p.astype(_CDT))

                @pl.when(r == 0)
                def _():
                    imp_ref[:, rows] = part

                @pl.when(r != 0)
                def _():
                    imp_ref[:, rows] += part

        if is_cmp:
            @pl.when(r == R - 1)
            def _():
                shape = (n_slc, S)
                j = lax.broadcasted_iota(jnp.int32, shape, 0)
                tb = jnp.right_shift(lax.broadcasted_iota(jnp.int32, shape, 1), 6)
                forced = (j == 0) | (j == tb) | (j == tb - 1)
                imp = jnp.where(j > tb, NEG, jnp.where(forced, NSA_BONUS, imp_ref[0:n_slc, :]))
                imp_ref[0:n_slc, :] = imp
                cnt = jnp.zeros(shape, F32)
                for jp in range(n_slc):
                    row = imp_ref[jp:jp + 1, :]
                    ahead = (row > imp) | ((row == imp) & (jp < j))
                    cnt = cnt + ahead.astype(F32)
                imp_ref[0:n_slc, :] = (cnt < float(min(NSA_TOPK, n_slc))).astype(F32)
                expand = _sel_expand(n_slc, S)
                for i in range(nq):
                    rows = slice(i * BQ, (i + 1) * BQ)
                    chosen = _dot_tn(imp_ref[0:n_slc, rows].astype(_CDT), expand)
                    selo_ref[0, rows, :] = chosen.astype(jnp.bfloat16)

    qspec = pl.BlockSpec((1, 1, S, DH), lambda g, r: (g, r, 0, 0))
    ins = [q, k, v]
    specs = [pl.BlockSpec((None, S, DH), lambda g, r: (q0 + g * R + r, 0, 0)),
             pl.BlockSpec((1, Sk, DH), lambda g, r: (k0 + g, 0, 0)), pl.BlockSpec((1, Sk, DH), lambda g, r: (v0 + g, 0, 0))]
    if has_bias:
        ins.append(bias)
        specs.append(pl.BlockSpec((1, 1, S), lambda g, r: (g, 0, 0)))
    if has_sink:
        ins.append(sinks)
        specs.append(pl.BlockSpec((1, 1, 8, LANES), lambda g, r: (g, r, 0, 0)))
    if has_sel:
        ins.append(selT)
        specs.append(pl.BlockSpec((1, S, S), lambda g, r: (g, 0, 0)))
    if is_cmp:
        ins.append(ovT)
        specs.append(pl.BlockSpec((LANES, Sk), lambda g, r: (0, 0)))
    outs = [jax.ShapeDtypeStruct((G, R, S, DH), F32), jax.ShapeDtypeStruct((G, R, 1, S), F32)]
    ospecs = [qspec, pl.BlockSpec((1, 1, 1, S), lambda g, r: (g, r, 0, 0))]
    scratch = []
    if is_cmp:
        outs.append(jax.ShapeDtypeStruct((G, S, S), jnp.bfloat16))
        ospecs.append(pl.BlockSpec((1, S, S), lambda g, r: (g, 0, 0)))
        scratch.append(pltpu.VMEM((LANES, S), F32))
    return _pcall(body, out_shape=tuple(outs), grid=(G, R), in_specs=specs, out_specs=tuple(ospecs),
                  scratch_shapes=scratch, name=name, jobs=ctx.take(name) if ctx else (), ins=ins)


def _attn_bwd(q, k, v, o, do, lse, *, heads=(0, 0, 0), kind, window=0, bias=None, sinks=None, selT=None, gate=None,
              ctx=None, name):
    G, R, S, _ = o.shape
    Sk = k.shape[1]
    q0, k0, v0 = heads
    nq = S // BQ
    has_bias, has_sink, has_sel, has_gate = bias is not None, sinks is not None, selT is not None, gate is not None

    def body(*refs):
        it = iter(refs)
        q_ref, k_ref, v_ref, o_ref, do_ref, lse_ref = (next(it) for _ in range(6))
        crow_ref = next(it) if has_bias else None
        sink_ref = next(it) if has_sink else None
        sel_ref = next(it) if has_sel else None
        z_ref = next(it) if has_gate else None
        dq_ref, dk_ref, dv_ref = next(it), next(it), next(it)
        dc_ref = next(it) if has_bias else None
        dsink_ref = next(it) if has_sink else None
        dz_ref = next(it) if has_gate else None
        r = pl.program_id(1)

        @pl.when(r == 0)
        def _():
            dk_ref[...] = jnp.zeros_like(dk_ref)
            dv_ref[...] = jnp.zeros_like(dv_ref)
            if has_bias:
                dc_ref[...] = jnp.zeros_like(dc_ref)

        dsink = jnp.zeros((1, 1), F32)
        for i in range(nq):
            lo, hi = _key_range(kind, i, window, Sk)
            rows = slice(i * BQ, (i + 1) * BQ)
            qb = q_ref[rows, :].astype(_CDT)
            kb = k_ref[0, lo:hi, :].astype(_CDT)
            vb = v_ref[0, lo:hi, :].astype(_CDT)
            s, mask = _scores(kind, i, lo, hi, window, qb, kb, crow_ref, sel_ref)
            lse_i = _to_col(lse_ref[0, 0, :, rows])
            p = jnp.where(mask, jnp.exp(s - lse_i), 0.0)
            dob = do_ref[0, 0, rows, :]
            if has_gate:
                od = jnp.sum(o_ref[0, 0, rows, :] * dob, axis=-1, keepdims=True)
                sg = jax.nn.sigmoid(_to_col(z_ref[0, 0, :, rows]))
                dob = dob * sg
                dz_ref[0, 0, :, rows] = _to_row(od * sg * (1.0 - sg))
            dob = dob.astype(_CDT)
            dp = _dot_nt(dob, vb)
            delta = jnp.sum(p * dp, axis=-1, keepdims=True)
            ds = p * (dp - delta)
            dsb = ds.astype(_CDT)
            dq_ref[0, 0, rows, :] = _dot(dsb, kb) * SCALE
            dk_ref[0, lo:hi, :] += _dot_tn(dsb, qb) * SCALE
            dv_ref[0, lo:hi, :] += _dot_tn(p.astype(_CDT), dob)
            if has_bias:
                dc_ref[0, :, rows] += _to_row(jnp.sum(ds, axis=-1, keepdims=True))
                dc_ref[0, :, lo:hi] -= jnp.sum(ds, axis=0, keepdims=True)
            if has_sink:
                sk = sink_ref[0, 0, 0:1, 0:1]
                dsink = dsink - jnp.sum(jnp.exp(sk - lse_i) * delta, axis=0, keepdims=True)
        if has_sink:
            dsink_ref[0, 0] = jnp.broadcast_to(dsink, (8, LANES))

    qspec = pl.BlockSpec((1, 1, S, DH), lambda g, r: (g, r, 0, 0))
    cspec = pl.BlockSpec((1, 1, 1, S), lambda g, r: (g, r, 0, 0))
    kspec = pl.BlockSpec((1, Sk, DH), lambda g, r: (g, 0, 0))
    ins = [q, k, v, o, do, lse]
    specs = [pl.BlockSpec((None, S, DH), lambda g, r: (q0 + g * R + r, 0, 0)),
             pl.BlockSpec((1, Sk, DH), lambda g, r: (k0 + g, 0, 0)), pl.BlockSpec((1, Sk, DH), lambda g, r: (v0 + g, 0, 0)),
             qspec, qspec, cspec]
    if has_bias:
        ins.append(bias)
        specs.append(pl.BlockSpec((1, 1, S), lambda g, r: (g, 0, 0)))
    if has_sink:
        ins.append(sinks)
        specs.append(pl.BlockSpec((1, 1, 8, LANES), lambda g, r: (g, r, 0, 0)))
    if has_sel:
        ins.append(selT)
        specs.append(pl.BlockSpec((1, S, S), lambda g, r: (g, 0, 0)))
    if has_gate:
        ins.append(gate)
        specs.append(cspec)
    names = ["dq", "dk", "dv"]
    outs = [jax.ShapeDtypeStruct((G, R, S, DH), F32), jax.ShapeDtypeStruct((G, Sk, DH), F32),
            jax.ShapeDtypeStruct((G, Sk, DH), F32)]
    ospecs = [qspec, kspec, kspec]
    if has_bias:
        assert R == 1
        names.append("dc")
        outs.append(jax.ShapeDtypeStruct((G, 1, S), F32))
        ospecs.append(pl.BlockSpec((1, 1, S), lambda g, r: (g, 0, 0)))
    if has_sink:
        names.append("dsink")
        outs.append(jax.ShapeDtypeStruct((G, R, 8, LANES), F32))
        ospecs.append(pl.BlockSpec((1, 1, 8, LANES), lambda g, r: (g, r, 0, 0)))
    if has_gate:
        names.append("dz")
        outs.append(jax.ShapeDtypeStruct((G, R, 1, S), F32))
        ospecs.append(cspec)
    res = _pcall(body, out_shape=tuple(outs), grid=(G, R), in_specs=specs, out_specs=tuple(ospecs), name=name,
                 jobs=ctx.take(name) if ctx else (), ins=ins)
    return dict(zip(names, res))


def _combine(o0, o1, o2, z, *, name):
    H, S, _ = o0.shape

    def body(o0_ref, o1_ref, o2_ref, z_ref, o_ref):
        for i in range(S // BQ):
            rows = slice(i * BQ, (i + 1) * BQ)
            acc = jax.nn.sigmoid(_to_col(z_ref[0, 0, :, rows])) * o0_ref[0, rows, :]
            acc = acc + jax.nn.sigmoid(_to_col(z_ref[1, 0, :, rows])) * o1_ref[0, rows, :]
            acc = acc + jax.nn.sigmoid(_to_col(z_ref[2, 0, :, rows])) * o2_ref[0, rows, :]
            o_ref[0, rows, :] = acc

    blk = pl.BlockSpec((1, S, DH), lambda h: (h, 0, 0))
    return pl.pallas_call(body, out_shape=jax.ShapeDtypeStruct((H, S, DH), F32), grid=(H,),
                          in_specs=[blk, blk, blk, pl.BlockSpec((3, 1, 1, S), lambda h: (0, h, 0, 0))], out_specs=blk,
                          compiler_params=_params(), name=name)(o0, o1, o2, z)


_GC = math.sqrt(2.0 / math.pi)


def _gelu(x):
    return 0.5 * x * (1.0 + jnp.tanh(_GC * (x + 0.044715 * x * x * x)))


def _gelu_grad(x):
    t = jnp.tanh(_GC * (x + 0.044715 * x * x * x))
    return 0.5 * (1.0 + t) + 0.5 * x * (1.0 - t * t) * _GC * (1.0 + 3.0 * 0.044715 * x * x)


def _make_xb(k):
    G, S, _ = k.shape
    chunks = k.reshape(G, S // 16, 16 * DH)
    shift = jnp.concatenate([chunks[:, 1:], jnp.zeros((G, 1, 16 * DH), k.dtype)], axis=1)
    return jnp.concatenate([chunks, shift], axis=-1)


def _unmake_xb(dxb, *, name):
    G, n, _ = dxb.shape
    a = dxb[..., :16 * DH]
    b = jnp.concatenate([jnp.zeros((G, 1, 16 * DH), F32), dxb[:, :-1, 16 * DH:]], axis=1)
    return _addn(a, b, name=name).reshape(G, n * 16, DH)


def _compress_fwd(xb, pe, w1, w2, *, name):
    G, n, W = xb.shape
    Hc = w1.shape[1]

    def body(xb_ref, pe_ref, w1_ref, w2_ref, kc_ref, hid_ref):
        xv = (xb_ref[0] + pe_ref[...]).astype(_CDT)
        hid = _dot(xv, w1_ref[...].astype(_CDT))
        hid_ref[0] = hid
        kc_ref[0] = _dot(_gelu(hid).astype(_CDT), w2_ref[...].astype(_CDT))

    return pl.pallas_call(
        body, out_shape=(jax.ShapeDtypeStruct((G, n, DH), F32), jax.ShapeDtypeStruct((G, n, Hc), F32)), grid=(G,),
        in_specs=[pl.BlockSpec((1, n, W), lambda g: (g, 0, 0)), pl.BlockSpec((1, W), lambda g: (0, 0)),
                  pl.BlockSpec((W, Hc), lambda g: (0, 0)), pl.BlockSpec((Hc, DH), lambda g: (0, 0))],
        out_specs=(pl.BlockSpec((1, n, DH), lambda g: (g, 0, 0)), pl.BlockSpec((1, n, Hc), lambda g: (g, 0, 0))),
        compiler_params=_params(), name=name)(xb, pe.reshape(1, W), w1, w2)


def _compress_bwd(xb, pe, w1, w2, hid, dkc, *, name):
    G, n, W = xb.shape
    Hc = w1.shape[1]

    def body(xb_ref, pe_ref, w1_ref, w2_ref, hid_ref, dkc_ref, dxb_ref, dw1_ref, dw2_ref, dpe_ref):
        @pl.when(pl.program_id(0) == 0)
        def _():
            dw1_ref[...] = jnp.zeros_like(dw1_ref)
            dw2_ref[...] = jnp.zeros_like(dw2_ref)
            dpe_ref[...] = jnp.zeros_like(dpe_ref)

        xv = (xb_ref[0] + pe_ref[...]).astype(_CDT)
        hid = hid_ref[0]
        dk = dkc_ref[0].astype(_CDT)
        dact = _dot_nt(dk, w2_ref[...].astype(_CDT))
        dhid = (dact * _gelu_grad(hid)).astype(_CDT)
        dw2_ref[...] += _dot_tn(_gelu(hid).astype(_CDT), dk)
        dxb = _dot_nt(dhid, w1_ref[...].astype(_CDT))
        dxb_ref[0] = dxb
        dw1_ref[...] += _dot_tn(xv, dhid)
        dpe_ref[...] += jnp.sum(dxb, axis=0, keepdims=True)

    return pl.pallas_call(
        body, out_shape=(jax.ShapeDtypeStruct((G, n, W), F32), jax.ShapeDtypeStruct((W, Hc), F32),
                         jax.ShapeDtypeStruct((Hc, DH), F32), jax.ShapeDtypeStruct((1, W), F32)), grid=(G,),
        in_specs=[pl.BlockSpec((1, n, W), lambda g: (g, 0, 0)), pl.BlockSpec((1, W), lambda g: (0, 0)),
                  pl.BlockSpec((W, Hc), lambda g: (0, 0)), pl.BlockSpec((Hc, DH), lambda g: (0, 0)),
                  pl.BlockSpec((1, n, Hc), lambda g: (g, 0, 0)), pl.BlockSpec((1, n, DH), lambda g: (g, 0, 0))],
        out_specs=(pl.BlockSpec((1, n, W), lambda g: (g, 0, 0)), pl.BlockSpec((W, Hc), lambda g: (0, 0)),
                   pl.BlockSpec((Hc, DH), lambda g: (0, 0)), pl.BlockSpec((1, W), lambda g: (0, 0))),
        compiler_params=_params(), name=name)(xb, pe.reshape(1, W), w1, w2, hid, dkc)


def _overlap_T(S):
    n_cmp, n_slc = S // 16 - 1, S // 64
    cs = np.arange(n_cmp) * 16
    ce = cs + 32
    ss = np.arange(n_slc) * 64
    se = ss + 64
    ov = np.clip(np.minimum(ce[:, None], se[None, :]) - np.maximum(cs[:, None], ss[None, :]), 0, None) / 32.0
    out = np.zeros((LANES, S // 16), np.float32)
    out[:n_slc, :n_cmp] = ov.T
    return jnp.asarray(out)


def _tri(n, upper):
    r = lax.broadcasted_iota(jnp.int32, (n, n), 0)
    c = lax.broadcasted_iota(jnp.int32, (n, n), 1)
    return ((c >= r) if upper else (c <= r)).astype(jnp.bfloat16)


def _fox_gate_fwd(zf, b, *, name):
    S, H = zf.shape
    nb = S // BQ

    def body(z_ref, b_ref, c_ref):
        tri = _tri(BQ, False)
        carry = jnp.zeros((1, H), F32)
        for i in range(nb):
            z = z_ref[i * BQ:(i + 1) * BQ, :] + b_ref[...]
            lf = jnp.minimum(z, 0.0) - jnp.log(1.0 + jnp.exp(-jnp.abs(z)))
            c_ref[i * BQ:(i + 1) * BQ, :] = _dot_exact_left(tri, lf) + carry
            carry = carry + jnp.sum(lf, axis=0, keepdims=True)

    return pl.pallas_call(body, out_shape=jax.ShapeDtypeStruct((S, H), F32), compiler_params=_params(),
                          name=name)(zf, b)


def _fox_gate_bwd(zf, b, dc, *, name):
    S, H = zf.shape
    nb = S // BQ

    def body(z_ref, b_ref, dc_ref, dz_ref, db_ref):
        tri = _tri(BQ, True)
        carry = jnp.zeros((1, H), F32)
        db = jnp.zeros((1, H), F32)
        for i in reversed(range(nb)):
            rows = slice(i * BQ, (i + 1) * BQ)
            dcb = dc_ref[rows, :]
            dlf = _dot_exact_left(tri, dcb) + carry
            carry = carry + jnp.sum(dcb, axis=0, keepdims=True)
            z = z_ref[rows, :] + b_ref[...]
            dz = dlf * jax.nn.sigmoid(-z)
            dz_ref[rows, :] = dz
            db = db + jnp.sum(dz, axis=0, keepdims=True)
        db_ref[...] = db

    return pl.pallas_call(body, out_shape=(jax.ShapeDtypeStruct((S, H), F32), jax.ShapeDtypeStruct((1, H), F32)),
                          compiler_params=_params(), name=name)(zf, b, dc)


def _from_heads(a):
    return a.transpose(1, 0, 2).reshape(a.shape[1], -1)


def _pad_lanes(a):
    return jnp.pad(a, ((0, 0), (0, LANES - a.shape[1])))


FFN_TM = 512


def _ffn_up(h, w4, *, ctx, name):
    S, D = h.shape
    C = w4.shape[2]

    def body(h_ref, wg_ref, wu_ref, gu_ref, a_ref):
        hv = h_ref[...].astype(_CDT)
        g = _dot(hv, wg_ref[...].astype(_CDT))
        u = _dot(hv, wu_ref[...].astype(_CDT))
        gu_ref[0] = g
        gu_ref[1] = u
        a_ref[...] = (g * jax.nn.sigmoid(g) * u).astype(_CDT)

    return _pcall(
        body, out_shape=(jax.ShapeDtypeStruct((2, S, 2 * C), F32), jax.ShapeDtypeStruct((S, 2 * C), _CDT)),
        grid=(S // FFN_TM, 2),
        in_specs=[pl.BlockSpec((FFN_TM, D), lambda i, j: (i, 0)), pl.BlockSpec((None, D, C), lambda i, j: (j, 0, 0)),
                  pl.BlockSpec((None, D, C), lambda i, j: (j + 2, 0, 0))],
        out_specs=(pl.BlockSpec((2, FFN_TM, C), lambda i, j: (0, i, j)), pl.BlockSpec((FFN_TM, C), lambda i, j: (i, j))),
        name=name, jobs=ctx.take(name) if ctx else (), ins=[h, w4, w4])


def _ffn_dact(dxo, wd, gu, *, ctx, name):
    S, D = dxo.shape
    C = gu.shape[2] // 2

    def body(dx_ref, wd_ref, gu_ref, d_ref):
        da = _dot_nt(dx_ref[...].astype(_CDT), wd_ref[...].astype(_CDT)) * 0.5
        g, u = gu_ref[0], gu_ref[1]
        sg = jax.nn.sigmoid(g)
        silu = g * sg
        d_ref[0] = da * u * (sg + silu * (1.0 - sg))
        d_ref[1] = da * silu

    blk = pl.BlockSpec((2, FFN_TM, C), lambda i, j: (0, i, j))
    return _pcall(body, out_shape=jax.ShapeDtypeStruct(gu.shape, F32), grid=(S // FFN_TM, 2),
                  in_specs=[pl.BlockSpec((FFN_TM, D), lambda i, j: (i, 0)), pl.BlockSpec((C, D), lambda i, j: (j, 0)), blk],
                  out_specs=blk, name=name, jobs=ctx.take(name) if ctx else (), ins=[dxo, wd, gu])


def _ffn_dwgu(h, dgu, *, ctx, name):
    S, D = h.shape
    C = dgu.shape[2] // 2
    tm = 512

    def body(h_ref, d_ref, o_ref):
        o_ref[...] = _dot_tn(h_ref[...].astype(_CDT), d_ref[...].astype(_CDT)).astype(_WIRE)

    return _pcall(body, out_shape=jax.ShapeDtypeStruct((4, D, C), _WIRE), grid=(D // tm, 4),
                  in_specs=[pl.BlockSpec((S, tm), lambda i, j: (0, i)),
                            pl.BlockSpec((None, S, C), lambda i, j: (j // 2, 0, j % 2))],
                  out_specs=pl.BlockSpec((None, tm, C), lambda i, j: (j, i, 0)), name=name,
                  jobs=ctx.take(name) if ctx else (), ins=[h, dgu])


def _ffn_dh(dgu, w4, *, ctx, name):
    _, S, F2 = dgu.shape
    C, D = F2 // 2, w4.shape[1]

    def body(d_ref, w_ref, o_ref, acc_ref):
        k = pl.program_id(1)
        prod = _dot_nt(d_ref[...].astype(_CDT), w_ref[...].astype(_CDT))

        @pl.when(k == 0)
        def _():
            acc_ref[...] = prod

        @pl.when(k != 0)
        def _():
            acc_ref[...] += prod

        @pl.when(k == 3)
        def _():
            o_ref[...] = acc_ref[...]

    return _pcall(body, out_shape=jax.ShapeDtypeStruct((S, D), F32), grid=(S // FFN_TM, 4),
                  in_specs=[pl.BlockSpec((None, FFN_TM, C), lambda i, k: (k // 2, i, k % 2)),
                            pl.BlockSpec((None, D, C), lambda i, k: (k, 0, 0))],
                  out_specs=pl.BlockSpec((FFN_TM, D), lambda i, k: (i, 0)), scratch_shapes=[pltpu.VMEM((FFN_TM, D), F32)],
                  name=name, jobs=ctx.take(name) if ctx else (), ins=[dgu, w4])


def _ffn_fwd(x, g, P, l, tag):
    h = _rms_fwd(x, g, name=tag + "_rms")
    gu, a = _ffn_up(h, P.weight("gu", l), ctx=P.ctx, name=tag + "_gu")
    xo = _mm(a, P.weight("down", l), tm=512, tn=1024, res=x, alpha=0.5, ctx=P.ctx, name=tag + "_down")
    return xo, (x, h, gu, a)


def _ffn_bwd(dxo, saved, g, P, l, tag):
    x, h, gu, a = saved
    dgu = _ffn_dact(dxo, P.weight("down", l), gu, ctx=P.ctx, name=tag + "_dact")
    P.grad("gu", l, _ffn_dwgu(h, dgu, ctx=P.ctx, name=tag + "_dwgu"))
    dwd = _mm(a, dxo, ta=True, tm=1408, tn=1024, alpha=0.5, out_dtype=_WIRE, ctx=P.ctx, name=tag + "_dwd")
    P.grad("down", l, dwd.reshape(4, -1, D_MODEL))
    dh = _ffn_dh(dgu, P.weight("gu", l), ctx=P.ctx, name=tag + "_dh")
    dx, dg = _rms_bwd(dh, x, g, dxo, name=tag + "_drms")
    return dx, dg


def _nsa_fwd(x, g, w, cc, ss, tag):
    S = x.shape[0]
    h = _rms_fwd(x, g, name=tag + "_rms")
    hm = _mm(h, w["w_in"], tm=512, tn=896, o_lead=("heads",), ctx=w["ctx"], name=tag + "_in")
    roped = _rope(hm, cc, ss, NSA_ROTATED, name=tag + "_rope")
    z = roped[40, :, :3 * N_HEADS].reshape(S, 3, N_HEADS).transpose(1, 2, 0)[:, :, None, :]
    xbk, xbv = _make_xb(roped[16:20]), _make_xb(roped[20:24])
    kc, hidk = _compress_fwd(xbk, w["ck_pe"], w["ck_w1"], w["ck_w2"], name=tag + "_ck")
    vc, hidv = _compress_fwd(xbv, w["cv_pe"], w["cv_w1"], w["cv_w2"], name=tag + "_cv")
    ctx, GR = w["ctx"], dict(G=NSA_G, R=NSA_R)
    o0, lse0, selT = _attn_fwd(roped, kc, vc, kind="cmp", ovT=_overlap_T(S), ctx=ctx, name=tag + "_cmp", **GR)
    o1, lse1 = _attn_fwd(roped, roped, roped, heads=(0, 24, 28), kind="sel", selT=selT, ctx=ctx, name=tag + "_slc", **GR)
    o2, lse2 = _attn_fwd(roped, roped, roped, heads=(0, 32, 36), kind="band", window=NSA_WINDOW, ctx=ctx,
                         name=tag + "_win", **GR)
    o = _combine(o0.reshape(N_HEADS, S, DH), o1.reshape(N_HEADS, S, DH), o2.reshape(N_HEADS, S, DH), z,
                 name=tag + "_mix")
    of = _from_heads(o)
    xo = _mm(of, w["out"], tm=512, tn=1024, res=x, ctx=w["ctx"], name=tag + "_out")
    saved = (x, h, roped, z, xbk, xbv, hidk, hidv, kc, vc, (o0, o1, o2), (lse0, lse1, lse2), selT, of)
    return xo, saved


def _out_bwd(dxo, of, w, tag):
    dof = _mm(dxo, w["out"], tb=True, tm=512, tn=1024, o_lead=("heads",), ctx=w["ctx"], name=tag + "_dof")
    dw = _mm(of, dxo, ta=True, tm=512, tn=1024, out_dtype=_WIRE, ctx=w["ctx"], name=tag + "_dwout")
    w["P"].grad("out", w["out_l"], dw.reshape(4, -1, D_MODEL))
    return dof


def _split_cols(dw, n_in):
    cs = n_in // 4
    return dw[:, :n_in].reshape(D_MODEL, 4, cs).transpose(1, 0, 2).astype(_WIRE)


def _nsa_bwd(dxo, saved, g, w, cc, ss, tag):
    x, h, roped, z, xbk, xbv, hidk, hidv, kc, vc, os_, lses, selT, of = saved
    S = x.shape[0]
    do = _out_bwd(dxo, of, w, tag).reshape(NSA_G, NSA_R, S, DH)
    zg = z.reshape(3, NSA_G, NSA_R, 1, S)
    ctx = w["ctx"]
    b0 = _attn_bwd(roped, kc, vc, os_[0], do, lses[0], kind="cmp", gate=zg[0], ctx=ctx, name=tag + "_dcmp")
    b1 = _attn_bwd(roped, roped, roped, os_[1], do, lses[1], heads=(0, 24, 28), kind="sel", selT=selT, gate=zg[1],
                   ctx=ctx, name=tag + "_dslc")
    b2 = _attn_bwd(roped, roped, roped, os_[2], do, lses[2], heads=(0, 32, 36), kind="band", window=NSA_WINDOW,
                   gate=zg[2], ctx=ctx, name=tag + "_dwin")
    dxbk, dck_w1, dck_w2, dck_pe = _compress_bwd(xbk, w["ck_pe"], w["ck_w1"], w["ck_w2"], hidk, b0["dk"],
                                                 name=tag + "_dck")
    dxbv, dcv_w1, dcv_w2, dcv_pe = _compress_bwd(xbv, w["cv_pe"], w["cv_w1"], w["cv_w2"], hidv, b0["dv"],
                                                 name=tag + "_dcv")
    dk0 = _unmake_xb(dxbk, name=tag + "_dk0")
    dv0 = _unmake_xb(dxbv, name=tag + "_dv0")
    dq = _addn(b0["dq"], b1["dq"], b2["dq"], name=tag + "_dqsum").reshape(N_HEADS, S, DH)
    dz = jnp.stack([b0["dz"], b1["dz"], b2["dz"]], axis=0).reshape(3 * N_HEADS, S).T
    dgates = jnp.pad(dz, ((0, 0), (0, 2 * DH - 3 * N_HEADS))).reshape(S, 2, DH).transpose(1, 0, 2)
    dhm = jnp.concatenate([dq, dk0, dv0, b1["dk"], b1["dv"], b2["dk"], b2["dv"], dgates], axis=0)
    dproj = _from_heads(_rope(dhm, cc, -ss, NSA_ROTATED, name=tag + "_drope"))
    dh = _mm(dproj, w["w_in"], tb=True, tm=512, tn=512, ctx=w["ctx"], name=tag + "_dh")
    dw_in = _mm(h, dproj, ta=True, tm=512, tn=896, ctx=w["ctx"], name=tag + "_dwin_w")
    dx, dg = _rms_bwd(dh, x, g, dxo, name=tag + "_drms")
    P, j = w["P"], w["j"]
    P.grad("nsa_in", j, _split_cols(dw_in, NSA_IN))
    P.grad("cw1", j, dck_w1.astype(_WIRE).reshape(4, -1, dck_w1.shape[1]))
    P.grad("cw1", 2 + j, dcv_w1.astype(_WIRE).reshape(4, -1, dcv_w1.shape[1]))
    grads = dict(ck_pe=dck_pe.reshape(32, DH), ck_w2=dck_w2, cv_pe=dcv_pe.reshape(32, DH), cv_w2=dcv_w2)
    return dx, dg, grads


def _swa_fwd(x, g, w, cc, ss, tag):
    S = x.shape[0]
    h = _rms_fwd(x, g, name=tag + "_rms")
    hm = _mm(h, w["w_in"], tm=512, tn=640, o_lead=("heads",), ctx=w["ctx"], name=tag + "_in")
    roped = _rope(hm, cc, ss, SWA_ROTATED, name=tag + "_rope")
    sinks = jnp.broadcast_to(w["sinks"].reshape(SWA_G, SWA_R, 1, 1), (SWA_G, SWA_R, 8, LANES))
    o, lse = _attn_fwd(roped, roped, roped, G=SWA_G, R=SWA_R, heads=SWA_HEADS, kind="band", window=SWA_WINDOW,
                       sinks=sinks, ctx=w["ctx"], name=tag + "_attn")
    of = _from_heads(o.reshape(N_HEADS, S, DH))
    xo = _mm(of, w["out"], tm=512, tn=1024, res=x, ctx=w["ctx"], name=tag + "_out")
    return xo, (x, h, roped, sinks, o, lse, of)


def _swa_bwd(dxo, saved, g, w, cc, ss, tag):
    x, h, roped, sinks, o, lse, of = saved
    S = x.shape[0]
    do = _out_bwd(dxo, of, w, tag).reshape(SWA_G, SWA_R, S, DH)
    b = _attn_bwd(roped, roped, roped, o, do, lse, heads=SWA_HEADS, kind="band", window=SWA_WINDOW, sinks=sinks,
                  ctx=w["ctx"], name=tag + "_dattn")
    dhm = jnp.concatenate([b["dq"].reshape(N_HEADS, S, DH), b["dk"], b["dv"]], axis=0)
    dproj = _from_heads(_rope(dhm, cc, -ss, SWA_ROTATED, name=tag + "_drope"))
    dh = _mm(dproj, w["w_in"], tb=True, tm=512, tn=512, ctx=w["ctx"], name=tag + "_dh")
    dw_in = _mm(h, dproj, ta=True, tm=512, tn=640, ctx=w["ctx"], name=tag + "_dwin_w")
    dx, dg = _rms_bwd(dh, x, g, dxo, name=tag + "_drms")
    w["P"].grad("swa_in", w["j"], _split_cols(dw_in, SWA_IN))
    return dx, dg, dict(sinks=b["dsink"][:, :, 0, 0].reshape(N_HEADS))


def _fox_fwd(x, g, w, tag):
    S = x.shape[0]
    h = _rms_fwd(x, g, name=tag + "_rms")
    hm = _mm(h, w["w_in"], tm=512, tn=640, o_lead=("heads",), ctx=w["ctx"], name=tag + "_in")
    zf = jnp.pad(hm[48], ((0, 0), (0, LANES - DH)))
    bf = _pad_lanes(w["b_f"].reshape(1, N_HEADS))
    c = _fox_gate_fwd(zf, bf, name=tag + "_gate")[:, :N_HEADS]
    bias = c.T[:, None, :]
    o, lse = _attn_fwd(hm, hm, hm, G=N_HEADS, R=1, heads=FOX_HEADS, kind="causal", bias=bias, ctx=w["ctx"],
                       name=tag + "_attn")
    of = _from_heads(o.reshape(N_HEADS, S, DH))
    xo = _mm(of, w["out"], tm=512, tn=1024, res=x, ctx=w["ctx"], name=tag + "_out")
    return xo, (x, h, hm, zf, bf, bias, o, lse, of)


def _fox_bwd(dxo, saved, g, w, tag):
    x, h, hm, zf, bf, bias, o, lse, of = saved
    S = x.shape[0]
    do = _out_bwd(dxo, of, w, tag).reshape(N_HEADS, 1, S, DH)
    b = _attn_bwd(hm, hm, hm, o, do, lse, heads=FOX_HEADS, kind="causal", bias=bias, ctx=w["ctx"], name=tag + "_dattn")
    dzf, db = _fox_gate_bwd(zf, bf, _pad_lanes(b["dc"].reshape(N_HEADS, S).T), name=tag + "_dgate")
    dgates = dzf.reshape(S, 2, DH).transpose(1, 0, 2)
    dhm = jnp.concatenate([b["dq"].reshape(N_HEADS, S, DH), b["dk"], b["dv"], dgates], axis=0)
    dproj = _from_heads(dhm)
    dh = _mm(dproj, w["w_in"], tb=True, tm=512, tn=512, ctx=w["ctx"], name=tag + "_dh")
    dw_in = _mm(h, dproj, ta=True, tm=512, tn=640, ctx=w["ctx"], name=tag + "_dwin_w")
    dx, dg = _rms_bwd(dh, x, g, dxo, name=tag + "_drms")
    w["P"].grad("fox_in", w["j"], _split_cols(dw_in, FOX_IN))
    return dx, dg, dict(b_f=db[0, :N_HEADS])


GROUPS = {
    "gu": (("ffn1_w_gu", "ffn2_w_gu"), 2),
    "down": (("ffn1_w_down", "ffn2_w_down"), 1),
    "out": (("nsa_w_out", "swa_w_out", "fox_w_out"), 1),
    "cw1": (("nsa_ck_w1", "nsa_cv_w1"), 1),
    "nsa_in": (("nsa_w_in",), 2),
    "swa_in": (("swa_w_in",), 2),
    "fox_in": (("fox_w_in",), 2),
}
OUT_SLAB = {0: 0, 3: 1, 1: 2, 2: 3}


def _pieces_in_order():
    chunks = []
    for i in range(DEPTH):
        kind, j = i % 3, i // 3
        chunks.append([("gu", i), ("down", i)])
        if kind == 0:
            chunks.append([("nsa_in", j), ("cw1", j), ("cw1", 2 + j), ("out", OUT_SLAB[i])])
        else:
            chunks.append([("swa_in" if kind == 1 else "fox_in", j), ("out", OUT_SLAB[i])])
        chunks.append([("gu", DEPTH + i), ("down", DEPTH + i)])
    return chunks


def _consumer_layout(group, F):
    _, rows, C = F.shape
    if group == "gu":
        return F
    if GROUPS[group][1] == 1:
        return F.reshape(4 * rows, C)
    w = F.transpose(1, 0, 2).reshape(rows, 4 * C)
    pad = {"nsa_in": NSA_IN_PAD, "swa_in": SWA_IN, "fox_in": FOX_IN_PAD}[group] - 4 * C
    return jnp.pad(w, ((0, 0), (0, pad)))


class _Given:
    def __init__(self, pieces, small):
        self.pieces, self.small, self.ctx, self.grads = pieces, small, None, {}

    def weight(self, group, l):
        return _consumer_layout(group, self.pieces[group, l])

    def grad(self, group, l, G):
        self.grads[group, l] = G


class _MixerWeights(dict):
    def __init__(self, P, pieces, **given):
        super().__init__(P=P, ctx=P.ctx, **given)
        self.pieces = pieces

    def __missing__(self, key):
        self[key] = self["P"].weight(*self.pieces[key])
        return self[key]


def _mixer_weights(P, i):
    kind, j = i % 3, i // 3
    out = {"out": ("out", OUT_SLAB[i])}
    if kind == 0:
        small = {k: P.small["nsa_" + k][j] for k in ("ck_pe", "ck_w2", "cv_pe", "cv_w2")}
        return _MixerWeights(P, dict(out, w_in=("nsa_in", j), ck_w1=("cw1", j), cv_w1=("cw1", 2 + j)), j=j,
                             out_l=OUT_SLAB[i], **small)
    if kind == 1:
        return _MixerWeights(P, dict(out, w_in=("swa_in", j)), j=j, out_l=OUT_SLAB[i], sinks=P.small["swa_sinks"][j])
    return _MixerWeights(P, dict(out, w_in=("fox_in", j)), j=j, out_l=OUT_SLAB[i], b_f=P.small["fox_b_f"][j])


def _local_step(x, tgt, P):
    S = x.shape[0]
    cc, ss = _rope_tables(S)
    sm = P.small
    saved = []
    for i in range(DEPTH):
        kind = i % 3
        x, s1 = _ffn_fwd(x, sm["ffn1_norm"][i], P, i, f"l{i}f1")
        mw = _mixer_weights(P, i)
        if kind == 0:
            x, s2 = _nsa_fwd(x, sm["mix_norm"][i], mw, cc, ss, f"l{i}nsa")
        elif kind == 1:
            x, s2 = _swa_fwd(x, sm["mix_norm"][i], mw, cc, ss, f"l{i}swa")
        else:
            x, s2 = _fox_fwd(x, sm["mix_norm"][i], mw, f"l{i}fox")
        x, s3 = _ffn_fwd(x, sm["ffn2_norm"][i], P, DEPTH + i, f"l{i}f2")
        saved.append((s1, mw, s2, s3))
    loss, dx, d_final = _loss_head(x, sm["final_norm"], tgt, name="loss_head")

    norms = {k: [None] * DEPTH for k in ("ffn1_norm", "mix_norm", "ffn2_norm")}
    mix = {}
    for i in reversed(range(DEPTH)):
        kind, j = i % 3, i // 3
        s1, mw, s2, s3 = saved[i]
        dx, norms["ffn2_norm"][i] = _ffn_bwd(dx, s3, sm["ffn2_norm"][i], P, DEPTH + i, f"l{i}f2")
        if kind == 0:
            dx, dg, gm = _nsa_bwd(dx, s2, sm["mix_norm"][i], mw, cc, ss, f"l{i}nsa")
            pre = "nsa_"
        elif kind == 1:
            dx, dg, gm = _swa_bwd(dx, s2, sm["mix_norm"][i], mw, cc, ss, f"l{i}swa")
            pre = "swa_"
        else:
            dx, dg, gm = _fox_bwd(dx, s2, sm["mix_norm"][i], mw, f"l{i}fox")
            pre = "fox_"
        norms["mix_norm"][i] = dg
        for k, val in gm.items():
            mix.setdefault(pre + k, {})[j] = val
        dx, norms["ffn1_norm"][i] = _ffn_bwd(dx, s1, sm["ffn1_norm"][i], P, i, f"l{i}f1")
    small = {k: jnp.stack(v, axis=0) for k, v in norms.items()}
    small.update({k: jnp.stack([d[j] for j in sorted(d)], axis=0) for k, d in mix.items()})
    small["final_norm"] = d_final
    return loss, dx, small


def _sum_slots(own, recv, me, into, row0, *, name):
    _, R, C = recv.shape
    tr = _row_tile(math.gcd(R, row0), C)
    blk0 = row0 // tr

    def body(_, g_ref, r_ref, __, o_ref):
        acc = g_ref[...].astype(F32) + r_ref[0].astype(F32)
        acc = acc + r_ref[1].astype(F32)
        o_ref[...] = acc + r_ref[2].astype(F32)

    spec = pltpu.PrefetchScalarGridSpec(
        num_scalar_prefetch=1, grid=(R // tr,),
        in_specs=[pl.BlockSpec((None, tr, C), lambda i, me_ref: (me_ref[0], i, 0)),
                  pl.BlockSpec((3, tr, C), lambda i, me_ref: (0, i, 0)), _ANY],
        out_specs=pl.BlockSpec((tr, C), lambda i, me_ref: (blk0 + i, 0)))
    return pl.pallas_call(body, grid_spec=spec, out_shape=jax.ShapeDtypeStruct(into.shape, F32),
                          input_output_aliases={3: 0}, compiler_params=_params(), name=name)(me, own, recv, into)


def _swap_sibling(parts):
    n = len(parts)

    def body(*refs):
        srcs, outs = refs[:n], refs[n:2 * n]
        send_sems, recv_sems = refs[2 * n:]
        x, y, c = lax.axis_index("x"), lax.axis_index("y"), lax.axis_index("c")
        cps = [pltpu.make_async_remote_copy(src_ref=srcs[g], dst_ref=outs[g], send_sem=send_sems.at[g],
                                            recv_sem=recv_sems.at[g], device_id=(x, y, 1 - c), device_id_type=MESH)
               for g in range(n)]
        for cp in cps:
            cp.start()
        for cp in cps:
            cp.wait()

    return pl.pallas_call(
        body, out_shape=tuple(jax.ShapeDtypeStruct(p.shape, p.dtype) for p in parts),
        in_specs=[_ANY] * n, out_specs=(_ANY,) * n,
        scratch_shapes=[pltpu.SemaphoreType.DMA((n,)), pltpu.SemaphoreType.DMA((n,))],
        compiler_params=pltpu.CompilerParams(has_side_effects=True), name="swap_core_grads")(*parts)


def _flip(coord, bit):
    return 1 - coord if bit else coord


def _allreduce_small(v):
    n, C = v.shape

    def body(v_ref, o_ref, buf, send_sems, recv_sems):
        x, y, c = lax.axis_index("x"), lax.axis_index("y"), lax.axis_index("c")
        me = 4 * x + 2 * y + c
        buf[me] = v_ref[...]
        peers = [(_flip(x, (j >> 2) & 1), _flip(y, (j >> 1) & 1), _flip(c, j & 1)) for j in range(1, 8)]
        sends = [pltpu.make_async_remote_copy(src_ref=v_ref, dst_ref=buf.at[me], send_sem=send_sems.at[j],
                                              recv_sem=recv_sems.at[j], device_id=peer, device_id_type=MESH)
                 for j, peer in enumerate(peers)]
        for cp in sends:
            cp.start()
        for j, (px, py, pc) in enumerate(peers):
            pltpu.make_async_remote_copy(src_ref=v_ref, dst_ref=buf.at[4 * px + 2 * py + pc], send_sem=send_sems.at[j],
                                         recv_sem=recv_sems.at[j], device_id=(px, py, pc),
                                         device_id_type=MESH).wait_recv()
        for cp in sends:
            cp.wait_send()
        acc = buf[0]
        for d in range(1, 8):
            acc = acc + buf[d]
        o_ref[...] = acc

    return pl.pallas_call(
        body, out_shape=jax.ShapeDtypeStruct((n, C), F32),
        in_specs=[pl.BlockSpec(memory_space=pltpu.VMEM)], out_specs=pl.BlockSpec(memory_space=pltpu.VMEM),
        scratch_shapes=[pltpu.VMEM((8, n, C), F32), pltpu.SemaphoreType.DMA((7,)), pltpu.SemaphoreType.DMA((7,))],
        compiler_params=pltpu.CompilerParams(has_side_effects=True), name="allreduce_small")(v)


def _adamw(w, m, v, gs, *, row0=0, name):
    shape = w.shape
    w3, m3, v3 = (a.reshape((-1,) + a.shape[-2:]) for a in (w, m, v))
    g2 = [_rows2d(g) for g in gs]
    L, rows, C = w3.shape
    tr = _row_tile(math.gcd(rows, row0), C, budget=1024 * 1024)
    ng = len(g2)
    blk0, nb = row0 // tr, rows // tr

    def body(*refs):
        w_ref, m_ref, v_ref = refs[:3]
        g = refs[3][...]
        for r in refs[4:3 + ng]:
            g = g + r[...]
        g_ref, d_ref, nm_ref, nv_ref = refs[3 + ng:]
        mn = B1 * m_ref[...] + (1.0 - B1) * g
        vn = B2 * v_ref[...] + (1.0 - B2) * (g * g)
        m_hat = mn / (1.0 - B1 ** STEP)
        v_hat = vn / (1.0 - B2 ** STEP)
        g_ref[...] = g
        d_ref[...] = -LR * (m_hat / (jnp.sqrt(v_hat) + EPS) + WD * w_ref[...])
        nm_ref[...] = mn
        nv_ref[...] = vn

    spec = pl.BlockSpec((None, tr, C), lambda l, i: (l, i, 0))
    gspec = pl.BlockSpec((tr, C), lambda l, i: (blk0 + l * nb + i, 0))
    outs = pl.pallas_call(body, out_shape=tuple(jax.ShapeDtypeStruct((L, rows, C), F32) for _ in range(4)),
                          grid=(L, nb), in_specs=[spec] * 3 + [gspec] * ng, out_specs=(spec,) * 4,
                          compiler_params=_params(), name=name)(w3, m3, v3, *g2)
    return tuple(o.reshape(shape) for o in outs)


def _small_layout(shapes):
    offs, off = {}, 0
    for k in REPLICATED:
        n = int(np.prod(shapes[k]))
        offs[k] = (off, n)
        off += -(-n // LANES) * LANES
    return offs, off


def _pack_small(d, shapes):
    offs, total = _small_layout(shapes)
    parts = []
    for k in REPLICATED:
        n = offs[k][1]
        parts.append(jnp.pad(d[k].reshape(-1).astype(F32), (0, -(-n // LANES) * LANES - n)))
    rows = -(-(total // LANES) // 8) * 8
    return jnp.pad(jnp.concatenate(parts), (0, rows * LANES - total)).reshape(rows, LANES)


def _unpack_small(a, shapes):
    offs, _ = _small_layout(shapes)
    flat = a.reshape(-1)
    return {k: flat[offs[k][0]:offs[k][0] + offs[k][1]].reshape(shapes[k]) for k in REPLICATED}


def _group_shards(w):
    shards = []
    for members, _ in GROUPS.values():
        s = jnp.concatenate([w[k].astype(_WIRE) for k in members], axis=0)
        shards.append(s.reshape(s.shape[0], 2, s.shape[1] // 2, s.shape[2]))
    return shards


class _Exchanged:
    def __init__(self, w):
        self.small = {k: w[k] for k in REPLICATED}
        self.ctx = _Sched()
        shards = dict(zip(GROUPS, _group_shards(w)))
        self.gather, self.cache, self.scatter = {}, {}, {}
        chunks = _pieces_in_order()
        self.me = jnp.reshape(2 * lax.axis_index("x") + lax.axis_index("y"), (1,)).astype(jnp.int32)
        first = {(g, l): _gather_task1(shards[g], l, _place_own(shards[g], l, self.me, name=f"own_{g}{l}"))
                 for chunk in chunks for g, l in chunk}
        self.gather = {p: _gather_task2(task) for p, task in first.items()}
        pieces = [p for chunk in chunks for p in chunk]
        for n, p in enumerate(pieces):
            self.ctx.push(first[p])
            if n:
                self.ctx.push(self.gather[pieces[n - 1]])
        self.ctx.push(self.gather[pieces[-1]])

    def weight(self, group, l):
        if (group, l) not in self.cache:
            task = self.gather[group, l]
            self.ctx.finish(task)
            F = task.result()
            self.cache[group, l] = _consumer_layout(group, F.reshape(4, -1, F.shape[-1]))
        return self.cache[group, l]

    def grad(self, group, l, G):
        self.scatter[group, l] = (G, self.ctx.push(_scatter_task(G)))

    def partial_sums(self, w):
        self.ctx.finish()
        parts = []
        for group, (members, _) in GROUPS.items():
            rows, C = w[members[0]].shape[1:]
            n = sum(w[k].shape[0] for k in members)
            part = jnp.zeros((n * rows, C), F32)
            for l in range(n):
                G, task = self.scatter[group, l]
                part = _sum_slots(G, task.result(), self.me, part, l * rows, name=f"sum_{group}{l}")
            parts.append(part)
        return parts


def _reduce_and_update(w, m, v, parts, small):
    others = _swap_sibling(parts)
    small_shapes = {k: w[k].shape for k in REPLICATED}
    g_small = _unpack_small(_allreduce_small(_pack_small(small, small_shapes)), small_shapes)

    out_g, out_d, out_m, out_v = {}, {}, {}, {}
    for (members, _), part, other in zip(GROUPS.values(), parts, others):
        row0 = 0
        for k in members:
            out_g[k], out_d[k], out_m[k], out_v[k] = _adamw(w[k], m[k], v[k], [part, other], row0=row0,
                                                            name="adamw_" + k)
            row0 += w[k].shape[0] * w[k].shape[1]
    sm = _adamw(_pack_small(w, small_shapes), _pack_small(m, small_shapes), _pack_small(v, small_shapes),
                [_pack_small(g_small, small_shapes)], name="adamw_small")
    for d, packed in zip((out_g, out_d, out_m, out_v), sm):
        d.update(_unpack_small(packed, small_shapes))
    return out_g, out_d, out_m, out_v


def kernel(x, ffn1_norm, ffn1_w_gu, ffn1_w_down, mix_norm, ffn2_norm, ffn2_w_gu, ffn2_w_down, nsa_w_in, nsa_ck_pe, nsa_ck_w1, nsa_ck_w2, nsa_cv_pe, nsa_cv_w1, nsa_cv_w2, nsa_w_out, swa_w_in, swa_sinks, swa_w_out, fox_w_in, fox_b_f, fox_w_out, final_norm, loss_target, m_ffn1_norm, m_ffn1_w_gu, m_ffn1_w_down, m_mix_norm, m_ffn2_norm, m_ffn2_w_gu, m_ffn2_w_down, m_nsa_w_in, m_nsa_ck_pe, m_nsa_ck_w1, m_nsa_ck_w2, m_nsa_cv_pe, m_nsa_cv_w1, m_nsa_cv_w2, m_nsa_w_out, m_swa_w_in, m_swa_sinks, m_swa_w_out, m_fox_w_in, m_fox_b_f, m_fox_w_out, m_final_norm, v_ffn1_norm, v_ffn1_w_gu, v_ffn1_w_down, v_mix_norm, v_ffn2_norm, v_ffn2_w_gu, v_ffn2_w_down, v_nsa_w_in, v_nsa_ck_pe, v_nsa_ck_w1, v_nsa_ck_w2, v_nsa_cv_pe, v_nsa_cv_w1, v_nsa_cv_w2, v_nsa_w_out, v_swa_w_in, v_swa_sinks, v_swa_w_out, v_fox_w_in, v_fox_b_f, v_fox_w_out, v_final_norm):
    args = dict(locals())
    w = {k: args[k] for k in WEIGHTS}
    m = {k: args["m_" + k] for k in WEIGHTS}
    v = {k: args["v_" + k] for k in WEIGHTS}
    P = _Exchanged(w)
    loss_part, dx, small = _local_step(x[0], loss_target[0], P)
    loss = lax.psum(loss_part, ("x", "y", "c"))
    out_g, out_d, out_m, out_v = _reduce_and_update(w, m, v, P.partial_sums(w), small)
    return (loss, dx[None], *[out_g[k] for k in WEIGHTS], *[out_d[k] for k in WEIGHTS],
            *[out_m[k] for k in WEIGHTS], *[out_v[k] for k in WEIGHTS])
```

```python
import math

import numpy as np
import jax
import jax.numpy as jnp
from jax import lax
from jax.experimental import pallas as pl
from jax.experimental.pallas import tpu as pltpu

F32 = jnp.float32
_CDT = jnp.bfloat16
_WIRE = jnp.bfloat16
_VMEM_LIMIT = 56 * 1024 * 1024

D_MODEL = 1024
DEPTH = 4
DH = 64
N_HEADS = 16
RMS_EPS = 1e-6
NEG = -1e30
SCALE = DH ** -0.5
BQ = 256
LANES = 128
NSA_G, NSA_R = 4, 4
NSA_WINDOW = 512
NSA_TOPK = 16
NSA_BONUS = 1e4
SWA_G, SWA_R = 2, 8
SWA_WINDOW = 128
NSA_ROTATED = tuple(hd < 16 or (hd < 40 and (hd - 16) % 8 < 4) for hd in range(42))
NSA_IN, NSA_IN_PAD = 2608, 2688
SWA_IN = 1280
SWA_ROTATED = (True,) * 18 + (False,) * 2
SWA_HEADS = (0, 16, 18)
FOX_IN, FOX_IN_PAD = 3088, 3200
FOX_HEADS = (0, 16, 32)
LR, B1, B2, EPS, WD, STEP = 0.001, 0.9, 0.999, 1e-08, 0.01, 10
MESH = pl.DeviceIdType.MESH

REPLICATED = ["ffn1_norm", "mix_norm", "ffn2_norm", "nsa_ck_pe", "nsa_ck_w2", "nsa_cv_pe", "nsa_cv_w2",
              "swa_sinks", "fox_b_f", "final_norm"]
WEIGHTS = ['ffn1_norm', 'ffn1_w_gu', 'ffn1_w_down', 'mix_norm', 'ffn2_norm', 'ffn2_w_gu', 'ffn2_w_down',
           'nsa_w_in', 'nsa_ck_pe', 'nsa_ck_w1', 'nsa_ck_w2', 'nsa_cv_pe', 'nsa_cv_w1', 'nsa_cv_w2', 'nsa_w_out',
           'swa_w_in', 'swa_sinks', 'swa_w_out', 'fox_w_in', 'fox_b_f', 'fox_w_out', 'final_norm']


def _params(**kw):
    return pltpu.CompilerParams(vmem_limit_bytes=_VMEM_LIMIT, **kw)


def _dot(a, b):
    return lax.dot_general(a, b, (((1,), (0,)), ((), ())), preferred_element_type=F32)


def _dot_nt(a, b):
    return lax.dot_general(a, b, (((1,), (1,)), ((), ())), preferred_element_type=F32)


def _dot_tn(a, b):
    return lax.dot_general(a, b, (((0,), (0,)), ((), ())), preferred_element_type=F32)


def _split3(x):
    hi = x.astype(jnp.bfloat16)
    r1 = x - hi.astype(F32)
    mid = r1.astype(jnp.bfloat16)
    lo = (r1 - mid.astype(F32)).astype(jnp.bfloat16)
    return hi, mid, lo


def _dot_exact(x, p):
    hi, mid, lo = _split3(x)
    return _dot(hi, p) + _dot(mid, p) + _dot(lo, p)


def _dot_exact_left(p, x):
    hi, mid, lo = _split3(x)
    return _dot(p, hi) + _dot(p, mid) + _dot(p, lo)


_ANY = pl.BlockSpec(memory_space=pl.ANY)


def _chip_peers():
    x, y, c = lax.axis_index("x"), lax.axis_index("y"), lax.axis_index("c")
    return x, y, c, [(1 - x, y), (x, 1 - y), (1 - x, 1 - y)]


class _Job:
    def __init__(self, ins, outs, nsem, copies, cost, alias):
        self.ins, self.outs, self.nsem, self.copies, self.cost, self.alias = ins, outs, nsem, copies, cost, alias
        self.results = None

    def inputs(self):
        return self.ins


class _Task:
    def __init__(self, src, out, parts, init=None):
        self.src, self.out, self.parts, self.init, self.done, self.jobs = src, out, parts, init, 0, []

    def result(self):
        return self.jobs[-1].results[0]

    def complete(self):
        return self.done == len(self.parts) and self.jobs[-1].results is not None

    def ready(self):
        if isinstance(self.src, _Task) and not self.src.complete():
            return False
        return not self.jobs or self.jobs[-1].results is not None

    def next_job(self, n):
        parts = self.parts[self.done:self.done + n]
        in_place = isinstance(self.src, _Task)
        ins = [] if in_place else [self.src]
        prev = self.jobs[-1].results[0] if self.jobs else (self.src.result() if in_place else self.init)
        alias = {}
        if prev is not None:
            alias = {len(ins): 0}
            ins = ins + [prev]

        def copies(in_refs, out_refs, sems):
            src = out_refs[0] if in_place else in_refs[0]
            out, first = [], 0
            for build, nsem, _ in parts:
                out += build(src, out_refs[0], sems, first)
                first += nsem
            return out

        job = _Job(ins, [self.out], sum(p[1] for p in parts), copies, sum(p[2] for p in parts), alias)
        self.done += n
        self.jobs.append(job)
        return job


def _pcall(body, *, out_shape, in_specs, out_specs, name, grid=(), scratch_shapes=(), aliases=None, jobs=(), ins):
    single = not isinstance(out_shape, (tuple, list))
    out_shape = (out_shape,) if single else tuple(out_shape)
    out_specs = (out_specs,) if single else tuple(out_specs)
    n_in, n_out, n_scr = len(ins), len(out_shape), len(scratch_shapes)
    side_ins = [a for j in jobs for a in j.inputs()]
    side_outs = [o for j in jobs for o in j.outs]
    aliases = dict(aliases or {})
    i0, o0 = n_in, n_out
    for j in jobs:
        for a, b in j.alias.items():
            aliases[i0 + a] = o0 + b
        i0 += len(j.ins)
        o0 += len(j.outs)

    def body2(*refs):
        in_refs, s_in = refs[:n_in], refs[n_in:n_in + len(side_ins)]
        r = n_in + len(side_ins)
        out_refs, s_out = refs[r:r + n_out], refs[r + n_out:r + n_out + len(side_outs)]
        r += n_out + len(side_outs)
        scr, sems = refs[r:r + n_scr], refs[r + n_scr:]
        def descriptors():
            out, a, b = [], 0, 0
            for j, sem in zip(jobs, sems):
                out += j.copies(s_in[a:a + len(j.ins)], s_out[b:b + len(j.outs)], sem)
                a += len(j.ins)
                b += len(j.outs)
            return out

        def at_step(steps, fn):
            cond = None
            for d, s in enumerate(steps):
                hit = pl.program_id(d) == s
                cond = hit if cond is None else jnp.logical_and(cond, hit)
            if cond is None:
                fn()
            else:
                pl.when(cond)(fn)

        def start_all():
            for cp in descriptors():
                cp.start()

        def wait_all():
            for cp in descriptors():
                cp.wait()

        if jobs:
            at_step([0] * len(grid), start_all)
        body(*in_refs, *out_refs, *scr)
        if jobs:
            at_step([n - 1 for n in grid], wait_all)

    res = pl.pallas_call(
        body2, out_shape=out_shape + tuple(side_outs), grid=grid, in_specs=list(in_specs) + [_ANY] * len(side_ins),
        out_specs=out_specs + (_ANY,) * len(side_outs),
        scratch_shapes=list(scratch_shapes) + [pltpu.SemaphoreType.DMA((j.nsem,)) for j in jobs],
        input_output_aliases=aliases,
        compiler_params=_params(has_side_effects=True) if jobs else _params(), name=name)(*ins, *side_ins)
    b = n_out
    for j in jobs:
        j.results = list(res[b:b + len(j.outs)])
        b += len(j.outs)
    return res[0] if single else tuple(res[:n_out])


def _comm_call(jobs, name):
    def body():
        pass

    _pcall(body, out_shape=(), in_specs=[], out_specs=(), name=name, jobs=jobs, ins=[])


def _place_own(shard, l, me, *, name):
    _, _, h, C = shard.shape

    def body(_, s_ref, o_ref):
        o_ref[...] = s_ref[...]

    spec = pltpu.PrefetchScalarGridSpec(
        num_scalar_prefetch=1, grid=(2,),
        in_specs=[pl.BlockSpec((None, None, h, C), lambda i, me_ref: (l, i, 0, 0))],
        out_specs=pl.BlockSpec((None, None, h, C), lambda i, me_ref: (me_ref[0], i, 0, 0)))
    return pl.pallas_call(body, grid_spec=spec, out_shape=jax.ShapeDtypeStruct((4, 2, h, C), shard.dtype),
                          compiler_params=_params(), name=name)(me, shard)


def _gather_task1(shard, l, own):
    _, _, h, C = shard.shape
    mb = 3 * h * C * jnp.dtype(_WIRE).itemsize / 1e6
    n = _row_splits(h, ICI_US_PER_MB * mb)
    hr = h // n

    def rows_to_peers(r):
        def build(src, dst, sems, s0):
            x, y, c, chips = _chip_peers()
            me, rows = 2 * x + y, pl.ds(r * hr, hr)
            return [pltpu.make_async_remote_copy(src_ref=src.at[l, c, rows], dst_ref=dst.at[me, c, rows],
                                                 send_sem=sems.at[s0 + 2 * j], recv_sem=sems.at[s0 + 2 * j + 1],
                                                 device_id=(px, py, c), device_id_type=MESH)
                    for j, (px, py) in enumerate(chips)]
        return build

    parts = [(rows_to_peers(r), 6, ICI_US_PER_MB * mb / n) for r in range(n)]
    return _Task(shard, jax.ShapeDtypeStruct((4, 2, h, C), shard.dtype), parts, init=own)


def _gather_task2(task1):
    _, _, h, C = task1.out.shape
    mb = h * C * jnp.dtype(_WIRE).itemsize / 1e6

    def forward(j):
        def build(_, dst, sems, s0):
            x, y, c, chips = _chip_peers()
            px, py = chips[j]
            landed = dst.at[2 * px + py, c]
            return [pltpu.make_async_remote_copy(src_ref=landed, dst_ref=landed, send_sem=sems.at[s0],
                                                 recv_sem=sems.at[s0 + 1], device_id=(x, y, 1 - c),
                                                 device_id_type=MESH)]
        return build

    return _Task(task1, task1.out, [(forward(j), 2, 1.0 + D2D_US_PER_MB * mb) for j in range(3)])


def _scatter_task(G):
    mb = 3 * G.shape[1] * G.shape[2] * jnp.dtype(_WIRE).itemsize / 1e6
    n = _row_splits(G.shape[1], ICI_US_PER_MB * mb)
    rr = G.shape[1] // n

    def rows_to_peers(r):
        def build(src, dst, sems, s0):
            _, _, c, chips = _chip_peers()
            rows = pl.ds(r * rr, rr)
            return [pltpu.make_async_remote_copy(src_ref=src.at[2 * px + py, rows], dst_ref=dst.at[j, rows],
                                                 send_sem=sems.at[s0 + 2 * j], recv_sem=sems.at[s0 + 2 * j + 1],
                                                 device_id=(px, py, c), device_id_type=MESH)
                    for j, (px, py) in enumerate(chips)]
        return build

    parts = [(rows_to_peers(r), 6, ICI_US_PER_MB * mb / n) for r in range(n)]
    return _Task(G, jax.ShapeDtypeStruct((3,) + G.shape[1:], G.dtype), parts)


def _row_splits(rows, cost):
    n = 8
    while n > 1 and (rows % (16 * n) or cost / n < PART_US):
        n //= 2
    return n


CARRIER_US = {
    "f_gu": 48, "f_down": 20, "f_dact": 42, "f_dwd": 20, "f_dwgu": 42, "f_dh": 46,
    "nsa_in": 25, "nsa_cmp": 86, "nsa_slc": 122, "nsa_win": 74, "nsa_dcmp": 58, "nsa_dslc": 195, "nsa_dwin": 108,
    "nsa_dh": 26, "nsa_dwin_w": 25,
    "swa_in": 16, "swa_attn": 74, "swa_dattn": 74, "swa_dh": 17, "swa_dwin_w": 18,
    "fox_in": 34, "fox_attn": 104, "fox_dattn": 161, "fox_dh": 30, "fox_dwin_w": 30,
}
ICI_US_PER_MB = 15.0
PART_US = 12.0
D2D_US_PER_MB = 2.3
LOCAL_US_PER_MB = 1.5


class _Sched:
    def __init__(self):
        self.queue, self.credit, self.n_flush = [], 0.0, 0

    def push(self, task):
        self.queue.append(task)
        return task

    def _jobs(self, fits):
        jobs = []
        while self.queue and self.queue[0].ready():
            task, n = self.queue[0], 0
            while task.done + n < len(task.parts) and fits(task.parts[task.done + n][2]):
                n += 1
            if n:
                jobs.append(task.next_job(n))
            if task.done < len(task.parts):
                break
            self.queue.pop(0)
        return jobs

    def take(self, name):
        kind = name[2:].split("_", 1)
        key = ("f" if kind[0] in ("f1", "f2") else kind[0]) + "_" + kind[1]
        self.credit = min(self.credit, 0.0) + CARRIER_US.get(key, 0.0)

        def fits(cost):
            if self.credit < 0.5 * cost:
                return False
            self.credit -= cost
            return True

        return self._jobs(fits)

    def finish(self, task=None):
        while self.queue and (task is None or not task.complete()):
            jobs = []
            while self.queue and self.queue[0].ready():
                head = self.queue.pop(0)
                jobs.append(head.next_job(len(head.parts) - head.done))
                if head is task:
                    break
            assert jobs, "the task at the head of the queue waits for one that was never queued"
            _comm_call(jobs, name=f"exchange_{self.n_flush}")
            self.n_flush += 1


def _mm(a, b, *, ta=False, tb=False, tm, tn, b_lead=(), into=None, o_lead=(), out_dtype=F32, res=None, alpha=1.0,
        ctx=None, name):
    M = a.shape[1] if ta else a.shape[0]
    K = a.shape[0] if ta else a.shape[1]
    bk, bn = (b.shape[-1], b.shape[-2]) if tb else (b.shape[-2], b.shape[-1])
    j_lead, k_lead = "j" in b_lead, "k" in b_lead
    N = bn * (b.shape[b_lead.index("j")] if j_lead else 1)
    nk = b.shape[b_lead.index("k")] if k_lead else 1
    tk = K // nk
    assert tk == bk, (name, K, nk, bk)
    tm = min(tm, M)
    tn = bn if j_lead else min(tn, N)
    assert M % tm == 0 and N % tn == 0, (name, M, N, tm, tn)
    nb, no = len(b_lead), len(o_lead)

    def pick(lead, j, k):
        return tuple(j if t == "j" else k if t == "k" else t for t in lead)

    a_spec = pl.BlockSpec((tk, tm), lambda i, j, k: (k, i)) if ta else pl.BlockSpec((tm, tk), lambda i, j, k: (i, k))
    if tb:
        b_spec = pl.BlockSpec((None,) * nb + (tn, tk),
                              lambda i, j, k: pick(b_lead, j, k) + (0 if j_lead else j, 0 if k_lead else k))
    else:
        b_spec = pl.BlockSpec((None,) * nb + (tk, tn),
                              lambda i, j, k: pick(b_lead, j, k) + (0 if k_lead else k, 0 if j_lead else j))
    r_spec = pl.BlockSpec((tm, tn), lambda i, j, k: (i, j))
    if o_lead == ("heads",):
        o_spec = pl.BlockSpec((tn // DH, tm, DH), lambda i, j, k: (j, i, 0))
    else:
        o_spec = pl.BlockSpec((None,) * no + (tm, tn),
                              lambda i, j, k: pick(o_lead, j, k) + (i, 0 if "j" in o_lead else j))
    dn = (((0 if ta else 1,), (1 if tb else 0,)), ((), ()))
    has_res, has_into = res is not None, into is not None
    if has_into:
        out_dtype = into.dtype

    def body(*refs):
        a_ref, b_ref = refs[0], refs[1]
        r_ref = refs[2] if has_res else None
        o_ref = refs[2 + has_res + has_into]
        prod = lax.dot_general(a_ref[...].astype(_CDT), b_ref[...].astype(_CDT), dn, preferred_element_type=F32)

        def finish(acc):
            if alpha != 1.0:
                acc = acc * alpha
            if has_res:
                acc = r_ref[...] + acc
            if o_lead == ("heads",):
                for hd in range(tn // DH):
                    o_ref[hd] = acc[:, hd * DH:(hd + 1) * DH].astype(out_dtype)
            else:
                o_ref[...] = acc.astype(out_dtype)

        if nk == 1:
            finish(prod)
        else:
            acc_ref = refs[-1]
            k = pl.program_id(2)

            @pl.when(k == 0)
            def _():
                acc_ref[...] = prod

            @pl.when(k != 0)
            def _():
                acc_ref[...] += prod

            @pl.when(k == nk - 1)
            def _():
                finish(acc_ref[...])

    ins, specs = [a, b], [a_spec, b_spec]
    if has_res:
        ins.append(res)
        specs.append(r_spec)
    aliases = {}
    if has_into:
        aliases = {len(ins): 0}
        ins.append(into)
        specs.append(_ANY)
        out_shape = jax.ShapeDtypeStruct(into.shape, into.dtype)
    elif o_lead == ("j",):
        out_shape = jax.ShapeDtypeStruct((N // tn, M, tn), out_dtype)
    elif o_lead == ("heads",):
        out_shape = jax.ShapeDtypeStruct((N // DH, M, DH), out_dtype)
    else:
        assert not o_lead
        out_shape = jax.ShapeDtypeStruct((M, N), out_dtype)
    scratch = [pltpu.VMEM((tm, tn), F32)] if nk > 1 else []
    return _pcall(body, out_shape=out_shape, grid=(M // tm, N // tn, nk), in_specs=specs, out_specs=o_spec,
                  scratch_shapes=scratch, aliases=aliases, name=name, jobs=ctx.take(name) if ctx else (), ins=ins)


def _rows2d(a):
    return a.reshape(-1, a.shape[-1])


def _row_tile(rows, cols, itemsize=4, budget=2 * 1024 * 1024):
    t = rows
    while t % 2 == 0 and t * cols * itemsize > budget and (t // 2) % 8 == 0:
        t //= 2
    return t


def _addn(*xs, name):
    shape = xs[0].shape
    x2 = [_rows2d(x) for x in xs]
    R, C = x2[0].shape
    tr = _row_tile(R, C)

    def body(*refs):
        acc = refs[0][...]
        for r in refs[1:-1]:
            acc = acc + r[...]
        refs[-1][...] = acc

    spec = pl.BlockSpec((tr, C), lambda i: (i, 0))
    out = pl.pallas_call(body, out_shape=jax.ShapeDtypeStruct((R, C), F32), grid=(R // tr,),
                         in_specs=[spec] * len(x2), out_specs=spec, compiler_params=_params(), name=name)(*x2)
    return out.reshape(shape)


def _rms_fwd(x, g, *, name):
    S, D = x.shape
    tr = 256

    def body(x_ref, g_ref, h_ref):
        xv = x_ref[...]
        rstd = lax.rsqrt(jnp.mean(xv * xv, axis=-1, keepdims=True) + RMS_EPS)
        h_ref[...] = (xv * rstd * g_ref[...]).astype(_CDT)

    return pl.pallas_call(body, out_shape=jax.ShapeDtypeStruct((S, D), _CDT), grid=(S // tr,),
                          in_specs=[pl.BlockSpec((tr, D), lambda i: (i, 0)), pl.BlockSpec((1, D), lambda i: (0, 0))],
                          out_specs=pl.BlockSpec((tr, D), lambda i: (i, 0)), compiler_params=_params(),
                          name=name)(x, g.reshape(1, D))


def _rms_bwd(dh, x, g, dres, *, name):
    S, D = x.shape
    tr = 256

    def body(dh_ref, x_ref, g_ref, dres_ref, dx_ref, dg_ref):
        xv = x_ref[...]
        rstd = lax.rsqrt(jnp.mean(xv * xv, axis=-1, keepdims=True) + RMS_EPS)
        xhat = xv * rstd
        dhv = dh_ref[...]
        dxhat = dhv * g_ref[...]
        dx_ref[...] = dres_ref[...] + rstd * (dxhat - xhat * jnp.mean(dxhat * xhat, axis=-1, keepdims=True))

        @pl.when(pl.program_id(0) == 0)
        def _():
            dg_ref[...] = jnp.zeros_like(dg_ref)

        dg_ref[...] += jnp.sum(dhv * xhat, axis=0, keepdims=True)

    row = pl.BlockSpec((tr, D), lambda i: (i, 0))
    vec = pl.BlockSpec((1, D), lambda i: (0, 0))
    dx, dg = pl.pallas_call(body, out_shape=(jax.ShapeDtypeStruct((S, D), F32), jax.ShapeDtypeStruct((1, D), F32)),
                            grid=(S // tr,), in_specs=[row, row, vec, row], out_specs=(row, vec),
                            compiler_params=_params(), name=name)(dh, x, g.reshape(1, D), dres)
    return dx, dg.reshape(D)


def _loss_head(x, g, tgt, *, name):
    S, D = x.shape
    tr = 256

    def body(x_ref, g_ref, t_ref, loss_ref, dx_ref, dg_ref):
        xv = x_ref[...]
        rstd = lax.rsqrt(jnp.mean(xv * xv, axis=-1, keepdims=True) + RMS_EPS)
        xhat = xv * rstd
        err = xhat * g_ref[...] - t_ref[...]
        part = 0.5 * jnp.sum(jnp.mean(err * err, axis=-1, keepdims=True), axis=0, keepdims=True)
        dy = err * (1.0 / D)
        dxhat = dy * g_ref[...]
        dx_ref[...] = rstd * (dxhat - xhat * jnp.mean(dxhat * xhat, axis=-1, keepdims=True))

        @pl.when(pl.program_id(0) == 0)
        def _():
            dg_ref[...] = jnp.zeros_like(dg_ref)
            loss_ref[...] = jnp.zeros_like(loss_ref)

        dg_ref[...] += jnp.sum(dy * xhat, axis=0, keepdims=True)
        loss_ref[...] += jnp.broadcast_to(part, loss_ref.shape)

    row = pl.BlockSpec((tr, D), lambda i: (i, 0))
    vec = pl.BlockSpec((1, D), lambda i: (0, 0))
    loss, dx, dg = pl.pallas_call(
        body, out_shape=(jax.ShapeDtypeStruct((8, LANES), F32), jax.ShapeDtypeStruct((S, D), F32),
                         jax.ShapeDtypeStruct((1, D), F32)),
        grid=(S // tr,), in_specs=[row, vec, row], out_specs=(pl.BlockSpec((8, LANES), lambda i: (0, 0)), row, vec),
        compiler_params=_params(), name=name)(x, g.reshape(1, D), tgt)
    return loss[0, 0], dx, dg.reshape(D)


def _rope_tables(S):
    inv = 10000.0 ** (-jnp.arange(0, DH, 2, dtype=F32) / DH)
    ang = jnp.arange(S, dtype=F32)[:, None] * inv[None, :]
    cos, sin = jnp.cos(ang), jnp.sin(ang)
    return jnp.concatenate([cos, cos], -1), jnp.concatenate([-sin, sin], -1)


def _swap_matrix():
    p = np.zeros((DH, DH), np.float32)
    for j in range(DH // 2):
        p[j + DH // 2, j] = 1.0
        p[j, j + DH // 2] = 1.0
    return jnp.asarray(p, jnp.bfloat16)


def _rope(x, cc, ss, rotated, *, name):
    n, S, _ = x.shape
    assert len(rotated) == n
    starts = [i for i in range(n) if rotated[i] and (i == 0 or not rotated[i - 1])]
    tab = pl.BlockSpec((S, DH), lambda i: (0, 0))
    for k, first in enumerate(starts):
        count = next((i for i in range(first, n) if not rotated[i]), n) - first

        def body(x_ref, c_ref, s_ref, p_ref, o_ref):
            xv = x_ref[0]
            o_ref[0] = xv * c_ref[...] + _dot_exact(xv, p_ref[...]) * s_ref[...]

        blk = pl.BlockSpec((1, S, DH), lambda i, first=first: (first + i, 0, 0))
        x = pl.pallas_call(body, out_shape=jax.ShapeDtypeStruct(x.shape, F32), grid=(count,),
                           in_specs=[blk, tab, tab, pl.BlockSpec((DH, DH), lambda i: (0, 0))], out_specs=blk,
                           input_output_aliases={0: 0}, compiler_params=_params(),
                           name=f"{name}{k}")(x, cc, ss, _swap_matrix())
    return x


def _key_range(kind, i, window, Sk):
    if kind == "cmp":
        return 0, Sk
    hi = (i + 1) * BQ
    if kind == "band":
        return max(0, i * BQ - window), hi
    return 0, hi


def _attn_mask(kind, i, lo, hi, window):
    shape = (BQ, hi - lo)
    qpos = i * BQ + lax.broadcasted_iota(jnp.int32, shape, 0)
    kpos = lo + lax.broadcasted_iota(jnp.int32, shape, 1)
    if kind == "cmp":
        return kpos * 16 + 31 <= qpos
    mask = kpos <= qpos
    if kind == "band":
        mask = mask & (qpos - kpos < window)
    return mask


def _sel_expand(n_slc, n_keys):
    shape = (n_slc, n_keys)
    j = lax.broadcasted_iota(jnp.int32, shape, 0)
    key = lax.broadcasted_iota(jnp.int32, shape, 1)
    return (jnp.right_shift(key, 6) == j).astype(_CDT)


def _eye():
    return lax.broadcasted_iota(jnp.int32, (BQ, BQ), 0) == lax.broadcasted_iota(jnp.int32, (BQ, BQ), 1)


def _to_col(row):
    return jnp.sum(jnp.where(_eye(), row, 0.0), axis=1, keepdims=True)


def _to_row(col):
    return jnp.sum(jnp.where(_eye(), col, 0.0), axis=0, keepdims=True)


def _scores(kind, i, lo, hi, window, qb, kb, crow_ref, sel_ref):
    s = _dot_nt(qb, kb)
    if crow_ref is not None:
        s = s + _to_col(crow_ref[0, :, i * BQ:(i + 1) * BQ]) - crow_ref[0, :, lo:hi]
    mask = _attn_mask(kind, i, lo, hi, window)
    if sel_ref is not None:
        mask = mask & (sel_ref[0, i * BQ:(i + 1) * BQ, lo:hi].astype(F32) > 0.5)
    return jnp.where(mask, s, NEG), mask


def _attn_fwd(q, k, v, *, G, R, heads=(0, 0, 0), kind, window=0, bias=None, sinks=None, selT=None, ovT=None, ctx=None,
              name):
    S = q.shape[1]
    Sk = k.shape[1]
    q0, k0, v0 = heads
    nq = S // BQ
    n_slc = S // 64
    has_bias, has_sink, has_sel, is_cmp = bias is not None, sinks is not None, selT is not None, kind == "cmp"

    def body(*refs):
        it = iter(refs)
        q_ref, k_ref, v_ref = next(it), next(it), next(it)
        crow_ref = next(it) if has_bias else None
        sink_ref = next(it) if has_sink else None
        sel_ref = next(it) if has_sel else None
        ov_ref = next(it) if is_cmp else None
        o_ref, lse_ref = next(it), next(it)
        selo_ref, imp_ref = (next(it), next(it)) if is_cmp else (None, None)
        r = pl.program_id(1)
        for i in range(nq):
            lo, hi = _key_range(kind, i, window, Sk)
            rows = slice(i * BQ, (i + 1) * BQ)
            qb = (q_ref[rows, :] * SCALE).astype(_CDT)
            kb = k_ref[0, lo:hi, :].astype(_CDT)
            vb = v_ref[0, lo:hi, :].astype(_CDT)
            s, mask = _scores(kind, i, lo, hi, window, qb, kb, crow_ref, sel_ref)
            m = jnp.max(s, axis=-1, keepdims=True)
            if has_sink:
                sk = sink_ref[0, 0, 0:1, 0:1]
                m = jnp.maximum(m, sk)
            e = jnp.exp(s - m)
            if is_cmp:
                e = jnp.where(mask, e, 0.0)
            l = jnp.sum(e, axis=-1, keepdims=True)
            if has_sink:
                l = l + jnp.exp(sk - m)
            if is_cmp:
                l = jnp.where(l > 0.0, l, 1.0)
            p = e * (1.0 / l)
            o_ref[0, 0, rows, :] = _dot(p.astype(_CDT), vb)
            lse_ref[0, 0, :, rows] = _to_row(m + jnp.log(l))
            if is_cmp:
                part = _dot_nt(ov_ref[...].astype(_CDT), p.astype(_CDT))

                @pl.when(r == 0)
                def _():
                    imp_ref[:, rows] = part

                @pl.when(r != 0)
                def _():
                    imp_ref[:, rows] += part

        if is_cmp:
            @pl.when(r == R - 1)
            def _():
                shape = (n_slc, S)
                j = lax.broadcasted_iota(jnp.int32, shape, 0)
                tb = jnp.right_shift(lax.broadcasted_iota(jnp.int32, shape, 1), 6)
                forced = (j == 0) | (j == tb) | (j == tb - 1)
                imp = jnp.where(j > tb, NEG, jnp.where(forced, NSA_BONUS, imp_ref[0:n_slc, :]))
                imp_ref[0:n_slc, :] = imp
                cnt = jnp.zeros(shape, F32)
                for jp in range(n_slc):
                    row = imp_ref[jp:jp + 1, :]
                    ahead = (row > imp) | ((row == imp) & (jp < j))
                    cnt = cnt + ahead.astype(F32)
                imp_ref[0:n_slc, :] = (cnt < float(min(NSA_TOPK, n_slc))).astype(F32)
                expand = _sel_expand(n_slc, S)
                for i in range(nq):
                    rows = slice(i * BQ, (i + 1) * BQ)
                    chosen = _dot_tn(imp_ref[0:n_slc, rows].astype(_CDT), expand)
                    selo_ref[0, rows, :] = chosen.astype(jnp.bfloat16)

    qspec = pl.BlockSpec((1, 1, S, DH), lambda g, r: (g, r, 0, 0))
    ins = [q, k, v]
    specs = [pl.BlockSpec((None, S, DH), lambda g, r: (q0 + g * R + r, 0, 0)),
             pl.BlockSpec((1, Sk, DH), lambda g, r: (k0 + g, 0, 0)), pl.BlockSpec((1, Sk, DH), lambda g, r: (v0 + g, 0, 0))]
    if has_bias:
        ins.append(bias)
        specs.append(pl.BlockSpec((1, 1, S), lambda g, r: (g, 0, 0)))
    if has_sink:
        ins.append(sinks)
        specs.append(pl.BlockSpec((1, 1, 8, LANES), lambda g, r: (g, r, 0, 0)))
    if has_sel:
        ins.append(selT)
        specs.append(pl.BlockSpec((1, S, S), lambda g, r: (g, 0, 0)))
    if is_cmp:
        ins.append(ovT)
        specs.append(pl.BlockSpec((LANES, Sk), lambda g, r: (0, 0)))
    outs = [jax.ShapeDtypeStruct((G, R, S, DH), F32), jax.ShapeDtypeStruct((G, R, 1, S), F32)]
    ospecs = [qspec, pl.BlockSpec((1, 1, 1, S), lambda g, r: (g, r, 0, 0))]
    scratch = []
    if is_cmp:
        outs.append(jax.ShapeDtypeStruct((G, S, S), jnp.bfloat16))
        ospecs.append(pl.BlockSpec((1, S, S), lambda g, r: (g, 0, 0)))
        scratch.append(pltpu.VMEM((LANES, S), F32))
    return _pcall(body, out_shape=tuple(outs), grid=(G, R), in_specs=specs, out_specs=tuple(ospecs),
                  scratch_shapes=scratch, name=name, jobs=ctx.take(name) if ctx else (), ins=ins)


def _attn_bwd(q, k, v, o, do, lse, *, heads=(0, 0, 0), kind, window=0, bias=None, sinks=None, selT=None, gate=None,
              ctx=None, name):
    G, R, S, _ = o.shape
    Sk = k.shape[1]
    q0, k0, v0 = heads
    nq = S // BQ
    has_bias, has_sink, has_sel, has_gate = bias is not None, sinks is not None, selT is not None, gate is not None

    def body(*refs):
        it = iter(refs)
        q_ref, k_ref, v_ref, o_ref, do_ref, lse_ref = (next(it) for _ in range(6))
        crow_ref = next(it) if has_bias else None
        sink_ref = next(it) if has_sink else None
        sel_ref = next(it) if has_sel else None
        z_ref = next(it) if has_gate else None
        dq_ref, dk_ref, dv_ref = next(it), next(it), next(it)
        dc_ref = next(it) if has_bias else None
        dsink_ref = next(it) if has_sink else None
        dz_ref = next(it) if has_gate else None
        r = pl.program_id(1)

        @pl.when(r == 0)
        def _():
            dk_ref[...] = jnp.zeros_like(dk_ref)
            dv_ref[...] = jnp.zeros_like(dv_ref)
            if has_bias:
                dc_ref[...] = jnp.zeros_like(dc_ref)

        dsink = jnp.zeros((1, 1), F32)
        for i in range(nq):
            lo, hi = _key_range(kind, i, window, Sk)
            rows = slice(i * BQ, (i + 1) * BQ)
            qb = (q_ref[rows, :] * SCALE).astype(_CDT)
            kb = k_ref[0, lo:hi, :].astype(_CDT)
            vb = v_ref[0, lo:hi, :].astype(_CDT)
            s, mask = _scores(kind, i, lo, hi, window, qb, kb, crow_ref, sel_ref)
            lse_i = _to_col(lse_ref[0, 0, :, rows])
            p = jnp.where(mask, jnp.exp(s - lse_i), 0.0)
            dob = do_ref[0, 0, rows, :]
            if has_gate:
                od = jnp.sum(o_ref[0, 0, rows, :] * dob, axis=-1, keepdims=True)
                sg = jax.nn.sigmoid(_to_col(z_ref[0, 0, :, rows]))
                dob = dob * sg
                dz_ref[0, 0, :, rows] = _to_row(od * sg * (1.0 - sg))
            dob = dob.astype(_CDT)
            dp = _dot_nt(dob, vb)
            delta = jnp.sum(p * dp, axis=-1, keepdims=True)
            ds = p * (dp - delta)
            dsb = ds.astype(_CDT)
            dq_ref[0, 0, rows, :] = _dot(dsb, kb) * SCALE
            dk_ref[0, lo:hi, :] += _dot_tn(dsb, qb)
            dv_ref[0, lo:hi, :] += _dot_tn(p.astype(_CDT), dob)
            if has_bias:
                dc_ref[0, :, rows] += _to_row(jnp.sum(ds, axis=-1, keepdims=True))
                dc_ref[0, :, lo:hi] -= jnp.sum(ds, axis=0, keepdims=True)
            if has_sink:
                sk = sink_ref[0, 0, 0:1, 0:1]
                dsink = dsink - jnp.sum(jnp.exp(sk - lse_i) * delta, axis=0, keepdims=True)
        if has_sink:
            dsink_ref[0, 0] = jnp.broadcast_to(dsink, (8, LANES))

    qspec = pl.BlockSpec((1, 1, S, DH), lambda g, r: (g, r, 0, 0))
    cspec = pl.BlockSpec((1, 1, 1, S), lambda g, r: (g, r, 0, 0))
    kspec = pl.BlockSpec((1, Sk, DH), lambda g, r: (g, 0, 0))
    ins = [q, k, v, o, do, lse]
    specs = [pl.BlockSpec((None, S, DH), lambda g, r: (q0 + g * R + r, 0, 0)),
             pl.BlockSpec((1, Sk, DH), lambda g, r: (k0 + g, 0, 0)), pl.BlockSpec((1, Sk, DH), lambda g, r: (v0 + g, 0, 0)),
             qspec, qspec, cspec]
    if has_bias:
        ins.append(bias)
        specs.append(pl.BlockSpec((1, 1, S), lambda g, r: (g, 0, 0)))
    if has_sink:
        ins.append(sinks)
        specs.append(pl.BlockSpec((1, 1, 8, LANES), lambda g, r: (g, r, 0, 0)))
    if has_sel:
        ins.append(selT)
        specs.append(pl.BlockSpec((1, S, S), lambda g, r: (g, 0, 0)))
    if has_gate:
        ins.append(gate)
        specs.append(cspec)
    names = ["dq", "dk", "dv"]
    outs = [jax.ShapeDtypeStruct((G, R, S, DH), F32), jax.ShapeDtypeStruct((G, Sk, DH), F32),
            jax.ShapeDtypeStruct((G, Sk, DH), F32)]
    ospecs = [qspec, kspec, kspec]
    if has_bias:
        assert R == 1
        names.append("dc")
        outs.append(jax.ShapeDtypeStruct((G, 1, S), F32))
        ospecs.append(pl.BlockSpec((1, 1, S), lambda g, r: (g, 0, 0)))
    if has_sink:
        names.append("dsink")
        outs.append(jax.ShapeDtypeStruct((G, R, 8, LANES), F32))
        ospecs.append(pl.BlockSpec((1, 1, 8, LANES), lambda g, r: (g, r, 0, 0)))
    if has_gate:
        names.append("dz")
        outs.append(jax.ShapeDtypeStruct((G, R, 1, S), F32))
        ospecs.append(cspec)
    res = _pcall(body, out_shape=tuple(outs), grid=(G, R), in_specs=specs, out_specs=tuple(ospecs), name=name,
                 jobs=ctx.take(name) if ctx else (), ins=ins)
    return dict(zip(names, res))


def _combine(o0, o1, o2, z, *, name):
    H, S, _ = o0.shape

    def body(o0_ref, o1_ref, o2_ref, z_ref, o_ref):
        for i in range(S // BQ):
            rows = slice(i * BQ, (i + 1) * BQ)
            acc = jax.nn.sigmoid(_to_col(z_ref[0, 0, :, rows])) * o0_ref[0, rows, :]
            acc = acc + jax.nn.sigmoid(_to_col(z_ref[1, 0, :, rows])) * o1_ref[0, rows, :]
            acc = acc + jax.nn.sigmoid(_to_col(z_ref[2, 0, :, rows])) * o2_ref[0, rows, :]
            o_ref[0, rows, :] = acc

    blk = pl.BlockSpec((1, S, DH), lambda h: (h, 0, 0))
    return pl.pallas_call(body, out_shape=jax.ShapeDtypeStruct((H, S, DH), F32), grid=(H,),
                          in_specs=[blk, blk, blk, pl.BlockSpec((3, 1, 1, S), lambda h: (0, h, 0, 0))], out_specs=blk,
                          compiler_params=_params(), name=name)(o0, o1, o2, z)


_GC = math.sqrt(2.0 / math.pi)


def _gelu(x):
    return 0.5 * x * (1.0 + jnp.tanh(_GC * (x + 0.044715 * x * x * x)))


def _gelu_grad(x):
    t = jnp.tanh(_GC * (x + 0.044715 * x * x * x))
    return 0.5 * (1.0 + t) + 0.5 * x * (1.0 - t * t) * _GC * (1.0 + 3.0 * 0.044715 * x * x)


def _make_xb(k):
    G, S, _ = k.shape
    chunks = k.reshape(G, S // 16, 16 * DH)
    shift = jnp.concatenate([chunks[:, 1:], jnp.zeros((G, 1, 16 * DH), k.dtype)], axis=1)
    return jnp.concatenate([chunks, shift], axis=-1)


def _unmake_xb(dxb, *, name):
    G, n, _ = dxb.shape
    a = dxb[..., :16 * DH]
    b = jnp.concatenate([jnp.zeros((G, 1, 16 * DH), F32), dxb[:, :-1, 16 * DH:]], axis=1)
    return _addn(a, b, name=name).reshape(G, n * 16, DH)


def _compress_fwd(xb, pe, w1, w2, *, name):
    G, n, W = xb.shape
    Hc = w1.shape[1]

    def body(xb_ref, pe_ref, w1_ref, w2_ref, kc_ref, hid_ref):
        xv = (xb_ref[0] + pe_ref[...]).astype(_CDT)
        hid = _dot(xv, w1_ref[...].astype(_CDT))
        hid_ref[0] = hid
        kc_ref[0] = _dot(_gelu(hid).astype(_CDT), w2_ref[...].astype(_CDT))

    return pl.pallas_call(
        body, out_shape=(jax.ShapeDtypeStruct((G, n, DH), F32), jax.ShapeDtypeStruct((G, n, Hc), F32)), grid=(G,),
        in_specs=[pl.BlockSpec((1, n, W), lambda g: (g, 0, 0)), pl.BlockSpec((1, W), lambda g: (0, 0)),
                  pl.BlockSpec((W, Hc), lambda g: (0, 0)), pl.BlockSpec((Hc, DH), lambda g: (0, 0))],
        out_specs=(pl.BlockSpec((1, n, DH), lambda g: (g, 0, 0)), pl.BlockSpec((1, n, Hc), lambda g: (g, 0, 0))),
        compiler_params=_params(), name=name)(xb, pe.reshape(1, W), w1, w2)


def _compress_bwd(xb, pe, w1, w2, hid, dkc, *, name):
    G, n, W = xb.shape
    Hc = w1.shape[1]

    def body(xb_ref, pe_ref, w1_ref, w2_ref, hid_ref, dkc_ref, dxb_ref, dw1_ref, dw2_ref, dpe_ref):
        @pl.when(pl.program_id(0) == 0)
        def _():
            dw1_ref[...] = jnp.zeros_like(dw1_ref)
            dw2_ref[...] = jnp.zeros_like(dw2_ref)
            dpe_ref[...] = jnp.zeros_like(dpe_ref)

        xv = (xb_ref[0] + pe_ref[...]).astype(_CDT)
        hid = hid_ref[0]
        dk = dkc_ref[0].astype(_CDT)
        dact = _dot_nt(dk, w2_ref[...].astype(_CDT))
        dhid = (dact * _gelu_grad(hid)).astype(_CDT)
        dw2_ref[...] += _dot_tn(_gelu(hid).astype(_CDT), dk)
        dxb = _dot_nt(dhid, w1_ref[...].astype(_CDT))
        dxb_ref[0] = dxb
        dw1_ref[...] += _dot_tn(xv, dhid)
        dpe_ref[...] += jnp.sum(dxb, axis=0, keepdims=True)

    return pl.pallas_call(
        body, out_shape=(jax.ShapeDtypeStruct((G, n, W), F32), jax.ShapeDtypeStruct((W, Hc), F32),
                         jax.ShapeDtypeStruct((Hc, DH), F32), jax.ShapeDtypeStruct((1, W), F32)), grid=(G,),
        in_specs=[pl.BlockSpec((1, n, W), lambda g: (g, 0, 0)), pl.BlockSpec((1, W), lambda g: (0, 0)),
                  pl.BlockSpec((W, Hc), lambda g: (0, 0)), pl.BlockSpec((Hc, DH), lambda g: (0, 0)),
                  pl.BlockSpec((1, n, Hc), lambda g: (g, 0, 0)), pl.BlockSpec((1, n, DH), lambda g: (g, 0, 0))],
        out_specs=(pl.BlockSpec((1, n, W), lambda g: (g, 0, 0)), pl.BlockSpec((W, Hc), lambda g: (0, 0)),
                   pl.BlockSpec((Hc, DH), lambda g: (0, 0)), pl.BlockSpec((1, W), lambda g: (0, 0))),
        compiler_params=_params(), name=name)(xb, pe.reshape(1, W), w1, w2, hid, dkc)


def _overlap_T(S):
    n_cmp, n_slc = S // 16 - 1, S // 64
    cs = np.arange(n_cmp) * 16
    ce = cs + 32
    ss = np.arange(n_slc) * 64
    se = ss + 64
    ov = np.clip(np.minimum(ce[:, None], se[None, :]) - np.maximum(cs[:, None], ss[None, :]), 0, None) / 32.0
    out = np.zeros((LANES, S // 16), np.float32)
    out[:n_slc, :n_cmp] = ov.T
    return jnp.asarray(out)


def _tri(n, upper):
    r = lax.broadcasted_iota(jnp.int32, (n, n), 0)
    c = lax.broadcasted_iota(jnp.int32, (n, n), 1)
    return ((c >= r) if upper else (c <= r)).astype(jnp.bfloat16)


def _fox_gate_fwd(zf, b, *, name):
    S, H = zf.shape
    nb = S // BQ

    def body(z_ref, b_ref, c_ref):
        tri = _tri(BQ, False)
        carry = jnp.zeros((1, H), F32)
        for i in range(nb):
            z = z_ref[i * BQ:(i + 1) * BQ, :] + b_ref[...]
            lf = jnp.minimum(z, 0.0) - jnp.log(1.0 + jnp.exp(-jnp.abs(z)))
            c_ref[i * BQ:(i + 1) * BQ, :] = _dot_exact_left(tri, lf) + carry
            carry = carry + jnp.sum(lf, axis=0, keepdims=True)

    return pl.pallas_call(body, out_shape=jax.ShapeDtypeStruct((S, H), F32), compiler_params=_params(),
                          name=name)(zf, b)


def _fox_gate_bwd(zf, b, dc, *, name):
    S, H = zf.shape
    nb = S // BQ

    def body(z_ref, b_ref, dc_ref, dz_ref, db_ref):
        tri = _tri(BQ, True)
        carry = jnp.zeros((1, H), F32)
        db = jnp.zeros((1, H), F32)
        for i in reversed(range(nb)):
            rows = slice(i * BQ, (i + 1) * BQ)
            dcb = dc_ref[rows, :]
            dlf = _dot_exact_left(tri, dcb) + carry
            carry = carry + jnp.sum(dcb, axis=0, keepdims=True)
            z = z_ref[rows, :] + b_ref[...]
            dz = dlf * jax.nn.sigmoid(-z)
            dz_ref[rows, :] = dz
            db = db + jnp.sum(dz, axis=0, keepdims=True)
        db_ref[...] = db

    return pl.pallas_call(body, out_shape=(jax.ShapeDtypeStruct((S, H), F32), jax.ShapeDtypeStruct((1, H), F32)),
                          compiler_params=_params(), name=name)(zf, b, dc)


def _from_heads(a):
    return a.transpose(1, 0, 2).reshape(a.shape[1], -1).astype(_CDT)


def _pad_lanes(a):
    return jnp.pad(a, ((0, 0), (0, LANES - a.shape[1])))


FFN_TM = 512


def _ffn_up(h, w4, *, ctx, name):
    S, D = h.shape
    C = w4.shape[2]

    def body(h_ref, wg_ref, wu_ref, gu_ref, a_ref):
        hv = h_ref[...].astype(_CDT)
        g = _dot(hv, wg_ref[...].astype(_CDT))
        u = _dot(hv, wu_ref[...].astype(_CDT))
        gu_ref[0] = g
        gu_ref[1] = u
        a_ref[...] = (g * jax.nn.sigmoid(g) * u).astype(_CDT)

    return _pcall(
        body, out_shape=(jax.ShapeDtypeStruct((2, S, 2 * C), F32), jax.ShapeDtypeStruct((S, 2 * C), _CDT)),
        grid=(S // FFN_TM, 2),
        in_specs=[pl.BlockSpec((FFN_TM, D), lambda i, j: (i, 0)), pl.BlockSpec((None, D, C), lambda i, j: (j, 0, 0)),
                  pl.BlockSpec((None, D, C), lambda i, j: (j + 2, 0, 0))],
        out_specs=(pl.BlockSpec((2, FFN_TM, C), lambda i, j: (0, i, j)), pl.BlockSpec((FFN_TM, C), lambda i, j: (i, j))),
        name=name, jobs=ctx.take(name) if ctx else (), ins=[h, w4, w4])


def _ffn_dact(dxo, wd, gu, *, ctx, name):
    S, D = dxo.shape
    C = gu.shape[2] // 2

    def body(dx_ref, wd_ref, gu_ref, d_ref):
        da = _dot_nt(dx_ref[...].astype(_CDT), wd_ref[...].astype(_CDT)) * 0.5
        g, u = gu_ref[0], gu_ref[1]
        sg = jax.nn.sigmoid(g)
        silu = g * sg
        d_ref[0] = (da * u * (sg + silu * (1.0 - sg))).astype(_CDT)
        d_ref[1] = (da * silu).astype(_CDT)

    blk = pl.BlockSpec((2, FFN_TM, C), lambda i, j: (0, i, j))
    return _pcall(body, out_shape=jax.ShapeDtypeStruct(gu.shape, _CDT), grid=(S // FFN_TM, 2),
                  in_specs=[pl.BlockSpec((FFN_TM, D), lambda i, j: (i, 0)), pl.BlockSpec((C, D), lambda i, j: (j, 0)), blk],
                  out_specs=blk, name=name, jobs=ctx.take(name) if ctx else (), ins=[dxo, wd, gu])


def _ffn_dwgu(h, dgu, *, ctx, name):
    S, D = h.shape
    C = dgu.shape[2] // 2
    tm = 512

    def body(h_ref, d_ref, o_ref):
        o_ref[...] = _dot_tn(h_ref[...].astype(_CDT), d_ref[...].astype(_CDT)).astype(_WIRE)

    return _pcall(body, out_shape=jax.ShapeDtypeStruct((4, D, C), _WIRE), grid=(D // tm, 4),
                  in_specs=[pl.BlockSpec((S, tm), lambda i, j: (0, i)),
                            pl.BlockSpec((None, S, C), lambda i, j: (j // 2, 0, j % 2))],
                  out_specs=pl.BlockSpec((None, tm, C), lambda i, j: (j, i, 0)), name=name,
                  jobs=ctx.take(name) if ctx else (), ins=[h, dgu])


def _ffn_dh(dgu, w4, *, ctx, name):
    _, S, F2 = dgu.shape
    C, D = F2 // 2, w4.shape[1]

    def body(d_ref, w_ref, o_ref, acc_ref):
        k = pl.program_id(1)
        prod = _dot_nt(d_ref[...].astype(_CDT), w_ref[...].astype(_CDT))

        @pl.when(k == 0)
        def _():
            acc_ref[...] = prod

        @pl.when(k != 0)
        def _():
            acc_ref[...] += prod

        @pl.when(k == 3)
        def _():
            o_ref[...] = acc_ref[...]

    return _pcall(body, out_shape=jax.ShapeDtypeStruct((S, D), F32), grid=(S // FFN_TM, 4),
                  in_specs=[pl.BlockSpec((None, FFN_TM, C), lambda i, k: (k // 2, i, k % 2)),
                            pl.BlockSpec((None, D, C), lambda i, k: (k, 0, 0))],
                  out_specs=pl.BlockSpec((FFN_TM, D), lambda i, k: (i, 0)), scratch_shapes=[pltpu.VMEM((FFN_TM, D), F32)],
                  name=name, jobs=ctx.take(name) if ctx else (), ins=[dgu, w4])


def _ffn_fwd(x, g, P, l, tag):
    h = _rms_fwd(x, g, name=tag + "_rms")
    gu, a = _ffn_up(h, P.weight("gu", l), ctx=P.ctx, name=tag + "_gu")
    xo = _mm(a, P.weight("down", l), tm=512, tn=1024, res=x, alpha=0.5, ctx=P.ctx, name=tag + "_down")
    return xo, (x, h, gu, a)


def _ffn_bwd(dxo, saved, g, P, l, tag):
    x, h, gu, a = saved
    dgu = _ffn_dact(dxo, P.weight("down", l), gu, ctx=P.ctx, name=tag + "_dact")
    P.grad("gu", l, _ffn_dwgu(h, dgu, ctx=P.ctx, name=tag + "_dwgu"))
    dwd = _mm(a, dxo, ta=True, tm=1408, tn=1024, alpha=0.5, out_dtype=_WIRE, ctx=P.ctx, name=tag + "_dwd")
    P.grad("down", l, dwd.reshape(4, -1, D_MODEL))
    dh = _ffn_dh(dgu, P.weight("gu", l), ctx=P.ctx, name=tag + "_dh")
    dx, dg = _rms_bwd(dh, x, g, dxo, name=tag + "_drms")
    return dx, dg


def _nsa_fwd(x, g, w, cc, ss, tag):
    S = x.shape[0]
    h = _rms_fwd(x, g, name=tag + "_rms")
    hm = _mm(h, w["w_in"], tm=512, tn=896, o_lead=("heads",), ctx=w["ctx"], name=tag + "_in")
    roped = _rope(hm, cc, ss, NSA_ROTATED, name=tag + "_rope")
    z = roped[40, :, :3 * N_HEADS].reshape(S, 3, N_HEADS).transpose(1, 2, 0)[:, :, None, :]
    xbk, xbv = _make_xb(roped[16:20]), _make_xb(roped[20:24])
    kc, hidk = _compress_fwd(xbk, w["ck_pe"], w["ck_w1"], w["ck_w2"], name=tag + "_ck")
    vc, hidv = _compress_fwd(xbv, w["cv_pe"], w["cv_w1"], w["cv_w2"], name=tag + "_cv")
    ctx, GR = w["ctx"], dict(G=NSA_G, R=NSA_R)
    o0, lse0, selT = _attn_fwd(roped, kc, vc, kind="cmp", ovT=_overlap_T(S), ctx=ctx, name=tag + "_cmp", **GR)
    o1, lse1 = _attn_fwd(roped, roped, roped, heads=(0, 24, 28), kind="sel", selT=selT, ctx=ctx, name=tag + "_slc", **GR)
    o2, lse2 = _attn_fwd(roped, roped, roped, heads=(0, 32, 36), kind="band", window=NSA_WINDOW, ctx=ctx,
                         name=tag + "_win", **GR)
    o = _combine(o0.reshape(N_HEADS, S, DH), o1.reshape(N_HEADS, S, DH), o2.reshape(N_HEADS, S, DH), z,
                 name=tag + "_mix")
    of = _from_heads(o)
    xo = _mm(of, w["out"], tm=512, tn=1024, res=x, ctx=w["ctx"], name=tag + "_out")
    saved = (x, h, roped, z, xbk, xbv, hidk, hidv, kc, vc, (o0, o1, o2), (lse0, lse1, lse2), selT, of)
    return xo, saved


def _out_bwd(dxo, of, w, tag):
    dof = _mm(dxo, w["out"], tb=True, tm=512, tn=1024, o_lead=("heads",), ctx=w["ctx"], name=tag + "_dof")
    dw = _mm(of, dxo, ta=True, tm=512, tn=1024, out_dtype=_WIRE, ctx=w["ctx"], name=tag + "_dwout")
    w["P"].grad("out", w["out_l"], dw.reshape(4, -1, D_MODEL))
    return dof


def _split_cols(dw, n_in):
    cs = n_in // 4
    return dw[:, :n_in].reshape(D_MODEL, 4, cs).transpose(1, 0, 2).astype(_WIRE)


def _nsa_bwd(dxo, saved, g, w, cc, ss, tag):
    x, h, roped, z, xbk, xbv, hidk, hidv, kc, vc, os_, lses, selT, of = saved
    S = x.shape[0]
    do = _out_bwd(dxo, of, w, tag).reshape(NSA_G, NSA_R, S, DH)
    zg = z.reshape(3, NSA_G, NSA_R, 1, S)
    ctx = w["ctx"]
    b0 = _attn_bwd(roped, kc, vc, os_[0], do, lses[0], kind="cmp", gate=zg[0], ctx=ctx, name=tag + "_dcmp")
    b1 = _attn_bwd(roped, roped, roped, os_[1], do, lses[1], heads=(0, 24, 28), kind="sel", selT=selT, gate=zg[1],
                   ctx=ctx, name=tag + "_dslc")
    b2 = _attn_bwd(roped, roped, roped, os_[2], do, lses[2], heads=(0, 32, 36), kind="band", window=NSA_WINDOW,
                   gate=zg[2], ctx=ctx, name=tag + "_dwin")
    dxbk, dck_w1, dck_w2, dck_pe = _compress_bwd(xbk, w["ck_pe"], w["ck_w1"], w["ck_w2"], hidk, b0["dk"],
                                                 name=tag + "_dck")
    dxbv, dcv_w1, dcv_w2, dcv_pe = _compress_bwd(xbv, w["cv_pe"], w["cv_w1"], w["cv_w2"], hidv, b0["dv"],
                                                 name=tag + "_dcv")
    dk0 = _unmake_xb(dxbk, name=tag + "_dk0")
    dv0 = _unmake_xb(dxbv, name=tag + "_dv0")
    dq = _addn(b0["dq"], b1["dq"], b2["dq"], name=tag + "_dqsum").reshape(N_HEADS, S, DH)
    dz = jnp.stack([b0["dz"], b1["dz"], b2["dz"]], axis=0).reshape(3 * N_HEADS, S).T
    dgates = jnp.pad(dz, ((0, 0), (0, 2 * DH - 3 * N_HEADS))).reshape(S, 2, DH).transpose(1, 0, 2)
    dhm = jnp.concatenate([dq, dk0, dv0, b1["dk"], b1["dv"], b2["dk"], b2["dv"], dgates], axis=0)
    dproj = _from_heads(_rope(dhm, cc, -ss, NSA_ROTATED, name=tag + "_drope"))
    dh = _mm(dproj, w["w_in"], tb=True, tm=512, tn=512, ctx=w["ctx"], name=tag + "_dh")
    dw_in = _mm(h, dproj, ta=True, tm=512, tn=896, ctx=w["ctx"], name=tag + "_dwin_w")
    dx, dg = _rms_bwd(dh, x, g, dxo, name=tag + "_drms")
    P, j = w["P"], w["j"]
    P.grad("nsa_in", j, _split_cols(dw_in, NSA_IN))
    P.grad("cw1", j, dck_w1.astype(_WIRE).reshape(4, -1, dck_w1.shape[1]))
    P.grad("cw1", 2 + j, dcv_w1.astype(_WIRE).reshape(4, -1, dcv_w1.shape[1]))
    grads = dict(ck_pe=dck_pe.reshape(32, DH), ck_w2=dck_w2, cv_pe=dcv_pe.reshape(32, DH), cv_w2=dcv_w2)
    return dx, dg, grads


def _swa_fwd(x, g, w, cc, ss, tag):
    S = x.shape[0]
    h = _rms_fwd(x, g, name=tag + "_rms")
    hm = _mm(h, w["w_in"], tm=512, tn=640, o_lead=("heads",), ctx=w["ctx"], name=tag + "_in")
    roped = _rope(hm, cc, ss, SWA_ROTATED, name=tag + "_rope")
    sinks = jnp.broadcast_to(w["sinks"].reshape(SWA_G, SWA_R, 1, 1), (SWA_G, SWA_R, 8, LANES))
    o, lse = _attn_fwd(roped, roped, roped, G=SWA_G, R=SWA_R, heads=SWA_HEADS, kind="band", window=SWA_WINDOW,
                       sinks=sinks, ctx=w["ctx"], name=tag + "_attn")
    of = _from_heads(o.reshape(N_HEADS, S, DH))
    xo = _mm(of, w["out"], tm=512, tn=1024, res=x, ctx=w["ctx"], name=tag + "_out")
    return xo, (x, h, roped, sinks, o, lse, of)


def _swa_bwd(dxo, saved, g, w, cc, ss, tag):
    x, h, roped, sinks, o, lse, of = saved
    S = x.shape[0]
    do = _out_bwd(dxo, of, w, tag).reshape(SWA_G, SWA_R, S, DH)
    b = _attn_bwd(roped, roped, roped, o, do, lse, heads=SWA_HEADS, kind="band", window=SWA_WINDOW, sinks=sinks,
                  ctx=w["ctx"], name=tag + "_dattn")
    dhm = jnp.concatenate([b["dq"].reshape(N_HEADS, S, DH), b["dk"], b["dv"]], axis=0)
    dproj = _from_heads(_rope(dhm, cc, -ss, SWA_ROTATED, name=tag + "_drope"))
    dh = _mm(dproj, w["w_in"], tb=True, tm=512, tn=512, ctx=w["ctx"], name=tag + "_dh")
    dw_in = _mm(h, dproj, ta=True, tm=512, tn=640, ctx=w["ctx"], name=tag + "_dwin_w")
    dx, dg = _rms_bwd(dh, x, g, dxo, name=tag + "_drms")
    w["P"].grad("swa_in", w["j"], _split_cols(dw_in, SWA_IN))
    return dx, dg, dict(sinks=b["dsink"][:, :, 0, 0].reshape(N_HEADS))


def _fox_fwd(x, g, w, tag):
    S = x.shape[0]
    h = _rms_fwd(x, g, name=tag + "_rms")
    hm = _mm(h, w["w_in"], tm=512, tn=640, o_lead=("heads",), ctx=w["ctx"], name=tag + "_in")
    zf = jnp.pad(hm[48], ((0, 0), (0, LANES - DH)))
    bf = _pad_lanes(w["b_f"].reshape(1, N_HEADS))
    c = _fox_gate_fwd(zf, bf, name=tag + "_gate")[:, :N_HEADS]
    bias = c.T[:, None, :]
    o, lse = _attn_fwd(hm, hm, hm, G=N_HEADS, R=1, heads=FOX_HEADS, kind="causal", bias=bias, ctx=w["ctx"],
                       name=tag + "_attn")
    of = _from_heads(o.reshape(N_HEADS, S, DH))
    xo = _mm(of, w["out"], tm=512, tn=1024, res=x, ctx=w["ctx"], name=tag + "_out")
    return xo, (x, h, hm, zf, bf, bias, o, lse, of)


def _fox_bwd(dxo, saved, g, w, tag):
    x, h, hm, zf, bf, bias, o, lse, of = saved
    S = x.shape[0]
    do = _out_bwd(dxo, of, w, tag).reshape(N_HEADS, 1, S, DH)
    b = _attn_bwd(hm, hm, hm, o, do, lse, heads=FOX_HEADS, kind="causal", bias=bias, ctx=w["ctx"], name=tag + "_dattn")
    dzf, db = _fox_gate_bwd(zf, bf, _pad_lanes(b["dc"].reshape(N_HEADS, S).T), name=tag + "_dgate")
    dgates = dzf.reshape(S, 2, DH).transpose(1, 0, 2)
    dhm = jnp.concatenate([b["dq"].reshape(N_HEADS, S, DH), b["dk"], b["dv"], dgates], axis=0)
    dproj = _from_heads(dhm)
    dh = _mm(dproj, w["w_in"], tb=True, tm=512, tn=512, ctx=w["ctx"], name=tag + "_dh")
    dw_in = _mm(h, dproj, ta=True, tm=512, tn=640, ctx=w["ctx"], name=tag + "_dwin_w")
    dx, dg = _rms_bwd(dh, x, g, dxo, name=tag + "_drms")
    w["P"].grad("fox_in", w["j"], _split_cols(dw_in, FOX_IN))
    return dx, dg, dict(b_f=db[0, :N_HEADS])


GROUPS = {
    "gu": (("ffn1_w_gu", "ffn2_w_gu"), 2),
    "down": (("ffn1_w_down", "ffn2_w_down"), 1),
    "out": (("nsa_w_out", "swa_w_out", "fox_w_out"), 1),
    "cw1": (("nsa_ck_w1", "nsa_cv_w1"), 1),
    "nsa_in": (("nsa_w_in",), 2),
    "swa_in": (("swa_w_in",), 2),
    "fox_in": (("fox_w_in",), 2),
}
OUT_SLAB = {0: 0, 3: 1, 1: 2, 2: 3}


def _pieces_in_order():
    chunks = []
    for i in range(DEPTH):
        kind, j = i % 3, i // 3
        chunks.append([("gu", i), ("down", i)])
        if kind == 0:
            chunks.append([("nsa_in", j), ("cw1", j), ("cw1", 2 + j), ("out", OUT_SLAB[i])])
        else:
            chunks.append([("swa_in" if kind == 1 else "fox_in", j), ("out", OUT_SLAB[i])])
        chunks.append([("gu", DEPTH + i), ("down", DEPTH + i)])
    return chunks


def _consumer_layout(group, F):
    _, rows, C = F.shape
    if group == "gu":
        return F
    if GROUPS[group][1] == 1:
        return F.reshape(4 * rows, C)
    w = F.transpose(1, 0, 2).reshape(rows, 4 * C)
    pad = {"nsa_in": NSA_IN_PAD, "swa_in": SWA_IN, "fox_in": FOX_IN_PAD}[group] - 4 * C
    return jnp.pad(w, ((0, 0), (0, pad)))


class _Given:
    def __init__(self, pieces, small):
        self.pieces, self.small, self.ctx, self.grads = pieces, small, None, {}

    def weight(self, group, l):
        return _consumer_layout(group, self.pieces[group, l])

    def grad(self, group, l, G):
        self.grads[group, l] = G


class _MixerWeights(dict):
    def __init__(self, P, pieces, **given):
        super().__init__(P=P, ctx=P.ctx, **given)
        self.pieces = pieces

    def __missing__(self, key):
        self[key] = self["P"].weight(*self.pieces[key])
        return self[key]


def _mixer_weights(P, i):
    kind, j = i % 3, i // 3
    out = {"out": ("out", OUT_SLAB[i])}
    if kind == 0:
        small = {k: P.small["nsa_" + k][j] for k in ("ck_pe", "ck_w2", "cv_pe", "cv_w2")}
        return _MixerWeights(P, dict(out, w_in=("nsa_in", j), ck_w1=("cw1", j), cv_w1=("cw1", 2 + j)), j=j,
                             out_l=OUT_SLAB[i], **small)
    if kind == 1:
        return _MixerWeights(P, dict(out, w_in=("swa_in", j)), j=j, out_l=OUT_SLAB[i], sinks=P.small["swa_sinks"][j])
    return _MixerWeights(P, dict(out, w_in=("fox_in", j)), j=j, out_l=OUT_SLAB[i], b_f=P.small["fox_b_f"][j])


def _local_step(x, tgt, P):
    S = x.shape[0]
    cc, ss = _rope_tables(S)
    sm = P.small
    saved = []
    for i in range(DEPTH):
        kind = i % 3
        x, s1 = _ffn_fwd(x, sm["ffn1_norm"][i], P, i, f"l{i}f1")
        mw = _mixer_weights(P, i)
        if kind == 0:
            x, s2 = _nsa_fwd(x, sm["mix_norm"][i], mw, cc, ss, f"l{i}nsa")
        elif kind == 1:
            x, s2 = _swa_fwd(x, sm["mix_norm"][i], mw, cc, ss, f"l{i}swa")
        else:
            x, s2 = _fox_fwd(x, sm["mix_norm"][i], mw, f"l{i}fox")
        x, s3 = _ffn_fwd(x, sm["ffn2_norm"][i], P, DEPTH + i, f"l{i}f2")
        saved.append((s1, mw, s2, s3))
    loss, dx, d_final = _loss_head(x, sm["final_norm"], tgt, name="loss_head")

    norms = {k: [None] * DEPTH for k in ("ffn1_norm", "mix_norm", "ffn2_norm")}
    mix = {}
    for i in reversed(range(DEPTH)):
        kind, j = i % 3, i // 3
        s1, mw, s2, s3 = saved[i]
        dx, norms["ffn2_norm"][i] = _ffn_bwd(dx, s3, sm["ffn2_norm"][i], P, DEPTH + i, f"l{i}f2")
        if kind == 0:
            dx, dg, gm = _nsa_bwd(dx, s2, sm["mix_norm"][i], mw, cc, ss, f"l{i}nsa")
            pre = "nsa_"
        elif kind == 1:
            dx, dg, gm = _swa_bwd(dx, s2, sm["mix_norm"][i], mw, cc, ss, f"l{i}swa")
            pre = "swa_"
        else:
            dx, dg, gm = _fox_bwd(dx, s2, sm["mix_norm"][i], mw, f"l{i}fox")
            pre = "fox_"
        norms["mix_norm"][i] = dg
        for k, val in gm.items():
            mix.setdefault(pre + k, {})[j] = val
        dx, norms["ffn1_norm"][i] = _ffn_bwd(dx, s1, sm["ffn1_norm"][i], P, i, f"l{i}f1")
    small = {k: jnp.stack(v, axis=0) for k, v in norms.items()}
    small.update({k: jnp.stack([d[j] for j in sorted(d)], axis=0) for k, d in mix.items()})
    small["final_norm"] = d_final
    return loss, dx, small


def _sum_slots(own, recv, me, into, row0, *, name):
    _, R, C = recv.shape
    tr = _row_tile(math.gcd(R, row0), C)
    blk0 = row0 // tr

    def body(_, g_ref, r_ref, __, o_ref):
        acc = g_ref[...].astype(F32) + r_ref[0].astype(F32)
        acc = acc + r_ref[1].astype(F32)
        o_ref[...] = acc + r_ref[2].astype(F32)

    spec = pltpu.PrefetchScalarGridSpec(
        num_scalar_prefetch=1, grid=(R // tr,),
        in_specs=[pl.BlockSpec((None, tr, C), lambda i, me_ref: (me_ref[0], i, 0)),
                  pl.BlockSpec((3, tr, C), lambda i, me_ref: (0, i, 0)), _ANY],
        out_specs=pl.BlockSpec((tr, C), lambda i, me_ref: (blk0 + i, 0)))
    return pl.pallas_call(body, grid_spec=spec, out_shape=jax.ShapeDtypeStruct(into.shape, F32),
                          input_output_aliases={3: 0}, compiler_params=_params(), name=name)(me, own, recv, into)


def _swap_sibling(parts):
    n = len(parts)

    def body(*refs):
        srcs, outs = refs[:n], refs[n:2 * n]
        send_sems, recv_sems = refs[2 * n:]
        x, y, c = lax.axis_index("x"), lax.axis_index("y"), lax.axis_index("c")
        cps = [pltpu.make_async_remote_copy(src_ref=srcs[g], dst_ref=outs[g], send_sem=send_sems.at[g],
                                            recv_sem=recv_sems.at[g], device_id=(x, y, 1 - c), device_id_type=MESH)
               for g in range(n)]
        for cp in cps:
            cp.start()
        for cp in cps:
            cp.wait()

    return pl.pallas_call(
        body, out_shape=tuple(jax.ShapeDtypeStruct(p.shape, p.dtype) for p in parts),
        in_specs=[_ANY] * n, out_specs=(_ANY,) * n,
        scratch_shapes=[pltpu.SemaphoreType.DMA((n,)), pltpu.SemaphoreType.DMA((n,))],
        compiler_params=pltpu.CompilerParams(has_side_effects=True), name="swap_core_grads")(*parts)


def _flip(coord, bit):
    return 1 - coord if bit else coord


def _allreduce_small(v):
    n, C = v.shape

    def body(v_ref, o_ref, buf, send_sems, recv_sems):
        x, y, c = lax.axis_index("x"), lax.axis_index("y"), lax.axis_index("c")
        me = 4 * x + 2 * y + c
        buf[me] = v_ref[...]
        peers = [(_flip(x, (j >> 2) & 1), _flip(y, (j >> 1) & 1), _flip(c, j & 1)) for j in range(1, 8)]
        sends = [pltpu.make_async_remote_copy(src_ref=v_ref, dst_ref=buf.at[me], send_sem=send_sems.at[j],
                                              recv_sem=recv_sems.at[j], device_id=peer, device_id_type=MESH)
                 for j, peer in enumerate(peers)]
        for cp in sends:
            cp.start()
        for j, (px, py, pc) in enumerate(peers):
            pltpu.make_async_remote_copy(src_ref=v_ref, dst_ref=buf.at[4 * px + 2 * py + pc], send_sem=send_sems.at[j],
                                         recv_sem=recv_sems.at[j], device_id=(px, py, pc),
                                         device_id_type=MESH).wait_recv()
        for cp in sends:
            cp.wait_send()
        acc = buf[0]
        for d in range(1, 8):
            acc = acc + buf[d]
        o_ref[...] = acc

    return pl.pallas_call(
        body, out_shape=jax.ShapeDtypeStruct((n, C), F32),
        in_specs=[pl.BlockSpec(memory_space=pltpu.VMEM)], out_specs=pl.BlockSpec(memory_space=pltpu.VMEM),
        scratch_shapes=[pltpu.VMEM((8, n, C), F32), pltpu.SemaphoreType.DMA((7,)), pltpu.SemaphoreType.DMA((7,))],
        compiler_params=pltpu.CompilerParams(has_side_effects=True), name="allreduce_small")(v)


def _adamw(w, m, v, gs, *, row0=0, name):
    shape = w.shape
    w3, m3, v3 = (a.reshape((-1,) + a.shape[-2:]) for a in (w, m, v))
    g2 = [_rows2d(g) for g in gs]
    L, rows, C = w3.shape
    tr = _row_tile(math.gcd(rows, row0), C, budget=1024 * 1024)
    ng = len(g2)
    blk0, nb = row0 // tr, rows // tr

    def body(*refs):
        w_ref, m_ref, v_ref = refs[:3]
        g = refs[3][...]
        for r in refs[4:3 + ng]:
            g = g + r[...]
        g_ref, d_ref, nm_ref, nv_ref = refs[3 + ng:]
        mn = B1 * m_ref[...] + (1.0 - B1) * g
        vn = B2 * v_ref[...] + (1.0 - B2) * (g * g)
        m_hat = mn / (1.0 - B1 ** STEP)
        v_hat = vn / (1.0 - B2 ** STEP)
        g_ref[...] = g
        d_ref[...] = -LR * (m_hat / (jnp.sqrt(v_hat) + EPS) + WD * w_ref[...])
        nm_ref[...] = mn
        nv_ref[...] = vn

    spec = pl.BlockSpec((None, tr, C), lambda l, i: (l, i, 0))
    gspec = pl.BlockSpec((tr, C), lambda l, i: (blk0 + l * nb + i, 0))
    outs = pl.pallas_call(body, out_shape=tuple(jax.ShapeDtypeStruct((L, rows, C), F32) for _ in range(4)),
                          grid=(L, nb), in_specs=[spec] * 3 + [gspec] * ng, out_specs=(spec,) * 4,
                          compiler_params=_params(), name=name)(w3, m3, v3, *g2)
    return tuple(o.reshape(shape) for o in outs)


def _small_layout(shapes):
    offs, off = {}, 0
    for k in REPLICATED:
        n = int(np.prod(shapes[k]))
        offs[k] = (off, n)
        off += -(-n // LANES) * LANES
    return offs, off


def _pack_small(d, shapes):
    offs, total = _small_layout(shapes)
    parts = []
    for k in REPLICATED:
        n = offs[k][1]
        parts.append(jnp.pad(d[k].reshape(-1).astype(F32), (0, -(-n // LANES) * LANES - n)))
    rows = -(-(total // LANES) // 8) * 8
    return jnp.pad(jnp.concatenate(parts), (0, rows * LANES - total)).reshape(rows, LANES)


def _unpack_small(a, shapes):
    offs, _ = _small_layout(shapes)
    flat = a.reshape(-1)
    return {k: flat[offs[k][0]:offs[k][0] + offs[k][1]].reshape(shapes[k]) for k in REPLICATED}


def _group_shards(w):
    shards = []
    for members, _ in GROUPS.values():
        s = jnp.concatenate([w[k].astype(_WIRE) for k in members], axis=0)
        shards.append(s.reshape(s.shape[0], 2, s.shape[1] // 2, s.shape[2]))
    return shards


class _Exchanged:
    def __init__(self, w):
        self.small = {k: w[k] for k in REPLICATED}
        self.ctx = _Sched()
        shards = dict(zip(GROUPS, _group_shards(w)))
        self.gather, self.cache, self.scatter = {}, {}, {}
        chunks = _pieces_in_order()
        self.me = jnp.reshape(2 * lax.axis_index("x") + lax.axis_index("y"), (1,)).astype(jnp.int32)
        first = {(g, l): _gather_task1(shards[g], l, _place_own(shards[g], l, self.me, name=f"own_{g}{l}"))
                 for chunk in chunks for g, l in chunk}
        self.gather = {p: _gather_task2(task) for p, task in first.items()}
        pieces = [p for chunk in chunks for p in chunk]
        for n, p in enumerate(pieces):
            self.ctx.push(first[p])
            if n:
                self.ctx.push(self.gather[pieces[n - 1]])
        self.ctx.push(self.gather[pieces[-1]])

    def weight(self, group, l):
        if (group, l) not in self.cache:
            task = self.gather[group, l]
            self.ctx.finish(task)
            F = task.result()
            self.cache[group, l] = _consumer_layout(group, F.reshape(4, -1, F.shape[-1]))
        return self.cache[group, l]

    def grad(self, group, l, G):
        self.scatter[group, l] = (G, self.ctx.push(_scatter_task(G)))

    def partial_sums(self, w):
        self.ctx.finish()
        parts = []
        for group, (members, _) in GROUPS.items():
            rows, C = w[members[0]].shape[1:]
            n = sum(w[k].shape[0] for k in members)
            part = jnp.zeros((n * rows, C), F32)
            for l in range(n):
                G, task = self.scatter[group, l]
                part = _sum_slots(G, task.result(), self.me, part, l * rows, name=f"sum_{group}{l}")
            parts.append(part)
        return parts


def _reduce_and_update(w, m, v, parts, small):
    others = _swap_sibling(parts)
    small_shapes = {k: w[k].shape for k in REPLICATED}
    g_small = _unpack_small(_allreduce_small(_pack_small(small, small_shapes)), small_shapes)

    out_g, out_d, out_m, out_v = {}, {}, {}, {}
    for (members, _), part, other in zip(GROUPS.values(), parts, others):
        row0 = 0
        for k in members:
            out_g[k], out_d[k], out_m[k], out_v[k] = _adamw(w[k], m[k], v[k], [part, other], row0=row0,
                                                            name="adamw_" + k)
            row0 += w[k].shape[0] * w[k].shape[1]
    sm = _adamw(_pack_small(w, small_shapes), _pack_small(m, small_shapes), _pack_small(v, small_shapes),
                [_pack_small(g_small, small_shapes)], name="adamw_small")
    for d, packed in zip((out_g, out_d, out_m, out_v), sm):
        d.update(_unpack_small(packed, small_shapes))
    return out_g, out_d, out_m, out_v


def kernel(x, ffn1_norm, ffn1_w_gu, ffn1_w_down, mix_norm, ffn2_norm, ffn2_w_gu, ffn2_w_down, nsa_w_in, nsa_ck_pe, nsa_ck_w1, nsa_ck_w2, nsa_cv_pe, nsa_cv_w1, nsa_cv_w2, nsa_w_out, swa_w_in, swa_sinks, swa_w_out, fox_w_in, fox_b_f, fox_w_out, final_norm, loss_target, m_ffn1_norm, m_ffn1_w_gu, m_ffn1_w_down, m_mix_norm, m_ffn2_norm, m_ffn2_w_gu, m_ffn2_w_down, m_nsa_w_in, m_nsa_ck_pe, m_nsa_ck_w1, m_nsa_ck_w2, m_nsa_cv_pe, m_nsa_cv_w1, m_nsa_cv_w2, m_nsa_w_out, m_swa_w_in, m_swa_sinks, m_swa_w_out, m_fox_w_in, m_fox_b_f, m_fox_w_out, m_final_norm, v_ffn1_norm, v_ffn1_w_gu, v_ffn1_w_down, v_mix_norm, v_ffn2_norm, v_ffn2_w_gu, v_ffn2_w_down, v_nsa_w_in, v_nsa_ck_pe, v_nsa_ck_w1, v_nsa_ck_w2, v_nsa_cv_pe, v_nsa_cv_w1, v_nsa_cv_w2, v_nsa_w_out, v_swa_w_in, v_swa_sinks, v_swa_w_out, v_fox_w_in, v_fox_b_f, v_fox_w_out, v_final_norm):
    args = dict(locals())
    w = {k: args[k] for k in WEIGHTS}
    m = {k: args["m_" + k] for k in WEIGHTS}
    v = {k: args["v_" + k] for k in WEIGHTS}
    P = _Exchanged(w)
    loss_part, dx, small = _local_step(x[0], loss_target[0], P)
    loss = lax.psum(loss_part, ("x", "y", "c"))
    out_g, out_d, out_m, out_v = _reduce_and_update(w, m, v, P.partial_sums(w), small)
    return (loss, dx[None], *[out_g[k] for k in WEIGHTS], *[out_d[k] for k in WEIGHTS],
            *[out_m[k] for k in WEIGHTS], *[out_v[k] for k in WEIGHTS])
```

```python
import math

import numpy as np
import jax
import jax.numpy as jnp
from jax import lax
from jax.experimental import pallas as pl
from jax.experimental.pallas import tpu as pltpu

F32 = jnp.float32
_CDT = jnp.bfloat16
_WIRE = jnp.bfloat16
_VMEM_LIMIT = 56 * 1024 * 1024

D_MODEL = 1024
DEPTH = 4
DH = 64
N_HEADS = 16
RMS_EPS = 1e-6
NEG = -1e30
SCALE = DH ** -0.5
BQ = 256
LANES = 128
NSA_G, NSA_R = 4, 4
NSA_WINDOW = 512
NSA_TOPK = 16
NSA_BONUS = 1e4
SWA_G, SWA_R = 2, 8
SWA_WINDOW = 128
NSA_ROTATED = tuple(hd < 16 or (hd < 40 and (hd - 16) % 8 < 4) for hd in range(42))
NSA_IN, NSA_IN_PAD = 2608, 2688
SWA_IN = 1280
SWA_ROTATED = (True,) * 18 + (False,) * 2
SWA_HEADS = (0, 16, 18)
FOX_IN, FOX_IN_PAD = 3088, 3200
FOX_HEADS = (0, 16, 32)
LR, B1, B2, EPS, WD, STEP = 0.001, 0.9, 0.999, 1e-08, 0.01, 10
MESH = pl.DeviceIdType.MESH

REPLICATED = ["ffn1_norm", "mix_norm", "ffn2_norm", "nsa_ck_pe", "nsa_ck_w2", "nsa_cv_pe", "nsa_cv_w2",
              "swa_sinks", "fox_b_f", "final_norm"]
WEIGHTS = ['ffn1_norm', 'ffn1_w_gu', 'ffn1_w_down', 'mix_norm', 'ffn2_norm', 'ffn2_w_gu', 'ffn2_w_down',
           'nsa_w_in', 'nsa_ck_pe', 'nsa_ck_w1', 'nsa_ck_w2', 'nsa_cv_pe', 'nsa_cv_w1', 'nsa_cv_w2', 'nsa_w_out',
           'swa_w_in', 'swa_sinks', 'swa_w_out', 'fox_w_in', 'fox_b_f', 'fox_w_out', 'final_norm']


def _params(**kw):
    return pltpu.CompilerParams(vmem_limit_bytes=_VMEM_LIMIT, **kw)


def _dot(a, b):
    return lax.dot_general(a, b, (((1,), (0,)), ((), ())), preferred_element_type=F32)


def _dot_nt(a, b):
    return lax.dot_general(a, b, (((1,), (1,)), ((), ())), preferred_element_type=F32)


def _dot_tn(a, b):
    return lax.dot_general(a, b, (((0,), (0,)), ((), ())), preferred_element_type=F32)


def _split3(x):
    hi = x.astype(jnp.bfloat16)
    r1 = x - hi.astype(F32)
    mid = r1.astype(jnp.bfloat16)
    lo = (r1 - mid.astype(F32)).astype(jnp.bfloat16)
    return hi, mid, lo


def _dot_exact(x, p):
    hi, mid, lo = _split3(x)
    return _dot(hi, p) + _dot(mid, p) + _dot(lo, p)


def _dot_exact_left(p, x):
    hi, mid, lo = _split3(x)
    return _dot(p, hi) + _dot(p, mid) + _dot(p, lo)


_ANY = pl.BlockSpec(memory_space=pl.ANY)


def _chip_peers():
    x, y, c = lax.axis_index("x"), lax.axis_index("y"), lax.axis_index("c")
    return x, y, c, [(1 - x, y), (x, 1 - y), (1 - x, 1 - y)]


class _Job:
    def __init__(self, ins, outs, nsem, copies, cost, alias):
        self.ins, self.outs, self.nsem, self.copies, self.cost, self.alias = ins, outs, nsem, copies, cost, alias
        self.results = None

    def inputs(self):
        return self.ins


class _Task:
    def __init__(self, src, out, parts, init=None):
        self.src, self.out, self.parts, self.init, self.done, self.jobs = src, out, parts, init, 0, []

    def result(self):
        return self.jobs[-1].results[0]

    def complete(self):
        return self.done == len(self.parts) and self.jobs[-1].results is not None

    def ready(self):
        if isinstance(self.src, _Task) and not self.src.complete():
            return False
        return not self.jobs or self.jobs[-1].results is not None

    def next_job(self, n):
        parts = self.parts[self.done:self.done + n]
        in_place = isinstance(self.src, _Task)
        ins = [] if in_place else [self.src]
        prev = self.jobs[-1].results[0] if self.jobs else (self.src.result() if in_place else self.init)
        alias = {}
        if prev is not None:
            alias = {len(ins): 0}
            ins = ins + [prev]

        def copies(in_refs, out_refs, sems):
            src = out_refs[0] if in_place else in_refs[0]
            out, first = [], 0
            for build, nsem, _ in parts:
                out += build(src, out_refs[0], sems, first)
                first += nsem
            return out

        job = _Job(ins, [self.out], sum(p[1] for p in parts), copies, sum(p[2] for p in parts), alias)
        self.done += n
        self.jobs.append(job)
        return job


def _pcall(body, *, out_shape, in_specs, out_specs, name, grid=(), scratch_shapes=(), aliases=None, jobs=(), ins):
    single = not isinstance(out_shape, (tuple, list))
    out_shape = (out_shape,) if single else tuple(out_shape)
    out_specs = (out_specs,) if single else tuple(out_specs)
    n_in, n_out, n_scr = len(ins), len(out_shape), len(scratch_shapes)
    side_ins = [a for j in jobs for a in j.inputs()]
    side_outs = [o for j in jobs for o in j.outs]
    aliases = dict(aliases or {})
    i0, o0 = n_in, n_out
    for j in jobs:
        for a, b in j.alias.items():
            aliases[i0 + a] = o0 + b
        i0 += len(j.ins)
        o0 += len(j.outs)

    def body2(*refs):
        in_refs, s_in = refs[:n_in], refs[n_in:n_in + len(side_ins)]
        r = n_in + len(side_ins)
        out_refs, s_out = refs[r:r + n_out], refs[r + n_out:r + n_out + len(side_outs)]
        r += n_out + len(side_outs)
        scr, sems = refs[r:r + n_scr], refs[r + n_scr:]
        def descriptors():
            out, a, b = [], 0, 0
            for j, sem in zip(jobs, sems):
                out += j.copies(s_in[a:a + len(j.ins)], s_out[b:b + len(j.outs)], sem)
                a += len(j.ins)
                b += len(j.outs)
            return out

        def at_step(steps, fn):
            cond = None
            for d, s in enumerate(steps):
                hit = pl.program_id(d) == s
                cond = hit if cond is None else jnp.logical_and(cond, hit)
            if cond is None:
                fn()
            else:
                pl.when(cond)(fn)

        def start_all():
            for cp in descriptors():
                cp.start()

        def wait_all():
            for cp in descriptors():
                cp.wait()

        if jobs:
            at_step([0] * len(grid), start_all)
        body(*in_refs, *out_refs, *scr)
        if jobs:
            at_step([n - 1 for n in grid], wait_all)

    res = pl.pallas_call(
        body2, out_shape=out_shape + tuple(side_outs), grid=grid, in_specs=list(in_specs) + [_ANY] * len(side_ins),
        out_specs=out_specs + (_ANY,) * len(side_outs),
        scratch_shapes=list(scratch_shapes) + [pltpu.SemaphoreType.DMA((j.nsem,)) for j in jobs],
        input_output_aliases=aliases,
        compiler_params=_params(has_side_effects=True) if jobs else _params(), name=name)(*ins, *side_ins)
    b = n_out
    for j in jobs:
        j.results = list(res[b:b + len(j.outs)])
        b += len(j.outs)
    return res[0] if single else tuple(res[:n_out])


def _comm_call(jobs, name):
    def body():
        pass

    _pcall(body, out_shape=(), in_specs=[], out_specs=(), name=name, jobs=jobs, ins=[])


def _place_own(shard, l, me, *, name):
    _, _, h, C = shard.shape

    def body(_, s_ref, o_ref):
        o_ref[...] = s_ref[...]

    spec = pltpu.PrefetchScalarGridSpec(
        num_scalar_prefetch=1, grid=(2,),
        in_specs=[pl.BlockSpec((None, None, h, C), lambda i, me_ref: (l, i, 0, 0))],
        out_specs=pl.BlockSpec((None, None, h, C), lambda i, me_ref: (me_ref[0], i, 0, 0)))
    return pl.pallas_call(body, grid_spec=spec, out_shape=jax.ShapeDtypeStruct((4, 2, h, C), shard.dtype),
                          compiler_params=_params(), name=name)(me, shard)


def _gather_task1(shard, l, own):
    _, _, h, C = shard.shape
    mb = 3 * h * C * jnp.dtype(_WIRE).itemsize / 1e6
    n = _row_splits(h, ICI_US_PER_MB * mb)
    hr = h // n

    def rows_to_peers(r):
        def build(src, dst, sems, s0):
            x, y, c, chips = _chip_peers()
            me, rows = 2 * x + y, pl.ds(r * hr, hr)
            return [pltpu.make_async_remote_copy(src_ref=src.at[l, c, rows], dst_ref=dst.at[me, c, rows],
                                                 send_sem=sems.at[s0 + 2 * j], recv_sem=sems.at[s0 + 2 * j + 1],
                                                 device_id=(px, py, c), device_id_type=MESH)
                    for j, (px, py) in enumerate(chips)]
        return build

    parts = [(rows_to_peers(r), 6, ICI_US_PER_MB * mb / n) for r in range(n)]
    return _Task(shard, jax.ShapeDtypeStruct((4, 2, h, C), shard.dtype), parts, init=own)


def _gather_task2(task1):
    _, _, h, C = task1.out.shape
    mb = h * C * jnp.dtype(_WIRE).itemsize / 1e6

    def forward(j):
        def build(_, dst, sems, s0):
            x, y, c, chips = _chip_peers()
            px, py = chips[j]
            landed = dst.at[2 * px + py, c]
            return [pltpu.make_async_remote_copy(src_ref=landed, dst_ref=landed, send_sem=sems.at[s0],
                                                 recv_sem=sems.at[s0 + 1], device_id=(x, y, 1 - c),
                                                 device_id_type=MESH)]
        return build

    return _Task(task1, task1.out, [(forward(j), 2, 1.0 + D2D_US_PER_MB * mb) for j in range(3)])


def _scatter_task(G):
    mb = 3 * G.shape[1] * G.shape[2] * jnp.dtype(_WIRE).itemsize / 1e6
    n = _row_splits(G.shape[1], ICI_US_PER_MB * mb)
    rr = G.shape[1] // n

    def rows_to_peers(r):
        def build(src, dst, sems, s0):
            _, _, c, chips = _chip_peers()
            rows = pl.ds(r * rr, rr)
            return [pltpu.make_async_remote_copy(src_ref=src.at[2 * px + py, rows], dst_ref=dst.at[j, rows],
                                                 send_sem=sems.at[s0 + 2 * j], recv_sem=sems.at[s0 + 2 * j + 1],
                                                 device_id=(px, py, c), device_id_type=MESH)
                    for j, (px, py) in enumerate(chips)]
        return build

    parts = [(rows_to_peers(r), 6, ICI_US_PER_MB * mb / n) for r in range(n)]
    return _Task(G, jax.ShapeDtypeStruct((3,) + G.shape[1:], G.dtype), parts)


def _row_splits(rows, cost):
    n = 8
    while n > 1 and (rows % (16 * n) or cost / n < PART_US):
        n //= 2
    return n


CARRIER_US = {
    "f_gu": 48, "f_down": 20, "f_dact": 42, "f_dwd": 20, "f_dwgu": 42, "f_dh": 46,
    "nsa_in": 25, "nsa_cmp": 86, "nsa_slc": 122, "nsa_win": 74, "nsa_dcmp": 58, "nsa_dslc": 195, "nsa_dwin": 108,
    "nsa_dh": 26, "nsa_dwin_w": 25,
    "swa_in": 16, "swa_attn": 74, "swa_dattn": 74, "swa_dh": 17, "swa_dwin_w": 18,
    "fox_in": 34, "fox_attn": 104, "fox_dattn": 161, "fox_dh": 30, "fox_dwin_w": 30,
}
ICI_US_PER_MB = 15.0
PART_US = 12.0
D2D_US_PER_MB = 2.3


class _Sched:
    def __init__(self):
        self.queue, self.credit, self.n_flush = [], 0.0, 0

    def push(self, task):
        self.queue.append(task)
        return task

    def _jobs(self, fits):
        jobs = []
        while self.queue and self.queue[0].ready():
            task, n = self.queue[0], 0
            while task.done + n < len(task.parts) and fits(task.parts[task.done + n][2]):
                n += 1
            if n:
                jobs.append(task.next_job(n))
            if task.done < len(task.parts):
                break
            self.queue.pop(0)
        return jobs

    def take(self, name):
        kind = name[2:].split("_", 1)
        key = ("f" if kind[0] in ("f1", "f2") else kind[0]) + "_" + kind[1]
        self.credit = min(self.credit, 0.0) + CARRIER_US.get(key, 0.0)

        def fits(cost):
            if self.credit < 0.5 * cost:
                return False
            self.credit -= cost
            return True

        return self._jobs(fits)

    def finish(self, task=None):
        while self.queue and (task is None or not task.complete()):
            jobs = []
            while self.queue and self.queue[0].ready():
                head = self.queue.pop(0)
                jobs.append(head.next_job(len(head.parts) - head.done))
                if head is task:
                    break
            assert jobs, "the task at the head of the queue waits for one that was never queued"
            _comm_call(jobs, name=f"exchange_{self.n_flush}")
            self.n_flush += 1


def _mm(a, b, *, ta=False, tb=False, tm, tn, b_lead=(), into=None, o_lead=(), out_dtype=F32, res=None, alpha=1.0,
        ctx=None, name):
    M = a.shape[1] if ta else a.shape[0]
    K = a.shape[0] if ta else a.shape[1]
    bk, bn = (b.shape[-1], b.shape[-2]) if tb else (b.shape[-2], b.shape[-1])
    j_lead, k_lead = "j" in b_lead, "k" in b_lead
    N = bn * (b.shape[b_lead.index("j")] if j_lead else 1)
    nk = b.shape[b_lead.index("k")] if k_lead else 1
    tk = K // nk
    assert tk == bk, (name, K, nk, bk)
    tm = min(tm, M)
    tn = bn if j_lead else min(tn, N)
    assert M % tm == 0 and N % tn == 0, (name, M, N, tm, tn)
    nb, no = len(b_lead), len(o_lead)

    def pick(lead, j, k):
        return tuple(j if t == "j" else k if t == "k" else t for t in lead)

    a_spec = pl.BlockSpec((tk, tm), lambda i, j, k: (k, i)) if ta else pl.BlockSpec((tm, tk), lambda i, j, k: (i, k))
    if tb:
        b_spec = pl.BlockSpec((None,) * nb + (tn, tk),
                              lambda i, j, k: pick(b_lead, j, k) + (0 if j_lead else j, 0 if k_lead else k))
    else:
        b_spec = pl.BlockSpec((None,) * nb + (tk, tn),
                              lambda i, j, k: pick(b_lead, j, k) + (0 if k_lead else k, 0 if j_lead else j))
    r_spec = pl.BlockSpec((tm, tn), lambda i, j, k: (i, j))
    if o_lead == ("heads",):
        o_spec = pl.BlockSpec((tn // DH, tm, DH), lambda i, j, k: (j, i, 0))
    else:
        o_spec = pl.BlockSpec((None,) * no + (tm, tn),
                              lambda i, j, k: pick(o_lead, j, k) + (i, 0 if "j" in o_lead else j))
    dn = (((0 if ta else 1,), (1 if tb else 0,)), ((), ()))
    has_res, has_into = res is not None, into is not None
    if has_into:
        out_dtype = into.dtype

    def body(*refs):
        a_ref, b_ref = refs[0], refs[1]
        r_ref = refs[2] if has_res else None
        o_ref = refs[2 + has_res + has_into]
        prod = lax.dot_general(a_ref[...].astype(_CDT), b_ref[...].astype(_CDT), dn, preferred_element_type=F32)

        def finish(acc):
            if alpha != 1.0:
                acc = acc * alpha
            if has_res:
                acc = r_ref[...] + acc
            if o_lead == ("heads",):
                for hd in range(tn // DH):
                    o_ref[hd] = acc[:, hd * DH:(hd + 1) * DH].astype(out_dtype)
            else:
                o_ref[...] = acc.astype(out_dtype)

        if nk == 1:
            finish(prod)
        else:
            acc_ref = refs[-1]
            k = pl.program_id(2)

            @pl.when(k == 0)
            def _():
                acc_ref[...] = prod

            @pl.when(k != 0)
            def _():
                acc_ref[...] += prod

            @pl.when(k == nk - 1)
            def _():
                finish(acc_ref[...])

    ins, specs = [a, b], [a_spec, b_spec]
    if has_res:
        ins.append(res)
        specs.append(r_spec)
    aliases = {}
    if has_into:
        aliases = {len(ins): 0}
        ins.append(into)
        specs.append(_ANY)
        out_shape = jax.ShapeDtypeStruct(into.shape, into.dtype)
    elif o_lead == ("j",):
        out_shape = jax.ShapeDtypeStruct((N // tn, M, tn), out_dtype)
    elif o_lead == ("heads",):
        out_shape = jax.ShapeDtypeStruct((N // DH, M, DH), out_dtype)
    else:
        assert not o_lead
        out_shape = jax.ShapeDtypeStruct((M, N), out_dtype)
    scratch = [pltpu.VMEM((tm, tn), F32)] if nk > 1 else []
    return _pcall(body, out_shape=out_shape, grid=(M // tm, N // tn, nk), in_specs=specs, out_specs=o_spec,
                  scratch_shapes=scratch, aliases=aliases, name=name, jobs=ctx.take(name) if ctx else (), ins=ins)


def _rows2d(a):
    return a.reshape(-1, a.shape[-1])


def _row_tile(rows, cols, itemsize=4, budget=2 * 1024 * 1024):
    t = rows
    while t % 2 == 0 and t * cols * itemsize > budget and (t // 2) % 8 == 0:
        t //= 2
    return t


def _addn(*xs, name):
    shape = xs[0].shape
    x2 = [_rows2d(x) for x in xs]
    R, C = x2[0].shape
    tr = _row_tile(R, C)

    def body(*refs):
        acc = refs[0][...]
        for r in refs[1:-1]:
            acc = acc + r[...]
        refs[-1][...] = acc

    spec = pl.BlockSpec((tr, C), lambda i: (i, 0))
    out = pl.pallas_call(body, out_shape=jax.ShapeDtypeStruct((R, C), F32), grid=(R // tr,),
                         in_specs=[spec] * len(x2), out_specs=spec, compiler_params=_params(), name=name)(*x2)
    return out.reshape(shape)


def _rms_fwd(x, g, *, name):
    S, D = x.shape
    tr = 256

    def body(x_ref, g_ref, h_ref):
        xv = x_ref[...]
        rstd = lax.rsqrt(jnp.mean(xv * xv, axis=-1, keepdims=True) + RMS_EPS)
        h_ref[...] = (xv * rstd * g_ref[...]).astype(_CDT)

    return pl.pallas_call(body, out_shape=jax.ShapeDtypeStruct((S, D), _CDT), grid=(S // tr,),
                          in_specs=[pl.BlockSpec((tr, D), lambda i: (i, 0)), pl.BlockSpec((1, D), lambda i: (0, 0))],
                          out_specs=pl.BlockSpec((tr, D), lambda i: (i, 0)), compiler_params=_params(),
                          name=name)(x, g.reshape(1, D))


def _rms_bwd(dh, x, g, dres, *, name):
    S, D = x.shape
    tr = 256

    def body(dh_ref, x_ref, g_ref, dres_ref, dx_ref, dg_ref):
        xv = x_ref[...]
        rstd = lax.rsqrt(jnp.mean(xv * xv, axis=-1, keepdims=True) + RMS_EPS)
        xhat = xv * rstd
        dhv = dh_ref[...]
        dxhat = dhv * g_ref[...]
        dx_ref[...] = dres_ref[...] + rstd * (dxhat - xhat * jnp.mean(dxhat * xhat, axis=-1, keepdims=True))

        @pl.when(pl.program_id(0) == 0)
        def _():
            dg_ref[...] = jnp.zeros_like(dg_ref)

        dg_ref[...] += jnp.sum(dhv * xhat, axis=0, keepdims=True)

    row = pl.BlockSpec((tr, D), lambda i: (i, 0))
    vec = pl.BlockSpec((1, D), lambda i: (0, 0))
    dx, dg = pl.pallas_call(body, out_shape=(jax.ShapeDtypeStruct((S, D), F32), jax.ShapeDtypeStruct((1, D), F32)),
                            grid=(S // tr,), in_specs=[row, row, vec, row], out_specs=(row, vec),
                            compiler_params=_params(), name=name)(dh, x, g.reshape(1, D), dres)
    return dx, dg.reshape(D)


def _loss_head(x, g, tgt, *, name):
    S, D = x.shape
    tr = 256

    def body(x_ref, g_ref, t_ref, loss_ref, dx_ref, dg_ref):
        xv = x_ref[...]
        rstd = lax.rsqrt(jnp.mean(xv * xv, axis=-1, keepdims=True) + RMS_EPS)
        xhat = xv * rstd
        err = xhat * g_ref[...] - t_ref[...]
        part = 0.5 * jnp.sum(jnp.mean(err * err, axis=-1, keepdims=True), axis=0, keepdims=True)
        dy = err * (1.0 / D)
        dxhat = dy * g_ref[...]
        dx_ref[...] = rstd * (dxhat - xhat * jnp.mean(dxhat * xhat, axis=-1, keepdims=True))

        @pl.when(pl.program_id(0) == 0)
        def _():
            dg_ref[...] = jnp.zeros_like(dg_ref)
            loss_ref[...] = jnp.zeros_like(loss_ref)

        dg_ref[...] += jnp.sum(dy * xhat, axis=0, keepdims=True)
        loss_ref[...] += jnp.broadcast_to(part, loss_ref.shape)

    row = pl.BlockSpec((tr, D), lambda i: (i, 0))
    vec = pl.BlockSpec((1, D), lambda i: (0, 0))
    loss, dx, dg = pl.pallas_call(
        body, out_shape=(jax.ShapeDtypeStruct((8, LANES), F32), jax.ShapeDtypeStruct((S, D), F32),
                         jax.ShapeDtypeStruct((1, D), F32)),
        grid=(S // tr,), in_specs=[row, vec, row], out_specs=(pl.BlockSpec((8, LANES), lambda i: (0, 0)), row, vec),
        compiler_params=_params(), name=name)(x, g.reshape(1, D), tgt)
    return loss[0, 0], dx, dg.reshape(D)


def _rope_tables(S):
    inv = 10000.0 ** (-jnp.arange(0, DH, 2, dtype=F32) / DH)
    ang = jnp.arange(S, dtype=F32)[:, None] * inv[None, :]
    cos, sin = jnp.cos(ang), jnp.sin(ang)
    return jnp.concatenate([cos, cos], -1), jnp.concatenate([-sin, sin], -1)


def _swap_matrix():
    p = np.zeros((DH, DH), np.float32)
    for j in range(DH // 2):
        p[j + DH // 2, j] = 1.0
        p[j, j + DH // 2] = 1.0
    return jnp.asarray(p, jnp.bfloat16)


def _rope(x, cc, ss, rotated, *, name):
    n, S, _ = x.shape
    assert len(rotated) == n
    starts = [i for i in range(n) if rotated[i] and (i == 0 or not rotated[i - 1])]
    tab = pl.BlockSpec((S, DH), lambda i: (0, 0))
    for k, first in enumerate(starts):
        count = next((i for i in range(first, n) if not rotated[i]), n) - first

        def body(x_ref, c_ref, s_ref, p_ref, o_ref):
            xv = x_ref[0]
            o_ref[0] = xv * c_ref[...] + _dot_exact(xv, p_ref[...]) * s_ref[...]

        blk = pl.BlockSpec((1, S, DH), lambda i, first=first: (first + i, 0, 0))
        x = pl.pallas_call(body, out_shape=jax.ShapeDtypeStruct(x.shape, F32), grid=(count,),
                           in_specs=[blk, tab, tab, pl.BlockSpec((DH, DH), lambda i: (0, 0))], out_specs=blk,
                           input_output_aliases={0: 0}, compiler_params=_params(),
                           name=f"{name}{k}")(x, cc, ss, _swap_matrix())
    return x


def _key_range(kind, i, window, Sk):
    if kind == "cmp":
        return 0, Sk
    hi = (i + 1) * BQ
    if kind == "band":
        return max(0, i * BQ - window), hi
    return 0, hi


def _attn_mask(kind, i, lo, hi, window):
    shape = (BQ, hi - lo)
    qpos = i * BQ + lax.broadcasted_iota(jnp.int32, shape, 0)
    kpos = lo + lax.broadcasted_iota(jnp.int32, shape, 1)
    if kind == "cmp":
        return kpos * 16 + 31 <= qpos
    mask = kpos <= qpos
    if kind == "band":
        mask = mask & (qpos - kpos < window)
    return mask


def _sel_expand(n_slc, n_keys):
    shape = (n_slc, n_keys)
    j = lax.broadcasted_iota(jnp.int32, shape, 0)
    key = lax.broadcasted_iota(jnp.int32, shape, 1)
    return (jnp.right_shift(key, 6) == j).astype(_CDT)


def _eye():
    return lax.broadcasted_iota(jnp.int32, (BQ, BQ), 0) == lax.broadcasted_iota(jnp.int32, (BQ, BQ), 1)


def _to_col(row):
    return jnp.sum(jnp.where(_eye(), row, 0.0), axis=1, keepdims=True)


def _to_row(col):
    return jnp.sum(jnp.where(_eye(), col, 0.0), axis=0, keepdims=True)


def _scores(kind, i, lo, hi, window, qb, kb, crow_ref, sel_ref):
    s = _dot_nt(qb, kb)
    if crow_ref is not None:
        s = s + _to_col(crow_ref[0, :, i * BQ:(i + 1) * BQ]) - crow_ref[0, :, lo:hi]
    mask = _attn_mask(kind, i, lo, hi, window)
    if sel_ref is not None:
        mask = mask & (sel_ref[0, i * BQ:(i + 1) * BQ, lo:hi].astype(F32) > 0.5)
    return jnp.where(mask, s, NEG), mask


def _attn_fwd(q, k, v, *, G, R, heads=(0, 0, 0), kind, window=0, bias=None, sinks=None, selT=None, ovT=None, ctx=None,
              name):
    S = q.shape[1]
    Sk = k.shape[1]
    q0, k0, v0 = heads
    nq = S // BQ
    n_slc = S // 64
    has_bias, has_sink, has_sel, is_cmp = bias is not None, sinks is not None, selT is not None, kind == "cmp"

    def body(*refs):
        it = iter(refs)
        q_ref, k_ref, v_ref = next(it), next(it), next(it)
        crow_ref = next(it) if has_bias else None
        sink_ref = next(it) if has_sink else None
        sel_ref = next(it) if has_sel else None
        ov_ref = next(it) if is_cmp else None
        o_ref, lse_ref = next(it), next(it)
        selo_ref, imp_ref = (next(it), next(it)) if is_cmp else (None, None)
        r = pl.program_id(1)
        for i in range(nq):
            lo, hi = _key_range(kind, i, window, Sk)
            rows = slice(i * BQ, (i + 1) * BQ)
            qb = (q_ref[rows, :] * SCALE).astype(_CDT)
            kb = k_ref[0, lo:hi, :].astype(_CDT)
            vb = v_ref[0, lo:hi, :].astype(_CDT)
            s, mask = _scores(kind, i, lo, hi, window, qb, kb, crow_ref, sel_ref)
            m = jnp.max(s, axis=-1, keepdims=True)
            if has_sink:
                sk = sink_ref[0, 0, 0:1, 0:1]
                m = jnp.maximum(m, sk)
            e = jnp.exp(s - m)
            if is_cmp:
                e = jnp.where(mask, e, 0.0)
            l = jnp.sum(e, axis=-1, keepdims=True)
            if has_sink:
                l = l + jnp.exp(sk - m)
            if is_cmp:
                l = jnp.where(l > 0.0, l, 1.0)
            p = e * (1.0 / l)
            o_ref[0, 0, rows, :] = _dot(p.astype(_CDT), vb)
            lse_ref[0, 0, :, rows] = _to_row(m + jnp.log(l))
            if is_cmp:
                part = _dot_nt(ov_ref[...].astype(_CDT), p.astype(_CDT))

                @pl.when(r == 0)
                def _():
                    imp_ref[:, rows] = part

                @pl.when(r != 0)
                def _():
                    imp_ref[:, rows] += part

        if is_cmp:
            @pl.when(r == R - 1)
            def _():
                shape = (n_slc, S)
                j = lax.broadcasted_iota(jnp.int32, shape, 0)
                tb = jnp.right_shift(lax.broadcasted_iota(jnp.int32, shape, 1), 6)
                forced = (j == 0) | (j == tb) | (j == tb - 1)
                imp = jnp.where(j > tb, NEG, jnp.where(forced, NSA_BONUS, imp_ref[0:n_slc, :]))
                imp_ref[0:n_slc, :] = imp
                cnt = jnp.zeros(shape, F32)
                for jp in range(n_slc):
                    row = imp_ref[jp:jp + 1, :]
                    ahead = (row > imp) | ((row == imp) & (jp < j))
                    cnt = cnt + ahead.astype(F32)
                imp_ref[0:n_slc, :] = (cnt < float(min(NSA_TOPK, n_slc))).astype(F32)
                expand = _sel_expand(n_slc, S)
                for i in range(nq):
                    rows = slice(i * BQ, (i + 1) * BQ)
                    chosen = _dot_tn(imp_ref[0:n_slc, rows].astype(_CDT), expand)
                    selo_ref[0, rows, :] = chosen.astype(jnp.bfloat16)

    qspec = pl.BlockSpec((1, 1, S, DH), lambda g, r: (g, r, 0, 0))
    ins = [q, k, v]
    specs = [pl.BlockSpec((None, S, DH), lambda g, r: (q0 + g * R + r, 0, 0)),
             pl.BlockSpec((1, Sk, DH), lambda g, r: (k0 + g, 0, 0)), pl.BlockSpec((1, Sk, DH), lambda g, r: (v0 + g, 0, 0))]
    if has_bias:
        ins.append(bias)
        specs.append(pl.BlockSpec((1, 1, S), lambda g, r: (g, 0, 0)))
    if has_sink:
        ins.append(sinks)
        specs.append(pl.BlockSpec((1, 1, 8, LANES), lambda g, r: (g, r, 0, 0)))
    if has_sel:
        ins.append(selT)
        specs.append(pl.BlockSpec((1, S, S), lambda g, r: (g, 0, 0)))
    if is_cmp:
        ins.append(ovT)
        specs.append(pl.BlockSpec((LANES, Sk), lambda g, r: (0, 0)))
    outs = [jax.ShapeDtypeStruct((G, R, S, DH), F32), jax.ShapeDtypeStruct((G, R, 1, S), F32)]
    ospecs = [qspec, pl.BlockSpec((1, 1, 1, S), lambda g, r: (g, r, 0, 0))]
    scratch = []
    if is_cmp:
        outs.append(jax.ShapeDtypeStruct((G, S, S), jnp.bfloat16))
        ospecs.append(pl.BlockSpec((1, S, S), lambda g, r: (g, 0, 0)))
        scratch.append(pltpu.VMEM((LANES, S), F32))
    return _pcall(body, out_shape=tuple(outs), grid=(G, R), in_specs=specs, out_specs=tuple(ospecs),
                  scratch_shapes=scratch, name=name, jobs=ctx.take(name) if ctx else (), ins=ins)


def _attn_bwd(q, k, v, o, do, lse, *, heads=(0, 0, 0), kind, window=0, bias=None, sinks=None, selT=None, gate=None,
              ctx=None, name):
    G, R, S, _ = o.shape
    Sk = k.shape[1]
    q0, k0, v0 = heads
    nq = S // BQ
    has_bias, has_sink, has_sel, has_gate = bias is not None, sinks is not None, selT is not None, gate is not None

    def body(*refs):
        it = iter(refs)
        q_ref, k_ref, v_ref, o_ref, do_ref, lse_ref = (next(it) for _ in range(6))
        crow_ref = next(it) if has_bias else None
        sink_ref = next(it) if has_sink else None
        sel_ref = next(it) if has_sel else None
        z_ref = next(it) if has_gate else None
        dq_ref, dk_ref, dv_ref = next(it), next(it), next(it)
        dc_ref = next(it) if has_bias else None
        dsink_ref = next(it) if has_sink else None
        dz_ref = next(it) if has_gate else None
        r = pl.program_id(1)

        @pl.when(r == 0)
        def _():
            dk_ref[...] = jnp.zeros_like(dk_ref)
            dv_ref[...] = jnp.zeros_like(dv_ref)
            if has_bias:
                dc_ref[...] = jnp.zeros_like(dc_ref)

        dsink = jnp.zeros((1, 1), F32)
        for i in range(nq):
            lo, hi = _key_range(kind, i, window, Sk)
            rows = slice(i * BQ, (i + 1) * BQ)
            qb = (q_ref[rows, :] * SCALE).astype(_CDT)
            kb = k_ref[0, lo:hi, :].astype(_CDT)
            vb = v_ref[0, lo:hi, :].astype(_CDT)
            s, mask = _scores(kind, i, lo, hi, window, qb, kb, crow_ref, sel_ref)
            lse_i = _to_col(lse_ref[0, 0, :, rows])
            p = jnp.where(mask, jnp.exp(s - lse_i), 0.0)
            dob = do_ref[0, 0, rows, :]
            if has_gate:
                od = jnp.sum(o_ref[0, 0, rows, :] * dob, axis=-1, keepdims=True)
                sg = jax.nn.sigmoid(_to_col(z_ref[0, 0, :, rows]))
                dob = dob * sg
                dz_ref[0, 0, :, rows] = _to_row(od * sg * (1.0 - sg))
            dob = dob.astype(_CDT)
            dp = _dot_nt(dob, vb)
            delta = jnp.sum(p * dp, axis=-1, keepdims=True)
            ds = p * (dp - delta)
            dsb = ds.astype(_CDT)
            dq_ref[0, 0, rows, :] = _dot(dsb, kb) * SCALE
            dk_ref[0, lo:hi, :] += _dot_tn(dsb, qb)
            dv_ref[0, lo:hi, :] += _dot_tn(p.astype(_CDT), dob)
            if has_bias:
                dc_ref[0, :, rows] += _to_row(jnp.sum(ds, axis=-1, keepdims=True))
                dc_ref[0, :, lo:hi] -= jnp.sum(ds, axis=0, keepdims=True)
            if has_sink:
                sk = sink_ref[0, 0, 0:1, 0:1]
                dsink = dsink - jnp.sum(jnp.exp(sk - lse_i) * delta, axis=0, keepdims=True)
        if has_sink:
            dsink_ref[0, 0] = jnp.broadcast_to(dsink, (8, LANES))

    qspec = pl.BlockSpec((1, 1, S, DH), lambda g, r: (g, r, 0, 0))
    cspec = pl.BlockSpec((1, 1, 1, S), lambda g, r: (g, r, 0, 0))
    kspec = pl.BlockSpec((1, Sk, DH), lambda g, r: (g, 0, 0))
    ins = [q, k, v, o, do, lse]
    specs = [pl.BlockSpec((None, S, DH), lambda g, r: (q0 + g * R + r, 0, 0)),
             pl.BlockSpec((1, Sk, DH), lambda g, r: (k0 + g, 0, 0)), pl.BlockSpec((1, Sk, DH), lambda g, r: (v0 + g, 0, 0)),
             qspec, qspec, cspec]
    if has_bias:
        ins.append(bias)
        specs.append(pl.BlockSpec((1, 1, S), lambda g, r: (g, 0, 0)))
    if has_sink:
        ins.append(sinks)
        specs.append(pl.BlockSpec((1, 1, 8, LANES), lambda g, r: (g, r, 0, 0)))
    if has_sel:
        ins.append(selT)
        specs.append(pl.BlockSpec((1, S, S), lambda g, r: (g, 0, 0)))
    if has_gate:
        ins.append(gate)
        specs.append(cspec)
    names = ["dq", "dk", "dv"]
    outs = [jax.ShapeDtypeStruct((G, R, S, DH), F32), jax.ShapeDtypeStruct((G, Sk, DH), F32),
            jax.ShapeDtypeStruct((G, Sk, DH), F32)]
    ospecs = [qspec, kspec, kspec]
    if has_bias:
        assert R == 1
        names.append("dc")
        outs.append(jax.ShapeDtypeStruct((G, 1, S), F32))
        ospecs.append(pl.BlockSpec((1, 1, S), lambda g, r: (g, 0, 0)))
    if has_sink:
        names.append("dsink")
        outs.append(jax.ShapeDtypeStruct((G, R, 8, LANES), F32))
        ospecs.append(pl.BlockSpec((1, 1, 8, LANES), lambda g, r: (g, r, 0, 0)))
    if has_gate:
        names.append("dz")
        outs.append(jax.ShapeDtypeStruct((G, R, 1, S), F32))
        ospecs.append(cspec)
    res = _pcall(body, out_shape=tuple(outs), grid=(G, R), in_specs=specs, out_specs=tuple(ospecs), name=name,
                 jobs=ctx.take(name) if ctx else (), ins=ins)
    return dict(zip(names, res))


def _combine(o0, o1, o2, z, *, name):
    H, S, _ = o0.shape

    def body(o0_ref, o1_ref, o2_ref, z_ref, o_ref):
        for i in range(S // BQ):
            rows = slice(i * BQ, (i + 1) * BQ)
            acc = jax.nn.sigmoid(_to_col(z_ref[0, 0, :, rows])) * o0_ref[0, rows, :]
            acc = acc + jax.nn.sigmoid(_to_col(z_ref[1, 0, :, rows])) * o1_ref[0, rows, :]
            acc = acc + jax.nn.sigmoid(_to_col(z_ref[2, 0, :, rows])) * o2_ref[0, rows, :]
            o_ref[0, rows, :] = acc

    blk = pl.BlockSpec((1, S, DH), lambda h: (h, 0, 0))
    return pl.pallas_call(body, out_shape=jax.ShapeDtypeStruct((H, S, DH), F32), grid=(H,),
                          in_specs=[blk, blk, blk, pl.BlockSpec((3, 1, 1, S), lambda h: (0, h, 0, 0))], out_specs=blk,
                          compiler_params=_params(), name=name)(o0, o1, o2, z)


_GC = math.sqrt(2.0 / math.pi)


def _gelu(x):
    return 0.5 * x * (1.0 + jnp.tanh(_GC * (x + 0.044715 * x * x * x)))


def _gelu_grad(x):
    t = jnp.tanh(_GC * (x + 0.044715 * x * x * x))
    return 0.5 * (1.0 + t) + 0.5 * x * (1.0 - t * t) * _GC * (1.0 + 3.0 * 0.044715 * x * x)


def _make_xb(k):
    G, S, _ = k.shape
    chunks = k.reshape(G, S // 16, 16 * DH)
    shift = jnp.concatenate([chunks[:, 1:], jnp.zeros((G, 1, 16 * DH), k.dtype)], axis=1)
    return jnp.concatenate([chunks, shift], axis=-1)


def _unmake_xb(dxb, *, name):
    G, n, _ = dxb.shape
    a = dxb[..., :16 * DH]
    b = jnp.concatenate([jnp.zeros((G, 1, 16 * DH), F32), dxb[:, :-1, 16 * DH:]], axis=1)
    return _addn(a, b, name=name).reshape(G, n * 16, DH)


def _compress_fwd(xb, pe, w1, w2, *, name):
    G, n, W = xb.shape
    Hc = w1.shape[1]

    def body(xb_ref, pe_ref, w1_ref, w2_ref, kc_ref, hid_ref):
        xv = (xb_ref[0] + pe_ref[...]).astype(_CDT)
        hid = _dot(xv, w1_ref[...].astype(_CDT))
        hid_ref[0] = hid
        kc_ref[0] = _dot(_gelu(hid).astype(_CDT), w2_ref[...].astype(_CDT))

    return pl.pallas_call(
        body, out_shape=(jax.ShapeDtypeStruct((G, n, DH), F32), jax.ShapeDtypeStruct((G, n, Hc), F32)), grid=(G,),
        in_specs=[pl.BlockSpec((1, n, W), lambda g: (g, 0, 0)), pl.BlockSpec((1, W), lambda g: (0, 0)),
                  pl.BlockSpec((W, Hc), lambda g: (0, 0)), pl.BlockSpec((Hc, DH), lambda g: (0, 0))],
        out_specs=(pl.BlockSpec((1, n, DH), lambda g: (g, 0, 0)), pl.BlockSpec((1, n, Hc), lambda g: (g, 0, 0))),
        compiler_params=_params(), name=name)(xb, pe.reshape(1, W), w1, w2)


def _compress_bwd(xb, pe, w1, w2, hid, dkc, *, name):
    G, n, W = xb.shape
    Hc = w1.shape[1]

    def body(xb_ref, pe_ref, w1_ref, w2_ref, hid_ref, dkc_ref, dxb_ref, dw1_ref, dw2_ref, dpe_ref):
        @pl.when(pl.program_id(0) == 0)
        def _():
            dw1_ref[...] = jnp.zeros_like(dw1_ref)
            dw2_ref[...] = jnp.zeros_like(dw2_ref)
            dpe_ref[...] = jnp.zeros_like(dpe_ref)

        xv = (xb_ref[0] + pe_ref[...]).astype(_CDT)
        hid = hid_ref[0]
        dk = dkc_ref[0].astype(_CDT)
        dact = _dot_nt(dk, w2_ref[...].astype(_CDT))
        dhid = (dact * _gelu_grad(hid)).astype(_CDT)
        dw2_ref[...] += _dot_tn(_gelu(hid).astype(_CDT), dk)
        dxb = _dot_nt(dhid, w1_ref[...].astype(_CDT))
        dxb_ref[0] = dxb
        dw1_ref[...] += _dot_tn(xv, dhid)
        dpe_ref[...] += jnp.sum(dxb, axis=0, keepdims=True)

    return pl.pallas_call(
        body, out_shape=(jax.ShapeDtypeStruct((G, n, W), F32), jax.ShapeDtypeStruct((W, Hc), F32),
                         jax.ShapeDtypeStruct((Hc, DH), F32), jax.ShapeDtypeStruct((1, W), F32)), grid=(G,),
        in_specs=[pl.BlockSpec((1, n, W), lambda g: (g, 0, 0)), pl.BlockSpec((1, W), lambda g: (0, 0)),
                  pl.BlockSpec((W, Hc), lambda g: (0, 0)), pl.BlockSpec((Hc, DH), lambda g: (0, 0)),
                  pl.BlockSpec((1, n, Hc), lambda g: (g, 0, 0)), pl.BlockSpec((1, n, DH), lambda g: (g, 0, 0))],
        out_specs=(pl.BlockSpec((1, n, W), lambda g: (g, 0, 0)), pl.BlockSpec((W, Hc), lambda g: (0, 0)),
                   pl.BlockSpec((Hc, DH), lambda g: (0, 0)), pl.BlockSpec((1, W), lambda g: (0, 0))),
        compiler_params=_params(), name=name)(xb, pe.reshape(1, W), w1, w2, hid, dkc)


def _overlap_T(S):
    n_cmp, n_slc = S // 16 - 1, S // 64
    cs = np.arange(n_cmp) * 16
    ce = cs + 32
    ss = np.arange(n_slc) * 64
    se = ss + 64
    ov = np.clip(np.minimum(ce[:, None], se[None, :]) - np.maximum(cs[:, None], ss[None, :]), 0, None) / 32.0
    out = np.zeros((LANES, S // 16), np.float32)
    out[:n_slc, :n_cmp] = ov.T
    return jnp.asarray(out)


def _tri(n, upper):
    r = lax.broadcasted_iota(jnp.int32, (n, n), 0)
    c = lax.broadcasted_iota(jnp.int32, (n, n), 1)
    return ((c >= r) if upper else (c <= r)).astype(jnp.bfloat16)


def _fox_gate_fwd(zf, b, *, name):
    S, H = zf.shape
    nb = S // BQ

    def body(z_ref, b_ref, c_ref):
        tri = _tri(BQ, False)
        carry = jnp.zeros((1, H), F32)
        for i in range(nb):
            z = z_ref[i * BQ:(i + 1) * BQ, :] + b_ref[...]
            lf = jnp.minimum(z, 0.0) - jnp.log(1.0 + jnp.exp(-jnp.abs(z)))
            c_ref[i * BQ:(i + 1) * BQ, :] = _dot_exact_left(tri, lf) + carry
            carry = carry + jnp.sum(lf, axis=0, keepdims=True)

    return pl.pallas_call(body, out_shape=jax.ShapeDtypeStruct((S, H), F32), compiler_params=_params(),
                          name=name)(zf, b)


def _fox_gate_bwd(zf, b, dc, *, name):
    S, H = zf.shape
    nb = S // BQ

    def body(z_ref, b_ref, dc_ref, dz_ref, db_ref):
        tri = _tri(BQ, True)
        carry = jnp.zeros((1, H), F32)
        db = jnp.zeros((1, H), F32)
        for i in reversed(range(nb)):
            rows = slice(i * BQ, (i + 1) * BQ)
            dcb = dc_ref[rows, :]
            dlf = _dot_exact_left(tri, dcb) + carry
            carry = carry + jnp.sum(dcb, axis=0, keepdims=True)
            z = z_ref[rows, :] + b_ref[...]
            dz = dlf * jax.nn.sigmoid(-z)
            dz_ref[rows, :] = dz
            db = db + jnp.sum(dz, axis=0, keepdims=True)
        db_ref[...] = db

    return pl.pallas_call(body, out_shape=(jax.ShapeDtypeStruct((S, H), F32), jax.ShapeDtypeStruct((1, H), F32)),
                          compiler_params=_params(), name=name)(zf, b, dc)


def _from_heads(a):
    return a.transpose(1, 0, 2).reshape(a.shape[1], -1).astype(_CDT)


def _pad_lanes(a):
    return jnp.pad(a, ((0, 0), (0, LANES - a.shape[1])))


FFN_TM = 512


def _ffn_up(h, w4, *, ctx, name):
    S, D = h.shape
    C = w4.shape[2]

    def body(h_ref, wg_ref, wu_ref, gu_ref, a_ref):
        hv = h_ref[...].astype(_CDT)
        g = _dot(hv, wg_ref[...].astype(_CDT))
        u = _dot(hv, wu_ref[...].astype(_CDT))
        gu_ref[0] = g
        gu_ref[1] = u
        a_ref[...] = (g * jax.nn.sigmoid(g) * u).astype(_CDT)

    return _pcall(
        body, out_shape=(jax.ShapeDtypeStruct((2, S, 2 * C), F32), jax.ShapeDtypeStruct((S, 2 * C), _CDT)),
        grid=(S // FFN_TM, 2),
        in_specs=[pl.BlockSpec((FFN_TM, D), lambda i, j: (i, 0)), pl.BlockSpec((None, D, C), lambda i, j: (j, 0, 0)),
                  pl.BlockSpec((None, D, C), lambda i, j: (j + 2, 0, 0))],
        out_specs=(pl.BlockSpec((2, FFN_TM, C), lambda i, j: (0, i, j)), pl.BlockSpec((FFN_TM, C), lambda i, j: (i, j))),
        name=name, jobs=ctx.take(name) if ctx else (), ins=[h, w4, w4])


def _ffn_dact(dxo, wd, gu, *, ctx, name):
    S, D = dxo.shape
    C = gu.shape[2] // 2

    def body(dx_ref, wd_ref, gu_ref, d_ref):
        da = _dot_nt(dx_ref[...].astype(_CDT), wd_ref[...].astype(_CDT)) * 0.5
        g, u = gu_ref[0], gu_ref[1]
        sg = jax.nn.sigmoid(g)
        silu = g * sg
        d_ref[0] = (da * u * (sg + silu * (1.0 - sg))).astype(_CDT)
        d_ref[1] = (da * silu).astype(_CDT)

    blk = pl.BlockSpec((2, FFN_TM, C), lambda i, j: (0, i, j))
    return _pcall(body, out_shape=jax.ShapeDtypeStruct(gu.shape, _CDT), grid=(S // FFN_TM, 2),
                  in_specs=[pl.BlockSpec((FFN_TM, D), lambda i, j: (i, 0)), pl.BlockSpec((C, D), lambda i, j: (j, 0)), blk],
                  out_specs=blk, name=name, jobs=ctx.take(name) if ctx else (), ins=[dxo, wd, gu])


def _ffn_dwgu(h, dgu, *, ctx, name):
    S, D = h.shape
    C = dgu.shape[2] // 2
    tm = 512

    def body(h_ref, d_ref, o_ref):
        o_ref[...] = _dot_tn(h_ref[...].astype(_CDT), d_ref[...].astype(_CDT)).astype(_WIRE)

    return _pcall(body, out_shape=jax.ShapeDtypeStruct((4, D, C), _WIRE), grid=(D // tm, 4),
                  in_specs=[pl.BlockSpec((S, tm), lambda i, j: (0, i)),
                            pl.BlockSpec((None, S, C), lambda i, j: (j // 2, 0, j % 2))],
                  out_specs=pl.BlockSpec((None, tm, C), lambda i, j: (j, i, 0)), name=name,
                  jobs=ctx.take(name) if ctx else (), ins=[h, dgu])


def _ffn_dh(dgu, w4, *, ctx, name):
    _, S, F2 = dgu.shape
    C, D = F2 // 2, w4.shape[1]

    def body(d_ref, w_ref, o_ref, acc_ref):
        k = pl.program_id(1)
        prod = _dot_nt(d_ref[...].astype(_CDT), w_ref[...].astype(_CDT))

        @pl.when(k == 0)
        def _():
            acc_ref[...] = prod

        @pl.when(k != 0)
        def _():
            acc_ref[...] += prod

        @pl.when(k == 3)
        def _():
            o_ref[...] = acc_ref[...]

    tm = 2 * FFN_TM
    return _pcall(body, out_shape=jax.ShapeDtypeStruct((S, D), F32), grid=(S // tm, 4),
                  in_specs=[pl.BlockSpec((None, tm, C), lambda i, k: (k // 2, i, k % 2)),
                            pl.BlockSpec((None, D, C), lambda i, k: (k, 0, 0))],
                  out_specs=pl.BlockSpec((tm, D), lambda i, k: (i, 0)), scratch_shapes=[pltpu.VMEM((tm, D), F32)],
                  name=name, jobs=ctx.take(name) if ctx else (), ins=[dgu, w4])


def _ffn_fwd(x, g, P, l, tag):
    h = _rms_fwd(x, g, name=tag + "_rms")
    gu, a = _ffn_up(h, P.weight("gu", l), ctx=P.ctx, name=tag + "_gu")
    xo = _mm(a, P.weight("down", l), tm=512, tn=1024, res=x, alpha=0.5, ctx=P.ctx, name=tag + "_down")
    return xo, (x, h, gu, a)


def _ffn_bwd(dxo, saved, g, P, l, tag):
    x, h, gu, a = saved
    dgu = _ffn_dact(dxo, P.weight("down", l), gu, ctx=P.ctx, name=tag + "_dact")
    P.grad("gu", l, _ffn_dwgu(h, dgu, ctx=P.ctx, name=tag + "_dwgu"))
    dwd = _mm(a, dxo, ta=True, tm=1408, tn=1024, alpha=0.5, out_dtype=_WIRE, ctx=P.ctx, name=tag + "_dwd")
    P.grad("down", l, dwd.reshape(4, -1, D_MODEL))
    dh = _ffn_dh(dgu, P.weight("gu", l), ctx=P.ctx, name=tag + "_dh")
    dx, dg = _rms_bwd(dh, x, g, dxo, name=tag + "_drms")
    return dx, dg


def _nsa_fwd(x, g, w, cc, ss, tag):
    S = x.shape[0]
    h = _rms_fwd(x, g, name=tag + "_rms")
    hm = _mm(h, w["w_in"], tm=512, tn=896, o_lead=("heads",), ctx=w["ctx"], name=tag + "_in")
    roped = _rope(hm, cc, ss, NSA_ROTATED, name=tag + "_rope")
    z = roped[40, :, :3 * N_HEADS].reshape(S, 3, N_HEADS).transpose(1, 2, 0)[:, :, None, :]
    xbk, xbv = _make_xb(roped[16:20]), _make_xb(roped[20:24])
    kc, hidk = _compress_fwd(xbk, w["ck_pe"], w["ck_w1"], w["ck_w2"], name=tag + "_ck")
    vc, hidv = _compress_fwd(xbv, w["cv_pe"], w["cv_w1"], w["cv_w2"], name=tag + "_cv")
    ctx, GR = w["ctx"], dict(G=NSA_G, R=NSA_R)
    o0, lse0, selT = _attn_fwd(roped, kc, vc, kind="cmp", ovT=_overlap_T(S), ctx=ctx, name=tag + "_cmp", **GR)
    o1, lse1 = _attn_fwd(roped, roped, roped, heads=(0, 24, 28), kind="sel", selT=selT, ctx=ctx, name=tag + "_slc", **GR)
    o2, lse2 = _attn_fwd(roped, roped, roped, heads=(0, 32, 36), kind="band", window=NSA_WINDOW, ctx=ctx,
                         name=tag + "_win", **GR)
    o = _combine(o0.reshape(N_HEADS, S, DH), o1.reshape(N_HEADS, S, DH), o2.reshape(N_HEADS, S, DH), z,
                 name=tag + "_mix")
    of = _from_heads(o)
    xo = _mm(of, w["out"], tm=512, tn=1024, res=x, ctx=w["ctx"], name=tag + "_out")
    saved = (x, h, roped, z, xbk, xbv, hidk, hidv, kc, vc, (o0, o1, o2), (lse0, lse1, lse2), selT, of)
    return xo, saved


def _out_bwd(dxo, of, w, tag):
    dof = _mm(dxo, w["out"], tb=True, tm=512, tn=1024, o_lead=("heads",), ctx=w["ctx"], name=tag + "_dof")
    dw = _mm(of, dxo, ta=True, tm=512, tn=1024, out_dtype=_WIRE, ctx=w["ctx"], name=tag + "_dwout")
    w["P"].grad("out", w["out_l"], dw.reshape(4, -1, D_MODEL))
    return dof


def _split_cols(dw, n_in):
    cs = n_in // 4
    return dw[:, :n_in].reshape(D_MODEL, 4, cs).transpose(1, 0, 2).astype(_WIRE)


def _nsa_bwd(dxo, saved, g, w, cc, ss, tag):
    x, h, roped, z, xbk, xbv, hidk, hidv, kc, vc, os_, lses, selT, of = saved
    S = x.shape[0]
    do = _out_bwd(dxo, of, w, tag).reshape(NSA_G, NSA_R, S, DH)
    zg = z.reshape(3, NSA_G, NSA_R, 1, S)
    ctx = w["ctx"]
    b0 = _attn_bwd(roped, kc, vc, os_[0], do, lses[0], kind="cmp", gate=zg[0], ctx=ctx, name=tag + "_dcmp")
    b1 = _attn_bwd(roped, roped, roped, os_[1], do, lses[1], heads=(0, 24, 28), kind="sel", selT=selT, gate=zg[1],
                   ctx=ctx, name=tag + "_dslc")
    b2 = _attn_bwd(roped, roped, roped, os_[2], do, lses[2], heads=(0, 32, 36), kind="band", window=NSA_WINDOW,
                   gate=zg[2], ctx=ctx, name=tag + "_dwin")
    dxbk, dck_w1, dck_w2, dck_pe = _compress_bwd(xbk, w["ck_pe"], w["ck_w1"], w["ck_w2"], hidk, b0["dk"],
                                                 name=tag + "_dck")
    dxbv, dcv_w1, dcv_w2, dcv_pe = _compress_bwd(xbv, w["cv_pe"], w["cv_w1"], w["cv_w2"], hidv, b0["dv"],
                                                 name=tag + "_dcv")
    dk0 = _unmake_xb(dxbk, name=tag + "_dk0")
    dv0 = _unmake_xb(dxbv, name=tag + "_dv0")
    dq = _addn(b0["dq"], b1["dq"], b2["dq"], name=tag + "_dqsum").reshape(N_HEADS, S, DH)
    dz = jnp.stack([b0["dz"], b1["dz"], b2["dz"]], axis=0).reshape(3 * N_HEADS, S).T
    dgates = jnp.pad(dz, ((0, 0), (0, 2 * DH - 3 * N_HEADS))).reshape(S, 2, DH).transpose(1, 0, 2)
    dhm = jnp.concatenate([dq, dk0, dv0, b1["dk"], b1["dv"], b2["dk"], b2["dv"], dgates], axis=0)
    dproj = _from_heads(_rope(dhm, cc, -ss, NSA_ROTATED, name=tag + "_drope"))
    dh = _mm(dproj, w["w_in"], tb=True, tm=512, tn=512, ctx=w["ctx"], name=tag + "_dh")
    dw_in = _mm(h, dproj, ta=True, tm=512, tn=896, ctx=w["ctx"], name=tag + "_dwin_w")
    dx, dg = _rms_bwd(dh, x, g, dxo, name=tag + "_drms")
    P, j = w["P"], w["j"]
    P.grad("nsa_in", j, _split_cols(dw_in, NSA_IN))
    P.grad("cw1", j, dck_w1.astype(_WIRE).reshape(4, -1, dck_w1.shape[1]))
    P.grad("cw1", 2 + j, dcv_w1.astype(_WIRE).reshape(4, -1, dcv_w1.shape[1]))
    grads = dict(ck_pe=dck_pe.reshape(32, DH), ck_w2=dck_w2, cv_pe=dcv_pe.reshape(32, DH), cv_w2=dcv_w2)
    return dx, dg, grads


def _swa_fwd(x, g, w, cc, ss, tag):
    S = x.shape[0]
    h = _rms_fwd(x, g, name=tag + "_rms")
    hm = _mm(h, w["w_in"], tm=512, tn=640, o_lead=("heads",), ctx=w["ctx"], name=tag + "_in")
    roped = _rope(hm, cc, ss, SWA_ROTATED, name=tag + "_rope")
    sinks = jnp.broadcast_to(w["sinks"].reshape(SWA_G, SWA_R, 1, 1), (SWA_G, SWA_R, 8, LANES))
    o, lse = _attn_fwd(roped, roped, roped, G=SWA_G, R=SWA_R, heads=SWA_HEADS, kind="band", window=SWA_WINDOW,
                       sinks=sinks, ctx=w["ctx"], name=tag + "_attn")
    of = _from_heads(o.reshape(N_HEADS, S, DH))
    xo = _mm(of, w["out"], tm=512, tn=1024, res=x, ctx=w["ctx"], name=tag + "_out")
    return xo, (x, h, roped, sinks, o, lse, of)


def _swa_bwd(dxo, saved, g, w, cc, ss, tag):
    x, h, roped, sinks, o, lse, of = saved
    S = x.shape[0]
    do = _out_bwd(dxo, of, w, tag).reshape(SWA_G, SWA_R, S, DH)
    b = _attn_bwd(roped, roped, roped, o, do, lse, heads=SWA_HEADS, kind="band", window=SWA_WINDOW, sinks=sinks,
                  ctx=w["ctx"], name=tag + "_dattn")
    dhm = jnp.concatenate([b["dq"].reshape(N_HEADS, S, DH), b["dk"], b["dv"]], axis=0)
    dproj = _from_heads(_rope(dhm, cc, -ss, SWA_ROTATED, name=tag + "_drope"))
    dh = _mm(dproj, w["w_in"], tb=True, tm=512, tn=512, ctx=w["ctx"], name=tag + "_dh")
    dw_in = _mm(h, dproj, ta=True, tm=512, tn=640, ctx=w["ctx"], name=tag + "_dwin_w")
    dx, dg = _rms_bwd(dh, x, g, dxo, name=tag + "_drms")
    w["P"].grad("swa_in", w["j"], _split_cols(dw_in, SWA_IN))
    return dx, dg, dict(sinks=b["dsink"][:, :, 0, 0].reshape(N_HEADS))


def _fox_fwd(x, g, w, tag):
    S = x.shape[0]
    h = _rms_fwd(x, g, name=tag + "_rms")
    hm = _mm(h, w["w_in"], tm=512, tn=640, o_lead=("heads",), ctx=w["ctx"], name=tag + "_in")
    zf = jnp.pad(hm[48], ((0, 0), (0, LANES - DH)))
    bf = _pad_lanes(w["b_f"].reshape(1, N_HEADS))
    c = _fox_gate_fwd(zf, bf, name=tag + "_gate")[:, :N_HEADS]
    bias = c.T[:, None, :]
    o, lse = _attn_fwd(hm, hm, hm, G=N_HEADS, R=1, heads=FOX_HEADS, kind="causal", bias=bias, ctx=w["ctx"],
                       name=tag + "_attn")
    of = _from_heads(o.reshape(N_HEADS, S, DH))
    xo = _mm(of, w["out"], tm=512, tn=1024, res=x, ctx=w["ctx"], name=tag + "_out")
    return xo, (x, h, hm, zf, bf, bias, o, lse, of)


def _fox_bwd(dxo, saved, g, w, tag):
    x, h, hm, zf, bf, bias, o, lse, of = saved
    S = x.shape[0]
    do = _out_bwd(dxo, of, w, tag).reshape(N_HEADS, 1, S, DH)
    b = _attn_bwd(hm, hm, hm, o, do, lse, heads=FOX_HEADS, kind="causal", bias=bias, ctx=w["ctx"], name=tag + "_dattn")
    dzf, db = _fox_gate_bwd(zf, bf, _pad_lanes(b["dc"].reshape(N_HEADS, S).T), name=tag + "_dgate")
    dgates = dzf.reshape(S, 2, DH).transpose(1, 0, 2)
    dhm = jnp.concatenate([b["dq"].reshape(N_HEADS, S, DH), b["dk"], b["dv"], dgates], axis=0)
    dproj = _from_heads(dhm)
    dh = _mm(dproj, w["w_in"], tb=True, tm=512, tn=512, ctx=w["ctx"], name=tag + "_dh")
    dw_in = _mm(h, dproj, ta=True, tm=512, tn=640, ctx=w["ctx"], name=tag + "_dwin_w")
    dx, dg = _rms_bwd(dh, x, g, dxo, name=tag + "_drms")
    w["P"].grad("fox_in", w["j"], _split_cols(dw_in, FOX_IN))
    return dx, dg, dict(b_f=db[0, :N_HEADS])


GROUPS = {
    "gu": (("ffn1_w_gu", "ffn2_w_gu"), 2),
    "down": (("ffn1_w_down", "ffn2_w_down"), 1),
    "out": (("nsa_w_out", "swa_w_out", "fox_w_out"), 1),
    "cw1": (("nsa_ck_w1", "nsa_cv_w1"), 1),
    "nsa_in": (("nsa_w_in",), 2),
    "swa_in": (("swa_w_in",), 2),
    "fox_in": (("fox_w_in",), 2),
}
OUT_SLAB = {0: 0, 3: 1, 1: 2, 2: 3}


def _pieces_in_order():
    chunks = []
    for i in range(DEPTH):
        kind, j = i % 3, i // 3
        chunks.append([("gu", i), ("down", i)])
        if kind == 0:
            chunks.append([("nsa_in", j), ("cw1", j), ("cw1", 2 + j), ("out", OUT_SLAB[i])])
        else:
            chunks.append([("swa_in" if kind == 1 else "fox_in", j), ("out", OUT_SLAB[i])])
        chunks.append([("gu", DEPTH + i), ("down", DEPTH + i)])
    return chunks


def _consumer_layout(group, F):
    _, rows, C = F.shape
    if group == "gu":
        return F
    if GROUPS[group][1] == 1:
        return F.reshape(4 * rows, C)
    w = F.transpose(1, 0, 2).reshape(rows, 4 * C)
    pad = {"nsa_in": NSA_IN_PAD, "swa_in": SWA_IN, "fox_in": FOX_IN_PAD}[group] - 4 * C
    return jnp.pad(w, ((0, 0), (0, pad)))


class _MixerWeights(dict):
    def __init__(self, P, pieces, **given):
        super().__init__(P=P, ctx=P.ctx, **given)
        self.pieces = pieces

    def __missing__(self, key):
        self[key] = self["P"].weight(*self.pieces[key])
        return self[key]


def _mixer_weights(P, i):
    kind, j = i % 3, i // 3
    out = {"out": ("out", OUT_SLAB[i])}
    if kind == 0:
        small = {k: P.small["nsa_" + k][j] for k in ("ck_pe", "ck_w2", "cv_pe", "cv_w2")}
        return _MixerWeights(P, dict(out, w_in=("nsa_in", j), ck_w1=("cw1", j), cv_w1=("cw1", 2 + j)), j=j,
                             out_l=OUT_SLAB[i], **small)
    if kind == 1:
        return _MixerWeights(P, dict(out, w_in=("swa_in", j)), j=j, out_l=OUT_SLAB[i], sinks=P.small["swa_sinks"][j])
    return _MixerWeights(P, dict(out, w_in=("fox_in", j)), j=j, out_l=OUT_SLAB[i], b_f=P.small["fox_b_f"][j])


def _local_step(x, tgt, P):
    S = x.shape[0]
    cc, ss = _rope_tables(S)
    sm = P.small
    saved = []
    for i in range(DEPTH):
        kind = i % 3
        x, s1 = _ffn_fwd(x, sm["ffn1_norm"][i], P, i, f"l{i}f1")
        mw = _mixer_weights(P, i)
        if kind == 0:
            x, s2 = _nsa_fwd(x, sm["mix_norm"][i], mw, cc, ss, f"l{i}nsa")
        elif kind == 1:
            x, s2 = _swa_fwd(x, sm["mix_norm"][i], mw, cc, ss, f"l{i}swa")
        else:
            x, s2 = _fox_fwd(x, sm["mix_norm"][i], mw, f"l{i}fox")
        x, s3 = _ffn_fwd(x, sm["ffn2_norm"][i], P, DEPTH + i, f"l{i}f2")
        saved.append((s1, mw, s2, s3))
    loss, dx, d_final = _loss_head(x, sm["final_norm"], tgt, name="loss_head")

    norms = {k: [None] * DEPTH for k in ("ffn1_norm", "mix_norm", "ffn2_norm")}
    mix = {}
    for i in reversed(range(DEPTH)):
        kind, j = i % 3, i // 3
        s1, mw, s2, s3 = saved[i]
        dx, norms["ffn2_norm"][i] = _ffn_bwd(dx, s3, sm["ffn2_norm"][i], P, DEPTH + i, f"l{i}f2")
        if kind == 0:
            dx, dg, gm = _nsa_bwd(dx, s2, sm["mix_norm"][i], mw, cc, ss, f"l{i}nsa")
            pre = "nsa_"
        elif kind == 1:
            dx, dg, gm = _swa_bwd(dx, s2, sm["mix_norm"][i], mw, cc, ss, f"l{i}swa")
            pre = "swa_"
        else:
            dx, dg, gm = _fox_bwd(dx, s2, sm["mix_norm"][i], mw, f"l{i}fox")
            pre = "fox_"
        norms["mix_norm"][i] = dg
        for k, val in gm.items():
            mix.setdefault(pre + k, {})[j] = val
        dx, norms["ffn1_norm"][i] = _ffn_bwd(dx, s1, sm["ffn1_norm"][i], P, i, f"l{i}f1")
    small = {k: jnp.stack(v, axis=0) for k, v in norms.items()}
    small.update({k: jnp.stack([d[j] for j in sorted(d)], axis=0) for k, d in mix.items()})
    small["final_norm"] = d_final
    return loss, dx, small


def _sum_slots(own, recv, me, into, row0, *, name):
    _, R, C = recv.shape
    tr = _row_tile(math.gcd(R, row0), C)
    blk0 = row0 // tr

    def body(_, g_ref, r_ref, __, o_ref):
        acc = g_ref[...].astype(F32) + r_ref[0].astype(F32)
        acc = acc + r_ref[1].astype(F32)
        o_ref[...] = acc + r_ref[2].astype(F32)

    spec = pltpu.PrefetchScalarGridSpec(
        num_scalar_prefetch=1, grid=(R // tr,),
        in_specs=[pl.BlockSpec((None, tr, C), lambda i, me_ref: (me_ref[0], i, 0)),
                  pl.BlockSpec((3, tr, C), lambda i, me_ref: (0, i, 0)), _ANY],
        out_specs=pl.BlockSpec((tr, C), lambda i, me_ref: (blk0 + i, 0)))
    return pl.pallas_call(body, grid_spec=spec, out_shape=jax.ShapeDtypeStruct(into.shape, F32),
                          input_output_aliases={3: 0}, compiler_params=_params(), name=name)(me, own, recv, into)


def _swap_sibling(parts):
    n = len(parts)

    def body(*refs):
        srcs, outs = refs[:n], refs[n:2 * n]
        send_sems, recv_sems = refs[2 * n:]
        x, y, c = lax.axis_index("x"), lax.axis_index("y"), lax.axis_index("c")
        cps = [pltpu.make_async_remote_copy(src_ref=srcs[g], dst_ref=outs[g], send_sem=send_sems.at[g],
                                            recv_sem=recv_sems.at[g], device_id=(x, y, 1 - c), device_id_type=MESH)
               for g in range(n)]
        for cp in cps:
            cp.start()
        for cp in cps:
            cp.wait()

    return pl.pallas_call(
        body, out_shape=tuple(jax.ShapeDtypeStruct(p.shape, p.dtype) for p in parts),
        in_specs=[_ANY] * n, out_specs=(_ANY,) * n,
        scratch_shapes=[pltpu.SemaphoreType.DMA((n,)), pltpu.SemaphoreType.DMA((n,))],
        compiler_params=pltpu.CompilerParams(has_side_effects=True), name="swap_core_grads")(*parts)


def _flip(coord, bit):
    return 1 - coord if bit else coord


def _allreduce_small(v):
    n, C = v.shape

    def body(v_ref, o_ref, buf, send_sems, recv_sems):
        x, y, c = lax.axis_index("x"), lax.axis_index("y"), lax.axis_index("c")
        me = 4 * x + 2 * y + c
        buf[me] = v_ref[...]
        peers = [(_flip(x, (j >> 2) & 1), _flip(y, (j >> 1) & 1), _flip(c, j & 1)) for j in range(1, 8)]
        sends = [pltpu.make_async_remote_copy(src_ref=v_ref, dst_ref=buf.at[me], send_sem=send_sems.at[j],
                                              recv_sem=recv_sems.at[j], device_id=peer, device_id_type=MESH)
                 for j, peer in enumerate(peers)]
        for cp in sends:
            cp.start()
        for j, (px, py, pc) in enumerate(peers):
            pltpu.make_async_remote_copy(src_ref=v_ref, dst_ref=buf.at[4 * px + 2 * py + pc], send_sem=send_sems.at[j],
                                         recv_sem=recv_sems.at[j], device_id=(px, py, pc),
                                         device_id_type=MESH).wait_recv()
        for cp in sends:
            cp.wait_send()
        acc = buf[0]
        for d in range(1, 8):
            acc = acc + buf[d]
        o_ref[...] = acc

    return pl.pallas_call(
        body, out_shape=jax.ShapeDtypeStruct((n, C), F32),
        in_specs=[pl.BlockSpec(memory_space=pltpu.VMEM)], out_specs=pl.BlockSpec(memory_space=pltpu.VMEM),
        scratch_shapes=[pltpu.VMEM((8, n, C), F32), pltpu.SemaphoreType.DMA((7,)), pltpu.SemaphoreType.DMA((7,))],
        compiler_params=pltpu.CompilerParams(has_side_effects=True), name="allreduce_small")(v)


def _adamw(w, m, v, gs, *, row0=0, name):
    shape = w.shape
    w3, m3, v3 = (a.reshape((-1,) + a.shape[-2:]) for a in (w, m, v))
    g2 = [_rows2d(g) for g in gs]
    L, rows, C = w3.shape
    tr = _row_tile(math.gcd(rows, row0), C, budget=1024 * 1024)
    ng = len(g2)
    blk0, nb = row0 // tr, rows // tr

    def body(*refs):
        w_ref, m_ref, v_ref = refs[:3]
        g = refs[3][...]
        for r in refs[4:3 + ng]:
            g = g + r[...]
        g_ref, d_ref, nm_ref, nv_ref = refs[3 + ng:]
        mn = B1 * m_ref[...] + (1.0 - B1) * g
        vn = B2 * v_ref[...] + (1.0 - B2) * (g * g)
        m_hat = mn / (1.0 - B1 ** STEP)
        v_hat = vn / (1.0 - B2 ** STEP)
        g_ref[...] = g
        d_ref[...] = -LR * (m_hat / (jnp.sqrt(v_hat) + EPS) + WD * w_ref[...])
        nm_ref[...] = mn
        nv_ref[...] = vn

    spec = pl.BlockSpec((None, tr, C), lambda l, i: (l, i, 0))
    gspec = pl.BlockSpec((tr, C), lambda l, i: (blk0 + l * nb + i, 0))
    outs = pl.pallas_call(body, out_shape=tuple(jax.ShapeDtypeStruct((L, rows, C), F32) for _ in range(4)),
                          grid=(L, nb), in_specs=[spec] * 3 + [gspec] * ng, out_specs=(spec,) * 4,
                          compiler_params=_params(), name=name)(w3, m3, v3, *g2)
    return tuple(o.reshape(shape) for o in outs)


def _small_layout(shapes):
    offs, off = {}, 0
    for k in REPLICATED:
        n = int(np.prod(shapes[k]))
        offs[k] = (off, n)
        off += -(-n // LANES) * LANES
    return offs, off


def _pack_small(d, shapes):
    offs, total = _small_layout(shapes)
    parts = []
    for k in REPLICATED:
        n = offs[k][1]
        parts.append(jnp.pad(d[k].reshape(-1).astype(F32), (0, -(-n // LANES) * LANES - n)))
    rows = -(-(total // LANES) // 8) * 8
    return jnp.pad(jnp.concatenate(parts), (0, rows * LANES - total)).reshape(rows, LANES)


def _unpack_small(a, shapes):
    offs, _ = _small_layout(shapes)
    flat = a.reshape(-1)
    return {k: flat[offs[k][0]:offs[k][0] + offs[k][1]].reshape(shapes[k]) for k in REPLICATED}


def _group_shards(w):
    shards = []
    for members, _ in GROUPS.values():
        s = jnp.concatenate([w[k].astype(_WIRE) for k in members], axis=0)
        shards.append(s.reshape(s.shape[0], 2, s.shape[1] // 2, s.shape[2]))
    return shards


class _Exchanged:
    def __init__(self, w):
        self.small = {k: w[k] for k in REPLICATED}
        self.ctx = _Sched()
        shards = dict(zip(GROUPS, _group_shards(w)))
        self.gather, self.cache, self.scatter = {}, {}, {}
        chunks = _pieces_in_order()
        self.me = jnp.reshape(2 * lax.axis_index("x") + lax.axis_index("y"), (1,)).astype(jnp.int32)
        first = {(g, l): _gather_task1(shards[g], l, _place_own(shards[g], l, self.me, name=f"own_{g}{l}"))
                 for chunk in chunks for g, l in chunk}
        self.gather = {p: _gather_task2(task) for p, task in first.items()}
        pieces = [p for chunk in chunks for p in chunk]
        for n, p in enumerate(pieces):
            self.ctx.push(first[p])
            if n:
                self.ctx.push(self.gather[pieces[n - 1]])
        self.ctx.push(self.gather[pieces[-1]])

    def weight(self, group, l):
        if (group, l) not in self.cache:
            task = self.gather[group, l]
            self.ctx.finish(task)
            F = task.result()
            self.cache[group, l] = _consumer_layout(group, F.reshape(4, -1, F.shape[-1]))
        return self.cache[group, l]

    def grad(self, group, l, G):
        self.scatter[group, l] = (G, self.ctx.push(_scatter_task(G)))

    def partial_sums(self, w):
        self.ctx.finish()
        parts = []
        for group, (members, _) in GROUPS.items():
            rows, C = w[members[0]].shape[1:]
            n = sum(w[k].shape[0] for k in members)
            part = jnp.zeros((n * rows, C), F32)
            for l in range(n):
                G, task = self.scatter[group, l]
                part = _sum_slots(G, task.result(), self.me, part, l * rows, name=f"sum_{group}{l}")
            parts.append(part)
        return parts


def _reduce_and_update(w, m, v, parts, small):
    others = _swap_sibling(parts)
    small_shapes = {k: w[k].shape for k in REPLICATED}
    g_small = _unpack_small(_allreduce_small(_pack_small(small, small_shapes)), small_shapes)

    out_g, out_d, out_m, out_v = {}, {}, {}, {}
    for (members, _), part, other in zip(GROUPS.values(), parts, others):
        row0 = 0
        for k in members:
            out_g[k], out_d[k], out_m[k], out_v[k] = _adamw(w[k], m[k], v[k], [part, other], row0=row0,
                                                            name="adamw_" + k)
            row0 += w[k].shape[0] * w[k].shape[1]
    sm = _adamw(_pack_small(w, small_shapes), _pack_small(m, small_shapes), _pack_small(v, small_shapes),
                [_pack_small(g_small, small_shapes)], name="adamw_small")
    for d, packed in zip((out_g, out_d, out_m, out_v), sm):
        d.update(_unpack_small(packed, small_shapes))
    return out_g, out_d, out_m, out_v


def kernel(x, ffn1_norm, ffn1_w_gu, ffn1_w_down, mix_norm, ffn2_norm, ffn2_w_gu, ffn2_w_down, nsa_w_in, nsa_ck_pe, nsa_ck_w1, nsa_ck_w2, nsa_cv_pe, nsa_cv_w1, nsa_cv_w2, nsa_w_out, swa_w_in, swa_sinks, swa_w_out, fox_w_in, fox_b_f, fox_w_out, final_norm, loss_target, m_ffn1_norm, m_ffn1_w_gu, m_ffn1_w_down, m_mix_norm, m_ffn2_norm, m_ffn2_w_gu, m_ffn2_w_down, m_nsa_w_in, m_nsa_ck_pe, m_nsa_ck_w1, m_nsa_ck_w2, m_nsa_cv_pe, m_nsa_cv_w1, m_nsa_cv_w2, m_nsa_w_out, m_swa_w_in, m_swa_sinks, m_swa_w_out, m_fox_w_in, m_fox_b_f, m_fox_w_out, m_final_norm, v_ffn1_norm, v_ffn1_w_gu, v_ffn1_w_down, v_mix_norm, v_ffn2_norm, v_ffn2_w_gu, v_ffn2_w_down, v_nsa_w_in, v_nsa_ck_pe, v_nsa_ck_w1, v_nsa_ck_w2, v_nsa_cv_pe, v_nsa_cv_w1, v_nsa_cv_w2, v_nsa_w_out, v_swa_w_in, v_swa_sinks, v_swa_w_out, v_fox_w_in, v_fox_b_f, v_fox_w_out, v_final_norm):
    args = dict(locals())
    w = {k: args[k] for k in WEIGHTS}
    m = {k: args["m_" + k] for k in WEIGHTS}
    v = {k: args["v_" + k] for k in WEIGHTS}
    P = _Exchanged(w)
    loss_part, dx, small = _local_step(x[0], loss_target[0], P)
    loss = lax.psum(loss_part, ("x", "y", "c"))
    out_g, out_d, out_m, out_v = _reduce_and_update(w, m, v, P.partial_sums(w), small)
    return (loss, dx[None], *[out_g[k] for k in WEIGHTS], *[out_d[k] for k in WEIGHTS],
            *[out_m[k] for k in WEIGHTS], *[out_v[k] for k in WEIGHTS])
```

```python
import math

import numpy as np
import jax
import jax.numpy as jnp
from jax import lax
from jax.experimental import pallas as pl
from jax.experimental.pallas import tpu as pltpu

F32 = jnp.float32
_CDT = jnp.bfloat16
_WIRE = jnp.bfloat16
_VMEM_LIMIT = 56 * 1024 * 1024

D_MODEL = 1024
DEPTH = 4
DH = 64
N_HEADS = 16
RMS_EPS = 1e-6
NEG = -1e30
SCALE = DH ** -0.5
BQ = 256
LANES = 128
NSA_G, NSA_R = 4, 4
NSA_WINDOW = 512
NSA_TOPK = 16
NSA_BONUS = 1e4
SWA_G, SWA_R = 2, 8
SWA_WINDOW = 128
NSA_ROTATED = tuple(hd < 16 or (hd < 40 and (hd - 16) % 8 < 4) for hd in range(42))
NSA_IN, NSA_IN_PAD = 2608, 2688
SWA_IN = 1280
SWA_ROTATED = (True,) * 18 + (False,) * 2
SWA_HEADS = (0, 16, 18)
FOX_IN, FOX_IN_PAD = 3088, 3200
FOX_HEADS = (0, 16, 32)
LR, B1, B2, EPS, WD, STEP = 0.001, 0.9, 0.999, 1e-08, 0.01, 10
MESH = pl.DeviceIdType.MESH

REPLICATED = ["ffn1_norm", "mix_norm", "ffn2_norm", "nsa_ck_pe", "nsa_ck_w2", "nsa_cv_pe", "nsa_cv_w2",
              "swa_sinks", "fox_b_f", "final_norm"]
WEIGHTS = ['ffn1_norm', 'ffn1_w_gu', 'ffn1_w_down', 'mix_norm', 'ffn2_norm', 'ffn2_w_gu', 'ffn2_w_down',
           'nsa_w_in', 'nsa_ck_pe', 'nsa_ck_w1', 'nsa_ck_w2', 'nsa_cv_pe', 'nsa_cv_w1', 'nsa_cv_w2', 'nsa_w_out',
           'swa_w_in', 'swa_sinks', 'swa_w_out', 'fox_w_in', 'fox_b_f', 'fox_w_out', 'final_norm']


def _params(**kw):
    return pltpu.CompilerParams(vmem_limit_bytes=_VMEM_LIMIT, **kw)


def _dot(a, b):
    return lax.dot_general(a, b, (((1,), (0,)), ((), ())), preferred_element_type=F32)


def _dot_nt(a, b):
    return lax.dot_general(a, b, (((1,), (1,)), ((), ())), preferred_element_type=F32)


def _dot_tn(a, b):
    return lax.dot_general(a, b, (((0,), (0,)), ((), ())), preferred_element_type=F32)


def _split3(x):
    hi = x.astype(jnp.bfloat16)
    r1 = x - hi.astype(F32)
    mid = r1.astype(jnp.bfloat16)
    lo = (r1 - mid.astype(F32)).astype(jnp.bfloat16)
    return hi, mid, lo


def _dot_exact(x, p):
    hi, mid, lo = _split3(x)
    return _dot(hi, p) + _dot(mid, p) + _dot(lo, p)


def _dot_exact_left(p, x):
    hi, mid, lo = _split3(x)
    return _dot(p, hi) + _dot(p, mid) + _dot(p, lo)


_ANY = pl.BlockSpec(memory_space=pl.ANY)


def _chip_peers():
    x, y, c = lax.axis_index("x"), lax.axis_index("y"), lax.axis_index("c")
    return x, y, c, [(1 - x, y), (x, 1 - y), (1 - x, 1 - y)]


class _Job:
    def __init__(self, ins, outs, nsem, copies, cost, alias):
        self.ins, self.outs, self.nsem, self.copies, self.cost, self.alias = ins, outs, nsem, copies, cost, alias
        self.results = None

    def inputs(self):
        return self.ins


class _Task:
    def __init__(self, src, out, parts, init=None):
        self.src, self.out, self.parts, self.init, self.done, self.jobs = src, out, parts, init, 0, []

    def result(self):
        return self.jobs[-1].results[0]

    def complete(self):
        return self.done == len(self.parts) and self.jobs[-1].results is not None

    def ready(self):
        if isinstance(self.src, _Task) and not self.src.complete():
            return False
        return not self.jobs or self.jobs[-1].results is not None

    def next_job(self, n):
        parts = self.parts[self.done:self.done + n]
        in_place = isinstance(self.src, _Task)
        ins = [] if in_place else [self.src]
        prev = self.jobs[-1].results[0] if self.jobs else (self.src.result() if in_place else self.init)
        alias = {}
        if prev is not None:
            alias = {len(ins): 0}
            ins = ins + [prev]

        def copies(in_refs, out_refs, sems):
            src = out_refs[0] if in_place else in_refs[0]
            out, first = [], 0
            for build, nsem, _ in parts:
                out += build(src, out_refs[0], sems, first)
                first += nsem
            return out

        job = _Job(ins, [self.out], sum(p[1] for p in parts), copies, sum(p[2] for p in parts), alias)
        self.done += n
        self.jobs.append(job)
        return job


def _pcall(body, *, out_shape, in_specs, out_specs, name, grid=(), scratch_shapes=(), aliases=None, jobs=(), ins):
    single = not isinstance(out_shape, (tuple, list))
    out_shape = (out_shape,) if single else tuple(out_shape)
    out_specs = (out_specs,) if single else tuple(out_specs)
    n_in, n_out, n_scr = len(ins), len(out_shape), len(scratch_shapes)
    side_ins = [a for j in jobs for a in j.inputs()]
    side_outs = [o for j in jobs for o in j.outs]
    aliases = dict(aliases or {})
    i0, o0 = n_in, n_out
    for j in jobs:
        for a, b in j.alias.items():
            aliases[i0 + a] = o0 + b
        i0 += len(j.ins)
        o0 += len(j.outs)

    def body2(*refs):
        in_refs, s_in = refs[:n_in], refs[n_in:n_in + len(side_ins)]
        r = n_in + len(side_ins)
        out_refs, s_out = refs[r:r + n_out], refs[r + n_out:r + n_out + len(side_outs)]
        r += n_out + len(side_outs)
        scr, sems = refs[r:r + n_scr], refs[r + n_scr:]
        def descriptors():
            out, a, b = [], 0, 0
            for j, sem in zip(jobs, sems):
                out += j.copies(s_in[a:a + len(j.ins)], s_out[b:b + len(j.outs)], sem)
                a += len(j.ins)
                b += len(j.outs)
            return out

        def at_step(steps, fn):
            cond = None
            for d, s in enumerate(steps):
                hit = pl.program_id(d) == s
                cond = hit if cond is None else jnp.logical_and(cond, hit)
            if cond is None:
                fn()
            else:
                pl.when(cond)(fn)

        def start_all():
            for cp in descriptors():
                cp.start()

        def wait_all():
            for cp in descriptors():
                cp.wait()

        if jobs:
            at_step([0] * len(grid), start_all)
        body(*in_refs, *out_refs, *scr)
        if jobs:
            at_step([n - 1 for n in grid], wait_all)

    res = pl.pallas_call(
        body2, out_shape=out_shape + tuple(side_outs), grid=grid, in_specs=list(in_specs) + [_ANY] * len(side_ins),
        out_specs=out_specs + (_ANY,) * len(side_outs),
        scratch_shapes=list(scratch_shapes) + [pltpu.SemaphoreType.DMA((j.nsem,)) for j in jobs],
        input_output_aliases=aliases,
        compiler_params=_params(has_side_effects=True) if jobs else _params(), name=name)(*ins, *side_ins)
    b = n_out
    for j in jobs:
        j.results = list(res[b:b + len(j.outs)])
        b += len(j.outs)
    return res[0] if single else tuple(res[:n_out])


def _comm_call(jobs, name):
    def body():
        pass

    _pcall(body, out_shape=(), in_specs=[], out_specs=(), name=name, jobs=jobs, ins=[])


def _place_own(shard, l, me, *, name):
    _, _, h, C = shard.shape

    def body(_, s_ref, o_ref):
        o_ref[...] = s_ref[...]

    spec = pltpu.PrefetchScalarGridSpec(
        num_scalar_prefetch=1, grid=(2,),
        in_specs=[pl.BlockSpec((None, None, h, C), lambda i, me_ref: (l, i, 0, 0))],
        out_specs=pl.BlockSpec((None, None, h, C), lambda i, me_ref: (me_ref[0], i, 0, 0)))
    return pl.pallas_call(body, grid_spec=spec, out_shape=jax.ShapeDtypeStruct((4, 2, h, C), shard.dtype),
                          compiler_params=_params(), name=name)(me, shard)


def _gather_task1(shard, l, own):
    _, _, h, C = shard.shape
    mb = 3 * h * C * jnp.dtype(_WIRE).itemsize / 1e6
    n = _row_splits(h, ICI_US_PER_MB * mb)
    hr = h // n

    def rows_to_peers(r):
        def build(src, dst, sems, s0):
            x, y, c, chips = _chip_peers()
            me, rows = 2 * x + y, pl.ds(r * hr, hr)
            return [pltpu.make_async_remote_copy(src_ref=src.at[l, c, rows], dst_ref=dst.at[me, c, rows],
                                                 send_sem=sems.at[s0 + 2 * j], recv_sem=sems.at[s0 + 2 * j + 1],
                                                 device_id=(px, py, c), device_id_type=MESH)
                    for j, (px, py) in enumerate(chips)]
        return build

    parts = [(rows_to_peers(r), 6, ICI_US_PER_MB * mb / n) for r in range(n)]
    return _Task(shard, jax.ShapeDtypeStruct((4, 2, h, C), shard.dtype), parts, init=own)


def _gather_task2(task1):
    _, _, h, C = task1.out.shape
    mb = h * C * jnp.dtype(_WIRE).itemsize / 1e6

    def forward(j):
        def build(_, dst, sems, s0):
            x, y, c, chips = _chip_peers()
            px, py = chips[j]
            landed = dst.at[2 * px + py, c]
            return [pltpu.make_async_remote_copy(src_ref=landed, dst_ref=landed, send_sem=sems.at[s0],
                                                 recv_sem=sems.at[s0 + 1], device_id=(x, y, 1 - c),
                                                 device_id_type=MESH)]
        return build

    return _Task(task1, task1.out, [(forward(j), 2, 1.0 + D2D_US_PER_MB * mb) for j in range(3)])


def _scatter_task(G):
    mb = 3 * G.shape[1] * G.shape[2] * jnp.dtype(_WIRE).itemsize / 1e6
    n = _row_splits(G.shape[1], ICI_US_PER_MB * mb)
    rr = G.shape[1] // n

    def rows_to_peers(r):
        def build(src, dst, sems, s0):
            _, _, c, chips = _chip_peers()
            rows = pl.ds(r * rr, rr)
            return [pltpu.make_async_remote_copy(src_ref=src.at[2 * px + py, rows], dst_ref=dst.at[j, rows],
                                                 send_sem=sems.at[s0 + 2 * j], recv_sem=sems.at[s0 + 2 * j + 1],
                                                 device_id=(px, py, c), device_id_type=MESH)
                    for j, (px, py) in enumerate(chips)]
        return build

    parts = [(rows_to_peers(r), 6, ICI_US_PER_MB * mb / n) for r in range(n)]
    return _Task(G, jax.ShapeDtypeStruct((3,) + G.shape[1:], G.dtype), parts)


def _row_splits(rows, cost):
    n = 8
    while n > 1 and (rows % (16 * n) or cost / n < PART_US):
        n //= 2
    return n


CARRIER_US = {
    "f_gu": 48, "f_down": 20, "f_dact": 42, "f_dwd": 20, "f_dwgu": 42, "f_dh": 46,
    "nsa_in": 25, "nsa_cmp": 86, "nsa_slc": 122, "nsa_win": 74, "nsa_dcmp": 58, "nsa_dslc": 195, "nsa_dwin": 108,
    "nsa_dh": 26, "nsa_dwin_w": 25,
    "swa_in": 16, "swa_attn": 74, "swa_dattn": 74, "swa_dh": 17, "swa_dwin_w": 18,
    "fox_in": 34, "fox_attn": 104, "fox_dattn": 161, "fox_dh": 30, "fox_dwin_w": 30,
}
ICI_US_PER_MB = 15.0
PART_US = 12.0
D2D_US_PER_MB = 2.3
LOCAL_US_PER_MB = 1.5


class _Sched:
    def __init__(self):
        self.queue, self.credit, self.n_flush = [], 0.0, 0

    def push(self, task):
        self.queue.append(task)
        return task

    def _jobs(self, fits):
        jobs = []
        while self.queue and self.queue[0].ready():
            task, n = self.queue[0], 0
            while task.done + n < len(task.parts) and fits(task.parts[task.done + n][2]):
                n += 1
            if n:
                jobs.append(task.next_job(n))
            if task.done < len(task.parts):
                break
            self.queue.pop(0)
        return jobs

    def take(self, name):
        kind = name[2:].split("_", 1)
        key = ("f" if kind[0] in ("f1", "f2") else kind[0]) + "_" + kind[1]
        self.credit = min(self.credit, 0.0) + CARRIER_US.get(key, 0.0)

        def fits(cost):
            if self.credit < 0.5 * cost:
                return False
            self.credit -= cost
            return True

        return self._jobs(fits)

    def finish(self, task=None):
        while self.queue and (task is None or not task.complete()):
            jobs = []
            while self.queue and self.queue[0].ready():
                head = self.queue.pop(0)
                jobs.append(head.next_job(len(head.parts) - head.done))
                if head is task:
                    break
            assert jobs, "the task at the head of the queue waits for one that was never queued"
            _comm_call(jobs, name=f"exchange_{self.n_flush}")
            self.n_flush += 1


def _mm(a, b, *, ta=False, tb=False, tm, tn, b_lead=(), into=None, o_lead=(), out_dtype=F32, res=None, alpha=1.0,
        ctx=None, name):
    M = a.shape[1] if ta else a.shape[0]
    K = a.shape[0] if ta else a.shape[1]
    bk, bn = (b.shape[-1], b.shape[-2]) if tb else (b.shape[-2], b.shape[-1])
    j_lead, k_lead = "j" in b_lead, "k" in b_lead
    N = bn * (b.shape[b_lead.index("j")] if j_lead else 1)
    nk = b.shape[b_lead.index("k")] if k_lead else 1
    tk = K // nk
    assert tk == bk, (name, K, nk, bk)
    tm = min(tm, M)
    tn = bn if j_lead else min(tn, N)
    assert M % tm == 0 and N % tn == 0, (name, M, N, tm, tn)
    nb, no = len(b_lead), len(o_lead)

    def pick(lead, j, k):
        return tuple(j if t == "j" else k if t == "k" else t for t in lead)

    a_spec = pl.BlockSpec((tk, tm), lambda i, j, k: (k, i)) if ta else pl.BlockSpec((tm, tk), lambda i, j, k: (i, k))
    if tb:
        b_spec = pl.BlockSpec((None,) * nb + (tn, tk),
                              lambda i, j, k: pick(b_lead, j, k) + (0 if j_lead else j, 0 if k_lead else k))
    else:
        b_spec = pl.BlockSpec((None,) * nb + (tk, tn),
                              lambda i, j, k: pick(b_lead, j, k) + (0 if k_lead else k, 0 if j_lead else j))
    r_spec = pl.BlockSpec((tm, tn), lambda i, j, k: (i, j))
    if o_lead == ("heads",):
        o_spec = pl.BlockSpec((tn // DH, tm, DH), lambda i, j, k: (j, i, 0))
    else:
        o_spec = pl.BlockSpec((None,) * no + (tm, tn),
                              lambda i, j, k: pick(o_lead, j, k) + (i, 0 if "j" in o_lead else j))
    dn = (((0 if ta else 1,), (1 if tb else 0,)), ((), ()))
    has_res, has_into = res is not None, into is not None
    if has_into:
        out_dtype = into.dtype

    def body(*refs):
        a_ref, b_ref = refs[0], refs[1]
        r_ref = refs[2] if has_res else None
        o_ref = refs[2 + has_res + has_into]
        prod = lax.dot_general(a_ref[...].astype(_CDT), b_ref[...].astype(_CDT), dn, preferred_element_type=F32)

        def finish(acc):
            if alpha != 1.0:
                acc = acc * alpha
            if has_res:
                acc = r_ref[...] + acc
            if o_lead == ("heads",):
                for hd in range(tn // DH):
                    o_ref[hd] = acc[:, hd * DH:(hd + 1) * DH].astype(out_dtype)
            else:
                o_ref[...] = acc.astype(out_dtype)

        if nk == 1:
            finish(prod)
        else:
            acc_ref = refs[-1]
            k = pl.program_id(2)

            @pl.when(k == 0)
            def _():
                acc_ref[...] = prod

            @pl.when(k != 0)
            def _():
                acc_ref[...] += prod

            @pl.when(k == nk - 1)
            def _():
                finish(acc_ref[...])

    ins, specs = [a, b], [a_spec, b_spec]
    if has_res:
        ins.append(res)
        specs.append(r_spec)
    aliases = {}
    if has_into:
        aliases = {len(ins): 0}
        ins.append(into)
        specs.append(_ANY)
        out_shape = jax.ShapeDtypeStruct(into.shape, into.dtype)
    elif o_lead == ("j",):
        out_shape = jax.ShapeDtypeStruct((N // tn, M, tn), out_dtype)
    elif o_lead == ("heads",):
        out_shape = jax.ShapeDtypeStruct((N // DH, M, DH), out_dtype)
    else:
        assert not o_lead
        out_shape = jax.ShapeDtypeStruct((M, N), out_dtype)
    scratch = [pltpu.VMEM((tm, tn), F32)] if nk > 1 else []
    return _pcall(body, out_shape=out_shape, grid=(M // tm, N // tn, nk), in_specs=specs, out_specs=o_spec,
                  scratch_shapes=scratch, aliases=aliases, name=name, jobs=ctx.take(name) if ctx else (), ins=ins)


def _rows2d(a):
    return a.reshape(-1, a.shape[-1])


def _row_tile(rows, cols, itemsize=4, budget=2 * 1024 * 1024):
    t = rows
    while t % 2 == 0 and t * cols * itemsize > budget and (t // 2) % 8 == 0:
        t //= 2
    return t


def _addn(*xs, name):
    shape = xs[0].shape
    x2 = [_rows2d(x) for x in xs]
    R, C = x2[0].shape
    tr = _row_tile(R, C)

    def body(*refs):
        acc = refs[0][...]
        for r in refs[1:-1]:
            acc = acc + r[...]
        refs[-1][...] = acc

    spec = pl.BlockSpec((tr, C), lambda i: (i, 0))
    out = pl.pallas_call(body, out_shape=jax.ShapeDtypeStruct((R, C), F32), grid=(R // tr,),
                         in_specs=[spec] * len(x2), out_specs=spec, compiler_params=_params(), name=name)(*x2)
    return out.reshape(shape)


def _rms_fwd(x, g, *, name):
    S, D = x.shape
    tr = 256

    def body(x_ref, g_ref, h_ref):
        xv = x_ref[...]
        rstd = lax.rsqrt(jnp.mean(xv * xv, axis=-1, keepdims=True) + RMS_EPS)
        h_ref[...] = (xv * rstd * g_ref[...]).astype(_CDT)

    return pl.pallas_call(body, out_shape=jax.ShapeDtypeStruct((S, D), _CDT), grid=(S // tr,),
                          in_specs=[pl.BlockSpec((tr, D), lambda i: (i, 0)), pl.BlockSpec((1, D), lambda i: (0, 0))],
                          out_specs=pl.BlockSpec((tr, D), lambda i: (i, 0)), compiler_params=_params(),
                          name=name)(x, g.reshape(1, D))


def _rms_bwd(dh, x, g, dres, *, name):
    S, D = x.shape
    tr = 256

    def body(dh_ref, x_ref, g_ref, dres_ref, dx_ref, dg_ref):
        xv = x_ref[...]
        rstd = lax.rsqrt(jnp.mean(xv * xv, axis=-1, keepdims=True) + RMS_EPS)
        xhat = xv * rstd
        dhv = dh_ref[...]
        dxhat = dhv * g_ref[...]
        dx_ref[...] = dres_ref[...] + rstd * (dxhat - xhat * jnp.mean(dxhat * xhat, axis=-1, keepdims=True))

        @pl.when(pl.program_id(0) == 0)
        def _():
            dg_ref[...] = jnp.zeros_like(dg_ref)

        dg_ref[...] += jnp.sum(dhv * xhat, axis=0, keepdims=True)

    row = pl.BlockSpec((tr, D), lambda i: (i, 0))
    vec = pl.BlockSpec((1, D), lambda i: (0, 0))
    dx, dg = pl.pallas_call(body, out_shape=(jax.ShapeDtypeStruct((S, D), F32), jax.ShapeDtypeStruct((1, D), F32)),
                            grid=(S // tr,), in_specs=[row, row, vec, row], out_specs=(row, vec),
                            compiler_params=_params(), name=name)(dh, x, g.reshape(1, D), dres)
    return dx, dg.reshape(D)


def _loss_head(x, g, tgt, *, name):
    S, D = x.shape
    tr = 256

    def body(x_ref, g_ref, t_ref, loss_ref, dx_ref, dg_ref):
        xv = x_ref[...]
        rstd = lax.rsqrt(jnp.mean(xv * xv, axis=-1, keepdims=True) + RMS_EPS)
        xhat = xv * rstd
        err = xhat * g_ref[...] - t_ref[...]
        part = 0.5 * jnp.sum(jnp.mean(err * err, axis=-1, keepdims=True), axis=0, keepdims=True)
        dy = err * (1.0 / D)
        dxhat = dy * g_ref[...]
        dx_ref[...] = rstd * (dxhat - xhat * jnp.mean(dxhat * xhat, axis=-1, keepdims=True))

        @pl.when(pl.program_id(0) == 0)
        def _():
            dg_ref[...] = jnp.zeros_like(dg_ref)
            loss_ref[...] = jnp.zeros_like(loss_ref)

        dg_ref[...] += jnp.sum(dy * xhat, axis=0, keepdims=True)
        loss_ref[...] += jnp.broadcast_to(part, loss_ref.shape)

    row = pl.BlockSpec((tr, D), lambda i: (i, 0))
    vec = pl.BlockSpec((1, D), lambda i: (0, 0))
    loss, dx, dg = pl.pallas_call(
        body, out_shape=(jax.ShapeDtypeStruct((8, LANES), F32), jax.ShapeDtypeStruct((S, D), F32),
                         jax.ShapeDtypeStruct((1, D), F32)),
        grid=(S // tr,), in_specs=[row, vec, row], out_specs=(pl.BlockSpec((8, LANES), lambda i: (0, 0)), row, vec),
        compiler_params=_params(), name=name)(x, g.reshape(1, D), tgt)
    return loss[0, 0], dx, dg.reshape(D)


def _rope_tables(S):
    inv = 10000.0 ** (-jnp.arange(0, DH, 2, dtype=F32) / DH)
    ang = jnp.arange(S, dtype=F32)[:, None] * inv[None, :]
    cos, sin = jnp.cos(ang), jnp.sin(ang)
    return jnp.concatenate([cos, cos], -1), jnp.concatenate([-sin, sin], -1)


def _swap_matrix():
    p = np.zeros((DH, DH), np.float32)
    for j in range(DH // 2):
        p[j + DH // 2, j] = 1.0
        p[j, j + DH // 2] = 1.0
    return jnp.asarray(p, jnp.bfloat16)


def _rope(x, cc, ss, rotated, *, name):
    n, S, _ = x.shape
    assert len(rotated) == n
    starts = [i for i in range(n) if rotated[i] and (i == 0 or not rotated[i - 1])]
    tab = pl.BlockSpec((S, DH), lambda i: (0, 0))
    for k, first in enumerate(starts):
        count = next((i for i in range(first, n) if not rotated[i]), n) - first

        def body(x_ref, c_ref, s_ref, p_ref, o_ref):
            xv = x_ref[0]
            o_ref[0] = xv * c_ref[...] + _dot_exact(xv, p_ref[...]) * s_ref[...]

        blk = pl.BlockSpec((1, S, DH), lambda i, first=first: (first + i, 0, 0))
        x = pl.pallas_call(body, out_shape=jax.ShapeDtypeStruct(x.shape, F32), grid=(count,),
                           in_specs=[blk, tab, tab, pl.BlockSpec((DH, DH), lambda i: (0, 0))], out_specs=blk,
                           input_output_aliases={0: 0}, compiler_params=_params(),
                           name=f"{name}{k}")(x, cc, ss, _swap_matrix())
    return x


def _key_range(kind, i, window, Sk):
    if kind == "cmp":
        return 0, Sk
    hi = (i + 1) * BQ
    if kind == "band":
        return max(0, i * BQ - window), hi
    return 0, hi


def _attn_mask(kind, i, lo, hi, window):
    shape = (BQ, hi - lo)
    qpos = i * BQ + lax.broadcasted_iota(jnp.int32, shape, 0)
    kpos = lo + lax.broadcasted_iota(jnp.int32, shape, 1)
    if kind == "cmp":
        return kpos * 16 + 31 <= qpos
    mask = kpos <= qpos
    if kind == "band":
        mask = mask & (qpos - kpos < window)
    return mask


def _sel_expand(n_slc, n_keys):
    shape = (n_slc, n_keys)
    j = lax.broadcasted_iota(jnp.int32, shape, 0)
    key = lax.broadcasted_iota(jnp.int32, shape, 1)
    return (jnp.right_shift(key, 6) == j).astype(_CDT)


def _eye():
    return lax.broadcasted_iota(jnp.int32, (BQ, BQ), 0) == lax.broadcasted_iota(jnp.int32, (BQ, BQ), 1)


def _to_col(row):
    return jnp.sum(jnp.where(_eye(), row, 0.0), axis=1, keepdims=True)


def _to_row(col):
    return jnp.sum(jnp.where(_eye(), col, 0.0), axis=0, keepdims=True)


def _scores(kind, i, lo, hi, window, qb, kb, crow_ref, sel_ref):
    s = _dot_nt(qb, kb)
    if crow_ref is not None:
        s = s + _to_col(crow_ref[0, :, i * BQ:(i + 1) * BQ]) - crow_ref[0, :, lo:hi]
    mask = _attn_mask(kind, i, lo, hi, window)
    if sel_ref is not None:
        mask = mask & (sel_ref[0, i * BQ:(i + 1) * BQ, lo:hi].astype(F32) > 0.5)
    return jnp.where(mask, s, NEG), mask


def _attn_fwd(q, k, v, *, G, R, heads=(0, 0, 0), kind, window=0, bias=None, sinks=None, selT=None, ovT=None, ctx=None,
              name):
    S = q.shape[1]
    Sk = k.shape[1]
    q0, k0, v0 = heads
    nq = S // BQ
    n_slc = S // 64
    has_bias, has_sink, has_sel, is_cmp = bias is not None, sinks is not None, selT is not None, kind == "cmp"

    def body(*refs):
        it = iter(refs)
        q_ref, k_ref, v_ref = next(it), next(it), next(it)
        crow_ref = next(it) if has_bias else None
        sink_ref = next(it) if has_sink else None
        sel_ref = next(it) if has_sel else None
        ov_ref = next(it) if is_cmp else None
        o_ref, lse_ref = next(it), next(it)
        selo_ref, imp_ref = (next(it), next(it)) if is_cmp else (None, None)
        r = pl.program_id(1)
        for i in range(nq):
            lo, hi = _key_range(kind, i, window, Sk)
            rows = slice(i * BQ, (i + 1) * BQ)
            qb = (q_ref[rows, :] * SCALE).astype(_CDT)
            kb = k_ref[0, lo:hi, :].astype(_CDT)
            vb = v_ref[0, lo:hi, :].astype(_CDT)
            s, mask = _scores(kind, i, lo, hi, window, qb, kb, crow_ref, sel_ref)
            m = jnp.max(s, axis=-1, keepdims=True)
            if has_sink:
                sk = sink_ref[0, 0, 0:1, 0:1]
                m = jnp.maximum(m, sk)
            e = jnp.exp(s - m)
            if is_cmp:
                e = jnp.where(mask, e, 0.0)
            l = jnp.sum(e, axis=-1, keepdims=True)
            if has_sink:
                l = l + jnp.exp(sk - m)
            if is_cmp:
                l = jnp.where(l > 0.0, l, 1.0)
            p = e * (1.0 / l)
            o_ref[0, 0, rows, :] = _dot(p.astype(_CDT), vb)
            lse_ref[0, 0, :, rows] = _to_row(m + jnp.log(l))
            if is_cmp:
                part = _dot_nt(ov_ref[...].astype(_CDT), p.astype(_CDT))

                @pl.when(r == 0)
                def _():
                    imp_ref[:, rows] = part

                @pl.when(r != 0)
                def _():
                    imp_ref[:, rows] += part

        if is_cmp:
            @pl.when(r == R - 1)
            def _():
                shape = (n_slc, S)
                j = lax.broadcasted_iota(jnp.int32, shape, 0)
                tb = jnp.right_shift(lax.broadcasted_iota(jnp.int32, shape, 1), 6)
                forced = (j == 0) | (j == tb) | (j == tb - 1)
                imp = jnp.where(j > tb, NEG, jnp.where(forced, NSA_BONUS, imp_ref[0:n_slc, :]))
                imp_ref[0:n_slc, :] = imp
                cnt = jnp.zeros(shape, F32)
                for jp in range(n_slc):
                    row = imp_ref[jp:jp + 1, :]
                    ahead = (row > imp) | ((row == imp) & (jp < j))
                    cnt = cnt + ahead.astype(F32)
                imp_ref[0:n_slc, :] = (cnt < float(min(NSA_TOPK, n_slc))).astype(F32)
                expand = _sel_expand(n_slc, S)
                for i in range(nq):
                    rows = slice(i * BQ, (i + 1) * BQ)
                    chosen = _dot_tn(imp_ref[0:n_slc, rows].astype(_CDT), expand)
                    selo_ref[0, rows, :] = chosen.astype(jnp.bfloat16)

    qspec = pl.BlockSpec((1, 1, S, DH), lambda g, r: (g, r, 0, 0))
    ins = [q, k, v]
    specs = [pl.BlockSpec((None, S, DH), lambda g, r: (q0 + g * R + r, 0, 0)),
             pl.BlockSpec((1, Sk, DH), lambda g, r: (k0 + g, 0, 0)), pl.BlockSpec((1, Sk, DH), lambda g, r: (v0 + g, 0, 0))]
    if has_bias:
        ins.append(bias)
        specs.append(pl.BlockSpec((1, 1, S), lambda g, r: (g, 0, 0)))
    if has_sink:
        ins.append(sinks)
        specs.append(pl.BlockSpec((1, 1, 8, LANES), lambda g, r: (g, r, 0, 0)))
    if has_sel:
        ins.append(selT)
        specs.append(pl.BlockSpec((1, S, S), lambda g, r: (g, 0, 0)))
    if is_cmp:
        ins.append(ovT)
        specs.append(pl.BlockSpec((LANES, Sk), lambda g, r: (0, 0)))
    outs = [jax.ShapeDtypeStruct((G, R, S, DH), F32), jax.ShapeDtypeStruct((G, R, 1, S), F32)]
    ospecs = [qspec, pl.BlockSpec((1, 1, 1, S), lambda g, r: (g, r, 0, 0))]
    scratch = []
    if is_cmp:
        outs.append(jax.ShapeDtypeStruct((G, S, S), jnp.bfloat16))
        ospecs.append(pl.BlockSpec((1, S, S), lambda g, r: (g, 0, 0)))
        scratch.append(pltpu.VMEM((LANES, S), F32))
    return _pcall(body, out_shape=tuple(outs), grid=(G, R), in_specs=specs, out_specs=tuple(ospecs),
                  scratch_shapes=scratch, name=name, jobs=ctx.take(name) if ctx else (), ins=ins)


def _attn_bwd(q, k, v, o, do, lse, *, heads=(0, 0, 0), kind, window=0, bias=None, sinks=None, selT=None, gate=None,
              ctx=None, name):
    G, R, S, _ = o.shape
    Sk = k.shape[1]
    q0, k0, v0 = heads
    nq = S // BQ
    has_bias, has_sink, has_sel, has_gate = bias is not None, sinks is not None, selT is not None, gate is not None

    def body(*refs):
        it = iter(refs)
        q_ref, k_ref, v_ref, o_ref, do_ref, lse_ref = (next(it) for _ in range(6))
        crow_ref = next(it) if has_bias else None
        sink_ref = next(it) if has_sink else None
        sel_ref = next(it) if has_sel else None
        z_ref = next(it) if has_gate else None
        dq_ref, dk_ref, dv_ref = next(it), next(it), next(it)
        dc_ref = next(it) if has_bias else None
        dsink_ref = next(it) if has_sink else None
        dz_ref = next(it) if has_gate else None
        r = pl.program_id(1)

        @pl.when(r == 0)
        def _():
            dk_ref[...] = jnp.zeros_like(dk_ref)
            dv_ref[...] = jnp.zeros_like(dv_ref)
            if has_bias:
                dc_ref[...] = jnp.zeros_like(dc_ref)

        dsink = jnp.zeros((1, 1), F32)
        for i in range(nq):
            lo, hi = _key_range(kind, i, window, Sk)
            rows = slice(i * BQ, (i + 1) * BQ)
            qb = (q_ref[rows, :] * SCALE).astype(_CDT)
            kb = k_ref[0, lo:hi, :].astype(_CDT)
            vb = v_ref[0, lo:hi, :].astype(_CDT)
            s, mask = _scores(kind, i, lo, hi, window, qb, kb, crow_ref, sel_ref)
            lse_i = _to_col(lse_ref[0, 0, :, rows])
            p = jnp.where(mask, jnp.exp(s - lse_i), 0.0)
            dob = do_ref[0, 0, rows, :]
            if has_gate:
                od = jnp.sum(o_ref[0, 0, rows, :] * dob, axis=-1, keepdims=True)
                sg = jax.nn.sigmoid(_to_col(z_ref[0, 0, :, rows]))
                dob = dob * sg
                dz_ref[0, 0, :, rows] = _to_row(od * sg * (1.0 - sg))
            dob = dob.astype(_CDT)
            dp = _dot_nt(dob, vb)
            delta = jnp.sum(p * dp, axis=-1, keepdims=True)
            ds = p * (dp - delta)
            dsb = ds.astype(_CDT)
            dq_ref[0, 0, rows, :] = _dot(dsb, kb) * SCALE
            dk_ref[0, lo:hi, :] += _dot_tn(dsb, qb)
            dv_ref[0, lo:hi, :] += _dot_tn(p.astype(_CDT), dob)
            if has_bias:
                dc_ref[0, :, rows] += _to_row(jnp.sum(ds, axis=-1, keepdims=True))
                dc_ref[0, :, lo:hi] -= jnp.sum(ds, axis=0, keepdims=True)
            if has_sink:
                sk = sink_ref[0, 0, 0:1, 0:1]
                dsink = dsink - jnp.sum(jnp.exp(sk - lse_i) * delta, axis=0, keepdims=True)
        if has_sink:
            dsink_ref[0, 0] = jnp.broadcast_to(dsink, (8, LANES))

    qspec = pl.BlockSpec((1, 1, S, DH), lambda g, r: (g, r, 0, 0))
    cspec = pl.BlockSpec((1, 1, 1, S), lambda g, r: (g, r, 0, 0))
    kspec = pl.BlockSpec((1, Sk, DH), lambda g, r: (g, 0, 0))
    ins = [q, k, v, o, do, lse]
    specs = [pl.BlockSpec((None, S, DH), lambda g, r: (q0 + g * R + r, 0, 0)),
             pl.BlockSpec((1, Sk, DH), lambda g, r: (k0 + g, 0, 0)), pl.BlockSpec((1, Sk, DH), lambda g, r: (v0 + g, 0, 0)),
             qspec, qspec, cspec]
    if has_bias:
        ins.append(bias)
        specs.append(pl.BlockSpec((1, 1, S), lambda g, r: (g, 0, 0)))
    if has_sink:
        ins.append(sinks)
        specs.append(pl.BlockSpec((1, 1, 8, LANES), lambda g, r: (g, r, 0, 0)))
    if has_sel:
        ins.append(selT)
        specs.append(pl.BlockSpec((1, S, S), lambda g, r: (g, 0, 0)))
    if has_gate:
        ins.append(gate)
        specs.append(cspec)
    names = ["dq", "dk", "dv"]
    outs = [jax.ShapeDtypeStruct((G, R, S, DH), F32), jax.ShapeDtypeStruct((G, Sk, DH), F32),
            jax.ShapeDtypeStruct((G, Sk, DH), F32)]
    ospecs = [qspec, kspec, kspec]
    if has_bias:
        assert R == 1
        names.append("dc")
        outs.append(jax.ShapeDtypeStruct((G, 1, S), F32))
        ospecs.append(pl.BlockSpec((1, 1, S), lambda g, r: (g, 0, 0)))
    if has_sink:
        names.append("dsink")
        outs.append(jax.ShapeDtypeStruct((G, R, 8, LANES), F32))
        ospecs.append(pl.BlockSpec((1, 1, 8, LANES), lambda g, r: (g, r, 0, 0)))
    if has_gate:
        names.append("dz")
        outs.append(jax.ShapeDtypeStruct((G, R, 1, S), F32))
        ospecs.append(cspec)
    res = _pcall(body, out_shape=tuple(outs), grid=(G, R), in_specs=specs, out_specs=tuple(ospecs), name=name,
                 jobs=ctx.take(name) if ctx else (), ins=ins)
    return dict(zip(names, res))


def _combine(o0, o1, o2, z, *, name):
    H, S, _ = o0.shape

    def body(o0_ref, o1_ref, o2_ref, z_ref, o_ref):
        for i in range(S // BQ):
            rows = slice(i * BQ, (i + 1) * BQ)
            acc = jax.nn.sigmoid(_to_col(z_ref[0, 0, :, rows])) * o0_ref[0, rows, :]
            acc = acc + jax.nn.sigmoid(_to_col(z_ref[1, 0, :, rows])) * o1_ref[0, rows, :]
            acc = acc + jax.nn.sigmoid(_to_col(z_ref[2, 0, :, rows])) * o2_ref[0, rows, :]
            o_ref[0, rows, :] = acc

    blk = pl.BlockSpec((1, S, DH), lambda h: (h, 0, 0))
    return pl.pallas_call(body, out_shape=jax.ShapeDtypeStruct((H, S, DH), F32), grid=(H,),
                          in_specs=[blk, blk, blk, pl.BlockSpec((3, 1, 1, S), lambda h: (0, h, 0, 0))], out_specs=blk,
                          compiler_params=_params(), name=name)(o0, o1, o2, z)


_GC = math.sqrt(2.0 / math.pi)


def _gelu(x):
    return 0.5 * x * (1.0 + jnp.tanh(_GC * (x + 0.044715 * x * x * x)))


def _gelu_grad(x):
    t = jnp.tanh(_GC * (x + 0.044715 * x * x * x))
    return 0.5 * (1.0 + t) + 0.5 * x * (1.0 - t * t) * _GC * (1.0 + 3.0 * 0.044715 * x * x)


def _make_xb(k):
    G, S, _ = k.shape
    chunks = k.reshape(G, S // 16, 16 * DH)
    shift = jnp.concatenate([chunks[:, 1:], jnp.zeros((G, 1, 16 * DH), k.dtype)], axis=1)
    return jnp.concatenate([chunks, shift], axis=-1)


def _unmake_xb(dxb, *, name):
    G, n, _ = dxb.shape
    a = dxb[..., :16 * DH]
    b = jnp.concatenate([jnp.zeros((G, 1, 16 * DH), F32), dxb[:, :-1, 16 * DH:]], axis=1)
    return _addn(a, b, name=name).reshape(G, n * 16, DH)


def _compress_fwd(xb, pe, w1, w2, *, name):
    G, n, W = xb.shape
    Hc = w1.shape[1]

    def body(xb_ref, pe_ref, w1_ref, w2_ref, kc_ref, hid_ref):
        xv = (xb_ref[0] + pe_ref[...]).astype(_CDT)
        hid = _dot(xv, w1_ref[...].astype(_CDT))
        hid_ref[0] = hid
        kc_ref[0] = _dot(_gelu(hid).astype(_CDT), w2_ref[...].astype(_CDT))

    return pl.pallas_call(
        body, out_shape=(jax.ShapeDtypeStruct((G, n, DH), F32), jax.ShapeDtypeStruct((G, n, Hc), F32)), grid=(G,),
        in_specs=[pl.BlockSpec((1, n, W), lambda g: (g, 0, 0)), pl.BlockSpec((1, W), lambda g: (0, 0)),
                  pl.BlockSpec((W, Hc), lambda g: (0, 0)), pl.BlockSpec((Hc, DH), lambda g: (0, 0))],
        out_specs=(pl.BlockSpec((1, n, DH), lambda g: (g, 0, 0)), pl.BlockSpec((1, n, Hc), lambda g: (g, 0, 0))),
        compiler_params=_params(), name=name)(xb, pe.reshape(1, W), w1, w2)


def _compress_bwd(xb, pe, w1, w2, hid, dkc, *, name):
    G, n, W = xb.shape
    Hc = w1.shape[1]

    def body(xb_ref, pe_ref, w1_ref, w2_ref, hid_ref, dkc_ref, dxb_ref, dw1_ref, dw2_ref, dpe_ref):
        @pl.when(pl.program_id(0) == 0)
        def _():
            dw1_ref[...] = jnp.zeros_like(dw1_ref)
            dw2_ref[...] = jnp.zeros_like(dw2_ref)
            dpe_ref[...] = jnp.zeros_like(dpe_ref)

        xv = (xb_ref[0] + pe_ref[...]).astype(_CDT)
        hid = hid_ref[0]
        dk = dkc_ref[0].astype(_CDT)
        dact = _dot_nt(dk, w2_ref[...].astype(_CDT))
        dhid = (dact * _gelu_grad(hid)).astype(_CDT)
        dw2_ref[...] += _dot_tn(_gelu(hid).astype(_CDT), dk)
        dxb = _dot_nt(dhid, w1_ref[...].astype(_CDT))
        dxb_ref[0] = dxb
        dw1_ref[...] += _dot_tn(xv, dhid)
        dpe_ref[...] += jnp.sum(dxb, axis=0, keepdims=True)

    return pl.pallas_call(
        body, out_shape=(jax.ShapeDtypeStruct((G, n, W), F32), jax.ShapeDtypeStruct((W, Hc), F32),
                         jax.ShapeDtypeStruct((Hc, DH), F32), jax.ShapeDtypeStruct((1, W), F32)), grid=(G,),
        in_specs=[pl.BlockSpec((1, n, W), lambda g: (g, 0, 0)), pl.BlockSpec((1, W), lambda g: (0, 0)),
                  pl.BlockSpec((W, Hc), lambda g: (0, 0)), pl.BlockSpec((Hc, DH), lambda g: (0, 0)),
                  pl.BlockSpec((1, n, Hc), lambda g: (g, 0, 0)), pl.BlockSpec((1, n, DH), lambda g: (g, 0, 0))],
        out_specs=(pl.BlockSpec((1, n, W), lambda g: (g, 0, 0)), pl.BlockSpec((W, Hc), lambda g: (0, 0)),
                   pl.BlockSpec((Hc, DH), lambda g: (0, 0)), pl.BlockSpec((1, W), lambda g: (0, 0))),
        compiler_params=_params(), name=name)(xb, pe.reshape(1, W), w1, w2, hid, dkc)


def _overlap_T(S):
    n_cmp, n_slc = S // 16 - 1, S // 64
    cs = np.arange(n_cmp) * 16
    ce = cs + 32
    ss = np.arange(n_slc) * 64
    se = ss + 64
    ov = np.clip(np.minimum(ce[:, None], se[None, :]) - np.maximum(cs[:, None], ss[None, :]), 0, None) / 32.0
    out = np.zeros((LANES, S // 16), np.float32)
    out[:n_slc, :n_cmp] = ov.T
    return jnp.asarray(out)


def _tri(n, upper):
    r = lax.broadcasted_iota(jnp.int32, (n, n), 0)
    c = lax.broadcasted_iota(jnp.int32, (n, n), 1)
    return ((c >= r) if upper else (c <= r)).astype(jnp.bfloat16)


def _fox_gate_fwd(zf, b, *, name):
    S, H = zf.shape
    nb = S // BQ

    def body(z_ref, b_ref, c_ref):
        tri = _tri(BQ, False)
        carry = jnp.zeros((1, H), F32)
        for i in range(nb):
            z = z_ref[i * BQ:(i + 1) * BQ, :] + b_ref[...]
            lf = jnp.minimum(z, 0.0) - jnp.log(1.0 + jnp.exp(-jnp.abs(z)))
            c_ref[i * BQ:(i + 1) * BQ, :] = _dot_exact_left(tri, lf) + carry
            carry = carry + jnp.sum(lf, axis=0, keepdims=True)

    return pl.pallas_call(body, out_shape=jax.ShapeDtypeStruct((S, H), F32), compiler_params=_params(),
                          name=name)(zf, b)


def _fox_gate_bwd(zf, b, dc, *, name):
    S, H = zf.shape
    nb = S // BQ

    def body(z_ref, b_ref, dc_ref, dz_ref, db_ref):
        tri = _tri(BQ, True)
        carry = jnp.zeros((1, H), F32)
        db = jnp.zeros((1, H), F32)
        for i in reversed(range(nb)):
            rows = slice(i * BQ, (i + 1) * BQ)
            dcb = dc_ref[rows, :]
            dlf = _dot_exact_left(tri, dcb) + carry
            carry = carry + jnp.sum(dcb, axis=0, keepdims=True)
            z = z_ref[rows, :] + b_ref[...]
            dz = dlf * jax.nn.sigmoid(-z)
            dz_ref[rows, :] = dz
            db = db + jnp.sum(dz, axis=0, keepdims=True)
        db_ref[...] = db

    return pl.pallas_call(body, out_shape=(jax.ShapeDtypeStruct((S, H), F32), jax.ShapeDtypeStruct((1, H), F32)),
                          compiler_params=_params(), name=name)(zf, b, dc)


def _from_heads(a):
    return a.transpose(1, 0, 2).reshape(a.shape[1], -1).astype(_CDT)


def _pad_lanes(a):
    return jnp.pad(a, ((0, 0), (0, LANES - a.shape[1])))


FFN_TM = 512


def _ffn_up(h, w4, *, ctx, name):
    S, D = h.shape
    C = w4.shape[2]

    def body(h_ref, wg_ref, wu_ref, gu_ref, a_ref):
        hv = h_ref[...].astype(_CDT)
        g = _dot(hv, wg_ref[...].astype(_CDT))
        u = _dot(hv, wu_ref[...].astype(_CDT))
        gu_ref[0] = g
        gu_ref[1] = u
        a_ref[...] = (g * jax.nn.sigmoid(g) * u).astype(_CDT)

    return _pcall(
        body, out_shape=(jax.ShapeDtypeStruct((2, S, 2 * C), F32), jax.ShapeDtypeStruct((S, 2 * C), _CDT)),
        grid=(2, S // FFN_TM),
        in_specs=[pl.BlockSpec((FFN_TM, D), lambda j, i: (i, 0)), pl.BlockSpec((None, D, C), lambda j, i: (j, 0, 0)),
                  pl.BlockSpec((None, D, C), lambda j, i: (j + 2, 0, 0))],
        out_specs=(pl.BlockSpec((2, FFN_TM, C), lambda j, i: (0, i, j)), pl.BlockSpec((FFN_TM, C), lambda j, i: (i, j))),
        name=name, jobs=ctx.take(name) if ctx else (), ins=[h, w4, w4])


def _ffn_dact(dxo, wd, gu, *, ctx, name):
    S, D = dxo.shape
    C = gu.shape[2] // 2

    def body(dx_ref, wd_ref, gu_ref, d_ref):
        da = _dot_nt(dx_ref[...].astype(_CDT), wd_ref[...].astype(_CDT)) * 0.5
        g, u = gu_ref[0], gu_ref[1]
        sg = jax.nn.sigmoid(g)
        silu = g * sg
        d_ref[0] = (da * u * (sg + silu * (1.0 - sg))).astype(_CDT)
        d_ref[1] = (da * silu).astype(_CDT)

    blk = pl.BlockSpec((2, FFN_TM, C), lambda i, j: (0, i, j))
    return _pcall(body, out_shape=jax.ShapeDtypeStruct(gu.shape, _CDT), grid=(S // FFN_TM, 2),
                  in_specs=[pl.BlockSpec((FFN_TM, D), lambda i, j: (i, 0)), pl.BlockSpec((C, D), lambda i, j: (j, 0)), blk],
                  out_specs=blk, name=name, jobs=ctx.take(name) if ctx else (), ins=[dxo, wd, gu])


def _ffn_dwgu(h, dgu, *, ctx, name):
    S, D = h.shape
    C = dgu.shape[2] // 2
    tm = 512

    def body(h_ref, d_ref, o_ref):
        o_ref[...] = _dot_tn(h_ref[...].astype(_CDT), d_ref[...].astype(_CDT)).astype(_WIRE)

    return _pcall(body, out_shape=jax.ShapeDtypeStruct((4, D, C), _WIRE), grid=(4, D // tm),
                  in_specs=[pl.BlockSpec((S, tm), lambda j, i: (0, i)),
                            pl.BlockSpec((None, S, C), lambda j, i: (j // 2, 0, j % 2))],
                  out_specs=pl.BlockSpec((None, tm, C), lambda j, i: (j, i, 0)), name=name,
                  jobs=ctx.take(name) if ctx else (), ins=[h, dgu])


def _ffn_dh(dgu, w4, *, ctx, name):
    _, S, F2 = dgu.shape
    C, D = F2 // 2, w4.shape[1]

    def body(d_ref, w_ref, o_ref, acc_ref):
        k = pl.program_id(1)
        prod = _dot_nt(d_ref[...].astype(_CDT), w_ref[...].astype(_CDT))

        @pl.when(k == 0)
        def _():
            acc_ref[...] = prod

        @pl.when(k != 0)
        def _():
            acc_ref[...] += prod

        @pl.when(k == 3)
        def _():
            o_ref[...] = acc_ref[...]

    return _pcall(body, out_shape=jax.ShapeDtypeStruct((S, D), F32), grid=(S // FFN_TM, 4),
                  in_specs=[pl.BlockSpec((None, FFN_TM, C), lambda i, k: (k // 2, i, k % 2)),
                            pl.BlockSpec((None, D, C), lambda i, k: (k, 0, 0))],
                  out_specs=pl.BlockSpec((FFN_TM, D), lambda i, k: (i, 0)), scratch_shapes=[pltpu.VMEM((FFN_TM, D), F32)],
                  name=name, jobs=ctx.take(name) if ctx else (), ins=[dgu, w4])


def _ffn_fwd(x, g, P, l, tag):
    h = _rms_fwd(x, g, name=tag + "_rms")
    gu, a = _ffn_up(h, P.weight("gu", l), ctx=P.ctx, name=tag + "_gu")
    xo = _mm(a, P.weight("down", l), tm=512, tn=1024, res=x, alpha=0.5, ctx=P.ctx, name=tag + "_down")
    return xo, (x, h, gu, a)


def _ffn_bwd(dxo, saved, g, P, l, tag):
    x, h, gu, a = saved
    dgu = _ffn_dact(dxo, P.weight("down", l), gu, ctx=P.ctx, name=tag + "_dact")
    P.grad("gu", l, _ffn_dwgu(h, dgu, ctx=P.ctx, name=tag + "_dwgu"))
    dwd = _mm(a, dxo, ta=True, tm=1408, tn=1024, alpha=0.5, out_dtype=_WIRE, ctx=P.ctx, name=tag + "_dwd")
    P.grad("down", l, dwd.reshape(4, -1, D_MODEL))
    dh = _ffn_dh(dgu, P.weight("gu", l), ctx=P.ctx, name=tag + "_dh")
    dx, dg = _rms_bwd(dh, x, g, dxo, name=tag + "_drms")
    return dx, dg


def _nsa_fwd(x, g, w, cc, ss, tag):
    S = x.shape[0]
    h = _rms_fwd(x, g, name=tag + "_rms")
    hm = _mm(h, w["w_in"], tm=512, tn=896, o_lead=("heads",), ctx=w["ctx"], name=tag + "_in")
    roped = _rope(hm, cc, ss, NSA_ROTATED, name=tag + "_rope")
    z = roped[40, :, :3 * N_HEADS].reshape(S, 3, N_HEADS).transpose(1, 2, 0)[:, :, None, :]
    xbk, xbv = _make_xb(roped[16:20]), _make_xb(roped[20:24])
    kc, hidk = _compress_fwd(xbk, w["ck_pe"], w["ck_w1"], w["ck_w2"], name=tag + "_ck")
    vc, hidv = _compress_fwd(xbv, w["cv_pe"], w["cv_w1"], w["cv_w2"], name=tag + "_cv")
    ctx, GR = w["ctx"], dict(G=NSA_G, R=NSA_R)
    o0, lse0, selT = _attn_fwd(roped, kc, vc, kind="cmp", ovT=_overlap_T(S), ctx=ctx, name=tag + "_cmp", **GR)
    o1, lse1 = _attn_fwd(roped, roped, roped, heads=(0, 24, 28), kind="sel", selT=selT, ctx=ctx, name=tag + "_slc", **GR)
    o2, lse2 = _attn_fwd(roped, roped, roped, heads=(0, 32, 36), kind="band", window=NSA_WINDOW, ctx=ctx,
                         name=tag + "_win", **GR)
    o = _combine(o0.reshape(N_HEADS, S, DH), o1.reshape(N_HEADS, S, DH), o2.reshape(N_HEADS, S, DH), z,
                 name=tag + "_mix")
    of = _from_heads(o)
    xo = _mm(of, w["out"], tm=512, tn=1024, res=x, ctx=w["ctx"], name=tag + "_out")
    saved = (x, h, roped, z, xbk, xbv, hidk, hidv, kc, vc, (o0, o1, o2), (lse0, lse1, lse2), selT, of)
    return xo, saved


def _out_bwd(dxo, of, w, tag):
    dof = _mm(dxo, w["out"], tb=True, tm=512, tn=1024, o_lead=("heads",), ctx=w["ctx"], name=tag + "_dof")
    dw = _mm(of, dxo, ta=True, tm=512, tn=1024, out_dtype=_WIRE, ctx=w["ctx"], name=tag + "_dwout")
    w["P"].grad("out", w["out_l"], dw.reshape(4, -1, D_MODEL))
    return dof


def _split_cols(dw, n_in):
    cs = n_in // 4
    return dw[:, :n_in].reshape(D_MODEL, 4, cs).transpose(1, 0, 2).astype(_WIRE)


def _nsa_bwd(dxo, saved, g, w, cc, ss, tag):
    x, h, roped, z, xbk, xbv, hidk, hidv, kc, vc, os_, lses, selT, of = saved
    S = x.shape[0]
    do = _out_bwd(dxo, of, w, tag).reshape(NSA_G, NSA_R, S, DH)
    zg = z.reshape(3, NSA_G, NSA_R, 1, S)
    ctx = w["ctx"]
    b0 = _attn_bwd(roped, kc, vc, os_[0], do, lses[0], kind="cmp", gate=zg[0], ctx=ctx, name=tag + "_dcmp")
    b1 = _attn_bwd(roped, roped, roped, os_[1], do, lses[1], heads=(0, 24, 28), kind="sel", selT=selT, gate=zg[1],
                   ctx=ctx, name=tag + "_dslc")
    b2 = _attn_bwd(roped, roped, roped, os_[2], do, lses[2], heads=(0, 32, 36), kind="band", window=NSA_WINDOW,
                   gate=zg[2], ctx=ctx, name=tag + "_dwin")
    dxbk, dck_w1, dck_w2, dck_pe = _compress_bwd(xbk, w["ck_pe"], w["ck_w1"], w["ck_w2"], hidk, b0["dk"],
                                                 name=tag + "_dck")
    dxbv, dcv_w1, dcv_w2, dcv_pe = _compress_bwd(xbv, w["cv_pe"], w["cv_w1"], w["cv_w2"], hidv, b0["dv"],
                                                 name=tag + "_dcv")
    dk0 = _unmake_xb(dxbk, name=tag + "_dk0")
    dv0 = _unmake_xb(dxbv, name=tag + "_dv0")
    dq = _addn(b0["dq"], b1["dq"], b2["dq"], name=tag + "_dqsum").reshape(N_HEADS, S, DH)
    dz = jnp.stack([b0["dz"], b1["dz"], b2["dz"]], axis=0).reshape(3 * N_HEADS, S).T
    dgates = jnp.pad(dz, ((0, 0), (0, 2 * DH - 3 * N_HEADS))).reshape(S, 2, DH).transpose(1, 0, 2)
    dhm = jnp.concatenate([dq, dk0, dv0, b1["dk"], b1["dv"], b2["dk"], b2["dv"], dgates], axis=0)
    dproj = _from_heads(_rope(dhm, cc, -ss, NSA_ROTATED, name=tag + "_drope"))
    dh = _mm(dproj, w["w_in"], tb=True, tm=512, tn=512, ctx=w["ctx"], name=tag + "_dh")
    dw_in = _mm(h, dproj, ta=True, tm=512, tn=896, ctx=w["ctx"], name=tag + "_dwin_w")
    dx, dg = _rms_bwd(dh, x, g, dxo, name=tag + "_drms")
    P, j = w["P"], w["j"]
    P.grad("nsa_in", j, _split_cols(dw_in, NSA_IN))
    P.grad("cw1", j, dck_w1.astype(_WIRE).reshape(4, -1, dck_w1.shape[1]))
    P.grad("cw1", 2 + j, dcv_w1.astype(_WIRE).reshape(4, -1, dcv_w1.shape[1]))
    grads = dict(ck_pe=dck_pe.reshape(32, DH), ck_w2=dck_w2, cv_pe=dcv_pe.reshape(32, DH), cv_w2=dcv_w2)
    return dx, dg, grads


def _swa_fwd(x, g, w, cc, ss, tag):
    S = x.shape[0]
    h = _rms_fwd(x, g, name=tag + "_rms")
    hm = _mm(h, w["w_in"], tm=512, tn=640, o_lead=("heads",), ctx=w["ctx"], name=tag + "_in")
    roped = _rope(hm, cc, ss, SWA_ROTATED, name=tag + "_rope")
    sinks = jnp.broadcast_to(w["sinks"].reshape(SWA_G, SWA_R, 1, 1), (SWA_G, SWA_R, 8, LANES))
    o, lse = _attn_fwd(roped, roped, roped, G=SWA_G, R=SWA_R, heads=SWA_HEADS, kind="band", window=SWA_WINDOW,
                       sinks=sinks, ctx=w["ctx"], name=tag + "_attn")
    of = _from_heads(o.reshape(N_HEADS, S, DH))
    xo = _mm(of, w["out"], tm=512, tn=1024, res=x, ctx=w["ctx"], name=tag + "_out")
    return xo, (x, h, roped, sinks, o, lse, of)


def _swa_bwd(dxo, saved, g, w, cc, ss, tag):
    x, h, roped, sinks, o, lse, of = saved
    S = x.shape[0]
    do = _out_bwd(dxo, of, w, tag).reshape(SWA_G, SWA_R, S, DH)
    b = _attn_bwd(roped, roped, roped, o, do, lse, heads=SWA_HEADS, kind="band", window=SWA_WINDOW, sinks=sinks,
                  ctx=w["ctx"], name=tag + "_dattn")
    dhm = jnp.concatenate([b["dq"].reshape(N_HEADS, S, DH), b["dk"], b["dv"]], axis=0)
    dproj = _from_heads(_rope(dhm, cc, -ss, SWA_ROTATED, name=tag + "_drope"))
    dh = _mm(dproj, w["w_in"], tb=True, tm=512, tn=512, ctx=w["ctx"], name=tag + "_dh")
    dw_in = _mm(h, dproj, ta=True, tm=512, tn=640, ctx=w["ctx"], name=tag + "_dwin_w")
    dx, dg = _rms_bwd(dh, x, g, dxo, name=tag + "_drms")
    w["P"].grad("swa_in", w["j"], _split_cols(dw_in, SWA_IN))
    return dx, dg, dict(sinks=b["dsink"][:, :, 0, 0].reshape(N_HEADS))


def _fox_fwd(x, g, w, tag):
    S = x.shape[0]
    h = _rms_fwd(x, g, name=tag + "_rms")
    hm = _mm(h, w["w_in"], tm=512, tn=640, o_lead=("heads",), ctx=w["ctx"], name=tag + "_in")
    zf = jnp.pad(hm[48], ((0, 0), (0, LANES - DH)))
    bf = _pad_lanes(w["b_f"].reshape(1, N_HEADS))
    c = _fox_gate_fwd(zf, bf, name=tag + "_gate")[:, :N_HEADS]
    bias = c.T[:, None, :]
    o, lse = _attn_fwd(hm, hm, hm, G=N_HEADS, R=1, heads=FOX_HEADS, kind="causal", bias=bias, ctx=w["ctx"],
                       name=tag + "_attn")
    of = _from_heads(o.reshape(N_HEADS, S, DH))
    xo = _mm(of, w["out"], tm=512, tn=1024, res=x, ctx=w["ctx"], name=tag + "_out")
    return xo, (x, h, hm, zf, bf, bias, o, lse, of)


def _fox_bwd(dxo, saved, g, w, tag):
    x, h, hm, zf, bf, bias, o, lse, of = saved
    S = x.shape[0]
    do = _out_bwd(dxo, of, w, tag).reshape(N_HEADS, 1, S, DH)
    b = _attn_bwd(hm, hm, hm, o, do, lse, heads=FOX_HEADS, kind="causal", bias=bias, ctx=w["ctx"], name=tag + "_dattn")
    dzf, db = _fox_gate_bwd(zf, bf, _pad_lanes(b["dc"].reshape(N_HEADS, S).T), name=tag + "_dgate")
    dgates = dzf.reshape(S, 2, DH).transpose(1, 0, 2)
    dhm = jnp.concatenate([b["dq"].reshape(N_HEADS, S, DH), b["dk"], b["dv"], dgates], axis=0)
    dproj = _from_heads(dhm)
    dh = _mm(dproj, w["w_in"], tb=True, tm=512, tn=512, ctx=w["ctx"], name=tag + "_dh")
    dw_in = _mm(h, dproj, ta=True, tm=512, tn=640, ctx=w["ctx"], name=tag + "_dwin_w")
    dx, dg = _rms_bwd(dh, x, g, dxo, name=tag + "_drms")
    w["P"].grad("fox_in", w["j"], _split_cols(dw_in, FOX_IN))
    return dx, dg, dict(b_f=db[0, :N_HEADS])


GROUPS = {
    "gu": (("ffn1_w_gu", "ffn2_w_gu"), 2),
    "down": (("ffn1_w_down", "ffn2_w_down"), 1),
    "out": (("nsa_w_out", "swa_w_out", "fox_w_out"), 1),
    "cw1": (("nsa_ck_w1", "nsa_cv_w1"), 1),
    "nsa_in": (("nsa_w_in",), 2),
    "swa_in": (("swa_w_in",), 2),
    "fox_in": (("fox_w_in",), 2),
}
OUT_SLAB = {0: 0, 3: 1, 1: 2, 2: 3}


def _pieces_in_order():
    chunks = []
    for i in range(DEPTH):
        kind, j = i % 3, i // 3
        chunks.append([("gu", i), ("down", i)])
        if kind == 0:
            chunks.append([("nsa_in", j), ("cw1", j), ("cw1", 2 + j), ("out", OUT_SLAB[i])])
        else:
            chunks.append([("swa_in" if kind == 1 else "fox_in", j), ("out", OUT_SLAB[i])])
        chunks.append([("gu", DEPTH + i), ("down", DEPTH + i)])
    return chunks


def _consumer_layout(group, F):
    _, rows, C = F.shape
    if group == "gu":
        return F
    if GROUPS[group][1] == 1:
        return F.reshape(4 * rows, C)
    w = F.transpose(1, 0, 2).reshape(rows, 4 * C)
    pad = {"nsa_in": NSA_IN_PAD, "swa_in": SWA_IN, "fox_in": FOX_IN_PAD}[group] - 4 * C
    return jnp.pad(w, ((0, 0), (0, pad)))


class _Given:
    def __init__(self, pieces, small):
        self.pieces, self.small, self.ctx, self.grads = pieces, small, None, {}

    def weight(self, group, l):
        return _consumer_layout(group, self.pieces[group, l])

    def grad(self, group, l, G):
        self.grads[group, l] = G


class _MixerWeights(dict):
    def __init__(self, P, pieces, **given):
        super().__init__(P=P, ctx=P.ctx, **given)
        self.pieces = pieces

    def __missing__(self, key):
        self[key] = self["P"].weight(*self.pieces[key])
        return self[key]


def _mixer_weights(P, i):
    kind, j = i % 3, i // 3
    out = {"out": ("out", OUT_SLAB[i])}
    if kind == 0:
        small = {k: P.small["nsa_" + k][j] for k in ("ck_pe", "ck_w2", "cv_pe", "cv_w2")}
        return _MixerWeights(P, dict(out, w_in=("nsa_in", j), ck_w1=("cw1", j), cv_w1=("cw1", 2 + j)), j=j,
                             out_l=OUT_SLAB[i], **small)
    if kind == 1:
        return _MixerWeights(P, dict(out, w_in=("swa_in", j)), j=j, out_l=OUT_SLAB[i], sinks=P.small["swa_sinks"][j])
    return _MixerWeights(P, dict(out, w_in=("fox_in", j)), j=j, out_l=OUT_SLAB[i], b_f=P.small["fox_b_f"][j])


def _local_step(x, tgt, P):
    S = x.shape[0]
    cc, ss = _rope_tables(S)
    sm = P.small
    saved = []
    for i in range(DEPTH):
        kind = i % 3
        x, s1 = _ffn_fwd(x, sm["ffn1_norm"][i], P, i, f"l{i}f1")
        mw = _mixer_weights(P, i)
        if kind == 0:
            x, s2 = _nsa_fwd(x, sm["mix_norm"][i], mw, cc, ss, f"l{i}nsa")
        elif kind == 1:
            x, s2 = _swa_fwd(x, sm["mix_norm"][i], mw, cc, ss, f"l{i}swa")
        else:
            x, s2 = _fox_fwd(x, sm["mix_norm"][i], mw, f"l{i}fox")
        x, s3 = _ffn_fwd(x, sm["ffn2_norm"][i], P, DEPTH + i, f"l{i}f2")
        saved.append((s1, mw, s2, s3))
    loss, dx, d_final = _loss_head(x, sm["final_norm"], tgt, name="loss_head")

    norms = {k: [None] * DEPTH for k in ("ffn1_norm", "mix_norm", "ffn2_norm")}
    mix = {}
    for i in reversed(range(DEPTH)):
        kind, j = i % 3, i // 3
        s1, mw, s2, s3 = saved[i]
        dx, norms["ffn2_norm"][i] = _ffn_bwd(dx, s3, sm["ffn2_norm"][i], P, DEPTH + i, f"l{i}f2")
        if kind == 0:
            dx, dg, gm = _nsa_bwd(dx, s2, sm["mix_norm"][i], mw, cc, ss, f"l{i}nsa")
            pre = "nsa_"
        elif kind == 1:
            dx, dg, gm = _swa_bwd(dx, s2, sm["mix_norm"][i], mw, cc, ss, f"l{i}swa")
            pre = "swa_"
        else:
            dx, dg, gm = _fox_bwd(dx, s2, sm["mix_norm"][i], mw, f"l{i}fox")
            pre = "fox_"
        norms["mix_norm"][i] = dg
        for k, val in gm.items():
            mix.setdefault(pre + k, {})[j] = val
        dx, norms["ffn1_norm"][i] = _ffn_bwd(dx, s1, sm["ffn1_norm"][i], P, i, f"l{i}f1")
    small = {k: jnp.stack(v, axis=0) for k, v in norms.items()}
    small.update({k: jnp.stack([d[j] for j in sorted(d)], axis=0) for k, d in mix.items()})
    small["final_norm"] = d_final
    return loss, dx, small


def _sum_slots(own, recv, me, into, row0, *, name):
    _, R, C = recv.shape
    tr = _row_tile(math.gcd(R, row0), C)
    blk0 = row0 // tr

    def body(_, g_ref, r_ref, __, o_ref):
        acc = g_ref[...].astype(F32) + r_ref[0].astype(F32)
        acc = acc + r_ref[1].astype(F32)
        o_ref[...] = acc + r_ref[2].astype(F32)

    spec = pltpu.PrefetchScalarGridSpec(
        num_scalar_prefetch=1, grid=(R // tr,),
        in_specs=[pl.BlockSpec((None, tr, C), lambda i, me_ref: (me_ref[0], i, 0)),
                  pl.BlockSpec((3, tr, C), lambda i, me_ref: (0, i, 0)), _ANY],
        out_specs=pl.BlockSpec((tr, C), lambda i, me_ref: (blk0 + i, 0)))
    return pl.pallas_call(body, grid_spec=spec, out_shape=jax.ShapeDtypeStruct(into.shape, F32),
                          input_output_aliases={3: 0}, compiler_params=_params(), name=name)(me, own, recv, into)


def _swap_sibling(parts):
    n = len(parts)

    def body(*refs):
        srcs, outs = refs[:n], refs[n:2 * n]
        send_sems, recv_sems = refs[2 * n:]
        x, y, c = lax.axis_index("x"), lax.axis_index("y"), lax.axis_index("c")
        cps = [pltpu.make_async_remote_copy(src_ref=srcs[g], dst_ref=outs[g], send_sem=send_sems.at[g],
                                            recv_sem=recv_sems.at[g], device_id=(x, y, 1 - c), device_id_type=MESH)
               for g in range(n)]
        for cp in cps:
            cp.start()
        for cp in cps:
            cp.wait()

    return pl.pallas_call(
        body, out_shape=tuple(jax.ShapeDtypeStruct(p.shape, p.dtype) for p in parts),
        in_specs=[_ANY] * n, out_specs=(_ANY,) * n,
        scratch_shapes=[pltpu.SemaphoreType.DMA((n,)), pltpu.SemaphoreType.DMA((n,))],
        compiler_params=pltpu.CompilerParams(has_side_effects=True), name="swap_core_grads")(*parts)


def _flip(coord, bit):
    return 1 - coord if bit else coord


def _allreduce_small(v):
    n, C = v.shape

    def body(v_ref, o_ref, buf, send_sems, recv_sems):
        x, y, c = lax.axis_index("x"), lax.axis_index("y"), lax.axis_index("c")
        me = 4 * x + 2 * y + c
        buf[me] = v_ref[...]
        peers = [(_flip(x, (j >> 2) & 1), _flip(y, (j >> 1) & 1), _flip(c, j & 1)) for j in range(1, 8)]
        sends = [pltpu.make_async_remote_copy(src_ref=v_ref, dst_ref=buf.at[me], send_sem=send_sems.at[j],
                                              recv_sem=recv_sems.at[j], device_id=peer, device_id_type=MESH)
                 for j, peer in enumerate(peers)]
        for cp in sends:
            cp.start()
        for j, (px, py, pc) in enumerate(peers):
            pltpu.make_async_remote_copy(src_ref=v_ref, dst_ref=buf.at[4 * px + 2 * py + pc], send_sem=send_sems.at[j],
                                         recv_sem=recv_sems.at[j], device_id=(px, py, pc),
                                         device_id_type=MESH).wait_recv()
        for cp in sends:
            cp.wait_send()
        acc = buf[0]
        for d in range(1, 8):
            acc = acc + buf[d]
        o_ref[...] = acc

    return pl.pallas_call(
        body, out_shape=jax.ShapeDtypeStruct((n, C), F32),
        in_specs=[pl.BlockSpec(memory_space=pltpu.VMEM)], out_specs=pl.BlockSpec(memory_space=pltpu.VMEM),
        scratch_shapes=[pltpu.VMEM((8, n, C), F32), pltpu.SemaphoreType.DMA((7,)), pltpu.SemaphoreType.DMA((7,))],
        compiler_params=pltpu.CompilerParams(has_side_effects=True), name="allreduce_small")(v)


def _adamw(w, m, v, gs, *, row0=0, name):
    shape = w.shape
    w3, m3, v3 = (a.reshape((-1,) + a.shape[-2:]) for a in (w, m, v))
    g2 = [_rows2d(g) for g in gs]
    L, rows, C = w3.shape
    tr = _row_tile(math.gcd(rows, row0), C, budget=1024 * 1024)
    ng = len(g2)
    blk0, nb = row0 // tr, rows // tr

    def body(*refs):
        w_ref, m_ref, v_ref = refs[:3]
        g = refs[3][...]
        for r in refs[4:3 + ng]:
            g = g + r[...]
        g_ref, d_ref, nm_ref, nv_ref = refs[3 + ng:]
        mn = B1 * m_ref[...] + (1.0 - B1) * g
        vn = B2 * v_ref[...] + (1.0 - B2) * (g * g)
        m_hat = mn / (1.0 - B1 ** STEP)
        v_hat = vn / (1.0 - B2 ** STEP)
        g_ref[...] = g
        d_ref[...] = -LR * (m_hat / (jnp.sqrt(v_hat) + EPS) + WD * w_ref[...])
        nm_ref[...] = mn
        nv_ref[...] = vn

    spec = pl.BlockSpec((None, tr, C), lambda l, i: (l, i, 0))
    gspec = pl.BlockSpec((tr, C), lambda l, i: (blk0 + l * nb + i, 0))
    outs = pl.pallas_call(body, out_shape=tuple(jax.ShapeDtypeStruct((L, rows, C), F32) for _ in range(4)),
                          grid=(L, nb), in_specs=[spec] * 3 + [gspec] * ng, out_specs=(spec,) * 4,
                          compiler_params=_params(), name=name)(w3, m3, v3, *g2)
    return tuple(o.reshape(shape) for o in outs)


def _small_layout(shapes):
    offs, off = {}, 0
    for k in REPLICATED:
        n = int(np.prod(shapes[k]))
        offs[k] = (off, n)
        off += -(-n // LANES) * LANES
    return offs, off


def _pack_small(d, shapes):
    offs, total = _small_layout(shapes)
    parts = []
    for k in REPLICATED:
        n = offs[k][1]
        parts.append(jnp.pad(d[k].reshape(-1).astype(F32), (0, -(-n // LANES) * LANES - n)))
    rows = -(-(total // LANES) // 8) * 8
    return jnp.pad(jnp.concatenate(parts), (0, rows * LANES - total)).reshape(rows, LANES)


def _unpack_small(a, shapes):
    offs, _ = _small_layout(shapes)
    flat = a.reshape(-1)
    return {k: flat[offs[k][0]:offs[k][0] + offs[k][1]].reshape(shapes[k]) for k in REPLICATED}


def _group_shards(w):
    shards = []
    for members, _ in GROUPS.values():
        s = jnp.concatenate([w[k].astype(_WIRE) for k in members], axis=0)
        shards.append(s.reshape(s.shape[0], 2, s.shape[1] // 2, s.shape[2]))
    return shards


class _Exchanged:
    def __init__(self, w):
        self.small = {k: w[k] for k in REPLICATED}
        self.ctx = _Sched()
        shards = dict(zip(GROUPS, _group_shards(w)))
        self.gather, self.cache, self.scatter = {}, {}, {}
        chunks = _pieces_in_order()
        self.me = jnp.reshape(2 * lax.axis_index("x") + lax.axis_index("y"), (1,)).astype(jnp.int32)
        first = {(g, l): _gather_task1(shards[g], l, _place_own(shards[g], l, self.me, name=f"own_{g}{l}"))
                 for chunk in chunks for g, l in chunk}
        self.gather = {p: _gather_task2(task) for p, task in first.items()}
        pieces = [p for chunk in chunks for p in chunk]
        for n, p in enumerate(pieces):
            self.ctx.push(first[p])
            if n:
                self.ctx.push(self.gather[pieces[n - 1]])
        self.ctx.push(self.gather[pieces[-1]])

    def weight(self, group, l):
        if (group, l) not in self.cache:
            task = self.gather[group, l]
            self.ctx.finish(task)
            F = task.result()
            self.cache[group, l] = _consumer_layout(group, F.reshape(4, -1, F.shape[-1]))
        return self.cache[group, l]

    def grad(self, group, l, G):
        self.scatter[group, l] = (G, self.ctx.push(_scatter_task(G)))

    def partial_sums(self, w):
        self.ctx.finish()
        parts = []
        for group, (members, _) in GROUPS.items():
            rows, C = w[members[0]].shape[1:]
            n = sum(w[k].shape[0] for k in members)
            part = jnp.zeros((n * rows, C), F32)
            for l in range(n):
                G, task = self.scatter[group, l]
                part = _sum_slots(G, task.result(), self.me, part, l * rows, name=f"sum_{group}{l}")
            parts.append(part)
        return parts


def _reduce_and_update(w, m, v, parts, small):
    others = _swap_sibling(parts)
    small_shapes = {k: w[k].shape for k in REPLICATED}
    g_small = _unpack_small(_allreduce_small(_pack_small(small, small_shapes)), small_shapes)

    out_g, out_d, out_m, out_v = {}, {}, {}, {}
    for (members, _), part, other in zip(GROUPS.values(), parts, others):
        row0 = 0
        for k in members:
            out_g[k], out_d[k], out_m[k], out_v[k] = _adamw(w[k], m[k], v[k], [part, other], row0=row0,
                                                            name="adamw_" + k)
            row0 += w[k].shape[0] * w[k].shape[1]
    sm = _adamw(_pack_small(w, small_shapes), _pack_small(m, small_shapes), _pack_small(v, small_shapes),
                [_pack_small(g_small, small_shapes)], name="adamw_small")
    for d, packed in zip((out_g, out_d, out_m, out_v), sm):
        d.update(_unpack_small(packed, small_shapes))
    return out_g, out_d, out_m, out_v


def kernel(x, ffn1_norm, ffn1_w_gu, ffn1_w_down, mix_norm, ffn2_norm, ffn2_w_gu, ffn2_w_down, nsa_w_in, nsa_ck_pe, nsa_ck_w1, nsa_ck_w2, nsa_cv_pe, nsa_cv_w1, nsa_cv_w2, nsa_w_out, swa_w_in, swa_sinks, swa_w_out, fox_w_in, fox_b_f, fox_w_out, final_norm, loss_target, m_ffn1_norm, m_ffn1_w_gu, m_ffn1_w_down, m_mix_norm, m_ffn2_norm, m_ffn2_w_gu, m_ffn2_w_down, m_nsa_w_in, m_nsa_ck_pe, m_nsa_ck_w1, m_nsa_ck_w2, m_nsa_cv_pe, m_nsa_cv_w1, m_nsa_cv_w2, m_nsa_w_out, m_swa_w_in, m_swa_sinks, m_swa_w_out, m_fox_w_in, m_fox_b_f, m_fox_w_out, m_final_norm, v_ffn1_norm, v_ffn1_w_gu, v_ffn1_w_down, v_mix_norm, v_ffn2_norm, v_ffn2_w_gu, v_ffn2_w_down, v_nsa_w_in, v_nsa_ck_pe, v_nsa_ck_w1, v_nsa_ck_w2, v_nsa_cv_pe, v_nsa_cv_w1, v_nsa_cv_w2, v_nsa_w_out, v_swa_w_in, v_swa_sinks, v_swa_w_out, v_fox_w_in, v_fox_b_f, v_fox_w_out, v_final_norm):
    args = dict(locals())
    w = {k: args[k] for k in WEIGHTS}
    m = {k: args["m_" + k] for k in WEIGHTS}
    v = {k: args["v_" + k] for k in WEIGHTS}
    P = _Exchanged(w)
    loss_part, dx, small = _local_step(x[0], loss_target[0], P)
    loss = lax.psum(loss_part, ("x", "y", "c"))
    out_g, out_d, out_m, out_v = _reduce_and_update(w, m, v, P.partial_sums(w), small)
    return (loss, dx[None], *[out_g[k] for k in WEIGHTS], *[out_d[k] for k in WEIGHTS],
            *[out_m[k] for k in WEIGHTS], *[out_v[k] for k in WEIGHTS])
```

```python
import math

import numpy as np
import jax
import jax.numpy as jnp
from jax import lax
from jax.experimental import pallas as pl
from jax.experimental.pallas import tpu as pltpu

F32 = jnp.float32
_CDT = jnp.bfloat16
_WIRE = jnp.bfloat16
_VMEM_LIMIT = 56 * 1024 * 1024

D_MODEL = 1024
DEPTH = 4
DH = 64
N_HEADS = 16
RMS_EPS = 1e-6
NEG = -1e30
SCALE = DH ** -0.5
BQ = 256
LANES = 128
NSA_G, NSA_R = 4, 4
NSA_WINDOW = 512
NSA_TOPK = 16
NSA_BONUS = 1e4
SWA_G, SWA_R = 2, 8
SWA_WINDOW = 128
NSA_ROTATED = tuple(hd < 16 or (hd < 40 and (hd - 16) % 8 < 4) for hd in range(42))
NSA_IN, NSA_IN_PAD = 2608, 2688
SWA_IN = 1280
SWA_ROTATED = (True,) * 18 + (False,) * 2
SWA_HEADS = (0, 16, 18)
FOX_IN, FOX_IN_PAD = 3088, 3200
FOX_HEADS = (0, 16, 32)
LR, B1, B2, EPS, WD, STEP = 0.001, 0.9, 0.999, 1e-08, 0.01, 10
MESH = pl.DeviceIdType.MESH

REPLICATED = ["ffn1_norm", "mix_norm", "ffn2_norm", "nsa_ck_pe", "nsa_ck_w2", "nsa_cv_pe", "nsa_cv_w2",
              "swa_sinks", "fox_b_f", "final_norm"]
WEIGHTS = ['ffn1_norm', 'ffn1_w_gu', 'ffn1_w_down', 'mix_norm', 'ffn2_norm', 'ffn2_w_gu', 'ffn2_w_down',
           'nsa_w_in', 'nsa_ck_pe', 'nsa_ck_w1', 'nsa_ck_w2', 'nsa_cv_pe', 'nsa_cv_w1', 'nsa_cv_w2', 'nsa_w_out',
           'swa_w_in', 'swa_sinks', 'swa_w_out', 'fox_w_in', 'fox_b_f', 'fox_w_out', 'final_norm']


def _params(**kw):
    return pltpu.CompilerParams(vmem_limit_bytes=_VMEM_LIMIT, **kw)


def _dot(a, b):
    return lax.dot_general(a, b, (((1,), (0,)), ((), ())), preferred_element_type=F32)


def _dot_nt(a, b):
    return lax.dot_general(a, b, (((1,), (1,)), ((), ())), preferred_element_type=F32)


def _dot_tn(a, b):
    return lax.dot_general(a, b, (((0,), (0,)), ((), ())), preferred_element_type=F32)


def _split3(x):
    hi = x.astype(jnp.bfloat16)
    r1 = x - hi.astype(F32)
    mid = r1.astype(jnp.bfloat16)
    lo = (r1 - mid.astype(F32)).astype(jnp.bfloat16)
    return hi, mid, lo


def _dot_exact(x, p):
    hi, mid, lo = _split3(x)
    return _dot(hi, p) + _dot(mid, p) + _dot(lo, p)


def _dot_exact_left(p, x):
    hi, mid, lo = _split3(x)
    return _dot(p, hi) + _dot(p, mid) + _dot(p, lo)


_ANY = pl.BlockSpec(memory_space=pl.ANY)


def _chip_peers():
    x, y, c = lax.axis_index("x"), lax.axis_index("y"), lax.axis_index("c")
    return x, y, c, [(1 - x, y), (x, 1 - y), (1 - x, 1 - y)]


class _Job:
    def __init__(self, ins, outs, nsem, copies, cost, alias):
        self.ins, self.outs, self.nsem, self.copies, self.cost, self.alias = ins, outs, nsem, copies, cost, alias
        self.results = None

    def inputs(self):
        return self.ins


class _Task:
    def __init__(self, src, out, parts, init=None):
        self.src, self.out, self.parts, self.init, self.done, self.jobs = src, out, parts, init, 0, []

    def result(self):
        return self.jobs[-1].results[0]

    def complete(self):
        return self.done == len(self.parts) and self.jobs[-1].results is not None

    def ready(self):
        if isinstance(self.src, _Task) and not self.src.complete():
            return False
        return not self.jobs or self.jobs[-1].results is not None

    def next_job(self, n):
        parts = self.parts[self.done:self.done + n]
        in_place = isinstance(self.src, _Task)
        ins = [] if in_place else [self.src]
        prev = self.jobs[-1].results[0] if self.jobs else (self.src.result() if in_place else self.init)
        alias = {}
        if prev is not None:
            alias = {len(ins): 0}
            ins = ins + [prev]

        def copies(in_refs, out_refs, sems):
            src = out_refs[0] if in_place else in_refs[0]
            out, first = [], 0
            for build, nsem, _ in parts:
                out += build(src, out_refs[0], sems, first)
                first += nsem
            return out

        job = _Job(ins, [self.out], sum(p[1] for p in parts), copies, sum(p[2] for p in parts), alias)
        self.done += n
        self.jobs.append(job)
        return job


def _pcall(body, *, out_shape, in_specs, out_specs, name, grid=(), scratch_shapes=(), aliases=None, jobs=(), ins):
    single = not isinstance(out_shape, (tuple, list))
    out_shape = (out_shape,) if single else tuple(out_shape)
    out_specs = (out_specs,) if single else tuple(out_specs)
    n_in, n_out, n_scr = len(ins), len(out_shape), len(scratch_shapes)
    side_ins = [a for j in jobs for a in j.inputs()]
    side_outs = [o for j in jobs for o in j.outs]
    aliases = dict(aliases or {})
    i0, o0 = n_in, n_out
    for j in jobs:
        for a, b in j.alias.items():
            aliases[i0 + a] = o0 + b
        i0 += len(j.ins)
        o0 += len(j.outs)

    def body2(*refs):
        in_refs, s_in = refs[:n_in], refs[n_in:n_in + len(side_ins)]
        r = n_in + len(side_ins)
        out_refs, s_out = refs[r:r + n_out], refs[r + n_out:r + n_out + len(side_outs)]
        r += n_out + len(side_outs)
        scr, sems = refs[r:r + n_scr], refs[r + n_scr:]
        def descriptors():
            out, a, b = [], 0, 0
            for j, sem in zip(jobs, sems):
                out += j.copies(s_in[a:a + len(j.ins)], s_out[b:b + len(j.outs)], sem)
                a += len(j.ins)
                b += len(j.outs)
            return out

        def at_step(steps, fn):
            cond = None
            for d, s in enumerate(steps):
                hit = pl.program_id(d) == s
                cond = hit if cond is None else jnp.logical_and(cond, hit)
            if cond is None:
                fn()
            else:
                pl.when(cond)(fn)

        def start_all():
            for cp in descriptors():
                cp.start()

        def wait_all():
            for cp in descriptors():
                cp.wait()

        if jobs:
            at_step([0] * len(grid), start_all)
        body(*in_refs, *out_refs, *scr)
        if jobs:
            at_step([n - 1 for n in grid], wait_all)

    res = pl.pallas_call(
        body2, out_shape=out_shape + tuple(side_outs), grid=grid, in_specs=list(in_specs) + [_ANY] * len(side_ins),
        out_specs=out_specs + (_ANY,) * len(side_outs),
        scratch_shapes=list(scratch_shapes) + [pltpu.SemaphoreType.DMA((j.nsem,)) for j in jobs],
        input_output_aliases=aliases,
        compiler_params=_params(has_side_effects=True) if jobs else _params(), name=name)(*ins, *side_ins)
    b = n_out
    for j in jobs:
        j.results = list(res[b:b + len(j.outs)])
        b += len(j.outs)
    return res[0] if single else tuple(res[:n_out])


def _comm_call(jobs, name):
    def body():
        pass

    _pcall(body, out_shape=(), in_specs=[], out_specs=(), name=name, jobs=jobs, ins=[])


def _place_own(shard, l, me, *, name):
    _, _, h, C = shard.shape

    def body(_, s_ref, o_ref):
        o_ref[...] = s_ref[...]

    spec = pltpu.PrefetchScalarGridSpec(
        num_scalar_prefetch=1, grid=(2,),
        in_specs=[pl.BlockSpec((None, None, h, C), lambda i, me_ref: (l, i, 0, 0))],
        out_specs=pl.BlockSpec((None, None, h, C), lambda i, me_ref: (me_ref[0], i, 0, 0)))
    return pl.pallas_call(body, grid_spec=spec, out_shape=jax.ShapeDtypeStruct((4, 2, h, C), shard.dtype),
                          compiler_params=_params(), name=name)(me, shard)


def _gather_task1(shard, l, own):
    _, _, h, C = shard.shape
    mb = 3 * h * C * jnp.dtype(_WIRE).itemsize / 1e6
    n = _row_splits(h, ICI_US_PER_MB * mb)
    hr = h // n

    def rows_to_peers(r):
        def build(src, dst, sems, s0):
            x, y, c, chips = _chip_peers()
            me, rows = 2 * x + y, pl.ds(r * hr, hr)
            return [pltpu.make_async_remote_copy(src_ref=src.at[l, c, rows], dst_ref=dst.at[me, c, rows],
                                                 send_sem=sems.at[s0 + 2 * j], recv_sem=sems.at[s0 + 2 * j + 1],
                                                 device_id=(px, py, c), device_id_type=MESH)
                    for j, (px, py) in enumerate(chips)]
        return build

    parts = [(rows_to_peers(r), 6, ICI_US_PER_MB * mb / n) for r in range(n)]
    return _Task(shard, jax.ShapeDtypeStruct((4, 2, h, C), shard.dtype), parts, init=own)


def _gather_task2(task1):
    _, _, h, C = task1.out.shape
    mb = h * C * jnp.dtype(_WIRE).itemsize / 1e6

    def forward(j):
        def build(_, dst, sems, s0):
            x, y, c, chips = _chip_peers()
            px, py = chips[j]
            landed = dst.at[2 * px + py, c]
            return [pltpu.make_async_remote_copy(src_ref=landed, dst_ref=landed, send_sem=sems.at[s0],
                                                 recv_sem=sems.at[s0 + 1], device_id=(x, y, 1 - c),
                                                 device_id_type=MESH)]
        return build

    return _Task(task1, task1.out, [(forward(j), 2, 1.0 + D2D_US_PER_MB * mb) for j in range(3)])


def _scatter_task(G):
    mb = 3 * G.shape[1] * G.shape[2] * jnp.dtype(_WIRE).itemsize / 1e6
    n = _row_splits(G.shape[1], ICI_US_PER_MB * mb)
    rr = G.shape[1] // n

    def rows_to_peers(r):
        def build(src, dst, sems, s0):
            _, _, c, chips = _chip_peers()
            rows = pl.ds(r * rr, rr)
            return [pltpu.make_async_remote_copy(src_ref=src.at[2 * px + py, rows], dst_ref=dst.at[j, rows],
                                                 send_sem=sems.at[s0 + 2 * j], recv_sem=sems.at[s0 + 2 * j + 1],
                                                 device_id=(px, py, c), device_id_type=MESH)
                    for j, (px, py) in enumerate(chips)]
        return build

    parts = [(rows_to_peers(r), 6, ICI_US_PER_MB * mb / n) for r in range(n)]
    return _Task(G, jax.ShapeDtypeStruct((3,) + G.shape[1:], G.dtype), parts)


def _row_splits(rows, cost):
    n = 8
    while n > 1 and (rows % (16 * n) or cost / n < PART_US):
        n //= 2
    return n


CARRIER_US = {
    "f_gu": 48, "f_down": 20, "f_dact": 42, "f_dwd": 20, "f_dwgu": 42, "f_dh": 46,
    "nsa_in": 25, "nsa_cmp": 86, "nsa_slc": 122, "nsa_win": 74, "nsa_dcmp": 58, "nsa_dslc": 195, "nsa_dwin": 108,
    "nsa_dh": 26, "nsa_dwin_w": 25,
    "swa_in": 16, "swa_attn": 74, "swa_dattn": 74, "swa_dh": 17, "swa_dwin_w": 18,
    "fox_in": 34, "fox_attn": 104, "fox_dattn": 161, "fox_dh": 30, "fox_dwin_w": 30,
}
ICI_US_PER_MB = 15.0
PART_US = 12.0
D2D_US_PER_MB = 2.3
LOCAL_US_PER_MB = 1.5


class _Sched:
    def __init__(self):
        self.queue, self.credit, self.n_flush = [], 0.0, 0

    def push(self, task):
        self.queue.append(task)
        return task

    def _jobs(self, fits):
        jobs = []
        while self.queue and self.queue[0].ready():
            task, n = self.queue[0], 0
            while task.done + n < len(task.parts) and fits(task.parts[task.done + n][2]):
                n += 1
            if n:
                jobs.append(task.next_job(n))
            if task.done < len(task.parts):
                break
            self.queue.pop(0)
        return jobs

    def take(self, name):
        kind = name[2:].split("_", 1)
        key = ("f" if kind[0] in ("f1", "f2") else kind[0]) + "_" + kind[1]
        self.credit = min(self.credit, 0.0) + CARRIER_US.get(key, 0.0)

        def fits(cost):
            if self.credit < 0.5 * cost:
                return False
            self.credit -= cost
            return True

        return self._jobs(fits)

    def finish(self, task=None):
        while self.queue and (task is None or not task.complete()):
            jobs = []
            while self.queue and self.queue[0].ready():
                head = self.queue.pop(0)
                jobs.append(head.next_job(len(head.parts) - head.done))
                if head is task:
                    break
            assert jobs, "the task at the head of the queue waits for one that was never queued"
            _comm_call(jobs, name=f"exchange_{self.n_flush}")
            self.n_flush += 1


def _mm(a, b, *, ta=False, tb=False, tm, tn, b_lead=(), into=None, o_lead=(), out_dtype=F32, res=None, alpha=1.0,
        ctx=None, name):
    M = a.shape[1] if ta else a.shape[0]
    K = a.shape[0] if ta else a.shape[1]
    bk, bn = (b.shape[-1], b.shape[-2]) if tb else (b.shape[-2], b.shape[-1])
    j_lead, k_lead = "j" in b_lead, "k" in b_lead
    N = bn * (b.shape[b_lead.index("j")] if j_lead else 1)
    nk = b.shape[b_lead.index("k")] if k_lead else 1
    tk = K // nk
    assert tk == bk, (name, K, nk, bk)
    tm = min(tm, M)
    tn = bn if j_lead else min(tn, N)
    assert M % tm == 0 and N % tn == 0, (name, M, N, tm, tn)
    nb, no = len(b_lead), len(o_lead)

    def pick(lead, j, k):
        return tuple(j if t == "j" else k if t == "k" else t for t in lead)

    a_spec = pl.BlockSpec((tk, tm), lambda i, j, k: (k, i)) if ta else pl.BlockSpec((tm, tk), lambda i, j, k: (i, k))
    if tb:
        b_spec = pl.BlockSpec((None,) * nb + (tn, tk),
                              lambda i, j, k: pick(b_lead, j, k) + (0 if j_lead else j, 0 if k_lead else k))
    else:
        b_spec = pl.BlockSpec((None,) * nb + (tk, tn),
                              lambda i, j, k: pick(b_lead, j, k) + (0 if k_lead else k, 0 if j_lead else j))
    r_spec = pl.BlockSpec((tm, tn), lambda i, j, k: (i, j))
    if o_lead == ("heads",):
        o_spec = pl.BlockSpec((tn // DH, tm, DH), lambda i, j, k: (j, i, 0))
    else:
        o_spec = pl.BlockSpec((None,) * no + (tm, tn),
                              lambda i, j, k: pick(o_lead, j, k) + (i, 0 if "j" in o_lead else j))
    dn = (((0 if ta else 1,), (1 if tb else 0,)), ((), ()))
    has_res, has_into = res is not None, into is not None
    if has_into:
        out_dtype = into.dtype

    def body(*refs):
        a_ref, b_ref = refs[0], refs[1]
        r_ref = refs[2] if has_res else None
        o_ref = refs[2 + has_res + has_into]
        prod = lax.dot_general(a_ref[...].astype(_CDT), b_ref[...].astype(_CDT), dn, preferred_element_type=F32)

        def finish(acc):
            if alpha != 1.0:
                acc = acc * alpha
            if has_res:
                acc = r_ref[...] + acc
            if o_lead == ("heads",):
                for hd in range(tn // DH):
                    o_ref[hd] = acc[:, hd * DH:(hd + 1) * DH].astype(out_dtype)
            else:
                o_ref[...] = acc.astype(out_dtype)

        if nk == 1:
            finish(prod)
        else:
            acc_ref = refs[-1]
            k = pl.program_id(2)

            @pl.when(k == 0)
            def _():
                acc_ref[...] = prod

            @pl.when(k != 0)
            def _():
                acc_ref[...] += prod

            @pl.when(k == nk - 1)
            def _():
                finish(acc_ref[...])

    ins, specs = [a, b], [a_spec, b_spec]
    if has_res:
        ins.append(res)
        specs.append(r_spec)
    aliases = {}
    if has_into:
        aliases = {len(ins): 0}
        ins.append(into)
        specs.append(_ANY)
        out_shape = jax.ShapeDtypeStruct(into.shape, into.dtype)
    elif o_lead == ("j",):
        out_shape = jax.ShapeDtypeStruct((N // tn, M, tn), out_dtype)
    elif o_lead == ("heads",):
        out_shape = jax.ShapeDtypeStruct((N // DH, M, DH), out_dtype)
    else:
        assert not o_lead
        out_shape = jax.ShapeDtypeStruct((M, N), out_dtype)
    scratch = [pltpu.VMEM((tm, tn), F32)] if nk > 1 else []
    return _pcall(body, out_shape=out_shape, grid=(M // tm, N // tn, nk), in_specs=specs, out_specs=o_spec,
                  scratch_shapes=scratch, aliases=aliases, name=name, jobs=ctx.take(name) if ctx else (), ins=ins)


def _rows2d(a):
    return a.reshape(-1, a.shape[-1])


def _row_tile(rows, cols, itemsize=4, budget=2 * 1024 * 1024):
    t = rows
    while t % 2 == 0 and t * cols * itemsize > budget and (t // 2) % 8 == 0:
        t //= 2
    return t


def _addn(*xs, name):
    shape = xs[0].shape
    x2 = [_rows2d(x) for x in xs]
    R, C = x2[0].shape
    tr = _row_tile(R, C)

    def body(*refs):
        acc = refs[0][...]
        for r in refs[1:-1]:
            acc = acc + r[...]
        refs[-1][...] = acc

    spec = pl.BlockSpec((tr, C), lambda i: (i, 0))
    out = pl.pallas_call(body, out_shape=jax.ShapeDtypeStruct((R, C), F32), grid=(R // tr,),
                         in_specs=[spec] * len(x2), out_specs=spec, compiler_params=_params(), name=name)(*x2)
    return out.reshape(shape)


def _rms_fwd(x, g, *, name):
    S, D = x.shape
    tr = 256

    def body(x_ref, g_ref, h_ref):
        xv = x_ref[...]
        rstd = lax.rsqrt(jnp.mean(xv * xv, axis=-1, keepdims=True) + RMS_EPS)
        h_ref[...] = (xv * rstd * g_ref[...]).astype(_CDT)

    return pl.pallas_call(body, out_shape=jax.ShapeDtypeStruct((S, D), _CDT), grid=(S // tr,),
                          in_specs=[pl.BlockSpec((tr, D), lambda i: (i, 0)), pl.BlockSpec((1, D), lambda i: (0, 0))],
                          out_specs=pl.BlockSpec((tr, D), lambda i: (i, 0)), compiler_params=_params(),
                          name=name)(x, g.reshape(1, D))


def _rms_bwd(dh, x, g, dres, *, name):
    S, D = x.shape
    tr = 256

    def body(dh_ref, x_ref, g_ref, dres_ref, dx_ref, dg_ref):
        xv = x_ref[...]
        rstd = lax.rsqrt(jnp.mean(xv * xv, axis=-1, keepdims=True) + RMS_EPS)
        xhat = xv * rstd
        dhv = dh_ref[...]
        dxhat = dhv * g_ref[...]
        dx_ref[...] = dres_ref[...] + rstd * (dxhat - xhat * jnp.mean(dxhat * xhat, axis=-1, keepdims=True))

        @pl.when(pl.program_id(0) == 0)
        def _():
            dg_ref[...] = jnp.zeros_like(dg_ref)

        dg_ref[...] += jnp.sum(dhv * xhat, axis=0, keepdims=True)

    row = pl.BlockSpec((tr, D), lambda i: (i, 0))
    vec = pl.BlockSpec((1, D), lambda i: (0, 0))
    dx, dg = pl.pallas_call(body, out_shape=(jax.ShapeDtypeStruct((S, D), F32), jax.ShapeDtypeStruct((1, D), F32)),
                            grid=(S // tr,), in_specs=[row, row, vec, row], out_specs=(row, vec),
                            compiler_params=_params(), name=name)(dh, x, g.reshape(1, D), dres)
    return dx, dg.reshape(D)


def _loss_head(x, g, tgt, *, name):
    S, D = x.shape
    tr = 256

    def body(x_ref, g_ref, t_ref, loss_ref, dx_ref, dg_ref):
        xv = x_ref[...]
        rstd = lax.rsqrt(jnp.mean(xv * xv, axis=-1, keepdims=True) + RMS_EPS)
        xhat = xv * rstd
        err = xhat * g_ref[...] - t_ref[...]
        part = 0.5 * jnp.sum(jnp.mean(err * err, axis=-1, keepdims=True), axis=0, keepdims=True)
        dy = err * (1.0 / D)
        dxhat = dy * g_ref[...]
        dx_ref[...] = rstd * (dxhat - xhat * jnp.mean(dxhat * xhat, axis=-1, keepdims=True))

        @pl.when(pl.program_id(0) == 0)
        def _():
            dg_ref[...] = jnp.zeros_like(dg_ref)
            loss_ref[...] = jnp.zeros_like(loss_ref)

        dg_ref[...] += jnp.sum(dy * xhat, axis=0, keepdims=True)
        loss_ref[...] += jnp.broadcast_to(part, loss_ref.shape)

    row = pl.BlockSpec((tr, D), lambda i: (i, 0))
    vec = pl.BlockSpec((1, D), lambda i: (0, 0))
    loss, dx, dg = pl.pallas_call(
        body, out_shape=(jax.ShapeDtypeStruct((8, LANES), F32), jax.ShapeDtypeStruct((S, D), F32),
                         jax.ShapeDtypeStruct((1, D), F32)),
        grid=(S // tr,), in_specs=[row, vec, row], out_specs=(pl.BlockSpec((8, LANES), lambda i: (0, 0)), row, vec),
        compiler_params=_params(), name=name)(x, g.reshape(1, D), tgt)
    return loss[0, 0], dx, dg.reshape(D)


def _rope_tables(S):
    inv = 10000.0 ** (-jnp.arange(0, DH, 2, dtype=F32) / DH)
    ang = jnp.arange(S, dtype=F32)[:, None] * inv[None, :]
    cos, sin = jnp.cos(ang), jnp.sin(ang)
    return jnp.concatenate([cos, cos], -1), jnp.concatenate([-sin, sin], -1)


def _swap_matrix():
    p = np.zeros((DH, DH), np.float32)
    for j in range(DH // 2):
        p[j + DH // 2, j] = 1.0
        p[j, j + DH // 2] = 1.0
    return jnp.asarray(p, jnp.bfloat16)


def _rope(x, cc, ss, rotated, *, name):
    n, S, _ = x.shape
    assert len(rotated) == n
    starts = [i for i in range(n) if rotated[i] and (i == 0 or not rotated[i - 1])]
    tab = pl.BlockSpec((S, DH), lambda i: (0, 0))
    for k, first in enumerate(starts):
        count = next((i for i in range(first, n) if not rotated[i]), n) - first

        def body(x_ref, c_ref, s_ref, p_ref, o_ref):
            xv = x_ref[0]
            o_ref[0] = xv * c_ref[...] + _dot_exact(xv, p_ref[...]) * s_ref[...]

        blk = pl.BlockSpec((1, S, DH), lambda i, first=first: (first + i, 0, 0))
        x = pl.pallas_call(body, out_shape=jax.ShapeDtypeStruct(x.shape, F32), grid=(count,),
                           in_specs=[blk, tab, tab, pl.BlockSpec((DH, DH), lambda i: (0, 0))], out_specs=blk,
                           input_output_aliases={0: 0}, compiler_params=_params(),
                           name=f"{name}{k}")(x, cc, ss, _swap_matrix())
    return x


def _key_range(kind, i, window, Sk):
    if kind == "cmp":
        return 0, Sk
    hi = (i + 1) * BQ
    if kind == "band":
        return max(0, i * BQ - window), hi
    return 0, hi


def _attn_mask(kind, i, lo, hi, window):
    shape = (BQ, hi - lo)
    qpos = i * BQ + lax.broadcasted_iota(jnp.int32, shape, 0)
    kpos = lo + lax.broadcasted_iota(jnp.int32, shape, 1)
    if kind == "cmp":
        return kpos * 16 + 31 <= qpos
    mask = kpos <= qpos
    if kind == "band":
        mask = mask & (qpos - kpos < window)
    return mask


def _sel_expand(n_slc, n_keys):
    shape = (n_slc, n_keys)
    j = lax.broadcasted_iota(jnp.int32, shape, 0)
    key = lax.broadcasted_iota(jnp.int32, shape, 1)
    return (jnp.right_shift(key, 6) == j).astype(_CDT)


def _eye():
    return lax.broadcasted_iota(jnp.int32, (BQ, BQ), 0) == lax.broadcasted_iota(jnp.int32, (BQ, BQ), 1)


def _to_col(row):
    return jnp.sum(jnp.where(_eye(), row, 0.0), axis=1, keepdims=True)


def _to_row(col):
    return jnp.sum(jnp.where(_eye(), col, 0.0), axis=0, keepdims=True)


def _scores(kind, i, lo, hi, window, qb, kb, crow_ref, sel_ref):
    s = _dot_nt(qb, kb)
    if crow_ref is not None:
        s = s + _to_col(crow_ref[0, :, i * BQ:(i + 1) * BQ]) - crow_ref[0, :, lo:hi]
    mask = _attn_mask(kind, i, lo, hi, window)
    if sel_ref is not None:
        mask = mask & (sel_ref[0, i * BQ:(i + 1) * BQ, lo:hi].astype(F32) > 0.5)
    return jnp.where(mask, s, NEG), mask


def _attn_fwd(q, k, v, *, G, R, heads=(0, 0, 0), kind, window=0, bias=None, sinks=None, selT=None, ovT=None, ctx=None,
              name):
    S = q.shape[1]
    Sk = k.shape[1]
    q0, k0, v0 = heads
    nq = S // BQ
    n_slc = S // 64
    has_bias, has_sink, has_sel, is_cmp = bias is not None, sinks is not None, selT is not None, kind == "cmp"

    def body(*refs):
        it = iter(refs)
        q_ref, k_ref, v_ref = next(it), next(it), next(it)
        crow_ref = next(it) if has_bias else None
        sink_ref = next(it) if has_sink else None
        sel_ref = next(it) if has_sel else None
        ov_ref = next(it) if is_cmp else None
        o_ref, lse_ref = next(it), next(it)
        selo_ref, imp_ref = (next(it), next(it)) if is_cmp else (None, None)
        r = pl.program_id(1)
        for i in range(nq):
            lo, hi = _key_range(kind, i, window, Sk)
            rows = slice(i * BQ, (i + 1) * BQ)
            qb = (q_ref[rows, :] * SCALE).astype(_CDT)
            kb = k_ref[0, lo:hi, :].astype(_CDT)
            vb = v_ref[0, lo:hi, :].astype(_CDT)
            s, mask = _scores(kind, i, lo, hi, window, qb, kb, crow_ref, sel_ref)
            m = jnp.max(s, axis=-1, keepdims=True)
            if has_sink:
                sk = sink_ref[0, 0, 0:1, 0:1]
                m = jnp.maximum(m, sk)
            e = jnp.exp(s - m)
            if is_cmp:
                e = jnp.where(mask, e, 0.0)
            l = jnp.sum(e, axis=-1, keepdims=True)
            if has_sink:
                l = l + jnp.exp(sk - m)
            if is_cmp:
                l = jnp.where(l > 0.0, l, 1.0)
            p = e * (1.0 / l)
            o_ref[0, 0, rows, :] = _dot(p.astype(_CDT), vb)
            lse_ref[0, 0, :, rows] = _to_row(m + jnp.log(l))
            if is_cmp:
                part = _dot_nt(ov_ref[...].astype(_CDT), p.astype(_CDT))

                @pl.when(r == 0)
                def _():
                    imp_ref[:, rows] = part

                @pl.when(r != 0)
                def _():
                    imp_ref[:, rows] += part

        if is_cmp:
            @pl.when(r == R - 1)
            def _():
                shape = (n_slc, S)
                j = lax.broadcasted_iota(jnp.int32, shape, 0)
                tb = jnp.right_shift(lax.broadcasted_iota(jnp.int32, shape, 1), 6)
                forced = (j == 0) | (j == tb) | (j == tb - 1)
                imp = jnp.where(j > tb, NEG, jnp.where(forced, NSA_BONUS, imp_ref[0:n_slc, :]))
                imp_ref[0:n_slc, :] = imp
                cnt = jnp.zeros(shape, F32)
                for jp in range(n_slc):
                    row = imp_ref[jp:jp + 1, :]
                    ahead = (row > imp) | ((row == imp) & (jp < j))
                    cnt = cnt + ahead.astype(F32)
                imp_ref[0:n_slc, :] = (cnt < float(min(NSA_TOPK, n_slc))).astype(F32)
                expand = _sel_expand(n_slc, S)
                for i in range(nq):
                    rows = slice(i * BQ, (i + 1) * BQ)
                    chosen = _dot_tn(imp_ref[0:n_slc, rows].astype(_CDT), expand)
                    selo_ref[0, rows, :] = chosen.astype(jnp.bfloat16)

    qspec = pl.BlockSpec((1, 1, S, DH), lambda g, r: (g, r, 0, 0))
    ins = [q, k, v]
    specs = [pl.BlockSpec((None, S, DH), lambda g, r: (q0 + g * R + r, 0, 0)),
             pl.BlockSpec((1, Sk, DH), lambda g, r: (k0 + g, 0, 0)), pl.BlockSpec((1, Sk, DH), lambda g, r: (v0 + g, 0, 0))]
    if has_bias:
        ins.append(bias)
        specs.append(pl.BlockSpec((1, 1, S), lambda g, r: (g, 0, 0)))
    if has_sink:
        ins.append(sinks)
        specs.append(pl.BlockSpec((1, 1, 8, LANES), lambda g, r: (g, r, 0, 0)))
    if has_sel:
        ins.append(selT)
        specs.append(pl.BlockSpec((1, S, S), lambda g, r: (g, 0, 0)))
    if is_cmp:
        ins.append(ovT)
        specs.append(pl.BlockSpec((LANES, Sk), lambda g, r: (0, 0)))
    outs = [jax.ShapeDtypeStruct((G, R, S, DH), F32), jax.ShapeDtypeStruct((G, R, 1, S), F32)]
    ospecs = [qspec, pl.BlockSpec((1, 1, 1, S), lambda g, r: (g, r, 0, 0))]
    scratch = []
    if is_cmp:
        outs.append(jax.ShapeDtypeStruct((G, S, S), jnp.bfloat16))
        ospecs.append(pl.BlockSpec((1, S, S), lambda g, r: (g, 0, 0)))
        scratch.append(pltpu.VMEM((LANES, S), F32))
    return _pcall(body, out_shape=tuple(outs), grid=(G, R), in_specs=specs, out_specs=tuple(ospecs),
                  scratch_shapes=scratch, name=name, jobs=ctx.take(name) if ctx else (), ins=ins)


def _attn_bwd(q, k, v, o, do, lse, *, heads=(0, 0, 0), kind, window=0, bias=None, sinks=None, selT=None, gate=None,
              ctx=None, name):
    G, R, S, _ = o.shape
    Sk = k.shape[1]
    q0, k0, v0 = heads
    nq = S // BQ
    has_bias, has_sink, has_sel, has_gate = bias is not None, sinks is not None, selT is not None, gate is not None

    def body(*refs):
        it = iter(refs)
        q_ref, k_ref, v_ref, o_ref, do_ref, lse_ref = (next(it) for _ in range(6))
        crow_ref = next(it) if has_bias else None
        sink_ref = next(it) if has_sink else None
        sel_ref = next(it) if has_sel else None
        z_ref = next(it) if has_gate else None
        dq_ref, dk_ref, dv_ref = next(it), next(it), next(it)
        dc_ref = next(it) if has_bias else None
        dsink_ref = next(it) if has_sink else None
        dz_ref = next(it) if has_gate else None
        r = pl.program_id(1)

        @pl.when(r == 0)
        def _():
            dk_ref[...] = jnp.zeros_like(dk_ref)
            dv_ref[...] = jnp.zeros_like(dv_ref)
            if has_bias:
                dc_ref[...] = jnp.zeros_like(dc_ref)

        dsink = jnp.zeros((1, 1), F32)
        for i in range(nq):
            lo, hi = _key_range(kind, i, window, Sk)
            rows = slice(i * BQ, (i + 1) * BQ)
            qb = (q_ref[rows, :] * SCALE).astype(_CDT)
            kb = k_ref[0, lo:hi, :].astype(_CDT)
            vb = v_ref[0, lo:hi, :].astype(_CDT)
            s, mask = _scores(kind, i, lo, hi, window, qb, kb, crow_ref, sel_ref)
            lse_i = _to_col(lse_ref[0, 0, :, rows])
            p = jnp.where(mask, jnp.exp(s - lse_i), 0.0)
            dob = do_ref[0, 0, rows, :]
            if has_gate:
                od = jnp.sum(o_ref[0, 0, rows, :] * dob, axis=-1, keepdims=True)
                sg = jax.nn.sigmoid(_to_col(z_ref[0, 0, :, rows]))
                dob = dob * sg
                dz_ref[0, 0, :, rows] = _to_row(od * sg * (1.0 - sg))
            dob = dob.astype(_CDT)
            dp = _dot_nt(dob, vb)
            delta = jnp.sum(p * dp, axis=-1, keepdims=True)
            ds = p * (dp - delta)
            dsb = ds.astype(_CDT)
            dq_ref[0, 0, rows, :] = _dot(dsb, kb) * SCALE
            dk_ref[0, lo:hi, :] += _dot_tn(dsb, qb)
            dv_ref[0, lo:hi, :] += _dot_tn(p.astype(_CDT), dob)
            if has_bias:
                dc_ref[0, :, rows] += _to_row(jnp.sum(ds, axis=-1, keepdims=True))
                dc_ref[0, :, lo:hi] -= jnp.sum(ds, axis=0, keepdims=True)
            if has_sink:
                sk = sink_ref[0, 0, 0:1, 0:1]
                dsink = dsink - jnp.sum(jnp.exp(sk - lse_i) * delta, axis=0, keepdims=True)
        if has_sink:
            dsink_ref[0, 0] = jnp.broadcast_to(dsink, (8, LANES))

    qspec = pl.BlockSpec((1, 1, S, DH), lambda g, r: (g, r, 0, 0))
    cspec = pl.BlockSpec((1, 1, 1, S), lambda g, r: (g, r, 0, 0))
    kspec = pl.BlockSpec((1, Sk, DH), lambda g, r: (g, 0, 0))
    ins = [q, k, v, o, do, lse]
    specs = [pl.BlockSpec((None, S, DH), lambda g, r: (q0 + g * R + r, 0, 0)),
             pl.BlockSpec((1, Sk, DH), lambda g, r: (k0 + g, 0, 0)), pl.BlockSpec((1, Sk, DH), lambda g, r: (v0 + g, 0, 0)),
             qspec, qspec, cspec]
    if has_bias:
        ins.append(bias)
        specs.append(pl.BlockSpec((1, 1, S), lambda g, r: (g, 0, 0)))
    if has_sink:
        ins.append(sinks)
        specs.append(pl.BlockSpec((1, 1, 8, LANES), lambda g, r: (g, r, 0, 0)))
    if has_sel:
        ins.append(selT)
        specs.append(pl.BlockSpec((1, S, S), lambda g, r: (g, 0, 0)))
    if has_gate:
        ins.append(gate)
        specs.append(cspec)
    names = ["dq", "dk", "dv"]
    outs = [jax.ShapeDtypeStruct((G, R, S, DH), F32), jax.ShapeDtypeStruct((G, Sk, DH), F32),
            jax.ShapeDtypeStruct((G, Sk, DH), F32)]
    ospecs = [qspec, kspec, kspec]
    if has_bias:
        assert R == 1
        names.append("dc")
        outs.append(jax.ShapeDtypeStruct((G, 1, S), F32))
        ospecs.append(pl.BlockSpec((1, 1, S), lambda g, r: (g, 0, 0)))
    if has_sink:
        names.append("dsink")
        outs.append(jax.ShapeDtypeStruct((G, R, 8, LANES), F32))
        ospecs.append(pl.BlockSpec((1, 1, 8, LANES), lambda g, r: (g, r, 0, 0)))
    if has_gate:
        names.append("dz")
        outs.append(jax.ShapeDtypeStruct((G, R, 1, S), F32))
        ospecs.append(cspec)
    res = _pcall(body, out_shape=tuple(outs), grid=(G, R), in_specs=specs, out_specs=tuple(ospecs), name=name,
                 jobs=ctx.take(name) if ctx else (), ins=ins)
    return dict(zip(names, res))


def _combine(o0, o1, o2, z, *, name):
    H, S, _ = o0.shape

    def body(o0_ref, o1_ref, o2_ref, z_ref, o_ref):
        for i in range(S // BQ):
            rows = slice(i * BQ, (i + 1) * BQ)
            acc = jax.nn.sigmoid(_to_col(z_ref[0, 0, :, rows])) * o0_ref[0, rows, :]
            acc = acc + jax.nn.sigmoid(_to_col(z_ref[1, 0, :, rows])) * o1_ref[0, rows, :]
            acc = acc + jax.nn.sigmoid(_to_col(z_ref[2, 0, :, rows])) * o2_ref[0, rows, :]
            o_ref[0, rows, :] = acc

    blk = pl.BlockSpec((1, S, DH), lambda h: (h, 0, 0))
    return pl.pallas_call(body, out_shape=jax.ShapeDtypeStruct((H, S, DH), F32), grid=(H,),
                          in_specs=[blk, blk, blk, pl.BlockSpec((3, 1, 1, S), lambda h: (0, h, 0, 0))], out_specs=blk,
                          compiler_params=_params(), name=name)(o0, o1, o2, z)


_GC = math.sqrt(2.0 / math.pi)


def _gelu(x):
    return 0.5 * x * (1.0 + jnp.tanh(_GC * (x + 0.044715 * x * x * x)))


def _gelu_grad(x):
    t = jnp.tanh(_GC * (x + 0.044715 * x * x * x))
    return 0.5 * (1.0 + t) + 0.5 * x * (1.0 - t * t) * _GC * (1.0 + 3.0 * 0.044715 * x * x)


def _make_xb(k):
    G, S, _ = k.shape
    chunks = k.reshape(G, S // 16, 16 * DH)
    shift = jnp.concatenate([chunks[:, 1:], jnp.zeros((G, 1, 16 * DH), k.dtype)], axis=1)
    return jnp.concatenate([chunks, shift], axis=-1)


def _unmake_xb(dxb, *, name):
    G, n, _ = dxb.shape
    a = dxb[..., :16 * DH]
    b = jnp.concatenate([jnp.zeros((G, 1, 16 * DH), F32), dxb[:, :-1, 16 * DH:]], axis=1)
    return _addn(a, b, name=name).reshape(G, n * 16, DH)


def _compress_fwd(xb, pe, w1, w2, *, name):
    G, n, W = xb.shape
    Hc = w1.shape[1]

    def body(xb_ref, pe_ref, w1_ref, w2_ref, kc_ref, hid_ref):
        xv = (xb_ref[0] + pe_ref[...]).astype(_CDT)
        hid = _dot(xv, w1_ref[...].astype(_CDT))
        hid_ref[0] = hid
        kc_ref[0] = _dot(_gelu(hid).astype(_CDT), w2_ref[...].astype(_CDT))

    return pl.pallas_call(
        body, out_shape=(jax.ShapeDtypeStruct((G, n, DH), F32), jax.ShapeDtypeStruct((G, n, Hc), F32)), grid=(G,),
        in_specs=[pl.BlockSpec((1, n, W), lambda g: (g, 0, 0)), pl.BlockSpec((1, W), lambda g: (0, 0)),
                  pl.BlockSpec((W, Hc), lambda g: (0, 0)), pl.BlockSpec((Hc, DH), lambda g: (0, 0))],
        out_specs=(pl.BlockSpec((1, n, DH), lambda g: (g, 0, 0)), pl.BlockSpec((1, n, Hc), lambda g: (g, 0, 0))),
        compiler_params=_params(), name=name)(xb, pe.reshape(1, W), w1, w2)


def _compress_bwd(xb, pe, w1, w2, hid, dkc, *, name):
    G, n, W = xb.shape
    Hc = w1.shape[1]

    def body(xb_ref, pe_ref, w1_ref, w2_ref, hid_ref, dkc_ref, dxb_ref, dw1_ref, dw2_ref, dpe_ref):
        @pl.when(pl.program_id(0) == 0)
        def _():
            dw1_ref[...] = jnp.zeros_like(dw1_ref)
            dw2_ref[...] = jnp.zeros_like(dw2_ref)
            dpe_ref[...] = jnp.zeros_like(dpe_ref)

        xv = (xb_ref[0] + pe_ref[...]).astype(_CDT)
        hid = hid_ref[0]
        dk = dkc_ref[0].astype(_CDT)
        dact = _dot_nt(dk, w2_ref[...].astype(_CDT))
        dhid = (dact * _gelu_grad(hid)).astype(_CDT)
        dw2_ref[...] += _dot_tn(_gelu(hid).astype(_CDT), dk)
        dxb = _dot_nt(dhid, w1_ref[...].astype(_CDT))
        dxb_ref[0] = dxb
        dw1_ref[...] += _dot_tn(xv, dhid)
        dpe_ref[...] += jnp.sum(dxb, axis=0, keepdims=True)

    return pl.pallas_call(
        body, out_shape=(jax.ShapeDtypeStruct((G, n, W), F32), jax.ShapeDtypeStruct((W, Hc), F32),
                         jax.ShapeDtypeStruct((Hc, DH), F32), jax.ShapeDtypeStruct((1, W), F32)), grid=(G,),
        in_specs=[pl.BlockSpec((1, n, W), lambda g: (g, 0, 0)), pl.BlockSpec((1, W), lambda g: (0, 0)),
                  pl.BlockSpec((W, Hc), lambda g: (0, 0)), pl.BlockSpec((Hc, DH), lambda g: (0, 0)),
                  pl.BlockSpec((1, n, Hc), lambda g: (g, 0, 0)), pl.BlockSpec((1, n, DH), lambda g: (g, 0, 0))],
        out_specs=(pl.BlockSpec((1, n, W), lambda g: (g, 0, 0)), pl.BlockSpec((W, Hc), lambda g: (0, 0)),
                   pl.BlockSpec((Hc, DH), lambda g: (0, 0)), pl.BlockSpec((1, W), lambda g: (0, 0))),
        compiler_params=_params(), name=name)(xb, pe.reshape(1, W), w1, w2, hid, dkc)


def _overlap_T(S):
    n_cmp, n_slc = S // 16 - 1, S // 64
    cs = np.arange(n_cmp) * 16
    ce = cs + 32
    ss = np.arange(n_slc) * 64
    se = ss + 64
    ov = np.clip(np.minimum(ce[:, None], se[None, :]) - np.maximum(cs[:, None], ss[None, :]), 0, None) / 32.0
    out = np.zeros((LANES, S // 16), np.float32)
    out[:n_slc, :n_cmp] = ov.T
    return jnp.asarray(out)


def _tri(n, upper):
    r = lax.broadcasted_iota(jnp.int32, (n, n), 0)
    c = lax.broadcasted_iota(jnp.int32, (n, n), 1)
    return ((c >= r) if upper else (c <= r)).astype(jnp.bfloat16)


def _fox_gate_fwd(zf, b, *, name):
    S, H = zf.shape
    nb = S // BQ

    def body(z_ref, b_ref, c_ref):
        tri = _tri(BQ, False)
        carry = jnp.zeros((1, H), F32)
        for i in range(nb):
            z = z_ref[i * BQ:(i + 1) * BQ, :] + b_ref[...]
            lf = jnp.minimum(z, 0.0) - jnp.log(1.0 + jnp.exp(-jnp.abs(z)))
            c_ref[i * BQ:(i + 1) * BQ, :] = _dot_exact_left(tri, lf) + carry
            carry = carry + jnp.sum(lf, axis=0, keepdims=True)

    return pl.pallas_call(body, out_shape=jax.ShapeDtypeStruct((S, H), F32), compiler_params=_params(),
                          name=name)(zf, b)


def _fox_gate_bwd(zf, b, dc, *, name):
    S, H = zf.shape
    nb = S // BQ

    def body(z_ref, b_ref, dc_ref, dz_ref, db_ref):
        tri = _tri(BQ, True)
        carry = jnp.zeros((1, H), F32)
        db = jnp.zeros((1, H), F32)
        for i in reversed(range(nb)):
            rows = slice(i * BQ, (i + 1) * BQ)
            dcb = dc_ref[rows, :]
            dlf = _dot_exact_left(tri, dcb) + carry
            carry = carry + jnp.sum(dcb, axis=0, keepdims=True)
            z = z_ref[rows, :] + b_ref[...]
            dz = dlf * jax.nn.sigmoid(-z)
            dz_ref[rows, :] = dz
            db = db + jnp.sum(dz, axis=0, keepdims=True)
        db_ref[...] = db

    return pl.pallas_call(body, out_shape=(jax.ShapeDtypeStruct((S, H), F32), jax.ShapeDtypeStruct((1, H), F32)),
                          compiler_params=_params(), name=name)(zf, b, dc)


def _from_heads(a):
    return a.transpose(1, 0, 2).reshape(a.shape[1], -1).astype(_CDT)


def _pad_lanes(a):
    return jnp.pad(a, ((0, 0), (0, LANES - a.shape[1])))


FFN_TM = 512


def _ffn_up(h, w4, *, ctx, name):
    S, D = h.shape
    C = w4.shape[2]

    def body(h_ref, wg_ref, wu_ref, gu_ref, a_ref):
        hv = h_ref[...].astype(_CDT)
        g = _dot(hv, wg_ref[...].astype(_CDT))
        u = _dot(hv, wu_ref[...].astype(_CDT))
        gu_ref[0] = g
        gu_ref[1] = u
        a_ref[...] = (g * jax.nn.sigmoid(g) * u).astype(_CDT)

    return _pcall(
        body, out_shape=(jax.ShapeDtypeStruct((2, S, 2 * C), F32), jax.ShapeDtypeStruct((S, 2 * C), _CDT)),
        grid=(2, S // FFN_TM),
        in_specs=[pl.BlockSpec((FFN_TM, D), lambda j, i: (i, 0)), pl.BlockSpec((None, D, C), lambda j, i: (j, 0, 0)),
                  pl.BlockSpec((None, D, C), lambda j, i: (j + 2, 0, 0))],
        out_specs=(pl.BlockSpec((2, FFN_TM, C), lambda j, i: (0, i, j)), pl.BlockSpec((FFN_TM, C), lambda j, i: (i, j))),
        name=name, jobs=ctx.take(name) if ctx else (), ins=[h, w4, w4])


def _ffn_dact(dxo, wd, gu, *, ctx, name):
    S, D = dxo.shape
    C = gu.shape[2] // 2

    def body(dx_ref, wd_ref, gu_ref, d_ref):
        da = _dot_nt(dx_ref[...].astype(_CDT), wd_ref[...].astype(_CDT)) * 0.5
        g, u = gu_ref[0], gu_ref[1]
        sg = jax.nn.sigmoid(g)
        silu = g * sg
        d_ref[0] = (da * u * (sg + silu * (1.0 - sg))).astype(_CDT)
        d_ref[1] = (da * silu).astype(_CDT)

    blk = pl.BlockSpec((2, FFN_TM, C), lambda i, j: (0, i, j))
    return _pcall(body, out_shape=jax.ShapeDtypeStruct(gu.shape, _CDT), grid=(S // FFN_TM, 2),
                  in_specs=[pl.BlockSpec((FFN_TM, D), lambda i, j: (i, 0)), pl.BlockSpec((C, D), lambda i, j: (j, 0)), blk],
                  out_specs=blk, name=name, jobs=ctx.take(name) if ctx else (), ins=[dxo, wd, gu])


def _ffn_dwgu(h, dgu, *, ctx, name):
    S, D = h.shape
    C = dgu.shape[2] // 2
    tm = 512

    def body(h_ref, d_ref, o_ref):
        o_ref[...] = _dot_tn(h_ref[...].astype(_CDT), d_ref[...].astype(_CDT)).astype(_WIRE)

    return _pcall(body, out_shape=jax.ShapeDtypeStruct((4, D, C), _WIRE), grid=(4, D // tm),
                  in_specs=[pl.BlockSpec((S, tm), lambda j, i: (0, i)),
                            pl.BlockSpec((None, S, C), lambda j, i: (j // 2, 0, j % 2))],
                  out_specs=pl.BlockSpec((None, tm, C), lambda j, i: (j, i, 0)), name=name,
                  jobs=ctx.take(name) if ctx else (), ins=[h, dgu])


def _ffn_dh(dgu, w4, *, ctx, name):
    _, S, F2 = dgu.shape
    C, D = F2 // 2, w4.shape[1]

    def body(d_ref, w_ref, o_ref):
        acc = None
        for k in range(4):
            prod = _dot_nt(d_ref[k // 2, :, (k % 2) * C:(k % 2 + 1) * C].astype(_CDT), w_ref[k].astype(_CDT))
            acc = prod if acc is None else acc + prod
        o_ref[...] = acc

    return _pcall(body, out_shape=jax.ShapeDtypeStruct((S, D), F32), grid=(S // FFN_TM,),
                  in_specs=[pl.BlockSpec((2, FFN_TM, 2 * C), lambda i: (0, i, 0)),
                            pl.BlockSpec((4, D, C), lambda i: (0, 0, 0))],
                  out_specs=pl.BlockSpec((FFN_TM, D), lambda i: (i, 0)), name=name,
                  jobs=ctx.take(name) if ctx else (), ins=[dgu, w4])


def _ffn_fwd(x, g, P, l, tag):
    h = _rms_fwd(x, g, name=tag + "_rms")
    gu, a = _ffn_up(h, P.weight("gu", l), ctx=P.ctx, name=tag + "_gu")
    xo = _mm(a, P.weight("down", l), tm=512, tn=1024, res=x, alpha=0.5, ctx=P.ctx, name=tag + "_down")
    return xo, (x, h, gu, a)


def _ffn_bwd(dxo, saved, g, P, l, tag):
    x, h, gu, a = saved
    dgu = _ffn_dact(dxo, P.weight("down", l), gu, ctx=P.ctx, name=tag + "_dact")
    P.grad("gu", l, _ffn_dwgu(h, dgu, ctx=P.ctx, name=tag + "_dwgu"))
    dwd = _mm(a, dxo, ta=True, tm=1408, tn=1024, alpha=0.5, out_dtype=_WIRE, ctx=P.ctx, name=tag + "_dwd")
    P.grad("down", l, dwd.reshape(4, -1, D_MODEL))
    dh = _ffn_dh(dgu, P.weight("gu", l), ctx=P.ctx, name=tag + "_dh")
    dx, dg = _rms_bwd(dh, x, g, dxo, name=tag + "_drms")
    return dx, dg


def _nsa_fwd(x, g, w, cc, ss, tag):
    S = x.shape[0]
    h = _rms_fwd(x, g, name=tag + "_rms")
    hm = _mm(h, w["w_in"], tm=512, tn=896, o_lead=("heads",), ctx=w["ctx"], name=tag + "_in")
    roped = _rope(hm, cc, ss, NSA_ROTATED, name=tag + "_rope")
    z = roped[40, :, :3 * N_HEADS].reshape(S, 3, N_HEADS).transpose(1, 2, 0)[:, :, None, :]
    xbk, xbv = _make_xb(roped[16:20]), _make_xb(roped[20:24])
    kc, hidk = _compress_fwd(xbk, w["ck_pe"], w["ck_w1"], w["ck_w2"], name=tag + "_ck")
    vc, hidv = _compress_fwd(xbv, w["cv_pe"], w["cv_w1"], w["cv_w2"], name=tag + "_cv")
    ctx, GR = w["ctx"], dict(G=NSA_G, R=NSA_R)
    o0, lse0, selT = _attn_fwd(roped, kc, vc, kind="cmp", ovT=_overlap_T(S), ctx=ctx, name=tag + "_cmp", **GR)
    o1, lse1 = _attn_fwd(roped, roped, roped, heads=(0, 24, 28), kind="sel", selT=selT, ctx=ctx, name=tag + "_slc", **GR)
    o2, lse2 = _attn_fwd(roped, roped, roped, heads=(0, 32, 36), kind="band", window=NSA_WINDOW, ctx=ctx,
                         name=tag + "_win", **GR)
    o = _combine(o0.reshape(N_HEADS, S, DH), o1.reshape(N_HEADS, S, DH), o2.reshape(N_HEADS, S, DH), z,
                 name=tag + "_mix")
    of = _from_heads(o)
    xo = _mm(of, w["out"], tm=512, tn=1024, res=x, ctx=w["ctx"], name=tag + "_out")
    saved = (x, h, roped, z, xbk, xbv, hidk, hidv, kc, vc, (o0, o1, o2), (lse0, lse1, lse2), selT, of)
    return xo, saved


def _out_bwd(dxo, of, w, tag):
    dof = _mm(dxo, w["out"], tb=True, tm=512, tn=1024, o_lead=("heads",), ctx=w["ctx"], name=tag + "_dof")
    dw = _mm(of, dxo, ta=True, tm=512, tn=1024, out_dtype=_WIRE, ctx=w["ctx"], name=tag + "_dwout")
    w["P"].grad("out", w["out_l"], dw.reshape(4, -1, D_MODEL))
    return dof


def _split_cols(dw, n_in):
    cs = n_in // 4
    return dw[:, :n_in].reshape(D_MODEL, 4, cs).transpose(1, 0, 2).astype(_WIRE)


def _nsa_bwd(dxo, saved, g, w, cc, ss, tag):
    x, h, roped, z, xbk, xbv, hidk, hidv, kc, vc, os_, lses, selT, of = saved
    S = x.shape[0]
    do = _out_bwd(dxo, of, w, tag).reshape(NSA_G, NSA_R, S, DH)
    zg = z.reshape(3, NSA_G, NSA_R, 1, S)
    ctx = w["ctx"]
    b0 = _attn_bwd(roped, kc, vc, os_[0], do, lses[0], kind="cmp", gate=zg[0], ctx=ctx, name=tag + "_dcmp")
    b1 = _attn_bwd(roped, roped, roped, os_[1], do, lses[1], heads=(0, 24, 28), kind="sel", selT=selT, gate=zg[1],
                   ctx=ctx, name=tag + "_dslc")
    b2 = _attn_bwd(roped, roped, roped, os_[2], do, lses[2], heads=(0, 32, 36), kind="band", window=NSA_WINDOW,
                   gate=zg[2], ctx=ctx, name=tag + "_dwin")
    dxbk, dck_w1, dck_w2, dck_pe = _compress_bwd(xbk, w["ck_pe"], w["ck_w1"], w["ck_w2"], hidk, b0["dk"],
                                                 name=tag + "_dck")
    dxbv, dcv_w1, dcv_w2, dcv_pe = _compress_bwd(xbv, w["cv_pe"], w["cv_w1"], w["cv_w2"], hidv, b0["dv"],
                                                 name=tag + "_dcv")
    dk0 = _unmake_xb(dxbk, name=tag + "_dk0")
    dv0 = _unmake_xb(dxbv, name=tag + "_dv0")
    dq = _addn(b0["dq"], b1["dq"], b2["dq"], name=tag + "_dqsum").reshape(N_HEADS, S, DH)
    dz = jnp.stack([b0["dz"], b1["dz"], b2["dz"]], axis=0).reshape(3 * N_HEADS, S).T
    dgates = jnp.pad(dz, ((0, 0), (0, 2 * DH - 3 * N_HEADS))).reshape(S, 2, DH).transpose(1, 0, 2)
    dhm = jnp.concatenate([dq, dk0, dv0, b1["dk"], b1["dv"], b2["dk"], b2["dv"], dgates], axis=0)
    dproj = _from_heads(_rope(dhm, cc, -ss, NSA_ROTATED, name=tag + "_drope"))
    dh = _mm(dproj, w["w_in"], tb=True, tm=512, tn=512, ctx=w["ctx"], name=tag + "_dh")
    dw_in = _mm(h, dproj, ta=True, tm=512, tn=896, ctx=w["ctx"], name=tag + "_dwin_w")
    dx, dg = _rms_bwd(dh, x, g, dxo, name=tag + "_drms")
    P, j = w["P"], w["j"]
    P.grad("nsa_in", j, _split_cols(dw_in, NSA_IN))
    P.grad("cw1", j, dck_w1.astype(_WIRE).reshape(4, -1, dck_w1.shape[1]))
    P.grad("cw1", 2 + j, dcv_w1.astype(_WIRE).reshape(4, -1, dcv_w1.shape[1]))
    grads = dict(ck_pe=dck_pe.reshape(32, DH), ck_w2=dck_w2, cv_pe=dcv_pe.reshape(32, DH), cv_w2=dcv_w2)
    return dx, dg, grads


def _swa_fwd(x, g, w, cc, ss, tag):
    S = x.shape[0]
    h = _rms_fwd(x, g, name=tag + "_rms")
    hm = _mm(h, w["w_in"], tm=512, tn=640, o_lead=("heads",), ctx=w["ctx"], name=tag + "_in")
    roped = _rope(hm, cc, ss, SWA_ROTATED, name=tag + "_rope")
    sinks = jnp.broadcast_to(w["sinks"].reshape(SWA_G, SWA_R, 1, 1), (SWA_G, SWA_R, 8, LANES))
    o, lse = _attn_fwd(roped, roped, roped, G=SWA_G, R=SWA_R, heads=SWA_HEADS, kind="band", window=SWA_WINDOW,
                       sinks=sinks, ctx=w["ctx"], name=tag + "_attn")
    of = _from_heads(o.reshape(N_HEADS, S, DH))
    xo = _mm(of, w["out"], tm=512, tn=1024, res=x, ctx=w["ctx"], name=tag + "_out")
    return xo, (x, h, roped, sinks, o, lse, of)


def _swa_bwd(dxo, saved, g, w, cc, ss, tag):
    x, h, roped, sinks, o, lse, of = saved
    S = x.shape[0]
    do = _out_bwd(dxo, of, w, tag).reshape(SWA_G, SWA_R, S, DH)
    b = _attn_bwd(roped, roped, roped, o, do, lse, heads=SWA_HEADS, kind="band", window=SWA_WINDOW, sinks=sinks,
                  ctx=w["ctx"], name=tag + "_dattn")
    dhm = jnp.concatenate([b["dq"].reshape(N_HEADS, S, DH), b["dk"], b["dv"]], axis=0)
    dproj = _from_heads(_rope(dhm, cc, -ss, SWA_ROTATED, name=tag + "_drope"))
    dh = _mm(dproj, w["w_in"], tb=True, tm=512, tn=512, ctx=w["ctx"], name=tag + "_dh")
    dw_in = _mm(h, dproj, ta=True, tm=512, tn=640, ctx=w["ctx"], name=tag + "_dwin_w")
    dx, dg = _rms_bwd(dh, x, g, dxo, name=tag + "_drms")
    w["P"].grad("swa_in", w["j"], _split_cols(dw_in, SWA_IN))
    return dx, dg, dict(sinks=b["dsink"][:, :, 0, 0].reshape(N_HEADS))


def _fox_fwd(x, g, w, tag):
    S = x.shape[0]
    h = _rms_fwd(x, g, name=tag + "_rms")
    hm = _mm(h, w["w_in"], tm=512, tn=640, o_lead=("heads",), ctx=w["ctx"], name=tag + "_in")
    zf = jnp.pad(hm[48], ((0, 0), (0, LANES - DH)))
    bf = _pad_lanes(w["b_f"].reshape(1, N_HEADS))
    c = _fox_gate_fwd(zf, bf, name=tag + "_gate")[:, :N_HEADS]
    bias = c.T[:, None, :]
    o, lse = _attn_fwd(hm, hm, hm, G=N_HEADS, R=1, heads=FOX_HEADS, kind="causal", bias=bias, ctx=w["ctx"],
                       name=tag + "_attn")
    of = _from_heads(o.reshape(N_HEADS, S, DH))
    xo = _mm(of, w["out"], tm=512, tn=1024, res=x, ctx=w["ctx"], name=tag + "_out")
    return xo, (x, h, hm, zf, bf, bias, o, lse, of)


def _fox_bwd(dxo, saved, g, w, tag):
    x, h, hm, zf, bf, bias, o, lse, of = saved
    S = x.shape[0]
    do = _out_bwd(dxo, of, w, tag).reshape(N_HEADS, 1, S, DH)
    b = _attn_bwd(hm, hm, hm, o, do, lse, heads=FOX_HEADS, kind="causal", bias=bias, ctx=w["ctx"], name=tag + "_dattn")
    dzf, db = _fox_gate_bwd(zf, bf, _pad_lanes(b["dc"].reshape(N_HEADS, S).T), name=tag + "_dgate")
    dgates = dzf.reshape(S, 2, DH).transpose(1, 0, 2)
    dhm = jnp.concatenate([b["dq"].reshape(N_HEADS, S, DH), b["dk"], b["dv"], dgates], axis=0)
    dproj = _from_heads(dhm)
    dh = _mm(dproj, w["w_in"], tb=True, tm=512, tn=512, ctx=w["ctx"], name=tag + "_dh")
    dw_in = _mm(h, dproj, ta=True, tm=512, tn=640, ctx=w["ctx"], name=tag + "_dwin_w")
    dx, dg = _rms_bwd(dh, x, g, dxo, name=tag + "_drms")
    w["P"].grad("fox_in", w["j"], _split_cols(dw_in, FOX_IN))
    return dx, dg, dict(b_f=db[0, :N_HEADS])


GROUPS = {
    "gu": (("ffn1_w_gu", "ffn2_w_gu"), 2),
    "down": (("ffn1_w_down", "ffn2_w_down"), 1),
    "out": (("nsa_w_out", "swa_w_out", "fox_w_out"), 1),
    "cw1": (("nsa_ck_w1", "nsa_cv_w1"), 1),
    "nsa_in": (("nsa_w_in",), 2),
    "swa_in": (("swa_w_in",), 2),
    "fox_in": (("fox_w_in",), 2),
}
OUT_SLAB = {0: 0, 3: 1, 1: 2, 2: 3}


def _pieces_in_order():
    chunks = []
    for i in range(DEPTH):
        kind, j = i % 3, i // 3
        chunks.append([("gu", i), ("down", i)])
        if kind == 0:
            chunks.append([("nsa_in", j), ("cw1", j), ("cw1", 2 + j), ("out", OUT_SLAB[i])])
        else:
            chunks.append([("swa_in" if kind == 1 else "fox_in", j), ("out", OUT_SLAB[i])])
        chunks.append([("gu", DEPTH + i), ("down", DEPTH + i)])
    return chunks


def _consumer_layout(group, F):
    _, rows, C = F.shape
    if group == "gu":
        return F
    if GROUPS[group][1] == 1:
        return F.reshape(4 * rows, C)
    w = F.transpose(1, 0, 2).reshape(rows, 4 * C)
    pad = {"nsa_in": NSA_IN_PAD, "swa_in": SWA_IN, "fox_in": FOX_IN_PAD}[group] - 4 * C
    return jnp.pad(w, ((0, 0), (0, pad)))


class _Given:
    def __init__(self, pieces, small):
        self.pieces, self.small, self.ctx, self.grads = pieces, small, None, {}

    def weight(self, group, l):
        return _consumer_layout(group, self.pieces[group, l])

    def grad(self, group, l, G):
        self.grads[group, l] = G


class _MixerWeights(dict):
    def __init__(self, P, pieces, **given):
        super().__init__(P=P, ctx=P.ctx, **given)
        self.pieces = pieces

    def __missing__(self, key):
        self[key] = self["P"].weight(*self.pieces[key])
        return self[key]


def _mixer_weights(P, i):
    kind, j = i % 3, i // 3
    out = {"out": ("out", OUT_SLAB[i])}
    if kind == 0:
        small = {k: P.small["nsa_" + k][j] for k in ("ck_pe", "ck_w2", "cv_pe", "cv_w2")}
        return _MixerWeights(P, dict(out, w_in=("nsa_in", j), ck_w1=("cw1", j), cv_w1=("cw1", 2 + j)), j=j,
                             out_l=OUT_SLAB[i], **small)
    if kind == 1:
        return _MixerWeights(P, dict(out, w_in=("swa_in", j)), j=j, out_l=OUT_SLAB[i], sinks=P.small["swa_sinks"][j])
    return _MixerWeights(P, dict(out, w_in=("fox_in", j)), j=j, out_l=OUT_SLAB[i], b_f=P.small["fox_b_f"][j])


def _local_step(x, tgt, P):
    S = x.shape[0]
    cc, ss = _rope_tables(S)
    sm = P.small
    saved = []
    for i in range(DEPTH):
        kind = i % 3
        x, s1 = _ffn_fwd(x, sm["ffn1_norm"][i], P, i, f"l{i}f1")
        mw = _mixer_weights(P, i)
        if kind == 0:
            x, s2 = _nsa_fwd(x, sm["mix_norm"][i], mw, cc, ss, f"l{i}nsa")
        elif kind == 1:
            x, s2 = _swa_fwd(x, sm["mix_norm"][i], mw, cc, ss, f"l{i}swa")
        else:
            x, s2 = _fox_fwd(x, sm["mix_norm"][i], mw, f"l{i}fox")
        x, s3 = _ffn_fwd(x, sm["ffn2_norm"][i], P, DEPTH + i, f"l{i}f2")
        saved.append((s1, mw, s2, s3))
    loss, dx, d_final = _loss_head(x, sm["final_norm"], tgt, name="loss_head")

    norms = {k: [None] * DEPTH for k in ("ffn1_norm", "mix_norm", "ffn2_norm")}
    mix = {}
    for i in reversed(range(DEPTH)):
        kind, j = i % 3, i // 3
        s1, mw, s2, s3 = saved[i]
        dx, norms["ffn2_norm"][i] = _ffn_bwd(dx, s3, sm["ffn2_norm"][i], P, DEPTH + i, f"l{i}f2")
        if kind == 0:
            dx, dg, gm = _nsa_bwd(dx, s2, sm["mix_norm"][i], mw, cc, ss, f"l{i}nsa")
            pre = "nsa_"
        elif kind == 1:
            dx, dg, gm = _swa_bwd(dx, s2, sm["mix_norm"][i], mw, cc, ss, f"l{i}swa")
            pre = "swa_"
        else:
            dx, dg, gm = _fox_bwd(dx, s2, sm["mix_norm"][i], mw, f"l{i}fox")
            pre = "fox_"
        norms["mix_norm"][i] = dg
        for k, val in gm.items():
            mix.setdefault(pre + k, {})[j] = val
        dx, norms["ffn1_norm"][i] = _ffn_bwd(dx, s1, sm["ffn1_norm"][i], P, i, f"l{i}f1")
    small = {k: jnp.stack(v, axis=0) for k, v in norms.items()}
    small.update({k: jnp.stack([d[j] for j in sorted(d)], axis=0) for k, d in mix.items()})
    small["final_norm"] = d_final
    return loss, dx, small


def _sum_slots(own, recv, me, into, row0, *, name):
    _, R, C = recv.shape
    tr = _row_tile(math.gcd(R, row0), C)
    blk0 = row0 // tr

    def body(_, g_ref, r_ref, __, o_ref):
        acc = g_ref[...].astype(F32) + r_ref[0].astype(F32)
        acc = acc + r_ref[1].astype(F32)
        o_ref[...] = acc + r_ref[2].astype(F32)

    spec = pltpu.PrefetchScalarGridSpec(
        num_scalar_prefetch=1, grid=(R // tr,),
        in_specs=[pl.BlockSpec((None, tr, C), lambda i, me_ref: (me_ref[0], i, 0)),
                  pl.BlockSpec((3, tr, C), lambda i, me_ref: (0, i, 0)), _ANY],
        out_specs=pl.BlockSpec((tr, C), lambda i, me_ref: (blk0 + i, 0)))
    return pl.pallas_call(body, grid_spec=spec, out_shape=jax.ShapeDtypeStruct(into.shape, F32),
                          input_output_aliases={3: 0}, compiler_params=_params(), name=name)(me, own, recv, into)


def _swap_sibling(parts):
    n = len(parts)

    def body(*refs):
        srcs, outs = refs[:n], refs[n:2 * n]
        send_sems, recv_sems = refs[2 * n:]
        x, y, c = lax.axis_index("x"), lax.axis_index("y"), lax.axis_index("c")
        cps = [pltpu.make_async_remote_copy(src_ref=srcs[g], dst_ref=outs[g], send_sem=send_sems.at[g],
                                            recv_sem=recv_sems.at[g], device_id=(x, y, 1 - c), device_id_type=MESH)
               for g in range(n)]
        for cp in cps:
            cp.start()
        for cp in cps:
            cp.wait()

    return pl.pallas_call(
        body, out_shape=tuple(jax.ShapeDtypeStruct(p.shape, p.dtype) for p in parts),
        in_specs=[_ANY] * n, out_specs=(_ANY,) * n,
        scratch_shapes=[pltpu.SemaphoreType.DMA((n,)), pltpu.SemaphoreType.DMA((n,))],
        compiler_params=pltpu.CompilerParams(has_side_effects=True), name="swap_core_grads")(*parts)


def _flip(coord, bit):
    return 1 - coord if bit else coord


def _allreduce_small(v):
    n, C = v.shape

    def body(v_ref, o_ref, buf, send_sems, recv_sems):
        x, y, c = lax.axis_index("x"), lax.axis_index("y"), lax.axis_index("c")
        me = 4 * x + 2 * y + c
        buf[me] = v_ref[...]
        peers = [(_flip(x, (j >> 2) & 1), _flip(y, (j >> 1) & 1), _flip(c, j & 1)) for j in range(1, 8)]
        sends = [pltpu.make_async_remote_copy(src_ref=v_ref, dst_ref=buf.at[me], send_sem=send_sems.at[j],
                                              recv_sem=recv_sems.at[j], device_id=peer, device_id_type=MESH)
                 for j, peer in enumerate(peers)]
        for cp in sends:
            cp.start()
        for j, (px, py, pc) in enumerate(peers):
            pltpu.make_async_remote_copy(src_ref=v_ref, dst_ref=buf.at[4 * px + 2 * py + pc], send_sem=send_sems.at[j],
                                         recv_sem=recv_sems.at[j], device_id=(px, py, pc),
                                         device_id_type=MESH).wait_recv()
        for cp in sends:
            cp.wait_send()
        acc = buf[0]
        for d in range(1, 8):
            acc = acc + buf[d]
        o_ref[...] = acc

    return pl.pallas_call(
        body, out_shape=jax.ShapeDtypeStruct((n, C), F32),
        in_specs=[pl.BlockSpec(memory_space=pltpu.VMEM)], out_specs=pl.BlockSpec(memory_space=pltpu.VMEM),
        scratch_shapes=[pltpu.VMEM((8, n, C), F32), pltpu.SemaphoreType.DMA((7,)), pltpu.SemaphoreType.DMA((7,))],
        compiler_params=pltpu.CompilerParams(has_side_effects=True), name="allreduce_small")(v)


def _adamw(w, m, v, gs, *, row0=0, name):
    shape = w.shape
    w3, m3, v3 = (a.reshape((-1,) + a.shape[-2:]) for a in (w, m, v))
    g2 = [_rows2d(g) for g in gs]
    L, rows, C = w3.shape
    tr = _row_tile(math.gcd(rows, row0), C, budget=1024 * 1024)
    ng = len(g2)
    blk0, nb = row0 // tr, rows // tr

    def body(*refs):
        w_ref, m_ref, v_ref = refs[:3]
        g = refs[3][...]
        for r in refs[4:3 + ng]:
            g = g + r[...]
        g_ref, d_ref, nm_ref, nv_ref = refs[3 + ng:]
        mn = B1 * m_ref[...] + (1.0 - B1) * g
        vn = B2 * v_ref[...] + (1.0 - B2) * (g * g)
        m_hat = mn / (1.0 - B1 ** STEP)
        v_hat = vn / (1.0 - B2 ** STEP)
        g_ref[...] = g
        d_ref[...] = -LR * (m_hat / (jnp.sqrt(v_hat) + EPS) + WD * w_ref[...])
        nm_ref[...] = mn
        nv_ref[...] = vn

    spec = pl.BlockSpec((None, tr, C), lambda l, i: (l, i, 0))
    gspec = pl.BlockSpec((tr, C), lambda l, i: (blk0 + l * nb + i, 0))
    outs = pl.pallas_call(body, out_shape=tuple(jax.ShapeDtypeStruct((L, rows, C), F32) for _ in range(4)),
                          grid=(L, nb), in_specs=[spec] * 3 + [gspec] * ng, out_specs=(spec,) * 4,
                          compiler_params=_params(), name=name)(w3, m3, v3, *g2)
    return tuple(o.reshape(shape) for o in outs)


def _small_layout(shapes):
    offs, off = {}, 0
    for k in REPLICATED:
        n = int(np.prod(shapes[k]))
        offs[k] = (off, n)
        off += -(-n // LANES) * LANES
    return offs, off


def _pack_small(d, shapes):
    offs, total = _small_layout(shapes)
    parts = []
    for k in REPLICATED:
        n = offs[k][1]
        parts.append(jnp.pad(d[k].reshape(-1).astype(F32), (0, -(-n // LANES) * LANES - n)))
    rows = -(-(total // LANES) // 8) * 8
    return jnp.pad(jnp.concatenate(parts), (0, rows * LANES - total)).reshape(rows, LANES)


def _unpack_small(a, shapes):
    offs, _ = _small_layout(shapes)
    flat = a.reshape(-1)
    return {k: flat[offs[k][0]:offs[k][0] + offs[k][1]].reshape(shapes[k]) for k in REPLICATED}


def _group_shards(w):
    shards = []
    for members, _ in GROUPS.values():
        s = jnp.concatenate([w[k].astype(_WIRE) for k in members], axis=0)
        shards.append(s.reshape(s.shape[0], 2, s.shape[1] // 2, s.shape[2]))
    return shards


class _Exchanged:
    def __init__(self, w):
        self.small = {k: w[k] for k in REPLICATED}
        self.ctx = _Sched()
        shards = dict(zip(GROUPS, _group_shards(w)))
        self.gather, self.cache, self.scatter = {}, {}, {}
        chunks = _pieces_in_order()
        self.me = jnp.reshape(2 * lax.axis_index("x") + lax.axis_index("y"), (1,)).astype(jnp.int32)
        first = {(g, l): _gather_task1(shards[g], l, _place_own(shards[g], l, self.me, name=f"own_{g}{l}"))
                 for chunk in chunks for g, l in chunk}
        self.gather = {p: _gather_task2(task) for p, task in first.items()}
        pieces = [p for chunk in chunks for p in chunk]
        for n, p in enumerate(pieces):
            self.ctx.push(first[p])
            if n:
                self.ctx.push(self.gather[pieces[n - 1]])
        self.ctx.push(self.gather[pieces[-1]])

    def weight(self, group, l):
        if (group, l) not in self.cache:
            task = self.gather[group, l]
            self.ctx.finish(task)
            F = task.result()
            self.cache[group, l] = _consumer_layout(group, F.reshape(4, -1, F.shape[-1]))
        return self.cache[group, l]

    def grad(self, group, l, G):
        self.scatter[group, l] = (G, self.ctx.push(_scatter_task(G)))

    def partial_sums(self, w):
        self.ctx.finish()
        parts = []
        for group, (members, _) in GROUPS.items():
            rows, C = w[members[0]].shape[1:]
            n = sum(w[k].shape[0] for k in members)
            part = jnp.zeros((n * rows, C), F32)
            for l in range(n):
                G, task = self.scatter[group, l]
                part = _sum_slots(G, task.result(), self.me, part, l * rows, name=f"sum_{group}{l}")
            parts.append(part)
        return parts


def _reduce_and_update(w, m, v, parts, small):
    others = _swap_sibling(parts)
    small_shapes = {k: w[k].shape for k in REPLICATED}
    g_small = _unpack_small(_allreduce_small(_pack_small(small, small_shapes)), small_shapes)

    out_g, out_d, out_m, out_v = {}, {}, {}, {}
    for (members, _), part, other in zip(GROUPS.values(), parts, others):
        row0 = 0
        for k in members:
            out_g[k], out_d[k], out_m[k], out_v[k] = _adamw(w[k], m[k], v[k], [part, other], row0=row0,
                                                            name="adamw_" + k)
            row0 += w[k].shape[0] * w[k].shape[1]
    sm = _adamw(_pack_small(w, small_shapes), _pack_small(m, small_shapes), _pack_small(v, small_shapes),
                [_pack_small(g_small, small_shapes)], name="adamw_small")
    for d, packed in zip((out_g, out_d, out_m, out_v), sm):
        d.update(_unpack_small(packed, small_shapes))
    return out_g, out_d, out_m, out_v


def kernel(x, ffn1_norm, ffn1_w_gu, ffn1_w_down, mix_norm, ffn2_norm, ffn2_w_gu, ffn2_w_down, nsa_w_in, nsa_ck_pe, nsa_ck_w1, nsa_ck_w2, nsa_cv_pe, nsa_cv_w1, nsa_cv_w2, nsa_w_out, swa_w_in, swa_sinks, swa_w_out, fox_w_in, fox_b_f, fox_w_out, final_norm, loss_target, m_ffn1_norm, m_ffn1_w_gu, m_ffn1_w_down, m_mix_norm, m_ffn2_norm, m_ffn2_w_gu, m_ffn2_w_down, m_nsa_w_in, m_nsa_ck_pe, m_nsa_ck_w1, m_nsa_ck_w2, m_nsa_cv_pe, m_nsa_cv_w1, m_nsa_cv_w2, m_nsa_w_out, m_swa_w_in, m_swa_sinks, m_swa_w_out, m_fox_w_in, m_fox_b_f, m_fox_w_out, m_final_norm, v_ffn1_norm, v_ffn1_w_gu, v_ffn1_w_down, v_mix_norm, v_ffn2_norm, v_ffn2_w_gu, v_ffn2_w_down, v_nsa_w_in, v_nsa_ck_pe, v_nsa_ck_w1, v_nsa_ck_w2, v_nsa_cv_pe, v_nsa_cv_w1, v_nsa_cv_w2, v_nsa_w_out, v_swa_w_in, v_swa_sinks, v_swa_w_out, v_fox_w_in, v_fox_b_f, v_fox_w_out, v_final_norm):
    args = dict(locals())
    w = {k: args[k] for k in WEIGHTS}
    m = {k: args["m_" + k] for k in WEIGHTS}
    v = {k: args["v_" + k] for k in WEIGHTS}
    P = _Exchanged(w)
    loss_part, dx, small = _local_step(x[0], loss_target[0], P)
    loss = lax.psum(loss_part, ("x", "y", "c"))
    out_g, out_d, out_m, out_v = _reduce_and_update(w, m, v, P.partial_sums(w), small)
    return (loss, dx[None], *[out_g[k] for k in WEIGHTS], *[out_d[k] for k in WEIGHTS],
            *[out_m[k] for k in WEIGHTS], *[out_v[k] for k in WEIGHTS])
```

```python
import math

import numpy as np
import jax
import jax.numpy as jnp
from jax import lax
from jax.experimental import pallas as pl
from jax.experimental.pallas import tpu as pltpu

F32 = jnp.float32
_CDT = jnp.bfloat16
_WIRE = jnp.bfloat16
_VMEM_LIMIT = 56 * 1024 * 1024

D_MODEL = 1024
DEPTH = 4
DH = 64
N_HEADS = 16
RMS_EPS = 1e-6
NEG = -1e30
SCALE = DH ** -0.5
BQ = 256
LANES = 128
NSA_G, NSA_R = 4, 4
NSA_WINDOW = 512
NSA_TOPK = 16
NSA_BONUS = 1e4
SWA_G, SWA_R = 2, 8
SWA_WINDOW = 128
NSA_ROTATED = tuple(hd < 16 or (hd < 40 and (hd - 16) % 8 < 4) for hd in range(42))
NSA_IN, NSA_IN_PAD = 2608, 2688
SWA_IN = 1280
SWA_ROTATED = (True,) * 18 + (False,) * 2
SWA_HEADS = (0, 16, 18)
FOX_IN, FOX_IN_PAD = 3088, 3200
FOX_HEADS = (0, 16, 32)
LR, B1, B2, EPS, WD, STEP = 0.001, 0.9, 0.999, 1e-08, 0.01, 10
MESH = pl.DeviceIdType.MESH

REPLICATED = ["ffn1_norm", "mix_norm", "ffn2_norm", "nsa_ck_pe", "nsa_ck_w2", "nsa_cv_pe", "nsa_cv_w2",
              "swa_sinks", "fox_b_f", "final_norm"]
WEIGHTS = ['ffn1_norm', 'ffn1_w_gu', 'ffn1_w_down', 'mix_norm', 'ffn2_norm', 'ffn2_w_gu', 'ffn2_w_down',
           'nsa_w_in', 'nsa_ck_pe', 'nsa_ck_w1', 'nsa_ck_w2', 'nsa_cv_pe', 'nsa_cv_w1', 'nsa_cv_w2', 'nsa_w_out',
           'swa_w_in', 'swa_sinks', 'swa_w_out', 'fox_w_in', 'fox_b_f', 'fox_w_out', 'final_norm']


def _params(**kw):
    return pltpu.CompilerParams(vmem_limit_bytes=_VMEM_LIMIT, **kw)


def _dot(a, b):
    return lax.dot_general(a, b, (((1,), (0,)), ((), ())), preferred_element_type=F32)


def _dot_nt(a, b):
    return lax.dot_general(a, b, (((1,), (1,)), ((), ())), preferred_element_type=F32)


def _dot_tn(a, b):
    return lax.dot_general(a, b, (((0,), (0,)), ((), ())), preferred_element_type=F32)


def _split3(x):
    hi = x.astype(jnp.bfloat16)
    r1 = x - hi.astype(F32)
    mid = r1.astype(jnp.bfloat16)
    lo = (r1 - mid.astype(F32)).astype(jnp.bfloat16)
    return hi, mid, lo


def _dot_exact(x, p):
    hi, mid, lo = _split3(x)
    return _dot(hi, p) + _dot(mid, p) + _dot(lo, p)


def _dot_exact_left(p, x):
    hi, mid, lo = _split3(x)
    return _dot(p, hi) + _dot(p, mid) + _dot(p, lo)


_ANY = pl.BlockSpec(memory_space=pl.ANY)


def _chip_peers():
    x, y, c = lax.axis_index("x"), lax.axis_index("y"), lax.axis_index("c")
    return x, y, c, [(1 - x, y), (x, 1 - y), (1 - x, 1 - y)]


class _Job:
    def __init__(self, ins, outs, nsem, copies, cost, alias):
        self.ins, self.outs, self.nsem, self.copies, self.cost, self.alias = ins, outs, nsem, copies, cost, alias
        self.results = None

    def inputs(self):
        return self.ins


class _Task:
    def __init__(self, src, out, parts, init=None):
        self.src, self.out, self.parts, self.init, self.done, self.jobs = src, out, parts, init, 0, []

    def result(self):
        return self.jobs[-1].results[0]

    def complete(self):
        return self.done == len(self.parts) and self.jobs[-1].results is not None

    def ready(self):
        if isinstance(self.src, _Task) and not self.src.complete():
            return False
        return not self.jobs or self.jobs[-1].results is not None

    def next_job(self, n):
        parts = self.parts[self.done:self.done + n]
        in_place = isinstance(self.src, _Task)
        ins = [] if in_place else [self.src]
        prev = self.jobs[-1].results[0] if self.jobs else (self.src.result() if in_place else self.init)
        alias = {}
        if prev is not None:
            alias = {len(ins): 0}
            ins = ins + [prev]

        def copies(in_refs, out_refs, sems):
            src = out_refs[0] if in_place else in_refs[0]
            out, first = [], 0
            for build, nsem, _ in parts:
                out += build(src, out_refs[0], sems, first)
                first += nsem
            return out

        job = _Job(ins, [self.out], sum(p[1] for p in parts), copies, sum(p[2] for p in parts), alias)
        self.done += n
        self.jobs.append(job)
        return job


def _pcall(body, *, out_shape, in_specs, out_specs, name, grid=(), scratch_shapes=(), aliases=None, jobs=(), ins):
    single = not isinstance(out_shape, (tuple, list))
    out_shape = (out_shape,) if single else tuple(out_shape)
    out_specs = (out_specs,) if single else tuple(out_specs)
    n_in, n_out, n_scr = len(ins), len(out_shape), len(scratch_shapes)
    side_ins = [a for j in jobs for a in j.inputs()]
    side_outs = [o for j in jobs for o in j.outs]
    aliases = dict(aliases or {})
    i0, o0 = n_in, n_out
    for j in jobs:
        for a, b in j.alias.items():
            aliases[i0 + a] = o0 + b
        i0 += len(j.ins)
        o0 += len(j.outs)

    def body2(*refs):
        in_refs, s_in = refs[:n_in], refs[n_in:n_in + len(side_ins)]
        r = n_in + len(side_ins)
        out_refs, s_out = refs[r:r + n_out], refs[r + n_out:r + n_out + len(side_outs)]
        r += n_out + len(side_outs)
        scr, sems = refs[r:r + n_scr], refs[r + n_scr:]
        def descriptors():
            out, a, b = [], 0, 0
            for j, sem in zip(jobs, sems):
                out += j.copies(s_in[a:a + len(j.ins)], s_out[b:b + len(j.outs)], sem)
                a += len(j.ins)
                b += len(j.outs)
            return out

        def at_step(steps, fn):
            cond = None
            for d, s in enumerate(steps):
                hit = pl.program_id(d) == s
                cond = hit if cond is None else jnp.logical_and(cond, hit)
            if cond is None:
                fn()
            else:
                pl.when(cond)(fn)

        def start_all():
            for cp in descriptors():
                cp.start()

        def wait_all():
            for cp in descriptors():
                cp.wait()

        if jobs:
            at_step([0] * len(grid), start_all)
        body(*in_refs, *out_refs, *scr)
        if jobs:
            at_step([n - 1 for n in grid], wait_all)

    res = pl.pallas_call(
        body2, out_shape=out_shape + tuple(side_outs), grid=grid, in_specs=list(in_specs) + [_ANY] * len(side_ins),
        out_specs=out_specs + (_ANY,) * len(side_outs),
        scratch_shapes=list(scratch_shapes) + [pltpu.SemaphoreType.DMA((j.nsem,)) for j in jobs],
        input_output_aliases=aliases,
        compiler_params=_params(has_side_effects=True) if jobs else _params(), name=name)(*ins, *side_ins)
    b = n_out
    for j in jobs:
        j.results = list(res[b:b + len(j.outs)])
        b += len(j.outs)
    return res[0] if single else tuple(res[:n_out])


def _comm_call(jobs, name):
    def body():
        pass

    _pcall(body, out_shape=(), in_specs=[], out_specs=(), name=name, jobs=jobs, ins=[])


def _place_own(shard, l, me, *, name):
    _, _, h, C = shard.shape

    def body(_, s_ref, o_ref):
        o_ref[...] = s_ref[...]

    spec = pltpu.PrefetchScalarGridSpec(
        num_scalar_prefetch=1, grid=(2,),
        in_specs=[pl.BlockSpec((None, None, h, C), lambda i, me_ref: (l, i, 0, 0))],
        out_specs=pl.BlockSpec((None, None, h, C), lambda i, me_ref: (me_ref[0], i, 0, 0)))
    return pl.pallas_call(body, grid_spec=spec, out_shape=jax.ShapeDtypeStruct((4, 2, h, C), shard.dtype),
                          compiler_params=_params(), name=name)(me, shard)


def _gather_task1(shard, l, own):
    _, _, h, C = shard.shape
    mb = 3 * h * C * jnp.dtype(_WIRE).itemsize / 1e6
    n = _row_splits(h, ICI_US_PER_MB * mb)
    hr = h // n

    def rows_to_peers(r):
        def build(src, dst, sems, s0):
            x, y, c, chips = _chip_peers()
            me, rows = 2 * x + y, pl.ds(r * hr, hr)
            return [pltpu.make_async_remote_copy(src_ref=src.at[l, c, rows], dst_ref=dst.at[me, c, rows],
                                                 send_sem=sems.at[s0 + 2 * j], recv_sem=sems.at[s0 + 2 * j + 1],
                                                 device_id=(px, py, c), device_id_type=MESH)
                    for j, (px, py) in enumerate(chips)]
        return build

    parts = [(rows_to_peers(r), 6, ICI_US_PER_MB * mb / n) for r in range(n)]
    return _Task(shard, jax.ShapeDtypeStruct((4, 2, h, C), shard.dtype), parts, init=own)


def _gather_task2(task1):
    _, _, h, C = task1.out.shape
    mb = h * C * jnp.dtype(_WIRE).itemsize / 1e6

    def forward(j):
        def build(_, dst, sems, s0):
            x, y, c, chips = _chip_peers()
            px, py = chips[j]
            landed = dst.at[2 * px + py, c]
            return [pltpu.make_async_remote_copy(src_ref=landed, dst_ref=landed, send_sem=sems.at[s0],
                                                 recv_sem=sems.at[s0 + 1], device_id=(x, y, 1 - c),
                                                 device_id_type=MESH)]
        return build

    return _Task(task1, task1.out, [(forward(j), 2, 1.0 + D2D_US_PER_MB * mb) for j in range(3)])


def _scatter_task(G):
    mb = 3 * G.shape[1] * G.shape[2] * jnp.dtype(_WIRE).itemsize / 1e6
    n = _row_splits(G.shape[1], ICI_US_PER_MB * mb)
    rr = G.shape[1] // n

    def rows_to_peers(r):
        def build(src, dst, sems, s0):
            _, _, c, chips = _chip_peers()
            rows = pl.ds(r * rr, rr)
            return [pltpu.make_async_remote_copy(src_ref=src.at[2 * px + py, rows], dst_ref=dst.at[j, rows],
                                                 send_sem=sems.at[s0 + 2 * j], recv_sem=sems.at[s0 + 2 * j + 1],
                                                 device_id=(px, py, c), device_id_type=MESH)
                    for j, (px, py) in enumerate(chips)]
        return build

    parts = [(rows_to_peers(r), 6, ICI_US_PER_MB * mb / n) for r in range(n)]
    return _Task(G, jax.ShapeDtypeStruct((3,) + G.shape[1:], G.dtype), parts)


def _row_splits(rows, cost):
    n = 8
    while n > 1 and (rows % (16 * n) or cost / n < PART_US):
        n //= 2
    return n


CARRIER_US = {
    "f_gu": 48, "f_down": 20, "f_dact": 42, "f_dwd": 20, "f_dwgu": 42, "f_dh": 46,
    "nsa_in": 25, "nsa_cmp": 86, "nsa_slc": 122, "nsa_win": 74, "nsa_dcmp": 58, "nsa_dslc": 195, "nsa_dwin": 108,
    "nsa_dh": 26, "nsa_dwin_w": 25,
    "swa_in": 16, "swa_attn": 74, "swa_dattn": 74, "swa_dh": 17, "swa_dwin_w": 18,
    "fox_in": 34, "fox_attn": 104, "fox_dattn": 161, "fox_dh": 30, "fox_dwin_w": 30,
}
ICI_US_PER_MB = 15.0
PART_US = 12.0
D2D_US_PER_MB = 2.3
LOCAL_US_PER_MB = 1.5


class _Sched:
    def __init__(self):
        self.queue, self.credit, self.n_flush = [], 0.0, 0

    def push(self, task):
        self.queue.append(task)
        return task

    def _jobs(self, fits):
        jobs = []
        while self.queue and self.queue[0].ready():
            task, n = self.queue[0], 0
            while task.done + n < len(task.parts) and fits(task.parts[task.done + n][2]):
                n += 1
            if n:
                jobs.append(task.next_job(n))
            if task.done < len(task.parts):
                break
            self.queue.pop(0)
        return jobs

    def take(self, name):
        kind = name[2:].split("_", 1)
        key = ("f" if kind[0] in ("f1", "f2") else kind[0]) + "_" + kind[1]
        self.credit = min(self.credit, 0.0) + CARRIER_US.get(key, 0.0)

        def fits(cost):
            if self.credit < 0.5 * cost:
                return False
            self.credit -= cost
            return True

        return self._jobs(fits)

    def finish(self, task=None):
        while self.queue and (task is None or not task.complete()):
            jobs = []
            while self.queue and self.queue[0].ready():
                head = self.queue.pop(0)
                jobs.append(head.next_job(len(head.parts) - head.done))
                if head is task:
                    break
            assert jobs, "the task at the head of the queue waits for one that was never queued"
            _comm_call(jobs, name=f"exchange_{self.n_flush}")
            self.n_flush += 1


def _mm(a, b, *, ta=False, tb=False, tm, tn, b_lead=(), into=None, o_lead=(), out_dtype=F32, res=None, alpha=1.0,
        ctx=None, name):
    M = a.shape[1] if ta else a.shape[0]
    K = a.shape[0] if ta else a.shape[1]
    bk, bn = (b.shape[-1], b.shape[-2]) if tb else (b.shape[-2], b.shape[-1])
    j_lead, k_lead = "j" in b_lead, "k" in b_lead
    N = bn * (b.shape[b_lead.index("j")] if j_lead else 1)
    nk = b.shape[b_lead.index("k")] if k_lead else 1
    tk = K // nk
    assert tk == bk, (name, K, nk, bk)
    tm = min(tm, M)
    tn = bn if j_lead else min(tn, N)
    assert M % tm == 0 and N % tn == 0, (name, M, N, tm, tn)
    nb, no = len(b_lead), len(o_lead)

    def pick(lead, j, k):
        return tuple(j if t == "j" else k if t == "k" else t for t in lead)

    a_spec = pl.BlockSpec((tk, tm), lambda i, j, k: (k, i)) if ta else pl.BlockSpec((tm, tk), lambda i, j, k: (i, k))
    if tb:
        b_spec = pl.BlockSpec((None,) * nb + (tn, tk),
                              lambda i, j, k: pick(b_lead, j, k) + (0 if j_lead else j, 0 if k_lead else k))
    else:
        b_spec = pl.BlockSpec((None,) * nb + (tk, tn),
                              lambda i, j, k: pick(b_lead, j, k) + (0 if k_lead else k, 0 if j_lead else j))
    r_spec = pl.BlockSpec((tm, tn), lambda i, j, k: (i, j))
    if o_lead == ("heads",):
        o_spec = pl.BlockSpec((tn // DH, tm, DH), lambda i, j, k: (j, i, 0))
    else:
        o_spec = pl.BlockSpec((None,) * no + (tm, tn),
                              lambda i, j, k: pick(o_lead, j, k) + (i, 0 if "j" in o_lead else j))
    dn = (((0 if ta else 1,), (1 if tb else 0,)), ((), ()))
    has_res, has_into = res is not None, into is not None
    if has_into:
        out_dtype = into.dtype

    def body(*refs):
        a_ref, b_ref = refs[0], refs[1]
        r_ref = refs[2] if has_res else None
        o_ref = refs[2 + has_res + has_into]
        prod = lax.dot_general(a_ref[...].astype(_CDT), b_ref[...].astype(_CDT), dn, preferred_element_type=F32)

        def finish(acc):
            if alpha != 1.0:
                acc = acc * alpha
            if has_res:
                acc = r_ref[...] + acc
            if o_lead == ("heads",):
                for hd in range(tn // DH):
                    o_ref[hd] = acc[:, hd * DH:(hd + 1) * DH].astype(out_dtype)
            else:
                o_ref[...] = acc.astype(out_dtype)

        if nk == 1:
            finish(prod)
        else:
            acc_ref = refs[-1]
            k = pl.program_id(2)

            @pl.when(k == 0)
            def _():
                acc_ref[...] = prod

            @pl.when(k != 0)
            def _():
                acc_ref[...] += prod

            @pl.when(k == nk - 1)
            def _():
                finish(acc_ref[...])

    ins, specs = [a, b], [a_spec, b_spec]
    if has_res:
        ins.append(res)
        specs.append(r_spec)
    aliases = {}
    if has_into:
        aliases = {len(ins): 0}
        ins.append(into)
        specs.append(_ANY)
        out_shape = jax.ShapeDtypeStruct(into.shape, into.dtype)
    elif o_lead == ("j",):
        out_shape = jax.ShapeDtypeStruct((N // tn, M, tn), out_dtype)
    elif o_lead == ("heads",):
        out_shape = jax.ShapeDtypeStruct((N // DH, M, DH), out_dtype)
    else:
        assert not o_lead
        out_shape = jax.ShapeDtypeStruct((M, N), out_dtype)
    scratch = [pltpu.VMEM((tm, tn), F32)] if nk > 1 else []
    return _pcall(body, out_shape=out_shape, grid=(M // tm, N // tn, nk), in_specs=specs, out_specs=o_spec,
                  scratch_shapes=scratch, aliases=aliases, name=name, jobs=ctx.take(name) if ctx else (), ins=ins)


def _rows2d(a):
    return a.reshape(-1, a.shape[-1])


def _row_tile(rows, cols, itemsize=4, budget=2 * 1024 * 1024):
    t = rows
    while t % 2 == 0 and t * cols * itemsize > budget and (t // 2) % 8 == 0:
        t //= 2
    return t


def _addn(*xs, name):
    shape = xs[0].shape
    x2 = [_rows2d(x) for x in xs]
    R, C = x2[0].shape
    tr = _row_tile(R, C)

    def body(*refs):
        acc = refs[0][...]
        for r in refs[1:-1]:
            acc = acc + r[...]
        refs[-1][...] = acc

    spec = pl.BlockSpec((tr, C), lambda i: (i, 0))
    out = pl.pallas_call(body, out_shape=jax.ShapeDtypeStruct((R, C), F32), grid=(R // tr,),
                         in_specs=[spec] * len(x2), out_specs=spec, compiler_params=_params(), name=name)(*x2)
    return out.reshape(shape)


def _rms_fwd(x, g, *, name):
    S, D = x.shape
    tr = 256

    def body(x_ref, g_ref, h_ref):
        xv = x_ref[...]
        rstd = lax.rsqrt(jnp.mean(xv * xv, axis=-1, keepdims=True) + RMS_EPS)
        h_ref[...] = (xv * rstd * g_ref[...]).astype(_CDT)

    return pl.pallas_call(body, out_shape=jax.ShapeDtypeStruct((S, D), _CDT), grid=(S // tr,),
                          in_specs=[pl.BlockSpec((tr, D), lambda i: (i, 0)), pl.BlockSpec((1, D), lambda i: (0, 0))],
                          out_specs=pl.BlockSpec((tr, D), lambda i: (i, 0)), compiler_params=_params(),
                          name=name)(x, g.reshape(1, D))


def _rms_bwd(dh, x, g, dres, *, name):
    S, D = x.shape
    tr = 256

    def body(dh_ref, x_ref, g_ref, dres_ref, dx_ref, dg_ref):
        xv = x_ref[...]
        rstd = lax.rsqrt(jnp.mean(xv * xv, axis=-1, keepdims=True) + RMS_EPS)
        xhat = xv * rstd
        dhv = dh_ref[...]
        dxhat = dhv * g_ref[...]
        dx_ref[...] = dres_ref[...] + rstd * (dxhat - xhat * jnp.mean(dxhat * xhat, axis=-1, keepdims=True))

        @pl.when(pl.program_id(0) == 0)
        def _():
            dg_ref[...] = jnp.zeros_like(dg_ref)

        dg_ref[...] += jnp.sum(dhv * xhat, axis=0, keepdims=True)

    row = pl.BlockSpec((tr, D), lambda i: (i, 0))
    vec = pl.BlockSpec((1, D), lambda i: (0, 0))
    dx, dg = pl.pallas_call(body, out_shape=(jax.ShapeDtypeStruct((S, D), F32), jax.ShapeDtypeStruct((1, D), F32)),
                            grid=(S // tr,), in_specs=[row, row, vec, row], out_specs=(row, vec),
                            compiler_params=_params(), name=name)(dh, x, g.reshape(1, D), dres)
    return dx, dg.reshape(D)


def _loss_head(x, g, tgt, *, name):
    S, D = x.shape
    tr = 256

    def body(x_ref, g_ref, t_ref, loss_ref, dx_ref, dg_ref):
        xv = x_ref[...]
        rstd = lax.rsqrt(jnp.mean(xv * xv, axis=-1, keepdims=True) + RMS_EPS)
        xhat = xv * rstd
        err = xhat * g_ref[...] - t_ref[...]
        part = 0.5 * jnp.sum(jnp.mean(err * err, axis=-1, keepdims=True), axis=0, keepdims=True)
        dy = err * (1.0 / D)
        dxhat = dy * g_ref[...]
        dx_ref[...] = rstd * (dxhat - xhat * jnp.mean(dxhat * xhat, axis=-1, keepdims=True))

        @pl.when(pl.program_id(0) == 0)
        def _():
            dg_ref[...] = jnp.zeros_like(dg_ref)
            loss_ref[...] = jnp.zeros_like(loss_ref)

        dg_ref[...] += jnp.sum(dy * xhat, axis=0, keepdims=True)
        loss_ref[...] += jnp.broadcast_to(part, loss_ref.shape)

    row = pl.BlockSpec((tr, D), lambda i: (i, 0))
    vec = pl.BlockSpec((1, D), lambda i: (0, 0))
    loss, dx, dg = pl.pallas_call(
        body, out_shape=(jax.ShapeDtypeStruct((8, LANES), F32), jax.ShapeDtypeStruct((S, D), F32),
                         jax.ShapeDtypeStruct((1, D), F32)),
        grid=(S // tr,), in_specs=[row, vec, row], out_specs=(pl.BlockSpec((8, LANES), lambda i: (0, 0)), row, vec),
        compiler_params=_params(), name=name)(x, g.reshape(1, D), tgt)
    return loss[0, 0], dx, dg.reshape(D)


def _rope_tables(S):
    inv = 10000.0 ** (-jnp.arange(0, DH, 2, dtype=F32) / DH)
    ang = jnp.arange(S, dtype=F32)[:, None] * inv[None, :]
    cos, sin = jnp.cos(ang), jnp.sin(ang)
    return jnp.concatenate([cos, cos], -1), jnp.concatenate([-sin, sin], -1)


def _swap_matrix():
    p = np.zeros((DH, DH), np.float32)
    for j in range(DH // 2):
        p[j + DH // 2, j] = 1.0
        p[j, j + DH // 2] = 1.0
    return jnp.asarray(p, jnp.bfloat16)


def _rope(x, cc, ss, rotated, *, name):
    n, S, _ = x.shape
    assert len(rotated) == n
    starts = [i for i in range(n) if rotated[i] and (i == 0 or not rotated[i - 1])]
    tab = pl.BlockSpec((S, DH), lambda i: (0, 0))
    for k, first in enumerate(starts):
        count = next((i for i in range(first, n) if not rotated[i]), n) - first

        def body(x_ref, c_ref, s_ref, p_ref, o_ref):
            xv = x_ref[0]
            o_ref[0] = xv * c_ref[...] + _dot_exact(xv, p_ref[...]) * s_ref[...]

        blk = pl.BlockSpec((1, S, DH), lambda i, first=first: (first + i, 0, 0))
        x = pl.pallas_call(body, out_shape=jax.ShapeDtypeStruct(x.shape, F32), grid=(count,),
                           in_specs=[blk, tab, tab, pl.BlockSpec((DH, DH), lambda i: (0, 0))], out_specs=blk,
                           input_output_aliases={0: 0}, compiler_params=_params(),
                           name=f"{name}{k}")(x, cc, ss, _swap_matrix())
    return x


def _key_range(kind, i, window, Sk):
    if kind == "cmp":
        return 0, Sk
    hi = (i + 1) * BQ
    if kind == "band":
        return max(0, i * BQ - window), hi
    return 0, hi


def _attn_mask(kind, i, lo, hi, window):
    shape = (BQ, hi - lo)
    qpos = i * BQ + lax.broadcasted_iota(jnp.int32, shape, 0)
    kpos = lo + lax.broadcasted_iota(jnp.int32, shape, 1)
    if kind == "cmp":
        return kpos * 16 + 31 <= qpos
    mask = kpos <= qpos
    if kind == "band":
        mask = mask & (qpos - kpos < window)
    return mask


def _sel_expand(n_slc, n_keys):
    shape = (n_slc, n_keys)
    j = lax.broadcasted_iota(jnp.int32, shape, 0)
    key = lax.broadcasted_iota(jnp.int32, shape, 1)
    return (jnp.right_shift(key, 6) == j).astype(_CDT)


def _eye():
    return lax.broadcasted_iota(jnp.int32, (BQ, BQ), 0) == lax.broadcasted_iota(jnp.int32, (BQ, BQ), 1)


def _to_col(row):
    return jnp.sum(jnp.where(_eye(), row, 0.0), axis=1, keepdims=True)


def _to_row(col):
    return jnp.sum(jnp.where(_eye(), col, 0.0), axis=0, keepdims=True)


def _scores(kind, i, lo, hi, window, qb, kb, crow_ref, sel_ref):
    s = _dot_nt(qb, kb)
    if crow_ref is not None:
        s = s + _to_col(crow_ref[0, :, i * BQ:(i + 1) * BQ]) - crow_ref[0, :, lo:hi]
    mask = _attn_mask(kind, i, lo, hi, window)
    if sel_ref is not None:
        mask = mask & (sel_ref[0, i * BQ:(i + 1) * BQ, lo:hi].astype(F32) > 0.5)
    return jnp.where(mask, s, NEG), mask


def _attn_fwd(q, k, v, *, G, R, heads=(0, 0, 0), kind, window=0, bias=None, sinks=None, selT=None, ovT=None, ctx=None,
              name):
    S = q.shape[1]
    Sk = k.shape[1]
    q0, k0, v0 = heads
    nq = S // BQ
    n_slc = S // 64
    has_bias, has_sink, has_sel, is_cmp = bias is not None, sinks is not None, selT is not None, kind == "cmp"

    def body(*refs):
        it = iter(refs)
        q_ref, k_ref, v_ref = next(it), next(it), next(it)
        crow_ref = next(it) if has_bias else None
        sink_ref = next(it) if has_sink else None
        sel_ref = next(it) if has_sel else None
        ov_ref = next(it) if is_cmp else None
        o_ref, lse_ref = next(it), next(it)
        selo_ref, imp_ref = (next(it), next(it)) if is_cmp else (None, None)
        r = pl.program_id(1)
        for i in range(nq):
            lo, hi = _key_range(kind, i, window, Sk)
            rows = slice(i * BQ, (i + 1) * BQ)
            qb = (q_ref[rows, :] * SCALE).astype(_CDT)
            kb = k_ref[0, lo:hi, :].astype(_CDT)
            vb = v_ref[0, lo:hi, :].astype(_CDT)
            s, mask = _scores(kind, i, lo, hi, window, qb, kb, crow_ref, sel_ref)
            m = jnp.max(s, axis=-1, keepdims=True)
            if has_sink:
                sk = sink_ref[0, 0, 0:1, 0:1]
                m = jnp.maximum(m, sk)
            e = jnp.exp(s - m)
            if is_cmp:
                e = jnp.where(mask, e, 0.0)
            l = jnp.sum(e, axis=-1, keepdims=True)
            if has_sink:
                l = l + jnp.exp(sk - m)
            if is_cmp:
                l = jnp.where(l > 0.0, l, 1.0)
            p = e * (1.0 / l)
            o_ref[0, 0, rows, :] = _dot(p.astype(_CDT), vb)
            lse_ref[0, 0, :, rows] = _to_row(m + jnp.log(l))
            if is_cmp:
                part = _dot_nt(ov_ref[...].astype(_CDT), p.astype(_CDT))

                @pl.when(r == 0)
                def _():
                    imp_ref[:, rows] = part

                @pl.when(r != 0)
                def _():
                    imp_ref[:, rows] += part

        if is_cmp:
            @pl.when(r == R - 1)
            def _():
                shape = (n_slc, S)
                j = lax.broadcasted_iota(jnp.int32, shape, 0)
                tb = jnp.right_shift(lax.broadcasted_iota(jnp.int32, shape, 1), 6)
                forced = (j == 0) | (j == tb) | (j == tb - 1)
                imp = jnp.where(j > tb, NEG, jnp.where(forced, NSA_BONUS, imp_ref[0:n_slc, :]))
                imp_ref[0:n_slc, :] = imp
                cnt = jnp.zeros(shape, F32)
                for jp in range(n_slc):
                    row = imp_ref[jp:jp + 1, :]
                    ahead = (row > imp) | ((row == imp) & (jp < j))
                    cnt = cnt + ahead.astype(F32)
                imp_ref[0:n_slc, :] = (cnt < float(min(NSA_TOPK, n_slc))).astype(F32)
                expand = _sel_expand(n_slc, S)
                for i in range(nq):
                    rows = slice(i * BQ, (i + 1) * BQ)
                    chosen = _dot_tn(imp_ref[0:n_slc, rows].astype(_CDT), expand)
                    selo_ref[0, rows, :] = chosen.astype(jnp.bfloat16)

    qspec = pl.BlockSpec((1, 1, S, DH), lambda g, r: (g, r, 0, 0))
    ins = [q, k, v]
    specs = [pl.BlockSpec((None, S, DH), lambda g, r: (q0 + g * R + r, 0, 0)),
             pl.BlockSpec((1, Sk, DH), lambda g, r: (k0 + g, 0, 0)), pl.BlockSpec((1, Sk, DH), lambda g, r: (v0 + g, 0, 0))]
    if has_bias:
        ins.append(bias)
        specs.append(pl.BlockSpec((1, 1, S), lambda g, r: (g, 0, 0)))
    if has_sink:
        ins.append(sinks)
        specs.append(pl.BlockSpec((1, 1, 8, LANES), lambda g, r: (g, r, 0, 0)))
    if has_sel:
        ins.append(selT)
        specs.append(pl.BlockSpec((1, S, S), lambda g, r: (g, 0, 0)))
    if is_cmp:
        ins.append(ovT)
        specs.append(pl.BlockSpec((LANES, Sk), lambda g, r: (0, 0)))
    outs = [jax.ShapeDtypeStruct((G, R, S, DH), F32), jax.ShapeDtypeStruct((G, R, 1, S), F32)]
    ospecs = [qspec, pl.BlockSpec((1, 1, 1, S), lambda g, r: (g, r, 0, 0))]
    scratch = []
    if is_cmp:
        outs.append(jax.ShapeDtypeStruct((G, S, S), jnp.bfloat16))
        ospecs.append(pl.BlockSpec((1, S, S), lambda g, r: (g, 0, 0)))
        scratch.append(pltpu.VMEM((LANES, S), F32))
    return _pcall(body, out_shape=tuple(outs), grid=(G, R), in_specs=specs, out_specs=tuple(ospecs),
                  scratch_shapes=scratch, name=name, jobs=ctx.take(name) if ctx else (), ins=ins)


def _attn_bwd(q, k, v, o, do, lse, *, heads=(0, 0, 0), kind, window=0, bias=None, sinks=None, selT=None, gate=None,
              ctx=None, name):
    G, R, S, _ = o.shape
    Sk = k.shape[1]
    q0, k0, v0 = heads
    nq = S // BQ
    has_bias, has_sink, has_sel, has_gate = bias is not None, sinks is not None, selT is not None, gate is not None

    def body(*refs):
        it = iter(refs)
        q_ref, k_ref, v_ref, o_ref, do_ref, lse_ref = (next(it) for _ in range(6))
        crow_ref = next(it) if has_bias else None
        sink_ref = next(it) if has_sink else None
        sel_ref = next(it) if has_sel else None
        z_ref = next(it) if has_gate else None
        dq_ref, dk_ref, dv_ref = next(it), next(it), next(it)
        dc_ref = next(it) if has_bias else None
        dsink_ref = next(it) if has_sink else None
        dz_ref = next(it) if has_gate else None
        r = pl.program_id(1)

        @pl.when(r == 0)
        def _():
            dk_ref[...] = jnp.zeros_like(dk_ref)
            dv_ref[...] = jnp.zeros_like(dv_ref)
            if has_bias:
                dc_ref[...] = jnp.zeros_like(dc_ref)

        dsink = jnp.zeros((1, 1), F32)
        for i in range(nq):
            lo, hi = _key_range(kind, i, window, Sk)
            rows = slice(i * BQ, (i + 1) * BQ)
            qb = (q_ref[rows, :] * SCALE).astype(_CDT)
            kb = k_ref[0, lo:hi, :].astype(_CDT)
            vb = v_ref[0, lo:hi, :].astype(_CDT)
            s, mask = _scores(kind, i, lo, hi, window, qb, kb, crow_ref, sel_ref)
            lse_i = _to_col(lse_ref[0, 0, :, rows])
            p = jnp.where(mask, jnp.exp(s - lse_i), 0.0)
            dob = do_ref[0, 0, rows, :]
            if has_gate:
                od = jnp.sum(o_ref[0, 0, rows, :] * dob, axis=-1, keepdims=True)
                sg = jax.nn.sigmoid(_to_col(z_ref[0, 0, :, rows]))
                dob = dob * sg
                dz_ref[0, 0, :, rows] = _to_row(od * sg * (1.0 - sg))
            dob = dob.astype(_CDT)
            dp = _dot_nt(dob, vb)
            delta = jnp.sum(p * dp, axis=-1, keepdims=True)
            ds = p * (dp - delta)
            dsb = ds.astype(_CDT)
            dq_ref[0, 0, rows, :] = _dot(dsb, kb) * SCALE
            dk_ref[0, lo:hi, :] += _dot_tn(dsb, qb)
            dv_ref[0, lo:hi, :] += _dot_tn(p.astype(_CDT), dob)
            if has_bias:
                dc_ref[0, :, rows] += _to_row(jnp.sum(ds, axis=-1, keepdims=True))
                dc_ref[0, :, lo:hi] -= jnp.sum(ds, axis=0, keepdims=True)
            if has_sink:
                sk = sink_ref[0, 0, 0:1, 0:1]
                dsink = dsink - jnp.sum(jnp.exp(sk - lse_i) * delta, axis=0, keepdims=True)
        if has_sink:
            dsink_ref[0, 0] = jnp.broadcast_to(dsink, (8, LANES))

    qspec = pl.BlockSpec((1, 1, S, DH), lambda g, r: (g, r, 0, 0))
    cspec = pl.BlockSpec((1, 1, 1, S), lambda g, r: (g, r, 0, 0))
    kspec = pl.BlockSpec((1, Sk, DH), lambda g, r: (g, 0, 0))
    ins = [q, k, v, o, do, lse]
    specs = [pl.BlockSpec((None, S, DH), lambda g, r: (q0 + g * R + r, 0, 0)),
             pl.BlockSpec((1, Sk, DH), lambda g, r: (k0 + g, 0, 0)), pl.BlockSpec((1, Sk, DH), lambda g, r: (v0 + g, 0, 0)),
             qspec, qspec, cspec]
    if has_bias:
        ins.append(bias)
        specs.append(pl.BlockSpec((1, 1, S), lambda g, r: (g, 0, 0)))
    if has_sink:
        ins.append(sinks)
        specs.append(pl.BlockSpec((1, 1, 8, LANES), lambda g, r: (g, r, 0, 0)))
    if has_sel:
        ins.append(selT)
        specs.append(pl.BlockSpec((1, S, S), lambda g, r: (g, 0, 0)))
    if has_gate:
        ins.append(gate)
        specs.append(cspec)
    names = ["dq", "dk", "dv"]
    outs = [jax.ShapeDtypeStruct((G, R, S, DH), F32), jax.ShapeDtypeStruct((G, Sk, DH), F32),
            jax.ShapeDtypeStruct((G, Sk, DH), F32)]
    ospecs = [qspec, kspec, kspec]
    if has_bias:
        assert R == 1
        names.append("dc")
        outs.append(jax.ShapeDtypeStruct((G, 1, S), F32))
        ospecs.append(pl.BlockSpec((1, 1, S), lambda g, r: (g, 0, 0)))
    if has_sink:
        names.append("dsink")
        outs.append(jax.ShapeDtypeStruct((G, R, 8, LANES), F32))
        ospecs.append(pl.BlockSpec((1, 1, 8, LANES), lambda g, r: (g, r, 0, 0)))
    if has_gate:
        names.append("dz")
        outs.append(jax.ShapeDtypeStruct((G, R, 1, S), F32))
        ospecs.append(cspec)
    res = _pcall(body, out_shape=tuple(outs), grid=(G, R), in_specs=specs, out_specs=tuple(ospecs), name=name,
                 jobs=ctx.take(name) if ctx else (), ins=ins)
    return dict(zip(names, res))


def _combine(o0, o1, o2, z, *, name):
    H, S, _ = o0.shape

    def body(o0_ref, o1_ref, o2_ref, z_ref, o_ref):
        for i in range(S // BQ):
            rows = slice(i * BQ, (i + 1) * BQ)
            acc = jax.nn.sigmoid(_to_col(z_ref[0, 0, :, rows])) * o0_ref[0, rows, :]
            acc = acc + jax.nn.sigmoid(_to_col(z_ref[1, 0, :, rows])) * o1_ref[0, rows, :]
            acc = acc + jax.nn.sigmoid(_to_col(z_ref[2, 0, :, rows])) * o2_ref[0, rows, :]
            o_ref[0, rows, :] = acc

    blk = pl.BlockSpec((1, S, DH), lambda h: (h, 0, 0))
    return pl.pallas_call(body, out_shape=jax.ShapeDtypeStruct((H, S, DH), F32), grid=(H,),
                          in_specs=[blk, blk, blk, pl.BlockSpec((3, 1, 1, S), lambda h: (0, h, 0, 0))], out_specs=blk,
                          compiler_params=_params(), name=name)(o0, o1, o2, z)


_GC = math.sqrt(2.0 / math.pi)


def _gelu(x):
    return 0.5 * x * (1.0 + jnp.tanh(_GC * (x + 0.044715 * x * x * x)))


def _gelu_grad(x):
    t = jnp.tanh(_GC * (x + 0.044715 * x * x * x))
    return 0.5 * (1.0 + t) + 0.5 * x * (1.0 - t * t) * _GC * (1.0 + 3.0 * 0.044715 * x * x)


def _make_xb(k):
    G, S, _ = k.shape
    chunks = k.reshape(G, S // 16, 16 * DH)
    shift = jnp.concatenate([chunks[:, 1:], jnp.zeros((G, 1, 16 * DH), k.dtype)], axis=1)
    return jnp.concatenate([chunks, shift], axis=-1)


def _unmake_xb(dxb, *, name):
    G, n, _ = dxb.shape
    a = dxb[..., :16 * DH]
    b = jnp.concatenate([jnp.zeros((G, 1, 16 * DH), F32), dxb[:, :-1, 16 * DH:]], axis=1)
    return _addn(a, b, name=name).reshape(G, n * 16, DH)


def _compress_fwd(xb, pe, w1, w2, *, name):
    G, n, W = xb.shape
    Hc = w1.shape[1]

    def body(xb_ref, pe_ref, w1_ref, w2_ref, kc_ref, hid_ref):
        xv = (xb_ref[0] + pe_ref[...]).astype(_CDT)
        hid = _dot(xv, w1_ref[...].astype(_CDT))
        hid_ref[0] = hid
        kc_ref[0] = _dot(_gelu(hid).astype(_CDT), w2_ref[...].astype(_CDT))

    return pl.pallas_call(
        body, out_shape=(jax.ShapeDtypeStruct((G, n, DH), F32), jax.ShapeDtypeStruct((G, n, Hc), F32)), grid=(G,),
        in_specs=[pl.BlockSpec((1, n, W), lambda g: (g, 0, 0)), pl.BlockSpec((1, W), lambda g: (0, 0)),
                  pl.BlockSpec((W, Hc), lambda g: (0, 0)), pl.BlockSpec((Hc, DH), lambda g: (0, 0))],
        out_specs=(pl.BlockSpec((1, n, DH), lambda g: (g, 0, 0)), pl.BlockSpec((1, n, Hc), lambda g: (g, 0, 0))),
        compiler_params=_params(), name=name)(xb, pe.reshape(1, W), w1, w2)


def _compress_bwd(xb, pe, w1, w2, hid, dkc, *, name):
    G, n, W = xb.shape
    Hc = w1.shape[1]

    def body(xb_ref, pe_ref, w1_ref, w2_ref, hid_ref, dkc_ref, dxb_ref, dw1_ref, dw2_ref, dpe_ref):
        @pl.when(pl.program_id(0) == 0)
        def _():
            dw1_ref[...] = jnp.zeros_like(dw1_ref)
            dw2_ref[...] = jnp.zeros_like(dw2_ref)
            dpe_ref[...] = jnp.zeros_like(dpe_ref)

        xv = (xb_ref[0] + pe_ref[...]).astype(_CDT)
        hid = hid_ref[0]
        dk = dkc_ref[0].astype(_CDT)
        dact = _dot_nt(dk, w2_ref[...].astype(_CDT))
        dhid = (dact * _gelu_grad(hid)).astype(_CDT)
        dw2_ref[...] += _dot_tn(_gelu(hid).astype(_CDT), dk)
        dxb = _dot_nt(dhid, w1_ref[...].astype(_CDT))
        dxb_ref[0] = dxb
        dw1_ref[...] += _dot_tn(xv, dhid)
        dpe_ref[...] += jnp.sum(dxb, axis=0, keepdims=True)

    return pl.pallas_call(
        body, out_shape=(jax.ShapeDtypeStruct((G, n, W), F32), jax.ShapeDtypeStruct((W, Hc), F32),
                         jax.ShapeDtypeStruct((Hc, DH), F32), jax.ShapeDtypeStruct((1, W), F32)), grid=(G,),
        in_specs=[pl.BlockSpec((1, n, W), lambda g: (g, 0, 0)), pl.BlockSpec((1, W), lambda g: (0, 0)),
                  pl.BlockSpec((W, Hc), lambda g: (0, 0)), pl.BlockSpec((Hc, DH), lambda g: (0, 0)),
                  pl.BlockSpec((1, n, Hc), lambda g: (g, 0, 0)), pl.BlockSpec((1, n, DH), lambda g: (g, 0, 0))],
        out_specs=(pl.BlockSpec((1, n, W), lambda g: (g, 0, 0)), pl.BlockSpec((W, Hc), lambda g: (0, 0)),
                   pl.BlockSpec((Hc, DH), lambda g: (0, 0)), pl.BlockSpec((1, W), lambda g: (0, 0))),
        compiler_params=_params(), name=name)(xb, pe.reshape(1, W), w1, w2, hid, dkc)


def _overlap_T(S):
    n_cmp, n_slc = S // 16 - 1, S // 64
    cs = np.arange(n_cmp) * 16
    ce = cs + 32
    ss = np.arange(n_slc) * 64
    se = ss + 64
    ov = np.clip(np.minimum(ce[:, None], se[None, :]) - np.maximum(cs[:, None], ss[None, :]), 0, None) / 32.0
    out = np.zeros((LANES, S // 16), np.float32)
    out[:n_slc, :n_cmp] = ov.T
    return jnp.asarray(out)


def _tri(n, upper):
    r = lax.broadcasted_iota(jnp.int32, (n, n), 0)
    c = lax.broadcasted_iota(jnp.int32, (n, n), 1)
    return ((c >= r) if upper else (c <= r)).astype(jnp.bfloat16)


def _fox_gate_fwd(zf, b, *, name):
    S, H = zf.shape
    nb = S // BQ

    def body(z_ref, b_ref, c_ref):
        tri = _tri(BQ, False)
        carry = jnp.zeros((1, H), F32)
        for i in range(nb):
            z = z_ref[i * BQ:(i + 1) * BQ, :] + b_ref[...]
            lf = jnp.minimum(z, 0.0) - jnp.log(1.0 + jnp.exp(-jnp.abs(z)))
            c_ref[i * BQ:(i + 1) * BQ, :] = _dot_exact_left(tri, lf) + carry
            carry = carry + jnp.sum(lf, axis=0, keepdims=True)

    return pl.pallas_call(body, out_shape=jax.ShapeDtypeStruct((S, H), F32), compiler_params=_params(),
                          name=name)(zf, b)


def _fox_gate_bwd(zf, b, dc, *, name):
    S, H = zf.shape
    nb = S // BQ

    def body(z_ref, b_ref, dc_ref, dz_ref, db_ref):
        tri = _tri(BQ, True)
        carry = jnp.zeros((1, H), F32)
        db = jnp.zeros((1, H), F32)
        for i in reversed(range(nb)):
            rows = slice(i * BQ, (i + 1) * BQ)
            dcb = dc_ref[rows, :]
            dlf = _dot_exact_left(tri, dcb) + carry
            carry = carry + jnp.sum(dcb, axis=0, keepdims=True)
            z = z_ref[rows, :] + b_ref[...]
            dz = dlf * jax.nn.sigmoid(-z)
            dz_ref[rows, :] = dz
            db = db + jnp.sum(dz, axis=0, keepdims=True)
        db_ref[...] = db

    return pl.pallas_call(body, out_shape=(jax.ShapeDtypeStruct((S, H), F32), jax.ShapeDtypeStruct((1, H), F32)),
                          compiler_params=_params(), name=name)(zf, b, dc)


def _from_heads(a):
    return a.transpose(1, 0, 2).reshape(a.shape[1], -1).astype(_CDT)


def _pad_lanes(a):
    return jnp.pad(a, ((0, 0), (0, LANES - a.shape[1])))


FFN_TM = 512


def _ffn_up(h, w4, *, ctx, name):
    S, D = h.shape
    C = w4.shape[2]

    def body(h_ref, wg_ref, wu_ref, gu_ref, a_ref):
        hv = h_ref[...].astype(_CDT)
        g = _dot(hv, wg_ref[...].astype(_CDT))
        u = _dot(hv, wu_ref[...].astype(_CDT))
        gu_ref[0] = g
        gu_ref[1] = u
        a_ref[...] = (g * jax.nn.sigmoid(g) * u).astype(_CDT)

    return _pcall(
        body, out_shape=(jax.ShapeDtypeStruct((2, S, 2 * C), F32), jax.ShapeDtypeStruct((S, 2 * C), _CDT)),
        grid=(2, S // FFN_TM),
        in_specs=[pl.BlockSpec((FFN_TM, D), lambda j, i: (i, 0)), pl.BlockSpec((None, D, C), lambda j, i: (j, 0, 0)),
                  pl.BlockSpec((None, D, C), lambda j, i: (j + 2, 0, 0))],
        out_specs=(pl.BlockSpec((2, FFN_TM, C), lambda j, i: (0, i, j)), pl.BlockSpec((FFN_TM, C), lambda j, i: (i, j))),
        name=name, jobs=ctx.take(name) if ctx else (), ins=[h, w4, w4])


def _ffn_dact(dxo, wd, gu, *, ctx, name):
    S, D = dxo.shape
    C = gu.shape[2] // 2

    def body(dx_ref, wd_ref, gu_ref, d_ref):
        da = _dot_nt(dx_ref[...].astype(_CDT), wd_ref[...].astype(_CDT)) * 0.5
        g, u = gu_ref[0], gu_ref[1]
        sg = jax.nn.sigmoid(g)
        silu = g * sg
        d_ref[0] = (da * u * (sg + silu * (1.0 - sg))).astype(_CDT)
        d_ref[1] = (da * silu).astype(_CDT)

    blk = pl.BlockSpec((2, FFN_TM, C), lambda j, i: (0, i, j))
    return _pcall(body, out_shape=jax.ShapeDtypeStruct(gu.shape, _CDT), grid=(2, S // FFN_TM),
                  in_specs=[pl.BlockSpec((FFN_TM, D), lambda j, i: (i, 0)), pl.BlockSpec((C, D), lambda j, i: (j, 0)), blk],
                  out_specs=blk, name=name, jobs=ctx.take(name) if ctx else (), ins=[dxo, wd, gu])


def _ffn_dwgu(h, dgu, *, ctx, name):
    S, D = h.shape
    C = dgu.shape[2] // 2
    tm = 512

    def body(h_ref, d_ref, o_ref):
        o_ref[...] = _dot_tn(h_ref[...].astype(_CDT), d_ref[...].astype(_CDT)).astype(_WIRE)

    return _pcall(body, out_shape=jax.ShapeDtypeStruct((4, D, C), _WIRE), grid=(4, D // tm),
                  in_specs=[pl.BlockSpec((S, tm), lambda j, i: (0, i)),
                            pl.BlockSpec((None, S, C), lambda j, i: (j // 2, 0, j % 2))],
                  out_specs=pl.BlockSpec((None, tm, C), lambda j, i: (j, i, 0)), name=name,
                  jobs=ctx.take(name) if ctx else (), ins=[h, dgu])


def _ffn_dh(dgu, w4, *, ctx, name):
    _, S, F2 = dgu.shape
    C, D = F2 // 2, w4.shape[1]

    def body(d_ref, w_ref, o_ref, acc_ref):
        k = pl.program_id(1)
        prod = _dot_nt(d_ref[...].astype(_CDT), w_ref[...].astype(_CDT))

        @pl.when(k == 0)
        def _():
            acc_ref[...] = prod

        @pl.when(k != 0)
        def _():
            acc_ref[...] += prod

        @pl.when(k == 3)
        def _():
            o_ref[...] = acc_ref[...]

    return _pcall(body, out_shape=jax.ShapeDtypeStruct((S, D), F32), grid=(S // FFN_TM, 4),
                  in_specs=[pl.BlockSpec((None, FFN_TM, C), lambda i, k: (k // 2, i, k % 2)),
                            pl.BlockSpec((None, D, C), lambda i, k: (k, 0, 0))],
                  out_specs=pl.BlockSpec((FFN_TM, D), lambda i, k: (i, 0)), scratch_shapes=[pltpu.VMEM((FFN_TM, D), F32)],
                  name=name, jobs=ctx.take(name) if ctx else (), ins=[dgu, w4])


def _ffn_fwd(x, g, P, l, tag):
    h = _rms_fwd(x, g, name=tag + "_rms")
    gu, a = _ffn_up(h, P.weight("gu", l), ctx=P.ctx, name=tag + "_gu")
    xo = _mm(a, P.weight("down", l), tm=512, tn=1024, res=x, alpha=0.5, ctx=P.ctx, name=tag + "_down")
    return xo, (x, h, gu, a)


def _ffn_bwd(dxo, saved, g, P, l, tag):
    x, h, gu, a = saved
    dgu = _ffn_dact(dxo, P.weight("down", l), gu, ctx=P.ctx, name=tag + "_dact")
    P.grad("gu", l, _ffn_dwgu(h, dgu, ctx=P.ctx, name=tag + "_dwgu"))
    dwd = _mm(a, dxo, ta=True, tm=1408, tn=1024, alpha=0.5, out_dtype=_WIRE, ctx=P.ctx, name=tag + "_dwd")
    P.grad("down", l, dwd.reshape(4, -1, D_MODEL))
    dh = _ffn_dh(dgu, P.weight("gu", l), ctx=P.ctx, name=tag + "_dh")
    dx, dg = _rms_bwd(dh, x, g, dxo, name=tag + "_drms")
    return dx, dg


def _nsa_fwd(x, g, w, cc, ss, tag):
    S = x.shape[0]
    h = _rms_fwd(x, g, name=tag + "_rms")
    hm = _mm(h, w["w_in"], tm=512, tn=896, o_lead=("heads",), ctx=w["ctx"], name=tag + "_in")
    roped = _rope(hm, cc, ss, NSA_ROTATED, name=tag + "_rope")
    z = roped[40, :, :3 * N_HEADS].reshape(S, 3, N_HEADS).transpose(1, 2, 0)[:, :, None, :]
    xbk, xbv = _make_xb(roped[16:20]), _make_xb(roped[20:24])
    kc, hidk = _compress_fwd(xbk, w["ck_pe"], w["ck_w1"], w["ck_w2"], name=tag + "_ck")
    vc, hidv = _compress_fwd(xbv, w["cv_pe"], w["cv_w1"], w["cv_w2"], name=tag + "_cv")
    ctx, GR = w["ctx"], dict(G=NSA_G, R=NSA_R)
    o0, lse0, selT = _attn_fwd(roped, kc, vc, kind="cmp", ovT=_overlap_T(S), ctx=ctx, name=tag + "_cmp", **GR)
    o1, lse1 = _attn_fwd(roped, roped, roped, heads=(0, 24, 28), kind="sel", selT=selT, ctx=ctx, name=tag + "_slc", **GR)
    o2, lse2 = _attn_fwd(roped, roped, roped, heads=(0, 32, 36), kind="band", window=NSA_WINDOW, ctx=ctx,
                         name=tag + "_win", **GR)
    o = _combine(o0.reshape(N_HEADS, S, DH), o1.reshape(N_HEADS, S, DH), o2.reshape(N_HEADS, S, DH), z,
                 name=tag + "_mix")
    of = _from_heads(o)
    xo = _mm(of, w["out"], tm=512, tn=1024, res=x, ctx=w["ctx"], name=tag + "_out")
    saved = (x, h, roped, z, xbk, xbv, hidk, hidv, kc, vc, (o0, o1, o2), (lse0, lse1, lse2), selT, of)
    return xo, saved


def _out_bwd(dxo, of, w, tag):
    dof = _mm(dxo, w["out"], tb=True, tm=512, tn=1024, o_lead=("heads",), ctx=w["ctx"], name=tag + "_dof")
    dw = _mm(of, dxo, ta=True, tm=512, tn=1024, out_dtype=_WIRE, ctx=w["ctx"], name=tag + "_dwout")
    w["P"].grad("out", w["out_l"], dw.reshape(4, -1, D_MODEL))
    return dof


def _split_cols(dw, n_in):
    cs = n_in // 4
    return dw[:, :n_in].reshape(D_MODEL, 4, cs).transpose(1, 0, 2).astype(_WIRE)


def _nsa_bwd(dxo, saved, g, w, cc, ss, tag):
    x, h, roped, z, xbk, xbv, hidk, hidv, kc, vc, os_, lses, selT, of = saved
    S = x.shape[0]
    do = _out_bwd(dxo, of, w, tag).reshape(NSA_G, NSA_R, S, DH)
    zg = z.reshape(3, NSA_G, NSA_R, 1, S)
    ctx = w["ctx"]
    b0 = _attn_bwd(roped, kc, vc, os_[0], do, lses[0], kind="cmp", gate=zg[0], ctx=ctx, name=tag + "_dcmp")
    b1 = _attn_bwd(roped, roped, roped, os_[1], do, lses[1], heads=(0, 24, 28), kind="sel", selT=selT, gate=zg[1],
                   ctx=ctx, name=tag + "_dslc")
    b2 = _attn_bwd(roped, roped, roped, os_[2], do, lses[2], heads=(0, 32, 36), kind="band", window=NSA_WINDOW,
                   gate=zg[2], ctx=ctx, name=tag + "_dwin")
    dxbk, dck_w1, dck_w2, dck_pe = _compress_bwd(xbk, w["ck_pe"], w["ck_w1"], w["ck_w2"], hidk, b0["dk"],
                                                 name=tag + "_dck")
    dxbv, dcv_w1, dcv_w2, dcv_pe = _compress_bwd(xbv, w["cv_pe"], w["cv_w1"], w["cv_w2"], hidv, b0["dv"],
                                                 name=tag + "_dcv")
    dk0 = _unmake_xb(dxbk, name=tag + "_dk0")
    dv0 = _unmake_xb(dxbv, name=tag + "_dv0")
    dq = _addn(b0["dq"], b1["dq"], b2["dq"], name=tag + "_dqsum").reshape(N_HEADS, S, DH)
    dz = jnp.stack([b0["dz"], b1["dz"], b2["dz"]], axis=0).reshape(3 * N_HEADS, S).T
    dgates = jnp.pad(dz, ((0, 0), (0, 2 * DH - 3 * N_HEADS))).reshape(S, 2, DH).transpose(1, 0, 2)
    dhm = jnp.concatenate([dq, dk0, dv0, b1["dk"], b1["dv"], b2["dk"], b2["dv"], dgates], axis=0)
    dproj = _from_heads(_rope(dhm, cc, -ss, NSA_ROTATED, name=tag + "_drope"))
    dh = _mm(dproj, w["w_in"], tb=True, tm=512, tn=512, ctx=w["ctx"], name=tag + "_dh")
    dw_in = _mm(h, dproj, ta=True, tm=512, tn=896, ctx=w["ctx"], name=tag + "_dwin_w")
    dx, dg = _rms_bwd(dh, x, g, dxo, name=tag + "_drms")
    P, j = w["P"], w["j"]
    P.grad("nsa_in", j, _split_cols(dw_in, NSA_IN))
    P.grad("cw1", j, dck_w1.astype(_WIRE).reshape(4, -1, dck_w1.shape[1]))
    P.grad("cw1", 2 + j, dcv_w1.astype(_WIRE).reshape(4, -1, dcv_w1.shape[1]))
    grads = dict(ck_pe=dck_pe.reshape(32, DH), ck_w2=dck_w2, cv_pe=dcv_pe.reshape(32, DH), cv_w2=dcv_w2)
    return dx, dg, grads


def _swa_fwd(x, g, w, cc, ss, tag):
    S = x.shape[0]
    h = _rms_fwd(x, g, name=tag + "_rms")
    hm = _mm(h, w["w_in"], tm=512, tn=640, o_lead=("heads",), ctx=w["ctx"], name=tag + "_in")
    roped = _rope(hm, cc, ss, SWA_ROTATED, name=tag + "_rope")
    sinks = jnp.broadcast_to(w["sinks"].reshape(SWA_G, SWA_R, 1, 1), (SWA_G, SWA_R, 8, LANES))
    o, lse = _attn_fwd(roped, roped, roped, G=SWA_G, R=SWA_R, heads=SWA_HEADS, kind="band", window=SWA_WINDOW,
                       sinks=sinks, ctx=w["ctx"], name=tag + "_attn")
    of = _from_heads(o.reshape(N_HEADS, S, DH))
    xo = _mm(of, w["out"], tm=512, tn=1024, res=x, ctx=w["ctx"], name=tag + "_out")
    return xo, (x, h, roped, sinks, o, lse, of)


def _swa_bwd(dxo, saved, g, w, cc, ss, tag):
    x, h, roped, sinks, o, lse, of = saved
    S = x.shape[0]
    do = _out_bwd(dxo, of, w, tag).reshape(SWA_G, SWA_R, S, DH)
    b = _attn_bwd(roped, roped, roped, o, do, lse, heads=SWA_HEADS, kind="band", window=SWA_WINDOW, sinks=sinks,
                  ctx=w["ctx"], name=tag + "_dattn")
    dhm = jnp.concatenate([b["dq"].reshape(N_HEADS, S, DH), b["dk"], b["dv"]], axis=0)
    dproj = _from_heads(_rope(dhm, cc, -ss, SWA_ROTATED, name=tag + "_drope"))
    dh = _mm(dproj, w["w_in"], tb=True, tm=512, tn=512, ctx=w["ctx"], name=tag + "_dh")
    dw_in = _mm(h, dproj, ta=True, tm=512, tn=640, ctx=w["ctx"], name=tag + "_dwin_w")
    dx, dg = _rms_bwd(dh, x, g, dxo, name=tag + "_drms")
    w["P"].grad("swa_in", w["j"], _split_cols(dw_in, SWA_IN))
    return dx, dg, dict(sinks=b["dsink"][:, :, 0, 0].reshape(N_HEADS))


def _fox_fwd(x, g, w, tag):
    S = x.shape[0]
    h = _rms_fwd(x, g, name=tag + "_rms")
    hm = _mm(h, w["w_in"], tm=512, tn=640, o_lead=("heads",), ctx=w["ctx"], name=tag + "_in")
    zf = jnp.pad(hm[48], ((0, 0), (0, LANES - DH)))
    bf = _pad_lanes(w["b_f"].reshape(1, N_HEADS))
    c = _fox_gate_fwd(zf, bf, name=tag + "_gate")[:, :N_HEADS]
    bias = c.T[:, None, :]
    o, lse = _attn_fwd(hm, hm, hm, G=N_HEADS, R=1, heads=FOX_HEADS, kind="causal", bias=bias, ctx=w["ctx"],
                       name=tag + "_attn")
    of = _from_heads(o.reshape(N_HEADS, S, DH))
    xo = _mm(of, w["out"], tm=512, tn=1024, res=x, ctx=w["ctx"], name=tag + "_out")
    return xo, (x, h, hm, zf, bf, bias, o, lse, of)


def _fox_bwd(dxo, saved, g, w, tag):
    x, h, hm, zf, bf, bias, o, lse, of = saved
    S = x.shape[0]
    do = _out_bwd(dxo, of, w, tag).reshape(N_HEADS, 1, S, DH)
    b = _attn_bwd(hm, hm, hm, o, do, lse, heads=FOX_HEADS, kind="causal", bias=bias, ctx=w["ctx"], name=tag + "_dattn")
    dzf, db = _fox_gate_bwd(zf, bf, _pad_lanes(b["dc"].reshape(N_HEADS, S).T), name=tag + "_dgate")
    dgates = dzf.reshape(S, 2, DH).transpose(1, 0, 2)
    dhm = jnp.concatenate([b["dq"].reshape(N_HEADS, S, DH), b["dk"], b["dv"], dgates], axis=0)
    dproj = _from_heads(dhm)
    dh = _mm(dproj, w["w_in"], tb=True, tm=512, tn=512, ctx=w["ctx"], name=tag + "_dh")
    dw_in = _mm(h, dproj, ta=True, tm=512, tn=640, ctx=w["ctx"], name=tag + "_dwin_w")
    dx, dg = _rms_bwd(dh, x, g, dxo, name=tag + "_drms")
    w["P"].grad("fox_in", w["j"], _split_cols(dw_in, FOX_IN))
    return dx, dg, dict(b_f=db[0, :N_HEADS])


GROUPS = {
    "gu": (("ffn1_w_gu", "ffn2_w_gu"), 2),
    "down": (("ffn1_w_down", "ffn2_w_down"), 1),
    "out": (("nsa_w_out", "swa_w_out", "fox_w_out"), 1),
    "cw1": (("nsa_ck_w1", "nsa_cv_w1"), 1),
    "nsa_in": (("nsa_w_in",), 2),
    "swa_in": (("swa_w_in",), 2),
    "fox_in": (("fox_w_in",), 2),
}
OUT_SLAB = {0: 0, 3: 1, 1: 2, 2: 3}


def _pieces_in_order():
    chunks = []
    for i in range(DEPTH):
        kind, j = i % 3, i // 3
        chunks.append([("gu", i), ("down", i)])
        if kind == 0:
            chunks.append([("nsa_in", j), ("cw1", j), ("cw1", 2 + j), ("out", OUT_SLAB[i])])
        else:
            chunks.append([("swa_in" if kind == 1 else "fox_in", j), ("out", OUT_SLAB[i])])
        chunks.append([("gu", DEPTH + i), ("down", DEPTH + i)])
    return chunks


def _consumer_layout(group, F):
    _, rows, C = F.shape
    if group == "gu":
        return F
    if GROUPS[group][1] == 1:
        return F.reshape(4 * rows, C)
    w = F.transpose(1, 0, 2).reshape(rows, 4 * C)
    pad = {"nsa_in": NSA_IN_PAD, "swa_in": SWA_IN, "fox_in": FOX_IN_PAD}[group] - 4 * C
    return jnp.pad(w, ((0, 0), (0, pad)))


class _Given:
    def __init__(self, pieces, small):
        self.pieces, self.small, self.ctx, self.grads = pieces, small, None, {}

    def weight(self, group, l):
        return _consumer_layout(group, self.pieces[group, l])

    def grad(self, group, l, G):
        self.grads[group, l] = G


class _MixerWeights(dict):
    def __init__(self, P, pieces, **given):
        super().__init__(P=P, ctx=P.ctx, **given)
        self.pieces = pieces

    def __missing__(self, key):
        self[key] = self["P"].weight(*self.pieces[key])
        return self[key]


def _mixer_weights(P, i):
    kind, j = i % 3, i // 3
    out = {"out": ("out", OUT_SLAB[i])}
    if kind == 0:
        small = {k: P.small["nsa_" + k][j] for k in ("ck_pe", "ck_w2", "cv_pe", "cv_w2")}
        return _MixerWeights(P, dict(out, w_in=("nsa_in", j), ck_w1=("cw1", j), cv_w1=("cw1", 2 + j)), j=j,
                             out_l=OUT_SLAB[i], **small)
    if kind == 1:
        return _MixerWeights(P, dict(out, w_in=("swa_in", j)), j=j, out_l=OUT_SLAB[i], sinks=P.small["swa_sinks"][j])
    return _MixerWeights(P, dict(out, w_in=("fox_in", j)), j=j, out_l=OUT_SLAB[i], b_f=P.small["fox_b_f"][j])


def _local_step(x, tgt, P):
    S = x.shape[0]
    cc, ss = _rope_tables(S)
    sm = P.small
    saved = []
    for i in range(DEPTH):
        kind = i % 3
        x, s1 = _ffn_fwd(x, sm["ffn1_norm"][i], P, i, f"l{i}f1")
        mw = _mixer_weights(P, i)
        if kind == 0:
            x, s2 = _nsa_fwd(x, sm["mix_norm"][i], mw, cc, ss, f"l{i}nsa")
        elif kind == 1:
            x, s2 = _swa_fwd(x, sm["mix_norm"][i], mw, cc, ss, f"l{i}swa")
        else:
            x, s2 = _fox_fwd(x, sm["mix_norm"][i], mw, f"l{i}fox")
        x, s3 = _ffn_fwd(x, sm["ffn2_norm"][i], P, DEPTH + i, f"l{i}f2")
        saved.append((s1, mw, s2, s3))
    loss, dx, d_final = _loss_head(x, sm["final_norm"], tgt, name="loss_head")

    norms = {k: [None] * DEPTH for k in ("ffn1_norm", "mix_norm", "ffn2_norm")}
    mix = {}
    for i in reversed(range(DEPTH)):
        kind, j = i % 3, i // 3
        s1, mw, s2, s3 = saved[i]
        dx, norms["ffn2_norm"][i] = _ffn_bwd(dx, s3, sm["ffn2_norm"][i], P, DEPTH + i, f"l{i}f2")
        if kind == 0:
            dx, dg, gm = _nsa_bwd(dx, s2, sm["mix_norm"][i], mw, cc, ss, f"l{i}nsa")
            pre = "nsa_"
        elif kind == 1:
            dx, dg, gm = _swa_bwd(dx, s2, sm["mix_norm"][i], mw, cc, ss, f"l{i}swa")
            pre = "swa_"
        else:
            dx, dg, gm = _fox_bwd(dx, s2, sm["mix_norm"][i], mw, f"l{i}fox")
            pre = "fox_"
        norms["mix_norm"][i] = dg
        for k, val in gm.items():
            mix.setdefault(pre + k, {})[j] = val
        dx, norms["ffn1_norm"][i] = _ffn_bwd(dx, s1, sm["ffn1_norm"][i], P, i, f"l{i}f1")
    small = {k: jnp.stack(v, axis=0) for k, v in norms.items()}
    small.update({k: jnp.stack([d[j] for j in sorted(d)], axis=0) for k, d in mix.items()})
    small["final_norm"] = d_final
    return loss, dx, small


def _sum_slots(own, recv, me, into, row0, *, name):
    _, R, C = recv.shape
    tr = _row_tile(math.gcd(R, row0), C)
    blk0 = row0 // tr

    def body(_, g_ref, r_ref, __, o_ref):
        acc = g_ref[...].astype(F32) + r_ref[0].astype(F32)
        acc = acc + r_ref[1].astype(F32)
        o_ref[...] = acc + r_ref[2].astype(F32)

    spec = pltpu.PrefetchScalarGridSpec(
        num_scalar_prefetch=1, grid=(R // tr,),
        in_specs=[pl.BlockSpec((None, tr, C), lambda i, me_ref: (me_ref[0], i, 0)),
                  pl.BlockSpec((3, tr, C), lambda i, me_ref: (0, i, 0)), _ANY],
        out_specs=pl.BlockSpec((tr, C), lambda i, me_ref: (blk0 + i, 0)))
    return pl.pallas_call(body, grid_spec=spec, out_shape=jax.ShapeDtypeStruct(into.shape, F32),
                          input_output_aliases={3: 0}, compiler_params=_params(), name=name)(me, own, recv, into)


def _swap_sibling(parts):
    n = len(parts)

    def body(*refs):
        srcs, outs = refs[:n], refs[n:2 * n]
        send_sems, recv_sems = refs[2 * n:]
        x, y, c = lax.axis_index("x"), lax.axis_index("y"), lax.axis_index("c")
        cps = [pltpu.make_async_remote_copy(src_ref=srcs[g], dst_ref=outs[g], send_sem=send_sems.at[g],
                                            recv_sem=recv_sems.at[g], device_id=(x, y, 1 - c), device_id_type=MESH)
               for g in range(n)]
        for cp in cps:
            cp.start()
        for cp in cps:
            cp.wait()

    return pl.pallas_call(
        body, out_shape=tuple(jax.ShapeDtypeStruct(p.shape, p.dtype) for p in parts),
        in_specs=[_ANY] * n, out_specs=(_ANY,) * n,
        scratch_shapes=[pltpu.SemaphoreType.DMA((n,)), pltpu.SemaphoreType.DMA((n,))],
        compiler_params=pltpu.CompilerParams(has_side_effects=True), name="swap_core_grads")(*parts)


def _flip(coord, bit):
    return 1 - coord if bit else coord


def _allreduce_small(v):
    n, C = v.shape

    def body(v_ref, o_ref, buf, send_sems, recv_sems):
        x, y, c = lax.axis_index("x"), lax.axis_index("y"), lax.axis_index("c")
        me = 4 * x + 2 * y + c
        buf[me] = v_ref[...]
        peers = [(_flip(x, (j >> 2) & 1), _flip(y, (j >> 1) & 1), _flip(c, j & 1)) for j in range(1, 8)]
        sends = [pltpu.make_async_remote_copy(src_ref=v_ref, dst_ref=buf.at[me], send_sem=send_sems.at[j],
                                              recv_sem=recv_sems.at[j], device_id=peer, device_id_type=MESH)
                 for j, peer in enumerate(peers)]
        for cp in sends:
            cp.start()
        for j, (px, py, pc) in enumerate(peers):
            pltpu.make_async_remote_copy(src_ref=v_ref, dst_ref=buf.at[4 * px + 2 * py + pc], send_sem=send_sems.at[j],
                                         recv_sem=recv_sems.at[j], device_id=(px, py, pc),
                                         device_id_type=MESH).wait_recv()
        for cp in sends:
            cp.wait_send()
        acc = buf[0]
        for d in range(1, 8):
            acc = acc + buf[d]
        o_ref[...] = acc

    return pl.pallas_call(
        body, out_shape=jax.ShapeDtypeStruct((n, C), F32),
        in_specs=[pl.BlockSpec(memory_space=pltpu.VMEM)], out_specs=pl.BlockSpec(memory_space=pltpu.VMEM),
        scratch_shapes=[pltpu.VMEM((8, n, C), F32), pltpu.SemaphoreType.DMA((7,)), pltpu.SemaphoreType.DMA((7,))],
        compiler_params=pltpu.CompilerParams(has_side_effects=True), name="allreduce_small")(v)


def _adamw(w, m, v, gs, *, row0=0, name):
    shape = w.shape
    w3, m3, v3 = (a.reshape((-1,) + a.shape[-2:]) for a in (w, m, v))
    g2 = [_rows2d(g) for g in gs]
    L, rows, C = w3.shape
    tr = _row_tile(math.gcd(rows, row0), C, budget=1024 * 1024)
    ng = len(g2)
    blk0, nb = row0 // tr, rows // tr

    def body(*refs):
        w_ref, m_ref, v_ref = refs[:3]
        g = refs[3][...]
        for r in refs[4:3 + ng]:
            g = g + r[...]
        g_ref, d_ref, nm_ref, nv_ref = refs[3 + ng:]
        mn = B1 * m_ref[...] + (1.0 - B1) * g
        vn = B2 * v_ref[...] + (1.0 - B2) * (g * g)
        m_hat = mn / (1.0 - B1 ** STEP)
        v_hat = vn / (1.0 - B2 ** STEP)
        g_ref[...] = g
        d_ref[...] = -LR * (m_hat / (jnp.sqrt(v_hat) + EPS) + WD * w_ref[...])
        nm_ref[...] = mn
        nv_ref[...] = vn

    spec = pl.BlockSpec((None, tr, C), lambda l, i: (l, i, 0))
    gspec = pl.BlockSpec((tr, C), lambda l, i: (blk0 + l * nb + i, 0))
    outs = pl.pallas_call(body, out_shape=tuple(jax.ShapeDtypeStruct((L, rows, C), F32) for _ in range(4)),
                          grid=(L, nb), in_specs=[spec] * 3 + [gspec] * ng, out_specs=(spec,) * 4,
                          compiler_params=_params(), name=name)(w3, m3, v3, *g2)
    return tuple(o.reshape(shape) for o in outs)


def _small_layout(shapes):
    offs, off = {}, 0
    for k in REPLICATED:
        n = int(np.prod(shapes[k]))
        offs[k] = (off, n)
        off += -(-n // LANES) * LANES
    return offs, off


def _pack_small(d, shapes):
    offs, total = _small_layout(shapes)
    parts = []
    for k in REPLICATED:
        n = offs[k][1]
        parts.append(jnp.pad(d[k].reshape(-1).astype(F32), (0, -(-n // LANES) * LANES - n)))
    rows = -(-(total // LANES) // 8) * 8
    return jnp.pad(jnp.concatenate(parts), (0, rows * LANES - total)).reshape(rows, LANES)


def _unpack_small(a, shapes):
    offs, _ = _small_layout(shapes)
    flat = a.reshape(-1)
    return {k: flat[offs[k][0]:offs[k][0] + offs[k][1]].reshape(shapes[k]) for k in REPLICATED}


def _group_shards(w):
    shards = []
    for members, _ in GROUPS.values():
        s = jnp.concatenate([w[k].astype(_WIRE) for k in members], axis=0)
        shards.append(s.reshape(s.shape[0], 2, s.shape[1] // 2, s.shape[2]))
    return shards


class _Exchanged:
    def __init__(self, w):
        self.small = {k: w[k] for k in REPLICATED}
        self.ctx = _Sched()
        shards = dict(zip(GROUPS, _group_shards(w)))
        self.gather, self.cache, self.scatter = {}, {}, {}
        chunks = _pieces_in_order()
        self.me = jnp.reshape(2 * lax.axis_index("x") + lax.axis_index("y"), (1,)).astype(jnp.int32)
        first = {(g, l): _gather_task1(shards[g], l, _place_own(shards[g], l, self.me, name=f"own_{g}{l}"))
                 for chunk in chunks for g, l in chunk}
        self.gather = {p: _gather_task2(task) for p, task in first.items()}
        pieces = [p for chunk in chunks for p in chunk]
        for n, p in enumerate(pieces):
            self.ctx.push(first[p])
            if n:
                self.ctx.push(self.gather[pieces[n - 1]])
        self.ctx.push(self.gather[pieces[-1]])

    def weight(self, group, l):
        if (group, l) not in self.cache:
            task = self.gather[group, l]
            self.ctx.finish(task)
            F = task.result()
            self.cache[group, l] = _consumer_layout(group, F.reshape(4, -1, F.shape[-1]))
        return self.cache[group, l]

    def grad(self, group, l, G):
        self.scatter[group, l] = (G, self.ctx.push(_scatter_task(G)))

    def partial_sums(self, w):
        self.ctx.finish()
        parts = []
        for group, (members, _) in GROUPS.items():
            rows, C = w[members[0]].shape[1:]
            n = sum(w[k].shape[0] for k in members)
            part = jnp.zeros((n * rows, C), F32)
            for l in range(n):
                G, task = self.scatter[group, l]
                part = _sum_slots(G, task.result(), self.me, part, l * rows, name=f"sum_{group}{l}")
            parts.append(part)
        return parts


def _reduce_and_update(w, m, v, parts, small):
    others = _swap_sibling(parts)
    small_shapes = {k: w[k].shape for k in REPLICATED}
    g_small = _unpack_small(_allreduce_small(_pack_small(small, small_shapes)), small_shapes)

    out_g, out_d, out_m, out_v = {}, {}, {}, {}
    for (members, _), part, other in zip(GROUPS.values(), parts, others):
        row0 = 0
        for k in members:
            out_g[k], out_d[k], out_m[k], out_v[k] = _adamw(w[k], m[k], v[k], [part, other], row0=row0,
                                                            name="adamw_" + k)
            row0 += w[k].shape[0] * w[k].shape[1]
    sm = _adamw(_pack_small(w, small_shapes), _pack_small(m, small_shapes), _pack_small(v, small_shapes),
                [_pack_small(g_small, small_shapes)], name="adamw_small")
    for d, packed in zip((out_g, out_d, out_m, out_v), sm):
        d.update(_unpack_small(packed, small_shapes))
    return out_g, out_d, out_m, out_v


def kernel(x, ffn1_norm, ffn1_w_gu, ffn1_w_down, mix_norm, ffn2_norm, ffn2_w_gu, ffn2_w_down, nsa_w_in, nsa_ck_pe, nsa_ck_w1, nsa_ck_w2, nsa_cv_pe, nsa_cv_w1, nsa_cv_w2, nsa_w_out, swa_w_in, swa_sinks, swa_w_out, fox_w_in, fox_b_f, fox_w_out, final_norm, loss_target, m_ffn1_norm, m_ffn1_w_gu, m_ffn1_w_down, m_mix_norm, m_ffn2_norm, m_ffn2_w_gu, m_ffn2_w_down, m_nsa_w_in, m_nsa_ck_pe, m_nsa_ck_w1, m_nsa_ck_w2, m_nsa_cv_pe, m_nsa_cv_w1, m_nsa_cv_w2, m_nsa_w_out, m_swa_w_in, m_swa_sinks, m_swa_w_out, m_fox_w_in, m_fox_b_f, m_fox_w_out, m_final_norm, v_ffn1_norm, v_ffn1_w_gu, v_ffn1_w_down, v_mix_norm, v_ffn2_norm, v_ffn2_w_gu, v_ffn2_w_down, v_nsa_w_in, v_nsa_ck_pe, v_nsa_ck_w1, v_nsa_ck_w2, v_nsa_cv_pe, v_nsa_cv_w1, v_nsa_cv_w2, v_nsa_w_out, v_swa_w_in, v_swa_sinks, v_swa_w_out, v_fox_w_in, v_fox_b_f, v_fox_w_out, v_final_norm):
    args = dict(locals())
    w = {k: args[k] for k in WEIGHTS}
    m = {k: args["m_" + k] for k in WEIGHTS}
    v = {k: args["v_" + k] for k in WEIGHTS}
    P = _Exchanged(w)
    loss_part, dx, small = _local_step(x[0], loss_target[0], P)
    loss = lax.psum(loss_part, ("x", "y", "c"))
    out_g, out_d, out_m, out_v = _reduce_and_update(w, m, v, P.partial_sums(w), small)
    return (loss, dx[None], *[out_g[k] for k in WEIGHTS], *[out_d[k] for k in WEIGHTS],
            *[out_m[k] for k in WEIGHTS], *[out_v[k] for k in WEIGHTS])
```

```python
import math

import numpy as np
import jax
import jax.numpy as jnp
from jax import lax
from jax.experimental import pallas as pl
from jax.experimental.pallas import tpu as pltpu

F32 = jnp.float32
_CDT = jnp.bfloat16
_WIRE = jnp.bfloat16
_VMEM_LIMIT = 56 * 1024 * 1024

D_MODEL = 1024
DEPTH = 4
DH = 64
N_HEADS = 16
RMS_EPS = 1e-6
NEG = -1e30
SCALE = DH ** -0.5
BQ = 256
LANES = 128
NSA_G, NSA_R = 4, 4
NSA_WINDOW = 512
NSA_TOPK = 16
NSA_BONUS = 1e4
SWA_G, SWA_R = 2, 8
SWA_WINDOW = 128
NSA_ROTATED = tuple(hd < 16 or (hd < 40 and (hd - 16) % 8 < 4) for hd in range(42))
NSA_IN, NSA_IN_PAD = 2608, 2688
SWA_IN = 1280
SWA_ROTATED = (True,) * 18 + (False,) * 2
SWA_HEADS = (0, 16, 18)
FOX_IN, FOX_IN_PAD = 3088, 3200
FOX_HEADS = (0, 16, 32)
LR, B1, B2, EPS, WD, STEP = 0.001, 0.9, 0.999, 1e-08, 0.01, 10
MESH = pl.DeviceIdType.MESH

REPLICATED = ["ffn1_norm", "mix_norm", "ffn2_norm", "nsa_ck_pe", "nsa_ck_w2", "nsa_cv_pe", "nsa_cv_w2",
              "swa_sinks", "fox_b_f", "final_norm"]
WEIGHTS = ['ffn1_norm', 'ffn1_w_gu', 'ffn1_w_down', 'mix_norm', 'ffn2_norm', 'ffn2_w_gu', 'ffn2_w_down',
           'nsa_w_in', 'nsa_ck_pe', 'nsa_ck_w1', 'nsa_ck_w2', 'nsa_cv_pe', 'nsa_cv_w1', 'nsa_cv_w2', 'nsa_w_out',
           'swa_w_in', 'swa_sinks', 'swa_w_out', 'fox_w_in', 'fox_b_f', 'fox_w_out', 'final_norm']


def _params(**kw):
    return pltpu.CompilerParams(vmem_limit_bytes=_VMEM_LIMIT, **kw)


def _dot(a, b):
    return lax.dot_general(a, b, (((1,), (0,)), ((), ())), preferred_element_type=F32)


def _dot_nt(a, b):
    return lax.dot_general(a, b, (((1,), (1,)), ((), ())), preferred_element_type=F32)


def _dot_tn(a, b):
    return lax.dot_general(a, b, (((0,), (0,)), ((), ())), preferred_element_type=F32)


def _split3(x):
    hi = x.astype(jnp.bfloat16)
    r1 = x - hi.astype(F32)
    mid = r1.astype(jnp.bfloat16)
    lo = (r1 - mid.astype(F32)).astype(jnp.bfloat16)
    return hi, mid, lo


def _dot_exact(x, p):
    hi, mid, lo = _split3(x)
    return _dot(hi, p) + _dot(mid, p) + _dot(lo, p)


def _dot_exact_left(p, x):
    hi, mid, lo = _split3(x)
    return _dot(p, hi) + _dot(p, mid) + _dot(p, lo)


_ANY = pl.BlockSpec(memory_space=pl.ANY)


def _chip_peers():
    x, y, c = lax.axis_index("x"), lax.axis_index("y"), lax.axis_index("c")
    return x, y, c, [(1 - x, y), (x, 1 - y), (1 - x, 1 - y)]


class _Job:
    def __init__(self, ins, outs, nsem, copies, cost, alias):
        self.ins, self.outs, self.nsem, self.copies, self.cost, self.alias = ins, outs, nsem, copies, cost, alias
        self.results = None

    def inputs(self):
        return self.ins


class _Task:
    def __init__(self, src, out, parts, init=None):
        self.src, self.out, self.parts, self.init, self.done, self.jobs = src, out, parts, init, 0, []

    def result(self):
        return self.jobs[-1].results[0]

    def complete(self):
        return self.done == len(self.parts) and self.jobs[-1].results is not None

    def ready(self):
        if isinstance(self.src, _Task) and not self.src.complete():
            return False
        return not self.jobs or self.jobs[-1].results is not None

    def next_job(self, n):
        parts = self.parts[self.done:self.done + n]
        in_place = isinstance(self.src, _Task)
        ins = [] if in_place else [self.src]
        prev = self.jobs[-1].results[0] if self.jobs else (self.src.result() if in_place else self.init)
        alias = {}
        if prev is not None:
            alias = {len(ins): 0}
            ins = ins + [prev]

        def copies(in_refs, out_refs, sems):
            src = out_refs[0] if in_place else in_refs[0]
            out, first = [], 0
            for build, nsem, _ in parts:
                out += build(src, out_refs[0], sems, first)
                first += nsem
            return out

        job = _Job(ins, [self.out], sum(p[1] for p in parts), copies, sum(p[2] for p in parts), alias)
        self.done += n
        self.jobs.append(job)
        return job


def _pcall(body, *, out_shape, in_specs, out_specs, name, grid=(), scratch_shapes=(), aliases=None, jobs=(), ins):
    single = not isinstance(out_shape, (tuple, list))
    out_shape = (out_shape,) if single else tuple(out_shape)
    out_specs = (out_specs,) if single else tuple(out_specs)
    n_in, n_out, n_scr = len(ins), len(out_shape), len(scratch_shapes)
    side_ins = [a for j in jobs for a in j.inputs()]
    side_outs = [o for j in jobs for o in j.outs]
    aliases = dict(aliases or {})
    i0, o0 = n_in, n_out
    for j in jobs:
        for a, b in j.alias.items():
            aliases[i0 + a] = o0 + b
        i0 += len(j.ins)
        o0 += len(j.outs)

    def body2(*refs):
        in_refs, s_in = refs[:n_in], refs[n_in:n_in + len(side_ins)]
        r = n_in + len(side_ins)
        out_refs, s_out = refs[r:r + n_out], refs[r + n_out:r + n_out + len(side_outs)]
        r += n_out + len(side_outs)
        scr, sems = refs[r:r + n_scr], refs[r + n_scr:]
        def descriptors():
            out, a, b = [], 0, 0
            for j, sem in zip(jobs, sems):
                out += j.copies(s_in[a:a + len(j.ins)], s_out[b:b + len(j.outs)], sem)
                a += len(j.ins)
                b += len(j.outs)
            return out

        def at_step(steps, fn):
            cond = None
            for d, s in enumerate(steps):
                hit = pl.program_id(d) == s
                cond = hit if cond is None else jnp.logical_and(cond, hit)
            if cond is None:
                fn()
            else:
                pl.when(cond)(fn)

        def start_all():
            for cp in descriptors():
                cp.start()

        def wait_all():
            for cp in descriptors():
                cp.wait()

        if jobs:
            at_step([0] * len(grid), start_all)
        body(*in_refs, *out_refs, *scr)
        if jobs:
            at_step([n - 1 for n in grid], wait_all)

    res = pl.pallas_call(
        body2, out_shape=out_shape + tuple(side_outs), grid=grid, in_specs=list(in_specs) + [_ANY] * len(side_ins),
        out_specs=out_specs + (_ANY,) * len(side_outs),
        scratch_shapes=list(scratch_shapes) + [pltpu.SemaphoreType.DMA((j.nsem,)) for j in jobs],
        input_output_aliases=aliases,
        compiler_params=_params(has_side_effects=True) if jobs else _params(), name=name)(*ins, *side_ins)
    b = n_out
    for j in jobs:
        j.results = list(res[b:b + len(j.outs)])
        b += len(j.outs)
    return res[0] if single else tuple(res[:n_out])


def _comm_call(jobs, name):
    def body():
        pass

    _pcall(body, out_shape=(), in_specs=[], out_specs=(), name=name, jobs=jobs, ins=[])


def _place_own(shard, l, me, *, name):
    _, _, h, C = shard.shape

    def body(_, s_ref, o_ref):
        o_ref[...] = s_ref[...]

    spec = pltpu.PrefetchScalarGridSpec(
        num_scalar_prefetch=1, grid=(2,),
        in_specs=[pl.BlockSpec((None, None, h, C), lambda i, me_ref: (l, i, 0, 0))],
        out_specs=pl.BlockSpec((None, None, h, C), lambda i, me_ref: (me_ref[0], i, 0, 0)))
    return pl.pallas_call(body, grid_spec=spec, out_shape=jax.ShapeDtypeStruct((4, 2, h, C), shard.dtype),
                          compiler_params=_params(), name=name)(me, shard)


def _gather_task1(shard, l, own):
    _, _, h, C = shard.shape
    mb = 3 * h * C * jnp.dtype(_WIRE).itemsize / 1e6
    n = _row_splits(h, ICI_US_PER_MB * mb)
    hr = h // n

    def rows_to_peers(r):
        def build(src, dst, sems, s0):
            x, y, c, chips = _chip_peers()
            me, rows = 2 * x + y, pl.ds(r * hr, hr)
            return [pltpu.make_async_remote_copy(src_ref=src.at[l, c, rows], dst_ref=dst.at[me, c, rows],
                                                 send_sem=sems.at[s0 + 2 * j], recv_sem=sems.at[s0 + 2 * j + 1],
                                                 device_id=(px, py, c), device_id_type=MESH)
                    for j, (px, py) in enumerate(chips)]
        return build

    parts = [(rows_to_peers(r), 6, ICI_US_PER_MB * mb / n) for r in range(n)]
    return _Task(shard, jax.ShapeDtypeStruct((4, 2, h, C), shard.dtype), parts, init=own)


def _gather_task2(task1):
    _, _, h, C = task1.out.shape
    mb = h * C * jnp.dtype(_WIRE).itemsize / 1e6

    def forward(j):
        def build(_, dst, sems, s0):
            x, y, c, chips = _chip_peers()
            px, py = chips[j]
            landed = dst.at[2 * px + py, c]
            return [pltpu.make_async_remote_copy(src_ref=landed, dst_ref=landed, send_sem=sems.at[s0],
                                                 recv_sem=sems.at[s0 + 1], device_id=(x, y, 1 - c),
                                                 device_id_type=MESH)]
        return build

    return _Task(task1, task1.out, [(forward(j), 2, 1.0 + D2D_US_PER_MB * mb) for j in range(3)])


def _scatter_task(G):
    mb = 3 * G.shape[1] * G.shape[2] * jnp.dtype(_WIRE).itemsize / 1e6
    n = _row_splits(G.shape[1], ICI_US_PER_MB * mb)
    rr = G.shape[1] // n

    def rows_to_peers(r):
        def build(src, dst, sems, s0):
            _, _, c, chips = _chip_peers()
            rows = pl.ds(r * rr, rr)
            return [pltpu.make_async_remote_copy(src_ref=src.at[2 * px + py, rows], dst_ref=dst.at[j, rows],
                                                 send_sem=sems.at[s0 + 2 * j], recv_sem=sems.at[s0 + 2 * j + 1],
                                                 device_id=(px, py, c), device_id_type=MESH)
                    for j, (px, py) in enumerate(chips)]
        return build

    parts = [(rows_to_peers(r), 6, ICI_US_PER_MB * mb / n) for r in range(n)]
    return _Task(G, jax.ShapeDtypeStruct((3,) + G.shape[1:], G.dtype), parts)


def _row_splits(rows, cost):
    n = 8
    while n > 1 and (rows % (16 * n) or cost / n < PART_US):
        n //= 2
    return n


CARRIER_US = {
    "f_gu": 48, "f_down": 20, "f_dact": 42, "f_dwgu": 42, "f_dh": 46,
    "nsa_in": 25, "nsa_cmp": 86, "nsa_slc": 122, "nsa_win": 74, "nsa_dcmp": 58, "nsa_dslc": 195, "nsa_dwin": 108,
    "nsa_dh": 26, "nsa_dwin_w": 25,
    "swa_in": 16, "swa_attn": 74, "swa_dattn": 74, "swa_dh": 17, "swa_dwin_w": 18,
    "fox_in": 34, "fox_attn": 104, "fox_dattn": 161, "fox_dh": 30, "fox_dwin_w": 30,
}
ICI_US_PER_MB = 15.0
PART_US = 12.0
D2D_US_PER_MB = 2.3
LOCAL_US_PER_MB = 1.5


class _Sched:
    def __init__(self):
        self.queue, self.credit, self.n_flush = [], 0.0, 0

    def push(self, task):
        self.queue.append(task)
        return task

    def _jobs(self, fits):
        jobs = []
        while self.queue and self.queue[0].ready():
            task, n = self.queue[0], 0
            while task.done + n < len(task.parts) and fits(task.parts[task.done + n][2]):
                n += 1
            if n:
                jobs.append(task.next_job(n))
            if task.done < len(task.parts):
                break
            self.queue.pop(0)
        return jobs

    def take(self, name):
        kind = name[2:].split("_", 1)
        key = ("f" if kind[0] in ("f1", "f2") else kind[0]) + "_" + kind[1]
        self.credit = min(self.credit, 0.0) + CARRIER_US.get(key, 0.0)

        def fits(cost):
            if self.credit < 0.5 * cost:
                return False
            self.credit -= cost
            return True

        return self._jobs(fits)

    def finish(self, task=None):
        while self.queue and (task is None or not task.complete()):
            jobs = []
            while self.queue and self.queue[0].ready():
                head = self.queue.pop(0)
                jobs.append(head.next_job(len(head.parts) - head.done))
                if head is task:
                    break
            assert jobs, "the task at the head of the queue waits for one that was never queued"
            _comm_call(jobs, name=f"exchange_{self.n_flush}")
            self.n_flush += 1


def _mm(a, b, *, ta=False, tb=False, tm, tn, b_lead=(), into=None, o_lead=(), out_dtype=F32, res=None, alpha=1.0,
        ctx=None, name):
    M = a.shape[1] if ta else a.shape[0]
    K = a.shape[0] if ta else a.shape[1]
    bk, bn = (b.shape[-1], b.shape[-2]) if tb else (b.shape[-2], b.shape[-1])
    j_lead, k_lead = "j" in b_lead, "k" in b_lead
    N = bn * (b.shape[b_lead.index("j")] if j_lead else 1)
    nk = b.shape[b_lead.index("k")] if k_lead else 1
    tk = K // nk
    assert tk == bk, (name, K, nk, bk)
    tm = min(tm, M)
    tn = bn if j_lead else min(tn, N)
    assert M % tm == 0 and N % tn == 0, (name, M, N, tm, tn)
    nb, no = len(b_lead), len(o_lead)

    def pick(lead, j, k):
        return tuple(j if t == "j" else k if t == "k" else t for t in lead)

    a_spec = pl.BlockSpec((tk, tm), lambda i, j, k: (k, i)) if ta else pl.BlockSpec((tm, tk), lambda i, j, k: (i, k))
    if tb:
        b_spec = pl.BlockSpec((None,) * nb + (tn, tk),
                              lambda i, j, k: pick(b_lead, j, k) + (0 if j_lead else j, 0 if k_lead else k))
    else:
        b_spec = pl.BlockSpec((None,) * nb + (tk, tn),
                              lambda i, j, k: pick(b_lead, j, k) + (0 if k_lead else k, 0 if j_lead else j))
    r_spec = pl.BlockSpec((tm, tn), lambda i, j, k: (i, j))
    if o_lead == ("heads",):
        o_spec = pl.BlockSpec((tn // DH, tm, DH), lambda i, j, k: (j, i, 0))
    else:
        o_spec = pl.BlockSpec((None,) * no + (tm, tn),
                              lambda i, j, k: pick(o_lead, j, k) + (i, 0 if "j" in o_lead else j))
    dn = (((0 if ta else 1,), (1 if tb else 0,)), ((), ()))
    has_res, has_into = res is not None, into is not None
    if has_into:
        out_dtype = into.dtype

    def body(*refs):
        a_ref, b_ref = refs[0], refs[1]
        r_ref = refs[2] if has_res else None
        o_ref = refs[2 + has_res + has_into]
        prod = lax.dot_general(a_ref[...].astype(_CDT), b_ref[...].astype(_CDT), dn, preferred_element_type=F32)

        def finish(acc):
            if alpha != 1.0:
                acc = acc * alpha
            if has_res:
                acc = r_ref[...] + acc
            if o_lead == ("heads",):
                for hd in range(tn // DH):
                    o_ref[hd] = acc[:, hd * DH:(hd + 1) * DH].astype(out_dtype)
            else:
                o_ref[...] = acc.astype(out_dtype)

        if nk == 1:
            finish(prod)
        else:
            acc_ref = refs[-1]
            k = pl.program_id(2)

            @pl.when(k == 0)
            def _():
                acc_ref[...] = prod

            @pl.when(k != 0)
            def _():
                acc_ref[...] += prod

            @pl.when(k == nk - 1)
            def _():
                finish(acc_ref[...])

    ins, specs = [a, b], [a_spec, b_spec]
    if has_res:
        ins.append(res)
        specs.append(r_spec)
    aliases = {}
    if has_into:
        aliases = {len(ins): 0}
        ins.append(into)
        specs.append(_ANY)
        out_shape = jax.ShapeDtypeStruct(into.shape, into.dtype)
    elif o_lead == ("j",):
        out_shape = jax.ShapeDtypeStruct((N // tn, M, tn), out_dtype)
    elif o_lead == ("heads",):
        out_shape = jax.ShapeDtypeStruct((N // DH, M, DH), out_dtype)
    else:
        assert not o_lead
        out_shape = jax.ShapeDtypeStruct((M, N), out_dtype)
    scratch = [pltpu.VMEM((tm, tn), F32)] if nk > 1 else []
    return _pcall(body, out_shape=out_shape, grid=(M // tm, N // tn, nk), in_specs=specs, out_specs=o_spec,
                  scratch_shapes=scratch, aliases=aliases, name=name, jobs=ctx.take(name) if ctx else (), ins=ins)


def _rows2d(a):
    return a.reshape(-1, a.shape[-1])


def _row_tile(rows, cols, itemsize=4, budget=2 * 1024 * 1024):
    t = rows
    while t % 2 == 0 and t * cols * itemsize > budget and (t // 2) % 8 == 0:
        t //= 2
    return t


def _addn(*xs, name):
    shape = xs[0].shape
    x2 = [_rows2d(x) for x in xs]
    R, C = x2[0].shape
    tr = _row_tile(R, C)

    def body(*refs):
        acc = refs[0][...]
        for r in refs[1:-1]:
            acc = acc + r[...]
        refs[-1][...] = acc

    spec = pl.BlockSpec((tr, C), lambda i: (i, 0))
    out = pl.pallas_call(body, out_shape=jax.ShapeDtypeStruct((R, C), F32), grid=(R // tr,),
                         in_specs=[spec] * len(x2), out_specs=spec, compiler_params=_params(), name=name)(*x2)
    return out.reshape(shape)


def _rms_fwd(x, g, *, name):
    S, D = x.shape
    tr = 256

    def body(x_ref, g_ref, h_ref):
        xv = x_ref[...]
        rstd = lax.rsqrt(jnp.mean(xv * xv, axis=-1, keepdims=True) + RMS_EPS)
        h_ref[...] = (xv * rstd * g_ref[...]).astype(_CDT)

    return pl.pallas_call(body, out_shape=jax.ShapeDtypeStruct((S, D), _CDT), grid=(S // tr,),
                          in_specs=[pl.BlockSpec((tr, D), lambda i: (i, 0)), pl.BlockSpec((1, D), lambda i: (0, 0))],
                          out_specs=pl.BlockSpec((tr, D), lambda i: (i, 0)), compiler_params=_params(),
                          name=name)(x, g.reshape(1, D))


def _rms_bwd(dh, x, g, dres, *, name):
    S, D = x.shape
    tr = 256

    def body(dh_ref, x_ref, g_ref, dres_ref, dx_ref, dg_ref):
        xv = x_ref[...]
        rstd = lax.rsqrt(jnp.mean(xv * xv, axis=-1, keepdims=True) + RMS_EPS)
        xhat = xv * rstd
        dhv = dh_ref[...]
        dxhat = dhv * g_ref[...]
        dx_ref[...] = dres_ref[...] + rstd * (dxhat - xhat * jnp.mean(dxhat * xhat, axis=-1, keepdims=True))

        @pl.when(pl.program_id(0) == 0)
        def _():
            dg_ref[...] = jnp.zeros_like(dg_ref)

        dg_ref[...] += jnp.sum(dhv * xhat, axis=0, keepdims=True)

    row = pl.BlockSpec((tr, D), lambda i: (i, 0))
    vec = pl.BlockSpec((1, D), lambda i: (0, 0))
    dx, dg = pl.pallas_call(body, out_shape=(jax.ShapeDtypeStruct((S, D), F32), jax.ShapeDtypeStruct((1, D), F32)),
                            grid=(S // tr,), in_specs=[row, row, vec, row], out_specs=(row, vec),
                            compiler_params=_params(), name=name)(dh, x, g.reshape(1, D), dres)
    return dx, dg.reshape(D)


def _loss_head(x, g, tgt, *, name):
    S, D = x.shape
    tr = 256

    def body(x_ref, g_ref, t_ref, loss_ref, dx_ref, dg_ref):
        xv = x_ref[...]
        rstd = lax.rsqrt(jnp.mean(xv * xv, axis=-1, keepdims=True) + RMS_EPS)
        xhat = xv * rstd
        err = xhat * g_ref[...] - t_ref[...]
        part = 0.5 * jnp.sum(jnp.mean(err * err, axis=-1, keepdims=True), axis=0, keepdims=True)
        dy = err * (1.0 / D)
        dxhat = dy * g_ref[...]
        dx_ref[...] = rstd * (dxhat - xhat * jnp.mean(dxhat * xhat, axis=-1, keepdims=True))

        @pl.when(pl.program_id(0) == 0)
        def _():
            dg_ref[...] = jnp.zeros_like(dg_ref)
            loss_ref[...] = jnp.zeros_like(loss_ref)

        dg_ref[...] += jnp.sum(dy * xhat, axis=0, keepdims=True)
        loss_ref[...] += jnp.broadcast_to(part, loss_ref.shape)

    row = pl.BlockSpec((tr, D), lambda i: (i, 0))
    vec = pl.BlockSpec((1, D), lambda i: (0, 0))
    loss, dx, dg = pl.pallas_call(
        body, out_shape=(jax.ShapeDtypeStruct((8, LANES), F32), jax.ShapeDtypeStruct((S, D), F32),
                         jax.ShapeDtypeStruct((1, D), F32)),
        grid=(S // tr,), in_specs=[row, vec, row], out_specs=(pl.BlockSpec((8, LANES), lambda i: (0, 0)), row, vec),
        compiler_params=_params(), name=name)(x, g.reshape(1, D), tgt)
    return loss[0, 0], dx, dg.reshape(D)


def _rope_tables(S):
    inv = 10000.0 ** (-jnp.arange(0, DH, 2, dtype=F32) / DH)
    ang = jnp.arange(S, dtype=F32)[:, None] * inv[None, :]
    cos, sin = jnp.cos(ang), jnp.sin(ang)
    return jnp.concatenate([cos, cos], -1), jnp.concatenate([-sin, sin], -1)


def _swap_matrix():
    p = np.zeros((DH, DH), np.float32)
    for j in range(DH // 2):
        p[j + DH // 2, j] = 1.0
        p[j, j + DH // 2] = 1.0
    return jnp.asarray(p, jnp.bfloat16)


def _rope(x, cc, ss, rotated, *, name):
    n, S, _ = x.shape
    assert len(rotated) == n
    starts = [i for i in range(n) if rotated[i] and (i == 0 or not rotated[i - 1])]
    tab = pl.BlockSpec((S, DH), lambda i: (0, 0))
    for k, first in enumerate(starts):
        count = next((i for i in range(first, n) if not rotated[i]), n) - first

        def body(x_ref, c_ref, s_ref, p_ref, o_ref):
            xv = x_ref[0]
            o_ref[0] = xv * c_ref[...] + _dot_exact(xv, p_ref[...]) * s_ref[...]

        blk = pl.BlockSpec((1, S, DH), lambda i, first=first: (first + i, 0, 0))
        x = pl.pallas_call(body, out_shape=jax.ShapeDtypeStruct(x.shape, F32), grid=(count,),
                           in_specs=[blk, tab, tab, pl.BlockSpec((DH, DH), lambda i: (0, 0))], out_specs=blk,
                           input_output_aliases={0: 0}, compiler_params=_params(),
                           name=f"{name}{k}")(x, cc, ss, _swap_matrix())
    return x


def _key_range(kind, i, window, Sk):
    if kind == "cmp":
        return 0, Sk
    hi = (i + 1) * BQ
    if kind == "band":
        return max(0, i * BQ - window), hi
    return 0, hi


def _attn_mask(kind, i, lo, hi, window):
    shape = (BQ, hi - lo)
    qpos = i * BQ + lax.broadcasted_iota(jnp.int32, shape, 0)
    kpos = lo + lax.broadcasted_iota(jnp.int32, shape, 1)
    if kind == "cmp":
        return kpos * 16 + 31 <= qpos
    mask = kpos <= qpos
    if kind == "band":
        mask = mask & (qpos - kpos < window)
    return mask


def _sel_expand(n_slc, n_keys):
    shape = (n_slc, n_keys)
    j = lax.broadcasted_iota(jnp.int32, shape, 0)
    key = lax.broadcasted_iota(jnp.int32, shape, 1)
    return (jnp.right_shift(key, 6) == j).astype(_CDT)


def _eye():
    return lax.broadcasted_iota(jnp.int32, (BQ, BQ), 0) == lax.broadcasted_iota(jnp.int32, (BQ, BQ), 1)


def _to_col(row):
    return jnp.sum(jnp.where(_eye(), row, 0.0), axis=1, keepdims=True)


def _to_row(col):
    return jnp.sum(jnp.where(_eye(), col, 0.0), axis=0, keepdims=True)


def _scores(kind, i, lo, hi, window, qb, kb, crow_ref, sel_ref):
    s = _dot_nt(qb, kb)
    if crow_ref is not None:
        s = s + _to_col(crow_ref[0, :, i * BQ:(i + 1) * BQ]) - crow_ref[0, :, lo:hi]
    mask = _attn_mask(kind, i, lo, hi, window)
    if sel_ref is not None:
        mask = mask & (sel_ref[0, i * BQ:(i + 1) * BQ, lo:hi].astype(F32) > 0.5)
    return jnp.where(mask, s, NEG), mask


def _attn_fwd(q, k, v, *, G, R, heads=(0, 0, 0), kind, window=0, bias=None, sinks=None, selT=None, ovT=None, ctx=None,
              name):
    S = q.shape[1]
    Sk = k.shape[1]
    q0, k0, v0 = heads
    nq = S // BQ
    n_slc = S // 64
    has_bias, has_sink, has_sel, is_cmp = bias is not None, sinks is not None, selT is not None, kind == "cmp"

    def body(*refs):
        it = iter(refs)
        q_ref, k_ref, v_ref = next(it), next(it), next(it)
        crow_ref = next(it) if has_bias else None
        sink_ref = next(it) if has_sink else None
        sel_ref = next(it) if has_sel else None
        ov_ref = next(it) if is_cmp else None
        o_ref, lse_ref = next(it), next(it)
        selo_ref, imp_ref = (next(it), next(it)) if is_cmp else (None, None)
        r = pl.program_id(1)
        for i in range(nq):
            lo, hi = _key_range(kind, i, window, Sk)
            rows = slice(i * BQ, (i + 1) * BQ)
            qb = (q_ref[rows, :] * SCALE).astype(_CDT)
            kb = k_ref[0, lo:hi, :].astype(_CDT)
            vb = v_ref[0, lo:hi, :].astype(_CDT)
            s, mask = _scores(kind, i, lo, hi, window, qb, kb, crow_ref, sel_ref)
            m = jnp.max(s, axis=-1, keepdims=True)
            if has_sink:
                sk = sink_ref[0, 0, 0:1, 0:1]
                m = jnp.maximum(m, sk)
            e = jnp.exp(s - m)
            if is_cmp:
                e = jnp.where(mask, e, 0.0)
            l = jnp.sum(e, axis=-1, keepdims=True)
            if has_sink:
                l = l + jnp.exp(sk - m)
            if is_cmp:
                l = jnp.where(l > 0.0, l, 1.0)
            p = e * (1.0 / l)
            o_ref[0, 0, rows, :] = _dot(p.astype(_CDT), vb)
            lse_ref[0, 0, :, rows] = _to_row(m + jnp.log(l))
            if is_cmp:
                part = _dot_nt(ov_ref[...].astype(_CDT), p.astype(_CDT))

                @pl.when(r == 0)
                def _():
                    imp_ref[:, rows] = part

                @pl.when(r != 0)
                def _():
                    imp_ref[:, rows] += part

        if is_cmp:
            @pl.when(r == R - 1)
            def _():
                shape = (n_slc, S)
                j = lax.broadcasted_iota(jnp.int32, shape, 0)
                tb = jnp.right_shift(lax.broadcasted_iota(jnp.int32, shape, 1), 6)
                forced = (j == 0) | (j == tb) | (j == tb - 1)
                imp = jnp.where(j > tb, NEG, jnp.where(forced, NSA_BONUS, imp_ref[0:n_slc, :]))
                imp_ref[0:n_slc, :] = imp
                cnt = jnp.zeros(shape, F32)
                for jp in range(n_slc):
                    row = imp_ref[jp:jp + 1, :]
                    ahead = (row > imp) | ((row == imp) & (jp < j))
                    cnt = cnt + ahead.astype(F32)
                imp_ref[0:n_slc, :] = (cnt < float(min(NSA_TOPK, n_slc))).astype(F32)
                expand = _sel_expand(n_slc, S)
                for i in range(nq):
                    rows = slice(i * BQ, (i + 1) * BQ)
                    chosen = _dot_tn(imp_ref[0:n_slc, rows].astype(_CDT), expand)
                    selo_ref[0, rows, :] = chosen.astype(jnp.bfloat16)

    qspec = pl.BlockSpec((1, 1, S, DH), lambda g, r: (g, r, 0, 0))
    ins = [q, k, v]
    specs = [pl.BlockSpec((None, S, DH), lambda g, r: (q0 + g * R + r, 0, 0)),
             pl.BlockSpec((1, Sk, DH), lambda g, r: (k0 + g, 0, 0)), pl.BlockSpec((1, Sk, DH), lambda g, r: (v0 + g, 0, 0))]
    if has_bias:
        ins.append(bias)
        specs.append(pl.BlockSpec((1, 1, S), lambda g, r: (g, 0, 0)))
    if has_sink:
        ins.append(sinks)
        specs.append(pl.BlockSpec((1, 1, 8, LANES), lambda g, r: (g, r, 0, 0)))
    if has_sel:
        ins.append(selT)
        specs.append(pl.BlockSpec((1, S, S), lambda g, r: (g, 0, 0)))
    if is_cmp:
        ins.append(ovT)
        specs.append(pl.BlockSpec((LANES, Sk), lambda g, r: (0, 0)))
    outs = [jax.ShapeDtypeStruct((G, R, S, DH), F32), jax.ShapeDtypeStruct((G, R, 1, S), F32)]
    ospecs = [qspec, pl.BlockSpec((1, 1, 1, S), lambda g, r: (g, r, 0, 0))]
    scratch = []
    if is_cmp:
        outs.append(jax.ShapeDtypeStruct((G, S, S), jnp.bfloat16))
        ospecs.append(pl.BlockSpec((1, S, S), lambda g, r: (g, 0, 0)))
        scratch.append(pltpu.VMEM((LANES, S), F32))
    return _pcall(body, out_shape=tuple(outs), grid=(G, R), in_specs=specs, out_specs=tuple(ospecs),
                  scratch_shapes=scratch, name=name, jobs=ctx.take(name) if ctx else (), ins=ins)


def _attn_bwd(q, k, v, o, do, lse, *, heads=(0, 0, 0), kind, window=0, bias=None, sinks=None, selT=None, gate=None,
              ctx=None, name):
    G, R, S, _ = o.shape
    Sk = k.shape[1]
    q0, k0, v0 = heads
    nq = S // BQ
    has_bias, has_sink, has_sel, has_gate = bias is not None, sinks is not None, selT is not None, gate is not None

    def body(*refs):
        it = iter(refs)
        q_ref, k_ref, v_ref, o_ref, do_ref, lse_ref = (next(it) for _ in range(6))
        crow_ref = next(it) if has_bias else None
        sink_ref = next(it) if has_sink else None
        sel_ref = next(it) if has_sel else None
        z_ref = next(it) if has_gate else None
        dq_ref, dk_ref, dv_ref = next(it), next(it), next(it)
        dc_ref = next(it) if has_bias else None
        dsink_ref = next(it) if has_sink else None
        dz_ref = next(it) if has_gate else None
        r = pl.program_id(1)

        @pl.when(r == 0)
        def _():
            dk_ref[...] = jnp.zeros_like(dk_ref)
            dv_ref[...] = jnp.zeros_like(dv_ref)
            if has_bias:
                dc_ref[...] = jnp.zeros_like(dc_ref)

        dsink = jnp.zeros((1, 1), F32)
        for i in range(nq):
            lo, hi = _key_range(kind, i, window, Sk)
            rows = slice(i * BQ, (i + 1) * BQ)
            qb = (q_ref[rows, :] * SCALE).astype(_CDT)
            kb = k_ref[0, lo:hi, :].astype(_CDT)
            vb = v_ref[0, lo:hi, :].astype(_CDT)
            s, mask = _scores(kind, i, lo, hi, window, qb, kb, crow_ref, sel_ref)
            lse_i = _to_col(lse_ref[0, 0, :, rows])
            p = jnp.where(mask, jnp.exp(s - lse_i), 0.0)
            dob = do_ref[0, 0, rows, :]
            if has_gate:
                od = jnp.sum(o_ref[0, 0, rows, :] * dob, axis=-1, keepdims=True)
                sg = jax.nn.sigmoid(_to_col(z_ref[0, 0, :, rows]))
                dob = dob * sg
                dz_ref[0, 0, :, rows] = _to_row(od * sg * (1.0 - sg))
            dob = dob.astype(_CDT)
            dp = _dot_nt(dob, vb)
            delta = jnp.sum(p * dp, axis=-1, keepdims=True)
            ds = p * (dp - delta)
            dsb = ds.astype(_CDT)
            dq_ref[0, 0, rows, :] = _dot(dsb, kb) * SCALE
            dk_ref[0, lo:hi, :] += _dot_tn(dsb, qb)
            dv_ref[0, lo:hi, :] += _dot_tn(p.astype(_CDT), dob)
            if has_bias:
                dc_ref[0, :, rows] += _to_row(jnp.sum(ds, axis=-1, keepdims=True))
                dc_ref[0, :, lo:hi] -= jnp.sum(ds, axis=0, keepdims=True)
            if has_sink:
                sk = sink_ref[0, 0, 0:1, 0:1]
                dsink = dsink - jnp.sum(jnp.exp(sk - lse_i) * delta, axis=0, keepdims=True)
        if has_sink:
            dsink_ref[0, 0] = jnp.broadcast_to(dsink, (8, LANES))

    qspec = pl.BlockSpec((1, 1, S, DH), lambda g, r: (g, r, 0, 0))
    cspec = pl.BlockSpec((1, 1, 1, S), lambda g, r: (g, r, 0, 0))
    kspec = pl.BlockSpec((1, Sk, DH), lambda g, r: (g, 0, 0))
    ins = [q, k, v, o, do, lse]
    specs = [pl.BlockSpec((None, S, DH), lambda g, r: (q0 + g * R + r, 0, 0)),
             pl.BlockSpec((1, Sk, DH), lambda g, r: (k0 + g, 0, 0)), pl.BlockSpec((1, Sk, DH), lambda g, r: (v0 + g, 0, 0)),
             qspec, qspec, cspec]
    if has_bias:
        ins.append(bias)
        specs.append(pl.BlockSpec((1, 1, S), lambda g, r: (g, 0, 0)))
    if has_sink:
        ins.append(sinks)
        specs.append(pl.BlockSpec((1, 1, 8, LANES), lambda g, r: (g, r, 0, 0)))
    if has_sel:
        ins.append(selT)
        specs.append(pl.BlockSpec((1, S, S), lambda g, r: (g, 0, 0)))
    if has_gate:
        ins.append(gate)
        specs.append(cspec)
    names = ["dq", "dk", "dv"]
    outs = [jax.ShapeDtypeStruct((G, R, S, DH), F32), jax.ShapeDtypeStruct((G, Sk, DH), F32),
            jax.ShapeDtypeStruct((G, Sk, DH), F32)]
    ospecs = [qspec, kspec, kspec]
    if has_bias:
        assert R == 1
        names.append("dc")
        outs.append(jax.ShapeDtypeStruct((G, 1, S), F32))
        ospecs.append(pl.BlockSpec((1, 1, S), lambda g, r: (g, 0, 0)))
    if has_sink:
        names.append("dsink")
        outs.append(jax.ShapeDtypeStruct((G, R, 8, LANES), F32))
        ospecs.append(pl.BlockSpec((1, 1, 8, LANES), lambda g, r: (g, r, 0, 0)))
    if has_gate:
        names.append("dz")
        outs.append(jax.ShapeDtypeStruct((G, R, 1, S), F32))
        ospecs.append(cspec)
    res = _pcall(body, out_shape=tuple(outs), grid=(G, R), in_specs=specs, out_specs=tuple(ospecs), name=name,
                 jobs=ctx.take(name) if ctx else (), ins=ins)
    return dict(zip(names, res))


def _combine(o0, o1, o2, z, *, name):
    H, S, _ = o0.shape

    def body(o0_ref, o1_ref, o2_ref, z_ref, o_ref):
        for i in range(S // BQ):
            rows = slice(i * BQ, (i + 1) * BQ)
            acc = jax.nn.sigmoid(_to_col(z_ref[0, 0, :, rows])) * o0_ref[0, rows, :]
            acc = acc + jax.nn.sigmoid(_to_col(z_ref[1, 0, :, rows])) * o1_ref[0, rows, :]
            acc = acc + jax.nn.sigmoid(_to_col(z_ref[2, 0, :, rows])) * o2_ref[0, rows, :]
            o_ref[0, rows, :] = acc

    blk = pl.BlockSpec((1, S, DH), lambda h: (h, 0, 0))
    return pl.pallas_call(body, out_shape=jax.ShapeDtypeStruct((H, S, DH), F32), grid=(H,),
                          in_specs=[blk, blk, blk, pl.BlockSpec((3, 1, 1, S), lambda h: (0, h, 0, 0))], out_specs=blk,
                          compiler_params=_params(), name=name)(o0, o1, o2, z)


_GC = math.sqrt(2.0 / math.pi)


def _gelu(x):
    return 0.5 * x * (1.0 + jnp.tanh(_GC * (x + 0.044715 * x * x * x)))


def _gelu_grad(x):
    t = jnp.tanh(_GC * (x + 0.044715 * x * x * x))
    return 0.5 * (1.0 + t) + 0.5 * x * (1.0 - t * t) * _GC * (1.0 + 3.0 * 0.044715 * x * x)


def _make_xb(k):
    G, S, _ = k.shape
    chunks = k.reshape(G, S // 16, 16 * DH)
    shift = jnp.concatenate([chunks[:, 1:], jnp.zeros((G, 1, 16 * DH), k.dtype)], axis=1)
    return jnp.concatenate([chunks, shift], axis=-1)


def _unmake_xb(dxb, *, name):
    G, n, _ = dxb.shape
    a = dxb[..., :16 * DH]
    b = jnp.concatenate([jnp.zeros((G, 1, 16 * DH), F32), dxb[:, :-1, 16 * DH:]], axis=1)
    return _addn(a, b, name=name).reshape(G, n * 16, DH)


def _compress_fwd(xb, pe, w1, w2, *, name):
    G, n, W = xb.shape
    Hc = w1.shape[1]

    def body(xb_ref, pe_ref, w1_ref, w2_ref, kc_ref, hid_ref):
        xv = (xb_ref[0] + pe_ref[...]).astype(_CDT)
        hid = _dot(xv, w1_ref[...].astype(_CDT))
        hid_ref[0] = hid
        kc_ref[0] = _dot(_gelu(hid).astype(_CDT), w2_ref[...].astype(_CDT))

    return pl.pallas_call(
        body, out_shape=(jax.ShapeDtypeStruct((G, n, DH), F32), jax.ShapeDtypeStruct((G, n, Hc), F32)), grid=(G,),
        in_specs=[pl.BlockSpec((1, n, W), lambda g: (g, 0, 0)), pl.BlockSpec((1, W), lambda g: (0, 0)),
                  pl.BlockSpec((W, Hc), lambda g: (0, 0)), pl.BlockSpec((Hc, DH), lambda g: (0, 0))],
        out_specs=(pl.BlockSpec((1, n, DH), lambda g: (g, 0, 0)), pl.BlockSpec((1, n, Hc), lambda g: (g, 0, 0))),
        compiler_params=_params(), name=name)(xb, pe.reshape(1, W), w1, w2)


def _compress_bwd(xb, pe, w1, w2, hid, dkc, *, name):
    G, n, W = xb.shape
    Hc = w1.shape[1]

    def body(xb_ref, pe_ref, w1_ref, w2_ref, hid_ref, dkc_ref, dxb_ref, dw1_ref, dw2_ref, dpe_ref):
        @pl.when(pl.program_id(0) == 0)
        def _():
            dw1_ref[...] = jnp.zeros_like(dw1_ref)
            dw2_ref[...] = jnp.zeros_like(dw2_ref)
            dpe_ref[...] = jnp.zeros_like(dpe_ref)

        xv = (xb_ref[0] + pe_ref[...]).astype(_CDT)
        hid = hid_ref[0]
        dk = dkc_ref[0].astype(_CDT)
        dact = _dot_nt(dk, w2_ref[...].astype(_CDT))
        dhid = (dact * _gelu_grad(hid)).astype(_CDT)
        dw2_ref[...] += _dot_tn(_gelu(hid).astype(_CDT), dk)
        dxb = _dot_nt(dhid, w1_ref[...].astype(_CDT))
        dxb_ref[0] = dxb
        dw1_ref[...] += _dot_tn(xv, dhid)
        dpe_ref[...] += jnp.sum(dxb, axis=0, keepdims=True)

    return pl.pallas_call(
        body, out_shape=(jax.ShapeDtypeStruct((G, n, W), F32), jax.ShapeDtypeStruct((W, Hc), F32),
                         jax.ShapeDtypeStruct((Hc, DH), F32), jax.ShapeDtypeStruct((1, W), F32)), grid=(G,),
        in_specs=[pl.BlockSpec((1, n, W), lambda g: (g, 0, 0)), pl.BlockSpec((1, W), lambda g: (0, 0)),
                  pl.BlockSpec((W, Hc), lambda g: (0, 0)), pl.BlockSpec((Hc, DH), lambda g: (0, 0)),
                  pl.BlockSpec((1, n, Hc), lambda g: (g, 0, 0)), pl.BlockSpec((1, n, DH), lambda g: (g, 0, 0))],
        out_specs=(pl.BlockSpec((1, n, W), lambda g: (g, 0, 0)), pl.BlockSpec((W, Hc), lambda g: (0, 0)),
                   pl.BlockSpec((Hc, DH), lambda g: (0, 0)), pl.BlockSpec((1, W), lambda g: (0, 0))),
        compiler_params=_params(), name=name)(xb, pe.reshape(1, W), w1, w2, hid, dkc)


def _overlap_T(S):
    n_cmp, n_slc = S // 16 - 1, S // 64
    cs = np.arange(n_cmp) * 16
    ce = cs + 32
    ss = np.arange(n_slc) * 64
    se = ss + 64
    ov = np.clip(np.minimum(ce[:, None], se[None, :]) - np.maximum(cs[:, None], ss[None, :]), 0, None) / 32.0
    out = np.zeros((LANES, S // 16), np.float32)
    out[:n_slc, :n_cmp] = ov.T
    return jnp.asarray(out)


def _tri(n, upper):
    r = lax.broadcasted_iota(jnp.int32, (n, n), 0)
    c = lax.broadcasted_iota(jnp.int32, (n, n), 1)
    return ((c >= r) if upper else (c <= r)).astype(jnp.bfloat16)


def _fox_gate_fwd(zf, b, *, name):
    S, H = zf.shape
    nb = S // BQ

    def body(z_ref, b_ref, c_ref):
        tri = _tri(BQ, False)
        carry = jnp.zeros((1, H), F32)
        for i in range(nb):
            z = z_ref[i * BQ:(i + 1) * BQ, :] + b_ref[...]
            lf = jnp.minimum(z, 0.0) - jnp.log(1.0 + jnp.exp(-jnp.abs(z)))
            c_ref[i * BQ:(i + 1) * BQ, :] = _dot_exact_left(tri, lf) + carry
            carry = carry + jnp.sum(lf, axis=0, keepdims=True)

    return pl.pallas_call(body, out_shape=jax.ShapeDtypeStruct((S, H), F32), compiler_params=_params(),
                          name=name)(zf, b)


def _fox_gate_bwd(zf, b, dc, *, name):
    S, H = zf.shape
    nb = S // BQ

    def body(z_ref, b_ref, dc_ref, dz_ref, db_ref):
        tri = _tri(BQ, True)
        carry = jnp.zeros((1, H), F32)
        db = jnp.zeros((1, H), F32)
        for i in reversed(range(nb)):
            rows = slice(i * BQ, (i + 1) * BQ)
            dcb = dc_ref[rows, :]
            dlf = _dot_exact_left(tri, dcb) + carry
            carry = carry + jnp.sum(dcb, axis=0, keepdims=True)
            z = z_ref[rows, :] + b_ref[...]
            dz = dlf * jax.nn.sigmoid(-z)
            dz_ref[rows, :] = dz
            db = db + jnp.sum(dz, axis=0, keepdims=True)
        db_ref[...] = db

    return pl.pallas_call(body, out_shape=(jax.ShapeDtypeStruct((S, H), F32), jax.ShapeDtypeStruct((1, H), F32)),
                          compiler_params=_params(), name=name)(zf, b, dc)


def _from_heads(a):
    return a.transpose(1, 0, 2).reshape(a.shape[1], -1).astype(_CDT)


def _pad_lanes(a):
    return jnp.pad(a, ((0, 0), (0, LANES - a.shape[1])))


FFN_TM = 512


def _ffn_up(h, w4, *, ctx, name):
    S, D = h.shape
    C = w4.shape[2]

    def body(h_ref, wg_ref, wu_ref, gu_ref, a_ref):
        hv = h_ref[...].astype(_CDT)
        g = _dot(hv, wg_ref[...].astype(_CDT))
        u = _dot(hv, wu_ref[...].astype(_CDT))
        gu_ref[0] = g
        gu_ref[1] = u
        a_ref[...] = (g * jax.nn.sigmoid(g) * u).astype(_CDT)

    return _pcall(
        body, out_shape=(jax.ShapeDtypeStruct((2, S, 2 * C), F32), jax.ShapeDtypeStruct((S, 2 * C), _CDT)),
        grid=(2, S // FFN_TM),
        in_specs=[pl.BlockSpec((FFN_TM, D), lambda j, i: (i, 0)), pl.BlockSpec((None, D, C), lambda j, i: (j, 0, 0)),
                  pl.BlockSpec((None, D, C), lambda j, i: (j + 2, 0, 0))],
        out_specs=(pl.BlockSpec((2, FFN_TM, C), lambda j, i: (0, i, j)), pl.BlockSpec((FFN_TM, C), lambda j, i: (i, j))),
        name=name, jobs=ctx.take(name) if ctx else (), ins=[h, w4, w4])


def _ffn_dact(dxo, wd, gu, *, ctx, name):
    S, D = dxo.shape
    C = gu.shape[2] // 2

    def body(dx_ref, wd_ref, gu_ref, d_ref):
        da = _dot_nt(dx_ref[...].astype(_CDT), wd_ref[...].astype(_CDT)) * 0.5
        g, u = gu_ref[0], gu_ref[1]
        sg = jax.nn.sigmoid(g)
        silu = g * sg
        d_ref[0] = (da * u * (sg + silu * (1.0 - sg))).astype(_CDT)
        d_ref[1] = (da * silu).astype(_CDT)

    blk = pl.BlockSpec((2, FFN_TM, C), lambda j, i: (0, i, j))
    return _pcall(body, out_shape=jax.ShapeDtypeStruct(gu.shape, _CDT), grid=(2, S // FFN_TM),
                  in_specs=[pl.BlockSpec((FFN_TM, D), lambda j, i: (i, 0)), pl.BlockSpec((C, D), lambda j, i: (j, 0)), blk],
                  out_specs=blk, name=name, jobs=ctx.take(name) if ctx else (), ins=[dxo, wd, gu])


def _ffn_dwgu(h, dgu, *, ctx, name):
    S, D = h.shape
    C = dgu.shape[2] // 2
    tm = 512

    def body(h_ref, d_ref, o_ref):
        o_ref[...] = _dot_tn(h_ref[...].astype(_CDT), d_ref[...].astype(_CDT)).astype(_WIRE)

    return _pcall(body, out_shape=jax.ShapeDtypeStruct((4, D, C), _WIRE), grid=(4, D // tm),
                  in_specs=[pl.BlockSpec((S, tm), lambda j, i: (0, i)),
                            pl.BlockSpec((None, S, C), lambda j, i: (j // 2, 0, j % 2))],
                  out_specs=pl.BlockSpec((None, tm, C), lambda j, i: (j, i, 0)), name=name,
                  jobs=ctx.take(name) if ctx else (), ins=[h, dgu])


def _ffn_dh(dgu, w4, *, ctx, name):
    _, S, F2 = dgu.shape
    C, D = F2 // 2, w4.shape[1]

    def body(d_ref, w_ref, o_ref, acc_ref):
        k = pl.program_id(1)
        prod = _dot_nt(d_ref[...].astype(_CDT), w_ref[...].astype(_CDT))

        @pl.when(k == 0)
        def _():
            acc_ref[...] = prod

        @pl.when(k != 0)
        def _():
            acc_ref[...] += prod

        @pl.when(k == 3)
        def _():
            o_ref[...] = acc_ref[...]

    return _pcall(body, out_shape=jax.ShapeDtypeStruct((S, D), F32), grid=(S // FFN_TM, 4),
                  in_specs=[pl.BlockSpec((None, FFN_TM, C), lambda i, k: (k // 2, i, k % 2)),
                            pl.BlockSpec((None, D, C), lambda i, k: (k, 0, 0))],
                  out_specs=pl.BlockSpec((FFN_TM, D), lambda i, k: (i, 0)), scratch_shapes=[pltpu.VMEM((FFN_TM, D), F32)],
                  name=name, jobs=ctx.take(name) if ctx else (), ins=[dgu, w4])


def _ffn_fwd(x, g, P, l, tag):
    h = _rms_fwd(x, g, name=tag + "_rms")
    gu, a = _ffn_up(h, P.weight("gu", l), ctx=P.ctx, name=tag + "_gu")
    xo = _mm(a, P.weight("down", l), tm=512, tn=1024, res=x, alpha=0.5, ctx=P.ctx, name=tag + "_down")
    return xo, (x, h, gu, a)


def _ffn_bwd(dxo, saved, g, P, l, tag):
    x, h, gu, a = saved
    dgu = _ffn_dact(dxo, P.weight("down", l), gu, ctx=P.ctx, name=tag + "_dact")
    P.grad("gu", l, _ffn_dwgu(h, dgu, ctx=P.ctx, name=tag + "_dwgu"))
    dwd = _mm(a, dxo, ta=True, tm=1408, tn=1024, alpha=0.5, out_dtype=_WIRE, ctx=P.ctx, name=tag + "_dwd")
    P.grad("down", l, dwd.reshape(4, -1, D_MODEL))
    dh = _ffn_dh(dgu, P.weight("gu", l), ctx=P.ctx, name=tag + "_dh")
    dx, dg = _rms_bwd(dh, x, g, dxo, name=tag + "_drms")
    return dx, dg


def _nsa_fwd(x, g, w, cc, ss, tag):
    S = x.shape[0]
    h = _rms_fwd(x, g, name=tag + "_rms")
    hm = _mm(h, w["w_in"], tm=512, tn=896, o_lead=("heads",), ctx=w["ctx"], name=tag + "_in")
    roped = _rope(hm, cc, ss, NSA_ROTATED, name=tag + "_rope")
    z = roped[40, :, :3 * N_HEADS].reshape(S, 3, N_HEADS).transpose(1, 2, 0)[:, :, None, :]
    xbk, xbv = _make_xb(roped[16:20]), _make_xb(roped[20:24])
    kc, hidk = _compress_fwd(xbk, w["ck_pe"], w["ck_w1"], w["ck_w2"], name=tag + "_ck")
    vc, hidv = _compress_fwd(xbv, w["cv_pe"], w["cv_w1"], w["cv_w2"], name=tag + "_cv")
    ctx, GR = w["ctx"], dict(G=NSA_G, R=NSA_R)
    o0, lse0, selT = _attn_fwd(roped, kc, vc, kind="cmp", ovT=_overlap_T(S), ctx=ctx, name=tag + "_cmp", **GR)
    o1, lse1 = _attn_fwd(roped, roped, roped, heads=(0, 24, 28), kind="sel", selT=selT, ctx=ctx, name=tag + "_slc", **GR)
    o2, lse2 = _attn_fwd(roped, roped, roped, heads=(0, 32, 36), kind="band", window=NSA_WINDOW, ctx=ctx,
                         name=tag + "_win", **GR)
    o = _combine(o0.reshape(N_HEADS, S, DH), o1.reshape(N_HEADS, S, DH), o2.reshape(N_HEADS, S, DH), z,
                 name=tag + "_mix")
    of = _from_heads(o)
    xo = _mm(of, w["out"], tm=512, tn=1024, res=x, ctx=w["ctx"], name=tag + "_out")
    saved = (x, h, roped, z, xbk, xbv, hidk, hidv, kc, vc, (o0, o1, o2), (lse0, lse1, lse2), selT, of)
    return xo, saved


def _out_bwd(dxo, of, w, tag):
    dof = _mm(dxo, w["out"], tb=True, tm=512, tn=1024, o_lead=("heads",), ctx=w["ctx"], name=tag + "_dof")
    dw = _mm(of, dxo, ta=True, tm=512, tn=1024, out_dtype=_WIRE, ctx=w["ctx"], name=tag + "_dwout")
    w["P"].grad("out", w["out_l"], dw.reshape(4, -1, D_MODEL))
    return dof


def _split_cols(dw, n_in):
    cs = n_in // 4
    return dw[:, :n_in].reshape(D_MODEL, 4, cs).transpose(1, 0, 2).astype(_WIRE)


def _nsa_bwd(dxo, saved, g, w, cc, ss, tag):
    x, h, roped, z, xbk, xbv, hidk, hidv, kc, vc, os_, lses, selT, of = saved
    S = x.shape[0]
    do = _out_bwd(dxo, of, w, tag).reshape(NSA_G, NSA_R, S, DH)
    zg = z.reshape(3, NSA_G, NSA_R, 1, S)
    ctx = w["ctx"]
    b0 = _attn_bwd(roped, kc, vc, os_[0], do, lses[0], kind="cmp", gate=zg[0], ctx=ctx, name=tag + "_dcmp")
    b1 = _attn_bwd(roped, roped, roped, os_[1], do, lses[1], heads=(0, 24, 28), kind="sel", selT=selT, gate=zg[1],
                   ctx=ctx, name=tag + "_dslc")
    b2 = _attn_bwd(roped, roped, roped, os_[2], do, lses[2], heads=(0, 32, 36), kind="band", window=NSA_WINDOW,
                   gate=zg[2], ctx=ctx, name=tag + "_dwin")
    dxbk, dck_w1, dck_w2, dck_pe = _compress_bwd(xbk, w["ck_pe"], w["ck_w1"], w["ck_w2"], hidk, b0["dk"],
                                                 name=tag + "_dck")
    dxbv, dcv_w1, dcv_w2, dcv_pe = _compress_bwd(xbv, w["cv_pe"], w["cv_w1"], w["cv_w2"], hidv, b0["dv"],
                                                 name=tag + "_dcv")
    dk0 = _unmake_xb(dxbk, name=tag + "_dk0")
    dv0 = _unmake_xb(dxbv, name=tag + "_dv0")
    dq = _addn(b0["dq"], b1["dq"], b2["dq"], name=tag + "_dqsum").reshape(N_HEADS, S, DH)
    dz = jnp.stack([b0["dz"], b1["dz"], b2["dz"]], axis=0).reshape(3 * N_HEADS, S).T
    dgates = jnp.pad(dz, ((0, 0), (0, 2 * DH - 3 * N_HEADS))).reshape(S, 2, DH).transpose(1, 0, 2)
    dhm = jnp.concatenate([dq, dk0, dv0, b1["dk"], b1["dv"], b2["dk"], b2["dv"], dgates], axis=0)
    dproj = _from_heads(_rope(dhm, cc, -ss, NSA_ROTATED, name=tag + "_drope"))
    dh = _mm(dproj, w["w_in"], tb=True, tm=512, tn=512, ctx=w["ctx"], name=tag + "_dh")
    dw_in = _mm(h, dproj, ta=True, tm=512, tn=896, ctx=w["ctx"], name=tag + "_dwin_w")
    dx, dg = _rms_bwd(dh, x, g, dxo, name=tag + "_drms")
    P, j = w["P"], w["j"]
    P.grad("nsa_in", j, _split_cols(dw_in, NSA_IN))
    P.grad("cw1", j, dck_w1.astype(_WIRE).reshape(4, -1, dck_w1.shape[1]))
    P.grad("cw1", 2 + j, dcv_w1.astype(_WIRE).reshape(4, -1, dcv_w1.shape[1]))
    grads = dict(ck_pe=dck_pe.reshape(32, DH), ck_w2=dck_w2, cv_pe=dcv_pe.reshape(32, DH), cv_w2=dcv_w2)
    return dx, dg, grads


def _swa_fwd(x, g, w, cc, ss, tag):
    S = x.shape[0]
    h = _rms_fwd(x, g, name=tag + "_rms")
    hm = _mm(h, w["w_in"], tm=512, tn=640, o_lead=("heads",), ctx=w["ctx"], name=tag + "_in")
    roped = _rope(hm, cc, ss, SWA_ROTATED, name=tag + "_rope")
    sinks = jnp.broadcast_to(w["sinks"].reshape(SWA_G, SWA_R, 1, 1), (SWA_G, SWA_R, 8, LANES))
    o, lse = _attn_fwd(roped, roped, roped, G=SWA_G, R=SWA_R, heads=SWA_HEADS, kind="band", window=SWA_WINDOW,
                       sinks=sinks, ctx=w["ctx"], name=tag + "_attn")
    of = _from_heads(o.reshape(N_HEADS, S, DH))
    xo = _mm(of, w["out"], tm=512, tn=1024, res=x, ctx=w["ctx"], name=tag + "_out")
    return xo, (x, h, roped, sinks, o, lse, of)


def _swa_bwd(dxo, saved, g, w, cc, ss, tag):
    x, h, roped, sinks, o, lse, of = saved
    S = x.shape[0]
    do = _out_bwd(dxo, of, w, tag).reshape(SWA_G, SWA_R, S, DH)
    b = _attn_bwd(roped, roped, roped, o, do, lse, heads=SWA_HEADS, kind="band", window=SWA_WINDOW, sinks=sinks,
                  ctx=w["ctx"], name=tag + "_dattn")
    dhm = jnp.concatenate([b["dq"].reshape(N_HEADS, S, DH), b["dk"], b["dv"]], axis=0)
    dproj = _from_heads(_rope(dhm, cc, -ss, SWA_ROTATED, name=tag + "_drope"))
    dh = _mm(dproj, w["w_in"], tb=True, tm=512, tn=512, ctx=w["ctx"], name=tag + "_dh")
    dw_in = _mm(h, dproj, ta=True, tm=512, tn=640, ctx=w["ctx"], name=tag + "_dwin_w")
    dx, dg = _rms_bwd(dh, x, g, dxo, name=tag + "_drms")
    w["P"].grad("swa_in", w["j"], _split_cols(dw_in, SWA_IN))
    return dx, dg, dict(sinks=b["dsink"][:, :, 0, 0].reshape(N_HEADS))


def _fox_fwd(x, g, w, tag):
    S = x.shape[0]
    h = _rms_fwd(x, g, name=tag + "_rms")
    hm = _mm(h, w["w_in"], tm=512, tn=640, o_lead=("heads",), ctx=w["ctx"], name=tag + "_in")
    zf = jnp.pad(hm[48], ((0, 0), (0, LANES - DH)))
    bf = _pad_lanes(w["b_f"].reshape(1, N_HEADS))
    c = _fox_gate_fwd(zf, bf, name=tag + "_gate")[:, :N_HEADS]
    bias = c.T[:, None, :]
    o, lse = _attn_fwd(hm, hm, hm, G=N_HEADS, R=1, heads=FOX_HEADS, kind="causal", bias=bias, ctx=w["ctx"],
                       name=tag + "_attn")
    of = _from_heads(o.reshape(N_HEADS, S, DH))
    xo = _mm(of, w["out"], tm=512, tn=1024, res=x, ctx=w["ctx"], name=tag + "_out")
    return xo, (x, h, hm, zf, bf, bias, o, lse, of)


def _fox_bwd(dxo, saved, g, w, tag):
    x, h, hm, zf, bf, bias, o, lse, of = saved
    S = x.shape[0]
    do = _out_bwd(dxo, of, w, tag).reshape(N_HEADS, 1, S, DH)
    b = _attn_bwd(hm, hm, hm, o, do, lse, heads=FOX_HEADS, kind="causal", bias=bias, ctx=w["ctx"], name=tag + "_dattn")
    dzf, db = _fox_gate_bwd(zf, bf, _pad_lanes(b["dc"].reshape(N_HEADS, S).T), name=tag + "_dgate")
    dgates = dzf.reshape(S, 2, DH).transpose(1, 0, 2)
    dhm = jnp.concatenate([b["dq"].reshape(N_HEADS, S, DH), b["dk"], b["dv"], dgates], axis=0)
    dproj = _from_heads(dhm)
    dh = _mm(dproj, w["w_in"], tb=True, tm=512, tn=512, ctx=w["ctx"], name=tag + "_dh")
    dw_in = _mm(h, dproj, ta=True, tm=512, tn=640, ctx=w["ctx"], name=tag + "_dwin_w")
    dx, dg = _rms_bwd(dh, x, g, dxo, name=tag + "_drms")
    w["P"].grad("fox_in", w["j"], _split_cols(dw_in, FOX_IN))
    return dx, dg, dict(b_f=db[0, :N_HEADS])


GROUPS = {
    "gu": (("ffn1_w_gu", "ffn2_w_gu"), 2),
    "down": (("ffn1_w_down", "ffn2_w_down"), 1),
    "out": (("nsa_w_out", "swa_w_out", "fox_w_out"), 1),
    "cw1": (("nsa_ck_w1", "nsa_cv_w1"), 1),
    "nsa_in": (("nsa_w_in",), 2),
    "swa_in": (("swa_w_in",), 2),
    "fox_in": (("fox_w_in",), 2),
}
OUT_SLAB = {0: 0, 3: 1, 1: 2, 2: 3}


def _pieces_in_order():
    chunks = []
    for i in range(DEPTH):
        kind, j = i % 3, i // 3
        chunks.append([("gu", i), ("down", i)])
        if kind == 0:
            chunks.append([("nsa_in", j), ("cw1", j), ("cw1", 2 + j), ("out", OUT_SLAB[i])])
        else:
            chunks.append([("swa_in" if kind == 1 else "fox_in", j), ("out", OUT_SLAB[i])])
        chunks.append([("gu", DEPTH + i), ("down", DEPTH + i)])
    return chunks


def _consumer_layout(group, F):
    _, rows, C = F.shape
    if group == "gu":
        return F
    if GROUPS[group][1] == 1:
        return F.reshape(4 * rows, C)
    w = F.transpose(1, 0, 2).reshape(rows, 4 * C)
    pad = {"nsa_in": NSA_IN_PAD, "swa_in": SWA_IN, "fox_in": FOX_IN_PAD}[group] - 4 * C
    return jnp.pad(w, ((0, 0), (0, pad)))


class _Given:
    def __init__(self, pieces, small):
        self.pieces, self.small, self.ctx, self.grads = pieces, small, None, {}

    def weight(self, group, l):
        return _consumer_layout(group, self.pieces[group, l])

    def grad(self, group, l, G):
        self.grads[group, l] = G


class _MixerWeights(dict):
    def __init__(self, P, pieces, **given):
        super().__init__(P=P, ctx=P.ctx, **given)
        self.pieces = pieces

    def __missing__(self, key):
        self[key] = self["P"].weight(*self.pieces[key])
        return self[key]


def _mixer_weights(P, i):
    kind, j = i % 3, i // 3
    out = {"out": ("out", OUT_SLAB[i])}
    if kind == 0:
        small = {k: P.small["nsa_" + k][j] for k in ("ck_pe", "ck_w2", "cv_pe", "cv_w2")}
        return _MixerWeights(P, dict(out, w_in=("nsa_in", j), ck_w1=("cw1", j), cv_w1=("cw1", 2 + j)), j=j,
                             out_l=OUT_SLAB[i], **small)
    if kind == 1:
        return _MixerWeights(P, dict(out, w_in=("swa_in", j)), j=j, out_l=OUT_SLAB[i], sinks=P.small["swa_sinks"][j])
    return _MixerWeights(P, dict(out, w_in=("fox_in", j)), j=j, out_l=OUT_SLAB[i], b_f=P.small["fox_b_f"][j])


def _local_step(x, tgt, P):
    S = x.shape[0]
    cc, ss = _rope_tables(S)
    sm = P.small
    saved = []
    for i in range(DEPTH):
        kind = i % 3
        x, s1 = _ffn_fwd(x, sm["ffn1_norm"][i], P, i, f"l{i}f1")
        mw = _mixer_weights(P, i)
        if kind == 0:
            x, s2 = _nsa_fwd(x, sm["mix_norm"][i], mw, cc, ss, f"l{i}nsa")
        elif kind == 1:
            x, s2 = _swa_fwd(x, sm["mix_norm"][i], mw, cc, ss, f"l{i}swa")
        else:
            x, s2 = _fox_fwd(x, sm["mix_norm"][i], mw, f"l{i}fox")
        x, s3 = _ffn_fwd(x, sm["ffn2_norm"][i], P, DEPTH + i, f"l{i}f2")
        saved.append((s1, mw, s2, s3))
    loss, dx, d_final = _loss_head(x, sm["final_norm"], tgt, name="loss_head")

    norms = {k: [None] * DEPTH for k in ("ffn1_norm", "mix_norm", "ffn2_norm")}
    mix = {}
    for i in reversed(range(DEPTH)):
        kind, j = i % 3, i // 3
        s1, mw, s2, s3 = saved[i]
        dx, norms["ffn2_norm"][i] = _ffn_bwd(dx, s3, sm["ffn2_norm"][i], P, DEPTH + i, f"l{i}f2")
        if kind == 0:
            dx, dg, gm = _nsa_bwd(dx, s2, sm["mix_norm"][i], mw, cc, ss, f"l{i}nsa")
            pre = "nsa_"
        elif kind == 1:
            dx, dg, gm = _swa_bwd(dx, s2, sm["mix_norm"][i], mw, cc, ss, f"l{i}swa")
            pre = "swa_"
        else:
            dx, dg, gm = _fox_bwd(dx, s2, sm["mix_norm"][i], mw, f"l{i}fox")
            pre = "fox_"
        norms["mix_norm"][i] = dg
        for k, val in gm.items():
            mix.setdefault(pre + k, {})[j] = val
        dx, norms["ffn1_norm"][i] = _ffn_bwd(dx, s1, sm["ffn1_norm"][i], P, i, f"l{i}f1")
    small = {k: jnp.stack(v, axis=0) for k, v in norms.items()}
    small.update({k: jnp.stack([d[j] for j in sorted(d)], axis=0) for k, d in mix.items()})
    small["final_norm"] = d_final
    return loss, dx, small


def _sum_slots(own, recv, me, into, row0, *, name):
    _, R, C = recv.shape
    tr = _row_tile(math.gcd(R, row0), C)
    blk0 = row0 // tr

    def body(_, g_ref, r_ref, __, o_ref):
        acc = g_ref[...].astype(F32) + r_ref[0].astype(F32)
        acc = acc + r_ref[1].astype(F32)
        o_ref[...] = acc + r_ref[2].astype(F32)

    spec = pltpu.PrefetchScalarGridSpec(
        num_scalar_prefetch=1, grid=(R // tr,),
        in_specs=[pl.BlockSpec((None, tr, C), lambda i, me_ref: (me_ref[0], i, 0)),
                  pl.BlockSpec((3, tr, C), lambda i, me_ref: (0, i, 0)), _ANY],
        out_specs=pl.BlockSpec((tr, C), lambda i, me_ref: (blk0 + i, 0)))
    return pl.pallas_call(body, grid_spec=spec, out_shape=jax.ShapeDtypeStruct(into.shape, F32),
                          input_output_aliases={3: 0}, compiler_params=_params(), name=name)(me, own, recv, into)


def _swap_sibling(parts):
    n = len(parts)

    def body(*refs):
        srcs, outs = refs[:n], refs[n:2 * n]
        send_sems, recv_sems = refs[2 * n:]
        x, y, c = lax.axis_index("x"), lax.axis_index("y"), lax.axis_index("c")
        cps = [pltpu.make_async_remote_copy(src_ref=srcs[g], dst_ref=outs[g], send_sem=send_sems.at[g],
                                            recv_sem=recv_sems.at[g], device_id=(x, y, 1 - c), device_id_type=MESH)
               for g in range(n)]
        for cp in cps:
            cp.start()
        for cp in cps:
            cp.wait()

    return pl.pallas_call(
        body, out_shape=tuple(jax.ShapeDtypeStruct(p.shape, p.dtype) for p in parts),
        in_specs=[_ANY] * n, out_specs=(_ANY,) * n,
        scratch_shapes=[pltpu.SemaphoreType.DMA((n,)), pltpu.SemaphoreType.DMA((n,))],
        compiler_params=pltpu.CompilerParams(has_side_effects=True), name="swap_core_grads")(*parts)


def _flip(coord, bit):
    return 1 - coord if bit else coord


def _allreduce_small(v):
    n, C = v.shape

    def body(v_ref, o_ref, buf, send_sems, recv_sems):
        x, y, c = lax.axis_index("x"), lax.axis_index("y"), lax.axis_index("c")
        me = 4 * x + 2 * y + c
        buf[me] = v_ref[...]
        peers = [(_flip(x, (j >> 2) & 1), _flip(y, (j >> 1) & 1), _flip(c, j & 1)) for j in range(1, 8)]
        sends = [pltpu.make_async_remote_copy(src_ref=v_ref, dst_ref=buf.at[me], send_sem=send_sems.at[j],
                                              recv_sem=recv_sems.at[j], device_id=peer, device_id_type=MESH)
                 for j, peer in enumerate(peers)]
        for cp in sends:
            cp.start()
        for j, (px, py, pc) in enumerate(peers):
            pltpu.make_async_remote_copy(src_ref=v_ref, dst_ref=buf.at[4 * px + 2 * py + pc], send_sem=send_sems.at[j],
                                         recv_sem=recv_sems.at[j], device_id=(px, py, pc),
                                         device_id_type=MESH).wait_recv()
        for cp in sends:
            cp.wait_send()
        acc = buf[0]
        for d in range(1, 8):
            acc = acc + buf[d]
        o_ref[...] = acc

    return pl.pallas_call(
        body, out_shape=jax.ShapeDtypeStruct((n, C), F32),
        in_specs=[pl.BlockSpec(memory_space=pltpu.VMEM)], out_specs=pl.BlockSpec(memory_space=pltpu.VMEM),
        scratch_shapes=[pltpu.VMEM((8, n, C), F32), pltpu.SemaphoreType.DMA((7,)), pltpu.SemaphoreType.DMA((7,))],
        compiler_params=pltpu.CompilerParams(has_side_effects=True), name="allreduce_small")(v)


def _adamw(w, m, v, gs, *, row0=0, name):
    shape = w.shape
    w3, m3, v3 = (a.reshape((-1,) + a.shape[-2:]) for a in (w, m, v))
    g2 = [_rows2d(g) for g in gs]
    L, rows, C = w3.shape
    tr = _row_tile(math.gcd(rows, row0), C, budget=1024 * 1024)
    ng = len(g2)
    blk0, nb = row0 // tr, rows // tr

    def body(*refs):
        w_ref, m_ref, v_ref = refs[:3]
        g = refs[3][...]
        for r in refs[4:3 + ng]:
            g = g + r[...]
        g_ref, d_ref, nm_ref, nv_ref = refs[3 + ng:]
        mn = B1 * m_ref[...] + (1.0 - B1) * g
        vn = B2 * v_ref[...] + (1.0 - B2) * (g * g)
        m_hat = mn / (1.0 - B1 ** STEP)
        v_hat = vn / (1.0 - B2 ** STEP)
        g_ref[...] = g
        d_ref[...] = -LR * (m_hat / (jnp.sqrt(v_hat) + EPS) + WD * w_ref[...])
        nm_ref[...] = mn
        nv_ref[...] = vn

    spec = pl.BlockSpec((None, tr, C), lambda l, i: (l, i, 0))
    gspec = pl.BlockSpec((tr, C), lambda l, i: (blk0 + l * nb + i, 0))
    outs = pl.pallas_call(body, out_shape=tuple(jax.ShapeDtypeStruct((L, rows, C), F32) for _ in range(4)),
                          grid=(L, nb), in_specs=[spec] * 3 + [gspec] * ng, out_specs=(spec,) * 4,
                          compiler_params=_params(), name=name)(w3, m3, v3, *g2)
    return tuple(o.reshape(shape) for o in outs)


def _small_layout(shapes):
    offs, off = {}, 0
    for k in REPLICATED:
        n = int(np.prod(shapes[k]))
        offs[k] = (off, n)
        off += -(-n // LANES) * LANES
    return offs, off


def _pack_small(d, shapes):
    offs, total = _small_layout(shapes)
    parts = []
    for k in REPLICATED:
        n = offs[k][1]
        parts.append(jnp.pad(d[k].reshape(-1).astype(F32), (0, -(-n // LANES) * LANES - n)))
    rows = -(-(total // LANES) // 8) * 8
    return jnp.pad(jnp.concatenate(parts), (0, rows * LANES - total)).reshape(rows, LANES)


def _unpack_small(a, shapes):
    offs, _ = _small_layout(shapes)
    flat = a.reshape(-1)
    return {k: flat[offs[k][0]:offs[k][0] + offs[k][1]].reshape(shapes[k]) for k in REPLICATED}


def _group_shards(w):
    shards = []
    for members, _ in GROUPS.values():
        s = jnp.concatenate([w[k].astype(_WIRE) for k in members], axis=0)
        shards.append(s.reshape(s.shape[0], 2, s.shape[1] // 2, s.shape[2]))
    return shards


class _Exchanged:
    def __init__(self, w):
        self.small = {k: w[k] for k in REPLICATED}
        self.ctx = _Sched()
        shards = dict(zip(GROUPS, _group_shards(w)))
        self.gather, self.cache, self.scatter = {}, {}, {}
        chunks = _pieces_in_order()
        self.me = jnp.reshape(2 * lax.axis_index("x") + lax.axis_index("y"), (1,)).astype(jnp.int32)
        first = {(g, l): _gather_task1(shards[g], l, _place_own(shards[g], l, self.me, name=f"own_{g}{l}"))
                 for chunk in chunks for g, l in chunk}
        self.gather = {p: _gather_task2(task) for p, task in first.items()}
        pieces = [p for chunk in chunks for p in chunk]
        for n, p in enumerate(pieces):
            self.ctx.push(first[p])
            if n:
                self.ctx.push(self.gather[pieces[n - 1]])
        self.ctx.push(self.gather[pieces[-1]])

    def weight(self, group, l):
        if (group, l) not in self.cache:
            task = self.gather[group, l]
            self.ctx.finish(task)
            F = task.result()
            self.cache[group, l] = _consumer_layout(group, F.reshape(4, -1, F.shape[-1]))
        return self.cache[group, l]

    def grad(self, group, l, G):
        self.scatter[group, l] = (G, self.ctx.push(_scatter_task(G)))

    def partial_sums(self, w):
        self.ctx.finish()
        parts = []
        for group, (members, _) in GROUPS.items():
            rows, C = w[members[0]].shape[1:]
            n = sum(w[k].shape[0] for k in members)
            part = jnp.zeros((n * rows, C), F32)
            for l in range(n):
                G, task = self.scatter[group, l]
                part = _sum_slots(G, task.result(), self.me, part, l * rows, name=f"sum_{group}{l}")
            parts.append(part)
        return parts


def _reduce_and_update(w, m, v, parts, small):
    others = _swap_sibling(parts)
    small_shapes = {k: w[k].shape for k in REPLICATED}
    g_small = _unpack_small(_allreduce_small(_pack_small(small, small_shapes)), small_shapes)

    out_g, out_d, out_m, out_v = {}, {}, {}, {}
    for (members, _), part, other in zip(GROUPS.values(), parts, others):
        row0 = 0
        for k in members:
            out_g[k], out_d[k], out_m[k], out_v[k] = _adamw(w[k], m[k], v[k], [part, other], row0=row0,
                                                            name="adamw_" + k)
            row0 += w[k].shape[0] * w[k].shape[1]
    sm = _adamw(_pack_small(w, small_shapes), _pack_small(m, small_shapes), _pack_small(v, small_shapes),
                [_pack_small(g_small, small_shapes)], name="adamw_small")
    for d, packed in zip((out_g, out_d, out_m, out_v), sm):
        d.update(_unpack_small(packed, small_shapes))
    return out_g, out_d, out_m, out_v


def kernel(x, ffn1_norm, ffn1_w_gu, ffn1_w_down, mix_norm, ffn2_norm, ffn2_w_gu, ffn2_w_down, nsa_w_in, nsa_ck_pe, nsa_ck_w1, nsa_ck_w2, nsa_cv_pe, nsa_cv_w1, nsa_cv_w2, nsa_w_out, swa_w_in, swa_sinks, swa_w_out, fox_w_in, fox_b_f, fox_w_out, final_norm, loss_target, m_ffn1_norm, m_ffn1_w_gu, m_ffn1_w_down, m_mix_norm, m_ffn2_norm, m_ffn2_w_gu, m_ffn2_w_down, m_nsa_w_in, m_nsa_ck_pe, m_nsa_ck_w1, m_nsa_ck_w2, m_nsa_cv_pe, m_nsa_cv_w1, m_nsa_cv_w2, m_nsa_w_out, m_swa_w_in, m_swa_sinks, m_swa_w_out, m_fox_w_in, m_fox_b_f, m_fox_w_out, m_final_norm, v_ffn1_norm, v_ffn1_w_gu, v_ffn1_w_down, v_mix_norm, v_ffn2_norm, v_ffn2_w_gu, v_ffn2_w_down, v_nsa_w_in, v_nsa_ck_pe, v_nsa_ck_w1, v_nsa_ck_w2, v_nsa_cv_pe, v_nsa_cv_w1, v_nsa_cv_w2, v_nsa_w_out, v_swa_w_in, v_swa_sinks, v_swa_w_out, v_fox_w_in, v_fox_b_f, v_fox_w_out, v_final_norm):
    args = dict(locals())
    w = {k: args[k] for k in WEIGHTS}
    m = {k: args["m_" + k] for k in WEIGHTS}
    v = {k: args["v_" + k] for k in WEIGHTS}
    P = _Exchanged(w)
    loss_part, dx, small = _local_step(x[0], loss_target[0], P)
    loss = lax.psum(loss_part, ("x", "y", "c"))
    out_g, out_d, out_m, out_v = _reduce_and_update(w, m, v, P.partial_sums(w), small)
    return (loss, dx[None], *[out_g[k] for k in WEIGHTS], *[out_d[k] for k in WEIGHTS],
            *[out_m[k] for k in WEIGHTS], *[out_v[k] for k in WEIGHTS])
```

```python
import math

import numpy as np
import jax
import jax.numpy as jnp
from jax import lax
from jax.experimental import pallas as pl
from jax.experimental.pallas import tpu as pltpu

F32 = jnp.float32
_CDT = jnp.bfloat16
_WIRE = jnp.bfloat16
_VMEM_LIMIT = 56 * 1024 * 1024

D_MODEL = 1024
DEPTH = 4
DH = 64
N_HEADS = 16
RMS_EPS = 1e-6
NEG = -1e30
SCALE = DH ** -0.5
BQ = 256
LANES = 128
NSA_G, NSA_R = 4, 4
NSA_WINDOW = 512
NSA_TOPK = 16
NSA_BONUS = 1e4
SWA_G, SWA_R = 2, 8
SWA_WINDOW = 128
NSA_ROTATED = tuple(hd < 16 or (hd < 40 and (hd - 16) % 8 < 4) for hd in range(42))
NSA_IN, NSA_IN_PAD = 2608, 2688
SWA_IN = 1280
SWA_ROTATED = (True,) * 18 + (False,) * 2
SWA_HEADS = (0, 16, 18)
FOX_IN, FOX_IN_PAD = 3088, 3200
FOX_HEADS = (0, 16, 32)
LR, B1, B2, EPS, WD, STEP = 0.001, 0.9, 0.999, 1e-08, 0.01, 10
MESH = pl.DeviceIdType.MESH

REPLICATED = ["ffn1_norm", "mix_norm", "ffn2_norm", "nsa_ck_pe", "nsa_ck_w2", "nsa_cv_pe", "nsa_cv_w2",
              "swa_sinks", "fox_b_f", "final_norm"]
WEIGHTS = ['ffn1_norm', 'ffn1_w_gu', 'ffn1_w_down', 'mix_norm', 'ffn2_norm', 'ffn2_w_gu', 'ffn2_w_down',
           'nsa_w_in', 'nsa_ck_pe', 'nsa_ck_w1', 'nsa_ck_w2', 'nsa_cv_pe', 'nsa_cv_w1', 'nsa_cv_w2', 'nsa_w_out',
           'swa_w_in', 'swa_sinks', 'swa_w_out', 'fox_w_in', 'fox_b_f', 'fox_w_out', 'final_norm']


def _params(**kw):
    return pltpu.CompilerParams(vmem_limit_bytes=_VMEM_LIMIT, **kw)


def _dot(a, b):
    return lax.dot_general(a, b, (((1,), (0,)), ((), ())), preferred_element_type=F32)


def _dot_nt(a, b):
    return lax.dot_general(a, b, (((1,), (1,)), ((), ())), preferred_element_type=F32)


def _dot_tn(a, b):
    return lax.dot_general(a, b, (((0,), (0,)), ((), ())), preferred_element_type=F32)


def _split3(x):
    hi = x.astype(jnp.bfloat16)
    r1 = x - hi.astype(F32)
    mid = r1.astype(jnp.bfloat16)
    lo = (r1 - mid.astype(F32)).astype(jnp.bfloat16)
    return hi, mid, lo


def _dot_exact(x, p):
    hi, mid, lo = _split3(x)
    return _dot(hi, p) + _dot(mid, p) + _dot(lo, p)


def _dot_exact_left(p, x):
    hi, mid, lo = _split3(x)
    return _dot(p, hi) + _dot(p, mid) + _dot(p, lo)


_ANY = pl.BlockSpec(memory_space=pl.ANY)


def _chip_peers():
    x, y, c = lax.axis_index("x"), lax.axis_index("y"), lax.axis_index("c")
    return x, y, c, [(1 - x, y), (x, 1 - y), (1 - x, 1 - y)]


class _Job:
    def __init__(self, ins, outs, nsem, copies, cost, alias):
        self.ins, self.outs, self.nsem, self.copies, self.cost, self.alias = ins, outs, nsem, copies, cost, alias
        self.results = None

    def inputs(self):
        return self.ins


class _Task:
    def __init__(self, src, out, parts, init=None):
        self.src, self.out, self.parts, self.init, self.done, self.jobs = src, out, parts, init, 0, []

    def result(self):
        return self.jobs[-1].results[0]

    def complete(self):
        return self.done == len(self.parts) and self.jobs[-1].results is not None

    def ready(self):
        if isinstance(self.src, _Task) and not self.src.complete():
            return False
        return not self.jobs or self.jobs[-1].results is not None

    def next_job(self, n):
        parts = self.parts[self.done:self.done + n]
        in_place = isinstance(self.src, _Task)
        ins = [] if in_place else [self.src]
        prev = self.jobs[-1].results[0] if self.jobs else (self.src.result() if in_place else self.init)
        alias = {}
        if prev is not None:
            alias = {len(ins): 0}
            ins = ins + [prev]

        def copies(in_refs, out_refs, sems):
            src = out_refs[0] if in_place else in_refs[0]
            out, first = [], 0
            for build, nsem, _ in parts:
                out += build(src, out_refs[0], sems, first)
                first += nsem
            return out

        job = _Job(ins, [self.out], sum(p[1] for p in parts), copies, sum(p[2] for p in parts), alias)
        self.done += n
        self.jobs.append(job)
        return job


def _pcall(body, *, out_shape, in_specs, out_specs, name, grid=(), scratch_shapes=(), aliases=None, jobs=(), ins):
    single = not isinstance(out_shape, (tuple, list))
    out_shape = (out_shape,) if single else tuple(out_shape)
    out_specs = (out_specs,) if single else tuple(out_specs)
    n_in, n_out, n_scr = len(ins), len(out_shape), len(scratch_shapes)
    side_ins = [a for j in jobs for a in j.inputs()]
    side_outs = [o for j in jobs for o in j.outs]
    aliases = dict(aliases or {})
    i0, o0 = n_in, n_out
    for j in jobs:
        for a, b in j.alias.items():
            aliases[i0 + a] = o0 + b
        i0 += len(j.ins)
        o0 += len(j.outs)

    def body2(*refs):
        in_refs, s_in = refs[:n_in], refs[n_in:n_in + len(side_ins)]
        r = n_in + len(side_ins)
        out_refs, s_out = refs[r:r + n_out], refs[r + n_out:r + n_out + len(side_outs)]
        r += n_out + len(side_outs)
        scr, sems = refs[r:r + n_scr], refs[r + n_scr:]
        def descriptors():
            out, a, b = [], 0, 0
            for j, sem in zip(jobs, sems):
                out += j.copies(s_in[a:a + len(j.ins)], s_out[b:b + len(j.outs)], sem)
                a += len(j.ins)
                b += len(j.outs)
            return out

        def at_step(steps, fn):
            cond = None
            for d, s in enumerate(steps):
                hit = pl.program_id(d) == s
                cond = hit if cond is None else jnp.logical_and(cond, hit)
            if cond is None:
                fn()
            else:
                pl.when(cond)(fn)

        def start_all():
            for cp in descriptors():
                cp.start()

        def wait_all():
            for cp in descriptors():
                cp.wait()

        if jobs:
            at_step([0] * len(grid), start_all)
        body(*in_refs, *out_refs, *scr)
        if jobs:
            at_step([n - 1 for n in grid], wait_all)

    res = pl.pallas_call(
        body2, out_shape=out_shape + tuple(side_outs), grid=grid, in_specs=list(in_specs) + [_ANY] * len(side_ins),
        out_specs=out_specs + (_ANY,) * len(side_outs),
        scratch_shapes=list(scratch_shapes) + [pltpu.SemaphoreType.DMA((j.nsem,)) for j in jobs],
        input_output_aliases=aliases,
        compiler_params=_params(has_side_effects=True) if jobs else _params(), name=name)(*ins, *side_ins)
    b = n_out
    for j in jobs:
        j.results = list(res[b:b + len(j.outs)])
        b += len(j.outs)
    return res[0] if single else tuple(res[:n_out])


def _comm_call(jobs, name):
    def body():
        pass

    _pcall(body, out_shape=(), in_specs=[], out_specs=(), name=name, jobs=jobs, ins=[])


def _place_own(shard, l, me, *, name):
    _, _, h, C = shard.shape

    def body(_, s_ref, o_ref):
        o_ref[...] = s_ref[...]

    spec = pltpu.PrefetchScalarGridSpec(
        num_scalar_prefetch=1, grid=(2,),
        in_specs=[pl.BlockSpec((None, None, h, C), lambda i, me_ref: (l, i, 0, 0))],
        out_specs=pl.BlockSpec((None, None, h, C), lambda i, me_ref: (me_ref[0], i, 0, 0)))
    return pl.pallas_call(body, grid_spec=spec, out_shape=jax.ShapeDtypeStruct((4, 2, h, C), shard.dtype),
                          compiler_params=_params(), name=name)(me, shard)


def _gather_task1(shard, l, own):
    _, _, h, C = shard.shape
    mb = 3 * h * C * jnp.dtype(_WIRE).itemsize / 1e6
    n = _row_splits(h, ICI_US_PER_MB * mb)
    hr = h // n

    def rows_to_peers(r):
        def build(src, dst, sems, s0):
            x, y, c, chips = _chip_peers()
            me, rows = 2 * x + y, pl.ds(r * hr, hr)
            return [pltpu.make_async_remote_copy(src_ref=src.at[l, c, rows], dst_ref=dst.at[me, c, rows],
                                                 send_sem=sems.at[s0 + 2 * j], recv_sem=sems.at[s0 + 2 * j + 1],
                                                 device_id=(px, py, c), device_id_type=MESH)
                    for j, (px, py) in enumerate(chips)]
        return build

    parts = [(rows_to_peers(r), 6, ICI_US_PER_MB * mb / n) for r in range(n)]
    return _Task(shard, jax.ShapeDtypeStruct((4, 2, h, C), shard.dtype), parts, init=own)


def _gather_task2(task1):
    _, _, h, C = task1.out.shape
    mb = h * C * jnp.dtype(_WIRE).itemsize / 1e6

    def forward(j):
        def build(_, dst, sems, s0):
            x, y, c, chips = _chip_peers()
            px, py = chips[j]
            landed = dst.at[2 * px + py, c]
            return [pltpu.make_async_remote_copy(src_ref=landed, dst_ref=landed, send_sem=sems.at[s0],
                                                 recv_sem=sems.at[s0 + 1], device_id=(x, y, 1 - c),
                                                 device_id_type=MESH)]
        return build

    return _Task(task1, task1.out, [(forward(j), 2, 1.0 + D2D_US_PER_MB * mb) for j in range(3)])


def _scatter_task(G):
    mb = 3 * G.shape[1] * G.shape[2] * jnp.dtype(_WIRE).itemsize / 1e6
    n = _row_splits(G.shape[1], ICI_US_PER_MB * mb)
    rr = G.shape[1] // n

    def rows_to_peers(r):
        def build(src, dst, sems, s0):
            _, _, c, chips = _chip_peers()
            rows = pl.ds(r * rr, rr)
            return [pltpu.make_async_remote_copy(src_ref=src.at[2 * px + py, rows], dst_ref=dst.at[j, rows],
                                                 send_sem=sems.at[s0 + 2 * j], recv_sem=sems.at[s0 + 2 * j + 1],
                                                 device_id=(px, py, c), device_id_type=MESH)
                    for j, (px, py) in enumerate(chips)]
        return build

    parts = [(rows_to_peers(r), 6, ICI_US_PER_MB * mb / n) for r in range(n)]
    return _Task(G, jax.ShapeDtypeStruct((3,) + G.shape[1:], G.dtype), parts)


def _row_splits(rows, cost):
    n = 8
    while n > 1 and (rows % (16 * n) or cost / n < PART_US):
        n //= 2
    return n


CARRIER_US = {
    "f_gu": 48, "f_down": 20, "f_dact": 42, "f_dwgu": 42, "f_dh": 46,
    "nsa_in": 25, "nsa_cmp": 86, "nsa_slc": 122, "nsa_win": 74, "nsa_dcmp": 58, "nsa_dslc": 195, "nsa_dwin": 108,
    "nsa_dh": 26, "nsa_dwin_w": 25,
    "swa_in": 16, "swa_attn": 74, "swa_dattn": 74, "swa_dh": 17, "swa_dwin_w": 18,
    "fox_in": 34, "fox_attn": 104, "fox_dattn": 161, "fox_dh": 30, "fox_dwin_w": 30,
}
ICI_US_PER_MB = 15.0
PART_US = 12.0
D2D_US_PER_MB = 2.3
LOCAL_US_PER_MB = 1.5


class _Sched:
    def __init__(self):
        self.queue, self.credit, self.n_flush = [], 0.0, 0

    def push(self, task):
        self.queue.append(task)
        return task

    def _jobs(self, fits):
        jobs = []
        while self.queue and self.queue[0].ready():
            task, n = self.queue[0], 0
            while task.done + n < len(task.parts) and fits(task.parts[task.done + n][2]):
                n += 1
            if n:
                jobs.append(task.next_job(n))
            if task.done < len(task.parts):
                break
            self.queue.pop(0)
        return jobs

    def take(self, name):
        kind = name[2:].split("_", 1)
        key = ("f" if kind[0] in ("f1", "f2") else kind[0]) + "_" + kind[1]
        self.credit = min(self.credit, 0.0) + CARRIER_US.get(key, 0.0)

        def fits(cost):
            if self.credit < 0.5 * cost:
                return False
            self.credit -= cost
            return True

        return self._jobs(fits)

    def finish(self, task=None):
        while self.queue and (task is None or not task.complete()):
            jobs = []
            while self.queue and self.queue[0].ready():
                head = self.queue.pop(0)
                jobs.append(head.next_job(len(head.parts) - head.done))
                if head is task:
                    break
            assert jobs, "the task at the head of the queue waits for one that was never queued"
            _comm_call(jobs, name=f"exchange_{self.n_flush}")
            self.n_flush += 1


def _mm(a, b, *, ta=False, tb=False, tm, tn, b_lead=(), into=None, o_lead=(), out_dtype=F32, res=None, alpha=1.0,
        ctx=None, name):
    M = a.shape[1] if ta else a.shape[0]
    K = a.shape[0] if ta else a.shape[1]
    bk, bn = (b.shape[-1], b.shape[-2]) if tb else (b.shape[-2], b.shape[-1])
    j_lead, k_lead = "j" in b_lead, "k" in b_lead
    N = bn * (b.shape[b_lead.index("j")] if j_lead else 1)
    nk = b.shape[b_lead.index("k")] if k_lead else 1
    tk = K // nk
    assert tk == bk, (name, K, nk, bk)
    tm = min(tm, M)
    tn = bn if j_lead else min(tn, N)
    assert M % tm == 0 and N % tn == 0, (name, M, N, tm, tn)
    nb, no = len(b_lead), len(o_lead)

    def pick(lead, j, k):
        return tuple(j if t == "j" else k if t == "k" else t for t in lead)

    a_spec = pl.BlockSpec((tk, tm), lambda i, j, k: (k, i)) if ta else pl.BlockSpec((tm, tk), lambda i, j, k: (i, k))
    if tb:
        b_spec = pl.BlockSpec((None,) * nb + (tn, tk),
                              lambda i, j, k: pick(b_lead, j, k) + (0 if j_lead else j, 0 if k_lead else k))
    else:
        b_spec = pl.BlockSpec((None,) * nb + (tk, tn),
                              lambda i, j, k: pick(b_lead, j, k) + (0 if k_lead else k, 0 if j_lead else j))
    r_spec = pl.BlockSpec((tm, tn), lambda i, j, k: (i, j))
    if o_lead == ("heads",):
        o_spec = pl.BlockSpec((tn // DH, tm, DH), lambda i, j, k: (j, i, 0))
    else:
        o_spec = pl.BlockSpec((None,) * no + (tm, tn),
                              lambda i, j, k: pick(o_lead, j, k) + (i, 0 if "j" in o_lead else j))
    dn = (((0 if ta else 1,), (1 if tb else 0,)), ((), ()))
    has_res, has_into = res is not None, into is not None
    if has_into:
        out_dtype = into.dtype

    def body(*refs):
        a_ref, b_ref = refs[0], refs[1]
        r_ref = refs[2] if has_res else None
        o_ref = refs[2 + has_res + has_into]
        prod = lax.dot_general(a_ref[...].astype(_CDT), b_ref[...].astype(_CDT), dn, preferred_element_type=F32)

        def finish(acc):
            if alpha != 1.0:
                acc = acc * alpha
            if has_res:
                acc = r_ref[...] + acc
            if o_lead == ("heads",):
                for hd in range(tn // DH):
                    o_ref[hd] = acc[:, hd * DH:(hd + 1) * DH].astype(out_dtype)
            else:
                o_ref[...] = acc.astype(out_dtype)

        if nk == 1:
            finish(prod)
        else:
            acc_ref = refs[-1]
            k = pl.program_id(2)

            @pl.when(k == 0)
            def _():
                acc_ref[...] = prod

            @pl.when(k != 0)
            def _():
                acc_ref[...] += prod

            @pl.when(k == nk - 1)
            def _():
                finish(acc_ref[...])

    ins, specs = [a, b], [a_spec, b_spec]
    if has_res:
        ins.append(res)
        specs.append(r_spec)
    aliases = {}
    if has_into:
        aliases = {len(ins): 0}
        ins.append(into)
        specs.append(_ANY)
        out_shape = jax.ShapeDtypeStruct(into.shape, into.dtype)
    elif o_lead == ("j",):
        out_shape = jax.ShapeDtypeStruct((N // tn, M, tn), out_dtype)
    elif o_lead == ("heads",):
        out_shape = jax.ShapeDtypeStruct((N // DH, M, DH), out_dtype)
    else:
        assert not o_lead
        out_shape = jax.ShapeDtypeStruct((M, N), out_dtype)
    scratch = [pltpu.VMEM((tm, tn), F32)] if nk > 1 else []
    return _pcall(body, out_shape=out_shape, grid=(M // tm, N // tn, nk), in_specs=specs, out_specs=o_spec,
                  scratch_shapes=scratch, aliases=aliases, name=name, jobs=ctx.take(name) if ctx else (), ins=ins)


def _rows2d(a):
    return a.reshape(-1, a.shape[-1])


def _row_tile(rows, cols, itemsize=4, budget=2 * 1024 * 1024):
    t = rows
    while t % 2 == 0 and t * cols * itemsize > budget and (t // 2) % 8 == 0:
        t //= 2
    return t


def _addn(*xs, name):
    shape = xs[0].shape
    x2 = [_rows2d(x) for x in xs]
    R, C = x2[0].shape
    tr = _row_tile(R, C)

    def body(*refs):
        acc = refs[0][...]
        for r in refs[1:-1]:
            acc = acc + r[...]
        refs[-1][...] = acc

    spec = pl.BlockSpec((tr, C), lambda i: (i, 0))
    out = pl.pallas_call(body, out_shape=jax.ShapeDtypeStruct((R, C), F32), grid=(R // tr,),
                         in_specs=[spec] * len(x2), out_specs=spec, compiler_params=_params(), name=name)(*x2)
    return out.reshape(shape)


def _rms_fwd(x, g, *, name):
    S, D = x.shape
    tr = 256

    def body(x_ref, g_ref, h_ref):
        xv = x_ref[...]
        rstd = lax.rsqrt(jnp.mean(xv * xv, axis=-1, keepdims=True) + RMS_EPS)
        h_ref[...] = (xv * rstd * g_ref[...]).astype(_CDT)

    return pl.pallas_call(body, out_shape=jax.ShapeDtypeStruct((S, D), _CDT), grid=(S // tr,),
                          in_specs=[pl.BlockSpec((tr, D), lambda i: (i, 0)), pl.BlockSpec((1, D), lambda i: (0, 0))],
                          out_specs=pl.BlockSpec((tr, D), lambda i: (i, 0)), compiler_params=_params(),
                          name=name)(x, g.reshape(1, D))


def _rms_bwd(dh, x, g, dres, *, name):
    S, D = x.shape
    tr = 256

    def body(dh_ref, x_ref, g_ref, dres_ref, dx_ref, dg_ref):
        xv = x_ref[...]
        rstd = lax.rsqrt(jnp.mean(xv * xv, axis=-1, keepdims=True) + RMS_EPS)
        xhat = xv * rstd
        dhv = dh_ref[...]
        dxhat = dhv * g_ref[...]
        dx_ref[...] = dres_ref[...] + rstd * (dxhat - xhat * jnp.mean(dxhat * xhat, axis=-1, keepdims=True))

        @pl.when(pl.program_id(0) == 0)
        def _():
            dg_ref[...] = jnp.zeros_like(dg_ref)

        dg_ref[...] += jnp.sum(dhv * xhat, axis=0, keepdims=True)

    row = pl.BlockSpec((tr, D), lambda i: (i, 0))
    vec = pl.BlockSpec((1, D), lambda i: (0, 0))
    dx, dg = pl.pallas_call(body, out_shape=(jax.ShapeDtypeStruct((S, D), F32), jax.ShapeDtypeStruct((1, D), F32)),
                            grid=(S // tr,), in_specs=[row, row, vec, row], out_specs=(row, vec),
                            compiler_params=_params(), name=name)(dh, x, g.reshape(1, D), dres)
    return dx, dg.reshape(D)


def _loss_head(x, g, tgt, *, name):
    S, D = x.shape
    tr = 256

    def body(x_ref, g_ref, t_ref, loss_ref, dx_ref, dg_ref):
        xv = x_ref[...]
        rstd = lax.rsqrt(jnp.mean(xv * xv, axis=-1, keepdims=True) + RMS_EPS)
        xhat = xv * rstd
        err = xhat * g_ref[...] - t_ref[...]
        part = 0.5 * jnp.sum(jnp.mean(err * err, axis=-1, keepdims=True), axis=0, keepdims=True)
        dy = err * (1.0 / D)
        dxhat = dy * g_ref[...]
        dx_ref[...] = rstd * (dxhat - xhat * jnp.mean(dxhat * xhat, axis=-1, keepdims=True))

        @pl.when(pl.program_id(0) == 0)
        def _():
            dg_ref[...] = jnp.zeros_like(dg_ref)
            loss_ref[...] = jnp.zeros_like(loss_ref)

        dg_ref[...] += jnp.sum(dy * xhat, axis=0, keepdims=True)
        loss_ref[...] += jnp.broadcast_to(part, loss_ref.shape)

    row = pl.BlockSpec((tr, D), lambda i: (i, 0))
    vec = pl.BlockSpec((1, D), lambda i: (0, 0))
    loss, dx, dg = pl.pallas_call(
        body, out_shape=(jax.ShapeDtypeStruct((8, LANES), F32), jax.ShapeDtypeStruct((S, D), F32),
                         jax.ShapeDtypeStruct((1, D), F32)),
        grid=(S // tr,), in_specs=[row, vec, row], out_specs=(pl.BlockSpec((8, LANES), lambda i: (0, 0)), row, vec),
        compiler_params=_params(), name=name)(x, g.reshape(1, D), tgt)
    return loss[0, 0], dx, dg.reshape(D)


def _rope_tables(S):
    inv = 10000.0 ** (-jnp.arange(0, DH, 2, dtype=F32) / DH)
    ang = jnp.arange(S, dtype=F32)[:, None] * inv[None, :]
    cos, sin = jnp.cos(ang), jnp.sin(ang)
    return jnp.concatenate([cos, cos], -1), jnp.concatenate([-sin, sin], -1)


def _swap_matrix():
    p = np.zeros((DH, DH), np.float32)
    for j in range(DH // 2):
        p[j + DH // 2, j] = 1.0
        p[j, j + DH // 2] = 1.0
    return jnp.asarray(p, jnp.bfloat16)


def _rope(x, cc, ss, rotated, *, name):
    n, S, _ = x.shape
    assert len(rotated) == n
    starts = [i for i in range(n) if rotated[i] and (i == 0 or not rotated[i - 1])]
    tab = pl.BlockSpec((S, DH), lambda i: (0, 0))
    for k, first in enumerate(starts):
        count = next((i for i in range(first, n) if not rotated[i]), n) - first

        def body(x_ref, c_ref, s_ref, p_ref, o_ref):
            xv = x_ref[0]
            o_ref[0] = xv * c_ref[...] + _dot_exact(xv, p_ref[...]) * s_ref[...]

        blk = pl.BlockSpec((1, S, DH), lambda i, first=first: (first + i, 0, 0))
        x = pl.pallas_call(body, out_shape=jax.ShapeDtypeStruct(x.shape, F32), grid=(count,),
                           in_specs=[blk, tab, tab, pl.BlockSpec((DH, DH), lambda i: (0, 0))], out_specs=blk,
                           input_output_aliases={0: 0}, compiler_params=_params(),
                           name=f"{name}{k}")(x, cc, ss, _swap_matrix())
    return x


def _key_range(kind, i, window, Sk):
    if kind == "cmp":
        return 0, Sk
    hi = (i + 1) * BQ
    if kind == "band":
        return max(0, i * BQ - window), hi
    return 0, hi


def _attn_mask(kind, i, lo, hi, window):
    shape = (BQ, hi - lo)
    qpos = i * BQ + lax.broadcasted_iota(jnp.int32, shape, 0)
    kpos = lo + lax.broadcasted_iota(jnp.int32, shape, 1)
    if kind == "cmp":
        return kpos * 16 + 31 <= qpos
    mask = kpos <= qpos
    if kind == "band":
        mask = mask & (qpos - kpos < window)
    return mask


def _sel_expand(n_slc, n_keys):
    shape = (n_slc, n_keys)
    j = lax.broadcasted_iota(jnp.int32, shape, 0)
    key = lax.broadcasted_iota(jnp.int32, shape, 1)
    return (jnp.right_shift(key, 6) == j).astype(_CDT)


def _eye():
    return lax.broadcasted_iota(jnp.int32, (BQ, BQ), 0) == lax.broadcasted_iota(jnp.int32, (BQ, BQ), 1)


def _to_col(row):
    return jnp.sum(jnp.where(_eye(), row, 0.0), axis=1, keepdims=True)


def _to_row(col):
    return jnp.sum(jnp.where(_eye(), col, 0.0), axis=0, keepdims=True)


def _scores(kind, i, lo, hi, window, qb, kb, crow_ref, sel_ref):
    s = _dot_nt(qb, kb)
    if crow_ref is not None:
        s = s + _to_col(crow_ref[0, :, i * BQ:(i + 1) * BQ]) - crow_ref[0, :, lo:hi]
    mask = _attn_mask(kind, i, lo, hi, window)
    if sel_ref is not None:
        mask = mask & (sel_ref[0, i * BQ:(i + 1) * BQ, lo:hi].astype(F32) > 0.5)
    return jnp.where(mask, s, NEG), mask


def _attn_fwd(q, k, v, *, G, R, heads=(0, 0, 0), kind, window=0, bias=None, sinks=None, selT=None, ovT=None, ctx=None,
              name):
    S = q.shape[1]
    Sk = k.shape[1]
    q0, k0, v0 = heads
    nq = S // BQ
    n_slc = S // 64
    has_bias, has_sink, has_sel, is_cmp = bias is not None, sinks is not None, selT is not None, kind == "cmp"

    def body(*refs):
        it = iter(refs)
        q_ref, k_ref, v_ref = next(it), next(it), next(it)
        crow_ref = next(it) if has_bias else None
        sink_ref = next(it) if has_sink else None
        sel_ref = next(it) if has_sel else None
        ov_ref = next(it) if is_cmp else None
        o_ref, lse_ref = next(it), next(it)
        selo_ref, imp_ref = (next(it), next(it)) if is_cmp else (None, None)
        r = pl.program_id(1)
        for i in range(nq):
            lo, hi = _key_range(kind, i, window, Sk)
            rows = slice(i * BQ, (i + 1) * BQ)
            qb = (q_ref[rows, :] * SCALE).astype(_CDT)
            kb = k_ref[0, lo:hi, :].astype(_CDT)
            vb = v_ref[0, lo:hi, :].astype(_CDT)
            s, mask = _scores(kind, i, lo, hi, window, qb, kb, crow_ref, sel_ref)
            m = jnp.max(s, axis=-1, keepdims=True)
            if has_sink:
                sk = sink_ref[0, 0, 0:1, 0:1]
                m = jnp.maximum(m, sk)
            e = jnp.exp(s - m)
            if is_cmp:
                e = jnp.where(mask, e, 0.0)
            l = jnp.sum(e, axis=-1, keepdims=True)
            if has_sink:
                l = l + jnp.exp(sk - m)
            if is_cmp:
                l = jnp.where(l > 0.0, l, 1.0)
            p = e * (1.0 / l)
            o_ref[0, 0, rows, :] = _dot(p.astype(_CDT), vb)
            lse_ref[0, 0, :, rows] = _to_row(m + jnp.log(l))
            if is_cmp:
                part = _dot_nt(ov_ref[...].astype(_CDT), p.astype(_CDT))

                @pl.when(r == 0)
                def _():
                    imp_ref[:, rows] = part

                @pl.when(r != 0)
                def _():
                    imp_ref[:, rows] += part

        if is_cmp:
            @pl.when(r == R - 1)
            def _():
                shape = (n_slc, S)
                j = lax.broadcasted_iota(jnp.int32, shape, 0)
                tb = jnp.right_shift(lax.broadcasted_iota(jnp.int32, shape, 1), 6)
                forced = (j == 0) | (j == tb) | (j == tb - 1)
                imp = jnp.where(j > tb, NEG, jnp.where(forced, NSA_BONUS, imp_ref[0:n_slc, :]))
                imp_ref[0:n_slc, :] = imp
                cnt = jnp.zeros(shape, F32)
                for jp in range(n_slc):
                    row = imp_ref[jp:jp + 1, :]
                    ahead = (row > imp) | ((row == imp) & (jp < j))
                    cnt = cnt + ahead.astype(F32)
                imp_ref[0:n_slc, :] = (cnt < float(min(NSA_TOPK, n_slc))).astype(F32)
                expand = _sel_expand(n_slc, S)
                for i in range(nq):
                    rows = slice(i * BQ, (i + 1) * BQ)
                    chosen = _dot_tn(imp_ref[0:n_slc, rows].astype(_CDT), expand)
                    selo_ref[0, rows, :] = chosen.astype(jnp.bfloat16)

    qspec = pl.BlockSpec((1, 1, S, DH), lambda g, r: (g, r, 0, 0))
    ins = [q, k, v]
    specs = [pl.BlockSpec((None, S, DH), lambda g, r: (q0 + g * R + r, 0, 0)),
             pl.BlockSpec((1, Sk, DH), lambda g, r: (k0 + g, 0, 0)), pl.BlockSpec((1, Sk, DH), lambda g, r: (v0 + g, 0, 0))]
    if has_bias:
        ins.append(bias)
        specs.append(pl.BlockSpec((1, 1, S), lambda g, r: (g, 0, 0)))
    if has_sink:
        ins.append(sinks)
        specs.append(pl.BlockSpec((1, 1, 8, LANES), lambda g, r: (g, r, 0, 0)))
    if has_sel:
        ins.append(selT)
        specs.append(pl.BlockSpec((1, S, S), lambda g, r: (g, 0, 0)))
    if is_cmp:
        ins.append(ovT)
        specs.append(pl.BlockSpec((LANES, Sk), lambda g, r: (0, 0)))
    outs = [jax.ShapeDtypeStruct((G, R, S, DH), F32), jax.ShapeDtypeStruct((G, R, 1, S), F32)]
    ospecs = [qspec, pl.BlockSpec((1, 1, 1, S), lambda g, r: (g, r, 0, 0))]
    scratch = []
    if is_cmp:
        outs.append(jax.ShapeDtypeStruct((G, S, S), jnp.bfloat16))
        ospecs.append(pl.BlockSpec((1, S, S), lambda g, r: (g, 0, 0)))
        scratch.append(pltpu.VMEM((LANES, S), F32))
    return _pcall(body, out_shape=tuple(outs), grid=(G, R), in_specs=specs, out_specs=tuple(ospecs),
                  scratch_shapes=scratch, name=name, jobs=ctx.take(name) if ctx else (), ins=ins)


def _attn_bwd(q, k, v, o, do, lse, *, heads=(0, 0, 0), kind, window=0, bias=None, sinks=None, selT=None, gate=None,
              ctx=None, name):
    G, R, S, _ = o.shape
    Sk = k.shape[1]
    q0, k0, v0 = heads
    nq = S // BQ
    has_bias, has_sink, has_sel, has_gate = bias is not None, sinks is not None, selT is not None, gate is not None

    def body(*refs):
        it = iter(refs)
        q_ref, k_ref, v_ref, o_ref, do_ref, lse_ref = (next(it) for _ in range(6))
        crow_ref = next(it) if has_bias else None
        sink_ref = next(it) if has_sink else None
        sel_ref = next(it) if has_sel else None
        z_ref = next(it) if has_gate else None
        dq_ref, dk_ref, dv_ref = next(it), next(it), next(it)
        dc_ref = next(it) if has_bias else None
        dsink_ref = next(it) if has_sink else None
        dz_ref = next(it) if has_gate else None
        r = pl.program_id(1)

        @pl.when(r == 0)
        def _():
            dk_ref[...] = jnp.zeros_like(dk_ref)
            dv_ref[...] = jnp.zeros_like(dv_ref)
            if has_bias:
                dc_ref[...] = jnp.zeros_like(dc_ref)

        dsink = jnp.zeros((1, 1), F32)
        for i in range(nq):
            lo, hi = _key_range(kind, i, window, Sk)
            rows = slice(i * BQ, (i + 1) * BQ)
            qb = (q_ref[rows, :] * SCALE).astype(_CDT)
            kb = k_ref[0, lo:hi, :].astype(_CDT)
            vb = v_ref[0, lo:hi, :].astype(_CDT)
            s, mask = _scores(kind, i, lo, hi, window, qb, kb, crow_ref, sel_ref)
            lse_i = _to_col(lse_ref[0, 0, :, rows])
            p = jnp.where(mask, jnp.exp(s - lse_i), 0.0)
            dob = do_ref[0, 0, rows, :]
            if has_gate:
                od = jnp.sum(o_ref[0, 0, rows, :] * dob, axis=-1, keepdims=True)
                sg = jax.nn.sigmoid(_to_col(z_ref[0, 0, :, rows]))
                dob = dob * sg
                dz_ref[0, 0, :, rows] = _to_row(od * sg * (1.0 - sg))
            dob = dob.astype(_CDT)
            dp = _dot_nt(dob, vb)
            delta = jnp.sum(p * dp, axis=-1, keepdims=True)
            ds = p * (dp - delta)
            dsb = ds.astype(_CDT)
            dq_ref[0, 0, rows, :] = _dot(dsb, kb) * SCALE
            dk_ref[0, lo:hi, :] += _dot_tn(dsb, qb)
            dv_ref[0, lo:hi, :] += _dot_tn(p.astype(_CDT), dob)
            if has_bias:
                dc_ref[0, :, rows] += _to_row(jnp.sum(ds, axis=-1, keepdims=True))
                dc_ref[0, :, lo:hi] -= jnp.sum(ds, axis=0, keepdims=True)
            if has_sink:
                sk = sink_ref[0, 0, 0:1, 0:1]
                dsink = dsink - jnp.sum(jnp.exp(sk - lse_i) * delta, axis=0, keepdims=True)
        if has_sink:
            dsink_ref[0, 0] = jnp.broadcast_to(dsink, (8, LANES))

    qspec = pl.BlockSpec((1, 1, S, DH), lambda g, r: (g, r, 0, 0))
    cspec = pl.BlockSpec((1, 1, 1, S), lambda g, r: (g, r, 0, 0))
    kspec = pl.BlockSpec((1, Sk, DH), lambda g, r: (g, 0, 0))
    ins = [q, k, v, o, do, lse]
    specs = [pl.BlockSpec((None, S, DH), lambda g, r: (q0 + g * R + r, 0, 0)),
             pl.BlockSpec((1, Sk, DH), lambda g, r: (k0 + g, 0, 0)), pl.BlockSpec((1, Sk, DH), lambda g, r: (v0 + g, 0, 0)),
             qspec, qspec, cspec]
    if has_bias:
        ins.append(bias)
        specs.append(pl.BlockSpec((1, 1, S), lambda g, r: (g, 0, 0)))
    if has_sink:
        ins.append(sinks)
        specs.append(pl.BlockSpec((1, 1, 8, LANES), lambda g, r: (g, r, 0, 0)))
    if has_sel:
        ins.append(selT)
        specs.append(pl.BlockSpec((1, S, S), lambda g, r: (g, 0, 0)))
    if has_gate:
        ins.append(gate)
        specs.append(cspec)
    names = ["dq", "dk", "dv"]
    outs = [jax.ShapeDtypeStruct((G, R, S, DH), F32), jax.ShapeDtypeStruct((G, Sk, DH), F32),
            jax.ShapeDtypeStruct((G, Sk, DH), F32)]
    ospecs = [qspec, kspec, kspec]
    if has_bias:
        assert R == 1
        names.append("dc")
        outs.append(jax.ShapeDtypeStruct((G, 1, S), F32))
        ospecs.append(pl.BlockSpec((1, 1, S), lambda g, r: (g, 0, 0)))
    if has_sink:
        names.append("dsink")
        outs.append(jax.ShapeDtypeStruct((G, R, 8, LANES), F32))
        ospecs.append(pl.BlockSpec((1, 1, 8, LANES), lambda g, r: (g, r, 0, 0)))
    if has_gate:
        names.append("dz")
        outs.append(jax.ShapeDtypeStruct((G, R, 1, S), F32))
        ospecs.append(cspec)
    res = _pcall(body, out_shape=tuple(outs), grid=(G, R), in_specs=specs, out_specs=tuple(ospecs), name=name,
                 jobs=ctx.take(name) if ctx else (), ins=ins)
    return dict(zip(names, res))


def _combine(o0, o1, o2, z, *, name):
    H, S, _ = o0.shape

    def body(o0_ref, o1_ref, o2_ref, z_ref, o_ref):
        for i in range(S // BQ):
            rows = slice(i * BQ, (i + 1) * BQ)
            acc = jax.nn.sigmoid(_to_col(z_ref[0, 0, :, rows])) * o0_ref[0, rows, :]
            acc = acc + jax.nn.sigmoid(_to_col(z_ref[1, 0, :, rows])) * o1_ref[0, rows, :]
            acc = acc + jax.nn.sigmoid(_to_col(z_ref[2, 0, :, rows])) * o2_ref[0, rows, :]
            o_ref[0, rows, :] = acc

    blk = pl.BlockSpec((1, S, DH), lambda h: (h, 0, 0))
    return pl.pallas_call(body, out_shape=jax.ShapeDtypeStruct((H, S, DH), F32), grid=(H,),
                          in_specs=[blk, blk, blk, pl.BlockSpec((3, 1, 1, S), lambda h: (0, h, 0, 0))], out_specs=blk,
                          compiler_params=_params(), name=name)(o0, o1, o2, z)


_GC = math.sqrt(2.0 / math.pi)


def _gelu(x):
    return 0.5 * x * (1.0 + jnp.tanh(_GC * (x + 0.044715 * x * x * x)))


def _gelu_grad(x):
    t = jnp.tanh(_GC * (x + 0.044715 * x * x * x))
    return 0.5 * (1.0 + t) + 0.5 * x * (1.0 - t * t) * _GC * (1.0 + 3.0 * 0.044715 * x * x)


def _make_xb(k):
    G, S, _ = k.shape
    chunks = k.reshape(G, S // 16, 16 * DH)
    shift = jnp.concatenate([chunks[:, 1:], jnp.zeros((G, 1, 16 * DH), k.dtype)], axis=1)
    return jnp.concatenate([chunks, shift], axis=-1)


def _unmake_xb(dxb, *, name):
    G, n, _ = dxb.shape
    a = dxb[..., :16 * DH]
    b = jnp.concatenate([jnp.zeros((G, 1, 16 * DH), F32), dxb[:, :-1, 16 * DH:]], axis=1)
    return _addn(a, b, name=name).reshape(G, n * 16, DH)


def _compress_fwd(xb, pe, w1, w2, *, name):
    G, n, W = xb.shape
    Hc = w1.shape[1]

    def body(xb_ref, pe_ref, w1_ref, w2_ref, kc_ref, hid_ref):
        xv = (xb_ref[0] + pe_ref[...]).astype(_CDT)
        hid = _dot(xv, w1_ref[...].astype(_CDT))
        hid_ref[0] = hid
        kc_ref[0] = _dot(_gelu(hid).astype(_CDT), w2_ref[...].astype(_CDT))

    return pl.pallas_call(
        body, out_shape=(jax.ShapeDtypeStruct((G, n, DH), F32), jax.ShapeDtypeStruct((G, n, Hc), F32)), grid=(G,),
        in_specs=[pl.BlockSpec((1, n, W), lambda g: (g, 0, 0)), pl.BlockSpec((1, W), lambda g: (0, 0)),
                  pl.BlockSpec((W, Hc), lambda g: (0, 0)), pl.BlockSpec((Hc, DH), lambda g: (0, 0))],
        out_specs=(pl.BlockSpec((1, n, DH), lambda g: (g, 0, 0)), pl.BlockSpec((1, n, Hc), lambda g: (g, 0, 0))),
        compiler_params=_params(), name=name)(xb, pe.reshape(1, W), w1, w2)


def _compress_bwd(xb, pe, w1, w2, hid, dkc, *, name):
    G, n, W = xb.shape
    Hc = w1.shape[1]

    def body(xb_ref, pe_ref, w1_ref, w2_ref, hid_ref, dkc_ref, dxb_ref, dw1_ref, dw2_ref, dpe_ref):
        @pl.when(pl.program_id(0) == 0)
        def _():
            dw1_ref[...] = jnp.zeros_like(dw1_ref)
            dw2_ref[...] = jnp.zeros_like(dw2_ref)
            dpe_ref[...] = jnp.zeros_like(dpe_ref)

        xv = (xb_ref[0] + pe_ref[...]).astype(_CDT)
        hid = hid_ref[0]
        dk = dkc_ref[0].astype(_CDT)
        dact = _dot_nt(dk, w2_ref[...].astype(_CDT))
        dhid = (dact * _gelu_grad(hid)).astype(_CDT)
        dw2_ref[...] += _dot_tn(_gelu(hid).astype(_CDT), dk)
        dxb = _dot_nt(dhid, w1_ref[...].astype(_CDT))
        dxb_ref[0] = dxb
        dw1_ref[...] += _dot_tn(xv, dhid)
        dpe_ref[...] += jnp.sum(dxb, axis=0, keepdims=True)

    return pl.pallas_call(
        body, out_shape=(jax.ShapeDtypeStruct((G, n, W), F32), jax.ShapeDtypeStruct((W, Hc), F32),
                         jax.ShapeDtypeStruct((Hc, DH), F32), jax.ShapeDtypeStruct((1, W), F32)), grid=(G,),
        in_specs=[pl.BlockSpec((1, n, W), lambda g: (g, 0, 0)), pl.BlockSpec((1, W), lambda g: (0, 0)),
                  pl.BlockSpec((W, Hc), lambda g: (0, 0)), pl.BlockSpec((Hc, DH), lambda g: (0, 0)),
                  pl.BlockSpec((1, n, Hc), lambda g: (g, 0, 0)), pl.BlockSpec((1, n, DH), lambda g: (g, 0, 0))],
        out_specs=(pl.BlockSpec((1, n, W), lambda g: (g, 0, 0)), pl.BlockSpec((W, Hc), lambda g: (0, 0)),
                   pl.BlockSpec((Hc, DH), lambda g: (0, 0)), pl.BlockSpec((1, W), lambda g: (0, 0))),
        compiler_params=_params(), name=name)(xb, pe.reshape(1, W), w1, w2, hid, dkc)


def _overlap_T(S):
    n_cmp, n_slc = S // 16 - 1, S // 64
    cs = np.arange(n_cmp) * 16
    ce = cs + 32
    ss = np.arange(n_slc) * 64
    se = ss + 64
    ov = np.clip(np.minimum(ce[:, None], se[None, :]) - np.maximum(cs[:, None], ss[None, :]), 0, None) / 32.0
    out = np.zeros((LANES, S // 16), np.float32)
    out[:n_slc, :n_cmp] = ov.T
    return jnp.asarray(out)


def _tri(n, upper):
    r = lax.broadcasted_iota(jnp.int32, (n, n), 0)
    c = lax.broadcasted_iota(jnp.int32, (n, n), 1)
    return ((c >= r) if upper else (c <= r)).astype(jnp.bfloat16)


def _fox_gate_fwd(zf, b, *, name):
    S, H = zf.shape
    nb = S // BQ

    def body(z_ref, b_ref, c_ref):
        tri = _tri(BQ, False)
        carry = jnp.zeros((1, H), F32)
        for i in range(nb):
            z = z_ref[i * BQ:(i + 1) * BQ, :] + b_ref[...]
            lf = jnp.minimum(z, 0.0) - jnp.log(1.0 + jnp.exp(-jnp.abs(z)))
            c_ref[i * BQ:(i + 1) * BQ, :] = _dot_exact_left(tri, lf) + carry
            carry = carry + jnp.sum(lf, axis=0, keepdims=True)

    return pl.pallas_call(body, out_shape=jax.ShapeDtypeStruct((S, H), F32), compiler_params=_params(),
                          name=name)(zf, b)


def _fox_gate_bwd(zf, b, dc, *, name):
    S, H = zf.shape
    nb = S // BQ

    def body(z_ref, b_ref, dc_ref, dz_ref, db_ref):
        tri = _tri(BQ, True)
        carry = jnp.zeros((1, H), F32)
        db = jnp.zeros((1, H), F32)
        for i in reversed(range(nb)):
            rows = slice(i * BQ, (i + 1) * BQ)
            dcb = dc_ref[rows, :]
            dlf = _dot_exact_left(tri, dcb) + carry
            carry = carry + jnp.sum(dcb, axis=0, keepdims=True)
            z = z_ref[rows, :] + b_ref[...]
            dz = dlf * jax.nn.sigmoid(-z)
            dz_ref[rows, :] = dz
            db = db + jnp.sum(dz, axis=0, keepdims=True)
        db_ref[...] = db

    return pl.pallas_call(body, out_shape=(jax.ShapeDtypeStruct((S, H), F32), jax.ShapeDtypeStruct((1, H), F32)),
                          compiler_params=_params(), name=name)(zf, b, dc)


def _from_heads(a):
    return a.transpose(1, 0, 2).reshape(a.shape[1], -1).astype(_CDT)


def _pad_lanes(a):
    return jnp.pad(a, ((0, 0), (0, LANES - a.shape[1])))


FFN_TM = 512


def _ffn_up(x, gain, w4, *, ctx, name):
    S, D = x.shape
    C = w4.shape[2]

    def body(x_ref, g_ref, wg_ref, wu_ref, gu_ref, a_ref, h_ref):
        xv = x_ref[...]
        rstd = lax.rsqrt(jnp.mean(xv * xv, axis=-1, keepdims=True) + RMS_EPS)
        hv = (xv * rstd * g_ref[...]).astype(_CDT)

        @pl.when(pl.program_id(0) == 0)
        def _():
            h_ref[...] = hv
        g = _dot(hv, wg_ref[...].astype(_CDT))
        u = _dot(hv, wu_ref[...].astype(_CDT))
        gu_ref[0] = g
        gu_ref[1] = u
        a_ref[...] = (g * jax.nn.sigmoid(g) * u).astype(_CDT)

    return _pcall(
        body, out_shape=(jax.ShapeDtypeStruct((2, S, 2 * C), F32), jax.ShapeDtypeStruct((S, 2 * C), _CDT),
                         jax.ShapeDtypeStruct((S, D), _CDT)),
        grid=(2, S // FFN_TM),
        in_specs=[pl.BlockSpec((FFN_TM, D), lambda j, i: (i, 0)), pl.BlockSpec((1, D), lambda j, i: (0, 0)),
                  pl.BlockSpec((None, D, C), lambda j, i: (j, 0, 0)), pl.BlockSpec((None, D, C), lambda j, i: (j + 2, 0, 0))],
        out_specs=(pl.BlockSpec((2, FFN_TM, C), lambda j, i: (0, i, j)), pl.BlockSpec((FFN_TM, C), lambda j, i: (i, j)),
                   pl.BlockSpec((FFN_TM, D), lambda j, i: (i * (1 - j) + (S // FFN_TM - 1) * j, 0))),
        name=name, jobs=ctx.take(name) if ctx else (), ins=[x, gain.reshape(1, D), w4, w4])


def _ffn_dact(dxo, wd, gu, *, ctx, name):
    S, D = dxo.shape
    C = gu.shape[2] // 2

    def body(dx_ref, wd_ref, gu_ref, d_ref):
        da = _dot_nt(dx_ref[...].astype(_CDT), wd_ref[...].astype(_CDT)) * 0.5
        g, u = gu_ref[0], gu_ref[1]
        sg = jax.nn.sigmoid(g)
        silu = g * sg
        d_ref[0] = (da * u * (sg + silu * (1.0 - sg))).astype(_CDT)
        d_ref[1] = (da * silu).astype(_CDT)

    blk = pl.BlockSpec((2, FFN_TM, C), lambda j, i: (0, i, j))
    return _pcall(body, out_shape=jax.ShapeDtypeStruct(gu.shape, _CDT), grid=(2, S // FFN_TM),
                  in_specs=[pl.BlockSpec((FFN_TM, D), lambda j, i: (i, 0)), pl.BlockSpec((C, D), lambda j, i: (j, 0)), blk],
                  out_specs=blk, name=name, jobs=ctx.take(name) if ctx else (), ins=[dxo, wd, gu])


def _ffn_dwgu(h, dgu, *, ctx, name):
    S, D = h.shape
    C = dgu.shape[2] // 2
    tm = 512

    def body(h_ref, d_ref, o_ref):
        o_ref[...] = _dot_tn(h_ref[...].astype(_CDT), d_ref[...].astype(_CDT)).astype(_WIRE)

    return _pcall(body, out_shape=jax.ShapeDtypeStruct((4, D, C), _WIRE), grid=(4, D // tm),
                  in_specs=[pl.BlockSpec((S, tm), lambda j, i: (0, i)),
                            pl.BlockSpec((None, S, C), lambda j, i: (j // 2, 0, j % 2))],
                  out_specs=pl.BlockSpec((None, tm, C), lambda j, i: (j, i, 0)), name=name,
                  jobs=ctx.take(name) if ctx else (), ins=[h, dgu])


def _ffn_dh(dgu, w4, *, ctx, name):
    _, S, F2 = dgu.shape
    C, D = F2 // 2, w4.shape[1]

    def body(d_ref, w_ref, o_ref, acc_ref):
        k = pl.program_id(1)
        prod = _dot_nt(d_ref[...].astype(_CDT), w_ref[...].astype(_CDT))

        @pl.when(k == 0)
        def _():
            acc_ref[...] = prod

        @pl.when(k != 0)
        def _():
            acc_ref[...] += prod

        @pl.when(k == 3)
        def _():
            o_ref[...] = acc_ref[...]

    return _pcall(body, out_shape=jax.ShapeDtypeStruct((S, D), F32), grid=(S // FFN_TM, 4),
                  in_specs=[pl.BlockSpec((None, FFN_TM, C), lambda i, k: (k // 2, i, k % 2)),
                            pl.BlockSpec((None, D, C), lambda i, k: (k, 0, 0))],
                  out_specs=pl.BlockSpec((FFN_TM, D), lambda i, k: (i, 0)), scratch_shapes=[pltpu.VMEM((FFN_TM, D), F32)],
                  name=name, jobs=ctx.take(name) if ctx else (), ins=[dgu, w4])


def _ffn_fwd(x, g, P, l, tag):
    gu, a, h = _ffn_up(x, g, P.weight("gu", l), ctx=P.ctx, name=tag + "_gu")
    xo = _mm(a, P.weight("down", l), tm=512, tn=1024, res=x, alpha=0.5, ctx=P.ctx, name=tag + "_down")
    return xo, (x, h, gu, a)


def _ffn_bwd(dxo, saved, g, P, l, tag):
    x, h, gu, a = saved
    dgu = _ffn_dact(dxo, P.weight("down", l), gu, ctx=P.ctx, name=tag + "_dact")
    P.grad("gu", l, _ffn_dwgu(h, dgu, ctx=P.ctx, name=tag + "_dwgu"))
    dwd = _mm(a, dxo, ta=True, tm=1408, tn=1024, alpha=0.5, out_dtype=_WIRE, ctx=P.ctx, name=tag + "_dwd")
    P.grad("down", l, dwd.reshape(4, -1, D_MODEL))
    dh = _ffn_dh(dgu, P.weight("gu", l), ctx=P.ctx, name=tag + "_dh")
    dx, dg = _rms_bwd(dh, x, g, dxo, name=tag + "_drms")
    return dx, dg


def _nsa_fwd(x, g, w, cc, ss, tag):
    S = x.shape[0]
    h = _rms_fwd(x, g, name=tag + "_rms")
    hm = _mm(h, w["w_in"], tm=512, tn=896, o_lead=("heads",), ctx=w["ctx"], name=tag + "_in")
    roped = _rope(hm, cc, ss, NSA_ROTATED, name=tag + "_rope")
    z = roped[40, :, :3 * N_HEADS].reshape(S, 3, N_HEADS).transpose(1, 2, 0)[:, :, None, :]
    xbk, xbv = _make_xb(roped[16:20]), _make_xb(roped[20:24])
    kc, hidk = _compress_fwd(xbk, w["ck_pe"], w["ck_w1"], w["ck_w2"], name=tag + "_ck")
    vc, hidv = _compress_fwd(xbv, w["cv_pe"], w["cv_w1"], w["cv_w2"], name=tag + "_cv")
    ctx, GR = w["ctx"], dict(G=NSA_G, R=NSA_R)
    o0, lse0, selT = _attn_fwd(roped, kc, vc, kind="cmp", ovT=_overlap_T(S), ctx=ctx, name=tag + "_cmp", **GR)
    o1, lse1 = _attn_fwd(roped, roped, roped, heads=(0, 24, 28), kind="sel", selT=selT, ctx=ctx, name=tag + "_slc", **GR)
    o2, lse2 = _attn_fwd(roped, roped, roped, heads=(0, 32, 36), kind="band", window=NSA_WINDOW, ctx=ctx,
                         name=tag + "_win", **GR)
    o = _combine(o0.reshape(N_HEADS, S, DH), o1.reshape(N_HEADS, S, DH), o2.reshape(N_HEADS, S, DH), z,
                 name=tag + "_mix")
    of = _from_heads(o)
    xo = _mm(of, w["out"], tm=512, tn=1024, res=x, ctx=w["ctx"], name=tag + "_out")
    saved = (x, h, roped, z, xbk, xbv, hidk, hidv, kc, vc, (o0, o1, o2), (lse0, lse1, lse2), selT, of)
    return xo, saved


def _out_bwd(dxo, of, w, tag):
    dof = _mm(dxo, w["out"], tb=True, tm=512, tn=1024, o_lead=("heads",), ctx=w["ctx"], name=tag + "_dof")
    dw = _mm(of, dxo, ta=True, tm=512, tn=1024, out_dtype=_WIRE, ctx=w["ctx"], name=tag + "_dwout")
    w["P"].grad("out", w["out_l"], dw.reshape(4, -1, D_MODEL))
    return dof


def _split_cols(dw, n_in):
    cs = n_in // 4
    return dw[:, :n_in].reshape(D_MODEL, 4, cs).transpose(1, 0, 2).astype(_WIRE)


def _nsa_bwd(dxo, saved, g, w, cc, ss, tag):
    x, h, roped, z, xbk, xbv, hidk, hidv, kc, vc, os_, lses, selT, of = saved
    S = x.shape[0]
    do = _out_bwd(dxo, of, w, tag).reshape(NSA_G, NSA_R, S, DH)
    zg = z.reshape(3, NSA_G, NSA_R, 1, S)
    ctx = w["ctx"]
    b0 = _attn_bwd(roped, kc, vc, os_[0], do, lses[0], kind="cmp", gate=zg[0], ctx=ctx, name=tag + "_dcmp")
    b1 = _attn_bwd(roped, roped, roped, os_[1], do, lses[1], heads=(0, 24, 28), kind="sel", selT=selT, gate=zg[1],
                   ctx=ctx, name=tag + "_dslc")
    b2 = _attn_bwd(roped, roped, roped, os_[2], do, lses[2], heads=(0, 32, 36), kind="band", window=NSA_WINDOW,
                   gate=zg[2], ctx=ctx, name=tag + "_dwin")
    dxbk, dck_w1, dck_w2, dck_pe = _compress_bwd(xbk, w["ck_pe"], w["ck_w1"], w["ck_w2"], hidk, b0["dk"],
                                                 name=tag + "_dck")
    dxbv, dcv_w1, dcv_w2, dcv_pe = _compress_bwd(xbv, w["cv_pe"], w["cv_w1"], w["cv_w2"], hidv, b0["dv"],
                                                 name=tag + "_dcv")
    dk0 = _unmake_xb(dxbk, name=tag + "_dk0")
    dv0 = _unmake_xb(dxbv, name=tag + "_dv0")
    dq = _addn(b0["dq"], b1["dq"], b2["dq"], name=tag + "_dqsum").reshape(N_HEADS, S, DH)
    dz = jnp.stack([b0["dz"], b1["dz"], b2["dz"]], axis=0).reshape(3 * N_HEADS, S).T
    dgates = jnp.pad(dz, ((0, 0), (0, 2 * DH - 3 * N_HEADS))).reshape(S, 2, DH).transpose(1, 0, 2)
    dhm = jnp.concatenate([dq, dk0, dv0, b1["dk"], b1["dv"], b2["dk"], b2["dv"], dgates], axis=0)
    dproj = _from_heads(_rope(dhm, cc, -ss, NSA_ROTATED, name=tag + "_drope"))
    dh = _mm(dproj, w["w_in"], tb=True, tm=512, tn=512, ctx=w["ctx"], name=tag + "_dh")
    dw_in = _mm(h, dproj, ta=True, tm=512, tn=896, ctx=w["ctx"], name=tag + "_dwin_w")
    dx, dg = _rms_bwd(dh, x, g, dxo, name=tag + "_drms")
    P, j = w["P"], w["j"]
    P.grad("nsa_in", j, _split_cols(dw_in, NSA_IN))
    P.grad("cw1", j, dck_w1.astype(_WIRE).reshape(4, -1, dck_w1.shape[1]))
    P.grad("cw1", 2 + j, dcv_w1.astype(_WIRE).reshape(4, -1, dcv_w1.shape[1]))
    grads = dict(ck_pe=dck_pe.reshape(32, DH), ck_w2=dck_w2, cv_pe=dcv_pe.reshape(32, DH), cv_w2=dcv_w2)
    return dx, dg, grads


def _swa_fwd(x, g, w, cc, ss, tag):
    S = x.shape[0]
    h = _rms_fwd(x, g, name=tag + "_rms")
    hm = _mm(h, w["w_in"], tm=512, tn=640, o_lead=("heads",), ctx=w["ctx"], name=tag + "_in")
    roped = _rope(hm, cc, ss, SWA_ROTATED, name=tag + "_rope")
    sinks = jnp.broadcast_to(w["sinks"].reshape(SWA_G, SWA_R, 1, 1), (SWA_G, SWA_R, 8, LANES))
    o, lse = _attn_fwd(roped, roped, roped, G=SWA_G, R=SWA_R, heads=SWA_HEADS, kind="band", window=SWA_WINDOW,
                       sinks=sinks, ctx=w["ctx"], name=tag + "_attn")
    of = _from_heads(o.reshape(N_HEADS, S, DH))
    xo = _mm(of, w["out"], tm=512, tn=1024, res=x, ctx=w["ctx"], name=tag + "_out")
    return xo, (x, h, roped, sinks, o, lse, of)


def _swa_bwd(dxo, saved, g, w, cc, ss, tag):
    x, h, roped, sinks, o, lse, of = saved
    S = x.shape[0]
    do = _out_bwd(dxo, of, w, tag).reshape(SWA_G, SWA_R, S, DH)
    b = _attn_bwd(roped, roped, roped, o, do, lse, heads=SWA_HEADS, kind="band", window=SWA_WINDOW, sinks=sinks,
                  ctx=w["ctx"], name=tag + "_dattn")
    dhm = jnp.concatenate([b["dq"].reshape(N_HEADS, S, DH), b["dk"], b["dv"]], axis=0)
    dproj = _from_heads(_rope(dhm, cc, -ss, SWA_ROTATED, name=tag + "_drope"))
    dh = _mm(dproj, w["w_in"], tb=True, tm=512, tn=512, ctx=w["ctx"], name=tag + "_dh")
    dw_in = _mm(h, dproj, ta=True, tm=512, tn=640, ctx=w["ctx"], name=tag + "_dwin_w")
    dx, dg = _rms_bwd(dh, x, g, dxo, name=tag + "_drms")
    w["P"].grad("swa_in", w["j"], _split_cols(dw_in, SWA_IN))
    return dx, dg, dict(sinks=b["dsink"][:, :, 0, 0].reshape(N_HEADS))


def _fox_fwd(x, g, w, tag):
    S = x.shape[0]
    h = _rms_fwd(x, g, name=tag + "_rms")
    hm = _mm(h, w["w_in"], tm=512, tn=640, o_lead=("heads",), ctx=w["ctx"], name=tag + "_in")
    zf = jnp.pad(hm[48], ((0, 0), (0, LANES - DH)))
    bf = _pad_lanes(w["b_f"].reshape(1, N_HEADS))
    c = _fox_gate_fwd(zf, bf, name=tag + "_gate")[:, :N_HEADS]
    bias = c.T[:, None, :]
    o, lse = _attn_fwd(hm, hm, hm, G=N_HEADS, R=1, heads=FOX_HEADS, kind="causal", bias=bias, ctx=w["ctx"],
                       name=tag + "_attn")
    of = _from_heads(o.reshape(N_HEADS, S, DH))
    xo = _mm(of, w["out"], tm=512, tn=1024, res=x, ctx=w["ctx"], name=tag + "_out")
    return xo, (x, h, hm, zf, bf, bias, o, lse, of)


def _fox_bwd(dxo, saved, g, w, tag):
    x, h, hm, zf, bf, bias, o, lse, of = saved
    S = x.shape[0]
    do = _out_bwd(dxo, of, w, tag).reshape(N_HEADS, 1, S, DH)
    b = _attn_bwd(hm, hm, hm, o, do, lse, heads=FOX_HEADS, kind="causal", bias=bias, ctx=w["ctx"], name=tag + "_dattn")
    dzf, db = _fox_gate_bwd(zf, bf, _pad_lanes(b["dc"].reshape(N_HEADS, S).T), name=tag + "_dgate")
    dgates = dzf.reshape(S, 2, DH).transpose(1, 0, 2)
    dhm = jnp.concatenate([b["dq"].reshape(N_HEADS, S, DH), b["dk"], b["dv"], dgates], axis=0)
    dproj = _from_heads(dhm)
    dh = _mm(dproj, w["w_in"], tb=True, tm=512, tn=512, ctx=w["ctx"], name=tag + "_dh")
    dw_in = _mm(h, dproj, ta=True, tm=512, tn=640, ctx=w["ctx"], name=tag + "_dwin_w")
    dx, dg = _rms_bwd(dh, x, g, dxo, name=tag + "_drms")
    w["P"].grad("fox_in", w["j"], _split_cols(dw_in, FOX_IN))
    return dx, dg, dict(b_f=db[0, :N_HEADS])


GROUPS = {
    "gu": (("ffn1_w_gu", "ffn2_w_gu"), 2),
    "down": (("ffn1_w_down", "ffn2_w_down"), 1),
    "out": (("nsa_w_out", "swa_w_out", "fox_w_out"), 1),
    "cw1": (("nsa_ck_w1", "nsa_cv_w1"), 1),
    "nsa_in": (("nsa_w_in",), 2),
    "swa_in": (("swa_w_in",), 2),
    "fox_in": (("fox_w_in",), 2),
}
OUT_SLAB = {0: 0, 3: 1, 1: 2, 2: 3}


def _pieces_in_order():
    chunks = []
    for i in range(DEPTH):
        kind, j = i % 3, i // 3
        chunks.append([("gu", i), ("down", i)])
        if kind == 0:
            chunks.append([("nsa_in", j), ("cw1", j), ("cw1", 2 + j), ("out", OUT_SLAB[i])])
        else:
            chunks.append([("swa_in" if kind == 1 else "fox_in", j), ("out", OUT_SLAB[i])])
        chunks.append([("gu", DEPTH + i), ("down", DEPTH + i)])
    return chunks


def _consumer_layout(group, F):
    _, rows, C = F.shape
    if group == "gu":
        return F
    if GROUPS[group][1] == 1:
        return F.reshape(4 * rows, C)
    w = F.transpose(1, 0, 2).reshape(rows, 4 * C)
    pad = {"nsa_in": NSA_IN_PAD, "swa_in": SWA_IN, "fox_in": FOX_IN_PAD}[group] - 4 * C
    return jnp.pad(w, ((0, 0), (0, pad)))


class _Given:
    def __init__(self, pieces, small):
        self.pieces, self.small, self.ctx, self.grads = pieces, small, None, {}

    def weight(self, group, l):
        return _consumer_layout(group, self.pieces[group, l])

    def grad(self, group, l, G):
        self.grads[group, l] = G


class _MixerWeights(dict):
    def __init__(self, P, pieces, **given):
        super().__init__(P=P, ctx=P.ctx, **given)
        self.pieces = pieces

    def __missing__(self, key):
        self[key] = self["P"].weight(*self.pieces[key])
        return self[key]


def _mixer_weights(P, i):
    kind, j = i % 3, i // 3
    out = {"out": ("out", OUT_SLAB[i])}
    if kind == 0:
        small = {k: P.small["nsa_" + k][j] for k in ("ck_pe", "ck_w2", "cv_pe", "cv_w2")}
        return _MixerWeights(P, dict(out, w_in=("nsa_in", j), ck_w1=("cw1", j), cv_w1=("cw1", 2 + j)), j=j,
                             out_l=OUT_SLAB[i], **small)
    if kind == 1:
        return _MixerWeights(P, dict(out, w_in=("swa_in", j)), j=j, out_l=OUT_SLAB[i], sinks=P.small["swa_sinks"][j])
    return _MixerWeights(P, dict(out, w_in=("fox_in", j)), j=j, out_l=OUT_SLAB[i], b_f=P.small["fox_b_f"][j])


def _local_step(x, tgt, P):
    S = x.shape[0]
    cc, ss = _rope_tables(S)
    sm = P.small
    saved = []
    for i in range(DEPTH):
        kind = i % 3
        x, s1 = _ffn_fwd(x, sm["ffn1_norm"][i], P, i, f"l{i}f1")
        mw = _mixer_weights(P, i)
        if kind == 0:
            x, s2 = _nsa_fwd(x, sm["mix_norm"][i], mw, cc, ss, f"l{i}nsa")
        elif kind == 1:
            x, s2 = _swa_fwd(x, sm["mix_norm"][i], mw, cc, ss, f"l{i}swa")
        else:
            x, s2 = _fox_fwd(x, sm["mix_norm"][i], mw, f"l{i}fox")
        x, s3 = _ffn_fwd(x, sm["ffn2_norm"][i], P, DEPTH + i, f"l{i}f2")
        saved.append((s1, mw, s2, s3))
    loss, dx, d_final = _loss_head(x, sm["final_norm"], tgt, name="loss_head")

    norms = {k: [None] * DEPTH for k in ("ffn1_norm", "mix_norm", "ffn2_norm")}
    mix = {}
    for i in reversed(range(DEPTH)):
        kind, j = i % 3, i // 3
        s1, mw, s2, s3 = saved[i]
        dx, norms["ffn2_norm"][i] = _ffn_bwd(dx, s3, sm["ffn2_norm"][i], P, DEPTH + i, f"l{i}f2")
        if kind == 0:
            dx, dg, gm = _nsa_bwd(dx, s2, sm["mix_norm"][i], mw, cc, ss, f"l{i}nsa")
            pre = "nsa_"
        elif kind == 1:
            dx, dg, gm = _swa_bwd(dx, s2, sm["mix_norm"][i], mw, cc, ss, f"l{i}swa")
            pre = "swa_"
        else:
            dx, dg, gm = _fox_bwd(dx, s2, sm["mix_norm"][i], mw, f"l{i}fox")
            pre = "fox_"
        norms["mix_norm"][i] = dg
        for k, val in gm.items():
            mix.setdefault(pre + k, {})[j] = val
        dx, norms["ffn1_norm"][i] = _ffn_bwd(dx, s1, sm["ffn1_norm"][i], P, i, f"l{i}f1")
    small = {k: jnp.stack(v, axis=0) for k, v in norms.items()}
    small.update({k: jnp.stack([d[j] for j in sorted(d)], axis=0) for k, d in mix.items()})
    small["final_norm"] = d_final
    return loss, dx, small


def _sum_slots(own, recv, me, into, row0, *, name):
    _, R, C = recv.shape
    tr = _row_tile(math.gcd(R, row0), C)
    blk0 = row0 // tr

    def body(_, g_ref, r_ref, __, o_ref):
        acc = g_ref[...].astype(F32) + r_ref[0].astype(F32)
        acc = acc + r_ref[1].astype(F32)
        o_ref[...] = acc + r_ref[2].astype(F32)

    spec = pltpu.PrefetchScalarGridSpec(
        num_scalar_prefetch=1, grid=(R // tr,),
        in_specs=[pl.BlockSpec((None, tr, C), lambda i, me_ref: (me_ref[0], i, 0)),
                  pl.BlockSpec((3, tr, C), lambda i, me_ref: (0, i, 0)), _ANY],
        out_specs=pl.BlockSpec((tr, C), lambda i, me_ref: (blk0 + i, 0)))
    return pl.pallas_call(body, grid_spec=spec, out_shape=jax.ShapeDtypeStruct(into.shape, F32),
                          input_output_aliases={3: 0}, compiler_params=_params(), name=name)(me, own, recv, into)


def _swap_sibling(parts):
    n = len(parts)

    def body(*refs):
        srcs, outs = refs[:n], refs[n:2 * n]
        send_sems, recv_sems = refs[2 * n:]
        x, y, c = lax.axis_index("x"), lax.axis_index("y"), lax.axis_index("c")
        cps = [pltpu.make_async_remote_copy(src_ref=srcs[g], dst_ref=outs[g], send_sem=send_sems.at[g],
                                            recv_sem=recv_sems.at[g], device_id=(x, y, 1 - c), device_id_type=MESH)
               for g in range(n)]
        for cp in cps:
            cp.start()
        for cp in cps:
            cp.wait()

    return pl.pallas_call(
        body, out_shape=tuple(jax.ShapeDtypeStruct(p.shape, p.dtype) for p in parts),
        in_specs=[_ANY] * n, out_specs=(_ANY,) * n,
        scratch_shapes=[pltpu.SemaphoreType.DMA((n,)), pltpu.SemaphoreType.DMA((n,))],
        compiler_params=pltpu.CompilerParams(has_side_effects=True), name="swap_core_grads")(*parts)


def _flip(coord, bit):
    return 1 - coord if bit else coord


def _allreduce_small(v):
    n, C = v.shape

    def body(v_ref, o_ref, buf, send_sems, recv_sems):
        x, y, c = lax.axis_index("x"), lax.axis_index("y"), lax.axis_index("c")
        me = 4 * x + 2 * y + c
        buf[me] = v_ref[...]
        peers = [(_flip(x, (j >> 2) & 1), _flip(y, (j >> 1) & 1), _flip(c, j & 1)) for j in range(1, 8)]
        sends = [pltpu.make_async_remote_copy(src_ref=v_ref, dst_ref=buf.at[me], send_sem=send_sems.at[j],
                                              recv_sem=recv_sems.at[j], device_id=peer, device_id_type=MESH)
                 for j, peer in enumerate(peers)]
        for cp in sends:
            cp.start()
        for j, (px, py, pc) in enumerate(peers):
            pltpu.make_async_remote_copy(src_ref=v_ref, dst_ref=buf.at[4 * px + 2 * py + pc], send_sem=send_sems.at[j],
                                         recv_sem=recv_sems.at[j], device_id=(px, py, pc),
                                         device_id_type=MESH).wait_recv()
        for cp in sends:
            cp.wait_send()
        acc = buf[0]
        for d in range(1, 8):
            acc = acc + buf[d]
        o_ref[...] = acc

    return pl.pallas_call(
        body, out_shape=jax.ShapeDtypeStruct((n, C), F32),
        in_specs=[pl.BlockSpec(memory_space=pltpu.VMEM)], out_specs=pl.BlockSpec(memory_space=pltpu.VMEM),
        scratch_shapes=[pltpu.VMEM((8, n, C), F32), pltpu.SemaphoreType.DMA((7,)), pltpu.SemaphoreType.DMA((7,))],
        compiler_params=pltpu.CompilerParams(has_side_effects=True), name="allreduce_small")(v)


def _adamw(w, m, v, gs, *, row0=0, name):
    shape = w.shape
    w3, m3, v3 = (a.reshape((-1,) + a.shape[-2:]) for a in (w, m, v))
    g2 = [_rows2d(g) for g in gs]
    L, rows, C = w3.shape
    tr = _row_tile(math.gcd(rows, row0), C, budget=1024 * 1024)
    ng = len(g2)
    blk0, nb = row0 // tr, rows // tr

    def body(*refs):
        w_ref, m_ref, v_ref = refs[:3]
        g = refs[3][...]
        for r in refs[4:3 + ng]:
            g = g + r[...]
        g_ref, d_ref, nm_ref, nv_ref = refs[3 + ng:]
        mn = B1 * m_ref[...] + (1.0 - B1) * g
        vn = B2 * v_ref[...] + (1.0 - B2) * (g * g)
        m_hat = mn / (1.0 - B1 ** STEP)
        v_hat = vn / (1.0 - B2 ** STEP)
        g_ref[...] = g
        d_ref[...] = -LR * (m_hat / (jnp.sqrt(v_hat) + EPS) + WD * w_ref[...])
        nm_ref[...] = mn
        nv_ref[...] = vn

    spec = pl.BlockSpec((None, tr, C), lambda l, i: (l, i, 0))
    gspec = pl.BlockSpec((tr, C), lambda l, i: (blk0 + l * nb + i, 0))
    outs = pl.pallas_call(body, out_shape=tuple(jax.ShapeDtypeStruct((L, rows, C), F32) for _ in range(4)),
                          grid=(L, nb), in_specs=[spec] * 3 + [gspec] * ng, out_specs=(spec,) * 4,
                          compiler_params=_params(), name=name)(w3, m3, v3, *g2)
    return tuple(o.reshape(shape) for o in outs)


def _small_layout(shapes):
    offs, off = {}, 0
    for k in REPLICATED:
        n = int(np.prod(shapes[k]))
        offs[k] = (off, n)
        off += -(-n // LANES) * LANES
    return offs, off


def _pack_small(d, shapes):
    offs, total = _small_layout(shapes)
    parts = []
    for k in REPLICATED:
        n = offs[k][1]
        parts.append(jnp.pad(d[k].reshape(-1).astype(F32), (0, -(-n // LANES) * LANES - n)))
    rows = -(-(total // LANES) // 8) * 8
    return jnp.pad(jnp.concatenate(parts), (0, rows * LANES - total)).reshape(rows, LANES)


def _unpack_small(a, shapes):
    offs, _ = _small_layout(shapes)
    flat = a.reshape(-1)
    return {k: flat[offs[k][0]:offs[k][0] + offs[k][1]].reshape(shapes[k]) for k in REPLICATED}


def _group_shards(w):
    shards = []
    for members, _ in GROUPS.values():
        s = jnp.concatenate([w[k].astype(_WIRE) for k in members], axis=0)
        shards.append(s.reshape(s.shape[0], 2, s.shape[1] // 2, s.shape[2]))
    return shards


class _Exchanged:
    def __init__(self, w):
        self.small = {k: w[k] for k in REPLICATED}
        self.ctx = _Sched()
        shards = dict(zip(GROUPS, _group_shards(w)))
        self.gather, self.cache, self.scatter = {}, {}, {}
        chunks = _pieces_in_order()
        self.me = jnp.reshape(2 * lax.axis_index("x") + lax.axis_index("y"), (1,)).astype(jnp.int32)
        first = {(g, l): _gather_task1(shards[g], l, _place_own(shards[g], l, self.me, name=f"own_{g}{l}"))
                 for chunk in chunks for g, l in chunk}
        self.gather = {p: _gather_task2(task) for p, task in first.items()}
        pieces = [p for chunk in chunks for p in chunk]
        for n, p in enumerate(pieces):
            self.ctx.push(first[p])
            if n:
                self.ctx.push(self.gather[pieces[n - 1]])
        self.ctx.push(self.gather[pieces[-1]])

    def weight(self, group, l):
        if (group, l) not in self.cache:
            task = self.gather[group, l]
            self.ctx.finish(task)
            F = task.result()
            self.cache[group, l] = _consumer_layout(group, F.reshape(4, -1, F.shape[-1]))
        return self.cache[group, l]

    def grad(self, group, l, G):
        self.scatter[group, l] = (G, self.ctx.push(_scatter_task(G)))

    def partial_sums(self, w):
        self.ctx.finish()
        parts = []
        for group, (members, _) in GROUPS.items():
            rows, C = w[members[0]].shape[1:]
            n = sum(w[k].shape[0] for k in members)
            part = jnp.zeros((n * rows, C), F32)
            for l in range(n):
                G, task = self.scatter[group, l]
                part = _sum_slots(G, task.result(), self.me, part, l * rows, name=f"sum_{group}{l}")
            parts.append(part)
        return parts


def _reduce_and_update(w, m, v, parts, small):
    others = _swap_sibling(parts)
    small_shapes = {k: w[k].shape for k in REPLICATED}
    g_small = _unpack_small(_allreduce_small(_pack_small(small, small_shapes)), small_shapes)

    out_g, out_d, out_m, out_v = {}, {}, {}, {}
    for (members, _), part, other in zip(GROUPS.values(), parts, others):
        row0 = 0
        for k in members:
            out_g[k], out_d[k], out_m[k], out_v[k] = _adamw(w[k], m[k], v[k], [part, other], row0=row0,
                                                            name="adamw_" + k)
            row0 += w[k].shape[0] * w[k].shape[1]
    sm = _adamw(_pack_small(w, small_shapes), _pack_small(m, small_shapes), _pack_small(v, small_shapes),
                [_pack_small(g_small, small_shapes)], name="adamw_small")
    for d, packed in zip((out_g, out_d, out_m, out_v), sm):
        d.update(_unpack_small(packed, small_shapes))
    return out_g, out_d, out_m, out_v


def kernel(x, ffn1_norm, ffn1_w_gu, ffn1_w_down, mix_norm, ffn2_norm, ffn2_w_gu, ffn2_w_down, nsa_w_in, nsa_ck_pe, nsa_ck_w1, nsa_ck_w2, nsa_cv_pe, nsa_cv_w1, nsa_cv_w2, nsa_w_out, swa_w_in, swa_sinks, swa_w_out, fox_w_in, fox_b_f, fox_w_out, final_norm, loss_target, m_ffn1_norm, m_ffn1_w_gu, m_ffn1_w_down, m_mix_norm, m_ffn2_norm, m_ffn2_w_gu, m_ffn2_w_down, m_nsa_w_in, m_nsa_ck_pe, m_nsa_ck_w1, m_nsa_ck_w2, m_nsa_cv_pe, m_nsa_cv_w1, m_nsa_cv_w2, m_nsa_w_out, m_swa_w_in, m_swa_sinks, m_swa_w_out, m_fox_w_in, m_fox_b_f, m_fox_w_out, m_final_norm, v_ffn1_norm, v_ffn1_w_gu, v_ffn1_w_down, v_mix_norm, v_ffn2_norm, v_ffn2_w_gu, v_ffn2_w_down, v_nsa_w_in, v_nsa_ck_pe, v_nsa_ck_w1, v_nsa_ck_w2, v_nsa_cv_pe, v_nsa_cv_w1, v_nsa_cv_w2, v_nsa_w_out, v_swa_w_in, v_swa_sinks, v_swa_w_out, v_fox_w_in, v_fox_b_f, v_fox_w_out, v_final_norm):
    args = dict(locals())
    w = {k: args[k] for k in WEIGHTS}
    m = {k: args["m_" + k] for k in WEIGHTS}
    v = {k: args["v_" + k] for k in WEIGHTS}
    P = _Exchanged(w)
    loss_part, dx, small = _local_step(x[0], loss_target[0], P)
    loss = lax.psum(loss_part, ("x", "y", "c"))
    out_g, out_d, out_m, out_v = _reduce_and_update(w, m, v, P.partial_sums(w), small)
    return (loss, dx[None], *[out_g[k] for k in WEIGHTS], *[out_d[k] for k in WEIGHTS],
            *[out_m[k] for k in WEIGHTS], *[out_v[k] for k in WEIGHTS])
```
